```python
import jax, jax.numpy as jnp
from jax import lax
import numpy as np

D_MODEL = 2048
BATCH = 8
SEQ = 8192
DEPTH = 2

N_MIXERS = 2
MIX_WIDTH = D_MODEL
FOX_HEADS = 16
FOX_HEAD_DIM = MIX_WIDTH // FOX_HEADS
Q_BLOCK = 128
HGRN_HEADS = 16
HGRN_KEY_DIM = MIX_WIDTH // HGRN_HEADS
HGRN_VAL_DIM = MIX_WIDTH // HGRN_HEADS
HGRN_CHUNK = 64
N_FOX_LAYERS = (DEPTH + 1) // 2
N_HGRN_LAYERS = DEPTH // 2
FOX_IN = 4 * MIX_WIDTH + FOX_HEADS
HGRN_IN = 4 * MIX_WIDTH
EPS = 1e-6

kernel_name = "fox_hgrn2_interleaved_hybrid"


def rms_norm(x, gain):
    xf = x.astype(jnp.float32)
    y = xf * lax.rsqrt(jnp.mean(xf * xf, axis=-1, keepdims=True) + EPS)
    return y.astype(x.dtype) * gain


def split_heads(t, n_heads):
    b, s, _ = t.shape
    return t.reshape(b, s, n_heads, -1).transpose(0, 2, 1, 3)


def fox_mixer(h, w_in, b_f):
    B, S, _ = h.shape
    W, H, dh = MIX_WIDTH, FOX_HEADS, FOX_HEAD_DIM
    proj = h @ w_in
    q = split_heads(proj[..., :W], H)
    k = split_heads(proj[..., W:2 * W], H)
    v = split_heads(proj[..., 2 * W:3 * W], H)
    f_logit = proj[..., 3 * W:3 * W + H]
    gate = proj[..., 3 * W + H:]
    log_f = jax.nn.log_sigmoid((f_logit + b_f).astype(jnp.float32)).transpose(0, 2, 1)
    c = jnp.cumsum(log_f, axis=-1)
    nq = S // Q_BLOCK
    qb = q.reshape(B, H, nq, Q_BLOCK, dh).transpose(2, 0, 1, 3, 4)
    cb = c.reshape(B, H, nq, Q_BLOCK).transpose(2, 0, 1, 3)
    starts = jnp.arange(nq) * Q_BLOCK
    key_pos = jnp.arange(S)
    scale = dh ** -0.5

    def block(args):
        q_blk, c_blk, start = args
        logits = jnp.einsum('bhqd,bhkd->bhqk', q_blk, k).astype(jnp.float32) * scale
        logits = logits + (c_blk[..., :, None] - c[:, :, None, :])
        q_pos = start + jnp.arange(Q_BLOCK)
        causal = key_pos[None, :] <= q_pos[:, None]
        logits = jnp.where(causal, logits, -jnp.inf)
        p = jax.nn.softmax(logits, axis=-1).astype(v.dtype)
        return jnp.einsum('bhqk,bhkd->bhqd', p, v)

    o = lax.map(block, (qb, cb, starts))
    o = o.transpose(1, 0, 3, 2, 4).reshape(B, S, W)
    return o * jax.nn.silu(gate)


def hgrn2_mixer(h, w_in, lb, onorm_gain):
    B, S, _ = h.shape
    H, dk, dv, C = HGRN_HEADS, HGRN_KEY_DIM, HGRN_VAL_DIM, HGRN_CHUNK
    proj = h @ w_in
    q_raw, f_raw, i_raw, gate = jnp.split(proj, 4, axis=-1)
    q = split_heads(jax.nn.silu(q_raw), H).astype(jnp.float32)
    fz = split_heads(f_raw, H).astype(jnp.float32)
    v = split_heads(i_raw, H).astype(jnp.float32)
    lb_h = lb.astype(jnp.float32).reshape(H, 1, dk)
    log_f = jnp.log(lb_h + (1.0 - lb_h) * jax.nn.sigmoid(fz))
    k = (1.0 - lb_h) * jax.nn.sigmoid(-fz)
    nc = S // C

    def to_chunks(t):
        return t.reshape(B, H, nc, C, t.shape[-1]).transpose(2, 0, 1, 3, 4)

    tri = jnp.tril(jnp.ones((C, C), dtype=bool))

    def step(state, inp):
        q_c, k_c, lf_c, v_c = inp
        b = jnp.cumsum(lf_c, axis=-2)
        b_last = b[:, :, -1, :]
        inter = jnp.einsum('bhtd,bhde->bhte', q_c * jnp.exp(b), state)
        rel = jnp.where(tri[:, :, None], b[:, :, :, None, :] - b[:, :, None, :, :], -jnp.inf)
        A = jnp.einsum('bhtd,bhsd,bhtsd->bhts', q_c, k_c, jnp.exp(rel))
        intra = jnp.einsum('bhts,bhse->bhte', A, v_c)
        new_state = jnp.exp(b_last)[..., None] * state + jnp.einsum(
            'bhsd,bhse->bhde', k_c * jnp.exp(b_last[:, :, None, :] - b), v_c)
        return new_state, inter + intra

    state0 = jnp.zeros((B, H, dk, dv), jnp.float32)
    _, o = lax.scan(step, state0, (to_chunks(q), to_chunks(k), to_chunks(log_f), to_chunks(v)))
    o = o.transpose(1, 2, 0, 3, 4).reshape(B, H, S, dv)
    o = o * lax.rsqrt(jnp.mean(o * o, axis=-1, keepdims=True) + EPS)
    o = o.transpose(0, 2, 1, 3).reshape(B, S, MIX_WIDTH).astype(h.dtype) * onorm_gain
    return o * jax.nn.silu(gate)


def _fwd_setup_inputs(seed: int = 0) -> dict:
    key = jax.random.key(seed)
    ks = jax.random.split(key, 10)
    f32 = jnp.float32
    x = jax.random.normal(ks[0], (BATCH, SEQ, D_MODEL), f32)
    norm_gains = 1.0 + 0.02 * jax.random.normal(ks[1], (DEPTH, D_MODEL), f32)
    fox_w_in = jax.random.normal(ks[2], (N_FOX_LAYERS, D_MODEL, FOX_IN), f32) * D_MODEL ** -0.5
    fox_b_f = 1.0 + 0.1 * jax.random.normal(ks[3], (N_FOX_LAYERS, FOX_HEADS), f32)
    hgrn_w_in = jax.random.normal(ks[4], (N_HGRN_LAYERS, D_MODEL, HGRN_IN), f32) * D_MODEL ** -0.5
    hgrn_lb_logits = 0.5 * jax.random.normal(ks[5], (DEPTH, MIX_WIDTH), f32)
    hgrn_onorm = 1.0 + 0.02 * jax.random.normal(ks[6], (N_HGRN_LAYERS, MIX_WIDTH), f32)
    w_out = jax.random.normal(ks[7], (DEPTH, MIX_WIDTH, D_MODEL), f32) * MIX_WIDTH ** -0.5
    final_gain = 1.0 + 0.02 * jax.random.normal(ks[8], (D_MODEL,), f32)
    return {"x": x, "norm_gains": norm_gains, "fox_w_in": fox_w_in, "fox_b_f": fox_b_f,
            "hgrn_w_in": hgrn_w_in, "hgrn_lb_logits": hgrn_lb_logits, "hgrn_onorm": hgrn_onorm,
            "w_out": w_out, "final_gain": final_gain}


def _fwd_reference(x, norm_gains, fox_w_in, fox_b_f, hgrn_w_in, hgrn_lb_logits, hgrn_onorm, w_out, final_gain):
    lb_all = jnp.cumsum(jax.nn.softmax(hgrn_lb_logits.astype(jnp.float32), axis=0), axis=0)
    lb_all = lb_all - lb_all[0:1]
    for i in range(DEPTH):
        h = rms_norm(x, norm_gains[i])
        j = i // N_MIXERS
        if i % N_MIXERS == 0:
            y = fox_mixer(h, fox_w_in[j], fox_b_f[j])
        else:
            y = hgrn2_mixer(h, hgrn_w_in[j], lb_all[i], hgrn_onorm[j])
        x = x + y @ w_out[i]
    return rms_norm(x, final_gain)


import jax as _jax
import jax.numpy as _jnp

TWIN_FORMAT = 'train_step'
FWD_PARAMS = ['x', 'norm_gains', 'fox_w_in', 'fox_b_f', 'hgrn_w_in', 'hgrn_lb_logits', 'hgrn_onorm', 'w_out', 'final_gain']
TWIN_WEIGHTS = ['norm_gains', 'fox_w_in', 'fox_b_f', 'hgrn_w_in', 'hgrn_lb_logits', 'hgrn_onorm', 'w_out', 'final_gain']
TWIN_DIFF_INPUT = 'x'
TWIN_INPUTS = ['x', 'norm_gains', 'fox_w_in', 'fox_b_f', 'hgrn_w_in', 'hgrn_lb_logits', 'hgrn_onorm', 'w_out', 'final_gain', 'loss_target', 'm_norm_gains', 'm_fox_w_in', 'm_fox_b_f', 'm_hgrn_w_in', 'm_hgrn_lb_logits', 'm_hgrn_onorm', 'm_w_out', 'm_final_gain', 'v_norm_gains', 'v_fox_w_in', 'v_fox_b_f', 'v_hgrn_w_in', 'v_hgrn_lb_logits', 'v_hgrn_onorm', 'v_w_out', 'v_final_gain']
TWIN_OUTPUTS = ['loss', 'grad_x', 'grad_norm_gains', 'grad_fox_w_in', 'grad_fox_b_f', 'grad_hgrn_w_in', 'grad_hgrn_lb_logits', 'grad_hgrn_onorm', 'grad_w_out', 'grad_final_gain', 'delta_norm_gains', 'delta_fox_w_in', 'delta_fox_b_f', 'delta_hgrn_w_in', 'delta_hgrn_lb_logits', 'delta_hgrn_onorm', 'delta_w_out', 'delta_final_gain', 'new_m_norm_gains', 'new_m_fox_w_in', 'new_m_fox_b_f', 'new_m_hgrn_w_in', 'new_m_hgrn_lb_logits', 'new_m_hgrn_onorm', 'new_m_w_out', 'new_m_final_gain', 'new_v_norm_gains', 'new_v_fox_w_in', 'new_v_fox_b_f', 'new_v_hgrn_w_in', 'new_v_hgrn_lb_logits', 'new_v_hgrn_onorm', 'new_v_w_out', 'new_v_final_gain']
TWIN_LEAF_KINDS = {'loss': 'loss', 'grad_x': 'grad_x', 'grad_norm_gains': 'grad_w', 'grad_fox_w_in': 'grad_w', 'grad_fox_b_f': 'grad_w', 'grad_hgrn_w_in': 'grad_w', 'grad_hgrn_lb_logits': 'grad_w', 'grad_hgrn_onorm': 'grad_w', 'grad_w_out': 'grad_w', 'grad_final_gain': 'grad_w', 'delta_norm_gains': 'delta_w', 'delta_fox_w_in': 'delta_w', 'delta_fox_b_f': 'delta_w', 'delta_hgrn_w_in': 'delta_w', 'delta_hgrn_lb_logits': 'delta_w', 'delta_hgrn_onorm': 'delta_w', 'delta_w_out': 'delta_w', 'delta_final_gain': 'delta_w', 'new_m_norm_gains': 'new_m', 'new_m_fox_w_in': 'new_m', 'new_m_fox_b_f': 'new_m', 'new_m_hgrn_w_in': 'new_m', 'new_m_hgrn_lb_logits': 'new_m', 'new_m_hgrn_onorm': 'new_m', 'new_m_w_out': 'new_m', 'new_m_final_gain': 'new_m', 'new_v_norm_gains': 'new_v', 'new_v_fox_w_in': 'new_v', 'new_v_fox_b_f': 'new_v', 'new_v_hgrn_w_in': 'new_v', 'new_v_hgrn_lb_logits': 'new_v', 'new_v_hgrn_onorm': 'new_v', 'new_v_w_out': 'new_v', 'new_v_final_gain': 'new_v'}


def _forward(args):
    return _fwd_reference(*[args[k] for k in FWD_PARAMS])


def _output_shape():
    def fwd():
        inp = _fwd_setup_inputs(0)
        return _fwd_reference(*[inp[k] for k in FWD_PARAMS])
    out = _jax.eval_shape(fwd)
    return out.shape, out.dtype

N_MICROBATCH = 1
ADAM_LR = 0.001
ADAM_B1 = 0.9
ADAM_B2 = 0.999
ADAM_EPS = 1e-08
ADAM_WD = 0.01
ADAM_STEP = 10
PER_EXAMPLE_BATCH_AXIS = {'x': 0, 'loss_target': 0}
SHARED_INPUTS = []
_WEIGHT_DTYPES = {'norm_gains': _jnp.float32, 'fox_w_in': _jnp.float32, 'fox_b_f': _jnp.float32, 'hgrn_w_in': _jnp.float32, 'hgrn_lb_logits': _jnp.float32, 'hgrn_onorm': _jnp.float32, 'w_out': _jnp.float32, 'final_gain': _jnp.float32}
MOMENT_SCALE = {'norm_gains': 8.339457e-02, 'fox_w_in': 3.770049e-02, 'fox_b_f': 2.423357e-01, 'hgrn_w_in': 4.513505e-02, 'hgrn_lb_logits': 5.453835e-03, 'hgrn_onorm': 6.285301e-02, 'w_out': 5.362334e-02, 'final_gain': 3.197024e+01}


def _to_microbatches(a, axis):
    t = _jnp.moveaxis(a, axis, 0)
    t = t.reshape((N_MICROBATCH, t.shape[0] // N_MICROBATCH) + t.shape[1:])
    return _jnp.moveaxis(t, 1, axis + 1)


def setup_inputs(seed: int = 0) -> dict:
    inp = _fwd_setup_inputs(seed)
    key = _jax.random.fold_in(_jax.random.key(seed), 7919)
    shape, _ = _output_shape()
    out = dict(inp)
    out["loss_target"] = _jax.random.normal(_jax.random.fold_in(key, 0), shape, _jnp.float32)
    for i, name in enumerate(TWIN_WEIGHTS):
        w = inp[name].astype(_jnp.float32)
        if MOMENT_SCALE is None:
            s = _jnp.sqrt(_jnp.mean(_jnp.square(w)) + 1e-30)
        else:
            s = MOMENT_SCALE[name]
        km, kv = _jax.random.split(_jax.random.fold_in(key, i + 1))
        out[name] = w
        out["m_" + name] = s * _jax.random.normal(km, w.shape, _jnp.float32)
        out["v_" + name] = (s * s) * _jax.random.uniform(kv, w.shape, _jnp.float32, 0.5, 1.5)
    if N_MICROBATCH > 1:
        for name, axis in PER_EXAMPLE_BATCH_AXIS.items():
            out[name] = _to_microbatches(out[name], axis)
    return {'x': out['x'], 'norm_gains': out['norm_gains'], 'fox_w_in': out['fox_w_in'], 'fox_b_f': out['fox_b_f'], 'hgrn_w_in': out['hgrn_w_in'], 'hgrn_lb_logits': out['hgrn_lb_logits'], 'hgrn_onorm': out['hgrn_onorm'], 'w_out': out['w_out'], 'final_gain': out['final_gain'], 'loss_target': out['loss_target'], 'm_norm_gains': out['m_norm_gains'], 'm_fox_w_in': out['m_fox_w_in'], 'm_fox_b_f': out['m_fox_b_f'], 'm_hgrn_w_in': out['m_hgrn_w_in'], 'm_hgrn_lb_logits': out['m_hgrn_lb_logits'], 'm_hgrn_onorm': out['m_hgrn_onorm'], 'm_w_out': out['m_w_out'], 'm_final_gain': out['m_final_gain'], 'v_norm_gains': out['v_norm_gains'], 'v_fox_w_in': out['v_fox_w_in'], 'v_fox_b_f': out['v_fox_b_f'], 'v_hgrn_w_in': out['v_hgrn_w_in'], 'v_hgrn_lb_logits': out['v_hgrn_lb_logits'], 'v_hgrn_onorm': out['v_hgrn_onorm'], 'v_w_out': out['v_w_out'], 'v_final_gain': out['v_final_gain']}


def _loss(weights, diff, rest, loss_target):
    with _jax.named_scope("forward"):
        args = {**rest, TWIN_DIFF_INPUT: diff, **{k: w.astype(_WEIGHT_DTYPES[k]) for k, w in weights.items()}}
        y = _forward(args)
    with _jax.named_scope("loss_head"):
        err = _jnp.square(y.astype(_jnp.float32) - loss_target)
        return 0.5 * _jnp.sum(_jnp.mean(err, axis=-1)) if err.ndim else 0.5 * err


def _adamw(w, g, m, v):
    m = ADAM_B1 * m + (1.0 - ADAM_B1) * g
    v = ADAM_B2 * v + (1.0 - ADAM_B2) * _jnp.square(g)
    m_hat = m / (1.0 - ADAM_B1 ** ADAM_STEP)
    v_hat = v / (1.0 - ADAM_B2 ** ADAM_STEP)
    delta = -ADAM_LR * (m_hat / (_jnp.sqrt(v_hat) + ADAM_EPS) + ADAM_WD * w)
    return delta, m, v


def reference(x, norm_gains, fox_w_in, fox_b_f, hgrn_w_in, hgrn_lb_logits, hgrn_onorm, w_out, final_gain, loss_target, m_norm_gains, m_fox_w_in, m_fox_b_f, m_hgrn_w_in, m_hgrn_lb_logits, m_hgrn_onorm, m_w_out, m_final_gain, v_norm_gains, v_fox_w_in, v_fox_b_f, v_hgrn_w_in, v_hgrn_lb_logits, v_hgrn_onorm, v_w_out, v_final_gain):
    given = dict(x=x, norm_gains=norm_gains, fox_w_in=fox_w_in, fox_b_f=fox_b_f, hgrn_w_in=hgrn_w_in, hgrn_lb_logits=hgrn_lb_logits, hgrn_onorm=hgrn_onorm, w_out=w_out, final_gain=final_gain, loss_target=loss_target, m_norm_gains=m_norm_gains, m_fox_w_in=m_fox_w_in, m_fox_b_f=m_fox_b_f, m_hgrn_w_in=m_hgrn_w_in, m_hgrn_lb_logits=m_hgrn_lb_logits, m_hgrn_onorm=m_hgrn_onorm, m_w_out=m_w_out, m_final_gain=m_final_gain, v_norm_gains=v_norm_gains, v_fox_w_in=v_fox_w_in, v_fox_b_f=v_fox_b_f, v_hgrn_w_in=v_hgrn_w_in, v_hgrn_lb_logits=v_hgrn_lb_logits, v_hgrn_onorm=v_hgrn_onorm, v_w_out=v_w_out, v_final_gain=v_final_gain)
    weights = {n: given[n] for n in TWIN_WEIGHTS}
    shared = {n: given[n] for n in SHARED_INPUTS}
    per_example = {n: given[n] for n in ['x']}
    grad_fn = _jax.value_and_grad(_loss, argnums=(0, 1))

    def one_microbatch(ex, loss_target):
        ex = dict(ex)
        diff = ex.pop(TWIN_DIFF_INPUT)
        return grad_fn(weights, diff, {**shared, **ex}, loss_target)

    if N_MICROBATCH == 1:
        loss, (grad_w, grad_x) = one_microbatch(per_example, given["loss_target"])
    else:
        def body(carry, xs):
            loss_sum, grad_sum = carry
            l_k, (gw_k, gx_k) = one_microbatch(xs[0], xs[1])
            with _jax.named_scope("update"):
                return (loss_sum + l_k, _jax.tree.map(_jnp.add, grad_sum, gw_k)), gx_k

        init = (_jnp.zeros((), _jnp.float32), _jax.tree.map(_jnp.zeros_like, weights))
        (loss, grad_w), grad_x = _jax.lax.scan(body, init, (per_example, given["loss_target"]))
    with _jax.named_scope("update"):
        delta_w, new_m, new_v = {}, {}, {}
        for n in TWIN_WEIGHTS:
            delta_w[n], new_m[n], new_v[n] = _adamw(weights[n], grad_w[n], given["m_" + n], given["v_" + n])
    return (loss, grad_x, *[grad_w[n] for n in TWIN_WEIGHTS], *[delta_w[n] for n in TWIN_WEIGHTS],
            *[new_m[n] for n in TWIN_WEIGHTS], *[new_v[n] for n in TWIN_WEIGHTS])
```

```python
import functools

import numpy as np
import jax
import jax.numpy as jnp
from jax import lax
from jax.experimental import pallas as pl
from jax.experimental.pallas import tpu as pltpu

F32 = jnp.float32
BF16 = jnp.bfloat16
SDS = jax.ShapeDtypeStruct
MESH = pl.DeviceIdType.MESH

EPS = 1e-6
ADAM_LR, ADAM_B1, ADAM_B2, ADAM_EPS, ADAM_WD, ADAM_STEP = 0.001, 0.9, 0.999, 1e-08, 0.01, 10

N_DEV = 8
FOX_HEADS = 16
HGRN_HEADS = 16
HEAD_DIM = 128
HGRN_CHUNK = 64
HGRN_LEAF = 16
EXP_CLAMP = 85.0
ATT_BLOCK = 512
NEG = -1e30

VMEM_LIMIT_V7X = 56 * 1024 * 1024


def _params(*sem):
    return pltpu.CompilerParams(dimension_semantics=sem, vmem_limit_bytes=VMEM_LIMIT_V7X)


def _silu(x):
    return x * jax.nn.sigmoid(x)


def _dsilu(x):
    s = jax.nn.sigmoid(x)
    return s * (1.0 + x * (1.0 - s))


def _dot(a, b):
    return jnp.dot(a, b, preferred_element_type=F32)


def _dot_nt(a, b):
    return lax.dot_general(a, b, (((1,), (1,)), ((), ())), preferred_element_type=F32)


def _dot_tn(a, b):
    return lax.dot_general(a, b, (((0,), (0,)), ((), ())), preferred_element_type=F32)


def _mm_nn(a_list, b, out_dtype, *, name, residual=None, tm=512, tn=1024, tk=2048):
    ns = len(a_list)
    M, Ks = a_list[0].shape
    K, N = b.shape
    assert K == ns * Ks and all(a.shape == (M, Ks) for a in a_list)
    tm, tn, tk = min(tm, M), min(tn, N), min(tk, Ks)
    assert M % tm == 0 and N % tn == 0 and Ks % tk == 0
    nks = Ks // tk
    nk = ns * nks
    has_res = residual is not None

    def body(*refs):
        a_refs, b_ref = refs[:ns], refs[ns]
        res_ref = refs[ns + 1] if has_res else None
        o_ref = refs[ns + 1 + has_res]

        def finish(r):
            if has_res:
                r = r + res_ref[...].astype(F32)
            o_ref[...] = r.astype(out_dtype)

        if nk == 1:
            finish(_dot(a_refs[0][...], b_ref[...]))
            return
        acc_ref = refs[ns + 2 + has_res]
        k = pl.program_id(2)

        @pl.when(k == 0)
        def _():
            acc_ref[...] = jnp.zeros_like(acc_ref)

        for s in range(ns):
            def step(s=s):
                acc_ref[...] += _dot(a_refs[s][...], b_ref[...])

            if ns == 1:
                step()
            else:
                pl.when(k // nks == s)(step)

        @pl.when(k == nk - 1)
        def _():
            finish(acc_ref[...])

    def a_map(i, j, k, s):
        return (i, jnp.clip(k - s * nks, 0, nks - 1))

    in_specs = [pl.BlockSpec((tm, tk), functools.partial(a_map, s=s)) for s in range(ns)]
    in_specs.append(pl.BlockSpec((tk, tn), lambda i, j, k: (k, j)))
    args = list(a_list) + [b]
    if has_res:
        in_specs.append(pl.BlockSpec((tm, tn), lambda i, j, k: (i, j)))
        args.append(residual)
    return pl.pallas_call(
        body, grid=(M // tm, N // tn, nk), in_specs=in_specs,
        out_specs=pl.BlockSpec((tm, tn), lambda i, j, k: (i, j)),
        out_shape=SDS((M, N), out_dtype),
        scratch_shapes=[] if nk == 1 else [pltpu.VMEM((tm, tn), F32)],
        compiler_params=_params("parallel", "parallel", "arbitrary"), name=name,
    )(*args)


def _mm_tn(a, b_list, out_dtype, *, name, tm=512, tn=1024, tk=1024):
    ns = len(b_list)
    S, M = a.shape
    Ns = b_list[0].shape[1]
    assert all(b.shape == (S, Ns) for b in b_list)
    tm, tn, tk = min(tm, M), min(tn, Ns), min(tk, S)
    assert M % tm == 0 and Ns % tn == 0 and S % tk == 0
    njs = Ns // tn
    nk = S // tk

    def body(*refs):
        a_ref, b_refs, o_ref, acc_ref = refs[0], refs[1:1 + ns], refs[1 + ns], refs[2 + ns]
        j, k = pl.program_id(1), pl.program_id(2)

        @pl.when(k == 0)
        def _():
            acc_ref[...] = jnp.zeros_like(acc_ref)

        for s in range(ns):
            def step(s=s):
                acc_ref[...] += _dot_tn(a_ref[...], b_refs[s][...])

            if ns == 1:
                step()
            else:
                pl.when(j // njs == s)(step)

        @pl.when(k == nk - 1)
        def _():
            o_ref[...] = acc_ref[...].astype(out_dtype)

    def b_map(i, j, k, s):
        return (k, jnp.clip(j - s * njs, 0, njs - 1))

    in_specs = [pl.BlockSpec((tk, tm), lambda i, j, k: (k, i))]
    in_specs += [pl.BlockSpec((tk, tn), functools.partial(b_map, s=s)) for s in range(ns)]
    return pl.pallas_call(
        body, grid=(M // tm, ns * njs, nk), in_specs=in_specs,
        out_specs=pl.BlockSpec((tm, tn), lambda i, j, k: (i, j)),
        out_shape=SDS((M, ns * Ns), out_dtype),
        scratch_shapes=[pltpu.VMEM((tm, tn), F32)],
        compiler_params=_params("parallel", "parallel", "arbitrary"), name=name,
    )(a, *b_list)


def _mm_nt_rows(w_t, h, *, name, tn=1024):
    R, K = w_t.shape
    S = h.shape[0]
    tn = min(tn, S)

    def body(w_ref, h_ref, o_ref):
        o_ref[...] = _dot_nt(w_ref[...], h_ref[...])

    return pl.pallas_call(
        body, grid=(S // tn,),
        in_specs=[pl.BlockSpec((R, K), lambda i: (0, 0)), pl.BlockSpec((tn, K), lambda i: (i, 0))],
        out_specs=pl.BlockSpec((R, tn), lambda i: (0, i)),
        out_shape=SDS((R, S), F32), compiler_params=_params("parallel"), name=name,
    )(w_t, h)


def _rms_fwd(x, gain, *, name, tm=512):
    S, D = x.shape
    tm = min(tm, S)

    def body(x_ref, g_ref, h_ref):
        xv = x_ref[...]
        r = lax.rsqrt(jnp.mean(xv * xv, axis=-1, keepdims=True) + EPS)
        h_ref[...] = ((xv * r) * g_ref[...]).astype(BF16)

    return pl.pallas_call(
        body, grid=(S // tm,),
        in_specs=[pl.BlockSpec((tm, D), lambda i: (i, 0)), pl.BlockSpec((1, D), lambda i: (0, 0))],
        out_specs=pl.BlockSpec((tm, D), lambda i: (i, 0)),
        out_shape=SDS((S, D), BF16), compiler_params=_params("parallel"), name=name,
    )(x, gain)


def _rms_bwd(x, gain, dh, dres, *, name, tm=256):
    S, D = x.shape
    tm = min(tm, S)

    def body(x_ref, g_ref, dh_ref, dres_ref, dx_ref, dxb_ref, dg_ref):
        @pl.when(pl.program_id(0) == 0)
        def _():
            dg_ref[...] = jnp.zeros_like(dg_ref)

        xv = x_ref[...]
        r = lax.rsqrt(jnp.mean(xv * xv, axis=-1, keepdims=True) + EPS)
        xh = xv * r
        dhv = dh_ref[...].astype(F32)
        dg_ref[...] += jnp.sum(dhv * xh, axis=0, keepdims=True)
        dxh = dhv * g_ref[...]
        dx = r * (dxh - xh * jnp.mean(dxh * xh, axis=-1, keepdims=True)) + dres_ref[...]
        dx_ref[...] = dx
        dxb_ref[...] = dx.astype(BF16)

    row = pl.BlockSpec((tm, D), lambda i: (i, 0))
    vec = pl.BlockSpec((1, D), lambda i: (0, 0))
    return pl.pallas_call(
        body, grid=(S // tm,), in_specs=[row, vec, row, row], out_specs=[row, row, vec],
        out_shape=[SDS((S, D), F32), SDS((S, D), BF16), SDS((1, D), F32)],
        compiler_params=_params("arbitrary"), name=name,
    )(x, gain, dh, dres)


def _loss_head(x, gain, target, *, name, tm=256):
    S, D = x.shape
    tm = min(tm, S)
    assert tm % 8 == 0 and D % 128 == 0

    def body(x_ref, g_ref, t_ref, dx_ref, dxb_ref, loss_ref, dg_ref):
        @pl.when(pl.program_id(0) == 0)
        def _():
            dg_ref[...] = jnp.zeros_like(dg_ref)
            loss_ref[...] = jnp.zeros_like(loss_ref)

        xv = x_ref[...]
        g = g_ref[...]
        r = lax.rsqrt(jnp.mean(xv * xv, axis=-1, keepdims=True) + EPS)
        xh = xv * r
        err = xh * g - t_ref[...]
        e2 = (err * err).reshape(tm // 8, 8, D).sum(axis=0)
        part = e2[:, 0:128]
        for k in range(1, D // 128):
            part = part + e2[:, k * 128:(k + 1) * 128]
        loss_ref[...] += part * (0.5 / D)
        dy = err * (1.0 / D)
        dg_ref[...] += jnp.sum(dy * xh, axis=0, keepdims=True)
        dxh = dy * g
        dx = r * (dxh - xh * jnp.mean(dxh * xh, axis=-1, keepdims=True))
        dx_ref[...] = dx
        dxb_ref[...] = dx.astype(BF16)

    row = pl.BlockSpec((tm, D), lambda i: (i, 0))
    vec = pl.BlockSpec((1, D), lambda i: (0, 0))
    return pl.pallas_call(
        body, grid=(S // tm,), in_specs=[row, vec, row],
        out_specs=[row, row, pl.BlockSpec((8, 128), lambda i: (0, 0)), vec],
        out_shape=[SDS((S, D), F32), SDS((S, D), BF16), SDS((8, 128), F32), SDS((1, D), F32)],
        compiler_params=_params("arbitrary"), name=name,
    )(x, gain, target)


def _split3(x):
    hi = x.astype(BF16)
    r1 = x - hi.astype(F32)
    mid = r1.astype(BF16)
    lo = (r1 - mid.astype(F32)).astype(BF16)
    return hi, mid, lo


def _split2(x):
    hi = x.astype(BF16)
    lo = (x - hi.astype(F32)).astype(BF16)
    return hi, lo


def _fox_gate_fwd(fl_t, b_col, *, name):
    H, S = fl_t.shape
    L = 128
    tri = jnp.asarray(np.triu(np.ones((L, L), np.float32)), BF16)

    def body(fl_ref, b_ref, tri_ref, c_ref, carry):
        @pl.when(pl.program_id(0) == 0)
        def _():
            carry[...] = jnp.zeros_like(carry)

        z = fl_ref[...] + b_ref[...]
        lf = jnp.minimum(z, 0.0) - jnp.log(1.0 + jnp.exp(-jnp.abs(z)))
        hi, mid, lo = _split3(lf)
        t = tri_ref[...]
        c = (_dot(hi, t) + _dot(mid, t)) + _dot(lo, t) + carry[...]
        c_ref[...] = c
        carry[...] = c[:, L - 1:L]

    return pl.pallas_call(
        body, grid=(S // L,),
        in_specs=[pl.BlockSpec((H, L), lambda i: (0, i)), pl.BlockSpec((H, 1), lambda i: (0, 0)),
                  pl.BlockSpec((L, L), lambda i: (0, 0))],
        out_specs=pl.BlockSpec((H, L), lambda i: (0, i)),
        out_shape=SDS((H, S), F32), scratch_shapes=[pltpu.VMEM((H, 1), F32)],
        compiler_params=_params("arbitrary"), name=name,
    )(fl_t, b_col, tri)


def _fox_gate_bwd(dc_row, dc_key, fl_t, b_col, *, name):
    H, S = fl_t.shape
    L = 128
    n = S // L
    tri = jnp.asarray(np.tril(np.ones((L, L), np.float32)), BF16)

    def body(dcr_ref, dck_ref, fl_ref, b_ref, tri_ref, dfl_ref, db_ref, carry):
        @pl.when(pl.program_id(0) == 0)
        def _():
            carry[...] = jnp.zeros_like(carry)
            db_ref[...] = jnp.zeros_like(db_ref)

        hi, mid, lo = _split3(dcr_ref[...] + dck_ref[...])
        t = tri_ref[...]
        dlf = (_dot(hi, t) + _dot(mid, t)) + _dot(lo, t) + carry[...]
        carry[...] = dlf[:, 0:1]
        z = fl_ref[...] + b_ref[...]
        dfl = dlf * jax.nn.sigmoid(-z)
        dfl_ref[...] = dfl
        db_ref[...] += jnp.sum(dfl, axis=1, keepdims=True)

    blk = pl.BlockSpec((H, L), lambda i: (0, n - 1 - i))
    col = pl.BlockSpec((H, 1), lambda i: (0, 0))
    return pl.pallas_call(
        body, grid=(n,), in_specs=[blk, blk, blk, col, pl.BlockSpec((L, L), lambda i: (0, 0))],
        out_specs=[blk, col], out_shape=[SDS((H, S), F32), SDS((H, 1), F32)],
        scratch_shapes=[pltpu.VMEM((H, 1), F32)], compiler_params=_params("arbitrary"), name=name,
    )(dc_row, dc_key, fl_t, b_col, tri)


def _fox_fwd(p0, c4, *, H, name):
    S = p0.shape[0]
    T = c4.shape[-1]
    nq = S // T
    dh = HEAD_DIM
    scale = dh ** -0.5

    def body(q_ref, k_ref, v_ref, g_ref, c_ref, o_ref, y_ref, lse_ref, m_sc, l_sc, acc_sc):
        i = pl.program_id(1)
        q = q_ref[...]
        cref = c_ref[0, i][:, 0:1]
        m_sc[...] = jnp.full_like(m_sc, NEG)
        l_sc[...] = jnp.zeros_like(l_sc)
        acc_sc[...] = jnp.zeros_like(acc_sc)

        def step(j, masked):
            off = pl.multiple_of(j * T, T)
            kj = k_ref[pl.ds(off, T), :]
            vj = v_ref[pl.ds(off, T), :]
            s = _dot_nt(q, kj) * scale + (cref - c_ref[0, j])
            if masked:
                row = lax.broadcasted_iota(jnp.int32, (T, T), 0)
                col = lax.broadcasted_iota(jnp.int32, (T, T), 1)
                s = jnp.where(row >= col, s, NEG)
            m_prev = m_sc[...]
            m_new = jnp.maximum(m_prev, jnp.max(s, axis=-1, keepdims=True))
            alpha = jnp.exp(m_prev - m_new)
            p = jnp.exp(s - m_new)
            l_sc[...] = alpha * l_sc[...] + jnp.sum(p, axis=-1, keepdims=True)
            acc_sc[...] = alpha * acc_sc[...] + _dot(p.astype(BF16), vj)
            m_sc[...] = m_new

        def loop_body(j, carry):
            step(j, False)
            return carry

        lax.fori_loop(0, i, loop_body, 0)
        step(i, True)
        l = l_sc[...]
        o = acc_sc[...] / l
        o_ref[...] = o
        y_ref[...] = (o * _silu(g_ref[...].astype(F32))).astype(BF16)
        lse_ref[0] = m_sc[...] + jnp.log(l)

    blk = lambda off: pl.BlockSpec((T, dh), lambda h, i: (i, off + h))
    full = lambda off: pl.BlockSpec((S, dh), lambda h, i: (0, off + h))
    out_blk = pl.BlockSpec((T, dh), lambda h, i: (i, h))
    return pl.pallas_call(
        body, grid=(H, nq),
        in_specs=[blk(0), full(H), full(2 * H), blk(3 * H),
                  pl.BlockSpec((1, nq, 1, T), lambda h, i: (h, 0, 0, 0))],
        out_specs=[out_blk, out_blk, pl.BlockSpec((1, T, 1), lambda h, i: (h, i, 0))],
        out_shape=[SDS((S, H * dh), F32), SDS((S, H * dh), BF16), SDS((H, S, 1), F32)],
        scratch_shapes=[pltpu.VMEM((T, 1), F32), pltpu.VMEM((T, 1), F32), pltpu.VMEM((T, dh), F32)],
        compiler_params=_params("parallel", "arbitrary"), name=name,
    )(p0, p0, p0, p0, c4)


def _fox_post_bwd(dy, o, p0, *, H, name, tm=512):
    S = dy.shape[0]
    dh = HEAD_DIM
    tm = min(tm, S)

    def body(dy_ref, o_ref, g_ref, do_ref, dg_ref, delta_ref):
        dyv = dy_ref[...].astype(F32)
        ov = o_ref[...]
        g = g_ref[...].astype(F32)
        do = (dyv * _silu(g)).astype(BF16)
        do_ref[...] = do
        dg_ref[...] = (dyv * ov * _dsilu(g)).astype(BF16)
        delta_ref[0] = jnp.sum(do.astype(F32) * ov, axis=-1, keepdims=True)

    blk = pl.BlockSpec((tm, dh), lambda h, i: (i, h))
    return pl.pallas_call(
        body, grid=(H, S // tm),
        in_specs=[blk, blk, pl.BlockSpec((tm, dh), lambda h, i: (i, 3 * H + h))],
        out_specs=[blk, blk, pl.BlockSpec((1, tm, 1), lambda h, i: (h, i, 0))],
        out_shape=[SDS((S, H * dh), BF16), SDS((S, H * dh), BF16), SDS((H, S, 1), F32)],
        compiler_params=_params("parallel", "parallel"), name=name,
    )(dy, o, p0)


def _fox_dq(p0, do, c4, lse_col, delta_col, *, H, name):
    S = p0.shape[0]
    T = c4.shape[-1]
    nq = S // T
    dh = HEAD_DIM
    scale = dh ** -0.5

    def body(q_ref, k_ref, v_ref, do_ref, c_ref, lse_ref, delta_ref, dq_ref, rs_ref, acc_sc, rs_sc):
        i = pl.program_id(1)
        q = q_ref[...]
        do = do_ref[...]
        lse = lse_ref[0]
        delta = delta_ref[0]
        cref = c_ref[0, i][:, 0:1]
        acc_sc[...] = jnp.zeros_like(acc_sc)
        rs_sc[...] = jnp.zeros_like(rs_sc)

        def step(j, masked):
            off = pl.multiple_of(j * T, T)
            kj = k_ref[pl.ds(off, T), :]
            vj = v_ref[pl.ds(off, T), :]
            s = _dot_nt(q, kj) * scale + (cref - c_ref[0, j])
            p = jnp.exp(s - lse)
            if masked:
                row = lax.broadcasted_iota(jnp.int32, (T, T), 0)
                col = lax.broadcasted_iota(jnp.int32, (T, T), 1)
                p = jnp.where(row >= col, p, 0.0)
            ds = p * (_dot_nt(do, vj) - delta)
            acc_sc[...] += _dot(ds.astype(BF16), kj)
            rs_sc[...] += jnp.sum(ds, axis=-1, keepdims=True)

        def loop_body(j, carry):
            step(j, False)
            return carry

        lax.fori_loop(0, i, loop_body, 0)
        step(i, True)
        dq_ref[...] = (acc_sc[...] * scale).astype(BF16)
        rs_ref[0] = rs_sc[...]

    blk = lambda off: pl.BlockSpec((T, dh), lambda h, i: (i, off + h))
    full = lambda off: pl.BlockSpec((S, dh), lambda h, i: (0, off + h))
    colv = pl.BlockSpec((1, T, 1), lambda h, i: (h, i, 0))
    return pl.pallas_call(
        body, grid=(H, nq),
        in_specs=[blk(0), full(H), full(2 * H), blk(0),
                  pl.BlockSpec((1, nq, 1, T), lambda h, i: (h, 0, 0, 0)), colv, colv],
        out_specs=[blk(0), colv], out_shape=[SDS((S, H * dh), BF16), SDS((H, S, 1), F32)],
        scratch_shapes=[pltpu.VMEM((T, dh), F32), pltpu.VMEM((T, 1), F32)],
        compiler_params=_params("parallel", "arbitrary"), name=name,
    )(p0, p0, p0, do, c4, lse_col, delta_col)


def _fox_dkv(p0, do, c4, c_col, lse4, delta4, *, H, name):
    S = p0.shape[0]
    T = c4.shape[-1]
    nq = S // T
    dh = HEAD_DIM
    scale = dh ** -0.5

    def body(q_ref, k_ref, v_ref, do_ref, c_ref, ccol_ref, lse_ref, delta_ref,
             dk_ref, dv_ref, dc_ref, dk_sc, dv_sc, dc_sc):
        j = pl.program_id(1)
        kj = k_ref[...]
        vj = v_ref[...]
        ccol = ccol_ref[0]
        dk_sc[...] = jnp.zeros_like(dk_sc)
        dv_sc[...] = jnp.zeros_like(dv_sc)
        dc_sc[...] = jnp.zeros_like(dc_sc)

        def step(i, masked):
            off = pl.multiple_of(i * T, T)
            qi = q_ref[pl.ds(off, T), :]
            doi = do_ref[pl.ds(off, T), :]
            cref = c_ref[0, i][:, 0:1]
            st = _dot_nt(kj, qi) * scale + (cref - ccol)
            pt = jnp.exp(st - lse_ref[0, i])
            if masked:
                row = lax.broadcasted_iota(jnp.int32, (T, T), 0)
                col = lax.broadcasted_iota(jnp.int32, (T, T), 1)
                pt = jnp.where(col >= row, pt, 0.0)
            dv_sc[...] += _dot(pt.astype(BF16), doi)
            dst = pt * (_dot_nt(vj, doi) - delta_ref[0, i])
            dk_sc[...] += _dot(dst.astype(BF16), qi)
            dc_sc[...] += jnp.sum(dst, axis=-1, keepdims=True)

        step(j, True)

        def loop_body(i, carry):
            step(i, False)
            return carry

        lax.fori_loop(j + 1, nq, loop_body, 0)
        dk_ref[...] = (dk_sc[...] * scale).astype(BF16)
        dv_ref[...] = dv_sc[...].astype(BF16)
        dc_ref[0] = -dc_sc[...]

    blk = lambda off: pl.BlockSpec((T, dh), lambda h, j: (j, off + h))
    full = lambda off: pl.BlockSpec((S, dh), lambda h, j: (0, off + h))
    rows4 = pl.BlockSpec((1, nq, 1, T), lambda h, j: (h, 0, 0, 0))
    colv = pl.BlockSpec((1, T, 1), lambda h, j: (h, j, 0))
    return pl.pallas_call(
        body, grid=(H, nq),
        in_specs=[full(0), blk(H), blk(2 * H), full(0), rows4, colv, rows4, rows4],
        out_specs=[blk(0), blk(0), colv],
        out_shape=[SDS((S, H * dh), BF16), SDS((S, H * dh), BF16), SDS((H, S, 1), F32)],
        scratch_shapes=[pltpu.VMEM((T, dh), F32), pltpu.VMEM((T, dh), F32), pltpu.VMEM((T, 1), F32)],
        compiler_params=_params("parallel", "arbitrary"), name=name,
    )(p0, p0, p0, do, c4, c_col, lse4, delta4)


def _hgrn_levels(C, leaf):
    levels = []
    h = C // 2
    while h >= leaf:
        levels.append(h)
        h //= 2
    return levels


def _hgrn_sum_matrix(C, leaf):
    t = np.arange(C)[:, None]
    u = np.arange(C)[None, :]
    mats = [(u <= t), (u > t)]
    for h in _hgrn_levels(C, leaf):
        start = (t // (2 * h)) * (2 * h)
        mid = start + h - 1
        second = t > mid
        m = np.where(second, (u > mid) & (u <= t), (u > t) & (u <= mid))
        mats.append(m)
    lstart = (t // leaf) * leaf
    mats.append((u >= lstart) & (u <= t))
    return np.concatenate([m.astype(np.float32) for m in mats], axis=0)


def _hgrn_chunk_terms(qr, fz, lb, msum, C, leaf):
    levels = _hgrn_levels(C, leaf)
    sq = _silu(qr)
    t = jnp.exp(-jnp.abs(fz))
    r = 1.0 / (1.0 + t)
    pos = fz >= 0.0
    sp = jnp.where(pos, r, t * r)
    sn = jnp.where(pos, t * r, r)
    f = lb + (1.0 - lb) * sp
    lf = jnp.log(f)
    k = (1.0 - lb) * sn
    hi, lo = _split2(lf)
    dsum = _dot(msum, hi) + _dot(msum, lo)
    b = dsum[0:C]
    kdec = dsum[C:2 * C]
    rowi = lax.broadcasted_iota(jnp.int32, (C, 1), 0)
    lev = []
    for n, h in enumerate(levels):
        e = jnp.exp(dsum[(2 + n) * C:(3 + n) * C])
        second = (rowi % (2 * h)) >= h
        qm = jnp.where(second, sq * e, 0.0).astype(BF16)
        km = jnp.where(second, 0.0, k * e).astype(BF16)
        lev.append((h, e, second, qm, km))
    dleaf = dsum[(2 + len(levels)) * C:(3 + len(levels)) * C]
    eq = jnp.exp(dleaf)
    ek = jnp.exp(jnp.minimum(-dleaf, EXP_CLAMP))
    return dict(sq=sq, sp=sp, sn=sn, f=f, k=k, b=b, kdec=kdec, lev=lev, eq=eq, ek=ek,
                ql=(sq * eq).astype(BF16), kl=(k * ek).astype(BF16),
                qs=(sq * jnp.exp(b)).astype(BF16), ke=(k * jnp.exp(kdec)).astype(BF16),
                e_c=jnp.exp(b[C - 1:C, :]))


def _hgrn_masks(C, leaf, transposed):
    a = lax.broadcasted_iota(jnp.int32, (C, C), 0)
    bb = lax.broadcasted_iota(jnp.int32, (C, C), 1)
    t, s = (bb, a) if transposed else (a, bb)
    lev = [None if 2 * h == C else (t // (2 * h)) == (s // (2 * h)) for h in _hgrn_levels(C, leaf)]
    if leaf == C:
        leafm = s <= t
    else:
        leafm = ((t // leaf) == (s // leaf)) & (s <= t)
    return lev, leafm


def _hgrn_fwd(p1, f1, lb, onorm, *, H, name, tb=512):
    S = p1.shape[0]
    dk = HEAD_DIM
    C = min(HGRN_CHUNK, S)
    leaf = min(HGRN_LEAF, C)
    tb = min(tb, S)
    nc = tb // C
    msum = jnp.asarray(_hgrn_sum_matrix(C, leaf), BF16)

    def body(q_ref, f_ref, v_ref, g_ref, lb_ref, on_ref, ms_ref, o_ref, y_ref, st_ref, st_sc):
        @pl.when(pl.program_id(1) == 0)
        def _():
            st_sc[...] = jnp.zeros_like(st_sc)

        lbv = lb_ref[...]
        onv = on_ref[...]
        msv = ms_ref[...]
        lmask, leafm = _hgrn_masks(C, leaf, False)

        def chunk(n, carry):
            rows = pl.ds(pl.multiple_of(n * C, C), C)
            tm = _hgrn_chunk_terms(q_ref[rows, :].astype(F32), f_ref[rows, :], lbv, msv, C, leaf)
            v = v_ref[rows, :]
            st = st_sc[...]
            st_ref[0, n] = st
            a = jnp.where(leafm, _dot_nt(tm["ql"], tm["kl"]), 0.0)
            for (h, e, second, qm, km), m in zip(tm["lev"], lmask):
                al = _dot_nt(qm, km)
                a = a + (al if m is None else jnp.where(m, al, 0.0))
            o = _dot_nt(tm["qs"], st.astype(BF16)) + _dot(a.astype(BF16), v)
            st_sc[...] = st * tm["e_c"] + _dot(v.T, tm["ke"])
            o_ref[rows, :] = o
            rn = lax.rsqrt(jnp.mean(o * o, axis=-1, keepdims=True) + EPS)
            y = ((o * rn) * onv) * _silu(g_ref[rows, :].astype(F32))
            y_ref[rows, :] = y.astype(BF16)
            return carry

        lax.fori_loop(0, nc, chunk, 0)

    blk = lambda off: pl.BlockSpec((tb, dk), lambda h, i: (i, off + h))
    vec = pl.BlockSpec((1, dk), lambda h, i: (0, h))
    return pl.pallas_call(
        body, grid=(H, S // tb),
        in_specs=[blk(0), blk(0), blk(H), blk(2 * H), vec, vec,
                  pl.BlockSpec(msum.shape, lambda h, i: (0, 0))],
        out_specs=[blk(0), blk(0), pl.BlockSpec((1, nc, dk, dk), lambda h, i: (h, i, 0, 0))],
        out_shape=[SDS((S, H * dk), F32), SDS((S, H * dk), BF16), SDS((H, S // C, dk, dk), F32)],
        scratch_shapes=[pltpu.VMEM((dk, dk), F32)],
        compiler_params=_params("parallel", "arbitrary"), name=name,
    )(p1, f1, p1, p1, lb, onorm, msum)


def _hgrn_post_bwd(dy, o, p1, onorm, *, H, name, tm=512):
    S = dy.shape[0]
    dk = HEAD_DIM
    tm = min(tm, S)

    def body(dy_ref, o_ref, g_ref, on_ref, do_ref, dg_ref, don_ref):
        @pl.when(pl.program_id(1) == 0)
        def _():
            don_ref[...] = jnp.zeros_like(don_ref)

        dyv = dy_ref[...].astype(F32)
        ov = o_ref[...]
        g = g_ref[...].astype(F32)
        onv = on_ref[...]
        rn = lax.rsqrt(jnp.mean(ov * ov, axis=-1, keepdims=True) + EPS)
        oh = ov * rn
        dn = dyv * _silu(g)
        dg_ref[...] = (dyv * (oh * onv) * _dsilu(g)).astype(BF16)
        don_ref[...] += jnp.sum(dn * oh, axis=0, keepdims=True)
        doh = dn * onv
        do_ref[...] = (rn * (doh - oh * jnp.mean(doh * oh, axis=-1, keepdims=True))).astype(BF16)

    blk = pl.BlockSpec((tm, dk), lambda h, i: (i, h))
    vec = pl.BlockSpec((1, dk), lambda h, i: (0, h))
    return pl.pallas_call(
        body, grid=(H, S // tm),
        in_specs=[blk, blk, pl.BlockSpec((tm, dk), lambda h, i: (i, 2 * H + h)), vec],
        out_specs=[blk, blk, vec],
        out_shape=[SDS((S, H * dk), BF16), SDS((S, H * dk), BF16), SDS((1, H * dk), F32)],
        compiler_params=_params("parallel", "arbitrary"), name=name,
    )(dy, o, p1, onorm)


def _hgrn_bwd(p1, f1, lb, do, states, *, H, name, tb=512):
    S = p1.shape[0]
    dk = HEAD_DIM
    C = min(HGRN_CHUNK, S)
    leaf = min(HGRN_LEAF, C)
    tb = min(tb, S)
    nc = tb // C
    nb = S // tb
    msum = jnp.asarray(_hgrn_sum_matrix(C, leaf), BF16)
    rtri = jnp.asarray(np.triu(np.ones((C, C), np.float32)), BF16)

    def body(q_ref, f_ref, v_ref, do_ref, st_ref, lb_ref, ms_ref, rt_ref,
             dq_ref, df_ref, dv_ref, dlb_ref, g_sc):
        @pl.when(pl.program_id(1) == 0)
        def _():
            g_sc[...] = jnp.zeros_like(g_sc)
            dlb_ref[...] = jnp.zeros_like(dlb_ref)

        lbv = lb_ref[...]
        msv = ms_ref[...]
        rtv = rt_ref[...]
        lmask, leafm = _hgrn_masks(C, leaf, False)
        lmask_t, leafm_t = _hgrn_masks(C, leaf, True)

        def chunk(nn, carry):
            n = nc - 1 - nn
            rows = pl.ds(pl.multiple_of(n * C, C), C)
            qr = q_ref[rows, :].astype(F32)
            tm = _hgrn_chunk_terms(qr, f_ref[rows, :], lbv, msv, C, leaf)
            v = v_ref[rows, :]
            dov = do_ref[rows, :]
            st0 = st_ref[0, n]
            gt = g_sc[...]
            gtb = gt.astype(BF16)
            da = _dot_nt(dov, v)
            da_t = _dot_nt(v, dov)
            sq, k = tm["sq"], tm["k"]

            f32 = lambda z: z.astype(F32)
            dal = jnp.where(leafm, da, 0.0).astype(BF16)
            dal_t = jnp.where(leafm_t, da_t, 0.0).astype(BF16)
            dql = _dot(dal, tm["kl"])
            dkl = _dot(dal_t, tm["ql"])
            dsq = dql * tm["eq"]
            dkk = dkl * tm["ek"]
            xq = f32(tm["ql"]) * dql
            xk = f32(tm["kl"]) * dkl
            a_t = jnp.where(leafm_t, _dot_nt(tm["kl"], tm["ql"]), 0.0)
            for (h, e, second, qm, km), m, m_t in zip(tm["lev"], lmask, lmask_t):
                dl = (da if m is None else jnp.where(m, da, 0.0)).astype(BF16)
                dl_t = (da_t if m_t is None else jnp.where(m_t, da_t, 0.0)).astype(BF16)
                dqm = _dot(dl, km)
                dkm = _dot(dl_t, qm)
                dsq = dsq + jnp.where(second, dqm * e, 0.0)
                dkk = dkk + jnp.where(second, 0.0, dkm * e)
                xq = xq + f32(qm) * dqm
                xk = xk + f32(km) * dkm
                al_t = _dot_nt(km, qm)
                a_t = a_t + (al_t if m_t is None else jnp.where(m_t, al_t, 0.0))
            dqs = _dot(dov, st0.astype(BF16))
            dke = _dot(v, gtb)
            dsq = dsq + dqs * jnp.exp(tm["b"])
            dkk = dkk + dke * jnp.exp(tm["kdec"])
            xq = xq + f32(tm["qs"]) * dqs
            xk = xk + f32(tm["ke"]) * dke
            dvv = _dot(a_t.astype(BF16), dov) + _dot_nt(tm["ke"], gtb)
            r_end = jnp.sum(f32(gtb) * _dot(v.T, tm["ke"]) + gt * (st0 * tm["e_c"]), axis=0, keepdims=True)
            g_sc[...] = gt * tm["e_c"] + _dot(dov.T, tm["qs"])
            xh, xm, xl = _split3(xq - xk)
            dlf = (_dot(rtv, xh) + _dot(rtv, xm)) + _dot(rtv, xl) + r_end
            dlf_f = dlf / tm["f"]
            dsp = (1.0 - lbv) * (dlf_f - dkk)
            df_ref[rows, :] = (dsp * (tm["sp"] * tm["sn"])).astype(BF16)
            dq_ref[rows, :] = (dsq * _dsilu(qr)).astype(BF16)
            dv_ref[rows, :] = dvv.astype(BF16)
            dlb_ref[...] += jnp.sum(dlf_f * tm["sn"] - dkk * tm["sn"], axis=0, keepdims=True)
            return carry

        lax.fori_loop(0, nc, chunk, 0)

    blk = lambda off: pl.BlockSpec((tb, dk), lambda h, i: (nb - 1 - i, off + h))
    vec = pl.BlockSpec((1, dk), lambda h, i: (0, h))
    return pl.pallas_call(
        body, grid=(H, nb),
        in_specs=[blk(0), blk(0), blk(H), blk(0),
                  pl.BlockSpec((1, nc, dk, dk), lambda h, i: (h, nb - 1 - i, 0, 0)), vec,
                  pl.BlockSpec(msum.shape, lambda h, i: (0, 0)), pl.BlockSpec((C, C), lambda h, i: (0, 0))],
        out_specs=[blk(0), blk(0), blk(0), vec],
        out_shape=[SDS((S, H * dk), BF16)] * 3 + [SDS((1, H * dk), F32)],
        scratch_shapes=[pltpu.VMEM((dk, dk), F32)],
        compiler_params=_params("parallel", "arbitrary"), name=name,
    )(p1, f1, p1, do, states, lb, msum, rtri)


def _lb_fwd(logits, *, name):
    W = logits.shape[1]

    def body(l_ref, lb_ref):
        l = l_ref[...]
        m = jnp.max(l, axis=0, keepdims=True)
        e = jnp.exp(l - m)
        p = e / jnp.sum(e, axis=0, keepdims=True)
        lb_ref[...] = (p[0:1] + p[1:2]) - p[0:1]

    return pl.pallas_call(body, out_shape=SDS((1, W), F32), name=name)(logits)


STAT_ROWS = 8


def _stats_reduce(stats_all, logits, *, name):
    W = logits.shape[1]

    def body(s_ref, l_ref, g_ref):
        tot = s_ref[0]
        for d in range(1, N_DEV):
            tot = tot + s_ref[d]
        l = l_ref[...]
        m = jnp.max(l, axis=0, keepdims=True)
        e = jnp.exp(l - m)
        p = e / jnp.sum(e, axis=0, keepdims=True)
        dlb = tot[2:3]
        dl0 = -(p[0:1] * p[1:2]) * dlb
        dl1 = (p[1:2] * (1.0 - p[1:2])) * dlb
        g_ref[0:2] = tot[0:2]
        g_ref[2:3] = dl0
        g_ref[3:4] = dl1
        g_ref[4:7] = tot[3:6]
        g_ref[7:8] = jnp.zeros((1, W), F32)

    return pl.pallas_call(body, out_shape=SDS((STAT_ROWS, W), F32), name=name)(stats_all, logits)


def _adamw(w, m, v, g_parts, *, name, tr=256):
    R, C = w.shape
    n = g_parts.shape[0]
    tr = min(tr, R)
    assert R % tr == 0
    c1 = 1.0 / (1.0 - ADAM_B1 ** ADAM_STEP)
    c2 = 1.0 / (1.0 - ADAM_B2 ** ADAM_STEP)

    def body(w_ref, m_ref, v_ref, g_ref, go_ref, d_ref, mo_ref, vo_ref):
        g = g_ref[0].astype(F32)
        for k in range(1, n):
            g = g + g_ref[k].astype(F32)
        mn = ADAM_B1 * m_ref[...] + (1.0 - ADAM_B1) * g
        vn = ADAM_B2 * v_ref[...] + (1.0 - ADAM_B2) * (g * g)
        d_ref[...] = -ADAM_LR * ((mn * c1) / (jnp.sqrt(vn * c2) + ADAM_EPS) + ADAM_WD * w_ref[...])
        go_ref[...] = g
        mo_ref[...] = mn
        vo_ref[...] = vn

    blk = pl.BlockSpec((tr, C), lambda i: (i, 0))
    return pl.pallas_call(
        body, grid=(R // tr,), in_specs=[blk, blk, blk, pl.BlockSpec((n, tr, C), lambda i: (0, i, 0))],
        out_specs=[blk] * 4, out_shape=[SDS((R, C), F32)] * 4,
        compiler_params=_params("parallel"), name=name,
    )(w, m, v, g_parts)


ANY = pl.BlockSpec(memory_space=pl.ANY)


def _all_gather(shards, out_shapes, views, *, name):
    n = len(shards)

    def body(*refs):
        ins, outs = refs[:n], refs[n:2 * n]
        send_sems, recv_sems, local_sems = refs[2 * n:]
        x, y, c = lax.axis_index("x"), lax.axis_index("y"), lax.axis_index("c")
        me, sibling = (x, y, c), (x, y, 1 - c)
        chips = [(1 - x, y), (x, 1 - y), (1 - x, 1 - y)]

        def dev(p):
            return 4 * p[0] + 2 * p[1] + p[2]

        def copy(a, k, block, to, src=None):
            dst = views[a](outs[a], dev(block))
            return pltpu.make_async_remote_copy(
                src_ref=dst if src is None else src, dst_ref=dst,
                send_sem=send_sems.at[a, k], recv_sem=recv_sems.at[a, k],
                device_id=to, device_id_type=MESH)

        mine, first, passed = [], [], []
        for a in range(n):
            cp = pltpu.make_async_copy(ins[a], views[a](outs[a], dev(me)), local_sems.at[a])
            cp.start()
            mine.append(cp)
            first.append(copy(a, 0, me, sibling, src=ins[a]))
            first += [copy(a, 1 + j, me, (*chip, c), src=ins[a]) for j, chip in enumerate(chips)]
        for cp in first:
            cp.start()
        for j, chip in enumerate(chips):
            for a in range(n):
                copy(a, 1 + j, (*chip, c), me).wait_recv()
                cp = copy(a, 4 + j, (*chip, c), sibling)
                cp.start()
                passed.append(cp)
        for a in range(n):
            copy(a, 0, sibling, me).wait_recv()
            for j, chip in enumerate(chips):
                copy(a, 4 + j, (*chip, 1 - c), me).wait_recv()
        for cp in first + passed:
            cp.wait_send()
        for cp in mine:
            cp.wait()

    return pl.pallas_call(
        body, in_specs=[ANY] * n, out_specs=[ANY] * n, out_shape=list(out_shapes),
        scratch_shapes=[pltpu.SemaphoreType.DMA((n, 7)), pltpu.SemaphoreType.DMA((n, 7)),
                        pltpu.SemaphoreType.DMA((n,))],
        name=name,
    )(*shards)


def _exchange_partials(parts, shard_shapes, views, *, name):
    n = len(parts)

    def body(*refs):
        ins, outs = refs[:n], refs[n:2 * n]
        send_sems, recv_sems, local_sems = refs[2 * n:]
        x, y, c = lax.axis_index("x"), lax.axis_index("y"), lax.axis_index("c")
        copies = []
        for a in range(n):
            cp = pltpu.make_async_copy(views[a](ins[a], 4 * x + 2 * y + c), outs[a].at[0], local_sems.at[a])
            cp.start()
            copies.append(cp)
        remote = []
        for m in range(1, N_DEV):
            px = 1 - x if m & 4 else x
            py = 1 - y if m & 2 else y
            pc = 1 - c if m & 1 else c
            for a in range(n):
                cp = pltpu.make_async_remote_copy(
                    src_ref=views[a](ins[a], 4 * px + 2 * py + pc), dst_ref=outs[a].at[m],
                    send_sem=send_sems.at[a, m - 1], recv_sem=recv_sems.at[a, m - 1],
                    device_id=(px, py, pc), device_id_type=MESH)
                cp.start()
                remote.append(cp)
        for cp in remote:
            cp.wait_send()
            cp.wait_recv()
        for cp in copies:
            cp.wait()

    return pl.pallas_call(
        body, in_specs=[ANY] * n, out_specs=[ANY] * n,
        out_shape=[SDS((N_DEV,) + tuple(s), p.dtype) for s, p in zip(shard_shapes, parts)],
        scratch_shapes=[pltpu.SemaphoreType.DMA((n, 7)), pltpu.SemaphoreType.DMA((n, 7)),
                        pltpu.SemaphoreType.DMA((n,))],
        name=name,
    )(*parts)


def kernel(x, norm_gains, fox_w_in, fox_b_f, hgrn_w_in, hgrn_lb_logits, hgrn_onorm, w_out, final_gain, loss_target, m_norm_gains, m_fox_w_in, m_fox_b_f, m_hgrn_w_in, m_hgrn_lb_logits, m_hgrn_onorm, m_w_out, m_final_gain, v_norm_gains, v_fox_w_in, v_fox_b_f, v_hgrn_w_in, v_hgrn_lb_logits, v_hgrn_onorm, v_w_out, v_final_gain):
    _, S, D = x.shape
    H = FOX_HEADS
    W = H * HEAD_DIM
    assert HGRN_HEADS == H and w_out.shape[2] == D
    cf = fox_w_in.shape[2]
    ch = hgrn_w_in.shape[2]
    ro = w_out.shape[1]
    co = hgrn_onorm.shape[1]
    assert N_DEV * cf == 4 * W + H and N_DEV * ch == 4 * W and N_DEV * ro == W and N_DEV * co == W
    T = min(ATT_BLOCK, S)
    nq = S // T

    x2 = x.reshape(S, D)
    tgt = loss_target.reshape(S, D)

    wf_g, wh, wo, onorm = _all_gather(
        [fox_w_in[0].astype(BF16), hgrn_w_in[0].astype(BF16), w_out.astype(BF16), hgrn_onorm],
        [SDS((N_DEV, D, cf), BF16), SDS((D, 4 * W), BF16), SDS((2, W, D), BF16), SDS((1, W), F32)],
        [lambda r, p: r.at[p],
         lambda r, p: r.at[:, pl.ds(pl.multiple_of(p * ch, ch), ch)],
         lambda r, p: r.at[:, pl.ds(pl.multiple_of(p * ro, ro), ro), :],
         lambda r, p: r.at[:, pl.ds(pl.multiple_of(p * co, co), co)]],
        name="gather_weights")
    wf = jnp.transpose(wf_g, (1, 0, 2)).reshape(D, N_DEV * cf)
    wf_main = jnp.concatenate([wf[:, :3 * W], wf[:, 3 * W + H:]], axis=1)
    wfl_t = wf[:, 3 * W:3 * W + H].T
    wf_main_t = wf_main.T
    wh_qig = jnp.concatenate([wh[:, :W], wh[:, 2 * W:]], axis=1)
    wh_f = wh[:, W:2 * W]
    wh_t = wh.T
    wo0, wo1 = wo[0], wo[1]
    wo0_t, wo1_t = wo0.T, wo1.T

    h0 = _rms_fwd(x2, norm_gains[0:1], name="rms0_fwd")
    p0 = _mm_nn([h0], wf_main, BF16, name="fox_in_proj")
    fl_t = _mm_nt_rows(wfl_t, h0, name="fox_forget_proj")
    b_col = fox_b_f.reshape(H, 1)
    c_t = _fox_gate_fwd(fl_t, b_col, name="fox_gate_fwd")
    c4 = c_t.reshape(H, nq, 1, T)
    o0, y0, lse = _fox_fwd(p0, c4, H=H, name="fox_attn_fwd")
    x1 = _mm_nn([y0], wo0, F32, residual=x2, name="fox_out_proj")

    lb = _lb_fwd(hgrn_lb_logits, name="hgrn_lower_bound")
    h1 = _rms_fwd(x1, norm_gains[1:2], name="rms1_fwd")
    p1 = _mm_nn([h1], wh_qig, BF16, name="hgrn_in_proj")
    f1 = _mm_nn([h1], wh_f, F32, name="hgrn_forget_proj")
    o1, y1, states = _hgrn_fwd(p1, f1, lb, onorm, H=H, name="hgrn_fwd")
    xo = _mm_nn([y1], wo1, F32, residual=x1, name="hgrn_out_proj")

    dx2, dx2b, loss_part, dgf = _loss_head(xo, final_gain.reshape(1, D), tgt, name="loss_head")
    loss = lax.psum(jnp.sum(loss_part), ("x", "y", "c"))

    dy1 = _mm_nn([dx2b], wo1_t, BF16, name="hgrn_out_proj_dx")
    dwo1 = _mm_tn(y1, [dx2b], BF16, name="hgrn_out_proj_dw")
    do1, dg1, donorm = _hgrn_post_bwd(dy1, o1, p1, onorm, H=H, name="hgrn_post_bwd")
    dq1, df1, di1, dlb = _hgrn_bwd(p1, f1, lb, do1, states, H=H, name="hgrn_bwd")
    segs1 = [dq1, df1, di1, dg1]
    dh1 = _mm_nn(segs1, wh_t, BF16, name="hgrn_in_proj_dx")
    dwh = _mm_tn(h1, segs1, BF16, name="hgrn_in_proj_dw")
    dx1, dx1b, dng1 = _rms_bwd(x1, norm_gains[1:2], dh1, dx2, name="rms1_bwd")

    dy0 = _mm_nn([dx1b], wo0_t, BF16, name="fox_out_proj_dx")
    dwo0 = _mm_tn(y0, [dx1b], BF16, name="fox_out_proj_dw")
    do0, dg0, delta = _fox_post_bwd(dy0, o0, p0, H=H, name="fox_post_bwd")
    dq0, dc_row = _fox_dq(p0, do0, c4, lse, delta, H=H, name="fox_attn_dq")
    dk0, dv0, dc_key = _fox_dkv(p0, do0, c4, c_t.reshape(H, S, 1), lse.reshape(H, nq, 1, T),
                                delta.reshape(H, nq, 1, T), H=H, name="fox_attn_dkv")
    dfl_t, dbf = _fox_gate_bwd(dc_row.reshape(H, S), dc_key.reshape(H, S), fl_t, b_col, name="fox_gate_bwd")
    dfl_tb = dfl_t.astype(BF16)
    dwfl_t = _mm_nn([dfl_tb], h0, BF16, name="fox_forget_proj_dw")
    dh0_f = _mm_nn([dfl_tb.T], wfl_t, BF16, name="fox_forget_proj_dx")
    segs0 = [dq0, dk0, dv0, dg0]
    dh0 = _mm_nn(segs0, wf_main_t, BF16, residual=dh0_f, name="fox_in_proj_dx")
    dwf_main = _mm_tn(h0, segs0, BF16, name="fox_in_proj_dw")
    grad_x, _, dng0 = _rms_bwd(x2, norm_gains[0:1], dh0, dx1, name="rms0_bwd")

    dwf = jnp.concatenate([dwf_main[:, :3 * W], dwfl_t.T, dwf_main[:, 3 * W:]], axis=1)
    dwf_blocks = jnp.transpose(dwf.reshape(D, N_DEV, cf), (1, 0, 2))
    dwo = jnp.stack([dwo0, dwo1])
    rf, rh, ro_parts = _exchange_partials(
        [dwf_blocks, dwh, dwo], [(D, cf), (D, ch), (2, ro, D)],
        [lambda r, p: r.at[p],
         lambda r, p: r.at[:, pl.ds(pl.multiple_of(p * ch, ch), ch)],
         lambda r, p: r.at[:, pl.ds(pl.multiple_of(p * ro, ro), ro), :]],
        name="exchange_weight_grads")

    pad = lambda a: jnp.pad(a, ((0, 0), (0, W - a.shape[1])))
    stats = jnp.concatenate([dng0, dng1, dlb, dgf, pad(dbf.reshape(1, H)), donorm,
                             jnp.zeros((2, W), F32)], axis=0)
    assert D == W
    (stats_all,) = _all_gather([stats], [SDS((N_DEV, STAT_ROWS, W), F32)], [lambda r, p: r.at[p]],
                               name="gather_small_grads")
    g_small = _stats_reduce(stats_all, hgrn_lb_logits, name="reduce_small_grads")
    me = 4 * lax.axis_index("x") + 2 * lax.axis_index("y") + lax.axis_index("c")
    g_onorm = lax.dynamic_slice_in_dim(g_small[6:7], me * co, co, axis=1)

    def upd(w, m, v, parts, name):
        shp = w.shape
        r2 = (-1, shp[-1])
        g, d, mn, vn = _adamw(w.reshape(r2), m.reshape(r2), v.reshape(r2),
                              parts.reshape((parts.shape[0],) + w.reshape(r2).shape), name=name)
        return g.reshape(shp), d.reshape(shp), mn.reshape(shp), vn.reshape(shp)

    res = {
        "norm_gains": upd(norm_gains, m_norm_gains, v_norm_gains, g_small[None, 0:2], "adamw_norm_gains"),
        "fox_w_in": upd(fox_w_in, m_fox_w_in, v_fox_w_in, rf, "adamw_fox_w_in"),
        "fox_b_f": upd(fox_b_f, m_fox_b_f, v_fox_b_f, g_small[None, 5:6, :H], "adamw_fox_b_f"),
        "hgrn_w_in": upd(hgrn_w_in, m_hgrn_w_in, v_hgrn_w_in, rh, "adamw_hgrn_w_in"),
        "hgrn_lb_logits": upd(hgrn_lb_logits, m_hgrn_lb_logits, v_hgrn_lb_logits, g_small[None, 2:4],
                              "adamw_hgrn_lb_logits"),
        "hgrn_onorm": upd(hgrn_onorm, m_hgrn_onorm, v_hgrn_onorm, g_onorm[None], "adamw_hgrn_onorm"),
        "w_out": upd(w_out, m_w_out, v_w_out, ro_parts, "adamw_w_out"),
        "final_gain": upd(final_gain.reshape(1, D), m_final_gain.reshape(1, D), v_final_gain.reshape(1, D),
                          g_small[None, 4:5], "adamw_final_gain"),
    }
    order = ["norm_gains", "fox_w_in", "fox_b_f", "hgrn_w_in", "hgrn_lb_logits", "hgrn_onorm", "w_out", "final_gain"]
    fix = lambda n, a: a.reshape(D) if n == "final_gain" else a
    outs = [loss, grad_x.reshape(1, S, D)]
    for k in range(4):
        outs += [fix(n, res[n][k]) for n in order]
    return tuple(outs)
```

```python
import functools

import numpy as np
import jax
import jax.numpy as jnp
from jax import lax
from jax.experimental import pallas as pl
from jax.experimental.pallas import tpu as pltpu

F32 = jnp.float32
BF16 = jnp.bfloat16
SDS = jax.ShapeDtypeStruct
MESH = pl.DeviceIdType.MESH

EPS = 1e-6
ADAM_LR, ADAM_B1, ADAM_B2, ADAM_EPS, ADAM_WD, ADAM_STEP = 0.001, 0.9, 0.999, 1e-08, 0.01, 10

N_DEV = 8
FOX_HEADS = 16
HGRN_HEADS = 16
HEAD_DIM = 128
HGRN_CHUNK = 64
HGRN_LEAF = 16
HGRN_HEADS_PER_STEP = 4
EXP_CLAMP = 85.0
ATT_BLOCK = 512
ATT_HEADS_PER_STEP = 2
NEG = -1e30
LOG2E = 1.4426950408889634
LN2 = 0.6931471805599453

VMEM_LIMIT_V7X = 56 * 1024 * 1024


def _params(*sem):
    return pltpu.CompilerParams(dimension_semantics=sem, vmem_limit_bytes=VMEM_LIMIT_V7X)


def _silu(x):
    return x * jax.nn.sigmoid(x)


def _dsilu(x):
    s = jax.nn.sigmoid(x)
    return s * (1.0 + x * (1.0 - s))


def _dot(a, b):
    return jnp.dot(a, b, preferred_element_type=F32)


def _dot_nt(a, b):
    return lax.dot_general(a, b, (((1,), (1,)), ((), ())), preferred_element_type=F32)


def _dot_tn(a, b):
    return lax.dot_general(a, b, (((0,), (0,)), ((), ())), preferred_element_type=F32)


def _mm_nn(a_list, b, out_dtype, *, name, residual=None, scale_cols=None, tm=512, tn=1024, tk=2048):
    ns = len(a_list)
    M, Ks = a_list[0].shape
    K, N = b.shape
    assert K == ns * Ks and all(a.shape == (M, Ks) for a in a_list)
    tm, tn, tk = min(tm, M), min(tn, N), min(tk, Ks)
    assert M % tm == 0 and N % tn == 0 and Ks % tk == 0
    assert scale_cols is None or scale_cols[0] % tn == 0
    nks = Ks // tk
    nk = ns * nks
    has_res = residual is not None

    def body(*refs):
        a_refs, b_ref = refs[:ns], refs[ns]
        res_ref = refs[ns + 1] if has_res else None
        o_ref = refs[ns + 1 + has_res]

        def finish(r):
            if has_res:
                r = r + res_ref[...].astype(F32)
            if scale_cols is not None:
                r = r * jnp.where(pl.program_id(1) < scale_cols[0] // tn, scale_cols[1], 1.0)
            o_ref[...] = r.astype(out_dtype)

        if nk == 1:
            finish(_dot(a_refs[0][...], b_ref[...]))
            return
        acc_ref = refs[ns + 2 + has_res]
        k = pl.program_id(2)

        @pl.when(k == 0)
        def _():
            acc_ref[...] = jnp.zeros_like(acc_ref)

        for s in range(ns):
            def step(s=s):
                acc_ref[...] += _dot(a_refs[s][...], b_ref[...])

            if ns == 1:
                step()
            else:
                pl.when(k // nks == s)(step)

        @pl.when(k == nk - 1)
        def _():
            finish(acc_ref[...])

    def a_map(i, j, k, s):
        return (i, jnp.clip(k - s * nks, 0, nks - 1))

    in_specs = [pl.BlockSpec((tm, tk), functools.partial(a_map, s=s)) for s in range(ns)]
    in_specs.append(pl.BlockSpec((tk, tn), lambda i, j, k: (k, j)))
    args = list(a_list) + [b]
    if has_res:
        in_specs.append(pl.BlockSpec((tm, tn), lambda i, j, k: (i, j)))
        args.append(residual)
    return pl.pallas_call(
        body, grid=(M // tm, N // tn, nk), in_specs=in_specs,
        out_specs=pl.BlockSpec((tm, tn), lambda i, j, k: (i, j)),
        out_shape=SDS((M, N), out_dtype),
        scratch_shapes=[] if nk == 1 else [pltpu.VMEM((tm, tn), F32)],
        compiler_params=_params("parallel", "parallel", "arbitrary"), name=name,
    )(*args)


def _mm_tn(a, b_list, out_dtype, *, name, tm=1024, tn=2048, tk=512):
    ns = len(b_list)
    S, M = a.shape
    Ns = b_list[0].shape[1]
    assert all(b.shape == (S, Ns) for b in b_list)
    tm, tn, tk = min(tm, M), min(tn, Ns), min(tk, S)
    assert M % tm == 0 and Ns % tn == 0 and S % tk == 0
    njs = Ns // tn
    nk = S // tk

    def body(*refs):
        a_ref, b_refs, o_ref, acc_ref = refs[0], refs[1:1 + ns], refs[1 + ns], refs[2 + ns]
        j, k = pl.program_id(1), pl.program_id(2)

        @pl.when(k == 0)
        def _():
            acc_ref[...] = jnp.zeros_like(acc_ref)

        for s in range(ns):
            def step(s=s):
                acc_ref[...] += _dot_tn(a_ref[...], b_refs[s][...])

            if ns == 1:
                step()
            else:
                pl.when(j // njs == s)(step)

        @pl.when(k == nk - 1)
        def _():
            o_ref[...] = acc_ref[...].astype(out_dtype)

    def b_map(i, j, k, s):
        return (k, jnp.clip(j - s * njs, 0, njs - 1))

    in_specs = [pl.BlockSpec((tk, tm), lambda i, j, k: (k, i))]
    in_specs += [pl.BlockSpec((tk, tn), functools.partial(b_map, s=s)) for s in range(ns)]
    return pl.pallas_call(
        body, grid=(M // tm, ns * njs, nk), in_specs=in_specs,
        out_specs=pl.BlockSpec((tm, tn), lambda i, j, k: (i, j)),
        out_shape=SDS((M, ns * Ns), out_dtype),
        scratch_shapes=[pltpu.VMEM((tm, tn), F32)],
        compiler_params=_params("parallel", "parallel", "arbitrary"), name=name,
    )(a, *b_list)


def _mm_nt_rows(w_t, h, *, name, tn=1024):
    R, K = w_t.shape
    S = h.shape[0]
    tn = min(tn, S)

    def body(w_ref, h_ref, o_ref):
        o_ref[...] = _dot_nt(w_ref[...], h_ref[...])

    return pl.pallas_call(
        body, grid=(S // tn,),
        in_specs=[pl.BlockSpec((R, K), lambda i: (0, 0)), pl.BlockSpec((tn, K), lambda i: (i, 0))],
        out_specs=pl.BlockSpec((R, tn), lambda i: (0, i)),
        out_shape=SDS((R, S), F32), compiler_params=_params("parallel"), name=name,
    )(w_t, h)


def _rms_fwd(x, gain, *, name, tm=512):
    S, D = x.shape
    tm = min(tm, S)

    def body(x_ref, g_ref, h_ref):
        xv = x_ref[...]
        r = lax.rsqrt(jnp.mean(xv * xv, axis=-1, keepdims=True) + EPS)
        h_ref[...] = ((xv * r) * g_ref[...]).astype(BF16)

    return pl.pallas_call(
        body, grid=(S // tm,),
        in_specs=[pl.BlockSpec((tm, D), lambda i: (i, 0)), pl.BlockSpec((1, D), lambda i: (0, 0))],
        out_specs=pl.BlockSpec((tm, D), lambda i: (i, 0)),
        out_shape=SDS((S, D), BF16), compiler_params=_params("parallel"), name=name,
    )(x, gain)


def _rms_bwd(x, gain, dh, dres, *, name, tm=256):
    S, D = x.shape
    tm = min(tm, S)

    def body(x_ref, g_ref, dh_ref, dres_ref, dx_ref, dxb_ref, dg_ref):
        @pl.when(pl.program_id(0) == 0)
        def _():
            dg_ref[...] = jnp.zeros_like(dg_ref)

        xv = x_ref[...]
        r = lax.rsqrt(jnp.mean(xv * xv, axis=-1, keepdims=True) + EPS)
        xh = xv * r
        dhv = dh_ref[...].astype(F32)
        dg_ref[...] += jnp.sum(dhv * xh, axis=0, keepdims=True)
        dxh = dhv * g_ref[...]
        dx = r * (dxh - xh * jnp.mean(dxh * xh, axis=-1, keepdims=True)) + dres_ref[...]
        dx_ref[...] = dx
        dxb_ref[...] = dx.astype(BF16)

    row = pl.BlockSpec((tm, D), lambda i: (i, 0))
    vec = pl.BlockSpec((1, D), lambda i: (0, 0))
    return pl.pallas_call(
        body, grid=(S // tm,), in_specs=[row, vec, row, row], out_specs=[row, row, vec],
        out_shape=[SDS((S, D), F32), SDS((S, D), BF16), SDS((1, D), F32)],
        compiler_params=_params("arbitrary"), name=name,
    )(x, gain, dh, dres)


def _loss_head(x, gain, target, *, name, tm=256):
    S, D = x.shape
    tm = min(tm, S)
    assert tm % 8 == 0 and D % 128 == 0

    def body(x_ref, g_ref, t_ref, dx_ref, dxb_ref, loss_ref, dg_ref):
        @pl.when(pl.program_id(0) == 0)
        def _():
            dg_ref[...] = jnp.zeros_like(dg_ref)
            loss_ref[...] = jnp.zeros_like(loss_ref)

        xv = x_ref[...]
        g = g_ref[...]
        r = lax.rsqrt(jnp.mean(xv * xv, axis=-1, keepdims=True) + EPS)
        xh = xv * r
        err = xh * g - t_ref[...]
        e2 = (err * err).reshape(tm // 8, 8, D).sum(axis=0)
        part = e2[:, 0:128]
        for k in range(1, D // 128):
            part = part + e2[:, k * 128:(k + 1) * 128]
        loss_ref[...] += part * (0.5 / D)
        dy = err * (1.0 / D)
        dg_ref[...] += jnp.sum(dy * xh, axis=0, keepdims=True)
        dxh = dy * g
        dx = r * (dxh - xh * jnp.mean(dxh * xh, axis=-1, keepdims=True))
        dx_ref[...] = dx
        dxb_ref[...] = dx.astype(BF16)

    row = pl.BlockSpec((tm, D), lambda i: (i, 0))
    vec = pl.BlockSpec((1, D), lambda i: (0, 0))
    return pl.pallas_call(
        body, grid=(S // tm,), in_specs=[row, vec, row],
        out_specs=[row, row, pl.BlockSpec((8, 128), lambda i: (0, 0)), vec],
        out_shape=[SDS((S, D), F32), SDS((S, D), BF16), SDS((8, 128), F32), SDS((1, D), F32)],
        compiler_params=_params("arbitrary"), name=name,
    )(x, gain, target)


def _split3(x):
    hi = x.astype(BF16)
    r1 = x - hi.astype(F32)
    mid = r1.astype(BF16)
    lo = (r1 - mid.astype(F32)).astype(BF16)
    return hi, mid, lo


def _split2(x):
    hi = x.astype(BF16)
    lo = (x - hi.astype(F32)).astype(BF16)
    return hi, lo


def _fox_gate_fwd(fl_t, b_col, *, name):
    H, S = fl_t.shape
    L = 128
    tri = jnp.asarray(np.triu(np.ones((L, L), np.float32)), BF16)

    def body(fl_ref, b_ref, tri_ref, hi_ref, mid_ref, lo_ref, carry):
        @pl.when(pl.program_id(0) == 0)
        def _():
            carry[...] = jnp.zeros_like(carry)

        z = fl_ref[...] + b_ref[...]
        lf = jnp.minimum(z, 0.0) - jnp.log(1.0 + jnp.exp(-jnp.abs(z)))
        hi, mid, lo = _split3(lf)
        t = tri_ref[...]
        c = (_dot(hi, t) + _dot(mid, t)) + _dot(lo, t) + carry[...]
        carry[...] = c[:, L - 1:L]
        hi_ref[...], mid_ref[...], lo_ref[...] = _split3(c * (-LOG2E))

    blk = pl.BlockSpec((H, L), lambda i: (0, i))
    return pl.pallas_call(
        body, grid=(S // L,),
        in_specs=[blk, pl.BlockSpec((H, 1), lambda i: (0, 0)), pl.BlockSpec((L, L), lambda i: (0, 0))],
        out_specs=[blk] * 3, out_shape=[SDS((H, S), BF16)] * 3, scratch_shapes=[pltpu.VMEM((H, 1), F32)],
        compiler_params=_params("arbitrary"), name=name,
    )(fl_t, b_col, tri)


def _fox_gate_bwd(dc_row, dc_key, fl_t, b_col, *, name):
    H, S = fl_t.shape
    L = 128
    n = S // L
    tri = jnp.asarray(np.tril(np.ones((L, L), np.float32)), BF16)

    def body(dcr_ref, dck_ref, fl_ref, b_ref, tri_ref, dfl_ref, db_ref, carry):
        @pl.when(pl.program_id(0) == 0)
        def _():
            carry[...] = jnp.zeros_like(carry)
            db_ref[...] = jnp.zeros_like(db_ref)

        hi, mid, lo = _split3(dcr_ref[...] + dck_ref[...])
        t = tri_ref[...]
        dlf = (_dot(hi, t) + _dot(mid, t)) + _dot(lo, t) + carry[...]
        carry[...] = dlf[:, 0:1]
        z = fl_ref[...] + b_ref[...]
        dfl = dlf * jax.nn.sigmoid(-z)
        dfl_ref[...] = dfl
        db_ref[...] += jnp.sum(dfl, axis=1, keepdims=True)

    blk = pl.BlockSpec((H, L), lambda i: (0, n - 1 - i))
    col = pl.BlockSpec((H, 1), lambda i: (0, 0))
    return pl.pallas_call(
        body, grid=(n,), in_specs=[blk, blk, blk, col, pl.BlockSpec((L, L), lambda i: (0, 0))],
        out_specs=[blk, col], out_shape=[SDS((H, S), F32), SDS((H, 1), F32)],
        scratch_shapes=[pltpu.VMEM((H, 1), F32)], compiler_params=_params("arbitrary"), name=name,
    )(dc_row, dc_key, fl_t, b_col, tri)


AUG = HEAD_DIM


def _lane_select(cols, shape):
    lane = lax.broadcasted_iota(jnp.int32, shape, 1)
    out = jnp.zeros(shape, BF16)
    for k, c in reversed(list(enumerate(cols))):
        c = jnp.full(shape, c, BF16) if isinstance(c, (int, float)) else jnp.broadcast_to(c, shape).astype(BF16)
        out = jnp.where(lane == k, c, out)
    return out


def _fox_key_aug(b_hi, b_mid, b_lo):
    H, S = b_hi.shape
    ones = jnp.ones((H, S), BF16)
    ka = jnp.stack([b_hi, b_mid, b_lo, ones, ones, ones], axis=-1)
    ka = jnp.pad(ka, ((0, 0), (0, 0), (0, AUG - 6)))
    return jnp.transpose(ka, (1, 0, 2)).reshape(S, H * AUG)


def _fox_fwd(p0, kaug, *, H, name):
    S = p0.shape[0]
    T = min(ATT_BLOCK, S)
    nq = S // T
    dh = HEAD_DIM
    G = ATT_HEADS_PER_STEP
    assert H % G == 0

    def body(q_ref, k_ref, ka_ref, v_ref, g_ref, o_ref, y_ref, qa_ref, m_sc, acc_sc):
        i = pl.program_id(1)
        qaug = _lane_select([1.0, 1.0, 1.0], (T, AUG))
        ones = jnp.ones((T, dh), BF16)
        m_sc[...] = jnp.full_like(m_sc, NEG)
        acc_sc[...] = jnp.zeros_like(acc_sc)

        def step(j, masked):
            rows = pl.ds(pl.multiple_of(j * T, T), T)
            for g in range(G):
                hd = slice(g * dh, (g + 1) * dh)
                q = jnp.concatenate([q_ref[:, hd], qaug], axis=1)
                kj = jnp.concatenate([k_ref[rows, hd], ka_ref[rows, hd]], axis=1)
                vj = jnp.concatenate([v_ref[rows, hd], ones], axis=1)
                t = _dot_nt(q, kj)
                if masked:
                    row = lax.broadcasted_iota(jnp.int32, (T, T), 0)
                    col = lax.broadcasted_iota(jnp.int32, (T, T), 1)
                    t = jnp.where(row >= col, t, NEG)
                m_prev = m_sc[g]
                m_new = jnp.maximum(m_prev, jnp.max(t, axis=-1, keepdims=True))
                p = jnp.exp2(t - jnp.tile(m_new, (1, T // 128)))
                alpha = jnp.exp2(m_prev - m_new)
                acc_sc[g] = jnp.tile(alpha, (1, 2)) * acc_sc[g] + _dot(p.astype(BF16), vj)
                m_sc[g] = m_new

        def loop_body(j, carry):
            step(j, False)
            return carry

        lax.fori_loop(0, i, loop_body, 0)
        step(i, True)
        for g in range(G):
            hd = slice(g * dh, (g + 1) * dh)
            l = acc_sc[g, :, dh:]
            o = acc_sc[g, :, :dh] / l
            o_ref[:, hd] = o
            y_ref[:, hd] = (o * _silu(g_ref[:, hd].astype(F32))).astype(BF16)
            hi, mid, lo = _split3(-(m_sc[g] + jnp.log2(l)))
            qa_ref[:, hd] = _lane_select([1.0, 1.0, 1.0, hi, mid, lo], (T, AUG))

    blk = lambda off: pl.BlockSpec((T, G * dh), lambda h, i: (i, off // G + h))
    full = lambda off: pl.BlockSpec((S, G * dh), lambda h, i: (0, off // G + h))
    return pl.pallas_call(
        body, grid=(H // G, nq),
        in_specs=[blk(0), full(H), full(0), full(2 * H), blk(3 * H)],
        out_specs=[blk(0), blk(0), blk(0)],
        out_shape=[SDS((S, H * dh), F32), SDS((S, H * dh), BF16), SDS((S, H * AUG), BF16)],
        scratch_shapes=[pltpu.VMEM((G, T, 128), F32), pltpu.VMEM((G, T, 2 * dh), F32)],
        compiler_params=_params("parallel", "arbitrary"), name=name,
    )(p0, p0, kaug, p0, p0)


def _fox_post_bwd(dy, o, p0, *, H, name, tm=512):
    S = dy.shape[0]
    dh = HEAD_DIM
    tm = min(tm, S)

    def body(dy_ref, o_ref, g_ref, do_ref, dg_ref, da_ref):
        dyv = dy_ref[...].astype(F32)
        ov = o_ref[...]
        g = g_ref[...].astype(F32)
        do = (dyv * _silu(g)).astype(BF16)
        do_ref[...] = do
        dg_ref[...] = (dyv * ov * _dsilu(g)).astype(BF16)
        delta = jnp.sum(do.astype(F32) * ov, axis=-1, keepdims=True)
        hi, mid, lo = _split3(-jnp.broadcast_to(delta, (tm, AUG)))
        da_ref[...] = _lane_select([hi, mid, lo], (tm, AUG))

    blk = pl.BlockSpec((tm, dh), lambda h, i: (i, h))
    return pl.pallas_call(
        body, grid=(H, S // tm),
        in_specs=[blk, blk, pl.BlockSpec((tm, dh), lambda h, i: (i, 3 * H + h))],
        out_specs=[blk, blk, blk],
        out_shape=[SDS((S, H * dh), BF16), SDS((S, H * dh), BF16), SDS((S, H * AUG), BF16)],
        compiler_params=_params("parallel", "parallel"), name=name,
    )(dy, o, p0)


def _fox_dq(p0, kaug, qaug, do, doaug, *, H, name):
    S = p0.shape[0]
    T = min(ATT_BLOCK, S)
    nq = S // T
    dh = HEAD_DIM
    scale = dh ** -0.5
    G = ATT_HEADS_PER_STEP
    assert H % G == 0

    def body(q_ref, qa_ref, k_ref, ka_ref, v_ref, do_ref, da_ref, dq_ref, rs_ref, acc_sc):
        i = pl.program_id(1)
        vaug = _lane_select([1.0, 1.0, 1.0], (T, AUG))
        ones = jnp.ones((T, dh), BF16)
        acc_sc[...] = jnp.zeros_like(acc_sc)

        def step(j, masked):
            rows = pl.ds(pl.multiple_of(j * T, T), T)
            for g in range(G):
                hd = slice(g * dh, (g + 1) * dh)
                q = jnp.concatenate([q_ref[:, hd], qa_ref[:, hd]], axis=1)
                do = jnp.concatenate([do_ref[:, hd], da_ref[:, hd]], axis=1)
                k = k_ref[rows, hd]
                p = jnp.exp2(_dot_nt(q, jnp.concatenate([k, ka_ref[rows, hd]], axis=1)))
                if masked:
                    row = lax.broadcasted_iota(jnp.int32, (T, T), 0)
                    col = lax.broadcasted_iota(jnp.int32, (T, T), 1)
                    p = jnp.where(row >= col, p, 0.0)
                ds = p * _dot_nt(do, jnp.concatenate([v_ref[rows, hd], vaug], axis=1))
                acc_sc[g] += _dot(ds.astype(BF16), jnp.concatenate([k, ones], axis=1))

        def loop_body(j, carry):
            step(j, False)
            return carry

        lax.fori_loop(0, i, loop_body, 0)
        step(i, True)
        for g in range(G):
            dq_ref[:, g * dh:(g + 1) * dh] = (acc_sc[g, :, :dh] * scale).astype(BF16)
            rs_ref[g] = acc_sc[g, :, dh:dh + 1]

    blk = lambda off: pl.BlockSpec((T, G * dh), lambda h, i: (i, off // G + h))
    full = lambda off: pl.BlockSpec((S, G * dh), lambda h, i: (0, off // G + h))
    return pl.pallas_call(
        body, grid=(H // G, nq),
        in_specs=[blk(0), blk(0), full(H), full(0), full(2 * H), blk(0), blk(0)],
        out_specs=[blk(0), pl.BlockSpec((G, T, 1), lambda h, i: (h, i, 0))],
        out_shape=[SDS((S, H * dh), BF16), SDS((H, S, 1), F32)],
        scratch_shapes=[pltpu.VMEM((G, T, 2 * dh), F32)],
        compiler_params=_params("parallel", "arbitrary"), name=name,
    )(p0, qaug, p0, kaug, p0, do, doaug)


def _fox_dkv(p0, kaug, qaug, do, doaug, *, H, name):
    S = p0.shape[0]
    T = min(ATT_BLOCK, S)
    nq = S // T
    dh = HEAD_DIM
    G = ATT_HEADS_PER_STEP
    assert H % G == 0

    def body(q_ref, qa_ref, k_ref, ka_ref, v_ref, do_ref, da_ref, dk_ref, dv_ref, dc_ref, dk_sc, dv_sc):
        j = pl.program_id(1)
        vaug = _lane_select([1.0, 1.0, 1.0], (T, AUG))
        ones = jnp.ones((T, dh), BF16)
        dk_sc[...] = jnp.zeros_like(dk_sc)
        dv_sc[...] = jnp.zeros_like(dv_sc)

        def step(i, masked):
            rows = pl.ds(pl.multiple_of(i * T, T), T)
            for g in range(G):
                hd = slice(g * dh, (g + 1) * dh)
                k = jnp.concatenate([k_ref[:, hd], ka_ref[:, hd]], axis=1)
                v = jnp.concatenate([v_ref[:, hd], vaug], axis=1)
                qi = q_ref[rows, hd]
                doi = do_ref[rows, hd]
                pt = jnp.exp2(_dot_nt(k, jnp.concatenate([qi, qa_ref[rows, hd]], axis=1)))
                if masked:
                    row = lax.broadcasted_iota(jnp.int32, (T, T), 0)
                    col = lax.broadcasted_iota(jnp.int32, (T, T), 1)
                    pt = jnp.where(col >= row, pt, 0.0)
                dv_sc[g] += _dot(pt.astype(BF16), doi)
                dst = pt * _dot_nt(v, jnp.concatenate([doi, da_ref[rows, hd]], axis=1))
                dk_sc[g] += _dot(dst.astype(BF16), jnp.concatenate([qi, ones], axis=1))

        step(j, True)

        def loop_body(i, carry):
            step(i, False)
            return carry

        lax.fori_loop(j + 1, nq, loop_body, 0)
        for g in range(G):
            hd = slice(g * dh, (g + 1) * dh)
            dk_ref[:, hd] = (dk_sc[g, :, :dh] * LN2).astype(BF16)
            dv_ref[:, hd] = dv_sc[g].astype(BF16)
            dc_ref[g] = -dk_sc[g, :, dh:dh + 1]

    blk = lambda off: pl.BlockSpec((T, G * dh), lambda h, j: (j, off // G + h))
    full = lambda off: pl.BlockSpec((S, G * dh), lambda h, j: (0, off // G + h))
    return pl.pallas_call(
        body, grid=(H // G, nq),
        in_specs=[full(0), full(0), blk(H), blk(0), blk(2 * H), full(0), full(0)],
        out_specs=[blk(0), blk(0), pl.BlockSpec((G, T, 1), lambda h, j: (h, j, 0))],
        out_shape=[SDS((S, H * dh), BF16), SDS((S, H * dh), BF16), SDS((H, S, 1), F32)],
        scratch_shapes=[pltpu.VMEM((G, T, 2 * dh), F32), pltpu.VMEM((G, T, dh), F32)],
        compiler_params=_params("parallel", "arbitrary"), name=name,
    )(p0, qaug, p0, kaug, p0, do, doaug)


def _hgrn_levels(C, leaf):
    levels = []
    h = C // 2
    while h >= leaf:
        levels.append(h)
        h //= 2
    return levels


def _hgrn_sum_matrix(C, leaf):
    t = np.arange(C)[:, None]
    u = np.arange(C)[None, :]
    mats = [(u <= t), (u > t)]
    for h in _hgrn_levels(C, leaf):
        start = (t // (2 * h)) * (2 * h)
        mid = start + h - 1
        second = t > mid
        m = np.where(second, (u > mid) & (u <= t), (u > t) & (u <= mid))
        mats.append(m)
    lstart = (t // leaf) * leaf
    mats.append((u >= lstart) & (u <= t))
    return np.concatenate([m.astype(np.float32) for m in mats], axis=0)


def _hgrn_chunk_terms(qr, fz, lb, msum, C, leaf):
    levels = _hgrn_levels(C, leaf)
    sq = _silu(qr)
    t = jnp.exp(-jnp.abs(fz))
    r = 1.0 / (1.0 + t)
    pos = fz >= 0.0
    sp = jnp.where(pos, r, t * r)
    sn = jnp.where(pos, t * r, r)
    f = lb + (1.0 - lb) * sp
    lf = jnp.log(f)
    k = (1.0 - lb) * sn
    hi, lo = _split2(lf)
    dsum = _dot(msum, hi) + _dot(msum, lo)
    b = dsum[0:C]
    kdec = dsum[C:2 * C]
    rowi = lax.broadcasted_iota(jnp.int32, (C, 1), 0)
    lev = []
    for n, h in enumerate(levels):
        e = jnp.exp(dsum[(2 + n) * C:(3 + n) * C])
        second = (rowi % (2 * h)) >= h
        qm = jnp.where(second, sq * e, 0.0).astype(BF16)
        km = jnp.where(second, 0.0, k * e).astype(BF16)
        lev.append((h, e, second, qm, km))
    dleaf = dsum[(2 + len(levels)) * C:(3 + len(levels)) * C]
    eq = jnp.exp(dleaf)
    ek = jnp.exp(jnp.minimum(-dleaf, EXP_CLAMP))
    return dict(sq=sq, sp=sp, sn=sn, f=f, k=k, b=b, kdec=kdec, lev=lev, eq=eq, ek=ek,
                ql=(sq * eq).astype(BF16), kl=(k * ek).astype(BF16),
                qs=(sq * jnp.exp(b)).astype(BF16), ke=(k * jnp.exp(kdec)).astype(BF16),
                e_c=jnp.exp(b[C - 1:C, :]))


def _hgrn_masks(C, leaf, transposed):
    a = lax.broadcasted_iota(jnp.int32, (C, C), 0)
    bb = lax.broadcasted_iota(jnp.int32, (C, C), 1)
    t, s = (bb, a) if transposed else (a, bb)
    lev = [None if 2 * h == C else (t // (2 * h)) == (s // (2 * h)) for h in _hgrn_levels(C, leaf)]
    if leaf == C:
        leafm = s <= t
    else:
        leafm = ((t // leaf) == (s // leaf)) & (s <= t)
    return lev, leafm


def _hgrn_fwd(p1, f1, lb, onorm, *, H, name, tb=512):
    S = p1.shape[0]
    dk = HEAD_DIM
    C = min(HGRN_CHUNK, S)
    leaf = min(HGRN_LEAF, C)
    tb = min(tb, S)
    nc = tb // C
    G = HGRN_HEADS_PER_STEP
    assert H % G == 0
    msum = jnp.asarray(_hgrn_sum_matrix(C, leaf), BF16)

    def body(q_ref, f_ref, v_ref, g_ref, lb_ref, on_ref, ms_ref, o_ref, y_ref, st_ref, st_sc):
        @pl.when(pl.program_id(1) == 0)
        def _():
            st_sc[...] = jnp.zeros_like(st_sc)

        msv = ms_ref[...]
        lmask, leafm = _hgrn_masks(C, leaf, False)

        def chunk(n, carry):
            rows = pl.ds(pl.multiple_of(n * C, C), C)
            for g in range(G):
                hd = slice(g * dk, (g + 1) * dk)
                tm = _hgrn_chunk_terms(q_ref[rows, hd].astype(F32), f_ref[rows, hd], lb_ref[:, hd], msv, C, leaf)
                v = v_ref[rows, hd]
                st = st_sc[g]
                st_ref[g, n] = st
                a = jnp.where(leafm, _dot_nt(tm["ql"], tm["kl"]), 0.0)
                for (h, e, second, qm, km), m in zip(tm["lev"], lmask):
                    al = _dot_nt(qm, km)
                    a = a + (al if m is None else jnp.where(m, al, 0.0))
                o = _dot_nt(tm["qs"], st.astype(BF16)) + _dot(a.astype(BF16), v)
                st_sc[g] = st * tm["e_c"] + _dot(v.T, tm["ke"])
                o_ref[rows, hd] = o
                rn = lax.rsqrt(jnp.mean(o * o, axis=-1, keepdims=True) + EPS)
                y = ((o * rn) * on_ref[:, hd]) * _silu(g_ref[rows, hd].astype(F32))
                y_ref[rows, hd] = y.astype(BF16)
            return carry

        lax.fori_loop(0, nc, chunk, 0)

    blk = lambda off: pl.BlockSpec((tb, G * dk), lambda h, i: (i, off // G + h))
    vec = pl.BlockSpec((1, G * dk), lambda h, i: (0, h))
    return pl.pallas_call(
        body, grid=(H // G, S // tb),
        in_specs=[blk(0), blk(0), blk(H), blk(2 * H), vec, vec,
                  pl.BlockSpec(msum.shape, lambda h, i: (0, 0))],
        out_specs=[blk(0), blk(0), pl.BlockSpec((G, nc, dk, dk), lambda h, i: (h, i, 0, 0))],
        out_shape=[SDS((S, H * dk), F32), SDS((S, H * dk), BF16), SDS((H, S // C, dk, dk), F32)],
        scratch_shapes=[pltpu.VMEM((G, dk, dk), F32)],
        compiler_params=_params("parallel", "arbitrary"), name=name,
    )(p1, f1, p1, p1, lb, onorm, msum)


def _hgrn_post_bwd(dy, o, p1, onorm, *, H, name, tm=512):
    S = dy.shape[0]
    dk = HEAD_DIM
    tm = min(tm, S)

    def body(dy_ref, o_ref, g_ref, on_ref, do_ref, dg_ref, don_ref):
        @pl.when(pl.program_id(1) == 0)
        def _():
            don_ref[...] = jnp.zeros_like(don_ref)

        dyv = dy_ref[...].astype(F32)
        ov = o_ref[...]
        g = g_ref[...].astype(F32)
        onv = on_ref[...]
        rn = lax.rsqrt(jnp.mean(ov * ov, axis=-1, keepdims=True) + EPS)
        oh = ov * rn
        dn = dyv * _silu(g)
        dg_ref[...] = (dyv * (oh * onv) * _dsilu(g)).astype(BF16)
        don_ref[...] += jnp.sum(dn * oh, axis=0, keepdims=True)
        doh = dn * onv
        do_ref[...] = (rn * (doh - oh * jnp.mean(doh * oh, axis=-1, keepdims=True))).astype(BF16)

    blk = pl.BlockSpec((tm, dk), lambda h, i: (i, h))
    vec = pl.BlockSpec((1, dk), lambda h, i: (0, h))
    return pl.pallas_call(
        body, grid=(H, S // tm),
        in_specs=[blk, blk, pl.BlockSpec((tm, dk), lambda h, i: (i, 2 * H + h)), vec],
        out_specs=[blk, blk, vec],
        out_shape=[SDS((S, H * dk), BF16), SDS((S, H * dk), BF16), SDS((1, H * dk), F32)],
        compiler_params=_params("parallel", "arbitrary"), name=name,
    )(dy, o, p1, onorm)


def _hgrn_bwd(p1, f1, lb, do, states, *, H, name, tb=512):
    S = p1.shape[0]
    dk = HEAD_DIM
    C = min(HGRN_CHUNK, S)
    leaf = min(HGRN_LEAF, C)
    tb = min(tb, S)
    nc = tb // C
    nb = S // tb
    G = HGRN_HEADS_PER_STEP
    assert H % G == 0
    msum = jnp.asarray(_hgrn_sum_matrix(C, leaf), BF16)
    rtri = jnp.asarray(np.triu(np.ones((C, C), np.float32)), BF16)

    def body(q_ref, f_ref, v_ref, do_ref, st_ref, lb_ref, ms_ref, rt_ref,
             dq_ref, df_ref, dv_ref, dlb_ref, g_sc):
        @pl.when(pl.program_id(1) == 0)
        def _():
            g_sc[...] = jnp.zeros_like(g_sc)
            dlb_ref[...] = jnp.zeros_like(dlb_ref)

        msv = ms_ref[...]
        rtv = rt_ref[...]
        lmask, leafm = _hgrn_masks(C, leaf, False)
        lmask_t, leafm_t = _hgrn_masks(C, leaf, True)
        f32 = lambda z: z.astype(F32)

        def head_chunk(g, n):
            hd = slice(g * dk, (g + 1) * dk)
            rows = pl.ds(pl.multiple_of(n * C, C), C)
            lbv = lb_ref[:, hd]
            qr = q_ref[rows, hd].astype(F32)
            tm = _hgrn_chunk_terms(qr, f_ref[rows, hd], lbv, msv, C, leaf)
            v = v_ref[rows, hd]
            dov = do_ref[rows, hd]
            st0 = st_ref[g, n]
            gt = g_sc[g]
            gtb = gt.astype(BF16)
            da = _dot_nt(dov, v)
            da_t = _dot_nt(v, dov)

            dal = jnp.where(leafm, da, 0.0).astype(BF16)
            dal_t = jnp.where(leafm_t, da_t, 0.0).astype(BF16)
            dql = _dot(dal, tm["kl"])
            dkl = _dot(dal_t, tm["ql"])
            dsq = dql * tm["eq"]
            dkk = dkl * tm["ek"]
            xq = f32(tm["ql"]) * dql
            xk = f32(tm["kl"]) * dkl
            a_t = jnp.where(leafm_t, _dot_nt(tm["kl"], tm["ql"]), 0.0)
            for (h, e, second, qm, km), m, m_t in zip(tm["lev"], lmask, lmask_t):
                dl = (da if m is None else jnp.where(m, da, 0.0)).astype(BF16)
                dl_t = (da_t if m_t is None else jnp.where(m_t, da_t, 0.0)).astype(BF16)
                dqm = _dot(dl, km)
                dkm = _dot(dl_t, qm)
                dsq = dsq + jnp.where(second, dqm * e, 0.0)
                dkk = dkk + jnp.where(second, 0.0, dkm * e)
                xq = xq + f32(qm) * dqm
                xk = xk + f32(km) * dkm
                al_t = _dot_nt(km, qm)
                a_t = a_t + (al_t if m_t is None else jnp.where(m_t, al_t, 0.0))
            dqs = _dot(dov, st0.astype(BF16))
            dke = _dot(v, gtb)
            dsq = dsq + dqs * jnp.exp(tm["b"])
            dkk = dkk + dke * jnp.exp(tm["kdec"])
            xq = xq + f32(tm["qs"]) * dqs
            xk = xk + f32(tm["ke"]) * dke
            dvv = _dot(a_t.astype(BF16), dov) + _dot_nt(tm["ke"], gtb)
            r_end = jnp.sum(f32(gtb) * _dot(v.T, tm["ke"]) + gt * (st0 * tm["e_c"]), axis=0, keepdims=True)
            g_sc[g] = gt * tm["e_c"] + _dot(dov.T, tm["qs"])
            xh, xm, xl = _split3(xq - xk)
            dlf = (_dot(rtv, xh) + _dot(rtv, xm)) + _dot(rtv, xl) + r_end
            dlf_f = dlf / tm["f"]
            dsp = (1.0 - lbv) * (dlf_f - dkk)
            df_ref[rows, hd] = (dsp * (tm["sp"] * tm["sn"])).astype(BF16)
            dq_ref[rows, hd] = (dsq * _dsilu(qr)).astype(BF16)
            dv_ref[rows, hd] = dvv.astype(BF16)
            dlb_ref[:, hd] += jnp.sum(dlf_f * tm["sn"] - dkk * tm["sn"], axis=0, keepdims=True)

        def chunk(nn, carry):
            for g in range(G):
                head_chunk(g, nc - 1 - nn)
            return carry

        lax.fori_loop(0, nc, chunk, 0)

    blk = lambda off: pl.BlockSpec((tb, G * dk), lambda h, i: (nb - 1 - i, off // G + h))
    vec = pl.BlockSpec((1, G * dk), lambda h, i: (0, h))
    return pl.pallas_call(
        body, grid=(H // G, nb),
        in_specs=[blk(0), blk(0), blk(H), blk(0),
                  pl.BlockSpec((G, nc, dk, dk), lambda h, i: (h, nb - 1 - i, 0, 0)), vec,
                  pl.BlockSpec(msum.shape, lambda h, i: (0, 0)), pl.BlockSpec((C, C), lambda h, i: (0, 0))],
        out_specs=[blk(0), blk(0), blk(0), vec],
        out_shape=[SDS((S, H * dk), BF16)] * 3 + [SDS((1, H * dk), F32)],
        scratch_shapes=[pltpu.VMEM((G, dk, dk), F32)],
        compiler_params=_params("parallel", "arbitrary"), name=name,
    )(p1, f1, p1, do, states, lb, msum, rtri)


def _lb_fwd(logits, *, name):
    W = logits.shape[1]

    def body(l_ref, lb_ref):
        l = l_ref[...]
        m = jnp.max(l, axis=0, keepdims=True)
        e = jnp.exp(l - m)
        p = e / jnp.sum(e, axis=0, keepdims=True)
        lb_ref[...] = (p[0:1] + p[1:2]) - p[0:1]

    return pl.pallas_call(body, out_shape=SDS((1, W), F32), name=name)(logits)


STAT_ROWS = 8


def _stats_reduce(stats_all, logits, *, name):
    W = logits.shape[1]

    def body(s_ref, l_ref, g_ref):
        tot = s_ref[0]
        for d in range(1, N_DEV):
            tot = tot + s_ref[d]
        l = l_ref[...]
        m = jnp.max(l, axis=0, keepdims=True)
        e = jnp.exp(l - m)
        p = e / jnp.sum(e, axis=0, keepdims=True)
        dlb = tot[2:3]
        dl0 = -(p[0:1] * p[1:2]) * dlb
        dl1 = (p[1:2] * (1.0 - p[1:2])) * dlb
        g_ref[0:2] = tot[0:2]
        g_ref[2:3] = dl0
        g_ref[3:4] = dl1
        g_ref[4:7] = tot[3:6]
        g_ref[7:8] = jnp.zeros((1, W), F32)

    return pl.pallas_call(body, out_shape=SDS((STAT_ROWS, W), F32), name=name)(stats_all, logits)


def _adamw(w, m, v, g_parts, *, name, tr=256):
    R, C = w.shape
    n = g_parts.shape[0]
    tr = min(tr, R)
    assert R % tr == 0
    c1 = 1.0 / (1.0 - ADAM_B1 ** ADAM_STEP)
    c2 = 1.0 / (1.0 - ADAM_B2 ** ADAM_STEP)

    def body(w_ref, m_ref, v_ref, g_ref, go_ref, d_ref, mo_ref, vo_ref):
        g = g_ref[0].astype(F32)
        for k in range(1, n):
            g = g + g_ref[k].astype(F32)
        mn = ADAM_B1 * m_ref[...] + (1.0 - ADAM_B1) * g
        vn = ADAM_B2 * v_ref[...] + (1.0 - ADAM_B2) * (g * g)
        d_ref[...] = -ADAM_LR * ((mn * c1) / (jnp.sqrt(vn * c2) + ADAM_EPS) + ADAM_WD * w_ref[...])
        go_ref[...] = g
        mo_ref[...] = mn
        vo_ref[...] = vn

    blk = pl.BlockSpec((tr, C), lambda i: (i, 0))
    return pl.pallas_call(
        body, grid=(R // tr,), in_specs=[blk, blk, blk, pl.BlockSpec((n, tr, C), lambda i: (0, i, 0))],
        out_specs=[blk] * 4, out_shape=[SDS((R, C), F32)] * 4,
        compiler_params=_params("parallel"), name=name,
    )(w, m, v, g_parts)


ANY = pl.BlockSpec(memory_space=pl.ANY)


def _all_gather(shards, out_shapes, views, *, name):
    n = len(shards)

    def body(*refs):
        ins, outs = refs[:n], refs[n:2 * n]
        send_sems, recv_sems, local_sems = refs[2 * n:]
        x, y, c = lax.axis_index("x"), lax.axis_index("y"), lax.axis_index("c")
        me, sibling = (x, y, c), (x, y, 1 - c)
        chips = [(1 - x, y), (x, 1 - y), (1 - x, 1 - y)]

        def dev(p):
            return 4 * p[0] + 2 * p[1] + p[2]

        def copy(a, k, block, to, src=None):
            dst = views[a](outs[a], dev(block))
            return pltpu.make_async_remote_copy(
                src_ref=dst if src is None else src, dst_ref=dst,
                send_sem=send_sems.at[a, k], recv_sem=recv_sems.at[a, k],
                device_id=to, device_id_type=MESH)

        mine, first, passed = [], [], []
        for a in range(n):
            cp = pltpu.make_async_copy(ins[a], views[a](outs[a], dev(me)), local_sems.at[a])
            cp.start()
            mine.append(cp)
            first.append(copy(a, 0, me, sibling, src=ins[a]))
            first += [copy(a, 1 + j, me, (*chip, c), src=ins[a]) for j, chip in enumerate(chips)]
        for cp in first:
            cp.start()
        for j, chip in enumerate(chips):
            for a in range(n):
                copy(a, 1 + j, (*chip, c), me).wait_recv()
                cp = copy(a, 4 + j, (*chip, c), sibling)
                cp.start()
                passed.append(cp)
        for a in range(n):
            copy(a, 0, sibling, me).wait_recv()
            for j, chip in enumerate(chips):
                copy(a, 4 + j, (*chip, 1 - c), me).wait_recv()
        for cp in first + passed:
            cp.wait_send()
        for cp in mine:
            cp.wait()

    return pl.pallas_call(
        body, in_specs=[ANY] * n, out_specs=[ANY] * n, out_shape=list(out_shapes),
        scratch_shapes=[pltpu.SemaphoreType.DMA((n, 7)), pltpu.SemaphoreType.DMA((n, 7)),
                        pltpu.SemaphoreType.DMA((n,))],
        name=name,
    )(*shards)


def _exchange_partials(parts, shard_shapes, views, *, name):
    n = len(parts)

    def body(*refs):
        ins, outs = refs[:n], refs[n:2 * n]
        send_sems, recv_sems, local_sems = refs[2 * n:]
        x, y, c = lax.axis_index("x"), lax.axis_index("y"), lax.axis_index("c")
        copies = []
        for a in range(n):
            cp = pltpu.make_async_copy(views[a](ins[a], 4 * x + 2 * y + c), outs[a].at[0], local_sems.at[a])
            cp.start()
            copies.append(cp)
        remote = []
        for m in range(1, N_DEV):
            px = 1 - x if m & 4 else x
            py = 1 - y if m & 2 else y
            pc = 1 - c if m & 1 else c
            for a in range(n):
                cp = pltpu.make_async_remote_copy(
                    src_ref=views[a](ins[a], 4 * px + 2 * py + pc), dst_ref=outs[a].at[m],
                    send_sem=send_sems.at[a, m - 1], recv_sem=recv_sems.at[a, m - 1],
                    device_id=(px, py, pc), device_id_type=MESH)
                cp.start()
                remote.append(cp)
        for cp in remote:
            cp.wait_send()
            cp.wait_recv()
        for cp in copies:
            cp.wait()

    return pl.pallas_call(
        body, in_specs=[ANY] * n, out_specs=[ANY] * n,
        out_shape=[SDS((N_DEV,) + tuple(s), p.dtype) for s, p in zip(shard_shapes, parts)],
        scratch_shapes=[pltpu.SemaphoreType.DMA((n, 7)), pltpu.SemaphoreType.DMA((n, 7)),
                        pltpu.SemaphoreType.DMA((n,))],
        name=name,
    )(*parts)


def kernel(x, norm_gains, fox_w_in, fox_b_f, hgrn_w_in, hgrn_lb_logits, hgrn_onorm, w_out, final_gain, loss_target, m_norm_gains, m_fox_w_in, m_fox_b_f, m_hgrn_w_in, m_hgrn_lb_logits, m_hgrn_onorm, m_w_out, m_final_gain, v_norm_gains, v_fox_w_in, v_fox_b_f, v_hgrn_w_in, v_hgrn_lb_logits, v_hgrn_onorm, v_w_out, v_final_gain):
    _, S, D = x.shape
    H = FOX_HEADS
    W = H * HEAD_DIM
    assert HGRN_HEADS == H and w_out.shape[2] == D
    cf = fox_w_in.shape[2]
    ch = hgrn_w_in.shape[2]
    ro = w_out.shape[1]
    co = hgrn_onorm.shape[1]
    assert N_DEV * cf == 4 * W + H and N_DEV * ch == 4 * W and N_DEV * ro == W and N_DEV * co == W
    x2 = x.reshape(S, D)
    tgt = loss_target.reshape(S, D)

    wf_g, wh, wo, onorm = _all_gather(
        [fox_w_in[0].astype(BF16), hgrn_w_in[0].astype(BF16), w_out.astype(BF16), hgrn_onorm],
        [SDS((N_DEV, D, cf), BF16), SDS((D, 4 * W), BF16), SDS((2, W, D), BF16), SDS((1, W), F32)],
        [lambda r, p: r.at[p],
         lambda r, p: r.at[:, pl.ds(pl.multiple_of(p * ch, ch), ch)],
         lambda r, p: r.at[:, pl.ds(pl.multiple_of(p * ro, ro), ro), :],
         lambda r, p: r.at[:, pl.ds(pl.multiple_of(p * co, co), co)]],
        name="gather_weights")
    wf = jnp.transpose(wf_g, (1, 0, 2)).reshape(D, N_DEV * cf)
    wf_main = jnp.concatenate([wf[:, :3 * W], wf[:, 3 * W + H:]], axis=1)
    wfl_t = wf[:, 3 * W:3 * W + H].T
    wf_main_t = wf_main.T
    wh_qig = jnp.concatenate([wh[:, :W], wh[:, 2 * W:]], axis=1)
    wh_f = wh[:, W:2 * W]
    wh_t = wh.T
    wo0, wo1 = wo[0], wo[1]
    wo0_t, wo1_t = wo0.T, wo1.T

    h0 = _rms_fwd(x2, norm_gains[0:1], name="rms0_fwd")
    p0 = _mm_nn([h0], wf_main, BF16, scale_cols=(W, LOG2E * HEAD_DIM ** -0.5), name="fox_in_proj")
    fl_t = _mm_nt_rows(wfl_t, h0, name="fox_forget_proj")
    b_col = fox_b_f.reshape(H, 1)
    kaug = _fox_key_aug(*_fox_gate_fwd(fl_t, b_col, name="fox_gate_fwd"))
    o0, y0, qaug = _fox_fwd(p0, kaug, H=H, name="fox_attn_fwd")
    x1 = _mm_nn([y0], wo0, F32, residual=x2, name="fox_out_proj")

    lb = _lb_fwd(hgrn_lb_logits, name="hgrn_lower_bound")
    h1 = _rms_fwd(x1, norm_gains[1:2], name="rms1_fwd")
    p1 = _mm_nn([h1], wh_qig, BF16, name="hgrn_in_proj")
    f1 = _mm_nn([h1], wh_f, F32, name="hgrn_forget_proj")
    o1, y1, states = _hgrn_fwd(p1, f1, lb, onorm, H=H, name="hgrn_fwd")
    xo = _mm_nn([y1], wo1, F32, residual=x1, name="hgrn_out_proj")

    dx2, dx2b, loss_part, dgf = _loss_head(xo, final_gain.reshape(1, D), tgt, name="loss_head")
    loss = lax.psum(jnp.sum(loss_part), ("x", "y", "c"))

    dy1 = _mm_nn([dx2b], wo1_t, BF16, name="hgrn_out_proj_dx")
    dwo1 = _mm_tn(y1, [dx2b], BF16, name="hgrn_out_proj_dw")
    do1, dg1, donorm = _hgrn_post_bwd(dy1, o1, p1, onorm, H=H, name="hgrn_post_bwd")
    dq1, df1, di1, dlb = _hgrn_bwd(p1, f1, lb, do1, states, H=H, name="hgrn_bwd")
    segs1 = [dq1, df1, di1, dg1]
    dh1 = _mm_nn(segs1, wh_t, BF16, name="hgrn_in_proj_dx")
    dwh = _mm_tn(h1, segs1, BF16, name="hgrn_in_proj_dw")
    dx1, dx1b, dng1 = _rms_bwd(x1, norm_gains[1:2], dh1, dx2, name="rms1_bwd")

    dy0 = _mm_nn([dx1b], wo0_t, BF16, name="fox_out_proj_dx")
    dwo0 = _mm_tn(y0, [dx1b], BF16, name="fox_out_proj_dw")
    do0, dg0, doaug = _fox_post_bwd(dy0, o0, p0, H=H, name="fox_post_bwd")
    dq0, dc_row = _fox_dq(p0, kaug, qaug, do0, doaug, H=H, name="fox_attn_dq")
    dk0, dv0, dc_key = _fox_dkv(p0, kaug, qaug, do0, doaug, H=H, name="fox_attn_dkv")
    dfl_t, dbf = _fox_gate_bwd(dc_row.reshape(H, S), dc_key.reshape(H, S), fl_t, b_col, name="fox_gate_bwd")
    dfl_tb = dfl_t.astype(BF16)
    dwfl_t = _mm_nn([dfl_tb], h0, BF16, name="fox_forget_proj_dw")
    dh0_f = _mm_nn([dfl_tb.T], wfl_t, BF16, name="fox_forget_proj_dx")
    segs0 = [dq0, dk0, dv0, dg0]
    dh0 = _mm_nn(segs0, wf_main_t, BF16, residual=dh0_f, name="fox_in_proj_dx")
    dwf_main = _mm_tn(h0, segs0, BF16, name="fox_in_proj_dw")
    grad_x, _, dng0 = _rms_bwd(x2, norm_gains[0:1], dh0, dx1, name="rms0_bwd")

    dwf = jnp.concatenate([dwf_main[:, :3 * W], dwfl_t.T, dwf_main[:, 3 * W:]], axis=1)
    dwf_blocks = jnp.transpose(dwf.reshape(D, N_DEV, cf), (1, 0, 2))
    dwo = jnp.stack([dwo0, dwo1])
    rf, rh, ro_parts = _exchange_partials(
        [dwf_blocks, dwh, dwo], [(D, cf), (D, ch), (2, ro, D)],
        [lambda r, p: r.at[p],
         lambda r, p: r.at[:, pl.ds(pl.multiple_of(p * ch, ch), ch)],
         lambda r, p: r.at[:, pl.ds(pl.multiple_of(p * ro, ro), ro), :]],
        name="exchange_weight_grads")

    pad = lambda a: jnp.pad(a, ((0, 0), (0, W - a.shape[1])))
    stats = jnp.concatenate([dng0, dng1, dlb, dgf, pad(dbf.reshape(1, H)), donorm,
                             jnp.zeros((2, W), F32)], axis=0)
    assert D == W
    (stats_all,) = _all_gather([stats], [SDS((N_DEV, STAT_ROWS, W), F32)], [lambda r, p: r.at[p]],
                               name="gather_small_grads")
    g_small = _stats_reduce(stats_all, hgrn_lb_logits, name="reduce_small_grads")
    me = 4 * lax.axis_index("x") + 2 * lax.axis_index("y") + lax.axis_index("c")
    g_onorm = lax.dynamic_slice_in_dim(g_small[6:7], me * co, co, axis=1)

    def upd(w, m, v, parts, name):
        shp = w.shape
        r2 = (-1, shp[-1])
        g, d, mn, vn = _adamw(w.reshape(r2), m.reshape(r2), v.reshape(r2),
                              parts.reshape((parts.shape[0],) + w.reshape(r2).shape), name=name)
        return g.reshape(shp), d.reshape(shp), mn.reshape(shp), vn.reshape(shp)

    res = {
        "norm_gains": upd(norm_gains, m_norm_gains, v_norm_gains, g_small[None, 0:2], "adamw_norm_gains"),
        "fox_w_in": upd(fox_w_in, m_fox_w_in, v_fox_w_in, rf, "adamw_fox_w_in"),
        "fox_b_f": upd(fox_b_f, m_fox_b_f, v_fox_b_f, g_small[None, 5:6, :H], "adamw_fox_b_f"),
        "hgrn_w_in": upd(hgrn_w_in, m_hgrn_w_in, v_hgrn_w_in, rh, "adamw_hgrn_w_in"),
        "hgrn_lb_logits": upd(hgrn_lb_logits, m_hgrn_lb_logits, v_hgrn_lb_logits, g_small[None, 2:4],
                              "adamw_hgrn_lb_logits"),
        "hgrn_onorm": upd(hgrn_onorm, m_hgrn_onorm, v_hgrn_onorm, g_onorm[None], "adamw_hgrn_onorm"),
        "w_out": upd(w_out, m_w_out, v_w_out, ro_parts, "adamw_w_out"),
        "final_gain": upd(final_gain.reshape(1, D), m_final_gain.reshape(1, D), v_final_gain.reshape(1, D),
                          g_small[None, 4:5], "adamw_final_gain"),
    }
    order = ["norm_gains", "fox_w_in", "fox_b_f", "hgrn_w_in", "hgrn_lb_logits", "hgrn_onorm", "w_out", "final_gain"]
    fix = lambda n, a: a.reshape(D) if n == "final_gain" else a
    outs = [loss, grad_x.reshape(1, S, D)]
    for k in range(4):
        outs += [fix(n, res[n][k]) for n in order]
    return tuple(outs)
```

```python
import functools

import numpy as np
import jax
import jax.numpy as jnp
from jax import lax
from jax.experimental import pallas as pl
from jax.experimental.pallas import tpu as pltpu

F32 = jnp.float32
BF16 = jnp.bfloat16
SDS = jax.ShapeDtypeStruct
MESH = pl.DeviceIdType.MESH

EPS = 1e-6
ADAM_LR, ADAM_B1, ADAM_B2, ADAM_EPS, ADAM_WD, ADAM_STEP = 0.001, 0.9, 0.999, 1e-08, 0.01, 10

N_DEV = 8
FOX_HEADS = 16
HGRN_HEADS = 16
HEAD_DIM = 128
HGRN_CHUNK = 64
HGRN_LEAF = 16
HGRN_HEADS_PER_STEP = 4
EXP_CLAMP = 85.0
ATT_BLOCK = 512
ATT_HEADS_PER_STEP = 2
POST_HEADS_PER_STEP = 4
NEG = -1e30
LOG2E = 1.4426950408889634
LN2 = 0.6931471805599453

VMEM_LIMIT_V7X = 56 * 1024 * 1024


def _params(*sem):
    return pltpu.CompilerParams(dimension_semantics=sem, vmem_limit_bytes=VMEM_LIMIT_V7X)


def _silu(x):
    return x * jax.nn.sigmoid(x)


def _dsilu(x):
    s = jax.nn.sigmoid(x)
    return s * (1.0 + x * (1.0 - s))


def _dot(a, b):
    return jnp.dot(a, b, preferred_element_type=F32)


def _dot_nt(a, b):
    return lax.dot_general(a, b, (((1,), (1,)), ((), ())), preferred_element_type=F32)


def _dot_tn(a, b):
    return lax.dot_general(a, b, (((0,), (0,)), ((), ())), preferred_element_type=F32)


def _mm_nn(a_list, b, out_dtype, *, name, residual=None, scale_cols=None, tm=512, tn=1024, tk=2048):
    ns = len(a_list)
    M, Ks = a_list[0].shape
    K, N = b.shape
    assert K == ns * Ks and all(a.shape == (M, Ks) for a in a_list)
    tm, tn, tk = min(tm, M), min(tn, N), min(tk, Ks)
    assert M % tm == 0 and N % tn == 0 and Ks % tk == 0
    assert scale_cols is None or scale_cols[0] % tn == 0
    nks = Ks // tk
    nk = ns * nks
    has_res = residual is not None

    def body(*refs):
        a_refs, b_ref = refs[:ns], refs[ns]
        res_ref = refs[ns + 1] if has_res else None
        o_ref = refs[ns + 1 + has_res]

        def finish(r):
            if has_res:
                r = r + res_ref[...].astype(F32)
            if scale_cols is not None:
                r = r * jnp.where(pl.program_id(1) < scale_cols[0] // tn, scale_cols[1], 1.0)
            o_ref[...] = r.astype(out_dtype)

        if nk == 1:
            finish(_dot(a_refs[0][...], b_ref[...]))
            return
        acc_ref = refs[ns + 2 + has_res]
        k = pl.program_id(2)

        @pl.when(k == 0)
        def _():
            acc_ref[...] = jnp.zeros_like(acc_ref)

        for s in range(ns):
            def step(s=s):
                acc_ref[...] += _dot(a_refs[s][...], b_ref[...])

            if ns == 1:
                step()
            else:
                pl.when(k // nks == s)(step)

        @pl.when(k == nk - 1)
        def _():
            finish(acc_ref[...])

    def a_map(i, j, k, s):
        return (i, jnp.clip(k - s * nks, 0, nks - 1))

    in_specs = [pl.BlockSpec((tm, tk), functools.partial(a_map, s=s)) for s in range(ns)]
    in_specs.append(pl.BlockSpec((tk, tn), lambda i, j, k: (k, j)))
    args = list(a_list) + [b]
    if has_res:
        in_specs.append(pl.BlockSpec((tm, tn), lambda i, j, k: (i, j)))
        args.append(residual)
    return pl.pallas_call(
        body, grid=(M // tm, N // tn, nk), in_specs=in_specs,
        out_specs=pl.BlockSpec((tm, tn), lambda i, j, k: (i, j)),
        out_shape=SDS((M, N), out_dtype),
        scratch_shapes=[] if nk == 1 else [pltpu.VMEM((tm, tn), F32)],
        compiler_params=_params("parallel", "parallel", "arbitrary"), name=name,
    )(*args)


def _mm_tn(a, b_list, out_dtype, *, name, tm=1024, tn=2048, tk=512):
    ns = len(b_list)
    S, M = a.shape
    Ns = b_list[0].shape[1]
    assert all(b.shape == (S, Ns) for b in b_list)
    tm, tn, tk = min(tm, M), min(tn, Ns), min(tk, S)
    assert M % tm == 0 and Ns % tn == 0 and S % tk == 0
    njs = Ns // tn
    nk = S // tk

    def body(*refs):
        a_ref, b_refs, o_ref, acc_ref = refs[0], refs[1:1 + ns], refs[1 + ns], refs[2 + ns]
        j, k = pl.program_id(1), pl.program_id(2)

        @pl.when(k == 0)
        def _():
            acc_ref[...] = jnp.zeros_like(acc_ref)

        for s in range(ns):
            def step(s=s):
                acc_ref[...] += _dot_tn(a_ref[...], b_refs[s][...])

            if ns == 1:
                step()
            else:
                pl.when(j // njs == s)(step)

        @pl.when(k == nk - 1)
        def _():
            o_ref[...] = acc_ref[...].astype(out_dtype)

    def b_map(i, j, k, s):
        return (k, jnp.clip(j - s * njs, 0, njs - 1))

    in_specs = [pl.BlockSpec((tk, tm), lambda i, j, k: (k, i))]
    in_specs += [pl.BlockSpec((tk, tn), functools.partial(b_map, s=s)) for s in range(ns)]
    return pl.pallas_call(
        body, grid=(M // tm, ns * njs, nk), in_specs=in_specs,
        out_specs=pl.BlockSpec((tm, tn), lambda i, j, k: (i, j)),
        out_shape=SDS((M, ns * Ns), out_dtype),
        scratch_shapes=[pltpu.VMEM((tm, tn), F32)],
        compiler_params=_params("parallel", "parallel", "arbitrary"), name=name,
    )(a, *b_list)


def _mm_nt_rows(w_t, h, *, name, tn=1024):
    R, K = w_t.shape
    S = h.shape[0]
    tn = min(tn, S)

    def body(w_ref, h_ref, o_ref):
        o_ref[...] = _dot_nt(w_ref[...], h_ref[...])

    return pl.pallas_call(
        body, grid=(S // tn,),
        in_specs=[pl.BlockSpec((R, K), lambda i: (0, 0)), pl.BlockSpec((tn, K), lambda i: (i, 0))],
        out_specs=pl.BlockSpec((R, tn), lambda i: (0, i)),
        out_shape=SDS((R, S), F32), compiler_params=_params("parallel"), name=name,
    )(w_t, h)


def _rms_fwd(x, gain, *, name, tm=512):
    S, D = x.shape
    tm = min(tm, S)

    def body(x_ref, g_ref, h_ref):
        xv = x_ref[...]
        r = lax.rsqrt(jnp.mean(xv * xv, axis=-1, keepdims=True) + EPS)
        h_ref[...] = ((xv * r) * g_ref[...]).astype(BF16)

    return pl.pallas_call(
        body, grid=(S // tm,),
        in_specs=[pl.BlockSpec((tm, D), lambda i: (i, 0)), pl.BlockSpec((1, D), lambda i: (0, 0))],
        out_specs=pl.BlockSpec((tm, D), lambda i: (i, 0)),
        out_shape=SDS((S, D), BF16), compiler_params=_params("parallel"), name=name,
    )(x, gain)


def _rms_bwd(x, gain, dh, dres, *, name, tm=256):
    S, D = x.shape
    tm = min(tm, S)

    def body(x_ref, g_ref, dh_ref, dres_ref, dx_ref, dxb_ref, dg_ref):
        @pl.when(pl.program_id(0) == 0)
        def _():
            dg_ref[...] = jnp.zeros_like(dg_ref)

        xv = x_ref[...]
        r = lax.rsqrt(jnp.mean(xv * xv, axis=-1, keepdims=True) + EPS)
        xh = xv * r
        dhv = dh_ref[...].astype(F32)
        dg_ref[...] += jnp.sum(dhv * xh, axis=0, keepdims=True)
        dxh = dhv * g_ref[...]
        dx = r * (dxh - xh * jnp.mean(dxh * xh, axis=-1, keepdims=True)) + dres_ref[...]
        dx_ref[...] = dx
        dxb_ref[...] = dx.astype(BF16)

    row = pl.BlockSpec((tm, D), lambda i: (i, 0))
    vec = pl.BlockSpec((1, D), lambda i: (0, 0))
    return pl.pallas_call(
        body, grid=(S // tm,), in_specs=[row, vec, row, row], out_specs=[row, row, vec],
        out_shape=[SDS((S, D), F32), SDS((S, D), BF16), SDS((1, D), F32)],
        compiler_params=_params("arbitrary"), name=name,
    )(x, gain, dh, dres)


def _loss_head(x, gain, target, *, name, tm=256):
    S, D = x.shape
    tm = min(tm, S)
    assert tm % 8 == 0 and D % 128 == 0

    def body(x_ref, g_ref, t_ref, dx_ref, dxb_ref, loss_ref, dg_ref):
        @pl.when(pl.program_id(0) == 0)
        def _():
            dg_ref[...] = jnp.zeros_like(dg_ref)
            loss_ref[...] = jnp.zeros_like(loss_ref)

        xv = x_ref[...]
        g = g_ref[...]
        r = lax.rsqrt(jnp.mean(xv * xv, axis=-1, keepdims=True) + EPS)
        xh = xv * r
        err = xh * g - t_ref[...]
        e2 = (err * err).reshape(tm // 8, 8, D).sum(axis=0)
        part = e2[:, 0:128]
        for k in range(1, D // 128):
            part = part + e2[:, k * 128:(k + 1) * 128]
        loss_ref[...] += part * (0.5 / D)
        dy = err * (1.0 / D)
        dg_ref[...] += jnp.sum(dy * xh, axis=0, keepdims=True)
        dxh = dy * g
        dx = r * (dxh - xh * jnp.mean(dxh * xh, axis=-1, keepdims=True))
        dx_ref[...] = dx
        dxb_ref[...] = dx.astype(BF16)

    row = pl.BlockSpec((tm, D), lambda i: (i, 0))
    vec = pl.BlockSpec((1, D), lambda i: (0, 0))
    return pl.pallas_call(
        body, grid=(S // tm,), in_specs=[row, vec, row],
        out_specs=[row, row, pl.BlockSpec((8, 128), lambda i: (0, 0)), vec],
        out_shape=[SDS((S, D), F32), SDS((S, D), BF16), SDS((8, 128), F32), SDS((1, D), F32)],
        compiler_params=_params("arbitrary"), name=name,
    )(x, gain, target)


def _split3(x):
    hi = x.astype(BF16)
    r1 = x - hi.astype(F32)
    mid = r1.astype(BF16)
    lo = (r1 - mid.astype(F32)).astype(BF16)
    return hi, mid, lo


def _split2(x):
    hi = x.astype(BF16)
    lo = (x - hi.astype(F32)).astype(BF16)
    return hi, lo


def _fox_gate_fwd(fl_t, b_col, *, name):
    H, S = fl_t.shape
    L = 128
    tri = jnp.asarray(np.triu(np.ones((L, L), np.float32)), BF16)

    def body(fl_ref, b_ref, tri_ref, hi_ref, mid_ref, lo_ref, carry):
        @pl.when(pl.program_id(0) == 0)
        def _():
            carry[...] = jnp.zeros_like(carry)

        z = fl_ref[...] + b_ref[...]
        lf = jnp.minimum(z, 0.0) - jnp.log(1.0 + jnp.exp(-jnp.abs(z)))
        hi, mid, lo = _split3(lf)
        t = tri_ref[...]
        c = (_dot(hi, t) + _dot(mid, t)) + _dot(lo, t) + carry[...]
        carry[...] = c[:, L - 1:L]
        hi_ref[...], mid_ref[...], lo_ref[...] = _split3(c * (-LOG2E))

    blk = pl.BlockSpec((H, L), lambda i: (0, i))
    return pl.pallas_call(
        body, grid=(S // L,),
        in_specs=[blk, pl.BlockSpec((H, 1), lambda i: (0, 0)), pl.BlockSpec((L, L), lambda i: (0, 0))],
        out_specs=[blk] * 3, out_shape=[SDS((H, S), BF16)] * 3, scratch_shapes=[pltpu.VMEM((H, 1), F32)],
        compiler_params=_params("arbitrary"), name=name,
    )(fl_t, b_col, tri)


def _fox_gate_bwd(dc_row, dc_key, fl_t, b_col, *, name):
    H, S = fl_t.shape
    L = 128
    n = S // L
    tri = jnp.asarray(np.tril(np.ones((L, L), np.float32)), BF16)

    def body(dcr_ref, dck_ref, fl_ref, b_ref, tri_ref, dfl_ref, db_ref, carry):
        @pl.when(pl.program_id(0) == 0)
        def _():
            carry[...] = jnp.zeros_like(carry)
            db_ref[...] = jnp.zeros_like(db_ref)

        hi, mid, lo = _split3(dcr_ref[...] + dck_ref[...])
        t = tri_ref[...]
        dlf = (_dot(hi, t) + _dot(mid, t)) + _dot(lo, t) + carry[...]
        carry[...] = dlf[:, 0:1]
        z = fl_ref[...] + b_ref[...]
        dfl = dlf * jax.nn.sigmoid(-z)
        dfl_ref[...] = dfl
        db_ref[...] += jnp.sum(dfl, axis=1, keepdims=True)

    blk = pl.BlockSpec((H, L), lambda i: (0, n - 1 - i))
    col = pl.BlockSpec((H, 1), lambda i: (0, 0))
    return pl.pallas_call(
        body, grid=(n,), in_specs=[blk, blk, blk, col, pl.BlockSpec((L, L), lambda i: (0, 0))],
        out_specs=[blk, col], out_shape=[SDS((H, S), F32), SDS((H, 1), F32)],
        scratch_shapes=[pltpu.VMEM((H, 1), F32)], compiler_params=_params("arbitrary"), name=name,
    )(dc_row, dc_key, fl_t, b_col, tri)


AUG = HEAD_DIM


def _lane_select(cols, shape):
    lane = lax.broadcasted_iota(jnp.int32, shape, 1)
    out = jnp.zeros(shape, BF16)
    for k, c in reversed(list(enumerate(cols))):
        c = jnp.full(shape, c, BF16) if isinstance(c, (int, float)) else jnp.broadcast_to(c, shape).astype(BF16)
        out = jnp.where(lane == k, c, out)
    return out


def _fox_key_aug(b_hi, b_mid, b_lo):
    H, S = b_hi.shape
    ones = jnp.ones((H, S), BF16)
    ka = jnp.stack([b_hi, b_mid, b_lo, ones, ones, ones], axis=-1)
    ka = jnp.pad(ka, ((0, 0), (0, 0), (0, AUG - 6)))
    return jnp.transpose(ka, (1, 0, 2)).reshape(S, H * AUG)


def _fox_fwd(p0, kaug, *, H, name):
    S = p0.shape[0]
    T = min(ATT_BLOCK, S)
    nq = S // T
    dh = HEAD_DIM
    G = ATT_HEADS_PER_STEP
    assert H % G == 0

    def body(q_ref, k_ref, ka_ref, v_ref, g_ref, o_ref, y_ref, qa_ref, m_sc, acc_sc):
        i = pl.program_id(1)
        qaug = _lane_select([1.0, 1.0, 1.0], (T, AUG))
        ones = jnp.ones((T, dh), BF16)
        m_sc[...] = jnp.full_like(m_sc, NEG)
        acc_sc[...] = jnp.zeros_like(acc_sc)

        def step(j, masked):
            rows = pl.ds(pl.multiple_of(j * T, T), T)
            for g in range(G):
                hd = slice(g * dh, (g + 1) * dh)
                q = jnp.concatenate([q_ref[:, hd], qaug], axis=1)
                kj = jnp.concatenate([k_ref[rows, hd], ka_ref[rows, hd]], axis=1)
                vj = jnp.concatenate([v_ref[rows, hd], ones], axis=1)
                t = _dot_nt(q, kj)
                if masked:
                    row = lax.broadcasted_iota(jnp.int32, (T, T), 0)
                    col = lax.broadcasted_iota(jnp.int32, (T, T), 1)
                    t = jnp.where(row >= col, t, NEG)
                m_prev = m_sc[g]
                m_new = jnp.maximum(m_prev, jnp.max(t, axis=-1, keepdims=True))
                p = jnp.exp2(t - jnp.tile(m_new, (1, T // 128)))
                alpha = jnp.exp2(m_prev - m_new)
                acc_sc[g] = jnp.tile(alpha, (1, 2)) * acc_sc[g] + _dot(p.astype(BF16), vj)
                m_sc[g] = m_new

        def loop_body(j, carry):
            step(j, False)
            return carry

        lax.fori_loop(0, i, loop_body, 0)
        step(i, True)
        for g in range(G):
            hd = slice(g * dh, (g + 1) * dh)
            l = acc_sc[g, :, dh:]
            o = acc_sc[g, :, :dh] / l
            o_ref[:, hd] = o
            y_ref[:, hd] = (o * _silu(g_ref[:, hd].astype(F32))).astype(BF16)
            hi, mid, lo = _split3(-(m_sc[g] + jnp.log2(l)))
            qa_ref[:, hd] = _lane_select([1.0, 1.0, 1.0, hi, mid, lo], (T, AUG))

    blk = lambda off: pl.BlockSpec((T, G * dh), lambda h, i: (i, off // G + h))
    full = lambda off: pl.BlockSpec((S, G * dh), lambda h, i: (0, off // G + h))
    return pl.pallas_call(
        body, grid=(H // G, nq),
        in_specs=[blk(0), full(H), full(0), full(2 * H), blk(3 * H)],
        out_specs=[blk(0), blk(0), blk(0)],
        out_shape=[SDS((S, H * dh), F32), SDS((S, H * dh), BF16), SDS((S, H * AUG), BF16)],
        scratch_shapes=[pltpu.VMEM((G, T, 128), F32), pltpu.VMEM((G, T, 2 * dh), F32)],
        compiler_params=_params("parallel", "arbitrary"), name=name,
    )(p0, p0, kaug, p0, p0)


def _fox_post_bwd(dy, o, p0, *, H, name, tm=512):
    S = dy.shape[0]
    dh = HEAD_DIM
    tm = min(tm, S)
    G = POST_HEADS_PER_STEP
    assert H % G == 0

    def body(dy_ref, o_ref, g_ref, do_ref, dg_ref, da_ref):
        dyv = dy_ref[...].astype(F32)
        ov = o_ref[...]
        g = g_ref[...].astype(F32)
        do = (dyv * _silu(g)).astype(BF16)
        do_ref[...] = do
        dg_ref[...] = (dyv * ov * _dsilu(g)).astype(BF16)
        prod = do.astype(F32) * ov
        for k in range(G):
            hd = slice(k * dh, (k + 1) * dh)
            delta = jnp.sum(prod[:, hd], axis=-1, keepdims=True)
            hi, mid, lo = _split3(-jnp.broadcast_to(delta, (tm, AUG)))
            da_ref[:, hd] = _lane_select([hi, mid, lo], (tm, AUG))

    blk = pl.BlockSpec((tm, G * dh), lambda h, i: (i, h))
    return pl.pallas_call(
        body, grid=(H // G, S // tm),
        in_specs=[blk, blk, pl.BlockSpec((tm, G * dh), lambda h, i: (i, 3 * H // G + h))],
        out_specs=[blk, blk, blk],
        out_shape=[SDS((S, H * dh), BF16), SDS((S, H * dh), BF16), SDS((S, H * AUG), BF16)],
        compiler_params=_params("parallel", "parallel"), name=name,
    )(dy, o, p0)


def _fox_dq(p0, kaug, qaug, do, doaug, *, H, name):
    S = p0.shape[0]
    T = min(ATT_BLOCK, S)
    nq = S // T
    dh = HEAD_DIM
    scale = dh ** -0.5
    G = ATT_HEADS_PER_STEP
    assert H % G == 0

    def body(q_ref, qa_ref, k_ref, ka_ref, v_ref, do_ref, da_ref, dq_ref, rs_ref, acc_sc):
        i = pl.program_id(1)
        vaug = _lane_select([1.0, 1.0, 1.0], (T, AUG))
        ones = jnp.ones((T, dh), BF16)
        acc_sc[...] = jnp.zeros_like(acc_sc)

        def step(j, masked):
            rows = pl.ds(pl.multiple_of(j * T, T), T)
            for g in range(G):
                hd = slice(g * dh, (g + 1) * dh)
                q = jnp.concatenate([q_ref[:, hd], qa_ref[:, hd]], axis=1)
                do = jnp.concatenate([do_ref[:, hd], da_ref[:, hd]], axis=1)
                k = k_ref[rows, hd]
                p = jnp.exp2(_dot_nt(q, jnp.concatenate([k, ka_ref[rows, hd]], axis=1)))
                if masked:
                    row = lax.broadcasted_iota(jnp.int32, (T, T), 0)
                    col = lax.broadcasted_iota(jnp.int32, (T, T), 1)
                    p = jnp.where(row >= col, p, 0.0)
                ds = p * _dot_nt(do, jnp.concatenate([v_ref[rows, hd], vaug], axis=1))
                acc_sc[g] += _dot(ds.astype(BF16), jnp.concatenate([k, ones], axis=1))

        def loop_body(j, carry):
            step(j, False)
            return carry

        lax.fori_loop(0, i, loop_body, 0)
        step(i, True)
        for g in range(G):
            dq_ref[:, g * dh:(g + 1) * dh] = (acc_sc[g, :, :dh] * scale).astype(BF16)
            rs_ref[g] = acc_sc[g, :, dh:dh + 1]

    blk = lambda off: pl.BlockSpec((T, G * dh), lambda h, i: (i, off // G + h))
    full = lambda off: pl.BlockSpec((S, G * dh), lambda h, i: (0, off // G + h))
    return pl.pallas_call(
        body, grid=(H // G, nq),
        in_specs=[blk(0), blk(0), full(H), full(0), full(2 * H), blk(0), blk(0)],
        out_specs=[blk(0), pl.BlockSpec((G, T, 1), lambda h, i: (h, i, 0))],
        out_shape=[SDS((S, H * dh), BF16), SDS((H, S, 1), F32)],
        scratch_shapes=[pltpu.VMEM((G, T, 2 * dh), F32)],
        compiler_params=_params("parallel", "arbitrary"), name=name,
    )(p0, qaug, p0, kaug, p0, do, doaug)


def _fox_dkv(p0, kaug, qaug, do, doaug, *, H, name):
    S = p0.shape[0]
    T = min(ATT_BLOCK, S)
    nq = S // T
    dh = HEAD_DIM
    G = ATT_HEADS_PER_STEP
    assert H % G == 0

    def body(q_ref, qa_ref, k_ref, ka_ref, v_ref, do_ref, da_ref, dk_ref, dv_ref, dc_ref, dk_sc, dv_sc):
        j = pl.program_id(1)
        vaug = _lane_select([1.0, 1.0, 1.0], (T, AUG))
        ones = jnp.ones((T, dh), BF16)
        dk_sc[...] = jnp.zeros_like(dk_sc)
        dv_sc[...] = jnp.zeros_like(dv_sc)

        def step(i, masked):
            rows = pl.ds(pl.multiple_of(i * T, T), T)
            for g in range(G):
                hd = slice(g * dh, (g + 1) * dh)
                k = jnp.concatenate([k_ref[:, hd], ka_ref[:, hd]], axis=1)
                v = jnp.concatenate([v_ref[:, hd], vaug], axis=1)
                qi = q_ref[rows, hd]
                doi = do_ref[rows, hd]
                pt = jnp.exp2(_dot_nt(k, jnp.concatenate([qi, qa_ref[rows, hd]], axis=1)))
                if masked:
                    row = lax.broadcasted_iota(jnp.int32, (T, T), 0)
                    col = lax.broadcasted_iota(jnp.int32, (T, T), 1)
                    pt = jnp.where(col >= row, pt, 0.0)
                dv_sc[g] += _dot(pt.astype(BF16), doi)
                dst = pt * _dot_nt(v, jnp.concatenate([doi, da_ref[rows, hd]], axis=1))
                dk_sc[g] += _dot(dst.astype(BF16), jnp.concatenate([qi, ones], axis=1))

        step(j, True)

        def loop_body(i, carry):
            step(i, False)
            return carry

        lax.fori_loop(j + 1, nq, loop_body, 0)
        for g in range(G):
            hd = slice(g * dh, (g + 1) * dh)
            dk_ref[:, hd] = (dk_sc[g, :, :dh] * LN2).astype(BF16)
            dv_ref[:, hd] = dv_sc[g].astype(BF16)
            dc_ref[g] = -dk_sc[g, :, dh:dh + 1]

    blk = lambda off: pl.BlockSpec((T, G * dh), lambda h, j: (j, off // G + h))
    full = lambda off: pl.BlockSpec((S, G * dh), lambda h, j: (0, off // G + h))
    return pl.pallas_call(
        body, grid=(H // G, nq),
        in_specs=[full(0), full(0), blk(H), blk(0), blk(2 * H), full(0), full(0)],
        out_specs=[blk(0), blk(0), pl.BlockSpec((G, T, 1), lambda h, j: (h, j, 0))],
        out_shape=[SDS((S, H * dh), BF16), SDS((S, H * dh), BF16), SDS((H, S, 1), F32)],
        scratch_shapes=[pltpu.VMEM((G, T, 2 * dh), F32), pltpu.VMEM((G, T, dh), F32)],
        compiler_params=_params("parallel", "arbitrary"), name=name,
    )(p0, qaug, p0, kaug, p0, do, doaug)


def _hgrn_levels(C, leaf):
    levels = []
    h = C // 2
    while h >= leaf:
        levels.append(h)
        h //= 2
    return levels


def _hgrn_sum_matrix(C, leaf):
    t = np.arange(C)[:, None]
    u = np.arange(C)[None, :]
    mats = [(u <= t), (u > t)]
    for h in _hgrn_levels(C, leaf):
        start = (t // (2 * h)) * (2 * h)
        mid = start + h - 1
        second = t > mid
        m = np.where(second, (u > mid) & (u <= t), (u > t) & (u <= mid))
        mats.append(m)
    lstart = (t // leaf) * leaf
    mats.append((u >= lstart) & (u <= t))
    return np.concatenate([m.astype(np.float32) for m in mats], axis=0)


def _hgrn_chunk_terms(qr, fz, lb, msum, C, leaf):
    levels = _hgrn_levels(C, leaf)
    sq = _silu(qr)
    t = jnp.exp(-jnp.abs(fz))
    r = 1.0 / (1.0 + t)
    pos = fz >= 0.0
    sp = jnp.where(pos, r, t * r)
    sn = jnp.where(pos, t * r, r)
    f = lb + (1.0 - lb) * sp
    lf = jnp.log(f)
    k = (1.0 - lb) * sn
    hi, lo = _split2(lf)
    dsum = _dot(msum, hi) + _dot(msum, lo)
    b = dsum[0:C]
    kdec = dsum[C:2 * C]
    rowi = lax.broadcasted_iota(jnp.int32, (C, 1), 0)
    lev = []
    for n, h in enumerate(levels):
        e = jnp.exp(dsum[(2 + n) * C:(3 + n) * C])
        second = (rowi % (2 * h)) >= h
        qm = jnp.where(second, sq * e, 0.0).astype(BF16)
        km = jnp.where(second, 0.0, k * e).astype(BF16)
        lev.append((h, e, second, qm, km))
    dleaf = dsum[(2 + len(levels)) * C:(3 + len(levels)) * C]
    eq = jnp.exp(dleaf)
    ek = jnp.exp(jnp.minimum(-dleaf, EXP_CLAMP))
    return dict(sq=sq, sp=sp, sn=sn, f=f, k=k, b=b, kdec=kdec, lev=lev, eq=eq, ek=ek,
                ql=(sq * eq).astype(BF16), kl=(k * ek).astype(BF16),
                qs=(sq * jnp.exp(b)).astype(BF16), ke=(k * jnp.exp(kdec)).astype(BF16),
                e_c=jnp.exp(b[C - 1:C, :]))


def _hgrn_masks(C, leaf, transposed):
    a = lax.broadcasted_iota(jnp.int32, (C, C), 0)
    bb = lax.broadcasted_iota(jnp.int32, (C, C), 1)
    t, s = (bb, a) if transposed else (a, bb)
    lev = [None if 2 * h == C else (t // (2 * h)) == (s // (2 * h)) for h in _hgrn_levels(C, leaf)]
    if leaf == C:
        leafm = s <= t
    else:
        leafm = ((t // leaf) == (s // leaf)) & (s <= t)
    return lev, leafm


def _hgrn_fwd(p1, f1, lb, onorm, *, H, name, tb=512):
    S = p1.shape[0]
    dk = HEAD_DIM
    C = min(HGRN_CHUNK, S)
    leaf = min(HGRN_LEAF, C)
    tb = min(tb, S)
    nc = tb // C
    G = HGRN_HEADS_PER_STEP
    assert H % G == 0
    msum = jnp.asarray(_hgrn_sum_matrix(C, leaf), BF16)

    def body(q_ref, f_ref, v_ref, g_ref, lb_ref, on_ref, ms_ref, o_ref, y_ref, st_ref, st_sc):
        @pl.when(pl.program_id(1) == 0)
        def _():
            st_sc[...] = jnp.zeros_like(st_sc)

        msv = ms_ref[...]
        lmask, leafm = _hgrn_masks(C, leaf, False)

        def chunk(n, carry):
            rows = pl.ds(pl.multiple_of(n * C, C), C)
            for g in range(G):
                hd = slice(g * dk, (g + 1) * dk)
                tm = _hgrn_chunk_terms(q_ref[rows, hd].astype(F32), f_ref[rows, hd], lb_ref[:, hd], msv, C, leaf)
                v = v_ref[rows, hd]
                st = st_sc[g]
                st_ref[g, n] = st
                a = jnp.where(leafm, _dot_nt(tm["ql"], tm["kl"]), 0.0)
                for (h, e, second, qm, km), m in zip(tm["lev"], lmask):
                    al = _dot_nt(qm, km)
                    a = a + (al if m is None else jnp.where(m, al, 0.0))
                o = _dot_nt(tm["qs"], st.astype(BF16)) + _dot(a.astype(BF16), v)
                st_sc[g] = st * tm["e_c"] + _dot(v.T, tm["ke"])
                o_ref[rows, hd] = o
                rn = lax.rsqrt(jnp.mean(o * o, axis=-1, keepdims=True) + EPS)
                y = ((o * rn) * on_ref[:, hd]) * _silu(g_ref[rows, hd].astype(F32))
                y_ref[rows, hd] = y.astype(BF16)
            return carry

        lax.fori_loop(0, nc, chunk, 0)

    blk = lambda off: pl.BlockSpec((tb, G * dk), lambda h, i: (i, off // G + h))
    vec = pl.BlockSpec((1, G * dk), lambda h, i: (0, h))
    return pl.pallas_call(
        body, grid=(H // G, S // tb),
        in_specs=[blk(0), blk(0), blk(H), blk(2 * H), vec, vec,
                  pl.BlockSpec(msum.shape, lambda h, i: (0, 0))],
        out_specs=[blk(0), blk(0), pl.BlockSpec((G, nc, dk, dk), lambda h, i: (h, i, 0, 0))],
        out_shape=[SDS((S, H * dk), F32), SDS((S, H * dk), BF16), SDS((H, S // C, dk, dk), F32)],
        scratch_shapes=[pltpu.VMEM((G, dk, dk), F32)],
        compiler_params=_params("parallel", "arbitrary"), name=name,
    )(p1, f1, p1, p1, lb, onorm, msum)


def _hgrn_post_bwd(dy, o, p1, onorm, *, H, name, tm=512):
    S = dy.shape[0]
    dk = HEAD_DIM
    tm = min(tm, S)
    G = POST_HEADS_PER_STEP
    assert H % G == 0

    def body(dy_ref, o_ref, g_ref, on_ref, do_ref, dg_ref, don_ref):
        @pl.when(pl.program_id(1) == 0)
        def _():
            don_ref[...] = jnp.zeros_like(don_ref)

        for k in range(G):
            hd = slice(k * dk, (k + 1) * dk)
            dyv = dy_ref[:, hd].astype(F32)
            ov = o_ref[:, hd]
            g = g_ref[:, hd].astype(F32)
            onv = on_ref[:, hd]
            rn = lax.rsqrt(jnp.mean(ov * ov, axis=-1, keepdims=True) + EPS)
            oh = ov * rn
            dn = dyv * _silu(g)
            dg_ref[:, hd] = (dyv * (oh * onv) * _dsilu(g)).astype(BF16)
            don_ref[:, hd] += jnp.sum(dn * oh, axis=0, keepdims=True)
            doh = dn * onv
            do_ref[:, hd] = (rn * (doh - oh * jnp.mean(doh * oh, axis=-1, keepdims=True))).astype(BF16)

    blk = pl.BlockSpec((tm, G * dk), lambda h, i: (i, h))
    vec = pl.BlockSpec((1, G * dk), lambda h, i: (0, h))
    return pl.pallas_call(
        body, grid=(H // G, S // tm),
        in_specs=[blk, blk, pl.BlockSpec((tm, G * dk), lambda h, i: (i, 2 * H // G + h)), vec],
        out_specs=[blk, blk, vec],
        out_shape=[SDS((S, H * dk), BF16), SDS((S, H * dk), BF16), SDS((1, H * dk), F32)],
        compiler_params=_params("parallel", "arbitrary"), name=name,
    )(dy, o, p1, onorm)


def _hgrn_bwd(p1, f1, lb, do, states, *, H, name, tb=512):
    S = p1.shape[0]
    dk = HEAD_DIM
    C = min(HGRN_CHUNK, S)
    leaf = min(HGRN_LEAF, C)
    tb = min(tb, S)
    nc = tb // C
    nb = S // tb
    G = HGRN_HEADS_PER_STEP
    assert H % G == 0
    msum = jnp.asarray(_hgrn_sum_matrix(C, leaf), BF16)
    rtri = jnp.asarray(np.triu(np.ones((C, C), np.float32)), BF16)

    def body(q_ref, f_ref, v_ref, do_ref, st_ref, lb_ref, ms_ref, rt_ref,
             dq_ref, df_ref, dv_ref, dlb_ref, g_sc):
        @pl.when(pl.program_id(1) == 0)
        def _():
            g_sc[...] = jnp.zeros_like(g_sc)
            dlb_ref[...] = jnp.zeros_like(dlb_ref)

        msv = ms_ref[...]
        rtv = rt_ref[...]
        lmask, leafm = _hgrn_masks(C, leaf, False)
        lmask_t, leafm_t = _hgrn_masks(C, leaf, True)
        f32 = lambda z: z.astype(F32)

        def head_chunk(g, n):
            hd = slice(g * dk, (g + 1) * dk)
            rows = pl.ds(pl.multiple_of(n * C, C), C)
            lbv = lb_ref[:, hd]
            qr = q_ref[rows, hd].astype(F32)
            tm = _hgrn_chunk_terms(qr, f_ref[rows, hd], lbv, msv, C, leaf)
            v = v_ref[rows, hd]
            dov = do_ref[rows, hd]
            st0 = st_ref[g, n]
            gt = g_sc[g]
            gtb = gt.astype(BF16)
            da = _dot_nt(dov, v)
            da_t = _dot_nt(v, dov)

            dal = jnp.where(leafm, da, 0.0).astype(BF16)
            dal_t = jnp.where(leafm_t, da_t, 0.0).astype(BF16)
            dql = _dot(dal, tm["kl"])
            dkl = _dot(dal_t, tm["ql"])
            dsq = dql * tm["eq"]
            dkk = dkl * tm["ek"]
            xq = f32(tm["ql"]) * dql
            xk = f32(tm["kl"]) * dkl
            a_t = jnp.where(leafm_t, _dot_nt(tm["kl"], tm["ql"]), 0.0)
            for (h, e, second, qm, km), m, m_t in zip(tm["lev"], lmask, lmask_t):
                dl = (da if m is None else jnp.where(m, da, 0.0)).astype(BF16)
                dl_t = (da_t if m_t is None else jnp.where(m_t, da_t, 0.0)).astype(BF16)
                dqm = _dot(dl, km)
                dkm = _dot(dl_t, qm)
                dsq = dsq + jnp.where(second, dqm * e, 0.0)
                dkk = dkk + jnp.where(second, 0.0, dkm * e)
                xq = xq + f32(qm) * dqm
                xk = xk + f32(km) * dkm
                al_t = _dot_nt(km, qm)
                a_t = a_t + (al_t if m_t is None else jnp.where(m_t, al_t, 0.0))
            dqs = _dot(dov, st0.astype(BF16))
            dke = _dot(v, gtb)
            dsq = dsq + dqs * jnp.exp(tm["b"])
            dkk = dkk + dke * jnp.exp(tm["kdec"])
            xq = xq + f32(tm["qs"]) * dqs
            xk = xk + f32(tm["ke"]) * dke
            dvv = _dot(a_t.astype(BF16), dov) + _dot_nt(tm["ke"], gtb)
            r_end = jnp.sum(f32(gtb) * _dot(v.T, tm["ke"]) + gt * (st0 * tm["e_c"]), axis=0, keepdims=True)
            g_sc[g] = gt * tm["e_c"] + _dot(dov.T, tm["qs"])
            xh, xm, xl = _split3(xq - xk)
            dlf = (_dot(rtv, xh) + _dot(rtv, xm)) + _dot(rtv, xl) + r_end
            dlf_f = dlf / tm["f"]
            dsp = (1.0 - lbv) * (dlf_f - dkk)
            df_ref[rows, hd] = (dsp * (tm["sp"] * tm["sn"])).astype(BF16)
            dq_ref[rows, hd] = (dsq * _dsilu(qr)).astype(BF16)
            dv_ref[rows, hd] = dvv.astype(BF16)
            dlb_ref[:, hd] += jnp.sum(dlf_f * tm["sn"] - dkk * tm["sn"], axis=0, keepdims=True)

        def chunk(nn, carry):
            for g in range(G):
                head_chunk(g, nc - 1 - nn)
            return carry

        lax.fori_loop(0, nc, chunk, 0)

    blk = lambda off: pl.BlockSpec((tb, G * dk), lambda h, i: (nb - 1 - i, off // G + h))
    vec = pl.BlockSpec((1, G * dk), lambda h, i: (0, h))
    return pl.pallas_call(
        body, grid=(H // G, nb),
        in_specs=[blk(0), blk(0), blk(H), blk(0),
                  pl.BlockSpec((G, nc, dk, dk), lambda h, i: (h, nb - 1 - i, 0, 0)), vec,
                  pl.BlockSpec(msum.shape, lambda h, i: (0, 0)), pl.BlockSpec((C, C), lambda h, i: (0, 0))],
        out_specs=[blk(0), blk(0), blk(0), vec],
        out_shape=[SDS((S, H * dk), BF16)] * 3 + [SDS((1, H * dk), F32)],
        scratch_shapes=[pltpu.VMEM((G, dk, dk), F32)],
        compiler_params=_params("parallel", "arbitrary"), name=name,
    )(p1, f1, p1, do, states, lb, msum, rtri)


def _lb_fwd(logits, *, name):
    W = logits.shape[1]

    def body(l_ref, lb_ref):
        l = l_ref[...]
        m = jnp.max(l, axis=0, keepdims=True)
        e = jnp.exp(l - m)
        p = e / jnp.sum(e, axis=0, keepdims=True)
        lb_ref[...] = (p[0:1] + p[1:2]) - p[0:1]

    return pl.pallas_call(body, out_shape=SDS((1, W), F32), name=name)(logits)


STAT_ROWS = 8


def _stats_reduce(stats_all, logits, *, name):
    W = logits.shape[1]

    def body(s_ref, l_ref, g_ref):
        tot = s_ref[0]
        for d in range(1, N_DEV):
            tot = tot + s_ref[d]
        l = l_ref[...]
        m = jnp.max(l, axis=0, keepdims=True)
        e = jnp.exp(l - m)
        p = e / jnp.sum(e, axis=0, keepdims=True)
        dlb = tot[2:3]
        dl0 = -(p[0:1] * p[1:2]) * dlb
        dl1 = (p[1:2] * (1.0 - p[1:2])) * dlb
        g_ref[0:2] = tot[0:2]
        g_ref[2:3] = dl0
        g_ref[3:4] = dl1
        g_ref[4:7] = tot[3:6]
        g_ref[7:8] = jnp.zeros((1, W), F32)

    return pl.pallas_call(body, out_shape=SDS((STAT_ROWS, W), F32), name=name)(stats_all, logits)


def _adamw(w, m, v, g_parts, *, name, tr=128):
    R, C = w.shape
    ns = len(g_parts)
    n, Rs = g_parts[0].shape[0], g_parts[0].shape[1]
    assert all(p.shape == (n, Rs, C) for p in g_parts) and ns * Rs == R
    tr = min(tr, Rs)
    assert Rs % tr == 0
    nts = Rs // tr
    c1 = 1.0 / (1.0 - ADAM_B1 ** ADAM_STEP)
    c2 = 1.0 / (1.0 - ADAM_B2 ** ADAM_STEP)

    def body(*refs):
        w_ref, m_ref, v_ref = refs[:3]
        g_refs = refs[3:3 + ns]
        go_ref, d_ref, mo_ref, vo_ref = refs[3 + ns:]

        def update(g_ref):
            g = g_ref[0].astype(F32)
            for k in range(1, n):
                g = g + g_ref[k].astype(F32)
            mn = ADAM_B1 * m_ref[...] + (1.0 - ADAM_B1) * g
            vn = ADAM_B2 * v_ref[...] + (1.0 - ADAM_B2) * (g * g)
            d_ref[...] = -ADAM_LR * ((mn * c1) / (jnp.sqrt(vn * c2) + ADAM_EPS) + ADAM_WD * w_ref[...])
            go_ref[...] = g
            mo_ref[...] = mn
            vo_ref[...] = vn

        for s in range(ns):
            if ns == 1:
                update(g_refs[s])
            else:
                pl.when(pl.program_id(0) // nts == s)(functools.partial(update, g_refs[s]))

    def g_map(i, s):
        return (0, jnp.clip(i - s * nts, 0, nts - 1), 0)

    blk = pl.BlockSpec((tr, C), lambda i: (i, 0))
    return pl.pallas_call(
        body, grid=(R // tr,),
        in_specs=[blk, blk, blk] + [pl.BlockSpec((n, tr, C), functools.partial(g_map, s=s)) for s in range(ns)],
        out_specs=[blk] * 4, out_shape=[SDS((R, C), F32)] * 4,
        compiler_params=_params("parallel"), name=name,
    )(w, m, v, *g_parts)


ANY = pl.BlockSpec(memory_space=pl.ANY)


def _all_gather(shards, out_shapes, views, *, name):
    n = len(shards)

    def body(*refs):
        ins, outs = refs[:n], refs[n:2 * n]
        send_sems, recv_sems, local_sems = refs[2 * n:]
        x, y, c = lax.axis_index("x"), lax.axis_index("y"), lax.axis_index("c")
        me, sibling = (x, y, c), (x, y, 1 - c)
        chips = [(1 - x, y), (x, 1 - y), (1 - x, 1 - y)]

        def dev(p):
            return 4 * p[0] + 2 * p[1] + p[2]

        def copy(a, k, block, to, src=None):
            dst = views[a](outs[a], dev(block))
            return pltpu.make_async_remote_copy(
                src_ref=dst if src is None else src, dst_ref=dst,
                send_sem=send_sems.at[a, k], recv_sem=recv_sems.at[a, k],
                device_id=to, device_id_type=MESH)

        mine, first, passed = [], [], []
        for a in range(n):
            cp = pltpu.make_async_copy(ins[a], views[a](outs[a], dev(me)), local_sems.at[a])
            cp.start()
            mine.append(cp)
            first.append(copy(a, 0, me, sibling, src=ins[a]))
            first += [copy(a, 1 + j, me, (*chip, c), src=ins[a]) for j, chip in enumerate(chips)]
        for cp in first:
            cp.start()
        for j, chip in enumerate(chips):
            for a in range(n):
                copy(a, 1 + j, (*chip, c), me).wait_recv()
                cp = copy(a, 4 + j, (*chip, c), sibling)
                cp.start()
                passed.append(cp)
        for a in range(n):
            copy(a, 0, sibling, me).wait_recv()
            for j, chip in enumerate(chips):
                copy(a, 4 + j, (*chip, 1 - c), me).wait_recv()
        for cp in first + passed:
            cp.wait_send()
        for cp in mine:
            cp.wait()

    return pl.pallas_call(
        body, in_specs=[ANY] * n, out_specs=[ANY] * n, out_shape=list(out_shapes),
        scratch_shapes=[pltpu.SemaphoreType.DMA((n, 7)), pltpu.SemaphoreType.DMA((n, 7)),
                        pltpu.SemaphoreType.DMA((n,))],
        name=name,
    )(*shards)


HBM = pl.BlockSpec(memory_space=pltpu.HBM)
SEM = pl.BlockSpec(memory_space=pltpu.SEMAPHORE)
EFFECT = pltpu.SideEffectType.DATAFLOW_SIDE_EFFECTING


def _relations(x, y, c):
    for m in range(1, N_DEV):
        yield m, (1 - x if m & 4 else x, 1 - y if m & 2 else y, 1 - c if m & 1 else c)


def _dev_id(p):
    return 4 * p[0] + 2 * p[1] + p[2]


def _send_start(srcs, land_shapes, src_views, dst_views, *, name):
    n = len(srcs)

    def body(*refs):
        ins, lands = refs[:n], refs[n:2 * n]
        send_sems, recv_sems, token = refs[2 * n], refs[2 * n + 1], refs[-1]
        x, y, c = lax.axis_index("x"), lax.axis_index("y"), lax.axis_index("c")
        me = _dev_id((x, y, c))
        for m, p in _relations(x, y, c):
            for a in range(n):
                pltpu.make_async_remote_copy(
                    src_ref=src_views[a](ins[a], me, _dev_id(p), m), dst_ref=dst_views[a](lands[a], me, m),
                    send_sem=send_sems.at[a * (N_DEV - 1) + m - 1], recv_sem=recv_sems.at[a * (N_DEV - 1) + m - 1],
                    device_id=p, device_id_type=MESH).start()
        token[...] = jnp.zeros_like(token)

    lands = [pltpu.with_memory_space_constraint(lax.empty(s.shape, s.dtype), pltpu.HBM) for s in land_shapes]
    srcs = [pltpu.with_memory_space_constraint(v, pltpu.HBM) for v in srcs]
    res = pl.pallas_call(
        body, name=name,
        out_shape=[pltpu.SemaphoreType.DMA((n * (N_DEV - 1),)), pltpu.SemaphoreType.DMA((n * (N_DEV - 1),))]
        + [pltpu.HBM(v.shape, v.dtype) for v in srcs] + [pltpu.HBM(s.shape, s.dtype) for s in land_shapes]
        + [SDS((8, 128), F32)],
        in_specs=[HBM] * (2 * n), out_specs=[SEM, SEM] + [HBM] * (2 * n) + [pl.BlockSpec(memory_space=pltpu.VMEM)],
        input_output_aliases={i: 2 + i for i in range(2 * n)},
        compiler_params=pltpu.CompilerParams(has_side_effects=EFFECT),
    )(*srcs, *lands)
    return res[0], res[1], res[2:2 + n], res[2 + n:2 + 2 * n], res[-1]


def _send_wait(started, src_views, dst_views, own_views, after, *, name):
    send_sems, recv_sems, srcs, lands, _ = started
    n = len(srcs)

    def body(*refs):
        ins, lnd = refs[:n], refs[n:2 * n]
        send_sems, recv_sems = refs[2 * n], refs[2 * n + 1]
        got = refs[2 * n + 3 + n:2 * n + 3 + 2 * n]
        local_sems = refs[-1]
        x, y, c = lax.axis_index("x"), lax.axis_index("y"), lax.axis_index("c")
        me = _dev_id((x, y, c))
        for m, p in _relations(x, y, c):
            for a in range(n):
                cp = pltpu.make_async_remote_copy(
                    src_ref=src_views[a](ins[a], me, _dev_id(p), m), dst_ref=dst_views[a](lnd[a], me, m),
                    send_sem=send_sems.at[a * (N_DEV - 1) + m - 1], recv_sem=recv_sems.at[a * (N_DEV - 1) + m - 1],
                    device_id=p, device_id_type=MESH)
                cp.wait_send()
                cp.wait_recv()
        own = []
        for a in range(n):
            frm, to = own_views[a](ins[a], got[a], me)
            cp = pltpu.make_async_copy(frm, to, local_sems.at[a])
            cp.start()
            own.append(cp)
        for cp in own:
            cp.wait()

    res = pl.pallas_call(
        body, name=name,
        out_shape=[pltpu.HBM(v.shape, v.dtype) for v in srcs] + [pltpu.HBM(v.shape, v.dtype) for v in lands],
        in_specs=[HBM] * (2 * n) + [SEM, SEM, ANY], out_specs=[HBM] * (2 * n),
        input_output_aliases={i: i for i in range(2 * n)},
        scratch_shapes=[pltpu.SemaphoreType.DMA((n,))],
        compiler_params=pltpu.CompilerParams(has_side_effects=EFFECT),
    )(*srcs, *lands, send_sems, recv_sems, after)
    return res[n:]


def kernel(x, norm_gains, fox_w_in, fox_b_f, hgrn_w_in, hgrn_lb_logits, hgrn_onorm, w_out, final_gain, loss_target, m_norm_gains, m_fox_w_in, m_fox_b_f, m_hgrn_w_in, m_hgrn_lb_logits, m_hgrn_onorm, m_w_out, m_final_gain, v_norm_gains, v_fox_w_in, v_fox_b_f, v_hgrn_w_in, v_hgrn_lb_logits, v_hgrn_onorm, v_w_out, v_final_gain):
    _, S, D = x.shape
    H = FOX_HEADS
    W = H * HEAD_DIM
    assert HGRN_HEADS == H and w_out.shape[2] == D
    cf = fox_w_in.shape[2]
    ch = hgrn_w_in.shape[2]
    ro = w_out.shape[1]
    co = hgrn_onorm.shape[1]
    assert N_DEV * cf == 4 * W + H and N_DEV * ch == 4 * W and N_DEV * ro == W and N_DEV * co == W
    x2 = x.reshape(S, D)
    tgt = loss_target.reshape(S, D)

    col = lambda n: (lambda r, i: r.at[:, pl.ds(pl.multiple_of(i * n, n), n)])
    row = lambda n: (lambda r, i: r.at[pl.ds(pl.multiple_of(i * n, n), n), :])
    late_views = [col(ch), row(ro), col(co)]
    late = _send_start(
        [hgrn_w_in[0].astype(BF16), w_out[1].astype(BF16), hgrn_onorm],
        [SDS((D, 4 * W), BF16), SDS((W, D), BF16), SDS((1, W), F32)],
        [lambda r, me, p, m: r] * 3, [lambda r, me, m, v=v: v(r, me) for v in late_views],
        name="gather_layer1_start")
    ng0 = norm_gains[0:1] + late[4][0:1, 0:1]

    wf_g, wo0 = _all_gather(
        [fox_w_in[0].astype(BF16), w_out[0].astype(BF16)], [SDS((N_DEV, D, cf), BF16), SDS((W, D), BF16)],
        [lambda r, p: r.at[p], row(ro)], name="gather_layer0")
    wf = jnp.transpose(wf_g, (1, 0, 2)).reshape(D, N_DEV * cf)
    wf_main = jnp.concatenate([wf[:, :3 * W], wf[:, 3 * W + H:]], axis=1)
    wfl_t = wf[:, 3 * W:3 * W + H].T
    wf_main_t = wf_main.T
    wo0_t = wo0.T

    h0 = _rms_fwd(x2, ng0, name="rms0_fwd")
    p0 = _mm_nn([h0], wf_main, BF16, scale_cols=(W, LOG2E * HEAD_DIM ** -0.5), name="fox_in_proj")
    fl_t = _mm_nt_rows(wfl_t, h0, name="fox_forget_proj")
    b_col = fox_b_f.reshape(H, 1)
    kaug = _fox_key_aug(*_fox_gate_fwd(fl_t, b_col, name="fox_gate_fwd"))
    o0, y0, qaug = _fox_fwd(p0, kaug, H=H, name="fox_attn_fwd")
    x1 = _mm_nn([y0], wo0, F32, residual=x2, name="fox_out_proj")

    wh, wo1, onorm = _send_wait(
        late, [lambda r, me, p, m: r] * 3, [lambda r, me, m, v=v: v(r, me) for v in late_views],
        [lambda src, land, me, v=v: (src, v(land, me)) for v in late_views], x1, name="gather_layer1_wait")
    wh_qig = jnp.concatenate([wh[:, :W], wh[:, 2 * W:]], axis=1)
    wh_f = wh[:, W:2 * W]
    wh_t = wh.T
    wo1_t = wo1.T
    lb = _lb_fwd(hgrn_lb_logits, name="hgrn_lower_bound")
    h1 = _rms_fwd(x1, norm_gains[1:2], name="rms1_fwd")
    p1 = _mm_nn([h1], wh_qig, BF16, name="hgrn_in_proj")
    f1 = _mm_nn([h1], wh_f, F32, name="hgrn_forget_proj")
    o1, y1, states = _hgrn_fwd(p1, f1, lb, onorm, H=H, name="hgrn_fwd")
    xo = _mm_nn([y1], wo1, F32, residual=x1, name="hgrn_out_proj")

    dx2, dx2b, loss_part, dgf = _loss_head(xo, final_gain.reshape(1, D), tgt, name="loss_head")
    loss = lax.psum(jnp.sum(loss_part), ("x", "y", "c"))

    dy1 = _mm_nn([dx2b], wo1_t, BF16, name="hgrn_out_proj_dx")
    dwo1 = _mm_tn(y1, [dx2b], BF16, name="hgrn_out_proj_dw")
    do1, dg1, donorm = _hgrn_post_bwd(dy1, o1, p1, onorm, H=H, name="hgrn_post_bwd")
    dq1, df1, di1, dlb = _hgrn_bwd(p1, f1, lb, do1, states, H=H, name="hgrn_bwd")
    segs1 = [dq1, df1, di1, dg1]
    dh1 = _mm_nn(segs1, wh_t, BF16, name="hgrn_in_proj_dx")
    dwh = _mm_tn(h1, segs1, BF16, name="hgrn_in_proj_dw")
    part_views = [col(ch), row(ro)]
    slot = lambda r, me, m: r.at[m]
    ex1 = _send_start([dwh, dwo1], [SDS((N_DEV, D, ch), BF16), SDS((N_DEV, ro, D), BF16)],
                      [lambda r, me, p, m, v=v: v(r, p) for v in part_views], [slot] * 2,
                      name="exchange_layer1_start")
    ng1 = norm_gains[1:2] + ex1[4][0:1, 0:1]
    dx1, dx1b, dng1 = _rms_bwd(x1, ng1, dh1, dx2, name="rms1_bwd")

    dy0 = _mm_nn([dx1b], wo0_t, BF16, name="fox_out_proj_dx")
    dwo0 = _mm_tn(y0, [dx1b], BF16, name="fox_out_proj_dw")
    do0, dg0, doaug = _fox_post_bwd(dy0, o0, p0, H=H, name="fox_post_bwd")
    dq0, dc_row = _fox_dq(p0, kaug, qaug, do0, doaug, H=H, name="fox_attn_dq")
    dk0, dv0, dc_key = _fox_dkv(p0, kaug, qaug, do0, doaug, H=H, name="fox_attn_dkv")
    dfl_t, dbf = _fox_gate_bwd(dc_row.reshape(H, S), dc_key.reshape(H, S), fl_t, b_col, name="fox_gate_bwd")
    dfl_tb = dfl_t.astype(BF16)
    dwfl_t = _mm_nn([dfl_tb], h0, BF16, name="fox_forget_proj_dw")
    segs0 = [dq0, dk0, dv0, dg0]
    dwf_main = _mm_tn(h0, segs0, BF16, name="fox_in_proj_dw")
    dwf = jnp.concatenate([dwf_main[:, :3 * W], dwfl_t.T, dwf_main[:, 3 * W:]], axis=1)
    dwf_blocks = jnp.transpose(dwf.reshape(D, N_DEV, cf), (1, 0, 2))
    ex0 = _send_start([dwf_blocks, dwo0], [SDS((N_DEV, D, cf), BF16), SDS((N_DEV, ro, D), BF16)],
                      [lambda r, me, p, m: r.at[p], lambda r, me, p, m: row(ro)(r, p)], [slot] * 2,
                      name="exchange_layer0_start")
    wfl_t0 = wfl_t + ex0[4][0:1, 0:1].astype(BF16)
    dh0_f = _mm_nn([dfl_tb.T], wfl_t0, BF16, name="fox_forget_proj_dx")
    dh0 = _mm_nn(segs0, wf_main_t, BF16, residual=dh0_f, name="fox_in_proj_dx")
    grad_x, _, dng0 = _rms_bwd(x2, norm_gains[0:1], dh0, dx1, name="rms0_bwd")

    own1 = [lambda src, land, me, v=v: (v(src, me), land.at[0]) for v in part_views]
    rh, ro1 = _send_wait(ex1, [lambda r, me, p, m, v=v: v(r, p) for v in part_views], [slot] * 2, own1, grad_x,
                         name="exchange_layer1_wait")
    rf, ro0 = _send_wait(ex0, [lambda r, me, p, m: r.at[p], lambda r, me, p, m: row(ro)(r, p)], [slot] * 2,
                         [lambda src, land, me: (src.at[me], land.at[0]),
                          lambda src, land, me: (row(ro)(src, me), land.at[0])], grad_x,
                         name="exchange_layer0_wait")

    pad = lambda a: jnp.pad(a, ((0, 0), (0, W - a.shape[1])))
    stats = jnp.concatenate([dng0, dng1, dlb, dgf, pad(dbf.reshape(1, H)), donorm,
                             jnp.zeros((2, W), F32)], axis=0)
    assert D == W
    (stats_all,) = _all_gather([stats], [SDS((N_DEV, STAT_ROWS, W), F32)], [lambda r, p: r.at[p]],
                               name="gather_small_grads")
    g_small = _stats_reduce(stats_all, hgrn_lb_logits, name="reduce_small_grads")
    me = 4 * lax.axis_index("x") + 2 * lax.axis_index("y") + lax.axis_index("c")
    g_onorm = lax.dynamic_slice_in_dim(g_small[6:7], me * co, co, axis=1)

    def upd(w, m, v, parts, name):
        shp = w.shape
        r2 = (-1, shp[-1])
        g, d, mn, vn = _adamw(w.reshape(r2), m.reshape(r2), v.reshape(r2), parts, name=name)
        return g.reshape(shp), d.reshape(shp), mn.reshape(shp), vn.reshape(shp)

    res = {
        "norm_gains": upd(norm_gains, m_norm_gains, v_norm_gains, [g_small[None, 0:2]], "adamw_norm_gains"),
        "fox_w_in": upd(fox_w_in, m_fox_w_in, v_fox_w_in, [rf], "adamw_fox_w_in"),
        "fox_b_f": upd(fox_b_f, m_fox_b_f, v_fox_b_f, [g_small[None, 5:6, :H]], "adamw_fox_b_f"),
        "hgrn_w_in": upd(hgrn_w_in, m_hgrn_w_in, v_hgrn_w_in, [rh], "adamw_hgrn_w_in"),
        "hgrn_lb_logits": upd(hgrn_lb_logits, m_hgrn_lb_logits, v_hgrn_lb_logits, [g_small[None, 2:4]],
                              "adamw_hgrn_lb_logits"),
        "hgrn_onorm": upd(hgrn_onorm, m_hgrn_onorm, v_hgrn_onorm, [g_onorm[None]], "adamw_hgrn_onorm"),
        "w_out": upd(w_out, m_w_out, v_w_out, [ro0, ro1], "adamw_w_out"),
        "final_gain": upd(final_gain.reshape(1, D), m_final_gain.reshape(1, D), v_final_gain.reshape(1, D),
                          [g_small[None, 4:5]], "adamw_final_gain"),
    }
    order = ["norm_gains", "fox_w_in", "fox_b_f", "hgrn_w_in", "hgrn_lb_logits", "hgrn_onorm", "w_out", "final_gain"]
    fix = lambda n, a: a.reshape(D) if n == "final_gain" else a
    outs = [loss, grad_x.reshape(1, S, D)]
    for k in range(4):
        outs += [fix(n, res[n][k]) for n in order]
    return tuple(outs)
```

```python
import functools

import numpy as np
import jax
import jax.numpy as jnp
from jax import lax
from jax.experimental import pallas as pl
from jax.experimental.pallas import tpu as pltpu

F32 = jnp.float32
BF16 = jnp.bfloat16
SDS = jax.ShapeDtypeStruct
MESH = pl.DeviceIdType.MESH

EPS = 1e-6
ADAM_LR, ADAM_B1, ADAM_B2, ADAM_EPS, ADAM_WD, ADAM_STEP = 0.001, 0.9, 0.999, 1e-08, 0.01, 10

N_DEV = 8
FOX_HEADS = 16
HGRN_HEADS = 16
HEAD_DIM = 128
HGRN_CHUNK = 128
HGRN_LEAF = 16
HGRN_HEADS_PER_STEP = 4
EXP_CLAMP = 85.0
ATT_BLOCK = 512
ATT_HEADS_PER_STEP = 2
POST_HEADS_PER_STEP = 4
NEG = -1e30
LOG2E = 1.4426950408889634
LN2 = 0.6931471805599453

VMEM_LIMIT_V7X = 56 * 1024 * 1024


def _params(*sem):
    return pltpu.CompilerParams(dimension_semantics=sem, vmem_limit_bytes=VMEM_LIMIT_V7X)


def _silu(x):
    return x * jax.nn.sigmoid(x)


def _dsilu(x):
    s = jax.nn.sigmoid(x)
    return s * (1.0 + x * (1.0 - s))


def _dot(a, b):
    return jnp.dot(a, b, preferred_element_type=F32)


def _dot_nt(a, b):
    return lax.dot_general(a, b, (((1,), (1,)), ((), ())), preferred_element_type=F32)


def _dot_tn(a, b):
    return lax.dot_general(a, b, (((0,), (0,)), ((), ())), preferred_element_type=F32)


def _mm_nn(a_list, b, out_dtype, *, name, residual=None, scale_cols=None, tm=512, tn=1024, tk=2048):
    ns = len(a_list)
    M, Ks = a_list[0].shape
    K, N = b.shape
    assert K == ns * Ks and all(a.shape == (M, Ks) for a in a_list)
    tm, tn, tk = min(tm, M), min(tn, N), min(tk, Ks)
    assert M % tm == 0 and N % tn == 0 and Ks % tk == 0
    assert scale_cols is None or scale_cols[0] % tn == 0
    nks = Ks // tk
    nk = ns * nks
    has_res = residual is not None

    def body(*refs):
        a_refs, b_ref = refs[:ns], refs[ns]
        res_ref = refs[ns + 1] if has_res else None
        o_ref = refs[ns + 1 + has_res]

        def finish(r):
            if has_res:
                r = r + res_ref[...].astype(F32)
            if scale_cols is not None:
                r = r * jnp.where(pl.program_id(1) < scale_cols[0] // tn, scale_cols[1], 1.0)
            o_ref[...] = r.astype(out_dtype)

        if nk == 1:
            finish(_dot(a_refs[0][...], b_ref[...]))
            return
        acc_ref = refs[ns + 2 + has_res]
        k = pl.program_id(2)

        @pl.when(k == 0)
        def _():
            acc_ref[...] = jnp.zeros_like(acc_ref)

        for s in range(ns):
            def step(s=s):
                acc_ref[...] += _dot(a_refs[s][...], b_ref[...])

            if ns == 1:
                step()
            else:
                pl.when(k // nks == s)(step)

        @pl.when(k == nk - 1)
        def _():
            finish(acc_ref[...])

    def a_map(i, j, k, s):
        return (i, jnp.clip(k - s * nks, 0, nks - 1))

    in_specs = [pl.BlockSpec((tm, tk), functools.partial(a_map, s=s)) for s in range(ns)]
    in_specs.append(pl.BlockSpec((tk, tn), lambda i, j, k: (k, j)))
    args = list(a_list) + [b]
    if has_res:
        in_specs.append(pl.BlockSpec((tm, tn), lambda i, j, k: (i, j)))
        args.append(residual)
    return pl.pallas_call(
        body, grid=(M // tm, N // tn, nk), in_specs=in_specs,
        out_specs=pl.BlockSpec((tm, tn), lambda i, j, k: (i, j)),
        out_shape=SDS((M, N), out_dtype),
        scratch_shapes=[] if nk == 1 else [pltpu.VMEM((tm, tn), F32)],
        compiler_params=_params("parallel", "parallel", "arbitrary"), name=name,
    )(*args)


def _mm_tn(a, b_list, out_dtype, *, name, tm=1024, tn=2048, tk=512):
    ns = len(b_list)
    S, M = a.shape
    Ns = b_list[0].shape[1]
    assert all(b.shape == (S, Ns) for b in b_list)
    tm, tn, tk = min(tm, M), min(tn, Ns), min(tk, S)
    assert M % tm == 0 and Ns % tn == 0 and S % tk == 0
    njs = Ns // tn
    nk = S // tk

    def body(*refs):
        a_ref, b_refs, o_ref, acc_ref = refs[0], refs[1:1 + ns], refs[1 + ns], refs[2 + ns]
        j, k = pl.program_id(1), pl.program_id(2)

        @pl.when(k == 0)
        def _():
            acc_ref[...] = jnp.zeros_like(acc_ref)

        for s in range(ns):
            def step(s=s):
                acc_ref[...] += _dot_tn(a_ref[...], b_refs[s][...])

            if ns == 1:
                step()
            else:
                pl.when(j // njs == s)(step)

        @pl.when(k == nk - 1)
        def _():
            o_ref[...] = acc_ref[...].astype(out_dtype)

    def b_map(i, j, k, s):
        return (k, jnp.clip(j - s * njs, 0, njs - 1))

    in_specs = [pl.BlockSpec((tk, tm), lambda i, j, k: (k, i))]
    in_specs += [pl.BlockSpec((tk, tn), functools.partial(b_map, s=s)) for s in range(ns)]
    return pl.pallas_call(
        body, grid=(M // tm, ns * njs, nk), in_specs=in_specs,
        out_specs=pl.BlockSpec((tm, tn), lambda i, j, k: (i, j)),
        out_shape=SDS((M, ns * Ns), out_dtype),
        scratch_shapes=[pltpu.VMEM((tm, tn), F32)],
        compiler_params=_params("parallel", "parallel", "arbitrary"), name=name,
    )(a, *b_list)


def _mm_nt_rows(w_t, h, *, name, tn=1024):
    R, K = w_t.shape
    S = h.shape[0]
    tn = min(tn, S)

    def body(w_ref, h_ref, o_ref):
        o_ref[...] = _dot_nt(w_ref[...], h_ref[...])

    return pl.pallas_call(
        body, grid=(S // tn,),
        in_specs=[pl.BlockSpec((R, K), lambda i: (0, 0)), pl.BlockSpec((tn, K), lambda i: (i, 0))],
        out_specs=pl.BlockSpec((R, tn), lambda i: (0, i)),
        out_shape=SDS((R, S), F32), compiler_params=_params("parallel"), name=name,
    )(w_t, h)


def _rms_fwd(x, gain, *, name, tm=512):
    S, D = x.shape
    tm = min(tm, S)

    def body(x_ref, g_ref, h_ref):
        xv = x_ref[...]
        r = lax.rsqrt(jnp.mean(xv * xv, axis=-1, keepdims=True) + EPS)
        h_ref[...] = ((xv * r) * g_ref[...]).astype(BF16)

    return pl.pallas_call(
        body, grid=(S // tm,),
        in_specs=[pl.BlockSpec((tm, D), lambda i: (i, 0)), pl.BlockSpec((1, D), lambda i: (0, 0))],
        out_specs=pl.BlockSpec((tm, D), lambda i: (i, 0)),
        out_shape=SDS((S, D), BF16), compiler_params=_params("parallel"), name=name,
    )(x, gain)


def _rms_bwd(x, gain, dh, dres, *, name, tm=256):
    S, D = x.shape
    tm = min(tm, S)

    def body(x_ref, g_ref, dh_ref, dres_ref, dx_ref, dxb_ref, dg_ref):
        @pl.when(pl.program_id(0) == 0)
        def _():
            dg_ref[...] = jnp.zeros_like(dg_ref)

        xv = x_ref[...]
        r = lax.rsqrt(jnp.mean(xv * xv, axis=-1, keepdims=True) + EPS)
        xh = xv * r
        dhv = dh_ref[...].astype(F32)
        dg_ref[...] += jnp.sum(dhv * xh, axis=0, keepdims=True)
        dxh = dhv * g_ref[...]
        dx = r * (dxh - xh * jnp.mean(dxh * xh, axis=-1, keepdims=True)) + dres_ref[...]
        dx_ref[...] = dx
        dxb_ref[...] = dx.astype(BF16)

    row = pl.BlockSpec((tm, D), lambda i: (i, 0))
    vec = pl.BlockSpec((1, D), lambda i: (0, 0))
    return pl.pallas_call(
        body, grid=(S // tm,), in_specs=[row, vec, row, row], out_specs=[row, row, vec],
        out_shape=[SDS((S, D), F32), SDS((S, D), BF16), SDS((1, D), F32)],
        compiler_params=_params("arbitrary"), name=name,
    )(x, gain, dh, dres)


def _loss_head(x, gain, target, *, name, tm=256):
    S, D = x.shape
    tm = min(tm, S)
    assert tm % 8 == 0 and D % 128 == 0

    def body(x_ref, g_ref, t_ref, dx_ref, dxb_ref, loss_ref, dg_ref):
        @pl.when(pl.program_id(0) == 0)
        def _():
            dg_ref[...] = jnp.zeros_like(dg_ref)
            loss_ref[...] = jnp.zeros_like(loss_ref)

        xv = x_ref[...]
        g = g_ref[...]
        r = lax.rsqrt(jnp.mean(xv * xv, axis=-1, keepdims=True) + EPS)
        xh = xv * r
        err = xh * g - t_ref[...]
        e2 = (err * err).reshape(tm // 8, 8, D).sum(axis=0)
        part = e2[:, 0:128]
        for k in range(1, D // 128):
            part = part + e2[:, k * 128:(k + 1) * 128]
        loss_ref[...] += part * (0.5 / D)
        dy = err * (1.0 / D)
        dg_ref[...] += jnp.sum(dy * xh, axis=0, keepdims=True)
        dxh = dy * g
        dx = r * (dxh - xh * jnp.mean(dxh * xh, axis=-1, keepdims=True))
        dx_ref[...] = dx
        dxb_ref[...] = dx.astype(BF16)

    row = pl.BlockSpec((tm, D), lambda i: (i, 0))
    vec = pl.BlockSpec((1, D), lambda i: (0, 0))
    return pl.pallas_call(
        body, grid=(S // tm,), in_specs=[row, vec, row],
        out_specs=[row, row, pl.BlockSpec((8, 128), lambda i: (0, 0)), vec],
        out_shape=[SDS((S, D), F32), SDS((S, D), BF16), SDS((8, 128), F32), SDS((1, D), F32)],
        compiler_params=_params("arbitrary"), name=name,
    )(x, gain, target)


def _split3(x):
    hi = x.astype(BF16)
    r1 = x - hi.astype(F32)
    mid = r1.astype(BF16)
    lo = (r1 - mid.astype(F32)).astype(BF16)
    return hi, mid, lo


def _split2(x):
    hi = x.astype(BF16)
    lo = (x - hi.astype(F32)).astype(BF16)
    return hi, lo


def _fox_gate_fwd(fl_t, b_col, *, name):
    H, S = fl_t.shape
    L = 128
    tri = jnp.asarray(np.triu(np.ones((L, L), np.float32)), BF16)

    def body(fl_ref, b_ref, tri_ref, hi_ref, mid_ref, lo_ref, carry):
        @pl.when(pl.program_id(0) == 0)
        def _():
            carry[...] = jnp.zeros_like(carry)

        z = fl_ref[...] + b_ref[...]
        lf = jnp.minimum(z, 0.0) - jnp.log(1.0 + jnp.exp(-jnp.abs(z)))
        hi, mid, lo = _split3(lf)
        t = tri_ref[...]
        c = (_dot(hi, t) + _dot(mid, t)) + _dot(lo, t) + carry[...]
        carry[...] = c[:, L - 1:L]
        hi_ref[...], mid_ref[...], lo_ref[...] = _split3(c * (-LOG2E))

    blk = pl.BlockSpec((H, L), lambda i: (0, i))
    return pl.pallas_call(
        body, grid=(S // L,),
        in_specs=[blk, pl.BlockSpec((H, 1), lambda i: (0, 0)), pl.BlockSpec((L, L), lambda i: (0, 0))],
        out_specs=[blk] * 3, out_shape=[SDS((H, S), BF16)] * 3, scratch_shapes=[pltpu.VMEM((H, 1), F32)],
        compiler_params=_params("arbitrary"), name=name,
    )(fl_t, b_col, tri)


def _fox_gate_bwd(dc_row, dc_key, fl_t, b_col, *, name):
    H, S = fl_t.shape
    L = 128
    n = S // L
    tri = jnp.asarray(np.tril(np.ones((L, L), np.float32)), BF16)

    def body(dcr_ref, dck_ref, fl_ref, b_ref, tri_ref, dfl_ref, db_ref, carry):
        @pl.when(pl.program_id(0) == 0)
        def _():
            carry[...] = jnp.zeros_like(carry)
            db_ref[...] = jnp.zeros_like(db_ref)

        hi, mid, lo = _split3(dcr_ref[...] + dck_ref[...])
        t = tri_ref[...]
        dlf = (_dot(hi, t) + _dot(mid, t)) + _dot(lo, t) + carry[...]
        carry[...] = dlf[:, 0:1]
        z = fl_ref[...] + b_ref[...]
        dfl = dlf * jax.nn.sigmoid(-z)
        dfl_ref[...] = dfl
        db_ref[...] += jnp.sum(dfl, axis=1, keepdims=True)

    blk = pl.BlockSpec((H, L), lambda i: (0, n - 1 - i))
    col = pl.BlockSpec((H, 1), lambda i: (0, 0))
    return pl.pallas_call(
        body, grid=(n,), in_specs=[blk, blk, blk, col, pl.BlockSpec((L, L), lambda i: (0, 0))],
        out_specs=[blk, col], out_shape=[SDS((H, S), F32), SDS((H, 1), F32)],
        scratch_shapes=[pltpu.VMEM((H, 1), F32)], compiler_params=_params("arbitrary"), name=name,
    )(dc_row, dc_key, fl_t, b_col, tri)


AUG = HEAD_DIM


def _lane_select(cols, shape):
    lane = lax.broadcasted_iota(jnp.int32, shape, 1)
    out = jnp.zeros(shape, BF16)
    for k, c in reversed(list(enumerate(cols))):
        c = jnp.full(shape, c, BF16) if isinstance(c, (int, float)) else jnp.broadcast_to(c, shape).astype(BF16)
        out = jnp.where(lane == k, c, out)
    return out


def _fox_key_aug(b_hi, b_mid, b_lo):
    H, S = b_hi.shape
    ones = jnp.ones((H, S), BF16)
    ka = jnp.stack([b_hi, b_mid, b_lo, ones, ones, ones], axis=-1)
    ka = jnp.pad(ka, ((0, 0), (0, 0), (0, AUG - 6)))
    return jnp.transpose(ka, (1, 0, 2)).reshape(S, H * AUG)


def _fox_fwd(p0, kaug, *, H, name):
    S = p0.shape[0]
    T = min(ATT_BLOCK, S)
    nq = S // T
    dh = HEAD_DIM
    G = ATT_HEADS_PER_STEP
    assert H % G == 0

    def body(q_ref, k_ref, ka_ref, v_ref, g_ref, o_ref, y_ref, qa_ref, m_sc, acc_sc):
        i = pl.program_id(1)
        qaug = _lane_select([1.0, 1.0, 1.0], (T, AUG))
        ones = jnp.ones((T, dh), BF16)
        m_sc[...] = jnp.full_like(m_sc, NEG)
        acc_sc[...] = jnp.zeros_like(acc_sc)

        def step(j, masked):
            rows = pl.ds(pl.multiple_of(j * T, T), T)
            for g in range(G):
                hd = slice(g * dh, (g + 1) * dh)
                q = jnp.concatenate([q_ref[:, hd], qaug], axis=1)
                kj = jnp.concatenate([k_ref[rows, hd], ka_ref[rows, hd]], axis=1)
                vj = jnp.concatenate([v_ref[rows, hd], ones], axis=1)
                t = _dot_nt(q, kj)
                if masked:
                    row = lax.broadcasted_iota(jnp.int32, (T, T), 0)
                    col = lax.broadcasted_iota(jnp.int32, (T, T), 1)
                    t = jnp.where(row >= col, t, NEG)
                m_prev = m_sc[g]
                m_new = jnp.maximum(m_prev, jnp.max(t, axis=-1, keepdims=True))
                p = jnp.exp2(t - jnp.tile(m_new, (1, T // 128)))
                alpha = jnp.exp2(m_prev - m_new)
                acc_sc[g] = jnp.tile(alpha, (1, 2)) * acc_sc[g] + _dot(p.astype(BF16), vj)
                m_sc[g] = m_new

        def loop_body(j, carry):
            step(j, False)
            return carry

        lax.fori_loop(0, i, loop_body, 0)
        step(i, True)
        for g in range(G):
            hd = slice(g * dh, (g + 1) * dh)
            l = acc_sc[g, :, dh:]
            o = acc_sc[g, :, :dh] / l
            o_ref[:, hd] = o
            y_ref[:, hd] = (o * _silu(g_ref[:, hd].astype(F32))).astype(BF16)
            hi, mid, lo = _split3(-(m_sc[g] + jnp.log2(l)))
            qa_ref[:, hd] = _lane_select([1.0, 1.0, 1.0, hi, mid, lo], (T, AUG))

    blk = lambda off: pl.BlockSpec((T, G * dh), lambda h, i: (i, off // G + h))
    full = lambda off: pl.BlockSpec((S, G * dh), lambda h, i: (0, off // G + h))
    return pl.pallas_call(
        body, grid=(H // G, nq),
        in_specs=[blk(0), full(H), full(0), full(2 * H), blk(3 * H)],
        out_specs=[blk(0), blk(0), blk(0)],
        out_shape=[SDS((S, H * dh), F32), SDS((S, H * dh), BF16), SDS((S, H * AUG), BF16)],
        scratch_shapes=[pltpu.VMEM((G, T, 128), F32), pltpu.VMEM((G, T, 2 * dh), F32)],
        compiler_params=_params("parallel", "arbitrary"), name=name,
    )(p0, p0, kaug, p0, p0)


def _fox_post_bwd(dy, o, p0, *, H, name, tm=512):
    S = dy.shape[0]
    dh = HEAD_DIM
    tm = min(tm, S)
    G = POST_HEADS_PER_STEP
    assert H % G == 0

    def body(dy_ref, o_ref, g_ref, do_ref, dg_ref, da_ref):
        dyv = dy_ref[...].astype(F32)
        ov = o_ref[...]
        g = g_ref[...].astype(F32)
        do = (dyv * _silu(g)).astype(BF16)
        do_ref[...] = do
        dg_ref[...] = (dyv * ov * _dsilu(g)).astype(BF16)
        prod = do.astype(F32) * ov
        for k in range(G):
            hd = slice(k * dh, (k + 1) * dh)
            delta = jnp.sum(prod[:, hd], axis=-1, keepdims=True)
            hi, mid, lo = _split3(-jnp.broadcast_to(delta, (tm, AUG)))
            da_ref[:, hd] = _lane_select([hi, mid, lo], (tm, AUG))

    blk = pl.BlockSpec((tm, G * dh), lambda h, i: (i, h))
    return pl.pallas_call(
        body, grid=(H // G, S // tm),
        in_specs=[blk, blk, pl.BlockSpec((tm, G * dh), lambda h, i: (i, 3 * H // G + h))],
        out_specs=[blk, blk, blk],
        out_shape=[SDS((S, H * dh), BF16), SDS((S, H * dh), BF16), SDS((S, H * AUG), BF16)],
        compiler_params=_params("parallel", "parallel"), name=name,
    )(dy, o, p0)


def _fox_dq(p0, kaug, qaug, do, doaug, *, H, name):
    S = p0.shape[0]
    T = min(ATT_BLOCK, S)
    nq = S // T
    dh = HEAD_DIM
    scale = dh ** -0.5
    G = ATT_HEADS_PER_STEP
    assert H % G == 0

    def body(q_ref, qa_ref, k_ref, ka_ref, v_ref, do_ref, da_ref, dq_ref, rs_ref, acc_sc):
        i = pl.program_id(1)
        vaug = _lane_select([1.0, 1.0, 1.0], (T, AUG))
        ones = jnp.ones((T, dh), BF16)
        acc_sc[...] = jnp.zeros_like(acc_sc)

        def step(j, masked):
            rows = pl.ds(pl.multiple_of(j * T, T), T)
            for g in range(G):
                hd = slice(g * dh, (g + 1) * dh)
                q = jnp.concatenate([q_ref[:, hd], qa_ref[:, hd]], axis=1)
                do = jnp.concatenate([do_ref[:, hd], da_ref[:, hd]], axis=1)
                k = k_ref[rows, hd]
                p = jnp.exp2(_dot_nt(q, jnp.concatenate([k, ka_ref[rows, hd]], axis=1)))
                if masked:
                    row = lax.broadcasted_iota(jnp.int32, (T, T), 0)
                    col = lax.broadcasted_iota(jnp.int32, (T, T), 1)
                    p = jnp.where(row >= col, p, 0.0)
                ds = p * _dot_nt(do, jnp.concatenate([v_ref[rows, hd], vaug], axis=1))
                acc_sc[g] += _dot(ds.astype(BF16), jnp.concatenate([k, ones], axis=1))

        def loop_body(j, carry):
            step(j, False)
            return carry

        lax.fori_loop(0, i, loop_body, 0)
        step(i, True)
        for g in range(G):
            dq_ref[:, g * dh:(g + 1) * dh] = (acc_sc[g, :, :dh] * scale).astype(BF16)
            rs_ref[g] = acc_sc[g, :, dh:dh + 1]

    blk = lambda off: pl.BlockSpec((T, G * dh), lambda h, i: (i, off // G + h))
    full = lambda off: pl.BlockSpec((S, G * dh), lambda h, i: (0, off // G + h))
    return pl.pallas_call(
        body, grid=(H // G, nq),
        in_specs=[blk(0), blk(0), full(H), full(0), full(2 * H), blk(0), blk(0)],
        out_specs=[blk(0), pl.BlockSpec((G, T, 1), lambda h, i: (h, i, 0))],
        out_shape=[SDS((S, H * dh), BF16), SDS((H, S, 1), F32)],
        scratch_shapes=[pltpu.VMEM((G, T, 2 * dh), F32)],
        compiler_params=_params("parallel", "arbitrary"), name=name,
    )(p0, qaug, p0, kaug, p0, do, doaug)


def _fox_dkv(p0, kaug, qaug, do, doaug, *, H, name):
    S = p0.shape[0]
    T = min(ATT_BLOCK, S)
    nq = S // T
    dh = HEAD_DIM
    G = ATT_HEADS_PER_STEP
    assert H % G == 0

    def body(q_ref, qa_ref, k_ref, ka_ref, v_ref, do_ref, da_ref, dk_ref, dv_ref, dc_ref, dk_sc, dv_sc):
        j = pl.program_id(1)
        vaug = _lane_select([1.0, 1.0, 1.0], (T, AUG))
        ones = jnp.ones((T, dh), BF16)
        dk_sc[...] = jnp.zeros_like(dk_sc)
        dv_sc[...] = jnp.zeros_like(dv_sc)

        def step(i, masked):
            rows = pl.ds(pl.multiple_of(i * T, T), T)
            for g in range(G):
                hd = slice(g * dh, (g + 1) * dh)
                k = jnp.concatenate([k_ref[:, hd], ka_ref[:, hd]], axis=1)
                v = jnp.concatenate([v_ref[:, hd], vaug], axis=1)
                qi = q_ref[rows, hd]
                doi = do_ref[rows, hd]
                pt = jnp.exp2(_dot_nt(k, jnp.concatenate([qi, qa_ref[rows, hd]], axis=1)))
                if masked:
                    row = lax.broadcasted_iota(jnp.int32, (T, T), 0)
                    col = lax.broadcasted_iota(jnp.int32, (T, T), 1)
                    pt = jnp.where(col >= row, pt, 0.0)
                dv_sc[g] += _dot(pt.astype(BF16), doi)
                dst = pt * _dot_nt(v, jnp.concatenate([doi, da_ref[rows, hd]], axis=1))
                dk_sc[g] += _dot(dst.astype(BF16), jnp.concatenate([qi, ones], axis=1))

        step(j, True)

        def loop_body(i, carry):
            step(i, False)
            return carry

        lax.fori_loop(j + 1, nq, loop_body, 0)
        for g in range(G):
            hd = slice(g * dh, (g + 1) * dh)
            dk_ref[:, hd] = (dk_sc[g, :, :dh] * LN2).astype(BF16)
            dv_ref[:, hd] = dv_sc[g].astype(BF16)
            dc_ref[g] = -dk_sc[g, :, dh:dh + 1]

    blk = lambda off: pl.BlockSpec((T, G * dh), lambda h, j: (j, off // G + h))
    full = lambda off: pl.BlockSpec((S, G * dh), lambda h, j: (0, off // G + h))
    return pl.pallas_call(
        body, grid=(H // G, nq),
        in_specs=[full(0), full(0), blk(H), blk(0), blk(2 * H), full(0), full(0)],
        out_specs=[blk(0), blk(0), pl.BlockSpec((G, T, 1), lambda h, j: (h, j, 0))],
        out_shape=[SDS((S, H * dh), BF16), SDS((S, H * dh), BF16), SDS((H, S, 1), F32)],
        scratch_shapes=[pltpu.VMEM((G, T, 2 * dh), F32), pltpu.VMEM((G, T, dh), F32)],
        compiler_params=_params("parallel", "arbitrary"), name=name,
    )(p0, qaug, p0, kaug, p0, do, doaug)


def _hgrn_levels(C, leaf):
    levels = []
    h = C // 2
    while h >= leaf:
        levels.append(h)
        h //= 2
    return levels


def _hgrn_sum_matrix(C, leaf):
    t = np.arange(C)[:, None]
    u = np.arange(C)[None, :]
    mats = [(u <= t), (u > t)]
    for h in _hgrn_levels(C, leaf):
        start = (t // (2 * h)) * (2 * h)
        mid = start + h - 1
        second = t > mid
        m = np.where(second, (u > mid) & (u <= t), (u > t) & (u <= mid))
        mats.append(m)
    lstart = (t // leaf) * leaf
    mats.append((u >= lstart) & (u <= t))
    return np.concatenate([m.astype(np.float32) for m in mats], axis=0)


def _hgrn_chunk_terms(qr, fz, lb, msum, C, leaf):
    levels = _hgrn_levels(C, leaf)
    sq = _silu(qr)
    t = jnp.exp(-jnp.abs(fz))
    r = 1.0 / (1.0 + t)
    pos = fz >= 0.0
    sp = jnp.where(pos, r, t * r)
    sn = jnp.where(pos, t * r, r)
    f = lb + (1.0 - lb) * sp
    lf = jnp.log(f)
    k = (1.0 - lb) * sn
    hi, lo = _split2(lf)
    dsum = _dot(msum, hi) + _dot(msum, lo)
    b = dsum[0:C]
    kdec = dsum[C:2 * C]
    rowi = lax.broadcasted_iota(jnp.int32, (C, 1), 0)
    lev = []
    for n, h in enumerate(levels):
        e = jnp.exp(dsum[(2 + n) * C:(3 + n) * C])
        second = (rowi % (2 * h)) >= h
        qm = jnp.where(second, sq * e, 0.0).astype(BF16)
        km = jnp.where(second, 0.0, k * e).astype(BF16)
        lev.append((h, e, second, qm, km))
    dleaf = dsum[(2 + len(levels)) * C:(3 + len(levels)) * C]
    eq = jnp.exp(dleaf)
    ek = jnp.exp(jnp.minimum(-dleaf, EXP_CLAMP))
    return dict(sq=sq, sp=sp, sn=sn, f=f, k=k, b=b, kdec=kdec, lev=lev, eq=eq, ek=ek,
                ql=(sq * eq).astype(BF16), kl=(k * ek).astype(BF16),
                qs=(sq * jnp.exp(b)).astype(BF16), ke=(k * jnp.exp(kdec)).astype(BF16),
                e_c=jnp.exp(b[C - 1:C, :]))


def _hgrn_masks(C, leaf, transposed):
    a = lax.broadcasted_iota(jnp.int32, (C, C), 0)
    bb = lax.broadcasted_iota(jnp.int32, (C, C), 1)
    t, s = (bb, a) if transposed else (a, bb)
    lev = [None if 2 * h == C else (t // (2 * h)) == (s // (2 * h)) for h in _hgrn_levels(C, leaf)]
    if leaf == C:
        leafm = s <= t
    else:
        leafm = ((t // leaf) == (s // leaf)) & (s <= t)
    return lev, leafm


def _hgrn_fwd(p1, f1, lb, onorm, *, H, name, tb=512):
    S = p1.shape[0]
    dk = HEAD_DIM
    C = min(HGRN_CHUNK, S)
    leaf = min(HGRN_LEAF, C)
    tb = min(tb, S)
    nc = tb // C
    G = HGRN_HEADS_PER_STEP
    assert H % G == 0
    msum = jnp.asarray(_hgrn_sum_matrix(C, leaf), BF16)

    def body(q_ref, f_ref, v_ref, g_ref, lb_ref, on_ref, ms_ref, o_ref, y_ref, st_ref, st_sc):
        @pl.when(pl.program_id(1) == 0)
        def _():
            st_sc[...] = jnp.zeros_like(st_sc)

        msv = ms_ref[...]
        lmask, leafm = _hgrn_masks(C, leaf, False)

        def chunk(n, carry):
            rows = pl.ds(pl.multiple_of(n * C, C), C)
            for g in range(G):
                hd = slice(g * dk, (g + 1) * dk)
                tm = _hgrn_chunk_terms(q_ref[rows, hd].astype(F32), f_ref[rows, hd], lb_ref[:, hd], msv, C, leaf)
                v = v_ref[rows, hd]
                st = st_sc[g]
                st_ref[g, n] = st
                a = jnp.where(leafm, _dot_nt(tm["ql"], tm["kl"]), 0.0)
                for (h, e, second, qm, km), m in zip(tm["lev"], lmask):
                    al = _dot_nt(qm, km)
                    a = a + (al if m is None else jnp.where(m, al, 0.0))
                o = _dot_nt(tm["qs"], st.astype(BF16)) + _dot(a.astype(BF16), v)
                st_sc[g] = st * tm["e_c"] + _dot(v.T, tm["ke"])
                o_ref[rows, hd] = o
                rn = lax.rsqrt(jnp.mean(o * o, axis=-1, keepdims=True) + EPS)
                y = ((o * rn) * on_ref[:, hd]) * _silu(g_ref[rows, hd].astype(F32))
                y_ref[rows, hd] = y.astype(BF16)
            return carry

        lax.fori_loop(0, nc, chunk, 0)

    blk = lambda off: pl.BlockSpec((tb, G * dk), lambda h, i: (i, off // G + h))
    vec = pl.BlockSpec((1, G * dk), lambda h, i: (0, h))
    return pl.pallas_call(
        body, grid=(H // G, S // tb),
        in_specs=[blk(0), blk(0), blk(H), blk(2 * H), vec, vec,
                  pl.BlockSpec(msum.shape, lambda h, i: (0, 0))],
        out_specs=[blk(0), blk(0), pl.BlockSpec((G, nc, dk, dk), lambda h, i: (h, i, 0, 0))],
        out_shape=[SDS((S, H * dk), F32), SDS((S, H * dk), BF16), SDS((H, S // C, dk, dk), F32)],
        scratch_shapes=[pltpu.VMEM((G, dk, dk), F32)],
        compiler_params=_params("parallel", "arbitrary"), name=name,
    )(p1, f1, p1, p1, lb, onorm, msum)


def _hgrn_post_bwd(dy, o, p1, onorm, *, H, name, tm=512):
    S = dy.shape[0]
    dk = HEAD_DIM
    tm = min(tm, S)
    G = POST_HEADS_PER_STEP
    assert H % G == 0

    def body(dy_ref, o_ref, g_ref, on_ref, do_ref, dg_ref, don_ref):
        @pl.when(pl.program_id(1) == 0)
        def _():
            don_ref[...] = jnp.zeros_like(don_ref)

        for k in range(G):
            hd = slice(k * dk, (k + 1) * dk)
            dyv = dy_ref[:, hd].astype(F32)
            ov = o_ref[:, hd]
            g = g_ref[:, hd].astype(F32)
            onv = on_ref[:, hd]
            rn = lax.rsqrt(jnp.mean(ov * ov, axis=-1, keepdims=True) + EPS)
            oh = ov * rn
            dn = dyv * _silu(g)
            dg_ref[:, hd] = (dyv * (oh * onv) * _dsilu(g)).astype(BF16)
            don_ref[:, hd] += jnp.sum(dn * oh, axis=0, keepdims=True)
            doh = dn * onv
            do_ref[:, hd] = (rn * (doh - oh * jnp.mean(doh * oh, axis=-1, keepdims=True))).astype(BF16)

    blk = pl.BlockSpec((tm, G * dk), lambda h, i: (i, h))
    vec = pl.BlockSpec((1, G * dk), lambda h, i: (0, h))
    return pl.pallas_call(
        body, grid=(H // G, S // tm),
        in_specs=[blk, blk, pl.BlockSpec((tm, G * dk), lambda h, i: (i, 2 * H // G + h)), vec],
        out_specs=[blk, blk, vec],
        out_shape=[SDS((S, H * dk), BF16), SDS((S, H * dk), BF16), SDS((1, H * dk), F32)],
        compiler_params=_params("parallel", "arbitrary"), name=name,
    )(dy, o, p1, onorm)


def _hgrn_bwd(p1, f1, lb, do, states, *, H, name, tb=512):
    S = p1.shape[0]
    dk = HEAD_DIM
    C = min(HGRN_CHUNK, S)
    leaf = min(HGRN_LEAF, C)
    tb = min(tb, S)
    nc = tb // C
    nb = S // tb
    G = HGRN_HEADS_PER_STEP
    assert H % G == 0
    msum = jnp.asarray(_hgrn_sum_matrix(C, leaf), BF16)
    rtri = jnp.asarray(np.triu(np.ones((C, C), np.float32)), BF16)

    def body(q_ref, f_ref, v_ref, do_ref, st_ref, lb_ref, ms_ref, rt_ref,
             dq_ref, df_ref, dv_ref, dlb_ref, g_sc):
        @pl.when(pl.program_id(1) == 0)
        def _():
            g_sc[...] = jnp.zeros_like(g_sc)
            dlb_ref[...] = jnp.zeros_like(dlb_ref)

        msv = ms_ref[...]
        rtv = rt_ref[...]
        lmask, leafm = _hgrn_masks(C, leaf, False)
        lmask_t, leafm_t = _hgrn_masks(C, leaf, True)
        f32 = lambda z: z.astype(F32)

        def head_chunk(g, n):
            hd = slice(g * dk, (g + 1) * dk)
            rows = pl.ds(pl.multiple_of(n * C, C), C)
            lbv = lb_ref[:, hd]
            qr = q_ref[rows, hd].astype(F32)
            tm = _hgrn_chunk_terms(qr, f_ref[rows, hd], lbv, msv, C, leaf)
            v = v_ref[rows, hd]
            dov = do_ref[rows, hd]
            st0 = st_ref[g, n]
            gt = g_sc[g]
            gtb = gt.astype(BF16)
            da = _dot_nt(dov, v)
            da_t = _dot_nt(v, dov)

            dal = jnp.where(leafm, da, 0.0).astype(BF16)
            dal_t = jnp.where(leafm_t, da_t, 0.0).astype(BF16)
            dql = _dot(dal, tm["kl"])
            dkl = _dot(dal_t, tm["ql"])
            dsq = dql * tm["eq"]
            dkk = dkl * tm["ek"]
            xq = f32(tm["ql"]) * dql
            xk = f32(tm["kl"]) * dkl
            a_t = jnp.where(leafm_t, _dot_nt(tm["kl"], tm["ql"]), 0.0)
            for (h, e, second, qm, km), m, m_t in zip(tm["lev"], lmask, lmask_t):
                dl = (da if m is None else jnp.where(m, da, 0.0)).astype(BF16)
                dl_t = (da_t if m_t is None else jnp.where(m_t, da_t, 0.0)).astype(BF16)
                dqm = _dot(dl, km)
                dkm = _dot(dl_t, qm)
                dsq = dsq + jnp.where(second, dqm * e, 0.0)
                dkk = dkk + jnp.where(second, 0.0, dkm * e)
                xq = xq + f32(qm) * dqm
                xk = xk + f32(km) * dkm
                al_t = _dot_nt(km, qm)
                a_t = a_t + (al_t if m_t is None else jnp.where(m_t, al_t, 0.0))
            dqs = _dot(dov, st0.astype(BF16))
            dke = _dot(v, gtb)
            dsq = dsq + dqs * jnp.exp(tm["b"])
            dkk = dkk + dke * jnp.exp(tm["kdec"])
            xq = xq + f32(tm["qs"]) * dqs
            xk = xk + f32(tm["ke"]) * dke
            dvv = _dot(a_t.astype(BF16), dov) + _dot_nt(tm["ke"], gtb)
            r_end = jnp.sum(f32(gtb) * _dot(v.T, tm["ke"]) + gt * (st0 * tm["e_c"]), axis=0, keepdims=True)
            g_sc[g] = gt * tm["e_c"] + _dot(dov.T, tm["qs"])
            xh, xm, xl = _split3(xq - xk)
            dlf = (_dot(rtv, xh) + _dot(rtv, xm)) + _dot(rtv, xl) + r_end
            dlf_f = dlf / tm["f"]
            dsp = (1.0 - lbv) * (dlf_f - dkk)
            df_ref[rows, hd] = (dsp * (tm["sp"] * tm["sn"])).astype(BF16)
            dq_ref[rows, hd] = (dsq * _dsilu(qr)).astype(BF16)
            dv_ref[rows, hd] = dvv.astype(BF16)
            dlb_ref[:, hd] += jnp.sum(dlf_f * tm["sn"] - dkk * tm["sn"], axis=0, keepdims=True)

        def chunk(nn, carry):
            for g in range(G):
                head_chunk(g, nc - 1 - nn)
            return carry

        lax.fori_loop(0, nc, chunk, 0)

    blk = lambda off: pl.BlockSpec((tb, G * dk), lambda h, i: (nb - 1 - i, off // G + h))
    vec = pl.BlockSpec((1, G * dk), lambda h, i: (0, h))
    return pl.pallas_call(
        body, grid=(H // G, nb),
        in_specs=[blk(0), blk(0), blk(H), blk(0),
                  pl.BlockSpec((G, nc, dk, dk), lambda h, i: (h, nb - 1 - i, 0, 0)), vec,
                  pl.BlockSpec(msum.shape, lambda h, i: (0, 0)), pl.BlockSpec((C, C), lambda h, i: (0, 0))],
        out_specs=[blk(0), blk(0), blk(0), vec],
        out_shape=[SDS((S, H * dk), BF16)] * 3 + [SDS((1, H * dk), F32)],
        scratch_shapes=[pltpu.VMEM((G, dk, dk), F32)],
        compiler_params=_params("parallel", "arbitrary"), name=name,
    )(p1, f1, p1, do, states, lb, msum, rtri)


def _lb_fwd(logits, *, name):
    W = logits.shape[1]

    def body(l_ref, lb_ref):
        l = l_ref[...]
        m = jnp.max(l, axis=0, keepdims=True)
        e = jnp.exp(l - m)
        p = e / jnp.sum(e, axis=0, keepdims=True)
        lb_ref[...] = (p[0:1] + p[1:2]) - p[0:1]

    return pl.pallas_call(body, out_shape=SDS((1, W), F32), name=name)(logits)


STAT_ROWS = 8


def _stats_reduce(stats_all, logits, *, name):
    W = logits.shape[1]

    def body(s_ref, l_ref, g_ref):
        tot = s_ref[0]
        for d in range(1, N_DEV):
            tot = tot + s_ref[d]
        l = l_ref[...]
        m = jnp.max(l, axis=0, keepdims=True)
        e = jnp.exp(l - m)
        p = e / jnp.sum(e, axis=0, keepdims=True)
        dlb = tot[2:3]
        dl0 = -(p[0:1] * p[1:2]) * dlb
        dl1 = (p[1:2] * (1.0 - p[1:2])) * dlb
        g_ref[0:2] = tot[0:2]
        g_ref[2:3] = dl0
        g_ref[3:4] = dl1
        g_ref[4:7] = tot[3:6]
        g_ref[7:8] = jnp.zeros((1, W), F32)

    return pl.pallas_call(body, out_shape=SDS((STAT_ROWS, W), F32), name=name)(stats_all, logits)


def _adamw(w, m, v, g_parts, *, name, tr=128):
    R, C = w.shape
    ns = len(g_parts)
    n, Rs = g_parts[0].shape[0], g_parts[0].shape[1]
    assert all(p.shape == (n, Rs, C) for p in g_parts) and ns * Rs == R
    tr = min(tr, Rs)
    assert Rs % tr == 0
    nts = Rs // tr
    c1 = 1.0 / (1.0 - ADAM_B1 ** ADAM_STEP)
    c2 = 1.0 / (1.0 - ADAM_B2 ** ADAM_STEP)

    def body(*refs):
        w_ref, m_ref, v_ref = refs[:3]
        g_refs = refs[3:3 + ns]
        go_ref, d_ref, mo_ref, vo_ref = refs[3 + ns:]

        def update(g_ref):
            g = g_ref[0].astype(F32)
            for k in range(1, n):
                g = g + g_ref[k].astype(F32)
            mn = ADAM_B1 * m_ref[...] + (1.0 - ADAM_B1) * g
            vn = ADAM_B2 * v_ref[...] + (1.0 - ADAM_B2) * (g * g)
            d_ref[...] = -ADAM_LR * ((mn * c1) / (jnp.sqrt(vn * c2) + ADAM_EPS) + ADAM_WD * w_ref[...])
            go_ref[...] = g
            mo_ref[...] = mn
            vo_ref[...] = vn

        for s in range(ns):
            if ns == 1:
                update(g_refs[s])
            else:
                pl.when(pl.program_id(0) // nts == s)(functools.partial(update, g_refs[s]))

    def g_map(i, s):
        return (0, jnp.clip(i - s * nts, 0, nts - 1), 0)

    blk = pl.BlockSpec((tr, C), lambda i: (i, 0))
    return pl.pallas_call(
        body, grid=(R // tr,),
        in_specs=[blk, blk, blk] + [pl.BlockSpec((n, tr, C), functools.partial(g_map, s=s)) for s in range(ns)],
        out_specs=[blk] * 4, out_shape=[SDS((R, C), F32)] * 4,
        compiler_params=_params("parallel"), name=name,
    )(w, m, v, *g_parts)


ANY = pl.BlockSpec(memory_space=pl.ANY)


def _all_gather(shards, out_shapes, views, *, name):
    n = len(shards)

    def body(*refs):
        ins, outs = refs[:n], refs[n:2 * n]
        send_sems, recv_sems, local_sems = refs[2 * n:]
        x, y, c = lax.axis_index("x"), lax.axis_index("y"), lax.axis_index("c")
        me, sibling = (x, y, c), (x, y, 1 - c)
        chips = [(1 - x, y), (x, 1 - y), (1 - x, 1 - y)]

        def dev(p):
            return 4 * p[0] + 2 * p[1] + p[2]

        def copy(a, k, block, to, src=None):
            dst = views[a](outs[a], dev(block))
            return pltpu.make_async_remote_copy(
                src_ref=dst if src is None else src, dst_ref=dst,
                send_sem=send_sems.at[a, k], recv_sem=recv_sems.at[a, k],
                device_id=to, device_id_type=MESH)

        mine, first, passed = [], [], []
        for a in range(n):
            cp = pltpu.make_async_copy(ins[a], views[a](outs[a], dev(me)), local_sems.at[a])
            cp.start()
            mine.append(cp)
            first.append(copy(a, 0, me, sibling, src=ins[a]))
            first += [copy(a, 1 + j, me, (*chip, c), src=ins[a]) for j, chip in enumerate(chips)]
        for cp in first:
            cp.start()
        for j, chip in enumerate(chips):
            for a in range(n):
                copy(a, 1 + j, (*chip, c), me).wait_recv()
                cp = copy(a, 4 + j, (*chip, c), sibling)
                cp.start()
                passed.append(cp)
        for a in range(n):
            copy(a, 0, sibling, me).wait_recv()
            for j, chip in enumerate(chips):
                copy(a, 4 + j, (*chip, 1 - c), me).wait_recv()
        for cp in first + passed:
            cp.wait_send()
        for cp in mine:
            cp.wait()

    return pl.pallas_call(
        body, in_specs=[ANY] * n, out_specs=[ANY] * n, out_shape=list(out_shapes),
        scratch_shapes=[pltpu.SemaphoreType.DMA((n, 7)), pltpu.SemaphoreType.DMA((n, 7)),
                        pltpu.SemaphoreType.DMA((n,))],
        name=name,
    )(*shards)


HBM = pl.BlockSpec(memory_space=pltpu.HBM)
SEM = pl.BlockSpec(memory_space=pltpu.SEMAPHORE)
EFFECT = pltpu.SideEffectType.DATAFLOW_SIDE_EFFECTING


def _relations(x, y, c):
    for m in range(1, N_DEV):
        yield m, (1 - x if m & 4 else x, 1 - y if m & 2 else y, 1 - c if m & 1 else c)


def _dev_id(p):
    return 4 * p[0] + 2 * p[1] + p[2]


def _send_start(srcs, land_shapes, src_views, dst_views, *, name):
    n = len(srcs)

    def body(*refs):
        ins, lands = refs[:n], refs[n:2 * n]
        send_sems, recv_sems, token = refs[2 * n], refs[2 * n + 1], refs[-1]
        x, y, c = lax.axis_index("x"), lax.axis_index("y"), lax.axis_index("c")
        me = _dev_id((x, y, c))
        for m, p in _relations(x, y, c):
            for a in range(n):
                pltpu.make_async_remote_copy(
                    src_ref=src_views[a](ins[a], me, _dev_id(p), m), dst_ref=dst_views[a](lands[a], me, m),
                    send_sem=send_sems.at[a * (N_DEV - 1) + m - 1], recv_sem=recv_sems.at[a * (N_DEV - 1) + m - 1],
                    device_id=p, device_id_type=MESH).start()
        token[...] = jnp.zeros_like(token)

    lands = [pltpu.with_memory_space_constraint(lax.empty(s.shape, s.dtype), pltpu.HBM) for s in land_shapes]
    srcs = [pltpu.with_memory_space_constraint(v, pltpu.HBM) for v in srcs]
    res = pl.pallas_call(
        body, name=name,
        out_shape=[pltpu.SemaphoreType.DMA((n * (N_DEV - 1),)), pltpu.SemaphoreType.DMA((n * (N_DEV - 1),))]
        + [pltpu.HBM(v.shape, v.dtype) for v in srcs] + [pltpu.HBM(s.shape, s.dtype) for s in land_shapes]
        + [SDS((8, 128), F32)],
        in_specs=[HBM] * (2 * n), out_specs=[SEM, SEM] + [HBM] * (2 * n) + [pl.BlockSpec(memory_space=pltpu.VMEM)],
        input_output_aliases={i: 2 + i for i in range(2 * n)},
        compiler_params=pltpu.CompilerParams(has_side_effects=EFFECT),
    )(*srcs, *lands)
    return res[0], res[1], res[2:2 + n], res[2 + n:2 + 2 * n], res[-1]


def _send_wait(started, src_views, dst_views, own_views, after, *, name):
    send_sems, recv_sems, srcs, lands, _ = started
    n = len(srcs)

    def body(*refs):
        ins, lnd = refs[:n], refs[n:2 * n]
        send_sems, recv_sems = refs[2 * n], refs[2 * n + 1]
        got = refs[2 * n + 3 + n:2 * n + 3 + 2 * n]
        local_sems = refs[-1]
        x, y, c = lax.axis_index("x"), lax.axis_index("y"), lax.axis_index("c")
        me = _dev_id((x, y, c))
        for m, p in _relations(x, y, c):
            for a in range(n):
                cp = pltpu.make_async_remote_copy(
                    src_ref=src_views[a](ins[a], me, _dev_id(p), m), dst_ref=dst_views[a](lnd[a], me, m),
                    send_sem=send_sems.at[a * (N_DEV - 1) + m - 1], recv_sem=recv_sems.at[a * (N_DEV - 1) + m - 1],
                    device_id=p, device_id_type=MESH)
                cp.wait_send()
                cp.wait_recv()
        own = []
        for a in range(n):
            frm, to = own_views[a](ins[a], got[a], me)
            cp = pltpu.make_async_copy(frm, to, local_sems.at[a])
            cp.start()
            own.append(cp)
        for cp in own:
            cp.wait()

    res = pl.pallas_call(
        body, name=name,
        out_shape=[pltpu.HBM(v.shape, v.dtype) for v in srcs] + [pltpu.HBM(v.shape, v.dtype) for v in lands],
        in_specs=[HBM] * (2 * n) + [SEM, SEM, ANY], out_specs=[HBM] * (2 * n),
        input_output_aliases={i: i for i in range(2 * n)},
        scratch_shapes=[pltpu.SemaphoreType.DMA((n,))],
        compiler_params=pltpu.CompilerParams(has_side_effects=EFFECT),
    )(*srcs, *lands, send_sems, recv_sems, after)
    return res[n:]


def kernel(x, norm_gains, fox_w_in, fox_b_f, hgrn_w_in, hgrn_lb_logits, hgrn_onorm, w_out, final_gain, loss_target, m_norm_gains, m_fox_w_in, m_fox_b_f, m_hgrn_w_in, m_hgrn_lb_logits, m_hgrn_onorm, m_w_out, m_final_gain, v_norm_gains, v_fox_w_in, v_fox_b_f, v_hgrn_w_in, v_hgrn_lb_logits, v_hgrn_onorm, v_w_out, v_final_gain):
    _, S, D = x.shape
    H = FOX_HEADS
    W = H * HEAD_DIM
    assert HGRN_HEADS == H and w_out.shape[2] == D
    cf = fox_w_in.shape[2]
    ch = hgrn_w_in.shape[2]
    ro = w_out.shape[1]
    co = hgrn_onorm.shape[1]
    assert N_DEV * cf == 4 * W + H and N_DEV * ch == 4 * W and N_DEV * ro == W and N_DEV * co == W
    x2 = x.reshape(S, D)
    tgt = loss_target.reshape(S, D)

    col = lambda n: (lambda r, i: r.at[:, pl.ds(pl.multiple_of(i * n, n), n)])
    row = lambda n: (lambda r, i: r.at[pl.ds(pl.multiple_of(i * n, n), n), :])
    late_views = [col(ch), row(ro), col(co)]
    late = _send_start(
        [hgrn_w_in[0].astype(BF16), w_out[1].astype(BF16), hgrn_onorm],
        [SDS((D, 4 * W), BF16), SDS((W, D), BF16), SDS((1, W), F32)],
        [lambda r, me, p, m: r] * 3, [lambda r, me, m, v=v: v(r, me) for v in late_views],
        name="gather_layer1_start")
    ng0 = norm_gains[0:1] + late[4][0:1, 0:1]

    wf_g, wo0 = _all_gather(
        [fox_w_in[0].astype(BF16), w_out[0].astype(BF16)], [SDS((N_DEV, D, cf), BF16), SDS((W, D), BF16)],
        [lambda r, p: r.at[p], row(ro)], name="gather_layer0")
    wf = jnp.transpose(wf_g, (1, 0, 2)).reshape(D, N_DEV * cf)
    wf_main = jnp.concatenate([wf[:, :3 * W], wf[:, 3 * W + H:]], axis=1)
    wfl_t = wf[:, 3 * W:3 * W + H].T
    wf_main_t = wf_main.T
    wo0_t = wo0.T

    h0 = _rms_fwd(x2, ng0, name="rms0_fwd")
    p0 = _mm_nn([h0], wf_main, BF16, scale_cols=(W, LOG2E * HEAD_DIM ** -0.5), name="fox_in_proj")
    fl_t = _mm_nt_rows(wfl_t, h0, name="fox_forget_proj")
    b_col = fox_b_f.reshape(H, 1)
    kaug = _fox_key_aug(*_fox_gate_fwd(fl_t, b_col, name="fox_gate_fwd"))
    o0, y0, qaug = _fox_fwd(p0, kaug, H=H, name="fox_attn_fwd")
    x1 = _mm_nn([y0], wo0, F32, residual=x2, name="fox_out_proj")

    wh, wo1, onorm = _send_wait(
        late, [lambda r, me, p, m: r] * 3, [lambda r, me, m, v=v: v(r, me) for v in late_views],
        [lambda src, land, me, v=v: (src, v(land, me)) for v in late_views], x1[0:8], name="gather_layer1_wait")
    wh_qig = jnp.concatenate([wh[:, :W], wh[:, 2 * W:]], axis=1)
    wh_f = wh[:, W:2 * W]
    wh_t = wh.T
    wo1_t = wo1.T
    lb = _lb_fwd(hgrn_lb_logits, name="hgrn_lower_bound")
    h1 = _rms_fwd(x1, norm_gains[1:2], name="rms1_fwd")
    p1 = _mm_nn([h1], wh_qig, BF16, name="hgrn_in_proj")
    f1 = _mm_nn([h1], wh_f, F32, name="hgrn_forget_proj")
    o1, y1, states = _hgrn_fwd(p1, f1, lb, onorm, H=H, name="hgrn_fwd")
    xo = _mm_nn([y1], wo1, F32, residual=x1, name="hgrn_out_proj")

    dx2, dx2b, loss_part, dgf = _loss_head(xo, final_gain.reshape(1, D), tgt, name="loss_head")
    loss = lax.psum(jnp.sum(loss_part), ("x", "y", "c"))

    dy1 = _mm_nn([dx2b], wo1_t, BF16, name="hgrn_out_proj_dx")
    dwo1 = _mm_tn(y1, [dx2b], BF16, name="hgrn_out_proj_dw")
    do1, dg1, donorm = _hgrn_post_bwd(dy1, o1, p1, onorm, H=H, name="hgrn_post_bwd")
    dq1, df1, di1, dlb = _hgrn_bwd(p1, f1, lb, do1, states, H=H, name="hgrn_bwd")
    segs1 = [dq1, df1, di1, dg1]
    dh1 = _mm_nn(segs1, wh_t, BF16, name="hgrn_in_proj_dx")
    dwh = _mm_tn(h1, segs1, BF16, name="hgrn_in_proj_dw")
    part_views = [col(ch), row(ro)]
    slot = lambda r, me, m: r.at[m]
    ex1 = _send_start([dwh, dwo1], [SDS((N_DEV, D, ch), BF16), SDS((N_DEV, ro, D), BF16)],
                      [lambda r, me, p, m, v=v: v(r, p) for v in part_views], [slot] * 2,
                      name="exchange_layer1_start")
    ng1 = norm_gains[1:2] + ex1[4][0:1, 0:1]
    dx1, dx1b, dng1 = _rms_bwd(x1, ng1, dh1, dx2, name="rms1_bwd")

    dy0 = _mm_nn([dx1b], wo0_t, BF16, name="fox_out_proj_dx")
    dwo0 = _mm_tn(y0, [dx1b], BF16, name="fox_out_proj_dw")
    do0, dg0, doaug = _fox_post_bwd(dy0, o0, p0, H=H, name="fox_post_bwd")
    dq0, dc_row = _fox_dq(p0, kaug, qaug, do0, doaug, H=H, name="fox_attn_dq")
    dk0, dv0, dc_key = _fox_dkv(p0, kaug, qaug, do0, doaug, H=H, name="fox_attn_dkv")
    dfl_t, dbf = _fox_gate_bwd(dc_row.reshape(H, S), dc_key.reshape(H, S), fl_t, b_col, name="fox_gate_bwd")
    dfl_tb = dfl_t.astype(BF16)
    dwfl_t = _mm_nn([dfl_tb], h0, BF16, name="fox_forget_proj_dw")
    segs0 = [dq0, dk0, dv0, dg0]
    dwf_main = _mm_tn(h0, segs0, BF16, name="fox_in_proj_dw")
    dwf = jnp.concatenate([dwf_main[:, :3 * W], dwfl_t.T, dwf_main[:, 3 * W:]], axis=1)
    dwf_blocks = jnp.transpose(dwf.reshape(D, N_DEV, cf), (1, 0, 2))
    ex0 = _send_start([dwf_blocks, dwo0], [SDS((N_DEV, D, cf), BF16), SDS((N_DEV, ro, D), BF16)],
                      [lambda r, me, p, m: r.at[p], lambda r, me, p, m: row(ro)(r, p)], [slot] * 2,
                      name="exchange_layer0_start")
    wfl_t0 = wfl_t + ex0[4][0:1, 0:1].astype(BF16)
    dh0_f = _mm_nn([dfl_tb.T], wfl_t0, BF16, name="fox_forget_proj_dx")
    dh0 = _mm_nn(segs0, wf_main_t, BF16, residual=dh0_f, name="fox_in_proj_dx")
    grad_x, _, dng0 = _rms_bwd(x2, norm_gains[0:1], dh0, dx1, name="rms0_bwd")

    own1 = [lambda src, land, me, v=v: (v(src, me), land.at[0]) for v in part_views]
    rh, ro1 = _send_wait(ex1, [lambda r, me, p, m, v=v: v(r, p) for v in part_views], [slot] * 2, own1, dng0,
                         name="exchange_layer1_wait")
    rf, ro0 = _send_wait(ex0, [lambda r, me, p, m: r.at[p], lambda r, me, p, m: row(ro)(r, p)], [slot] * 2,
                         [lambda src, land, me: (src.at[me], land.at[0]),
                          lambda src, land, me: (row(ro)(src, me), land.at[0])], dng0,
                         name="exchange_layer0_wait")

    pad = lambda a: jnp.pad(a, ((0, 0), (0, W - a.shape[1])))
    stats = jnp.concatenate([dng0, dng1, dlb, dgf, pad(dbf.reshape(1, H)), donorm,
                             jnp.zeros((2, W), F32)], axis=0)
    assert D == W
    (stats_all,) = _all_gather([stats], [SDS((N_DEV, STAT_ROWS, W), F32)], [lambda r, p: r.at[p]],
                               name="gather_small_grads")
    g_small = _stats_reduce(stats_all, hgrn_lb_logits, name="reduce_small_grads")
    me = 4 * lax.axis_index("x") + 2 * lax.axis_index("y") + lax.axis_index("c")
    g_onorm = lax.dynamic_slice_in_dim(g_small[6:7], me * co, co, axis=1)

    def upd(w, m, v, parts, name):
        shp = w.shape
        r2 = (-1, shp[-1])
        g, d, mn, vn = _adamw(w.reshape(r2), m.reshape(r2), v.reshape(r2), parts, name=name)
        return g.reshape(shp), d.reshape(shp), mn.reshape(shp), vn.reshape(shp)

    res = {
        "norm_gains": upd(norm_gains, m_norm_gains, v_norm_gains, [g_small[None, 0:2]], "adamw_norm_gains"),
        "fox_w_in": upd(fox_w_in, m_fox_w_in, v_fox_w_in, [rf], "adamw_fox_w_in"),
        "fox_b_f": upd(fox_b_f, m_fox_b_f, v_fox_b_f, [g_small[None, 5:6, :H]], "adamw_fox_b_f"),
        "hgrn_w_in": upd(hgrn_w_in, m_hgrn_w_in, v_hgrn_w_in, [rh], "adamw_hgrn_w_in"),
        "hgrn_lb_logits": upd(hgrn_lb_logits, m_hgrn_lb_logits, v_hgrn_lb_logits, [g_small[None, 2:4]],
                              "adamw_hgrn_lb_logits"),
        "hgrn_onorm": upd(hgrn_onorm, m_hgrn_onorm, v_hgrn_onorm, [g_onorm[None]], "adamw_hgrn_onorm"),
        "w_out": upd(w_out, m_w_out, v_w_out, [ro0, ro1], "adamw_w_out"),
        "final_gain": upd(final_gain.reshape(1, D), m_final_gain.reshape(1, D), v_final_gain.reshape(1, D),
                          [g_small[None, 4:5]], "adamw_final_gain"),
    }
    order = ["norm_gains", "fox_w_in", "fox_b_f", "hgrn_w_in", "hgrn_lb_logits", "hgrn_onorm", "w_out", "final_gain"]
    fix = lambda n, a: a.reshape(D) if n == "final_gain" else a
    outs = [loss, grad_x.reshape(1, S, D)]
    for k in range(4):
        outs += [fix(n, res[n][k]) for n in order]
    return tuple(outs)
```

```python
import functools

import numpy as np
import jax
import jax.numpy as jnp
from jax import lax
from jax.experimental import pallas as pl
from jax.experimental.pallas import tpu as pltpu

F32 = jnp.float32
BF16 = jnp.bfloat16
SDS = jax.ShapeDtypeStruct
MESH = pl.DeviceIdType.MESH

EPS = 1e-6
ADAM_LR, ADAM_B1, ADAM_B2, ADAM_EPS, ADAM_WD, ADAM_STEP = 0.001, 0.9, 0.999, 1e-08, 0.01, 10

N_DEV = 8
FOX_HEADS = 16
HGRN_HEADS = 16
HEAD_DIM = 128
HGRN_CHUNK = 128
HGRN_LEAF = 16
HGRN_HEADS_PER_STEP = 4
EXP_CLAMP = 85.0
ATT_BLOCK = 512
ATT_HEADS_PER_STEP = 2
POST_HEADS_PER_STEP = 4
NEG = -1e30
LOG2E = 1.4426950408889634
LN2 = 0.6931471805599453

VMEM_LIMIT_V7X = 56 * 1024 * 1024


def _params(*sem):
    return pltpu.CompilerParams(dimension_semantics=sem, vmem_limit_bytes=VMEM_LIMIT_V7X)


def _silu(x):
    return x * jax.nn.sigmoid(x)


def _dsilu(x):
    s = jax.nn.sigmoid(x)
    return s * (1.0 + x * (1.0 - s))


def _dot(a, b):
    return jnp.dot(a, b, preferred_element_type=F32)


def _dot_nt(a, b):
    return lax.dot_general(a, b, (((1,), (1,)), ((), ())), preferred_element_type=F32)


def _dot_tn(a, b):
    return lax.dot_general(a, b, (((0,), (0,)), ((), ())), preferred_element_type=F32)


def _mm_nn(a_list, b, out_dtype, *, name, residual=None, scale_cols=None, b_t=False, tm=512, tn=1024, tk=2048):
    ns = len(a_list)
    M, Ks = a_list[0].shape
    K, N = (b.shape[1], b.shape[0]) if b_t else b.shape
    dot = _dot_nt if b_t else _dot
    assert K == ns * Ks and all(a.shape == (M, Ks) for a in a_list)
    tm, tn, tk = min(tm, M), min(tn, N), min(tk, Ks)
    assert M % tm == 0 and N % tn == 0 and Ks % tk == 0
    assert scale_cols is None or scale_cols[0] % tn == 0
    nks = Ks // tk
    nk = ns * nks
    has_res = residual is not None

    def body(*refs):
        a_refs, b_ref = refs[:ns], refs[ns]
        res_ref = refs[ns + 1] if has_res else None
        o_ref = refs[ns + 1 + has_res]

        def finish(r):
            if has_res:
                r = r + res_ref[...].astype(F32)
            if scale_cols is not None:
                r = r * jnp.where(pl.program_id(1) < scale_cols[0] // tn, scale_cols[1], 1.0)
            o_ref[...] = r.astype(out_dtype)

        if nk == 1:
            finish(dot(a_refs[0][...], b_ref[...]))
            return
        acc_ref = refs[ns + 2 + has_res]
        k = pl.program_id(2)

        @pl.when(k == 0)
        def _():
            acc_ref[...] = jnp.zeros_like(acc_ref)

        for s in range(ns):
            def step(s=s):
                acc_ref[...] += dot(a_refs[s][...], b_ref[...])

            if ns == 1:
                step()
            else:
                pl.when(k // nks == s)(step)

        @pl.when(k == nk - 1)
        def _():
            finish(acc_ref[...])

    def a_map(i, j, k, s):
        return (i, jnp.clip(k - s * nks, 0, nks - 1))

    in_specs = [pl.BlockSpec((tm, tk), functools.partial(a_map, s=s)) for s in range(ns)]
    if b_t:
        in_specs.append(pl.BlockSpec((tn, tk), lambda i, j, k: (j, k)))
    else:
        in_specs.append(pl.BlockSpec((tk, tn), lambda i, j, k: (k, j)))
    args = list(a_list) + [b]
    if has_res:
        in_specs.append(pl.BlockSpec((tm, tn), lambda i, j, k: (i, j)))
        args.append(residual)
    return pl.pallas_call(
        body, grid=(M // tm, N // tn, nk), in_specs=in_specs,
        out_specs=pl.BlockSpec((tm, tn), lambda i, j, k: (i, j)),
        out_shape=SDS((M, N), out_dtype),
        scratch_shapes=[] if nk == 1 else [pltpu.VMEM((tm, tn), F32)],
        compiler_params=_params("parallel", "parallel", "arbitrary"), name=name,
    )(*args)


def _mm_tn(a, b_list, out_dtype, *, name, tm=1024, tn=2048, tk=512):
    ns = len(b_list)
    S, M = a.shape
    Ns = b_list[0].shape[1]
    assert all(b.shape == (S, Ns) for b in b_list)
    tm, tn, tk = min(tm, M), min(tn, Ns), min(tk, S)
    assert M % tm == 0 and Ns % tn == 0 and S % tk == 0
    njs = Ns // tn
    nk = S // tk

    def body(*refs):
        a_ref, b_refs, o_ref, acc_ref = refs[0], refs[1:1 + ns], refs[1 + ns], refs[2 + ns]
        j, k = pl.program_id(1), pl.program_id(2)

        @pl.when(k == 0)
        def _():
            acc_ref[...] = jnp.zeros_like(acc_ref)

        for s in range(ns):
            def step(s=s):
                acc_ref[...] += _dot_tn(a_ref[...], b_refs[s][...])

            if ns == 1:
                step()
            else:
                pl.when(j // njs == s)(step)

        @pl.when(k == nk - 1)
        def _():
            o_ref[...] = acc_ref[...].astype(out_dtype)

    def b_map(i, j, k, s):
        return (k, jnp.clip(j - s * njs, 0, njs - 1))

    in_specs = [pl.BlockSpec((tk, tm), lambda i, j, k: (k, i))]
    in_specs += [pl.BlockSpec((tk, tn), functools.partial(b_map, s=s)) for s in range(ns)]
    return pl.pallas_call(
        body, grid=(M // tm, ns * njs, nk), in_specs=in_specs,
        out_specs=pl.BlockSpec((tm, tn), lambda i, j, k: (i, j)),
        out_shape=SDS((M, ns * Ns), out_dtype),
        scratch_shapes=[pltpu.VMEM((tm, tn), F32)],
        compiler_params=_params("parallel", "parallel", "arbitrary"), name=name,
    )(a, *b_list)


def _mm_nt_rows(w_t, h, *, name, tn=1024):
    R, K = w_t.shape
    S = h.shape[0]
    tn = min(tn, S)

    def body(w_ref, h_ref, o_ref):
        o_ref[...] = _dot_nt(w_ref[...], h_ref[...])

    return pl.pallas_call(
        body, grid=(S // tn,),
        in_specs=[pl.BlockSpec((R, K), lambda i: (0, 0)), pl.BlockSpec((tn, K), lambda i: (i, 0))],
        out_specs=pl.BlockSpec((R, tn), lambda i: (0, i)),
        out_shape=SDS((R, S), F32), compiler_params=_params("parallel"), name=name,
    )(w_t, h)


def _rms_fwd(x, gain, *, name, tm=512):
    S, D = x.shape
    tm = min(tm, S)

    def body(x_ref, g_ref, h_ref):
        xv = x_ref[...]
        r = lax.rsqrt(jnp.mean(xv * xv, axis=-1, keepdims=True) + EPS)
        h_ref[...] = ((xv * r) * g_ref[...]).astype(BF16)

    return pl.pallas_call(
        body, grid=(S // tm,),
        in_specs=[pl.BlockSpec((tm, D), lambda i: (i, 0)), pl.BlockSpec((1, D), lambda i: (0, 0))],
        out_specs=pl.BlockSpec((tm, D), lambda i: (i, 0)),
        out_shape=SDS((S, D), BF16), compiler_params=_params("parallel"), name=name,
    )(x, gain)


def _rms_bwd(x, gain, dh, dres, *, name, tm=256):
    S, D = x.shape
    tm = min(tm, S)

    def body(x_ref, g_ref, dh_ref, dres_ref, dx_ref, dxb_ref, dg_ref):
        @pl.when(pl.program_id(0) == 0)
        def _():
            dg_ref[...] = jnp.zeros_like(dg_ref)

        xv = x_ref[...]
        r = lax.rsqrt(jnp.mean(xv * xv, axis=-1, keepdims=True) + EPS)
        xh = xv * r
        dhv = dh_ref[...].astype(F32)
        dg_ref[...] += jnp.sum(dhv * xh, axis=0, keepdims=True)
        dxh = dhv * g_ref[...]
        dx = r * (dxh - xh * jnp.mean(dxh * xh, axis=-1, keepdims=True)) + dres_ref[...]
        dx_ref[...] = dx
        dxb_ref[...] = dx.astype(BF16)

    row = pl.BlockSpec((tm, D), lambda i: (i, 0))
    vec = pl.BlockSpec((1, D), lambda i: (0, 0))
    return pl.pallas_call(
        body, grid=(S // tm,), in_specs=[row, vec, row, row], out_specs=[row, row, vec],
        out_shape=[SDS((S, D), F32), SDS((S, D), BF16), SDS((1, D), F32)],
        compiler_params=_params("arbitrary"), name=name,
    )(x, gain, dh, dres)


def _loss_head(x, gain, target, *, name, tm=256):
    S, D = x.shape
    tm = min(tm, S)
    assert tm % 8 == 0 and D % 128 == 0

    def body(x_ref, g_ref, t_ref, dx_ref, dxb_ref, loss_ref, dg_ref):
        @pl.when(pl.program_id(0) == 0)
        def _():
            dg_ref[...] = jnp.zeros_like(dg_ref)
            loss_ref[...] = jnp.zeros_like(loss_ref)

        xv = x_ref[...]
        g = g_ref[...]
        r = lax.rsqrt(jnp.mean(xv * xv, axis=-1, keepdims=True) + EPS)
        xh = xv * r
        err = xh * g - t_ref[...]
        e2 = (err * err).reshape(tm // 8, 8, D).sum(axis=0)
        part = e2[:, 0:128]
        for k in range(1, D // 128):
            part = part + e2[:, k * 128:(k + 1) * 128]
        loss_ref[...] += part * (0.5 / D)
        dy = err * (1.0 / D)
        dg_ref[...] += jnp.sum(dy * xh, axis=0, keepdims=True)
        dxh = dy * g
        dx = r * (dxh - xh * jnp.mean(dxh * xh, axis=-1, keepdims=True))
        dx_ref[...] = dx
        dxb_ref[...] = dx.astype(BF16)

    row = pl.BlockSpec((tm, D), lambda i: (i, 0))
    vec = pl.BlockSpec((1, D), lambda i: (0, 0))
    return pl.pallas_call(
        body, grid=(S // tm,), in_specs=[row, vec, row],
        out_specs=[row, row, pl.BlockSpec((8, 128), lambda i: (0, 0)), vec],
        out_shape=[SDS((S, D), F32), SDS((S, D), BF16), SDS((8, 128), F32), SDS((1, D), F32)],
        compiler_params=_params("arbitrary"), name=name,
    )(x, gain, target)


def _split3(x):
    hi = x.astype(BF16)
    r1 = x - hi.astype(F32)
    mid = r1.astype(BF16)
    lo = (r1 - mid.astype(F32)).astype(BF16)
    return hi, mid, lo


def _split2(x):
    hi = x.astype(BF16)
    lo = (x - hi.astype(F32)).astype(BF16)
    return hi, lo


def _fox_gate_fwd(fl_t, b_col, *, name):
    H, S = fl_t.shape
    L = 128
    tri = jnp.asarray(np.triu(np.ones((L, L), np.float32)), BF16)

    def body(fl_ref, b_ref, tri_ref, hi_ref, mid_ref, lo_ref, carry):
        @pl.when(pl.program_id(0) == 0)
        def _():
            carry[...] = jnp.zeros_like(carry)

        z = fl_ref[...] + b_ref[...]
        lf = jnp.minimum(z, 0.0) - jnp.log(1.0 + jnp.exp(-jnp.abs(z)))
        hi, mid, lo = _split3(lf)
        t = tri_ref[...]
        c = (_dot(hi, t) + _dot(mid, t)) + _dot(lo, t) + carry[...]
        carry[...] = c[:, L - 1:L]
        hi_ref[...], mid_ref[...], lo_ref[...] = _split3(c * (-LOG2E))

    blk = pl.BlockSpec((H, L), lambda i: (0, i))
    return pl.pallas_call(
        body, grid=(S // L,),
        in_specs=[blk, pl.BlockSpec((H, 1), lambda i: (0, 0)), pl.BlockSpec((L, L), lambda i: (0, 0))],
        out_specs=[blk] * 3, out_shape=[SDS((H, S), BF16)] * 3, scratch_shapes=[pltpu.VMEM((H, 1), F32)],
        compiler_params=_params("arbitrary"), name=name,
    )(fl_t, b_col, tri)


def _fox_gate_bwd(dc_row, dc_key, fl_t, b_col, *, name):
    H, S = fl_t.shape
    L = 128
    n = S // L
    tri = jnp.asarray(np.tril(np.ones((L, L), np.float32)), BF16)

    def body(dcr_ref, dck_ref, fl_ref, b_ref, tri_ref, dfl_ref, db_ref, carry):
        @pl.when(pl.program_id(0) == 0)
        def _():
            carry[...] = jnp.zeros_like(carry)
            db_ref[...] = jnp.zeros_like(db_ref)

        hi, mid, lo = _split3(dcr_ref[...] + dck_ref[...])
        t = tri_ref[...]
        dlf = (_dot(hi, t) + _dot(mid, t)) + _dot(lo, t) + carry[...]
        carry[...] = dlf[:, 0:1]
        z = fl_ref[...] + b_ref[...]
        dfl = dlf * jax.nn.sigmoid(-z)
        dfl_ref[...] = dfl
        db_ref[...] += jnp.sum(dfl, axis=1, keepdims=True)

    blk = pl.BlockSpec((H, L), lambda i: (0, n - 1 - i))
    col = pl.BlockSpec((H, 1), lambda i: (0, 0))
    return pl.pallas_call(
        body, grid=(n,), in_specs=[blk, blk, blk, col, pl.BlockSpec((L, L), lambda i: (0, 0))],
        out_specs=[blk, col], out_shape=[SDS((H, S), F32), SDS((H, 1), F32)],
        scratch_shapes=[pltpu.VMEM((H, 1), F32)], compiler_params=_params("arbitrary"), name=name,
    )(dc_row, dc_key, fl_t, b_col, tri)


AUG = HEAD_DIM


def _lane_select(cols, shape):
    lane = lax.broadcasted_iota(jnp.int32, shape, 1)
    out = jnp.zeros(shape, BF16)
    for k, c in reversed(list(enumerate(cols))):
        c = jnp.full(shape, c, BF16) if isinstance(c, (int, float)) else jnp.broadcast_to(c, shape).astype(BF16)
        out = jnp.where(lane == k, c, out)
    return out


def _fox_key_aug(b_hi, b_mid, b_lo):
    H, S = b_hi.shape
    ones = jnp.ones((H, S), BF16)
    ka = jnp.stack([b_hi, b_mid, b_lo, ones, ones, ones], axis=-1)
    ka = jnp.pad(ka, ((0, 0), (0, 0), (0, AUG - 6)))
    return jnp.transpose(ka, (1, 0, 2)).reshape(S, H * AUG)


def _fox_fwd(p0, kaug, *, H, name):
    S = p0.shape[0]
    T = min(ATT_BLOCK, S)
    nq = S // T
    dh = HEAD_DIM
    G = ATT_HEADS_PER_STEP
    assert H % G == 0

    def body(q_ref, k_ref, ka_ref, v_ref, g_ref, o_ref, y_ref, qa_ref, m_sc, acc_sc):
        i = pl.program_id(1)
        qaug = _lane_select([1.0, 1.0, 1.0], (T, AUG))
        ones = jnp.ones((T, dh), BF16)
        m_sc[...] = jnp.full_like(m_sc, NEG)
        acc_sc[...] = jnp.zeros_like(acc_sc)

        def step(j, masked):
            rows = pl.ds(pl.multiple_of(j * T, T), T)
            for g in range(G):
                hd = slice(g * dh, (g + 1) * dh)
                q = jnp.concatenate([q_ref[:, hd], qaug], axis=1)
                kj = jnp.concatenate([k_ref[rows, hd], ka_ref[rows, hd]], axis=1)
                vj = jnp.concatenate([v_ref[rows, hd], ones], axis=1)
                t = _dot_nt(q, kj)
                if masked:
                    row = lax.broadcasted_iota(jnp.int32, (T, T), 0)
                    col = lax.broadcasted_iota(jnp.int32, (T, T), 1)
                    t = jnp.where(row >= col, t, NEG)
                m_prev = m_sc[g]
                m_new = jnp.maximum(m_prev, jnp.max(t, axis=-1, keepdims=True))
                p = jnp.exp2(t - jnp.tile(m_new, (1, T // 128)))
                alpha = jnp.exp2(m_prev - m_new)
                acc_sc[g] = jnp.tile(alpha, (1, 2)) * acc_sc[g] + _dot(p.astype(BF16), vj)
                m_sc[g] = m_new

        def loop_body(j, carry):
            step(j, False)
            return carry

        lax.fori_loop(0, i, loop_body, 0)
        step(i, True)
        for g in range(G):
            hd = slice(g * dh, (g + 1) * dh)
            l = acc_sc[g, :, dh:]
            o = acc_sc[g, :, :dh] / l
            o_ref[:, hd] = o
            y_ref[:, hd] = (o * _silu(g_ref[:, hd].astype(F32))).astype(BF16)
            hi, mid, lo = _split3(-(m_sc[g] + jnp.log2(l)))
            qa_ref[:, hd] = _lane_select([1.0, 1.0, 1.0, hi, mid, lo], (T, AUG))

    blk = lambda off: pl.BlockSpec((T, G * dh), lambda h, i: (i, off // G + h))
    full = lambda off: pl.BlockSpec((S, G * dh), lambda h, i: (0, off // G + h))
    return pl.pallas_call(
        body, grid=(H // G, nq),
        in_specs=[blk(0), full(H), full(0), full(2 * H), blk(3 * H)],
        out_specs=[blk(0), blk(0), blk(0)],
        out_shape=[SDS((S, H * dh), F32), SDS((S, H * dh), BF16), SDS((S, H * AUG), BF16)],
        scratch_shapes=[pltpu.VMEM((G, T, 128), F32), pltpu.VMEM((G, T, 2 * dh), F32)],
        compiler_params=_params("parallel", "arbitrary"), name=name,
    )(p0, p0, kaug, p0, p0)


def _fox_post_bwd(dy, o, p0, *, H, name, tm=512):
    S = dy.shape[0]
    dh = HEAD_DIM
    tm = min(tm, S)
    G = POST_HEADS_PER_STEP
    assert H % G == 0

    def body(dy_ref, o_ref, g_ref, do_ref, dg_ref, da_ref):
        dyv = dy_ref[...].astype(F32)
        ov = o_ref[...]
        g = g_ref[...].astype(F32)
        do = (dyv * _silu(g)).astype(BF16)
        do_ref[...] = do
        dg_ref[...] = (dyv * ov * _dsilu(g)).astype(BF16)
        prod = do.astype(F32) * ov
        for k in range(G):
            hd = slice(k * dh, (k + 1) * dh)
            delta = jnp.sum(prod[:, hd], axis=-1, keepdims=True)
            hi, mid, lo = _split3(-jnp.broadcast_to(delta, (tm, AUG)))
            da_ref[:, hd] = _lane_select([hi, mid, lo], (tm, AUG))

    blk = pl.BlockSpec((tm, G * dh), lambda h, i: (i, h))
    return pl.pallas_call(
        body, grid=(H // G, S // tm),
        in_specs=[blk, blk, pl.BlockSpec((tm, G * dh), lambda h, i: (i, 3 * H // G + h))],
        out_specs=[blk, blk, blk],
        out_shape=[SDS((S, H * dh), BF16), SDS((S, H * dh), BF16), SDS((S, H * AUG), BF16)],
        compiler_params=_params("parallel", "parallel"), name=name,
    )(dy, o, p0)


def _fox_dq(p0, kaug, qaug, do, doaug, *, H, name):
    S = p0.shape[0]
    T = min(ATT_BLOCK, S)
    nq = S // T
    dh = HEAD_DIM
    scale = dh ** -0.5
    G = ATT_HEADS_PER_STEP
    assert H % G == 0

    def body(q_ref, qa_ref, k_ref, ka_ref, v_ref, do_ref, da_ref, dq_ref, rs_ref, acc_sc):
        i = pl.program_id(1)
        vaug = _lane_select([1.0, 1.0, 1.0], (T, AUG))
        ones = jnp.ones((T, dh), BF16)
        acc_sc[...] = jnp.zeros_like(acc_sc)

        def step(j, masked):
            rows = pl.ds(pl.multiple_of(j * T, T), T)
            for g in range(G):
                hd = slice(g * dh, (g + 1) * dh)
                q = jnp.concatenate([q_ref[:, hd], qa_ref[:, hd]], axis=1)
                do = jnp.concatenate([do_ref[:, hd], da_ref[:, hd]], axis=1)
                k = k_ref[rows, hd]
                p = jnp.exp2(_dot_nt(q, jnp.concatenate([k, ka_ref[rows, hd]], axis=1)))
                if masked:
                    row = lax.broadcasted_iota(jnp.int32, (T, T), 0)
                    col = lax.broadcasted_iota(jnp.int32, (T, T), 1)
                    p = jnp.where(row >= col, p, 0.0)
                ds = p * _dot_nt(do, jnp.concatenate([v_ref[rows, hd], vaug], axis=1))
                acc_sc[g] += _dot(ds.astype(BF16), jnp.concatenate([k, ones], axis=1))

        def loop_body(j, carry):
            step(j, False)
            return carry

        lax.fori_loop(0, i, loop_body, 0)
        step(i, True)
        for g in range(G):
            dq_ref[:, g * dh:(g + 1) * dh] = (acc_sc[g, :, :dh] * scale).astype(BF16)
            rs_ref[g] = acc_sc[g, :, dh:dh + 1]

    blk = lambda off: pl.BlockSpec((T, G * dh), lambda h, i: (i, off // G + h))
    full = lambda off: pl.BlockSpec((S, G * dh), lambda h, i: (0, off // G + h))
    return pl.pallas_call(
        body, grid=(H // G, nq),
        in_specs=[blk(0), blk(0), full(H), full(0), full(2 * H), blk(0), blk(0)],
        out_specs=[blk(0), pl.BlockSpec((G, T, 1), lambda h, i: (h, i, 0))],
        out_shape=[SDS((S, H * dh), BF16), SDS((H, S, 1), F32)],
        scratch_shapes=[pltpu.VMEM((G, T, 2 * dh), F32)],
        compiler_params=_params("parallel", "arbitrary"), name=name,
    )(p0, qaug, p0, kaug, p0, do, doaug)


def _fox_dkv(p0, kaug, qaug, do, doaug, *, H, name):
    S = p0.shape[0]
    T = min(ATT_BLOCK, S)
    nq = S // T
    dh = HEAD_DIM
    G = ATT_HEADS_PER_STEP
    assert H % G == 0

    def body(q_ref, qa_ref, k_ref, ka_ref, v_ref, do_ref, da_ref, dk_ref, dv_ref, dc_ref, dk_sc, dv_sc):
        j = pl.program_id(1)
        vaug = _lane_select([1.0, 1.0, 1.0], (T, AUG))
        ones = jnp.ones((T, dh), BF16)
        dk_sc[...] = jnp.zeros_like(dk_sc)
        dv_sc[...] = jnp.zeros_like(dv_sc)

        def step(i, masked):
            rows = pl.ds(pl.multiple_of(i * T, T), T)
            for g in range(G):
                hd = slice(g * dh, (g + 1) * dh)
                k = jnp.concatenate([k_ref[:, hd], ka_ref[:, hd]], axis=1)
                v = jnp.concatenate([v_ref[:, hd], vaug], axis=1)
                qi = q_ref[rows, hd]
                doi = do_ref[rows, hd]
                pt = jnp.exp2(_dot_nt(k, jnp.concatenate([qi, qa_ref[rows, hd]], axis=1)))
                if masked:
                    row = lax.broadcasted_iota(jnp.int32, (T, T), 0)
                    col = lax.broadcasted_iota(jnp.int32, (T, T), 1)
                    pt = jnp.where(col >= row, pt, 0.0)
                dv_sc[g] += _dot(pt.astype(BF16), doi)
                dst = pt * _dot_nt(v, jnp.concatenate([doi, da_ref[rows, hd]], axis=1))
                dk_sc[g] += _dot(dst.astype(BF16), jnp.concatenate([qi, ones], axis=1))

        step(j, True)

        def loop_body(i, carry):
            step(i, False)
            return carry

        lax.fori_loop(j + 1, nq, loop_body, 0)
        for g in range(G):
            hd = slice(g * dh, (g + 1) * dh)
            dk_ref[:, hd] = (dk_sc[g, :, :dh] * LN2).astype(BF16)
            dv_ref[:, hd] = dv_sc[g].astype(BF16)
            dc_ref[g] = -dk_sc[g, :, dh:dh + 1]

    blk = lambda off: pl.BlockSpec((T, G * dh), lambda h, j: (j, off // G + h))
    full = lambda off: pl.BlockSpec((S, G * dh), lambda h, j: (0, off // G + h))
    return pl.pallas_call(
        body, grid=(H // G, nq),
        in_specs=[full(0), full(0), blk(H), blk(0), blk(2 * H), full(0), full(0)],
        out_specs=[blk(0), blk(0), pl.BlockSpec((G, T, 1), lambda h, j: (h, j, 0))],
        out_shape=[SDS((S, H * dh), BF16), SDS((S, H * dh), BF16), SDS((H, S, 1), F32)],
        scratch_shapes=[pltpu.VMEM((G, T, 2 * dh), F32), pltpu.VMEM((G, T, dh), F32)],
        compiler_params=_params("parallel", "arbitrary"), name=name,
    )(p0, qaug, p0, kaug, p0, do, doaug)


def _hgrn_levels(C, leaf):
    levels = []
    h = C // 2
    while h >= leaf:
        levels.append(h)
        h //= 2
    return levels


def _hgrn_sum_matrix(C, leaf):
    t = np.arange(C)[:, None]
    u = np.arange(C)[None, :]
    mats = [(u <= t), (u > t)]
    for h in _hgrn_levels(C, leaf):
        start = (t // (2 * h)) * (2 * h)
        mid = start + h - 1
        second = t > mid
        m = np.where(second, (u > mid) & (u <= t), (u > t) & (u <= mid))
        mats.append(m)
    lstart = (t // leaf) * leaf
    mats.append((u >= lstart) & (u <= t))
    return np.concatenate([m.astype(np.float32) for m in mats], axis=0)


def _hgrn_chunk_terms(qr, fz, lb, msum, C, leaf):
    levels = _hgrn_levels(C, leaf)
    sq = _silu(qr)
    t = jnp.exp(-jnp.abs(fz))
    r = 1.0 / (1.0 + t)
    pos = fz >= 0.0
    sp = jnp.where(pos, r, t * r)
    sn = jnp.where(pos, t * r, r)
    f = lb + (1.0 - lb) * sp
    lf = jnp.log(f)
    k = (1.0 - lb) * sn
    hi, lo = _split2(lf)
    dsum = _dot(msum, hi) + _dot(msum, lo)
    b = dsum[0:C]
    kdec = dsum[C:2 * C]
    rowi = lax.broadcasted_iota(jnp.int32, (C, 1), 0)
    lev = []
    for n, h in enumerate(levels):
        e = jnp.exp(dsum[(2 + n) * C:(3 + n) * C])
        second = (rowi % (2 * h)) >= h
        qm = jnp.where(second, sq * e, 0.0).astype(BF16)
        km = jnp.where(second, 0.0, k * e).astype(BF16)
        lev.append((h, e, second, qm, km))
    dleaf = dsum[(2 + len(levels)) * C:(3 + len(levels)) * C]
    eq = jnp.exp(dleaf)
    ek = jnp.exp(jnp.minimum(-dleaf, EXP_CLAMP))
    return dict(sq=sq, sp=sp, sn=sn, f=f, k=k, b=b, kdec=kdec, lev=lev, eq=eq, ek=ek,
                ql=(sq * eq).astype(BF16), kl=(k * ek).astype(BF16),
                qs=(sq * jnp.exp(b)).astype(BF16), ke=(k * jnp.exp(kdec)).astype(BF16),
                e_c=jnp.exp(b[C - 1:C, :]))


def _hgrn_masks(C, leaf, transposed):
    a = lax.broadcasted_iota(jnp.int32, (C, C), 0)
    bb = lax.broadcasted_iota(jnp.int32, (C, C), 1)
    t, s = (bb, a) if transposed else (a, bb)
    lev = [None if 2 * h == C else (t // (2 * h)) == (s // (2 * h)) for h in _hgrn_levels(C, leaf)]
    if leaf == C:
        leafm = s <= t
    else:
        leafm = ((t // leaf) == (s // leaf)) & (s <= t)
    return lev, leafm


def _hgrn_fwd(p1, f1, lb, onorm, *, H, name, tb=512):
    S = p1.shape[0]
    dk = HEAD_DIM
    C = min(HGRN_CHUNK, S)
    leaf = min(HGRN_LEAF, C)
    tb = min(tb, S)
    nc = tb // C
    G = HGRN_HEADS_PER_STEP
    assert H % G == 0
    msum = jnp.asarray(_hgrn_sum_matrix(C, leaf), BF16)

    def body(q_ref, f_ref, v_ref, g_ref, lb_ref, on_ref, ms_ref, o_ref, y_ref, st_ref, st_sc):
        @pl.when(pl.program_id(1) == 0)
        def _():
            st_sc[...] = jnp.zeros_like(st_sc)

        msv = ms_ref[...]
        lmask, leafm = _hgrn_masks(C, leaf, False)

        def chunk(n, carry):
            rows = pl.ds(pl.multiple_of(n * C, C), C)
            for g in range(G):
                hd = slice(g * dk, (g + 1) * dk)
                tm = _hgrn_chunk_terms(q_ref[rows, hd].astype(F32), f_ref[rows, hd], lb_ref[:, hd], msv, C, leaf)
                v = v_ref[rows, hd]
                st = st_sc[g]
                st_ref[g, n] = st
                a = jnp.where(leafm, _dot_nt(tm["ql"], tm["kl"]), 0.0)
                for (h, e, second, qm, km), m in zip(tm["lev"], lmask):
                    al = _dot_nt(qm, km)
                    a = a + (al if m is None else jnp.where(m, al, 0.0))
                o = _dot_nt(tm["qs"], st.astype(BF16)) + _dot(a.astype(BF16), v)
                st_sc[g] = st * tm["e_c"] + _dot(v.T, tm["ke"])
                o_ref[rows, hd] = o
                rn = lax.rsqrt(jnp.mean(o * o, axis=-1, keepdims=True) + EPS)
                y = ((o * rn) * on_ref[:, hd]) * _silu(g_ref[rows, hd].astype(F32))
                y_ref[rows, hd] = y.astype(BF16)
            return carry

        lax.fori_loop(0, nc, chunk, 0)

    blk = lambda off: pl.BlockSpec((tb, G * dk), lambda h, i: (i, off // G + h))
    vec = pl.BlockSpec((1, G * dk), lambda h, i: (0, h))
    return pl.pallas_call(
        body, grid=(H // G, S // tb),
        in_specs=[blk(0), blk(0), blk(H), blk(2 * H), vec, vec,
                  pl.BlockSpec(msum.shape, lambda h, i: (0, 0))],
        out_specs=[blk(0), blk(0), pl.BlockSpec((G, nc, dk, dk), lambda h, i: (h, i, 0, 0))],
        out_shape=[SDS((S, H * dk), F32), SDS((S, H * dk), BF16), SDS((H, S // C, dk, dk), F32)],
        scratch_shapes=[pltpu.VMEM((G, dk, dk), F32)],
        compiler_params=_params("parallel", "arbitrary"), name=name,
    )(p1, f1, p1, p1, lb, onorm, msum)


def _hgrn_post_bwd(dy, o, p1, onorm, *, H, name, tm=512):
    S = dy.shape[0]
    dk = HEAD_DIM
    tm = min(tm, S)
    G = POST_HEADS_PER_STEP
    assert H % G == 0

    def body(dy_ref, o_ref, g_ref, on_ref, do_ref, dg_ref, don_ref):
        @pl.when(pl.program_id(1) == 0)
        def _():
            don_ref[...] = jnp.zeros_like(don_ref)

        for k in range(G):
            hd = slice(k * dk, (k + 1) * dk)
            dyv = dy_ref[:, hd].astype(F32)
            ov = o_ref[:, hd]
            g = g_ref[:, hd].astype(F32)
            onv = on_ref[:, hd]
            rn = lax.rsqrt(jnp.mean(ov * ov, axis=-1, keepdims=True) + EPS)
            oh = ov * rn
            dn = dyv * _silu(g)
            dg_ref[:, hd] = (dyv * (oh * onv) * _dsilu(g)).astype(BF16)
            don_ref[:, hd] += jnp.sum(dn * oh, axis=0, keepdims=True)
            doh = dn * onv
            do_ref[:, hd] = (rn * (doh - oh * jnp.mean(doh * oh, axis=-1, keepdims=True))).astype(BF16)

    blk = pl.BlockSpec((tm, G * dk), lambda h, i: (i, h))
    vec = pl.BlockSpec((1, G * dk), lambda h, i: (0, h))
    return pl.pallas_call(
        body, grid=(H // G, S // tm),
        in_specs=[blk, blk, pl.BlockSpec((tm, G * dk), lambda h, i: (i, 2 * H // G + h)), vec],
        out_specs=[blk, blk, vec],
        out_shape=[SDS((S, H * dk), BF16), SDS((S, H * dk), BF16), SDS((1, H * dk), F32)],
        compiler_params=_params("parallel", "arbitrary"), name=name,
    )(dy, o, p1, onorm)


def _hgrn_bwd(p1, f1, lb, do, states, *, H, name, tb=512):
    S = p1.shape[0]
    dk = HEAD_DIM
    C = min(HGRN_CHUNK, S)
    leaf = min(HGRN_LEAF, C)
    tb = min(tb, S)
    nc = tb // C
    nb = S // tb
    G = HGRN_HEADS_PER_STEP
    assert H % G == 0
    msum = jnp.asarray(_hgrn_sum_matrix(C, leaf), BF16)
    rtri = jnp.asarray(np.triu(np.ones((C, C), np.float32)), BF16)

    def body(q_ref, f_ref, v_ref, do_ref, st_ref, lb_ref, ms_ref, rt_ref,
             dq_ref, df_ref, dv_ref, dlb_ref, g_sc):
        @pl.when(pl.program_id(1) == 0)
        def _():
            g_sc[...] = jnp.zeros_like(g_sc)
            dlb_ref[...] = jnp.zeros_like(dlb_ref)

        msv = ms_ref[...]
        rtv = rt_ref[...]
        lmask, leafm = _hgrn_masks(C, leaf, False)
        lmask_t, leafm_t = _hgrn_masks(C, leaf, True)
        f32 = lambda z: z.astype(F32)

        def head_chunk(g, n):
            hd = slice(g * dk, (g + 1) * dk)
            rows = pl.ds(pl.multiple_of(n * C, C), C)
            lbv = lb_ref[:, hd]
            qr = q_ref[rows, hd].astype(F32)
            tm = _hgrn_chunk_terms(qr, f_ref[rows, hd], lbv, msv, C, leaf)
            v = v_ref[rows, hd]
            dov = do_ref[rows, hd]
            st0 = st_ref[g, n]
            gt = g_sc[g]
            gtb = gt.astype(BF16)
            da = _dot_nt(dov, v)
            da_t = _dot_nt(v, dov)

            dal = jnp.where(leafm, da, 0.0).astype(BF16)
            dal_t = jnp.where(leafm_t, da_t, 0.0).astype(BF16)
            dql = _dot(dal, tm["kl"])
            dkl = _dot(dal_t, tm["ql"])
            dsq = dql * tm["eq"]
            dkk = dkl * tm["ek"]
            xq = f32(tm["ql"]) * dql
            xk = f32(tm["kl"]) * dkl
            a_t = jnp.where(leafm_t, _dot_nt(tm["kl"], tm["ql"]), 0.0)
            for (h, e, second, qm, km), m, m_t in zip(tm["lev"], lmask, lmask_t):
                dl = (da if m is None else jnp.where(m, da, 0.0)).astype(BF16)
                dl_t = (da_t if m_t is None else jnp.where(m_t, da_t, 0.0)).astype(BF16)
                dqm = _dot(dl, km)
                dkm = _dot(dl_t, qm)
                dsq = dsq + jnp.where(second, dqm * e, 0.0)
                dkk = dkk + jnp.where(second, 0.0, dkm * e)
                xq = xq + f32(qm) * dqm
                xk = xk + f32(km) * dkm
                al_t = _dot_nt(km, qm)
                a_t = a_t + (al_t if m_t is None else jnp.where(m_t, al_t, 0.0))
            dqs = _dot(dov, st0.astype(BF16))
            dke = _dot(v, gtb)
            dsq = dsq + dqs * jnp.exp(tm["b"])
            dkk = dkk + dke * jnp.exp(tm["kdec"])
            xq = xq + f32(tm["qs"]) * dqs
            xk = xk + f32(tm["ke"]) * dke
            dvv = _dot(a_t.astype(BF16), dov) + _dot_nt(tm["ke"], gtb)
            r_end = jnp.sum(f32(gtb) * _dot(v.T, tm["ke"]) + gt * (st0 * tm["e_c"]), axis=0, keepdims=True)
            g_sc[g] = gt * tm["e_c"] + _dot(dov.T, tm["qs"])
            xh, xm, xl = _split3(xq - xk)
            dlf = (_dot(rtv, xh) + _dot(rtv, xm)) + _dot(rtv, xl) + r_end
            dlf_f = dlf / tm["f"]
            dsp = (1.0 - lbv) * (dlf_f - dkk)
            df_ref[rows, hd] = (dsp * (tm["sp"] * tm["sn"])).astype(BF16)
            dq_ref[rows, hd] = (dsq * _dsilu(qr)).astype(BF16)
            dv_ref[rows, hd] = dvv.astype(BF16)
            dlb_ref[:, hd] += jnp.sum(dlf_f * tm["sn"] - dkk * tm["sn"], axis=0, keepdims=True)

        def chunk(nn, carry):
            for g in range(G):
                head_chunk(g, nc - 1 - nn)
            return carry

        lax.fori_loop(0, nc, chunk, 0)

    blk = lambda off: pl.BlockSpec((tb, G * dk), lambda h, i: (nb - 1 - i, off // G + h))
    vec = pl.BlockSpec((1, G * dk), lambda h, i: (0, h))
    return pl.pallas_call(
        body, grid=(H // G, nb),
        in_specs=[blk(0), blk(0), blk(H), blk(0),
                  pl.BlockSpec((G, nc, dk, dk), lambda h, i: (h, nb - 1 - i, 0, 0)), vec,
                  pl.BlockSpec(msum.shape, lambda h, i: (0, 0)), pl.BlockSpec((C, C), lambda h, i: (0, 0))],
        out_specs=[blk(0), blk(0), blk(0), vec],
        out_shape=[SDS((S, H * dk), BF16)] * 3 + [SDS((1, H * dk), F32)],
        scratch_shapes=[pltpu.VMEM((G, dk, dk), F32)],
        compiler_params=_params("parallel", "arbitrary"), name=name,
    )(p1, f1, p1, do, states, lb, msum, rtri)


def _lb_fwd(logits, *, name):
    W = logits.shape[1]

    def body(l_ref, lb_ref):
        l = l_ref[...]
        m = jnp.max(l, axis=0, keepdims=True)
        e = jnp.exp(l - m)
        p = e / jnp.sum(e, axis=0, keepdims=True)
        lb_ref[...] = (p[0:1] + p[1:2]) - p[0:1]

    return pl.pallas_call(body, out_shape=SDS((1, W), F32), name=name)(logits)


STAT_ROWS = 8


def _stats_reduce(stats_all, logits, *, name):
    W = logits.shape[1]

    def body(s_ref, l_ref, g_ref):
        tot = s_ref[0]
        for d in range(1, N_DEV):
            tot = tot + s_ref[d]
        l = l_ref[...]
        m = jnp.max(l, axis=0, keepdims=True)
        e = jnp.exp(l - m)
        p = e / jnp.sum(e, axis=0, keepdims=True)
        dlb = tot[2:3]
        dl0 = -(p[0:1] * p[1:2]) * dlb
        dl1 = (p[1:2] * (1.0 - p[1:2])) * dlb
        g_ref[0:2] = tot[0:2]
        g_ref[2:3] = dl0
        g_ref[3:4] = dl1
        g_ref[4:7] = tot[3:6]
        g_ref[7:8] = jnp.zeros((1, W), F32)

    return pl.pallas_call(body, out_shape=SDS((STAT_ROWS, W), F32), name=name)(stats_all, logits)


def _adamw(w, m, v, g_parts, *, name, tr=128):
    R, C = w.shape
    ns = len(g_parts)
    n, Rs = g_parts[0].shape[0], g_parts[0].shape[1]
    assert all(p.shape == (n, Rs, C) for p in g_parts) and ns * Rs == R
    tr = min(tr, Rs)
    assert Rs % tr == 0
    nts = Rs // tr
    c1 = 1.0 / (1.0 - ADAM_B1 ** ADAM_STEP)
    c2 = 1.0 / (1.0 - ADAM_B2 ** ADAM_STEP)

    def body(*refs):
        w_ref, m_ref, v_ref = refs[:3]
        g_refs = refs[3:3 + ns]
        go_ref, d_ref, mo_ref, vo_ref = refs[3 + ns:]

        def update(g_ref):
            g = g_ref[0].astype(F32)
            for k in range(1, n):
                g = g + g_ref[k].astype(F32)
            mn = ADAM_B1 * m_ref[...] + (1.0 - ADAM_B1) * g
            vn = ADAM_B2 * v_ref[...] + (1.0 - ADAM_B2) * (g * g)
            d_ref[...] = -ADAM_LR * ((mn * c1) / (jnp.sqrt(vn * c2) + ADAM_EPS) + ADAM_WD * w_ref[...])
            go_ref[...] = g
            mo_ref[...] = mn
            vo_ref[...] = vn

        for s in range(ns):
            if ns == 1:
                update(g_refs[s])
            else:
                pl.when(pl.program_id(0) // nts == s)(functools.partial(update, g_refs[s]))

    def g_map(i, s):
        return (0, jnp.clip(i - s * nts, 0, nts - 1), 0)

    blk = pl.BlockSpec((tr, C), lambda i: (i, 0))
    return pl.pallas_call(
        body, grid=(R // tr,),
        in_specs=[blk, blk, blk] + [pl.BlockSpec((n, tr, C), functools.partial(g_map, s=s)) for s in range(ns)],
        out_specs=[blk] * 4, out_shape=[SDS((R, C), F32)] * 4,
        compiler_params=_params("parallel"), name=name,
    )(w, m, v, *g_parts)


ANY = pl.BlockSpec(memory_space=pl.ANY)
STAGE_BYTES = 2 * 1024 * 1024


def _stage_shape(shape, dtype):
    row_bytes = int(np.prod(shape[1:])) * jnp.dtype(dtype).itemsize
    rows = max(1, min(shape[0], STAGE_BYTES // row_bytes))
    while shape[0] % rows:
        rows -= 1
    return (rows,) + tuple(shape[1:])


def _staged_copy(frm, to, buf, sems):
    rows = buf.shape[0]
    for r0 in range(0, frm.shape[0], rows):
        cp = pltpu.make_async_copy(frm.at[pl.ds(r0, rows)], buf, sems.at[0])
        cp.start()
        cp.wait()
        cp = pltpu.make_async_copy(buf, to.at[pl.ds(r0, rows)], sems.at[1])
        cp.start()
        cp.wait()


def _all_gather(shards, out_shapes, views, *, name):
    n = len(shards)

    def body(*refs):
        ins, outs = refs[:n], refs[n:2 * n]
        send_sems, recv_sems, local_sems = refs[2 * n:2 * n + 3]
        bufs = refs[2 * n + 3:]
        x, y, c = lax.axis_index("x"), lax.axis_index("y"), lax.axis_index("c")
        me, sibling = (x, y, c), (x, y, 1 - c)
        chips = [(1 - x, y), (x, 1 - y), (1 - x, 1 - y)]

        def dev(p):
            return 4 * p[0] + 2 * p[1] + p[2]

        def copy(a, k, block, to, src=None):
            dst = views[a](outs[a], dev(block))
            return pltpu.make_async_remote_copy(
                src_ref=dst if src is None else src, dst_ref=dst,
                send_sem=send_sems.at[a, k], recv_sem=recv_sems.at[a, k],
                device_id=to, device_id_type=MESH)

        first, passed = [], []
        for a in range(n):
            first.append(copy(a, 0, me, sibling, src=ins[a]))
            first += [copy(a, 1 + j, me, (*chip, c), src=ins[a]) for j, chip in enumerate(chips)]
        for cp in first:
            cp.start()
        for a in range(n):
            _staged_copy(ins[a], views[a](outs[a], dev(me)), bufs[a], local_sems)
        for j, chip in enumerate(chips):
            for a in range(n):
                copy(a, 1 + j, (*chip, c), me).wait_recv()
                cp = copy(a, 4 + j, (*chip, c), sibling)
                cp.start()
                passed.append(cp)
        for a in range(n):
            copy(a, 0, sibling, me).wait_recv()
            for j, chip in enumerate(chips):
                copy(a, 4 + j, (*chip, 1 - c), me).wait_recv()
        for cp in first + passed:
            cp.wait_send()

    return pl.pallas_call(
        body, in_specs=[ANY] * n, out_specs=[ANY] * n, out_shape=list(out_shapes),
        scratch_shapes=[pltpu.SemaphoreType.DMA((n, 7)), pltpu.SemaphoreType.DMA((n, 7)),
                        pltpu.SemaphoreType.DMA((2,))]
        + [pltpu.VMEM(_stage_shape(s.shape, s.dtype), s.dtype) for s in shards],
        name=name,
    )(*shards)


HBM = pl.BlockSpec(memory_space=pltpu.HBM)
SEM = pl.BlockSpec(memory_space=pltpu.SEMAPHORE)
EFFECT = pltpu.SideEffectType.DATAFLOW_SIDE_EFFECTING


def _relations(x, y, c):
    for m in range(1, N_DEV):
        yield m, (1 - x if m & 4 else x, 1 - y if m & 2 else y, 1 - c if m & 1 else c)


def _dev_id(p):
    return 4 * p[0] + 2 * p[1] + p[2]


def _send_start(srcs, land_shapes, src_views, dst_views, *, name):
    n = len(srcs)

    def body(*refs):
        ins, lands = refs[:n], refs[n:2 * n]
        send_sems, recv_sems, token = refs[2 * n], refs[2 * n + 1], refs[-1]
        x, y, c = lax.axis_index("x"), lax.axis_index("y"), lax.axis_index("c")
        me = _dev_id((x, y, c))
        for m, p in _relations(x, y, c):
            for a in range(n):
                pltpu.make_async_remote_copy(
                    src_ref=src_views[a](ins[a], me, _dev_id(p), m), dst_ref=dst_views[a](lands[a], me, m),
                    send_sem=send_sems.at[a * (N_DEV - 1) + m - 1], recv_sem=recv_sems.at[a * (N_DEV - 1) + m - 1],
                    device_id=p, device_id_type=MESH).start()
        token[...] = jnp.zeros_like(token)

    lands = [pltpu.with_memory_space_constraint(lax.empty(s.shape, s.dtype), pltpu.HBM) for s in land_shapes]
    srcs = [pltpu.with_memory_space_constraint(v, pltpu.HBM) for v in srcs]
    res = pl.pallas_call(
        body, name=name,
        out_shape=[pltpu.SemaphoreType.DMA((n * (N_DEV - 1),)), pltpu.SemaphoreType.DMA((n * (N_DEV - 1),))]
        + [pltpu.HBM(v.shape, v.dtype) for v in srcs] + [pltpu.HBM(s.shape, s.dtype) for s in land_shapes]
        + [SDS((8, 128), F32)],
        in_specs=[HBM] * (2 * n), out_specs=[SEM, SEM] + [HBM] * (2 * n) + [pl.BlockSpec(memory_space=pltpu.VMEM)],
        input_output_aliases={i: 2 + i for i in range(2 * n)},
        compiler_params=pltpu.CompilerParams(has_side_effects=EFFECT),
    )(*srcs, *lands)
    return res[0], res[1], res[2:2 + n], res[2 + n:2 + 2 * n], res[-1]


def _send_wait(started, src_views, dst_views, own_views, own_shapes, after, *, name):
    send_sems, recv_sems, srcs, lands, _ = started
    n = len(srcs)

    def body(*refs):
        ins, lnd = refs[:n], refs[n:2 * n]
        send_sems, recv_sems = refs[2 * n], refs[2 * n + 1]
        got = refs[2 * n + 3 + n:2 * n + 3 + 2 * n]
        local_sems = refs[2 * n + 3 + 2 * n]
        bufs = refs[2 * n + 4 + 2 * n:]
        x, y, c = lax.axis_index("x"), lax.axis_index("y"), lax.axis_index("c")
        me = _dev_id((x, y, c))
        for m, p in _relations(x, y, c):
            for a in range(n):
                cp = pltpu.make_async_remote_copy(
                    src_ref=src_views[a](ins[a], me, _dev_id(p), m), dst_ref=dst_views[a](lnd[a], me, m),
                    send_sem=send_sems.at[a * (N_DEV - 1) + m - 1], recv_sem=recv_sems.at[a * (N_DEV - 1) + m - 1],
                    device_id=p, device_id_type=MESH)
                cp.wait_send()
                cp.wait_recv()
        for a in range(n):
            frm, to = own_views[a](ins[a], got[a], me)
            _staged_copy(frm, to, bufs[a], local_sems)

    res = pl.pallas_call(
        body, name=name,
        out_shape=[pltpu.HBM(v.shape, v.dtype) for v in srcs] + [pltpu.HBM(v.shape, v.dtype) for v in lands],
        in_specs=[HBM] * (2 * n) + [SEM, SEM, ANY], out_specs=[HBM] * (2 * n),
        input_output_aliases={i: i for i in range(2 * n)},
        scratch_shapes=[pltpu.SemaphoreType.DMA((2,))]
        + [pltpu.VMEM(_stage_shape(s, v.dtype), v.dtype) for s, v in zip(own_shapes, srcs)],
        compiler_params=pltpu.CompilerParams(has_side_effects=EFFECT),
    )(*srcs, *lands, send_sems, recv_sems, after)
    return res[n:]


def kernel(x, norm_gains, fox_w_in, fox_b_f, hgrn_w_in, hgrn_lb_logits, hgrn_onorm, w_out, final_gain, loss_target, m_norm_gains, m_fox_w_in, m_fox_b_f, m_hgrn_w_in, m_hgrn_lb_logits, m_hgrn_onorm, m_w_out, m_final_gain, v_norm_gains, v_fox_w_in, v_fox_b_f, v_hgrn_w_in, v_hgrn_lb_logits, v_hgrn_onorm, v_w_out, v_final_gain):
    _, S, D = x.shape
    H = FOX_HEADS
    W = H * HEAD_DIM
    assert HGRN_HEADS == H and w_out.shape[2] == D
    cf = fox_w_in.shape[2]
    ch = hgrn_w_in.shape[2]
    ro = w_out.shape[1]
    co = hgrn_onorm.shape[1]
    assert N_DEV * cf == 4 * W + H and N_DEV * ch == 4 * W and N_DEV * ro == W and N_DEV * co == W
    x2 = x.reshape(S, D)
    tgt = loss_target.reshape(S, D)

    col = lambda n: (lambda r, i: r.at[:, pl.ds(pl.multiple_of(i * n, n), n)])
    row = lambda n: (lambda r, i: r.at[pl.ds(pl.multiple_of(i * n, n), n), :])
    late_views = [col(ch), row(ro), col(co)]
    late = _send_start(
        [hgrn_w_in[0].astype(BF16), w_out[1].astype(BF16), hgrn_onorm],
        [SDS((D, 4 * W), BF16), SDS((W, D), BF16), SDS((1, W), F32)],
        [lambda r, me, p, m: r] * 3, [lambda r, me, m, v=v: v(r, me) for v in late_views],
        name="gather_layer1_start")
    ng0 = norm_gains[0:1] + late[4][0:1, 0:1]

    wf_g, wo0 = _all_gather(
        [fox_w_in[0].astype(BF16), w_out[0].astype(BF16)], [SDS((N_DEV, D, cf), BF16), SDS((W, D), BF16)],
        [lambda r, p: r.at[p], row(ro)], name="gather_layer0")
    wf = jnp.transpose(wf_g, (1, 0, 2)).reshape(D, N_DEV * cf)
    wf_main = jnp.concatenate([wf[:, :3 * W], wf[:, 3 * W + H:]], axis=1)
    wfl_t = wf[:, 3 * W:3 * W + H].T

    h0 = _rms_fwd(x2, ng0, name="rms0_fwd")
    p0 = _mm_nn([h0], wf_main, BF16, scale_cols=(W, LOG2E * HEAD_DIM ** -0.5), name="fox_in_proj")
    fl_t = _mm_nt_rows(wfl_t, h0, name="fox_forget_proj")
    b_col = fox_b_f.reshape(H, 1)
    kaug = _fox_key_aug(*_fox_gate_fwd(fl_t, b_col, name="fox_gate_fwd"))
    o0, y0, qaug = _fox_fwd(p0, kaug, H=H, name="fox_attn_fwd")
    x1 = _mm_nn([y0], wo0, F32, residual=x2, name="fox_out_proj")

    wh, wo1, onorm = _send_wait(
        late, [lambda r, me, p, m: r] * 3, [lambda r, me, m, v=v: v(r, me) for v in late_views],
        [lambda src, land, me, v=v: (src, v(land, me)) for v in late_views], [(D, ch), (ro, D), (1, co)],
        x1[0:8], name="gather_layer1_wait")
    wh_qig = jnp.concatenate([wh[:, :W], wh[:, 2 * W:]], axis=1)
    wh_f = wh[:, W:2 * W]
    lb = _lb_fwd(hgrn_lb_logits, name="hgrn_lower_bound")
    h1 = _rms_fwd(x1, norm_gains[1:2], name="rms1_fwd")
    p1 = _mm_nn([h1], wh_qig, BF16, name="hgrn_in_proj")
    f1 = _mm_nn([h1], wh_f, F32, name="hgrn_forget_proj")
    o1, y1, states = _hgrn_fwd(p1, f1, lb, onorm, H=H, name="hgrn_fwd")
    xo = _mm_nn([y1], wo1, F32, residual=x1, name="hgrn_out_proj")

    dx2, dx2b, loss_part, dgf = _loss_head(xo, final_gain.reshape(1, D), tgt, name="loss_head")
    loss = lax.psum(jnp.sum(loss_part), ("x", "y", "c"))

    dy1 = _mm_nn([dx2b], wo1, BF16, b_t=True, name="hgrn_out_proj_dx")
    dwo1 = _mm_tn(y1, [dx2b], BF16, name="hgrn_out_proj_dw")
    do1, dg1, donorm = _hgrn_post_bwd(dy1, o1, p1, onorm, H=H, name="hgrn_post_bwd")
    dq1, df1, di1, dlb = _hgrn_bwd(p1, f1, lb, do1, states, H=H, name="hgrn_bwd")
    segs1 = [dq1, df1, di1, dg1]
    dh1 = _mm_nn(segs1, wh, BF16, b_t=True, name="hgrn_in_proj_dx")
    dwh = _mm_tn(h1, segs1, BF16, name="hgrn_in_proj_dw")
    part_views = [col(ch), row(ro)]
    slot = lambda r, me, m: r.at[m]
    ex1 = _send_start([dwh, dwo1], [SDS((N_DEV, D, ch), BF16), SDS((N_DEV, ro, D), BF16)],
                      [lambda r, me, p, m, v=v: v(r, p) for v in part_views], [slot] * 2,
                      name="exchange_layer1_start")
    ng1 = norm_gains[1:2] + ex1[4][0:1, 0:1]
    dx1, dx1b, dng1 = _rms_bwd(x1, ng1, dh1, dx2, name="rms1_bwd")

    dy0 = _mm_nn([dx1b], wo0, BF16, b_t=True, name="fox_out_proj_dx")
    dwo0 = _mm_tn(y0, [dx1b], BF16, name="fox_out_proj_dw")
    do0, dg0, doaug = _fox_post_bwd(dy0, o0, p0, H=H, name="fox_post_bwd")
    dq0, dc_row = _fox_dq(p0, kaug, qaug, do0, doaug, H=H, name="fox_attn_dq")
    dk0, dv0, dc_key = _fox_dkv(p0, kaug, qaug, do0, doaug, H=H, name="fox_attn_dkv")
    dfl_t, dbf = _fox_gate_bwd(dc_row.reshape(H, S), dc_key.reshape(H, S), fl_t, b_col, name="fox_gate_bwd")
    dfl_tb = dfl_t.astype(BF16)
    dwfl_t = _mm_nn([dfl_tb], h0, BF16, name="fox_forget_proj_dw")
    segs0 = [dq0, dk0, dv0, dg0]
    dwf_main = _mm_tn(h0, segs0, BF16, name="fox_in_proj_dw")
    dwf = jnp.concatenate([dwf_main[:, :3 * W], dwfl_t.T, dwf_main[:, 3 * W:]], axis=1)
    dwf_blocks = jnp.transpose(dwf.reshape(D, N_DEV, cf), (1, 0, 2))
    ex0 = _send_start([dwf_blocks, dwo0], [SDS((N_DEV, D, cf), BF16), SDS((N_DEV, ro, D), BF16)],
                      [lambda r, me, p, m: r.at[p], lambda r, me, p, m: row(ro)(r, p)], [slot] * 2,
                      name="exchange_layer0_start")
    wfl_t0 = wfl_t + ex0[4][0:1, 0:1].astype(BF16)
    dh0_f = _mm_nn([dfl_tb.T], wfl_t0, BF16, name="fox_forget_proj_dx")
    dh0 = _mm_nn(segs0, wf_main, BF16, residual=dh0_f, b_t=True, name="fox_in_proj_dx")
    grad_x, _, dng0 = _rms_bwd(x2, norm_gains[0:1], dh0, dx1, name="rms0_bwd")

    own1 = [lambda src, land, me, v=v: (v(src, me), land.at[0]) for v in part_views]
    rh, ro1 = _send_wait(ex1, [lambda r, me, p, m, v=v: v(r, p) for v in part_views], [slot] * 2, own1,
                         [(D, ch), (ro, D)], dng0, name="exchange_layer1_wait")
    rf, ro0 = _send_wait(ex0, [lambda r, me, p, m: r.at[p], lambda r, me, p, m: row(ro)(r, p)], [slot] * 2,
                         [lambda src, land, me: (src.at[me], land.at[0]),
                          lambda src, land, me: (row(ro)(src, me), land.at[0])],
                         [(D, cf), (ro, D)], dng0, name="exchange_layer0_wait")

    pad = lambda a: jnp.pad(a, ((0, 0), (0, W - a.shape[1])))
    stats = jnp.concatenate([dng0, dng1, dlb, dgf, pad(dbf.reshape(1, H)), donorm,
                             jnp.zeros((2, W), F32)], axis=0)
    assert D == W
    (stats_all,) = _all_gather([stats], [SDS((N_DEV, STAT_ROWS, W), F32)], [lambda r, p: r.at[p]],
                               name="gather_small_grads")
    g_small = _stats_reduce(stats_all, hgrn_lb_logits, name="reduce_small_grads")
    me = 4 * lax.axis_index("x") + 2 * lax.axis_index("y") + lax.axis_index("c")
    g_onorm = lax.dynamic_slice_in_dim(g_small[6:7], me * co, co, axis=1)

    def upd(w, m, v, parts, name):
        shp = w.shape
        r2 = (-1, shp[-1])
        g, d, mn, vn = _adamw(w.reshape(r2), m.reshape(r2), v.reshape(r2), parts, name=name)
        return g.reshape(shp), d.reshape(shp), mn.reshape(shp), vn.reshape(shp)

    res = {
        "norm_gains": upd(norm_gains, m_norm_gains, v_norm_gains, [g_small[None, 0:2]], "adamw_norm_gains"),
        "fox_w_in": upd(fox_w_in, m_fox_w_in, v_fox_w_in, [rf], "adamw_fox_w_in"),
        "fox_b_f": upd(fox_b_f, m_fox_b_f, v_fox_b_f, [g_small[None, 5:6, :H]], "adamw_fox_b_f"),
        "hgrn_w_in": upd(hgrn_w_in, m_hgrn_w_in, v_hgrn_w_in, [rh], "adamw_hgrn_w_in"),
        "hgrn_lb_logits": upd(hgrn_lb_logits, m_hgrn_lb_logits, v_hgrn_lb_logits, [g_small[None, 2:4]],
                              "adamw_hgrn_lb_logits"),
        "hgrn_onorm": upd(hgrn_onorm, m_hgrn_onorm, v_hgrn_onorm, [g_onorm[None]], "adamw_hgrn_onorm"),
        "w_out": upd(w_out, m_w_out, v_w_out, [ro0, ro1], "adamw_w_out"),
        "final_gain": upd(final_gain.reshape(1, D), m_final_gain.reshape(1, D), v_final_gain.reshape(1, D),
                          [g_small[None, 4:5]], "adamw_final_gain"),
    }
    order = ["norm_gains", "fox_w_in", "fox_b_f", "hgrn_w_in", "hgrn_lb_logits", "hgrn_onorm", "w_out", "final_gain"]
    fix = lambda n, a: a.reshape(D) if n == "final_gain" else a
    outs = [loss, grad_x.reshape(1, S, D)]
    for k in range(4):
        outs += [fix(n, res[n][k]) for n in order]
    return tuple(outs)
```

```python
import functools

import numpy as np
import jax
import jax.numpy as jnp
from jax import lax
from jax.experimental import pallas as pl
from jax.experimental.pallas import tpu as pltpu

F32 = jnp.float32
BF16 = jnp.bfloat16
SDS = jax.ShapeDtypeStruct
MESH = pl.DeviceIdType.MESH

EPS = 1e-6
ADAM_LR, ADAM_B1, ADAM_B2, ADAM_EPS, ADAM_WD, ADAM_STEP = 0.001, 0.9, 0.999, 1e-08, 0.01, 10

N_DEV = 8
FOX_HEADS = 16
HGRN_HEADS = 16
HEAD_DIM = 128
HGRN_CHUNK = 128
HGRN_LEAF = 16
HGRN_HEADS_PER_STEP = 4
EXP_CLAMP = 85.0
ATT_BLOCK = 512
ATT_HEADS_PER_STEP = 2
POST_HEADS_PER_STEP = 4
NEG = -1e30
LOG2E = 1.4426950408889634
LN2 = 0.6931471805599453

VMEM_LIMIT_V7X = 56 * 1024 * 1024


def _params(*sem):
    return pltpu.CompilerParams(dimension_semantics=sem, vmem_limit_bytes=VMEM_LIMIT_V7X)


def _silu(x):
    return x * jax.nn.sigmoid(x)


def _dsilu(x):
    s = jax.nn.sigmoid(x)
    return s * (1.0 + x * (1.0 - s))


def _dot(a, b):
    return jnp.dot(a, b, preferred_element_type=F32)


def _dot_nt(a, b):
    return lax.dot_general(a, b, (((1,), (1,)), ((), ())), preferred_element_type=F32)


def _dot_tn(a, b):
    return lax.dot_general(a, b, (((0,), (0,)), ((), ())), preferred_element_type=F32)


def _mm_nn(a_list, b, out_dtype, *, name, residual=None, scale_cols=None, b_t=False, tm=512, tn=1024, tk=2048):
    ns = len(a_list)
    M, Ks = a_list[0].shape
    K, N = (b.shape[1], b.shape[0]) if b_t else b.shape
    dot = _dot_nt if b_t else _dot
    assert K == ns * Ks and all(a.shape == (M, Ks) for a in a_list)
    tm, tn, tk = min(tm, M), min(tn, N), min(tk, Ks)
    assert M % tm == 0 and N % tn == 0 and Ks % tk == 0
    assert scale_cols is None or scale_cols[0] % tn == 0
    nks = Ks // tk
    nk = ns * nks
    has_res = residual is not None

    def body(*refs):
        a_refs, b_ref = refs[:ns], refs[ns]
        res_ref = refs[ns + 1] if has_res else None
        o_ref = refs[ns + 1 + has_res]

        def finish(r):
            if has_res:
                r = r + res_ref[...].astype(F32)
            if scale_cols is not None:
                r = r * jnp.where(pl.program_id(1) < scale_cols[0] // tn, scale_cols[1], 1.0)
            o_ref[...] = r.astype(out_dtype)

        if nk == 1:
            finish(dot(a_refs[0][...], b_ref[...]))
            return
        acc_ref = refs[ns + 2 + has_res]
        k = pl.program_id(2)

        @pl.when(k == 0)
        def _():
            acc_ref[...] = jnp.zeros_like(acc_ref)

        for s in range(ns):
            def step(s=s):
                acc_ref[...] += dot(a_refs[s][...], b_ref[...])

            if ns == 1:
                step()
            else:
                pl.when(k // nks == s)(step)

        @pl.when(k == nk - 1)
        def _():
            finish(acc_ref[...])

    def a_map(i, j, k, s):
        return (i, jnp.clip(k - s * nks, 0, nks - 1))

    in_specs = [pl.BlockSpec((tm, tk), functools.partial(a_map, s=s)) for s in range(ns)]
    if b_t:
        in_specs.append(pl.BlockSpec((tn, tk), lambda i, j, k: (j, k)))
    else:
        in_specs.append(pl.BlockSpec((tk, tn), lambda i, j, k: (k, j)))
    args = list(a_list) + [b]
    if has_res:
        in_specs.append(pl.BlockSpec((tm, tn), lambda i, j, k: (i, j)))
        args.append(residual)
    return pl.pallas_call(
        body, grid=(M // tm, N // tn, nk), in_specs=in_specs,
        out_specs=pl.BlockSpec((tm, tn), lambda i, j, k: (i, j)),
        out_shape=SDS((M, N), out_dtype),
        scratch_shapes=[] if nk == 1 else [pltpu.VMEM((tm, tn), F32)],
        compiler_params=_params("parallel", "parallel", "arbitrary"), name=name,
    )(*args)


def _mm_tn(a, b_list, out_dtype, *, name, tm=1024, tn=2048, tk=512):
    ns = len(b_list)
    S, M = a.shape
    Ns = b_list[0].shape[1]
    assert all(b.shape == (S, Ns) for b in b_list)
    tm, tn, tk = min(tm, M), min(tn, Ns), min(tk, S)
    assert M % tm == 0 and Ns % tn == 0 and S % tk == 0
    njs = Ns // tn
    nk = S // tk

    def body(*refs):
        a_ref, b_refs, o_ref, acc_ref = refs[0], refs[1:1 + ns], refs[1 + ns], refs[2 + ns]
        j, k = pl.program_id(1), pl.program_id(2)

        @pl.when(k == 0)
        def _():
            acc_ref[...] = jnp.zeros_like(acc_ref)

        for s in range(ns):
            def step(s=s):
                acc_ref[...] += _dot_tn(a_ref[...], b_refs[s][...])

            if ns == 1:
                step()
            else:
                pl.when(j // njs == s)(step)

        @pl.when(k == nk - 1)
        def _():
            o_ref[...] = acc_ref[...].astype(out_dtype)

    def b_map(i, j, k, s):
        return (k, jnp.clip(j - s * njs, 0, njs - 1))

    in_specs = [pl.BlockSpec((tk, tm), lambda i, j, k: (k, i))]
    in_specs += [pl.BlockSpec((tk, tn), functools.partial(b_map, s=s)) for s in range(ns)]
    return pl.pallas_call(
        body, grid=(M // tm, ns * njs, nk), in_specs=in_specs,
        out_specs=pl.BlockSpec((tm, tn), lambda i, j, k: (i, j)),
        out_shape=SDS((M, ns * Ns), out_dtype),
        scratch_shapes=[pltpu.VMEM((tm, tn), F32)],
        compiler_params=_params("parallel", "parallel", "arbitrary"), name=name,
    )(a, *b_list)


def _mm_nt_rows(w_t, h, *, name, tn=1024):
    R, K = w_t.shape
    S = h.shape[0]
    tn = min(tn, S)

    def body(w_ref, h_ref, o_ref):
        o_ref[...] = _dot_nt(w_ref[...], h_ref[...])

    return pl.pallas_call(
        body, grid=(S // tn,),
        in_specs=[pl.BlockSpec((R, K), lambda i: (0, 0)), pl.BlockSpec((tn, K), lambda i: (i, 0))],
        out_specs=pl.BlockSpec((R, tn), lambda i: (0, i)),
        out_shape=SDS((R, S), F32), compiler_params=_params("parallel"), name=name,
    )(w_t, h)


def _rms_fwd(x, gain, *, name, tm=512):
    S, D = x.shape
    tm = min(tm, S)

    def body(x_ref, g_ref, h_ref):
        xv = x_ref[...]
        r = lax.rsqrt(jnp.mean(xv * xv, axis=-1, keepdims=True) + EPS)
        h_ref[...] = ((xv * r) * g_ref[...]).astype(BF16)

    return pl.pallas_call(
        body, grid=(S // tm,),
        in_specs=[pl.BlockSpec((tm, D), lambda i: (i, 0)), pl.BlockSpec((1, D), lambda i: (0, 0))],
        out_specs=pl.BlockSpec((tm, D), lambda i: (i, 0)),
        out_shape=SDS((S, D), BF16), compiler_params=_params("parallel"), name=name,
    )(x, gain)


def _rms_bwd(x, gain, dh, dres, *, name, tm=256):
    S, D = x.shape
    tm = min(tm, S)

    def body(x_ref, g_ref, dh_ref, dres_ref, dx_ref, dxb_ref, dg_ref):
        @pl.when(pl.program_id(0) == 0)
        def _():
            dg_ref[...] = jnp.zeros_like(dg_ref)

        xv = x_ref[...]
        r = lax.rsqrt(jnp.mean(xv * xv, axis=-1, keepdims=True) + EPS)
        xh = xv * r
        dhv = dh_ref[...].astype(F32)
        dg_ref[...] += jnp.sum(dhv * xh, axis=0, keepdims=True)
        dxh = dhv * g_ref[...]
        dx = r * (dxh - xh * jnp.mean(dxh * xh, axis=-1, keepdims=True)) + dres_ref[...]
        dx_ref[...] = dx
        dxb_ref[...] = dx.astype(BF16)

    row = pl.BlockSpec((tm, D), lambda i: (i, 0))
    vec = pl.BlockSpec((1, D), lambda i: (0, 0))
    return pl.pallas_call(
        body, grid=(S // tm,), in_specs=[row, vec, row, row], out_specs=[row, row, vec],
        out_shape=[SDS((S, D), F32), SDS((S, D), BF16), SDS((1, D), F32)],
        compiler_params=_params("arbitrary"), name=name,
    )(x, gain, dh, dres)


def _loss_head(x, gain, target, *, name, tm=256):
    S, D = x.shape
    tm = min(tm, S)
    assert tm % 8 == 0 and D % 128 == 0

    def body(x_ref, g_ref, t_ref, dx_ref, dxb_ref, loss_ref, dg_ref):
        @pl.when(pl.program_id(0) == 0)
        def _():
            dg_ref[...] = jnp.zeros_like(dg_ref)
            loss_ref[...] = jnp.zeros_like(loss_ref)

        xv = x_ref[...]
        g = g_ref[...]
        r = lax.rsqrt(jnp.mean(xv * xv, axis=-1, keepdims=True) + EPS)
        xh = xv * r
        err = xh * g - t_ref[...]
        e2 = (err * err).reshape(tm // 8, 8, D).sum(axis=0)
        part = e2[:, 0:128]
        for k in range(1, D // 128):
            part = part + e2[:, k * 128:(k + 1) * 128]
        loss_ref[...] += part * (0.5 / D)
        dy = err * (1.0 / D)
        dg_ref[...] += jnp.sum(dy * xh, axis=0, keepdims=True)
        dxh = dy * g
        dx = r * (dxh - xh * jnp.mean(dxh * xh, axis=-1, keepdims=True))
        dx_ref[...] = dx
        dxb_ref[...] = dx.astype(BF16)

    row = pl.BlockSpec((tm, D), lambda i: (i, 0))
    vec = pl.BlockSpec((1, D), lambda i: (0, 0))
    return pl.pallas_call(
        body, grid=(S // tm,), in_specs=[row, vec, row],
        out_specs=[row, row, pl.BlockSpec((8, 128), lambda i: (0, 0)), vec],
        out_shape=[SDS((S, D), F32), SDS((S, D), BF16), SDS((8, 128), F32), SDS((1, D), F32)],
        compiler_params=_params("arbitrary"), name=name,
    )(x, gain, target)


def _split3(x):
    hi = x.astype(BF16)
    r1 = x - hi.astype(F32)
    mid = r1.astype(BF16)
    lo = (r1 - mid.astype(F32)).astype(BF16)
    return hi, mid, lo


def _split2(x):
    hi = x.astype(BF16)
    lo = (x - hi.astype(F32)).astype(BF16)
    return hi, lo


def _fox_gate_fwd(fl_t, b_col, *, name):
    H, S = fl_t.shape
    L = 128
    tri = jnp.asarray(np.triu(np.ones((L, L), np.float32)), BF16)

    def body(fl_ref, b_ref, tri_ref, hi_ref, mid_ref, lo_ref, carry):
        @pl.when(pl.program_id(0) == 0)
        def _():
            carry[...] = jnp.zeros_like(carry)

        z = fl_ref[...] + b_ref[...]
        lf = jnp.minimum(z, 0.0) - jnp.log(1.0 + jnp.exp(-jnp.abs(z)))
        hi, mid, lo = _split3(lf)
        t = tri_ref[...]
        c = (_dot(hi, t) + _dot(mid, t)) + _dot(lo, t) + carry[...]
        carry[...] = c[:, L - 1:L]
        hi_ref[...], mid_ref[...], lo_ref[...] = _split3(c * (-LOG2E))

    blk = pl.BlockSpec((H, L), lambda i: (0, i))
    return pl.pallas_call(
        body, grid=(S // L,),
        in_specs=[blk, pl.BlockSpec((H, 1), lambda i: (0, 0)), pl.BlockSpec((L, L), lambda i: (0, 0))],
        out_specs=[blk] * 3, out_shape=[SDS((H, S), BF16)] * 3, scratch_shapes=[pltpu.VMEM((H, 1), F32)],
        compiler_params=_params("arbitrary"), name=name,
    )(fl_t, b_col, tri)


def _fox_gate_bwd(dc_row, dc_key, fl_t, b_col, *, name):
    H, S = fl_t.shape
    L = 128
    n = S // L
    tri = jnp.asarray(np.tril(np.ones((L, L), np.float32)), BF16)

    def body(dcr_ref, dck_ref, fl_ref, b_ref, tri_ref, dfl_ref, db_ref, carry):
        @pl.when(pl.program_id(0) == 0)
        def _():
            carry[...] = jnp.zeros_like(carry)
            db_ref[...] = jnp.zeros_like(db_ref)

        hi, mid, lo = _split3(dcr_ref[...] + dck_ref[...])
        t = tri_ref[...]
        dlf = (_dot(hi, t) + _dot(mid, t)) + _dot(lo, t) + carry[...]
        carry[...] = dlf[:, 0:1]
        z = fl_ref[...] + b_ref[...]
        dfl = dlf * jax.nn.sigmoid(-z)
        dfl_ref[...] = dfl
        db_ref[...] += jnp.sum(dfl, axis=1, keepdims=True)

    blk = pl.BlockSpec((H, L), lambda i: (0, n - 1 - i))
    col = pl.BlockSpec((H, 1), lambda i: (0, 0))
    return pl.pallas_call(
        body, grid=(n,), in_specs=[blk, blk, blk, col, pl.BlockSpec((L, L), lambda i: (0, 0))],
        out_specs=[blk, col], out_shape=[SDS((H, S), F32), SDS((H, 1), F32)],
        scratch_shapes=[pltpu.VMEM((H, 1), F32)], compiler_params=_params("arbitrary"), name=name,
    )(dc_row, dc_key, fl_t, b_col, tri)


AUG = HEAD_DIM


def _lane_select(cols, shape):
    lane = lax.broadcasted_iota(jnp.int32, shape, 1)
    out = jnp.zeros(shape, BF16)
    for k, c in reversed(list(enumerate(cols))):
        c = jnp.full(shape, c, BF16) if isinstance(c, (int, float)) else jnp.broadcast_to(c, shape).astype(BF16)
        out = jnp.where(lane == k, c, out)
    return out


def _fox_key_aug(b_hi, b_mid, b_lo):
    H, S = b_hi.shape
    ones = jnp.ones((H, S), BF16)
    ka = jnp.stack([b_hi, b_mid, b_lo, ones, ones, ones], axis=-1)
    ka = jnp.pad(ka, ((0, 0), (0, 0), (0, AUG - 6)))
    return jnp.transpose(ka, (1, 0, 2)).reshape(S, H * AUG)


def _fox_fwd(p0, kaug, *, H, name):
    S = p0.shape[0]
    T = min(ATT_BLOCK, S)
    nq = S // T
    dh = HEAD_DIM
    G = ATT_HEADS_PER_STEP
    assert H % G == 0

    def body(q_ref, k_ref, ka_ref, v_ref, g_ref, o_ref, y_ref, qa_ref, m_sc, acc_sc, p_sc, al_sc):
        i = pl.program_id(1)
        qaug = _lane_select([1.0, 1.0, 1.0], (T, AUG))
        ones = jnp.ones((T, dh), BF16)
        m_sc[...] = jnp.full_like(m_sc, NEG)
        acc_sc[...] = jnp.zeros_like(acc_sc)
        p_sc[...] = jnp.zeros_like(p_sc)
        al_sc[...] = jnp.ones_like(al_sc)

        def step(j, masked):
            rows = pl.ds(pl.multiple_of(j * T, T), T)
            prev = pl.ds(pl.multiple_of(jnp.maximum(j - 1, 0) * T, T), T)
            for g in range(G):
                hd = slice(g * dh, (g + 1) * dh)
                vp = jnp.concatenate([v_ref[prev, hd], ones], axis=1)
                acc_sc[g] = jnp.tile(al_sc[g], (1, 2)) * acc_sc[g] + _dot(p_sc[g], vp)
                q = jnp.concatenate([q_ref[:, hd], qaug], axis=1)
                kj = jnp.concatenate([k_ref[rows, hd], ka_ref[rows, hd]], axis=1)
                t = _dot_nt(q, kj)
                if masked:
                    row = lax.broadcasted_iota(jnp.int32, (T, T), 0)
                    col = lax.broadcasted_iota(jnp.int32, (T, T), 1)
                    t = jnp.where(row >= col, t, NEG)
                m_prev = m_sc[g]
                m_new = jnp.maximum(m_prev, jnp.max(t, axis=-1, keepdims=True))
                p_sc[g] = jnp.exp2(t - jnp.tile(m_new, (1, T // 128))).astype(BF16)
                al_sc[g] = jnp.exp2(m_prev - m_new)
                m_sc[g] = m_new

        def loop_body(j, carry):
            step(j, False)
            return carry

        lax.fori_loop(0, i, loop_body, 0)
        step(i, True)
        rows = pl.ds(pl.multiple_of(i * T, T), T)
        for g in range(G):
            hd = slice(g * dh, (g + 1) * dh)
            vp = jnp.concatenate([v_ref[rows, hd], ones], axis=1)
            acc = jnp.tile(al_sc[g], (1, 2)) * acc_sc[g] + _dot(p_sc[g], vp)
            l = acc[:, dh:]
            o = acc[:, :dh] / l
            o_ref[:, hd] = o
            y_ref[:, hd] = (o * _silu(g_ref[:, hd].astype(F32))).astype(BF16)
            hi, mid, lo = _split3(-(m_sc[g] + jnp.log2(l)))
            qa_ref[:, hd] = _lane_select([1.0, 1.0, 1.0, hi, mid, lo], (T, AUG))

    blk = lambda off: pl.BlockSpec((T, G * dh), lambda h, i: (i, off // G + h))
    full = lambda off: pl.BlockSpec((S, G * dh), lambda h, i: (0, off // G + h))
    return pl.pallas_call(
        body, grid=(H // G, nq),
        in_specs=[blk(0), full(H), full(0), full(2 * H), blk(3 * H)],
        out_specs=[blk(0), blk(0), blk(0)],
        out_shape=[SDS((S, H * dh), F32), SDS((S, H * dh), BF16), SDS((S, H * AUG), BF16)],
        scratch_shapes=[pltpu.VMEM((G, T, 128), F32), pltpu.VMEM((G, T, 2 * dh), F32),
                        pltpu.VMEM((G, T, T), BF16), pltpu.VMEM((G, T, 128), F32)],
        compiler_params=_params("parallel", "arbitrary"), name=name,
    )(p0, p0, kaug, p0, p0)


def _fox_post_bwd(dy, o, p0, *, H, name, tm=512):
    S = dy.shape[0]
    dh = HEAD_DIM
    tm = min(tm, S)
    G = POST_HEADS_PER_STEP
    assert H % G == 0

    def body(dy_ref, o_ref, g_ref, do_ref, dg_ref, da_ref):
        dyv = dy_ref[...].astype(F32)
        ov = o_ref[...]
        g = g_ref[...].astype(F32)
        do = (dyv * _silu(g)).astype(BF16)
        do_ref[...] = do
        dg_ref[...] = (dyv * ov * _dsilu(g)).astype(BF16)
        prod = do.astype(F32) * ov
        for k in range(G):
            hd = slice(k * dh, (k + 1) * dh)
            delta = jnp.sum(prod[:, hd], axis=-1, keepdims=True)
            hi, mid, lo = _split3(-jnp.broadcast_to(delta, (tm, AUG)))
            da_ref[:, hd] = _lane_select([hi, mid, lo], (tm, AUG))

    blk = pl.BlockSpec((tm, G * dh), lambda h, i: (i, h))
    return pl.pallas_call(
        body, grid=(H // G, S // tm),
        in_specs=[blk, blk, pl.BlockSpec((tm, G * dh), lambda h, i: (i, 3 * H // G + h))],
        out_specs=[blk, blk, blk],
        out_shape=[SDS((S, H * dh), BF16), SDS((S, H * dh), BF16), SDS((S, H * AUG), BF16)],
        compiler_params=_params("parallel", "parallel"), name=name,
    )(dy, o, p0)


def _fox_dq(p0, kaug, qaug, do, doaug, *, H, name):
    S = p0.shape[0]
    T = min(ATT_BLOCK, S)
    nq = S // T
    dh = HEAD_DIM
    scale = dh ** -0.5
    G = ATT_HEADS_PER_STEP
    assert H % G == 0

    def body(q_ref, qa_ref, k_ref, ka_ref, v_ref, do_ref, da_ref, dq_ref, rs_ref, acc_sc, ds_sc):
        i = pl.program_id(1)
        vaug = _lane_select([1.0, 1.0, 1.0], (T, AUG))
        ones = jnp.ones((T, dh), BF16)
        acc_sc[...] = jnp.zeros_like(acc_sc)
        ds_sc[...] = jnp.zeros_like(ds_sc)

        def step(j, masked):
            rows = pl.ds(pl.multiple_of(j * T, T), T)
            prev = pl.ds(pl.multiple_of(jnp.maximum(j - 1, 0) * T, T), T)
            for g in range(G):
                hd = slice(g * dh, (g + 1) * dh)
                acc_sc[g] += _dot(ds_sc[g], jnp.concatenate([k_ref[prev, hd], ones], axis=1))
                q = jnp.concatenate([q_ref[:, hd], qa_ref[:, hd]], axis=1)
                do = jnp.concatenate([do_ref[:, hd], da_ref[:, hd]], axis=1)
                p = jnp.exp2(_dot_nt(q, jnp.concatenate([k_ref[rows, hd], ka_ref[rows, hd]], axis=1)))
                if masked:
                    row = lax.broadcasted_iota(jnp.int32, (T, T), 0)
                    col = lax.broadcasted_iota(jnp.int32, (T, T), 1)
                    p = jnp.where(row >= col, p, 0.0)
                ds = p * _dot_nt(do, jnp.concatenate([v_ref[rows, hd], vaug], axis=1))
                ds_sc[g] = ds.astype(BF16)

        def loop_body(j, carry):
            step(j, False)
            return carry

        lax.fori_loop(0, i, loop_body, 0)
        step(i, True)
        rows = pl.ds(pl.multiple_of(i * T, T), T)
        for g in range(G):
            hd = slice(g * dh, (g + 1) * dh)
            acc = acc_sc[g] + _dot(ds_sc[g], jnp.concatenate([k_ref[rows, hd], ones], axis=1))
            dq_ref[:, hd] = (acc[:, :dh] * scale).astype(BF16)
            rs_ref[g] = acc[:, dh:dh + 1]

    blk = lambda off: pl.BlockSpec((T, G * dh), lambda h, i: (i, off // G + h))
    full = lambda off: pl.BlockSpec((S, G * dh), lambda h, i: (0, off // G + h))
    return pl.pallas_call(
        body, grid=(H // G, nq),
        in_specs=[blk(0), blk(0), full(H), full(0), full(2 * H), blk(0), blk(0)],
        out_specs=[blk(0), pl.BlockSpec((G, T, 1), lambda h, i: (h, i, 0))],
        out_shape=[SDS((S, H * dh), BF16), SDS((H, S, 1), F32)],
        scratch_shapes=[pltpu.VMEM((G, T, 2 * dh), F32), pltpu.VMEM((G, T, T), BF16)],
        compiler_params=_params("parallel", "arbitrary"), name=name,
    )(p0, qaug, p0, kaug, p0, do, doaug)


def _fox_dkv(p0, kaug, qaug, do, doaug, *, H, name):
    S = p0.shape[0]
    T = min(ATT_BLOCK, S)
    nq = S // T
    dh = HEAD_DIM
    G = ATT_HEADS_PER_STEP
    assert H % G == 0

    def body(q_ref, qa_ref, k_ref, ka_ref, v_ref, do_ref, da_ref, dk_ref, dv_ref, dc_ref, dk_sc, dv_sc,
             pt_sc, dst_sc):
        j = pl.program_id(1)
        vaug = _lane_select([1.0, 1.0, 1.0], (T, AUG))
        ones = jnp.ones((T, dh), BF16)
        dk_sc[...] = jnp.zeros_like(dk_sc)
        dv_sc[...] = jnp.zeros_like(dv_sc)
        pt_sc[...] = jnp.zeros_like(pt_sc)
        dst_sc[...] = jnp.zeros_like(dst_sc)

        def apply(prev):
            for g in range(G):
                hd = slice(g * dh, (g + 1) * dh)
                dv_sc[g] += _dot(pt_sc[g], do_ref[prev, hd])
                dk_sc[g] += _dot(dst_sc[g], jnp.concatenate([q_ref[prev, hd], ones], axis=1))

        def step(i, masked):
            rows = pl.ds(pl.multiple_of(i * T, T), T)
            apply(pl.ds(pl.multiple_of(jnp.maximum(i - 1, j) * T, T), T))
            for g in range(G):
                hd = slice(g * dh, (g + 1) * dh)
                k = jnp.concatenate([k_ref[:, hd], ka_ref[:, hd]], axis=1)
                v = jnp.concatenate([v_ref[:, hd], vaug], axis=1)
                pt = jnp.exp2(_dot_nt(k, jnp.concatenate([q_ref[rows, hd], qa_ref[rows, hd]], axis=1)))
                if masked:
                    row = lax.broadcasted_iota(jnp.int32, (T, T), 0)
                    col = lax.broadcasted_iota(jnp.int32, (T, T), 1)
                    pt = jnp.where(col >= row, pt, 0.0)
                dst = pt * _dot_nt(v, jnp.concatenate([do_ref[rows, hd], da_ref[rows, hd]], axis=1))
                pt_sc[g] = pt.astype(BF16)
                dst_sc[g] = dst.astype(BF16)

        step(j, True)

        def loop_body(i, carry):
            step(i, False)
            return carry

        lax.fori_loop(j + 1, nq, loop_body, 0)
        apply(pl.ds((nq - 1) * T, T))
        for g in range(G):
            hd = slice(g * dh, (g + 1) * dh)
            dk_ref[:, hd] = (dk_sc[g, :, :dh] * LN2).astype(BF16)
            dv_ref[:, hd] = dv_sc[g].astype(BF16)
            dc_ref[g] = -dk_sc[g, :, dh:dh + 1]

    blk = lambda off: pl.BlockSpec((T, G * dh), lambda h, j: (j, off // G + h))
    full = lambda off: pl.BlockSpec((S, G * dh), lambda h, j: (0, off // G + h))
    return pl.pallas_call(
        body, grid=(H // G, nq),
        in_specs=[full(0), full(0), blk(H), blk(0), blk(2 * H), full(0), full(0)],
        out_specs=[blk(0), blk(0), pl.BlockSpec((G, T, 1), lambda h, j: (h, j, 0))],
        out_shape=[SDS((S, H * dh), BF16), SDS((S, H * dh), BF16), SDS((H, S, 1), F32)],
        scratch_shapes=[pltpu.VMEM((G, T, 2 * dh), F32), pltpu.VMEM((G, T, dh), F32),
                        pltpu.VMEM((G, T, T), BF16), pltpu.VMEM((G, T, T), BF16)],
        compiler_params=_params("parallel", "arbitrary"), name=name,
    )(p0, qaug, p0, kaug, p0, do, doaug)


def _hgrn_levels(C, leaf):
    levels = []
    h = C // 2
    while h >= leaf:
        levels.append(h)
        h //= 2
    return levels


def _hgrn_sum_matrix(C, leaf):
    t = np.arange(C)[:, None]
    u = np.arange(C)[None, :]
    mats = [(u <= t), (u > t)]
    for h in _hgrn_levels(C, leaf):
        start = (t // (2 * h)) * (2 * h)
        mid = start + h - 1
        second = t > mid
        m = np.where(second, (u > mid) & (u <= t), (u > t) & (u <= mid))
        mats.append(m)
    lstart = (t // leaf) * leaf
    mats.append((u >= lstart) & (u <= t))
    return np.concatenate([m.astype(np.float32) for m in mats], axis=0)


def _hgrn_chunk_terms(qr, fz, lb, msum, C, leaf):
    levels = _hgrn_levels(C, leaf)
    sq = _silu(qr)
    t = jnp.exp(-jnp.abs(fz))
    r = 1.0 / (1.0 + t)
    pos = fz >= 0.0
    sp = jnp.where(pos, r, t * r)
    sn = jnp.where(pos, t * r, r)
    f = lb + (1.0 - lb) * sp
    lf = jnp.log(f)
    k = (1.0 - lb) * sn
    hi, lo = _split2(lf)
    dsum = _dot(msum, hi) + _dot(msum, lo)
    b = dsum[0:C]
    kdec = dsum[C:2 * C]
    rowi = lax.broadcasted_iota(jnp.int32, (C, 1), 0)
    lev = []
    for n, h in enumerate(levels):
        e = jnp.exp(dsum[(2 + n) * C:(3 + n) * C])
        second = (rowi % (2 * h)) >= h
        qm = jnp.where(second, sq * e, 0.0).astype(BF16)
        km = jnp.where(second, 0.0, k * e).astype(BF16)
        lev.append((h, e, second, qm, km))
    dleaf = dsum[(2 + len(levels)) * C:(3 + len(levels)) * C]
    eq = jnp.exp(dleaf)
    ek = jnp.exp(jnp.minimum(-dleaf, EXP_CLAMP))
    return dict(sq=sq, sp=sp, sn=sn, f=f, k=k, b=b, kdec=kdec, lev=lev, eq=eq, ek=ek,
                ql=(sq * eq).astype(BF16), kl=(k * ek).astype(BF16),
                qs=(sq * jnp.exp(b)).astype(BF16), ke=(k * jnp.exp(kdec)).astype(BF16),
                e_c=jnp.exp(b[C - 1:C, :]))


def _hgrn_masks(C, leaf, transposed):
    a = lax.broadcasted_iota(jnp.int32, (C, C), 0)
    bb = lax.broadcasted_iota(jnp.int32, (C, C), 1)
    t, s = (bb, a) if transposed else (a, bb)
    lev = [None if 2 * h == C else (t // (2 * h)) == (s // (2 * h)) for h in _hgrn_levels(C, leaf)]
    if leaf == C:
        leafm = s <= t
    else:
        leafm = ((t // leaf) == (s // leaf)) & (s <= t)
    return lev, leafm


def _hgrn_fwd(p1, f1, lb, onorm, *, H, name, tb=512):
    S = p1.shape[0]
    dk = HEAD_DIM
    C = min(HGRN_CHUNK, S)
    leaf = min(HGRN_LEAF, C)
    tb = min(tb, S)
    nc = tb // C
    G = HGRN_HEADS_PER_STEP
    assert H % G == 0
    msum = jnp.asarray(_hgrn_sum_matrix(C, leaf), BF16)

    def body(q_ref, f_ref, v_ref, g_ref, lb_ref, on_ref, ms_ref, o_ref, y_ref, st_ref, st_sc):
        @pl.when(pl.program_id(1) == 0)
        def _():
            st_sc[...] = jnp.zeros_like(st_sc)

        msv = ms_ref[...]
        lmask, leafm = _hgrn_masks(C, leaf, False)

        def chunk(n, carry):
            rows = pl.ds(pl.multiple_of(n * C, C), C)
            for g in range(G):
                hd = slice(g * dk, (g + 1) * dk)
                tm = _hgrn_chunk_terms(q_ref[rows, hd].astype(F32), f_ref[rows, hd], lb_ref[:, hd], msv, C, leaf)
                v = v_ref[rows, hd]
                st = st_sc[g]
                st_ref[g, n] = st
                a = jnp.where(leafm, _dot_nt(tm["ql"], tm["kl"]), 0.0)
                for (h, e, second, qm, km), m in zip(tm["lev"], lmask):
                    al = _dot_nt(qm, km)
                    a = a + (al if m is None else jnp.where(m, al, 0.0))
                o = _dot_nt(tm["qs"], st.astype(BF16)) + _dot(a.astype(BF16), v)
                st_sc[g] = st * tm["e_c"] + _dot(v.T, tm["ke"])
                o_ref[rows, hd] = o
                rn = lax.rsqrt(jnp.mean(o * o, axis=-1, keepdims=True) + EPS)
                y = ((o * rn) * on_ref[:, hd]) * _silu(g_ref[rows, hd].astype(F32))
                y_ref[rows, hd] = y.astype(BF16)
            return carry

        lax.fori_loop(0, nc, chunk, 0)

    blk = lambda off: pl.BlockSpec((tb, G * dk), lambda h, i: (i, off // G + h))
    vec = pl.BlockSpec((1, G * dk), lambda h, i: (0, h))
    return pl.pallas_call(
        body, grid=(H // G, S // tb),
        in_specs=[blk(0), blk(0), blk(H), blk(2 * H), vec, vec,
                  pl.BlockSpec(msum.shape, lambda h, i: (0, 0))],
        out_specs=[blk(0), blk(0), pl.BlockSpec((G, nc, dk, dk), lambda h, i: (h, i, 0, 0))],
        out_shape=[SDS((S, H * dk), F32), SDS((S, H * dk), BF16), SDS((H, S // C, dk, dk), F32)],
        scratch_shapes=[pltpu.VMEM((G, dk, dk), F32)],
        compiler_params=_params("parallel", "arbitrary"), name=name,
    )(p1, f1, p1, p1, lb, onorm, msum)


def _hgrn_post_bwd(dy, o, p1, onorm, *, H, name, tm=512):
    S = dy.shape[0]
    dk = HEAD_DIM
    tm = min(tm, S)
    G = POST_HEADS_PER_STEP
    assert H % G == 0

    def body(dy_ref, o_ref, g_ref, on_ref, do_ref, dg_ref, don_ref):
        @pl.when(pl.program_id(1) == 0)
        def _():
            don_ref[...] = jnp.zeros_like(don_ref)

        for k in range(G):
            hd = slice(k * dk, (k + 1) * dk)
            dyv = dy_ref[:, hd].astype(F32)
            ov = o_ref[:, hd]
            g = g_ref[:, hd].astype(F32)
            onv = on_ref[:, hd]
            rn = lax.rsqrt(jnp.mean(ov * ov, axis=-1, keepdims=True) + EPS)
            oh = ov * rn
            dn = dyv * _silu(g)
            dg_ref[:, hd] = (dyv * (oh * onv) * _dsilu(g)).astype(BF16)
            don_ref[:, hd] += jnp.sum(dn * oh, axis=0, keepdims=True)
            doh = dn * onv
            do_ref[:, hd] = (rn * (doh - oh * jnp.mean(doh * oh, axis=-1, keepdims=True))).astype(BF16)

    blk = pl.BlockSpec((tm, G * dk), lambda h, i: (i, h))
    vec = pl.BlockSpec((1, G * dk), lambda h, i: (0, h))
    return pl.pallas_call(
        body, grid=(H // G, S // tm),
        in_specs=[blk, blk, pl.BlockSpec((tm, G * dk), lambda h, i: (i, 2 * H // G + h)), vec],
        out_specs=[blk, blk, vec],
        out_shape=[SDS((S, H * dk), BF16), SDS((S, H * dk), BF16), SDS((1, H * dk), F32)],
        compiler_params=_params("parallel", "arbitrary"), name=name,
    )(dy, o, p1, onorm)


def _hgrn_bwd(p1, f1, lb, do, states, *, H, name, tb=512):
    S = p1.shape[0]
    dk = HEAD_DIM
    C = min(HGRN_CHUNK, S)
    leaf = min(HGRN_LEAF, C)
    tb = min(tb, S)
    nc = tb // C
    nb = S // tb
    G = HGRN_HEADS_PER_STEP
    assert H % G == 0
    msum = jnp.asarray(_hgrn_sum_matrix(C, leaf), BF16)
    rtri = jnp.asarray(np.triu(np.ones((C, C), np.float32)), BF16)

    def body(q_ref, f_ref, v_ref, do_ref, st_ref, lb_ref, ms_ref, rt_ref,
             dq_ref, df_ref, dv_ref, dlb_ref, g_sc):
        @pl.when(pl.program_id(1) == 0)
        def _():
            g_sc[...] = jnp.zeros_like(g_sc)
            dlb_ref[...] = jnp.zeros_like(dlb_ref)

        msv = ms_ref[...]
        rtv = rt_ref[...]
        lmask, leafm = _hgrn_masks(C, leaf, False)
        lmask_t, leafm_t = _hgrn_masks(C, leaf, True)
        f32 = lambda z: z.astype(F32)

        def head_chunk(g, n):
            hd = slice(g * dk, (g + 1) * dk)
            rows = pl.ds(pl.multiple_of(n * C, C), C)
            lbv = lb_ref[:, hd]
            qr = q_ref[rows, hd].astype(F32)
            tm = _hgrn_chunk_terms(qr, f_ref[rows, hd], lbv, msv, C, leaf)
            v = v_ref[rows, hd]
            dov = do_ref[rows, hd]
            st0 = st_ref[g, n]
            gt = g_sc[g]
            gtb = gt.astype(BF16)
            da = _dot_nt(dov, v)
            da_t = _dot_nt(v, dov)

            dal = jnp.where(leafm, da, 0.0).astype(BF16)
            dal_t = jnp.where(leafm_t, da_t, 0.0).astype(BF16)
            dql = _dot(dal, tm["kl"])
            dkl = _dot(dal_t, tm["ql"])
            dsq = dql * tm["eq"]
            dkk = dkl * tm["ek"]
            xq = f32(tm["ql"]) * dql
            xk = f32(tm["kl"]) * dkl
            a_t = jnp.where(leafm_t, _dot_nt(tm["kl"], tm["ql"]), 0.0)
            for (h, e, second, qm, km), m, m_t in zip(tm["lev"], lmask, lmask_t):
                dl = (da if m is None else jnp.where(m, da, 0.0)).astype(BF16)
                dl_t = (da_t if m_t is None else jnp.where(m_t, da_t, 0.0)).astype(BF16)
                dqm = _dot(dl, km)
                dkm = _dot(dl_t, qm)
                dsq = dsq + jnp.where(second, dqm * e, 0.0)
                dkk = dkk + jnp.where(second, 0.0, dkm * e)
                xq = xq + f32(qm) * dqm
                xk = xk + f32(km) * dkm
                al_t = _dot_nt(km, qm)
                a_t = a_t + (al_t if m_t is None else jnp.where(m_t, al_t, 0.0))
            dqs = _dot(dov, st0.astype(BF16))
            dke = _dot(v, gtb)
            dsq = dsq + dqs * jnp.exp(tm["b"])
            dkk = dkk + dke * jnp.exp(tm["kdec"])
            xq = xq + f32(tm["qs"]) * dqs
            xk = xk + f32(tm["ke"]) * dke
            dvv = _dot(a_t.astype(BF16), dov) + _dot_nt(tm["ke"], gtb)
            r_end = jnp.sum(f32(gtb) * _dot(v.T, tm["ke"]) + gt * (st0 * tm["e_c"]), axis=0, keepdims=True)
            g_sc[g] = gt * tm["e_c"] + _dot(dov.T, tm["qs"])
            xh, xm, xl = _split3(xq - xk)
            dlf = (_dot(rtv, xh) + _dot(rtv, xm)) + _dot(rtv, xl) + r_end
            dlf_f = dlf / tm["f"]
            dsp = (1.0 - lbv) * (dlf_f - dkk)
            df_ref[rows, hd] = (dsp * (tm["sp"] * tm["sn"])).astype(BF16)
            dq_ref[rows, hd] = (dsq * _dsilu(qr)).astype(BF16)
            dv_ref[rows, hd] = dvv.astype(BF16)
            dlb_ref[:, hd] += jnp.sum(dlf_f * tm["sn"] - dkk * tm["sn"], axis=0, keepdims=True)

        def chunk(nn, carry):
            for g in range(G):
                head_chunk(g, nc - 1 - nn)
            return carry

        lax.fori_loop(0, nc, chunk, 0)

    blk = lambda off: pl.BlockSpec((tb, G * dk), lambda h, i: (nb - 1 - i, off // G + h))
    vec = pl.BlockSpec((1, G * dk), lambda h, i: (0, h))
    return pl.pallas_call(
        body, grid=(H // G, nb),
        in_specs=[blk(0), blk(0), blk(H), blk(0),
                  pl.BlockSpec((G, nc, dk, dk), lambda h, i: (h, nb - 1 - i, 0, 0)), vec,
                  pl.BlockSpec(msum.shape, lambda h, i: (0, 0)), pl.BlockSpec((C, C), lambda h, i: (0, 0))],
        out_specs=[blk(0), blk(0), blk(0), vec],
        out_shape=[SDS((S, H * dk), BF16)] * 3 + [SDS((1, H * dk), F32)],
        scratch_shapes=[pltpu.VMEM((G, dk, dk), F32)],
        compiler_params=_params("parallel", "arbitrary"), name=name,
    )(p1, f1, p1, do, states, lb, msum, rtri)


def _lb_fwd(logits, *, name):
    W = logits.shape[1]

    def body(l_ref, lb_ref):
        l = l_ref[...]
        m = jnp.max(l, axis=0, keepdims=True)
        e = jnp.exp(l - m)
        p = e / jnp.sum(e, axis=0, keepdims=True)
        lb_ref[...] = (p[0:1] + p[1:2]) - p[0:1]

    return pl.pallas_call(body, out_shape=SDS((1, W), F32), name=name)(logits)


STAT_ROWS = 8


def _stats_reduce(stats_all, logits, *, name):
    W = logits.shape[1]

    def body(s_ref, l_ref, g_ref):
        tot = s_ref[0]
        for d in range(1, N_DEV):
            tot = tot + s_ref[d]
        l = l_ref[...]
        m = jnp.max(l, axis=0, keepdims=True)
        e = jnp.exp(l - m)
        p = e / jnp.sum(e, axis=0, keepdims=True)
        dlb = tot[2:3]
        dl0 = -(p[0:1] * p[1:2]) * dlb
        dl1 = (p[1:2] * (1.0 - p[1:2])) * dlb
        g_ref[0:2] = tot[0:2]
        g_ref[2:3] = dl0
        g_ref[3:4] = dl1
        g_ref[4:7] = tot[3:6]
        g_ref[7:8] = jnp.zeros((1, W), F32)

    return pl.pallas_call(body, out_shape=SDS((STAT_ROWS, W), F32), name=name)(stats_all, logits)


def _adamw(w, m, v, g_parts, *, name, tr=128):
    R, C = w.shape
    ns = len(g_parts)
    n, Rs = g_parts[0].shape[0], g_parts[0].shape[1]
    assert all(p.shape == (n, Rs, C) for p in g_parts) and ns * Rs == R
    tr = min(tr, Rs)
    assert Rs % tr == 0
    nts = Rs // tr
    c1 = 1.0 / (1.0 - ADAM_B1 ** ADAM_STEP)
    c2 = 1.0 / (1.0 - ADAM_B2 ** ADAM_STEP)

    def body(*refs):
        w_ref, m_ref, v_ref = refs[:3]
        g_refs = refs[3:3 + ns]
        go_ref, d_ref, mo_ref, vo_ref = refs[3 + ns:]

        def update(g_ref):
            g = g_ref[0].astype(F32)
            for k in range(1, n):
                g = g + g_ref[k].astype(F32)
            mn = ADAM_B1 * m_ref[...] + (1.0 - ADAM_B1) * g
            vn = ADAM_B2 * v_ref[...] + (1.0 - ADAM_B2) * (g * g)
            d_ref[...] = -ADAM_LR * ((mn * c1) / (jnp.sqrt(vn * c2) + ADAM_EPS) + ADAM_WD * w_ref[...])
            go_ref[...] = g
            mo_ref[...] = mn
            vo_ref[...] = vn

        for s in range(ns):
            if ns == 1:
                update(g_refs[s])
            else:
                pl.when(pl.program_id(0) // nts == s)(functools.partial(update, g_refs[s]))

    def g_map(i, s):
        return (0, jnp.clip(i - s * nts, 0, nts - 1), 0)

    blk = pl.BlockSpec((tr, C), lambda i: (i, 0))
    return pl.pallas_call(
        body, grid=(R // tr,),
        in_specs=[blk, blk, blk] + [pl.BlockSpec((n, tr, C), functools.partial(g_map, s=s)) for s in range(ns)],
        out_specs=[blk] * 4, out_shape=[SDS((R, C), F32)] * 4,
        compiler_params=_params("parallel"), name=name,
    )(w, m, v, *g_parts)


ANY = pl.BlockSpec(memory_space=pl.ANY)
STAGE_BYTES = 2 * 1024 * 1024


def _stage_shape(shape, dtype):
    row_bytes = int(np.prod(shape[1:])) * jnp.dtype(dtype).itemsize
    rows = max(1, min(shape[0], STAGE_BYTES // row_bytes))
    while shape[0] % rows:
        rows -= 1
    return (rows,) + tuple(shape[1:])


def _staged_copy(frm, to, buf, sems):
    rows = buf.shape[0]
    for r0 in range(0, frm.shape[0], rows):
        cp = pltpu.make_async_copy(frm.at[pl.ds(r0, rows)], buf, sems.at[0])
        cp.start()
        cp.wait()
        cp = pltpu.make_async_copy(buf, to.at[pl.ds(r0, rows)], sems.at[1])
        cp.start()
        cp.wait()


def _all_gather(shards, out_shapes, views, *, name):
    n = len(shards)

    def body(*refs):
        ins, outs = refs[:n], refs[n:2 * n]
        send_sems, recv_sems, local_sems = refs[2 * n:2 * n + 3]
        bufs = refs[2 * n + 3:]
        x, y, c = lax.axis_index("x"), lax.axis_index("y"), lax.axis_index("c")
        me, sibling = (x, y, c), (x, y, 1 - c)
        chips = [(1 - x, y), (x, 1 - y), (1 - x, 1 - y)]

        def dev(p):
            return 4 * p[0] + 2 * p[1] + p[2]

        def copy(a, k, block, to, src=None):
            dst = views[a](outs[a], dev(block))
            return pltpu.make_async_remote_copy(
                src_ref=dst if src is None else src, dst_ref=dst,
                send_sem=send_sems.at[a, k], recv_sem=recv_sems.at[a, k],
                device_id=to, device_id_type=MESH)

        first, passed = [], []
        for a in range(n):
            first.append(copy(a, 0, me, sibling, src=ins[a]))
            first += [copy(a, 1 + j, me, (*chip, c), src=ins[a]) for j, chip in enumerate(chips)]
        for cp in first:
            cp.start()
        for a in range(n):
            _staged_copy(ins[a], views[a](outs[a], dev(me)), bufs[a], local_sems)
        for j, chip in enumerate(chips):
            for a in range(n):
                copy(a, 1 + j, (*chip, c), me).wait_recv()
                cp = copy(a, 4 + j, (*chip, c), sibling)
                cp.start()
                passed.append(cp)
        for a in range(n):
            copy(a, 0, sibling, me).wait_recv()
            for j, chip in enumerate(chips):
                copy(a, 4 + j, (*chip, 1 - c), me).wait_recv()
        for cp in first + passed:
            cp.wait_send()

    return pl.pallas_call(
        body, in_specs=[ANY] * n, out_specs=[ANY] * n, out_shape=list(out_shapes),
        scratch_shapes=[pltpu.SemaphoreType.DMA((n, 7)), pltpu.SemaphoreType.DMA((n, 7)),
                        pltpu.SemaphoreType.DMA((2,))]
        + [pltpu.VMEM(_stage_shape(s.shape, s.dtype), s.dtype) for s in shards],
        name=name,
    )(*shards)


HBM = pl.BlockSpec(memory_space=pltpu.HBM)
SEM = pl.BlockSpec(memory_space=pltpu.SEMAPHORE)
EFFECT = pltpu.SideEffectType.DATAFLOW_SIDE_EFFECTING


def _relations(x, y, c):
    for m in range(1, N_DEV):
        yield m, (1 - x if m & 4 else x, 1 - y if m & 2 else y, 1 - c if m & 1 else c)


def _dev_id(p):
    return 4 * p[0] + 2 * p[1] + p[2]


def _send_start(srcs, land_shapes, src_views, dst_views, *, name):
    n = len(srcs)

    def body(*refs):
        ins, lands = refs[:n], refs[n:2 * n]
        send_sems, recv_sems, token = refs[2 * n], refs[2 * n + 1], refs[-1]
        x, y, c = lax.axis_index("x"), lax.axis_index("y"), lax.axis_index("c")
        me = _dev_id((x, y, c))
        for m, p in _relations(x, y, c):
            for a in range(n):
                pltpu.make_async_remote_copy(
                    src_ref=src_views[a](ins[a], me, _dev_id(p), m), dst_ref=dst_views[a](lands[a], me, m),
                    send_sem=send_sems.at[a * (N_DEV - 1) + m - 1], recv_sem=recv_sems.at[a * (N_DEV - 1) + m - 1],
                    device_id=p, device_id_type=MESH).start()
        token[...] = jnp.zeros_like(token)

    lands = [pltpu.with_memory_space_constraint(lax.empty(s.shape, s.dtype), pltpu.HBM) for s in land_shapes]
    srcs = [pltpu.with_memory_space_constraint(v, pltpu.HBM) for v in srcs]
    res = pl.pallas_call(
        body, name=name,
        out_shape=[pltpu.SemaphoreType.DMA((n * (N_DEV - 1),)), pltpu.SemaphoreType.DMA((n * (N_DEV - 1),))]
        + [pltpu.HBM(v.shape, v.dtype) for v in srcs] + [pltpu.HBM(s.shape, s.dtype) for s in land_shapes]
        + [SDS((8, 128), F32)],
        in_specs=[HBM] * (2 * n), out_specs=[SEM, SEM] + [HBM] * (2 * n) + [pl.BlockSpec(memory_space=pltpu.VMEM)],
        input_output_aliases={i: 2 + i for i in range(2 * n)},
        compiler_params=pltpu.CompilerParams(has_side_effects=EFFECT),
    )(*srcs, *lands)
    return res[0], res[1], res[2:2 + n], res[2 + n:2 + 2 * n], res[-1]


def _send_wait(started, src_views, dst_views, own_views, own_shapes, after, *, name):
    send_sems, recv_sems, srcs, lands, _ = started
    n = len(srcs)

    def body(*refs):
        ins, lnd = refs[:n], refs[n:2 * n]
        send_sems, recv_sems = refs[2 * n], refs[2 * n + 1]
        got = refs[2 * n + 3 + n:2 * n + 3 + 2 * n]
        local_sems = refs[2 * n + 3 + 2 * n]
        bufs = refs[2 * n + 4 + 2 * n:]
        x, y, c = lax.axis_index("x"), lax.axis_index("y"), lax.axis_index("c")
        me = _dev_id((x, y, c))
        for m, p in _relations(x, y, c):
            for a in range(n):
                cp = pltpu.make_async_remote_copy(
                    src_ref=src_views[a](ins[a], me, _dev_id(p), m), dst_ref=dst_views[a](lnd[a], me, m),
                    send_sem=send_sems.at[a * (N_DEV - 1) + m - 1], recv_sem=recv_sems.at[a * (N_DEV - 1) + m - 1],
                    device_id=p, device_id_type=MESH)
                cp.wait_send()
                cp.wait_recv()
        for a in range(n):
            frm, to = own_views[a](ins[a], got[a], me)
            _staged_copy(frm, to, bufs[a], local_sems)

    res = pl.pallas_call(
        body, name=name,
        out_shape=[pltpu.HBM(v.shape, v.dtype) for v in srcs] + [pltpu.HBM(v.shape, v.dtype) for v in lands],
        in_specs=[HBM] * (2 * n) + [SEM, SEM, ANY], out_specs=[HBM] * (2 * n),
        input_output_aliases={i: i for i in range(2 * n)},
        scratch_shapes=[pltpu.SemaphoreType.DMA((2,))]
        + [pltpu.VMEM(_stage_shape(s, v.dtype), v.dtype) for s, v in zip(own_shapes, srcs)],
        compiler_params=pltpu.CompilerParams(has_side_effects=EFFECT),
    )(*srcs, *lands, send_sems, recv_sems, after)
    return res[n:]


def kernel(x, norm_gains, fox_w_in, fox_b_f, hgrn_w_in, hgrn_lb_logits, hgrn_onorm, w_out, final_gain, loss_target, m_norm_gains, m_fox_w_in, m_fox_b_f, m_hgrn_w_in, m_hgrn_lb_logits, m_hgrn_onorm, m_w_out, m_final_gain, v_norm_gains, v_fox_w_in, v_fox_b_f, v_hgrn_w_in, v_hgrn_lb_logits, v_hgrn_onorm, v_w_out, v_final_gain):
    _, S, D = x.shape
    H = FOX_HEADS
    W = H * HEAD_DIM
    assert HGRN_HEADS == H and w_out.shape[2] == D
    cf = fox_w_in.shape[2]
    ch = hgrn_w_in.shape[2]
    ro = w_out.shape[1]
    co = hgrn_onorm.shape[1]
    assert N_DEV * cf == 4 * W + H and N_DEV * ch == 4 * W and N_DEV * ro == W and N_DEV * co == W
    x2 = x.reshape(S, D)
    tgt = loss_target.reshape(S, D)

    col = lambda n: (lambda r, i: r.at[:, pl.ds(pl.multiple_of(i * n, n), n)])
    row = lambda n: (lambda r, i: r.at[pl.ds(pl.multiple_of(i * n, n), n), :])
    late_views = [col(ch), row(ro), col(co)]
    late = _send_start(
        [hgrn_w_in[0].astype(BF16), w_out[1].astype(BF16), hgrn_onorm],
        [SDS((D, 4 * W), BF16), SDS((W, D), BF16), SDS((1, W), F32)],
        [lambda r, me, p, m: r] * 3, [lambda r, me, m, v=v: v(r, me) for v in late_views],
        name="gather_layer1_start")
    ng0 = norm_gains[0:1] + late[4][0:1, 0:1]

    wf_g, wo0 = _all_gather(
        [fox_w_in[0].astype(BF16), w_out[0].astype(BF16)], [SDS((N_DEV, D, cf), BF16), SDS((W, D), BF16)],
        [lambda r, p: r.at[p], row(ro)], name="gather_layer0")
    wf = jnp.transpose(wf_g, (1, 0, 2)).reshape(D, N_DEV * cf)
    wf_main = jnp.concatenate([wf[:, :3 * W], wf[:, 3 * W + H:]], axis=1)
    wfl_t = wf[:, 3 * W:3 * W + H].T

    h0 = _rms_fwd(x2, ng0, name="rms0_fwd")
    p0 = _mm_nn([h0], wf_main, BF16, scale_cols=(W, LOG2E * HEAD_DIM ** -0.5), name="fox_in_proj")
    fl_t = _mm_nt_rows(wfl_t, h0, name="fox_forget_proj")
    b_col = fox_b_f.reshape(H, 1)
    kaug = _fox_key_aug(*_fox_gate_fwd(fl_t, b_col, name="fox_gate_fwd"))
    o0, y0, qaug = _fox_fwd(p0, kaug, H=H, name="fox_attn_fwd")
    x1 = _mm_nn([y0], wo0, F32, residual=x2, name="fox_out_proj")

    wh, wo1, onorm = _send_wait(
        late, [lambda r, me, p, m: r] * 3, [lambda r, me, m, v=v: v(r, me) for v in late_views],
        [lambda src, land, me, v=v: (src, v(land, me)) for v in late_views], [(D, ch), (ro, D), (1, co)],
        x1[0:8], name="gather_layer1_wait")
    wh_qig = jnp.concatenate([wh[:, :W], wh[:, 2 * W:]], axis=1)
    wh_f = wh[:, W:2 * W]
    lb = _lb_fwd(hgrn_lb_logits, name="hgrn_lower_bound")
    h1 = _rms_fwd(x1, norm_gains[1:2], name="rms1_fwd")
    p1 = _mm_nn([h1], wh_qig, BF16, name="hgrn_in_proj")
    f1 = _mm_nn([h1], wh_f, F32, name="hgrn_forget_proj")
    o1, y1, states = _hgrn_fwd(p1, f1, lb, onorm, H=H, name="hgrn_fwd")
    xo = _mm_nn([y1], wo1, F32, residual=x1, name="hgrn_out_proj")

    dx2, dx2b, loss_part, dgf = _loss_head(xo, final_gain.reshape(1, D), tgt, name="loss_head")
    loss = lax.psum(jnp.sum(loss_part), ("x", "y", "c"))

    dy1 = _mm_nn([dx2b], wo1, BF16, b_t=True, name="hgrn_out_proj_dx")
    dwo1 = _mm_tn(y1, [dx2b], BF16, name="hgrn_out_proj_dw")
    do1, dg1, donorm = _hgrn_post_bwd(dy1, o1, p1, onorm, H=H, name="hgrn_post_bwd")
    dq1, df1, di1, dlb = _hgrn_bwd(p1, f1, lb, do1, states, H=H, name="hgrn_bwd")
    segs1 = [dq1, df1, di1, dg1]
    dh1 = _mm_nn(segs1, wh, BF16, b_t=True, name="hgrn_in_proj_dx")
    dwh = _mm_tn(h1, segs1, BF16, name="hgrn_in_proj_dw")
    part_views = [col(ch), row(ro)]
    slot = lambda r, me, m: r.at[m]
    ex1 = _send_start([dwh, dwo1], [SDS((N_DEV, D, ch), BF16), SDS((N_DEV, ro, D), BF16)],
                      [lambda r, me, p, m, v=v: v(r, p) for v in part_views], [slot] * 2,
                      name="exchange_layer1_start")
    ng1 = norm_gains[1:2] + ex1[4][0:1, 0:1]
    dx1, dx1b, dng1 = _rms_bwd(x1, ng1, dh1, dx2, name="rms1_bwd")

    dy0 = _mm_nn([dx1b], wo0, BF16, b_t=True, name="fox_out_proj_dx")
    dwo0 = _mm_tn(y0, [dx1b], BF16, name="fox_out_proj_dw")
    do0, dg0, doaug = _fox_post_bwd(dy0, o0, p0, H=H, name="fox_post_bwd")
    dq0, dc_row = _fox_dq(p0, kaug, qaug, do0, doaug, H=H, name="fox_attn_dq")
    dk0, dv0, dc_key = _fox_dkv(p0, kaug, qaug, do0, doaug, H=H, name="fox_attn_dkv")
    dfl_t, dbf = _fox_gate_bwd(dc_row.reshape(H, S), dc_key.reshape(H, S), fl_t, b_col, name="fox_gate_bwd")
    dfl_tb = dfl_t.astype(BF16)
    dwfl_t = _mm_nn([dfl_tb], h0, BF16, name="fox_forget_proj_dw")
    segs0 = [dq0, dk0, dv0, dg0]
    dwf_main = _mm_tn(h0, segs0, BF16, name="fox_in_proj_dw")
    dwf = jnp.concatenate([dwf_main[:, :3 * W], dwfl_t.T, dwf_main[:, 3 * W:]], axis=1)
    dwf_blocks = jnp.transpose(dwf.reshape(D, N_DEV, cf), (1, 0, 2))
    ex0 = _send_start([dwf_blocks, dwo0], [SDS((N_DEV, D, cf), BF16), SDS((N_DEV, ro, D), BF16)],
                      [lambda r, me, p, m: r.at[p], lambda r, me, p, m: row(ro)(r, p)], [slot] * 2,
                      name="exchange_layer0_start")
    wfl_t0 = wfl_t + ex0[4][0:1, 0:1].astype(BF16)
    dh0_f = _mm_nn([dfl_tb.T], wfl_t0, BF16, name="fox_forget_proj_dx")
    dh0 = _mm_nn(segs0, wf_main, BF16, residual=dh0_f, b_t=True, name="fox_in_proj_dx")
    grad_x, _, dng0 = _rms_bwd(x2, norm_gains[0:1], dh0, dx1, name="rms0_bwd")

    own1 = [lambda src, land, me, v=v: (v(src, me), land.at[0]) for v in part_views]
    rh, ro1 = _send_wait(ex1, [lambda r, me, p, m, v=v: v(r, p) for v in part_views], [slot] * 2, own1,
                         [(D, ch), (ro, D)], dng0, name="exchange_layer1_wait")
    rf, ro0 = _send_wait(ex0, [lambda r, me, p, m: r.at[p], lambda r, me, p, m: row(ro)(r, p)], [slot] * 2,
                         [lambda src, land, me: (src.at[me], land.at[0]),
                          lambda src, land, me: (row(ro)(src, me), land.at[0])],
                         [(D, cf), (ro, D)], dng0, name="exchange_layer0_wait")

    pad = lambda a: jnp.pad(a, ((0, 0), (0, W - a.shape[1])))
    stats = jnp.concatenate([dng0, dng1, dlb, dgf, pad(dbf.reshape(1, H)), donorm,
                             jnp.zeros((2, W), F32)], axis=0)
    assert D == W
    (stats_all,) = _all_gather([stats], [SDS((N_DEV, STAT_ROWS, W), F32)], [lambda r, p: r.at[p]],
                               name="gather_small_grads")
    g_small = _stats_reduce(stats_all, hgrn_lb_logits, name="reduce_small_grads")
    me = 4 * lax.axis_index("x") + 2 * lax.axis_index("y") + lax.axis_index("c")
    g_onorm = lax.dynamic_slice_in_dim(g_small[6:7], me * co, co, axis=1)

    def upd(w, m, v, parts, name):
        shp = w.shape
        r2 = (-1, shp[-1])
        g, d, mn, vn = _adamw(w.reshape(r2), m.reshape(r2), v.reshape(r2), parts, name=name)
        return g.reshape(shp), d.reshape(shp), mn.reshape(shp), vn.reshape(shp)

    res = {
        "norm_gains": upd(norm_gains, m_norm_gains, v_norm_gains, [g_small[None, 0:2]], "adamw_norm_gains"),
        "fox_w_in": upd(fox_w_in, m_fox_w_in, v_fox_w_in, [rf], "adamw_fox_w_in"),
        "fox_b_f": upd(fox_b_f, m_fox_b_f, v_fox_b_f, [g_small[None, 5:6, :H]], "adamw_fox_b_f"),
        "hgrn_w_in": upd(hgrn_w_in, m_hgrn_w_in, v_hgrn_w_in, [rh], "adamw_hgrn_w_in"),
        "hgrn_lb_logits": upd(hgrn_lb_logits, m_hgrn_lb_logits, v_hgrn_lb_logits, [g_small[None, 2:4]],
                              "adamw_hgrn_lb_logits"),
        "hgrn_onorm": upd(hgrn_onorm, m_hgrn_onorm, v_hgrn_onorm, [g_onorm[None]], "adamw_hgrn_onorm"),
        "w_out": upd(w_out, m_w_out, v_w_out, [ro0, ro1], "adamw_w_out"),
        "final_gain": upd(final_gain.reshape(1, D), m_final_gain.reshape(1, D), v_final_gain.reshape(1, D),
                          [g_small[None, 4:5]], "adamw_final_gain"),
    }
    order = ["norm_gains", "fox_w_in", "fox_b_f", "hgrn_w_in", "hgrn_lb_logits", "hgrn_onorm", "w_out", "final_gain"]
    fix = lambda n, a: a.reshape(D) if n == "final_gain" else a
    outs = [loss, grad_x.reshape(1, S, D)]
    for k in range(4):
        outs += [fix(n, res[n][k]) for n in order]
    return tuple(outs)
```

```python
import functools

import numpy as np
import jax
import jax.numpy as jnp
from jax import lax
from jax.experimental import pallas as pl
from jax.experimental.pallas import tpu as pltpu

F32 = jnp.float32
BF16 = jnp.bfloat16
SDS = jax.ShapeDtypeStruct
MESH = pl.DeviceIdType.MESH

EPS = 1e-6
ADAM_LR, ADAM_B1, ADAM_B2, ADAM_EPS, ADAM_WD, ADAM_STEP = 0.001, 0.9, 0.999, 1e-08, 0.01, 10

N_DEV = 8
FOX_HEADS = 16
HGRN_HEADS = 16
HEAD_DIM = 128
HGRN_CHUNK = 128
HGRN_LEAF = 16
HGRN_HEADS_PER_STEP = 4
EXP_CLAMP = 85.0
ATT_BLOCK = 512
ATT_HEADS_PER_STEP = 2
POST_HEADS_PER_STEP = 4
NEG = -1e30
LOG2E = 1.4426950408889634
LN2 = 0.6931471805599453

VMEM_LIMIT_V7X = 56 * 1024 * 1024


def _params(*sem):
    return pltpu.CompilerParams(dimension_semantics=sem, vmem_limit_bytes=VMEM_LIMIT_V7X)


def _silu(x):
    return x * jax.nn.sigmoid(x)


def _dsilu(x):
    s = jax.nn.sigmoid(x)
    return s * (1.0 + x * (1.0 - s))


def _dot(a, b):
    return jnp.dot(a, b, preferred_element_type=F32)


def _dot_nt(a, b):
    return lax.dot_general(a, b, (((1,), (1,)), ((), ())), preferred_element_type=F32)


def _dot_tn(a, b):
    return lax.dot_general(a, b, (((0,), (0,)), ((), ())), preferred_element_type=F32)


def _mm_nn(a_list, b, out_dtype, *, name, residual=None, scale_cols=None, b_t=False, tm=1024, tn=1024, tk=2048):
    ns = len(a_list)
    M, Ks = a_list[0].shape
    K, N = (b.shape[1], b.shape[0]) if b_t else b.shape
    dot = _dot_nt if b_t else _dot
    assert K == ns * Ks and all(a.shape == (M, Ks) for a in a_list)
    if ns > 1:
        tk = tk // 2
    tm, tn, tk = min(tm, M), min(tn, N), min(tk, Ks)
    assert M % tm == 0 and N % tn == 0 and Ks % tk == 0
    assert scale_cols is None or scale_cols[0] % tn == 0
    nks = Ks // tk
    nk = ns * nks
    has_res = residual is not None

    def body(*refs):
        a_refs, b_ref = refs[:ns], refs[ns]
        res_ref = refs[ns + 1] if has_res else None
        o_ref = refs[ns + 1 + has_res]

        def finish(r):
            if has_res:
                r = r + res_ref[...].astype(F32)
            if scale_cols is not None:
                r = r * jnp.where(pl.program_id(1) < scale_cols[0] // tn, scale_cols[1], 1.0)
            o_ref[...] = r.astype(out_dtype)

        if nk == 1:
            finish(dot(a_refs[0][...], b_ref[...]))
            return
        acc_ref = refs[ns + 2 + has_res]
        k = pl.program_id(2)

        @pl.when(k == 0)
        def _():
            acc_ref[...] = jnp.zeros_like(acc_ref)

        for s in range(ns):
            def step(s=s):
                acc_ref[...] += dot(a_refs[s][...], b_ref[...])

            if ns == 1:
                step()
            else:
                pl.when(k // nks == s)(step)

        @pl.when(k == nk - 1)
        def _():
            finish(acc_ref[...])

    def a_map(i, j, k, s):
        return (i, jnp.clip(k - s * nks, 0, nks - 1))

    in_specs = [pl.BlockSpec((tm, tk), functools.partial(a_map, s=s)) for s in range(ns)]
    if b_t:
        in_specs.append(pl.BlockSpec((tn, tk), lambda i, j, k: (j, k)))
    else:
        in_specs.append(pl.BlockSpec((tk, tn), lambda i, j, k: (k, j)))
    args = list(a_list) + [b]
    if has_res:
        in_specs.append(pl.BlockSpec((tm, tn), lambda i, j, k: (i, j)))
        args.append(residual)
    return pl.pallas_call(
        body, grid=(M // tm, N // tn, nk), in_specs=in_specs,
        out_specs=pl.BlockSpec((tm, tn), lambda i, j, k: (i, j)),
        out_shape=SDS((M, N), out_dtype),
        scratch_shapes=[] if nk == 1 else [pltpu.VMEM((tm, tn), F32)],
        compiler_params=_params("parallel", "parallel", "arbitrary"), name=name,
    )(*args)


def _mm_tn(a, b_list, out_dtype, *, name, tm=1024, tn=2048, tk=512):
    ns = len(b_list)
    S, M = a.shape
    Ns = b_list[0].shape[1]
    assert all(b.shape == (S, Ns) for b in b_list)
    tm, tn, tk = min(tm, M), min(tn, Ns), min(tk, S)
    assert M % tm == 0 and Ns % tn == 0 and S % tk == 0
    njs = Ns // tn
    nk = S // tk

    def body(*refs):
        a_ref, b_refs, o_ref, acc_ref = refs[0], refs[1:1 + ns], refs[1 + ns], refs[2 + ns]
        j, k = pl.program_id(1), pl.program_id(2)

        @pl.when(k == 0)
        def _():
            acc_ref[...] = jnp.zeros_like(acc_ref)

        for s in range(ns):
            def step(s=s):
                acc_ref[...] += _dot_tn(a_ref[...], b_refs[s][...])

            if ns == 1:
                step()
            else:
                pl.when(j // njs == s)(step)

        @pl.when(k == nk - 1)
        def _():
            o_ref[...] = acc_ref[...].astype(out_dtype)

    def b_map(i, j, k, s):
        return (k, jnp.clip(j - s * njs, 0, njs - 1))

    in_specs = [pl.BlockSpec((tk, tm), lambda i, j, k: (k, i))]
    in_specs += [pl.BlockSpec((tk, tn), functools.partial(b_map, s=s)) for s in range(ns)]
    return pl.pallas_call(
        body, grid=(M // tm, ns * njs, nk), in_specs=in_specs,
        out_specs=pl.BlockSpec((tm, tn), lambda i, j, k: (i, j)),
        out_shape=SDS((M, ns * Ns), out_dtype),
        scratch_shapes=[pltpu.VMEM((tm, tn), F32)],
        compiler_params=_params("parallel", "parallel", "arbitrary"), name=name,
    )(a, *b_list)


def _mm_nt_rows(w_t, h, *, name, tn=1024):
    R, K = w_t.shape
    S = h.shape[0]
    tn = min(tn, S)

    def body(w_ref, h_ref, o_ref):
        o_ref[...] = _dot_nt(w_ref[...], h_ref[...])

    return pl.pallas_call(
        body, grid=(S // tn,),
        in_specs=[pl.BlockSpec((R, K), lambda i: (0, 0)), pl.BlockSpec((tn, K), lambda i: (i, 0))],
        out_specs=pl.BlockSpec((R, tn), lambda i: (0, i)),
        out_shape=SDS((R, S), F32), compiler_params=_params("parallel"), name=name,
    )(w_t, h)


def _rms_fwd(x, gain, *, name, tm=512):
    S, D = x.shape
    tm = min(tm, S)

    def body(x_ref, g_ref, h_ref):
        xv = x_ref[...]
        r = lax.rsqrt(jnp.mean(xv * xv, axis=-1, keepdims=True) + EPS)
        h_ref[...] = ((xv * r) * g_ref[...]).astype(BF16)

    return pl.pallas_call(
        body, grid=(S // tm,),
        in_specs=[pl.BlockSpec((tm, D), lambda i: (i, 0)), pl.BlockSpec((1, D), lambda i: (0, 0))],
        out_specs=pl.BlockSpec((tm, D), lambda i: (i, 0)),
        out_shape=SDS((S, D), BF16), compiler_params=_params("parallel"), name=name,
    )(x, gain)


def _rms_bwd(x, gain, dh, dres, *, name, tm=256):
    S, D = x.shape
    tm = min(tm, S)

    def body(x_ref, g_ref, dh_ref, dres_ref, dx_ref, dxb_ref, dg_ref):
        @pl.when(pl.program_id(0) == 0)
        def _():
            dg_ref[...] = jnp.zeros_like(dg_ref)

        xv = x_ref[...]
        r = lax.rsqrt(jnp.mean(xv * xv, axis=-1, keepdims=True) + EPS)
        xh = xv * r
        dhv = dh_ref[...].astype(F32)
        dg_ref[...] += jnp.sum(dhv * xh, axis=0, keepdims=True)
        dxh = dhv * g_ref[...]
        dx = r * (dxh - xh * jnp.mean(dxh * xh, axis=-1, keepdims=True)) + dres_ref[...]
        dx_ref[...] = dx
        dxb_ref[...] = dx.astype(BF16)

    row = pl.BlockSpec((tm, D), lambda i: (i, 0))
    vec = pl.BlockSpec((1, D), lambda i: (0, 0))
    return pl.pallas_call(
        body, grid=(S // tm,), in_specs=[row, vec, row, row], out_specs=[row, row, vec],
        out_shape=[SDS((S, D), F32), SDS((S, D), BF16), SDS((1, D), F32)],
        compiler_params=_params("arbitrary"), name=name,
    )(x, gain, dh, dres)


def _loss_head(x, gain, target, *, name, tm=256):
    S, D = x.shape
    tm = min(tm, S)
    assert tm % 8 == 0 and D % 128 == 0

    def body(x_ref, g_ref, t_ref, dx_ref, dxb_ref, loss_ref, dg_ref):
        @pl.when(pl.program_id(0) == 0)
        def _():
            dg_ref[...] = jnp.zeros_like(dg_ref)
            loss_ref[...] = jnp.zeros_like(loss_ref)

        xv = x_ref[...]
        g = g_ref[...]
        r = lax.rsqrt(jnp.mean(xv * xv, axis=-1, keepdims=True) + EPS)
        xh = xv * r
        err = xh * g - t_ref[...]
        e2 = (err * err).reshape(tm // 8, 8, D).sum(axis=0)
        part = e2[:, 0:128]
        for k in range(1, D // 128):
            part = part + e2[:, k * 128:(k + 1) * 128]
        loss_ref[...] += part * (0.5 / D)
        dy = err * (1.0 / D)
        dg_ref[...] += jnp.sum(dy * xh, axis=0, keepdims=True)
        dxh = dy * g
        dx = r * (dxh - xh * jnp.mean(dxh * xh, axis=-1, keepdims=True))
        dx_ref[...] = dx
        dxb_ref[...] = dx.astype(BF16)

    row = pl.BlockSpec((tm, D), lambda i: (i, 0))
    vec = pl.BlockSpec((1, D), lambda i: (0, 0))
    return pl.pallas_call(
        body, grid=(S // tm,), in_specs=[row, vec, row],
        out_specs=[row, row, pl.BlockSpec((8, 128), lambda i: (0, 0)), vec],
        out_shape=[SDS((S, D), F32), SDS((S, D), BF16), SDS((8, 128), F32), SDS((1, D), F32)],
        compiler_params=_params("arbitrary"), name=name,
    )(x, gain, target)


def _split3(x):
    hi = x.astype(BF16)
    r1 = x - hi.astype(F32)
    mid = r1.astype(BF16)
    lo = (r1 - mid.astype(F32)).astype(BF16)
    return hi, mid, lo


def _split2(x):
    hi = x.astype(BF16)
    lo = (x - hi.astype(F32)).astype(BF16)
    return hi, lo


def _fox_gate_fwd(fl_t, b_col, *, name):
    H, S = fl_t.shape
    L = 128
    tri = jnp.asarray(np.triu(np.ones((L, L), np.float32)), BF16)

    def body(fl_ref, b_ref, tri_ref, hi_ref, mid_ref, lo_ref, carry):
        @pl.when(pl.program_id(0) == 0)
        def _():
            carry[...] = jnp.zeros_like(carry)

        z = fl_ref[...] + b_ref[...]
        lf = jnp.minimum(z, 0.0) - jnp.log(1.0 + jnp.exp(-jnp.abs(z)))
        hi, mid, lo = _split3(lf)
        t = tri_ref[...]
        c = (_dot(hi, t) + _dot(mid, t)) + _dot(lo, t) + carry[...]
        carry[...] = c[:, L - 1:L]
        hi_ref[...], mid_ref[...], lo_ref[...] = _split3(c * (-LOG2E))

    blk = pl.BlockSpec((H, L), lambda i: (0, i))
    return pl.pallas_call(
        body, grid=(S // L,),
        in_specs=[blk, pl.BlockSpec((H, 1), lambda i: (0, 0)), pl.BlockSpec((L, L), lambda i: (0, 0))],
        out_specs=[blk] * 3, out_shape=[SDS((H, S), BF16)] * 3, scratch_shapes=[pltpu.VMEM((H, 1), F32)],
        compiler_params=_params("arbitrary"), name=name,
    )(fl_t, b_col, tri)


def _fox_gate_bwd(dc_row, dc_key, fl_t, b_col, *, name):
    H, S = fl_t.shape
    L = 128
    n = S // L
    tri = jnp.asarray(np.tril(np.ones((L, L), np.float32)), BF16)

    def body(dcr_ref, dck_ref, fl_ref, b_ref, tri_ref, dfl_ref, db_ref, carry):
        @pl.when(pl.program_id(0) == 0)
        def _():
            carry[...] = jnp.zeros_like(carry)
            db_ref[...] = jnp.zeros_like(db_ref)

        hi, mid, lo = _split3(dcr_ref[...] + dck_ref[...])
        t = tri_ref[...]
        dlf = (_dot(hi, t) + _dot(mid, t)) + _dot(lo, t) + carry[...]
        carry[...] = dlf[:, 0:1]
        z = fl_ref[...] + b_ref[...]
        dfl = dlf * jax.nn.sigmoid(-z)
        dfl_ref[...] = dfl
        db_ref[...] += jnp.sum(dfl, axis=1, keepdims=True)

    blk = pl.BlockSpec((H, L), lambda i: (0, n - 1 - i))
    col = pl.BlockSpec((H, 1), lambda i: (0, 0))
    return pl.pallas_call(
        body, grid=(n,), in_specs=[blk, blk, blk, col, pl.BlockSpec((L, L), lambda i: (0, 0))],
        out_specs=[blk, col], out_shape=[SDS((H, S), F32), SDS((H, 1), F32)],
        scratch_shapes=[pltpu.VMEM((H, 1), F32)], compiler_params=_params("arbitrary"), name=name,
    )(dc_row, dc_key, fl_t, b_col, tri)


AUG = HEAD_DIM


def _lane_select(cols, shape):
    lane = lax.broadcasted_iota(jnp.int32, shape, 1)
    out = jnp.zeros(shape, BF16)
    for k, c in reversed(list(enumerate(cols))):
        c = jnp.full(shape, c, BF16) if isinstance(c, (int, float)) else jnp.broadcast_to(c, shape).astype(BF16)
        out = jnp.where(lane == k, c, out)
    return out


def _fox_key_aug(b_hi, b_mid, b_lo):
    H, S = b_hi.shape
    ones = jnp.ones((H, S), BF16)
    ka = jnp.stack([b_hi, b_mid, b_lo, ones, ones, ones], axis=-1)
    ka = jnp.pad(ka, ((0, 0), (0, 0), (0, AUG - 6)))
    return jnp.transpose(ka, (1, 0, 2)).reshape(S, H * AUG)


def _fox_fwd(p0, kaug, *, H, name):
    S = p0.shape[0]
    T = min(ATT_BLOCK, S)
    nq = S // T
    dh = HEAD_DIM
    G = ATT_HEADS_PER_STEP
    assert H % G == 0

    def body(q_ref, k_ref, ka_ref, v_ref, g_ref, o_ref, y_ref, qa_ref, m_sc, acc_sc, p_sc, al_sc):
        i = pl.program_id(1)
        qaug = _lane_select([1.0, 1.0, 1.0], (T, AUG))
        ones = jnp.ones((T, dh), BF16)
        m_sc[...] = jnp.full_like(m_sc, NEG)
        acc_sc[...] = jnp.zeros_like(acc_sc)
        p_sc[...] = jnp.zeros_like(p_sc)
        al_sc[...] = jnp.ones_like(al_sc)

        def step(j, masked):
            rows = pl.ds(pl.multiple_of(j * T, T), T)
            prev = pl.ds(pl.multiple_of(jnp.maximum(j - 1, 0) * T, T), T)
            for g in range(G):
                hd = slice(g * dh, (g + 1) * dh)
                vp = jnp.concatenate([v_ref[prev, hd], ones], axis=1)
                acc_sc[g] = jnp.tile(al_sc[g], (1, 2)) * acc_sc[g] + _dot(p_sc[g], vp)
                q = jnp.concatenate([q_ref[:, hd], qaug], axis=1)
                kj = jnp.concatenate([k_ref[rows, hd], ka_ref[rows, hd]], axis=1)
                t = _dot_nt(q, kj)
                if masked:
                    row = lax.broadcasted_iota(jnp.int32, (T, T), 0)
                    col = lax.broadcasted_iota(jnp.int32, (T, T), 1)
                    t = jnp.where(row >= col, t, NEG)
                m_prev = m_sc[g]
                m_new = jnp.maximum(m_prev, jnp.max(t, axis=-1, keepdims=True))
                p_sc[g] = jnp.exp2(t - jnp.tile(m_new, (1, T // 128))).astype(BF16)
                al_sc[g] = jnp.exp2(m_prev - m_new)
                m_sc[g] = m_new

        def loop_body(j, carry):
            step(j, False)
            return carry

        lax.fori_loop(0, i, loop_body, 0)
        step(i, True)
        rows = pl.ds(pl.multiple_of(i * T, T), T)
        for g in range(G):
            hd = slice(g * dh, (g + 1) * dh)
            vp = jnp.concatenate([v_ref[rows, hd], ones], axis=1)
            acc = jnp.tile(al_sc[g], (1, 2)) * acc_sc[g] + _dot(p_sc[g], vp)
            l = acc[:, dh:]
            o = acc[:, :dh] / l
            o_ref[:, hd] = o
            y_ref[:, hd] = (o * _silu(g_ref[:, hd].astype(F32))).astype(BF16)
            hi, mid, lo = _split3(-(m_sc[g] + jnp.log2(l)))
            qa_ref[:, hd] = _lane_select([1.0, 1.0, 1.0, hi, mid, lo], (T, AUG))

    blk = lambda off: pl.BlockSpec((T, G * dh), lambda h, i: (i, off // G + h))
    full = lambda off: pl.BlockSpec((S, G * dh), lambda h, i: (0, off // G + h))
    return pl.pallas_call(
        body, grid=(H // G, nq),
        in_specs=[blk(0), full(H), full(0), full(2 * H), blk(3 * H)],
        out_specs=[blk(0), blk(0), blk(0)],
        out_shape=[SDS((S, H * dh), F32), SDS((S, H * dh), BF16), SDS((S, H * AUG), BF16)],
        scratch_shapes=[pltpu.VMEM((G, T, 128), F32), pltpu.VMEM((G, T, 2 * dh), F32),
                        pltpu.VMEM((G, T, T), BF16), pltpu.VMEM((G, T, 128), F32)],
        compiler_params=_params("parallel", "arbitrary"), name=name,
    )(p0, p0, kaug, p0, p0)


def _fox_post_bwd(dy, o, p0, *, H, name, tm=512):
    S = dy.shape[0]
    dh = HEAD_DIM
    tm = min(tm, S)
    G = POST_HEADS_PER_STEP
    assert H % G == 0

    def body(dy_ref, o_ref, g_ref, do_ref, dg_ref, da_ref):
        dyv = dy_ref[...].astype(F32)
        ov = o_ref[...]
        g = g_ref[...].astype(F32)
        do = (dyv * _silu(g)).astype(BF16)
        do_ref[...] = do
        dg_ref[...] = (dyv * ov * _dsilu(g)).astype(BF16)
        prod = do.astype(F32) * ov
        for k in range(G):
            hd = slice(k * dh, (k + 1) * dh)
            delta = jnp.sum(prod[:, hd], axis=-1, keepdims=True)
            hi, mid, lo = _split3(-jnp.broadcast_to(delta, (tm, AUG)))
            da_ref[:, hd] = _lane_select([hi, mid, lo], (tm, AUG))

    blk = pl.BlockSpec((tm, G * dh), lambda h, i: (i, h))
    return pl.pallas_call(
        body, grid=(H // G, S // tm),
        in_specs=[blk, blk, pl.BlockSpec((tm, G * dh), lambda h, i: (i, 3 * H // G + h))],
        out_specs=[blk, blk, blk],
        out_shape=[SDS((S, H * dh), BF16), SDS((S, H * dh), BF16), SDS((S, H * AUG), BF16)],
        compiler_params=_params("parallel", "parallel"), name=name,
    )(dy, o, p0)


def _fox_dq(p0, kaug, qaug, do, doaug, *, H, name):
    S = p0.shape[0]
    T = min(ATT_BLOCK, S)
    nq = S // T
    dh = HEAD_DIM
    scale = dh ** -0.5
    G = ATT_HEADS_PER_STEP
    assert H % G == 0

    def body(q_ref, qa_ref, k_ref, ka_ref, v_ref, do_ref, da_ref, dq_ref, rs_ref, acc_sc, ds_sc):
        i = pl.program_id(1)
        vaug = _lane_select([1.0, 1.0, 1.0], (T, AUG))
        ones = jnp.ones((T, dh), BF16)
        acc_sc[...] = jnp.zeros_like(acc_sc)
        ds_sc[...] = jnp.zeros_like(ds_sc)

        def step(j, masked):
            rows = pl.ds(pl.multiple_of(j * T, T), T)
            prev = pl.ds(pl.multiple_of(jnp.maximum(j - 1, 0) * T, T), T)
            for g in range(G):
                hd = slice(g * dh, (g + 1) * dh)
                acc_sc[g] += _dot(ds_sc[g], jnp.concatenate([k_ref[prev, hd], ones], axis=1))
                q = jnp.concatenate([q_ref[:, hd], qa_ref[:, hd]], axis=1)
                do = jnp.concatenate([do_ref[:, hd], da_ref[:, hd]], axis=1)
                p = jnp.exp2(_dot_nt(q, jnp.concatenate([k_ref[rows, hd], ka_ref[rows, hd]], axis=1)))
                if masked:
                    row = lax.broadcasted_iota(jnp.int32, (T, T), 0)
                    col = lax.broadcasted_iota(jnp.int32, (T, T), 1)
                    p = jnp.where(row >= col, p, 0.0)
                ds = p * _dot_nt(do, jnp.concatenate([v_ref[rows, hd], vaug], axis=1))
                ds_sc[g] = ds.astype(BF16)

        def loop_body(j, carry):
            step(j, False)
            return carry

        lax.fori_loop(0, i, loop_body, 0)
        step(i, True)
        rows = pl.ds(pl.multiple_of(i * T, T), T)
        for g in range(G):
            hd = slice(g * dh, (g + 1) * dh)
            acc = acc_sc[g] + _dot(ds_sc[g], jnp.concatenate([k_ref[rows, hd], ones], axis=1))
            dq_ref[:, hd] = (acc[:, :dh] * scale).astype(BF16)
            rs_ref[g] = acc[:, dh:dh + 1]

    blk = lambda off: pl.BlockSpec((T, G * dh), lambda h, i: (i, off // G + h))
    full = lambda off: pl.BlockSpec((S, G * dh), lambda h, i: (0, off // G + h))
    return pl.pallas_call(
        body, grid=(H // G, nq),
        in_specs=[blk(0), blk(0), full(H), full(0), full(2 * H), blk(0), blk(0)],
        out_specs=[blk(0), pl.BlockSpec((G, T, 1), lambda h, i: (h, i, 0))],
        out_shape=[SDS((S, H * dh), BF16), SDS((H, S, 1), F32)],
        scratch_shapes=[pltpu.VMEM((G, T, 2 * dh), F32), pltpu.VMEM((G, T, T), BF16)],
        compiler_params=_params("parallel", "arbitrary"), name=name,
    )(p0, qaug, p0, kaug, p0, do, doaug)


def _fox_dkv(p0, kaug, qaug, do, doaug, *, H, name):
    S = p0.shape[0]
    T = min(ATT_BLOCK, S)
    nq = S // T
    dh = HEAD_DIM
    G = ATT_HEADS_PER_STEP
    assert H % G == 0

    def body(q_ref, qa_ref, k_ref, ka_ref, v_ref, do_ref, da_ref, dk_ref, dv_ref, dc_ref, dk_sc, dv_sc,
             pt_sc, dst_sc):
        j = pl.program_id(1)
        vaug = _lane_select([1.0, 1.0, 1.0], (T, AUG))
        ones = jnp.ones((T, dh), BF16)
        dk_sc[...] = jnp.zeros_like(dk_sc)
        dv_sc[...] = jnp.zeros_like(dv_sc)
        pt_sc[...] = jnp.zeros_like(pt_sc)
        dst_sc[...] = jnp.zeros_like(dst_sc)

        def apply(prev):
            for g in range(G):
                hd = slice(g * dh, (g + 1) * dh)
                dv_sc[g] += _dot(pt_sc[g], do_ref[prev, hd])
                dk_sc[g] += _dot(dst_sc[g], jnp.concatenate([q_ref[prev, hd], ones], axis=1))

        def step(i, masked):
            rows = pl.ds(pl.multiple_of(i * T, T), T)
            apply(pl.ds(pl.multiple_of(jnp.maximum(i - 1, j) * T, T), T))
            for g in range(G):
                hd = slice(g * dh, (g + 1) * dh)
                k = jnp.concatenate([k_ref[:, hd], ka_ref[:, hd]], axis=1)
                v = jnp.concatenate([v_ref[:, hd], vaug], axis=1)
                pt = jnp.exp2(_dot_nt(k, jnp.concatenate([q_ref[rows, hd], qa_ref[rows, hd]], axis=1)))
                if masked:
                    row = lax.broadcasted_iota(jnp.int32, (T, T), 0)
                    col = lax.broadcasted_iota(jnp.int32, (T, T), 1)
                    pt = jnp.where(col >= row, pt, 0.0)
                dst = pt * _dot_nt(v, jnp.concatenate([do_ref[rows, hd], da_ref[rows, hd]], axis=1))
                pt_sc[g] = pt.astype(BF16)
                dst_sc[g] = dst.astype(BF16)

        step(j, True)

        def loop_body(i, carry):
            step(i, False)
            return carry

        lax.fori_loop(j + 1, nq, loop_body, 0)
        apply(pl.ds((nq - 1) * T, T))
        for g in range(G):
            hd = slice(g * dh, (g + 1) * dh)
            dk_ref[:, hd] = (dk_sc[g, :, :dh] * LN2).astype(BF16)
            dv_ref[:, hd] = dv_sc[g].astype(BF16)
            dc_ref[g] = -dk_sc[g, :, dh:dh + 1]

    blk = lambda off: pl.BlockSpec((T, G * dh), lambda h, j: (j, off // G + h))
    full = lambda off: pl.BlockSpec((S, G * dh), lambda h, j: (0, off // G + h))
    return pl.pallas_call(
        body, grid=(H // G, nq),
        in_specs=[full(0), full(0), blk(H), blk(0), blk(2 * H), full(0), full(0)],
        out_specs=[blk(0), blk(0), pl.BlockSpec((G, T, 1), lambda h, j: (h, j, 0))],
        out_shape=[SDS((S, H * dh), BF16), SDS((S, H * dh), BF16), SDS((H, S, 1), F32)],
        scratch_shapes=[pltpu.VMEM((G, T, 2 * dh), F32), pltpu.VMEM((G, T, dh), F32),
                        pltpu.VMEM((G, T, T), BF16), pltpu.VMEM((G, T, T), BF16)],
        compiler_params=_params("parallel", "arbitrary"), name=name,
    )(p0, qaug, p0, kaug, p0, do, doaug)


def _hgrn_levels(C, leaf):
    levels = []
    h = C // 2
    while h >= leaf:
        levels.append(h)
        h //= 2
    return levels


def _hgrn_sum_matrix(C, leaf):
    t = np.arange(C)[:, None]
    u = np.arange(C)[None, :]
    mats = [(u <= t), (u > t)]
    for h in _hgrn_levels(C, leaf):
        start = (t // (2 * h)) * (2 * h)
        mid = start + h - 1
        second = t > mid
        m = np.where(second, (u > mid) & (u <= t), (u > t) & (u <= mid))
        mats.append(m)
    lstart = (t // leaf) * leaf
    mats.append((u >= lstart) & (u <= t))
    return np.concatenate([m.astype(np.float32) for m in mats], axis=0)


def _hgrn_chunk_terms(qr, fz, lb, msum, C, leaf):
    levels = _hgrn_levels(C, leaf)
    sq = _silu(qr)
    t = jnp.exp(-jnp.abs(fz))
    r = 1.0 / (1.0 + t)
    pos = fz >= 0.0
    sp = jnp.where(pos, r, t * r)
    sn = jnp.where(pos, t * r, r)
    f = lb + (1.0 - lb) * sp
    lf = jnp.log(f)
    k = (1.0 - lb) * sn
    hi, lo = _split2(lf)
    dsum = _dot(msum, hi) + _dot(msum, lo)
    b = dsum[0:C]
    kdec = dsum[C:2 * C]
    rowi = lax.broadcasted_iota(jnp.int32, (C, 1), 0)
    lev = []
    for n, h in enumerate(levels):
        e = jnp.exp(dsum[(2 + n) * C:(3 + n) * C])
        second = (rowi % (2 * h)) >= h
        qm = jnp.where(second, sq * e, 0.0).astype(BF16)
        km = jnp.where(second, 0.0, k * e).astype(BF16)
        lev.append((h, e, second, qm, km))
    dleaf = dsum[(2 + len(levels)) * C:(3 + len(levels)) * C]
    eq = jnp.exp(dleaf)
    ek = jnp.exp(jnp.minimum(-dleaf, EXP_CLAMP))
    return dict(sq=sq, sp=sp, sn=sn, f=f, k=k, b=b, kdec=kdec, lev=lev, eq=eq, ek=ek,
                ql=(sq * eq).astype(BF16), kl=(k * ek).astype(BF16),
                qs=(sq * jnp.exp(b)).astype(BF16), ke=(k * jnp.exp(kdec)).astype(BF16),
                e_c=jnp.exp(b[C - 1:C, :]))


def _hgrn_masks(C, leaf, transposed):
    a = lax.broadcasted_iota(jnp.int32, (C, C), 0)
    bb = lax.broadcasted_iota(jnp.int32, (C, C), 1)
    t, s = (bb, a) if transposed else (a, bb)
    lev = [None if 2 * h == C else (t // (2 * h)) == (s // (2 * h)) for h in _hgrn_levels(C, leaf)]
    if leaf == C:
        leafm = s <= t
    else:
        leafm = ((t // leaf) == (s // leaf)) & (s <= t)
    return lev, leafm


def _hgrn_fwd(p1, f1, lb, onorm, *, H, name, tb=512):
    S = p1.shape[0]
    dk = HEAD_DIM
    C = min(HGRN_CHUNK, S)
    leaf = min(HGRN_LEAF, C)
    tb = min(tb, S)
    nc = tb // C
    G = HGRN_HEADS_PER_STEP
    assert H % G == 0
    msum = jnp.asarray(_hgrn_sum_matrix(C, leaf), BF16)

    def body(q_ref, f_ref, v_ref, g_ref, lb_ref, on_ref, ms_ref, o_ref, y_ref, st_ref, st_sc):
        @pl.when(pl.program_id(1) == 0)
        def _():
            st_sc[...] = jnp.zeros_like(st_sc)

        msv = ms_ref[...]
        lmask, leafm = _hgrn_masks(C, leaf, False)

        def chunk(n, carry):
            rows = pl.ds(pl.multiple_of(n * C, C), C)
            for g in range(G):
                hd = slice(g * dk, (g + 1) * dk)
                tm = _hgrn_chunk_terms(q_ref[rows, hd].astype(F32), f_ref[rows, hd], lb_ref[:, hd], msv, C, leaf)
                v = v_ref[rows, hd]
                st = st_sc[g]
                st_ref[g, n] = st
                a = jnp.where(leafm, _dot_nt(tm["ql"], tm["kl"]), 0.0)
                for (h, e, second, qm, km), m in zip(tm["lev"], lmask):
                    al = _dot_nt(qm, km)
                    a = a + (al if m is None else jnp.where(m, al, 0.0))
                o = _dot_nt(tm["qs"], st.astype(BF16)) + _dot(a.astype(BF16), v)
                st_sc[g] = st * tm["e_c"] + _dot(v.T, tm["ke"])
                o_ref[rows, hd] = o
                rn = lax.rsqrt(jnp.mean(o * o, axis=-1, keepdims=True) + EPS)
                y = ((o * rn) * on_ref[:, hd]) * _silu(g_ref[rows, hd].astype(F32))
                y_ref[rows, hd] = y.astype(BF16)
            return carry

        lax.fori_loop(0, nc, chunk, 0)

    blk = lambda off: pl.BlockSpec((tb, G * dk), lambda h, i: (i, off // G + h))
    vec = pl.BlockSpec((1, G * dk), lambda h, i: (0, h))
    return pl.pallas_call(
        body, grid=(H // G, S // tb),
        in_specs=[blk(0), blk(0), blk(H), blk(2 * H), vec, vec,
                  pl.BlockSpec(msum.shape, lambda h, i: (0, 0))],
        out_specs=[blk(0), blk(0), pl.BlockSpec((G, nc, dk, dk), lambda h, i: (h, i, 0, 0))],
        out_shape=[SDS((S, H * dk), F32), SDS((S, H * dk), BF16), SDS((H, S // C, dk, dk), F32)],
        scratch_shapes=[pltpu.VMEM((G, dk, dk), F32)],
        compiler_params=_params("parallel", "arbitrary"), name=name,
    )(p1, f1, p1, p1, lb, onorm, msum)


def _hgrn_post_bwd(dy, o, p1, onorm, *, H, name, tm=512):
    S = dy.shape[0]
    dk = HEAD_DIM
    tm = min(tm, S)
    G = POST_HEADS_PER_STEP
    assert H % G == 0

    def body(dy_ref, o_ref, g_ref, on_ref, do_ref, dg_ref, don_ref):
        @pl.when(pl.program_id(1) == 0)
        def _():
            don_ref[...] = jnp.zeros_like(don_ref)

        for k in range(G):
            hd = slice(k * dk, (k + 1) * dk)
            dyv = dy_ref[:, hd].astype(F32)
            ov = o_ref[:, hd]
            g = g_ref[:, hd].astype(F32)
            onv = on_ref[:, hd]
            rn = lax.rsqrt(jnp.mean(ov * ov, axis=-1, keepdims=True) + EPS)
            oh = ov * rn
            dn = dyv * _silu(g)
            dg_ref[:, hd] = (dyv * (oh * onv) * _dsilu(g)).astype(BF16)
            don_ref[:, hd] += jnp.sum(dn * oh, axis=0, keepdims=True)
            doh = dn * onv
            do_ref[:, hd] = (rn * (doh - oh * jnp.mean(doh * oh, axis=-1, keepdims=True))).astype(BF16)

    blk = pl.BlockSpec((tm, G * dk), lambda h, i: (i, h))
    vec = pl.BlockSpec((1, G * dk), lambda h, i: (0, h))
    return pl.pallas_call(
        body, grid=(H // G, S // tm),
        in_specs=[blk, blk, pl.BlockSpec((tm, G * dk), lambda h, i: (i, 2 * H // G + h)), vec],
        out_specs=[blk, blk, vec],
        out_shape=[SDS((S, H * dk), BF16), SDS((S, H * dk), BF16), SDS((1, H * dk), F32)],
        compiler_params=_params("parallel", "arbitrary"), name=name,
    )(dy, o, p1, onorm)


def _hgrn_bwd(p1, f1, lb, do, states, *, H, name, tb=512):
    S = p1.shape[0]
    dk = HEAD_DIM
    C = min(HGRN_CHUNK, S)
    leaf = min(HGRN_LEAF, C)
    tb = min(tb, S)
    nc = tb // C
    nb = S // tb
    G = HGRN_HEADS_PER_STEP
    assert H % G == 0
    msum = jnp.asarray(_hgrn_sum_matrix(C, leaf), BF16)
    rtri = jnp.asarray(np.triu(np.ones((C, C), np.float32)), BF16)

    def body(q_ref, f_ref, v_ref, do_ref, st_ref, lb_ref, ms_ref, rt_ref,
             dq_ref, df_ref, dv_ref, dlb_ref, g_sc):
        @pl.when(pl.program_id(1) == 0)
        def _():
            g_sc[...] = jnp.zeros_like(g_sc)
            dlb_ref[...] = jnp.zeros_like(dlb_ref)

        msv = ms_ref[...]
        rtv = rt_ref[...]
        lmask, leafm = _hgrn_masks(C, leaf, False)
        lmask_t, leafm_t = _hgrn_masks(C, leaf, True)
        f32 = lambda z: z.astype(F32)

        def head_chunk(g, n):
            hd = slice(g * dk, (g + 1) * dk)
            rows = pl.ds(pl.multiple_of(n * C, C), C)
            lbv = lb_ref[:, hd]
            qr = q_ref[rows, hd].astype(F32)
            tm = _hgrn_chunk_terms(qr, f_ref[rows, hd], lbv, msv, C, leaf)
            v = v_ref[rows, hd]
            dov = do_ref[rows, hd]
            st0 = st_ref[g, n]
            gt = g_sc[g]
            gtb = gt.astype(BF16)
            da = _dot_nt(dov, v)
            da_t = _dot_nt(v, dov)

            dal = jnp.where(leafm, da, 0.0).astype(BF16)
            dal_t = jnp.where(leafm_t, da_t, 0.0).astype(BF16)
            dql = _dot(dal, tm["kl"])
            dkl = _dot(dal_t, tm["ql"])
            dsq = dql * tm["eq"]
            dkk = dkl * tm["ek"]
            xq = f32(tm["ql"]) * dql
            xk = f32(tm["kl"]) * dkl
            a_t = jnp.where(leafm_t, _dot_nt(tm["kl"], tm["ql"]), 0.0)
            for (h, e, second, qm, km), m, m_t in zip(tm["lev"], lmask, lmask_t):
                dl = (da if m is None else jnp.where(m, da, 0.0)).astype(BF16)
                dl_t = (da_t if m_t is None else jnp.where(m_t, da_t, 0.0)).astype(BF16)
                dqm = _dot(dl, km)
                dkm = _dot(dl_t, qm)
                dsq = dsq + jnp.where(second, dqm * e, 0.0)
                dkk = dkk + jnp.where(second, 0.0, dkm * e)
                xq = xq + f32(qm) * dqm
                xk = xk + f32(km) * dkm
                al_t = _dot_nt(km, qm)
                a_t = a_t + (al_t if m_t is None else jnp.where(m_t, al_t, 0.0))
            dqs = _dot(dov, st0.astype(BF16))
            dke = _dot(v, gtb)
            dsq = dsq + dqs * jnp.exp(tm["b"])
            dkk = dkk + dke * jnp.exp(tm["kdec"])
            xq = xq + f32(tm["qs"]) * dqs
            xk = xk + f32(tm["ke"]) * dke
            dvv = _dot(a_t.astype(BF16), dov) + _dot_nt(tm["ke"], gtb)
            r_end = jnp.sum(f32(gtb) * _dot(v.T, tm["ke"]) + gt * (st0 * tm["e_c"]), axis=0, keepdims=True)
            g_sc[g] = gt * tm["e_c"] + _dot(dov.T, tm["qs"])
            xh, xm, xl = _split3(xq - xk)
            dlf = (_dot(rtv, xh) + _dot(rtv, xm)) + _dot(rtv, xl) + r_end
            dlf_f = dlf / tm["f"]
            dsp = (1.0 - lbv) * (dlf_f - dkk)
            df_ref[rows, hd] = (dsp * (tm["sp"] * tm["sn"])).astype(BF16)
            dq_ref[rows, hd] = (dsq * _dsilu(qr)).astype(BF16)
            dv_ref[rows, hd] = dvv.astype(BF16)
            dlb_ref[:, hd] += jnp.sum(dlf_f * tm["sn"] - dkk * tm["sn"], axis=0, keepdims=True)

        def chunk(nn, carry):
            for g in range(G):
                head_chunk(g, nc - 1 - nn)
            return carry

        lax.fori_loop(0, nc, chunk, 0)

    blk = lambda off: pl.BlockSpec((tb, G * dk), lambda h, i: (nb - 1 - i, off // G + h))
    vec = pl.BlockSpec((1, G * dk), lambda h, i: (0, h))
    return pl.pallas_call(
        body, grid=(H // G, nb),
        in_specs=[blk(0), blk(0), blk(H), blk(0),
                  pl.BlockSpec((G, nc, dk, dk), lambda h, i: (h, nb - 1 - i, 0, 0)), vec,
                  pl.BlockSpec(msum.shape, lambda h, i: (0, 0)), pl.BlockSpec((C, C), lambda h, i: (0, 0))],
        out_specs=[blk(0), blk(0), blk(0), vec],
        out_shape=[SDS((S, H * dk), BF16)] * 3 + [SDS((1, H * dk), F32)],
        scratch_shapes=[pltpu.VMEM((G, dk, dk), F32)],
        compiler_params=_params("parallel", "arbitrary"), name=name,
    )(p1, f1, p1, do, states, lb, msum, rtri)


def _lb_fwd(logits, *, name):
    W = logits.shape[1]

    def body(l_ref, lb_ref):
        l = l_ref[...]
        m = jnp.max(l, axis=0, keepdims=True)
        e = jnp.exp(l - m)
        p = e / jnp.sum(e, axis=0, keepdims=True)
        lb_ref[...] = (p[0:1] + p[1:2]) - p[0:1]

    return pl.pallas_call(body, out_shape=SDS((1, W), F32), name=name)(logits)


STAT_ROWS = 8


def _stats_reduce(stats_all, logits, *, name):
    W = logits.shape[1]

    def body(s_ref, l_ref, g_ref):
        tot = s_ref[0]
        for d in range(1, N_DEV):
            tot = tot + s_ref[d]
        l = l_ref[...]
        m = jnp.max(l, axis=0, keepdims=True)
        e = jnp.exp(l - m)
        p = e / jnp.sum(e, axis=0, keepdims=True)
        dlb = tot[2:3]
        dl0 = -(p[0:1] * p[1:2]) * dlb
        dl1 = (p[1:2] * (1.0 - p[1:2])) * dlb
        g_ref[0:2] = tot[0:2]
        g_ref[2:3] = dl0
        g_ref[3:4] = dl1
        g_ref[4:7] = tot[3:6]
        g_ref[7:8] = jnp.zeros((1, W), F32)

    return pl.pallas_call(body, out_shape=SDS((STAT_ROWS, W), F32), name=name)(stats_all, logits)


def _adamw(w, m, v, g_parts, *, name, tr=128):
    R, C = w.shape
    ns = len(g_parts)
    n, Rs = g_parts[0].shape[0], g_parts[0].shape[1]
    assert all(p.shape == (n, Rs, C) for p in g_parts) and ns * Rs == R
    tr = min(tr, Rs)
    assert Rs % tr == 0
    nts = Rs // tr
    c1 = 1.0 / (1.0 - ADAM_B1 ** ADAM_STEP)
    c2 = 1.0 / (1.0 - ADAM_B2 ** ADAM_STEP)

    def body(*refs):
        w_ref, m_ref, v_ref = refs[:3]
        g_refs = refs[3:3 + ns]
        go_ref, d_ref, mo_ref, vo_ref = refs[3 + ns:]

        def update(g_ref):
            g = g_ref[0].astype(F32)
            for k in range(1, n):
                g = g + g_ref[k].astype(F32)
            mn = ADAM_B1 * m_ref[...] + (1.0 - ADAM_B1) * g
            vn = ADAM_B2 * v_ref[...] + (1.0 - ADAM_B2) * (g * g)
            d_ref[...] = -ADAM_LR * ((mn * c1) / (jnp.sqrt(vn * c2) + ADAM_EPS) + ADAM_WD * w_ref[...])
            go_ref[...] = g
            mo_ref[...] = mn
            vo_ref[...] = vn

        for s in range(ns):
            if ns == 1:
                update(g_refs[s])
            else:
                pl.when(pl.program_id(0) // nts == s)(functools.partial(update, g_refs[s]))

    def g_map(i, s):
        return (0, jnp.clip(i - s * nts, 0, nts - 1), 0)

    blk = pl.BlockSpec((tr, C), lambda i: (i, 0))
    return pl.pallas_call(
        body, grid=(R // tr,),
        in_specs=[blk, blk, blk] + [pl.BlockSpec((n, tr, C), functools.partial(g_map, s=s)) for s in range(ns)],
        out_specs=[blk] * 4, out_shape=[SDS((R, C), F32)] * 4,
        compiler_params=_params("parallel"), name=name,
    )(w, m, v, *g_parts)


ANY = pl.BlockSpec(memory_space=pl.ANY)
STAGE_BYTES = 2 * 1024 * 1024


def _stage_shape(shape, dtype):
    row_bytes = int(np.prod(shape[1:])) * jnp.dtype(dtype).itemsize
    rows = max(1, min(shape[0], STAGE_BYTES // row_bytes))
    while shape[0] % rows:
        rows -= 1
    return (rows,) + tuple(shape[1:])


def _staged_copy(frm, to, buf, sems):
    rows = buf.shape[0]
    for r0 in range(0, frm.shape[0], rows):
        cp = pltpu.make_async_copy(frm.at[pl.ds(r0, rows)], buf, sems.at[0])
        cp.start()
        cp.wait()
        cp = pltpu.make_async_copy(buf, to.at[pl.ds(r0, rows)], sems.at[1])
        cp.start()
        cp.wait()


def _all_gather(shards, out_shapes, views, *, name):
    n = len(shards)

    def body(*refs):
        ins, outs = refs[:n], refs[n:2 * n]
        send_sems, recv_sems, local_sems = refs[2 * n:2 * n + 3]
        bufs = refs[2 * n + 3:]
        x, y, c = lax.axis_index("x"), lax.axis_index("y"), lax.axis_index("c")
        me, sibling = (x, y, c), (x, y, 1 - c)
        chips = [(1 - x, y), (x, 1 - y), (1 - x, 1 - y)]

        def dev(p):
            return 4 * p[0] + 2 * p[1] + p[2]

        def copy(a, k, block, to, src=None):
            dst = views[a](outs[a], dev(block))
            return pltpu.make_async_remote_copy(
                src_ref=dst if src is None else src, dst_ref=dst,
                send_sem=send_sems.at[a, k], recv_sem=recv_sems.at[a, k],
                device_id=to, device_id_type=MESH)

        first, passed = [], []
        for a in range(n):
            first.append(copy(a, 0, me, sibling, src=ins[a]))
            first += [copy(a, 1 + j, me, (*chip, c), src=ins[a]) for j, chip in enumerate(chips)]
        for cp in first:
            cp.start()
        for a in range(n):
            _staged_copy(ins[a], views[a](outs[a], dev(me)), bufs[a], local_sems)
        for j, chip in enumerate(chips):
            for a in range(n):
                copy(a, 1 + j, (*chip, c), me).wait_recv()
                cp = copy(a, 4 + j, (*chip, c), sibling)
                cp.start()
                passed.append(cp)
        for a in range(n):
            copy(a, 0, sibling, me).wait_recv()
            for j, chip in enumerate(chips):
                copy(a, 4 + j, (*chip, 1 - c), me).wait_recv()
        for cp in first + passed:
            cp.wait_send()

    return pl.pallas_call(
        body, in_specs=[ANY] * n, out_specs=[ANY] * n, out_shape=list(out_shapes),
        scratch_shapes=[pltpu.SemaphoreType.DMA((n, 7)), pltpu.SemaphoreType.DMA((n, 7)),
                        pltpu.SemaphoreType.DMA((2,))]
        + [pltpu.VMEM(_stage_shape(s.shape, s.dtype), s.dtype) for s in shards],
        name=name,
    )(*shards)


HBM = pl.BlockSpec(memory_space=pltpu.HBM)
SEM = pl.BlockSpec(memory_space=pltpu.SEMAPHORE)
EFFECT = pltpu.SideEffectType.DATAFLOW_SIDE_EFFECTING


def _relations(x, y, c):
    for m in range(1, N_DEV):
        yield m, (1 - x if m & 4 else x, 1 - y if m & 2 else y, 1 - c if m & 1 else c)


def _dev_id(p):
    return 4 * p[0] + 2 * p[1] + p[2]


def _send_start(srcs, land_shapes, src_views, dst_views, *, name):
    n = len(srcs)

    def body(*refs):
        ins, lands = refs[:n], refs[n:2 * n]
        send_sems, recv_sems, token = refs[2 * n], refs[2 * n + 1], refs[-1]
        x, y, c = lax.axis_index("x"), lax.axis_index("y"), lax.axis_index("c")
        me = _dev_id((x, y, c))
        for m, p in _relations(x, y, c):
            for a in range(n):
                pltpu.make_async_remote_copy(
                    src_ref=src_views[a](ins[a], me, _dev_id(p), m), dst_ref=dst_views[a](lands[a], me, m),
                    send_sem=send_sems.at[a * (N_DEV - 1) + m - 1], recv_sem=recv_sems.at[a * (N_DEV - 1) + m - 1],
                    device_id=p, device_id_type=MESH).start()
        token[...] = jnp.zeros_like(token)

    lands = [pltpu.with_memory_space_constraint(lax.empty(s.shape, s.dtype), pltpu.HBM) for s in land_shapes]
    srcs = [pltpu.with_memory_space_constraint(v, pltpu.HBM) for v in srcs]
    res = pl.pallas_call(
        body, name=name,
        out_shape=[pltpu.SemaphoreType.DMA((n * (N_DEV - 1),)), pltpu.SemaphoreType.DMA((n * (N_DEV - 1),))]
        + [pltpu.HBM(v.shape, v.dtype) for v in srcs] + [pltpu.HBM(s.shape, s.dtype) for s in land_shapes]
        + [SDS((8, 128), F32)],
        in_specs=[HBM] * (2 * n), out_specs=[SEM, SEM] + [HBM] * (2 * n) + [pl.BlockSpec(memory_space=pltpu.VMEM)],
        input_output_aliases={i: 2 + i for i in range(2 * n)},
        compiler_params=pltpu.CompilerParams(has_side_effects=EFFECT),
    )(*srcs, *lands)
    return res[0], res[1], res[2:2 + n], res[2 + n:2 + 2 * n], res[-1]


def _send_wait(started, src_views, dst_views, own_views, own_shapes, after, *, name):
    send_sems, recv_sems, srcs, lands, _ = started
    n = len(srcs)

    def body(*refs):
        ins, lnd = refs[:n], refs[n:2 * n]
        send_sems, recv_sems = refs[2 * n], refs[2 * n + 1]
        got = refs[2 * n + 3 + n:2 * n + 3 + 2 * n]
        local_sems = refs[2 * n + 3 + 2 * n]
        bufs = refs[2 * n + 4 + 2 * n:]
        x, y, c = lax.axis_index("x"), lax.axis_index("y"), lax.axis_index("c")
        me = _dev_id((x, y, c))
        for m, p in _relations(x, y, c):
            for a in range(n):
                cp = pltpu.make_async_remote_copy(
                    src_ref=src_views[a](ins[a], me, _dev_id(p), m), dst_ref=dst_views[a](lnd[a], me, m),
                    send_sem=send_sems.at[a * (N_DEV - 1) + m - 1], recv_sem=recv_sems.at[a * (N_DEV - 1) + m - 1],
                    device_id=p, device_id_type=MESH)
                cp.wait_send()
                cp.wait_recv()
        for a in range(n):
            frm, to = own_views[a](ins[a], got[a], me)
            _staged_copy(frm, to, bufs[a], local_sems)

    res = pl.pallas_call(
        body, name=name,
        out_shape=[pltpu.HBM(v.shape, v.dtype) for v in srcs] + [pltpu.HBM(v.shape, v.dtype) for v in lands],
        in_specs=[HBM] * (2 * n) + [SEM, SEM, ANY], out_specs=[HBM] * (2 * n),
        input_output_aliases={i: i for i in range(2 * n)},
        scratch_shapes=[pltpu.SemaphoreType.DMA((2,))]
        + [pltpu.VMEM(_stage_shape(s, v.dtype), v.dtype) for s, v in zip(own_shapes, srcs)],
        compiler_params=pltpu.CompilerParams(has_side_effects=EFFECT),
    )(*srcs, *lands, send_sems, recv_sems, after)
    return res[n:]


def kernel(x, norm_gains, fox_w_in, fox_b_f, hgrn_w_in, hgrn_lb_logits, hgrn_onorm, w_out, final_gain, loss_target, m_norm_gains, m_fox_w_in, m_fox_b_f, m_hgrn_w_in, m_hgrn_lb_logits, m_hgrn_onorm, m_w_out, m_final_gain, v_norm_gains, v_fox_w_in, v_fox_b_f, v_hgrn_w_in, v_hgrn_lb_logits, v_hgrn_onorm, v_w_out, v_final_gain):
    _, S, D = x.shape
    H = FOX_HEADS
    W = H * HEAD_DIM
    assert HGRN_HEADS == H and w_out.shape[2] == D
    cf = fox_w_in.shape[2]
    ch = hgrn_w_in.shape[2]
    ro = w_out.shape[1]
    co = hgrn_onorm.shape[1]
    assert N_DEV * cf == 4 * W + H and N_DEV * ch == 4 * W and N_DEV * ro == W and N_DEV * co == W
    x2 = x.reshape(S, D)
    tgt = loss_target.reshape(S, D)

    col = lambda n: (lambda r, i: r.at[:, pl.ds(pl.multiple_of(i * n, n), n)])
    row = lambda n: (lambda r, i: r.at[pl.ds(pl.multiple_of(i * n, n), n), :])
    late_views = [col(ch), row(ro), col(co)]
    late = _send_start(
        [hgrn_w_in[0].astype(BF16), w_out[1].astype(BF16), hgrn_onorm],
        [SDS((D, 4 * W), BF16), SDS((W, D), BF16), SDS((1, W), F32)],
        [lambda r, me, p, m: r] * 3, [lambda r, me, m, v=v: v(r, me) for v in late_views],
        name="gather_layer1_start")
    ng0 = norm_gains[0:1] + late[4][0:1, 0:1]

    wf_g, wo0 = _all_gather(
        [fox_w_in[0].astype(BF16), w_out[0].astype(BF16)], [SDS((N_DEV, D, cf), BF16), SDS((W, D), BF16)],
        [lambda r, p: r.at[p], row(ro)], name="gather_layer0")
    wf = jnp.transpose(wf_g, (1, 0, 2)).reshape(D, N_DEV * cf)
    wf_main = jnp.concatenate([wf[:, :3 * W], wf[:, 3 * W + H:]], axis=1)
    wfl_t = wf[:, 3 * W:3 * W + H].T

    h0 = _rms_fwd(x2, ng0, name="rms0_fwd")
    p0 = _mm_nn([h0], wf_main, BF16, scale_cols=(W, LOG2E * HEAD_DIM ** -0.5), name="fox_in_proj")
    fl_t = _mm_nt_rows(wfl_t, h0, name="fox_forget_proj")
    b_col = fox_b_f.reshape(H, 1)
    kaug = _fox_key_aug(*_fox_gate_fwd(fl_t, b_col, name="fox_gate_fwd"))
    o0, y0, qaug = _fox_fwd(p0, kaug, H=H, name="fox_attn_fwd")
    x1 = _mm_nn([y0], wo0, F32, residual=x2, name="fox_out_proj")

    wh, wo1, onorm = _send_wait(
        late, [lambda r, me, p, m: r] * 3, [lambda r, me, m, v=v: v(r, me) for v in late_views],
        [lambda src, land, me, v=v: (src, v(land, me)) for v in late_views], [(D, ch), (ro, D), (1, co)],
        x1[0:8], name="gather_layer1_wait")
    wh_qig = jnp.concatenate([wh[:, :W], wh[:, 2 * W:]], axis=1)
    wh_f = wh[:, W:2 * W]
    lb = _lb_fwd(hgrn_lb_logits, name="hgrn_lower_bound")
    h1 = _rms_fwd(x1, norm_gains[1:2], name="rms1_fwd")
    p1 = _mm_nn([h1], wh_qig, BF16, name="hgrn_in_proj")
    f1 = _mm_nn([h1], wh_f, F32, name="hgrn_forget_proj")
    o1, y1, states = _hgrn_fwd(p1, f1, lb, onorm, H=H, name="hgrn_fwd")
    xo = _mm_nn([y1], wo1, F32, residual=x1, name="hgrn_out_proj")

    dx2, dx2b, loss_part, dgf = _loss_head(xo, final_gain.reshape(1, D), tgt, name="loss_head")
    loss = lax.psum(jnp.sum(loss_part), ("x", "y", "c"))

    dy1 = _mm_nn([dx2b], wo1, BF16, b_t=True, name="hgrn_out_proj_dx")
    dwo1 = _mm_tn(y1, [dx2b], BF16, name="hgrn_out_proj_dw")
    do1, dg1, donorm = _hgrn_post_bwd(dy1, o1, p1, onorm, H=H, name="hgrn_post_bwd")
    dq1, df1, di1, dlb = _hgrn_bwd(p1, f1, lb, do1, states, H=H, name="hgrn_bwd")
    segs1 = [dq1, df1, di1, dg1]
    dh1 = _mm_nn(segs1, wh, BF16, b_t=True, name="hgrn_in_proj_dx")
    dwh = _mm_tn(h1, segs1, BF16, name="hgrn_in_proj_dw")
    part_views = [col(ch), row(ro)]
    slot = lambda r, me, m: r.at[m]
    ex1 = _send_start([dwh, dwo1], [SDS((N_DEV, D, ch), BF16), SDS((N_DEV, ro, D), BF16)],
                      [lambda r, me, p, m, v=v: v(r, p) for v in part_views], [slot] * 2,
                      name="exchange_layer1_start")
    ng1 = norm_gains[1:2] + ex1[4][0:1, 0:1]
    dx1, dx1b, dng1 = _rms_bwd(x1, ng1, dh1, dx2, name="rms1_bwd")

    dy0 = _mm_nn([dx1b], wo0, BF16, b_t=True, name="fox_out_proj_dx")
    dwo0 = _mm_tn(y0, [dx1b], BF16, name="fox_out_proj_dw")
    do0, dg0, doaug = _fox_post_bwd(dy0, o0, p0, H=H, name="fox_post_bwd")
    dq0, dc_row = _fox_dq(p0, kaug, qaug, do0, doaug, H=H, name="fox_attn_dq")
    dk0, dv0, dc_key = _fox_dkv(p0, kaug, qaug, do0, doaug, H=H, name="fox_attn_dkv")
    dfl_t, dbf = _fox_gate_bwd(dc_row.reshape(H, S), dc_key.reshape(H, S), fl_t, b_col, name="fox_gate_bwd")
    dfl_tb = dfl_t.astype(BF16)
    dwfl_t = _mm_nn([dfl_tb], h0, BF16, name="fox_forget_proj_dw")
    segs0 = [dq0, dk0, dv0, dg0]
    dwf_main = _mm_tn(h0, segs0, BF16, name="fox_in_proj_dw")
    dwf = jnp.concatenate([dwf_main[:, :3 * W], dwfl_t.T, dwf_main[:, 3 * W:]], axis=1)
    dwf_blocks = jnp.transpose(dwf.reshape(D, N_DEV, cf), (1, 0, 2))
    ex0 = _send_start([dwf_blocks, dwo0], [SDS((N_DEV, D, cf), BF16), SDS((N_DEV, ro, D), BF16)],
                      [lambda r, me, p, m: r.at[p], lambda r, me, p, m: row(ro)(r, p)], [slot] * 2,
                      name="exchange_layer0_start")
    wfl_t0 = wfl_t + ex0[4][0:1, 0:1].astype(BF16)
    dh0_f = _mm_nn([dfl_tb.T], wfl_t0, BF16, name="fox_forget_proj_dx")
    dh0 = _mm_nn(segs0, wf_main, BF16, residual=dh0_f, b_t=True, name="fox_in_proj_dx")
    grad_x, _, dng0 = _rms_bwd(x2, norm_gains[0:1], dh0, dx1, name="rms0_bwd")

    own1 = [lambda src, land, me, v=v: (v(src, me), land.at[0]) for v in part_views]
    rh, ro1 = _send_wait(ex1, [lambda r, me, p, m, v=v: v(r, p) for v in part_views], [slot] * 2, own1,
                         [(D, ch), (ro, D)], dng0, name="exchange_layer1_wait")
    rf, ro0 = _send_wait(ex0, [lambda r, me, p, m: r.at[p], lambda r, me, p, m: row(ro)(r, p)], [slot] * 2,
                         [lambda src, land, me: (src.at[me], land.at[0]),
                          lambda src, land, me: (row(ro)(src, me), land.at[0])],
                         [(D, cf), (ro, D)], dng0, name="exchange_layer0_wait")

    pad = lambda a: jnp.pad(a, ((0, 0), (0, W - a.shape[1])))
    stats = jnp.concatenate([dng0, dng1, dlb, dgf, pad(dbf.reshape(1, H)), donorm,
                             jnp.zeros((2, W), F32)], axis=0)
    assert D == W
    (stats_all,) = _all_gather([stats], [SDS((N_DEV, STAT_ROWS, W), F32)], [lambda r, p: r.at[p]],
                               name="gather_small_grads")
    g_small = _stats_reduce(stats_all, hgrn_lb_logits, name="reduce_small_grads")
    me = 4 * lax.axis_index("x") + 2 * lax.axis_index("y") + lax.axis_index("c")
    g_onorm = lax.dynamic_slice_in_dim(g_small[6:7], me * co, co, axis=1)

    def upd(w, m, v, parts, name):
        shp = w.shape
        r2 = (-1, shp[-1])
        g, d, mn, vn = _adamw(w.reshape(r2), m.reshape(r2), v.reshape(r2), parts, name=name)
        return g.reshape(shp), d.reshape(shp), mn.reshape(shp), vn.reshape(shp)

    res = {
        "norm_gains": upd(norm_gains, m_norm_gains, v_norm_gains, [g_small[None, 0:2]], "adamw_norm_gains"),
        "fox_w_in": upd(fox_w_in, m_fox_w_in, v_fox_w_in, [rf], "adamw_fox_w_in"),
        "fox_b_f": upd(fox_b_f, m_fox_b_f, v_fox_b_f, [g_small[None, 5:6, :H]], "adamw_fox_b_f"),
        "hgrn_w_in": upd(hgrn_w_in, m_hgrn_w_in, v_hgrn_w_in, [rh], "adamw_hgrn_w_in"),
        "hgrn_lb_logits": upd(hgrn_lb_logits, m_hgrn_lb_logits, v_hgrn_lb_logits, [g_small[None, 2:4]],
                              "adamw_hgrn_lb_logits"),
        "hgrn_onorm": upd(hgrn_onorm, m_hgrn_onorm, v_hgrn_onorm, [g_onorm[None]], "adamw_hgrn_onorm"),
        "w_out": upd(w_out, m_w_out, v_w_out, [ro0, ro1], "adamw_w_out"),
        "final_gain": upd(final_gain.reshape(1, D), m_final_gain.reshape(1, D), v_final_gain.reshape(1, D),
                          [g_small[None, 4:5]], "adamw_final_gain"),
    }
    order = ["norm_gains", "fox_w_in", "fox_b_f", "hgrn_w_in", "hgrn_lb_logits", "hgrn_onorm", "w_out", "final_gain"]
    fix = lambda n, a: a.reshape(D) if n == "final_gain" else a
    outs = [loss, grad_x.reshape(1, S, D)]
    for k in range(4):
        outs += [fix(n, res[n][k]) for n in order]
    return tuple(outs)
```

```python
import functools

import numpy as np
import jax
import jax.numpy as jnp
from jax import lax
from jax.experimental import pallas as pl
from jax.experimental.pallas import tpu as pltpu

F32 = jnp.float32
BF16 = jnp.bfloat16
SDS = jax.ShapeDtypeStruct
MESH = pl.DeviceIdType.MESH

EPS = 1e-6
ADAM_LR, ADAM_B1, ADAM_B2, ADAM_EPS, ADAM_WD, ADAM_STEP = 0.001, 0.9, 0.999, 1e-08, 0.01, 10

N_DEV = 8
FOX_HEADS = 16
HGRN_HEADS = 16
HEAD_DIM = 128
HGRN_CHUNK = 128
HGRN_LEAF = 16
HGRN_HEADS_PER_STEP = 4
EXP_CLAMP = 85.0
ATT_BLOCK = 512
ATT_HEADS_PER_STEP = 2
ATT_BWD_HEADS_PER_STEP = 2
POST_HEADS_PER_STEP = 4
NEG = -1e30
LOG2E = 1.4426950408889634
LN2 = 0.6931471805599453

VMEM_LIMIT_V7X = 56 * 1024 * 1024


def _params(*sem):
    return pltpu.CompilerParams(dimension_semantics=sem, vmem_limit_bytes=VMEM_LIMIT_V7X)


def _silu(x):
    return x * jax.nn.sigmoid(x)


def _dsilu(x):
    s = jax.nn.sigmoid(x)
    return s * (1.0 + x * (1.0 - s))


def _dot(a, b):
    return jnp.dot(a, b, preferred_element_type=F32)


def _dot_nt(a, b):
    return lax.dot_general(a, b, (((1,), (1,)), ((), ())), preferred_element_type=F32)


def _dot_tn(a, b):
    return lax.dot_general(a, b, (((0,), (0,)), ((), ())), preferred_element_type=F32)


def _mm_nn(a_list, b, out_dtype, *, name, residual=None, scale_cols=None, b_t=False, tm=1024, tn=1024, tk=2048):
    ns = len(a_list)
    M, Ks = a_list[0].shape
    K, N = (b.shape[1], b.shape[0]) if b_t else b.shape
    dot = _dot_nt if b_t else _dot
    assert K == ns * Ks and all(a.shape == (M, Ks) for a in a_list)
    if ns > 1:
        tk = tk // 2
    tm, tn, tk = min(tm, M), min(tn, N), min(tk, Ks)
    assert M % tm == 0 and N % tn == 0 and Ks % tk == 0
    assert scale_cols is None or scale_cols[0] % tn == 0
    nks = Ks // tk
    nk = ns * nks
    has_res = residual is not None

    def body(*refs):
        a_refs, b_ref = refs[:ns], refs[ns]
        res_ref = refs[ns + 1] if has_res else None
        o_ref = refs[ns + 1 + has_res]

        def finish(r):
            if has_res:
                r = r + res_ref[...].astype(F32)
            if scale_cols is not None:
                r = r * jnp.where(pl.program_id(1) < scale_cols[0] // tn, scale_cols[1], 1.0)
            o_ref[...] = r.astype(out_dtype)

        if nk == 1:
            finish(dot(a_refs[0][...], b_ref[...]))
            return
        acc_ref = refs[ns + 2 + has_res]
        k = pl.program_id(2)

        @pl.when(k == 0)
        def _():
            acc_ref[...] = jnp.zeros_like(acc_ref)

        for s in range(ns):
            def step(s=s):
                acc_ref[...] += dot(a_refs[s][...], b_ref[...])

            if ns == 1:
                step()
            else:
                pl.when(k // nks == s)(step)

        @pl.when(k == nk - 1)
        def _():
            finish(acc_ref[...])

    def a_map(i, j, k, s):
        return (i, jnp.clip(k - s * nks, 0, nks - 1))

    in_specs = [pl.BlockSpec((tm, tk), functools.partial(a_map, s=s)) for s in range(ns)]
    if b_t:
        in_specs.append(pl.BlockSpec((tn, tk), lambda i, j, k: (j, k)))
    else:
        in_specs.append(pl.BlockSpec((tk, tn), lambda i, j, k: (k, j)))
    args = list(a_list) + [b]
    if has_res:
        in_specs.append(pl.BlockSpec((tm, tn), lambda i, j, k: (i, j)))
        args.append(residual)
    return pl.pallas_call(
        body, grid=(M // tm, N // tn, nk), in_specs=in_specs,
        out_specs=pl.BlockSpec((tm, tn), lambda i, j, k: (i, j)),
        out_shape=SDS((M, N), out_dtype),
        scratch_shapes=[] if nk == 1 else [pltpu.VMEM((tm, tn), F32)],
        compiler_params=_params("parallel", "parallel", "arbitrary"), name=name,
    )(*args)


def _mm_tn(a, b_list, out_dtype, *, name, tm=1024, tn=2048, tk=512):
    ns = len(b_list)
    S, M = a.shape
    Ns = b_list[0].shape[1]
    assert all(b.shape == (S, Ns) for b in b_list)
    tm, tn, tk = min(tm, M), min(tn, Ns), min(tk, S)
    assert M % tm == 0 and Ns % tn == 0 and S % tk == 0
    njs = Ns // tn
    nk = S // tk

    def body(*refs):
        a_ref, b_refs, o_ref, acc_ref = refs[0], refs[1:1 + ns], refs[1 + ns], refs[2 + ns]
        j, k = pl.program_id(1), pl.program_id(2)

        @pl.when(k == 0)
        def _():
            acc_ref[...] = jnp.zeros_like(acc_ref)

        for s in range(ns):
            def step(s=s):
                acc_ref[...] += _dot_tn(a_ref[...], b_refs[s][...])

            if ns == 1:
                step()
            else:
                pl.when(j // njs == s)(step)

        @pl.when(k == nk - 1)
        def _():
            o_ref[...] = acc_ref[...].astype(out_dtype)

    def b_map(i, j, k, s):
        return (k, jnp.clip(j - s * njs, 0, njs - 1))

    in_specs = [pl.BlockSpec((tk, tm), lambda i, j, k: (k, i))]
    in_specs += [pl.BlockSpec((tk, tn), functools.partial(b_map, s=s)) for s in range(ns)]
    return pl.pallas_call(
        body, grid=(M // tm, ns * njs, nk), in_specs=in_specs,
        out_specs=pl.BlockSpec((tm, tn), lambda i, j, k: (i, j)),
        out_shape=SDS((M, ns * Ns), out_dtype),
        scratch_shapes=[pltpu.VMEM((tm, tn), F32)],
        compiler_params=_params("parallel", "parallel", "arbitrary"), name=name,
    )(a, *b_list)


def _mm_nt_rows(w_t, h, *, name, tn=1024):
    R, K = w_t.shape
    S = h.shape[0]
    tn = min(tn, S)

    def body(w_ref, h_ref, o_ref):
        o_ref[...] = _dot_nt(w_ref[...], h_ref[...])

    return pl.pallas_call(
        body, grid=(S // tn,),
        in_specs=[pl.BlockSpec((R, K), lambda i: (0, 0)), pl.BlockSpec((tn, K), lambda i: (i, 0))],
        out_specs=pl.BlockSpec((R, tn), lambda i: (0, i)),
        out_shape=SDS((R, S), F32), compiler_params=_params("parallel"), name=name,
    )(w_t, h)


def _rms_fwd(x, gain, *, name, tm=512):
    S, D = x.shape
    tm = min(tm, S)

    def body(x_ref, g_ref, h_ref):
        xv = x_ref[...]
        r = lax.rsqrt(jnp.mean(xv * xv, axis=-1, keepdims=True) + EPS)
        h_ref[...] = ((xv * r) * g_ref[...]).astype(BF16)

    return pl.pallas_call(
        body, grid=(S // tm,),
        in_specs=[pl.BlockSpec((tm, D), lambda i: (i, 0)), pl.BlockSpec((1, D), lambda i: (0, 0))],
        out_specs=pl.BlockSpec((tm, D), lambda i: (i, 0)),
        out_shape=SDS((S, D), BF16), compiler_params=_params("parallel"), name=name,
    )(x, gain)


def _rms_bwd(x, gain, dh, dres, *, name, tm=256):
    S, D = x.shape
    tm = min(tm, S)

    def body(x_ref, g_ref, dh_ref, dres_ref, dx_ref, dxb_ref, dg_ref):
        @pl.when(pl.program_id(0) == 0)
        def _():
            dg_ref[...] = jnp.zeros_like(dg_ref)

        xv = x_ref[...]
        r = lax.rsqrt(jnp.mean(xv * xv, axis=-1, keepdims=True) + EPS)
        xh = xv * r
        dhv = dh_ref[...].astype(F32)
        dg_ref[...] += jnp.sum(dhv * xh, axis=0, keepdims=True)
        dxh = dhv * g_ref[...]
        dx = r * (dxh - xh * jnp.mean(dxh * xh, axis=-1, keepdims=True)) + dres_ref[...]
        dx_ref[...] = dx
        dxb_ref[...] = dx.astype(BF16)

    row = pl.BlockSpec((tm, D), lambda i: (i, 0))
    vec = pl.BlockSpec((1, D), lambda i: (0, 0))
    return pl.pallas_call(
        body, grid=(S // tm,), in_specs=[row, vec, row, row], out_specs=[row, row, vec],
        out_shape=[SDS((S, D), F32), SDS((S, D), BF16), SDS((1, D), F32)],
        compiler_params=_params("arbitrary"), name=name,
    )(x, gain, dh, dres)


def _loss_head(x, gain, target, *, name, tm=256):
    S, D = x.shape
    tm = min(tm, S)
    assert tm % 8 == 0 and D % 128 == 0

    def body(x_ref, g_ref, t_ref, dx_ref, dxb_ref, loss_ref, dg_ref):
        @pl.when(pl.program_id(0) == 0)
        def _():
            dg_ref[...] = jnp.zeros_like(dg_ref)
            loss_ref[...] = jnp.zeros_like(loss_ref)

        xv = x_ref[...]
        g = g_ref[...]
        r = lax.rsqrt(jnp.mean(xv * xv, axis=-1, keepdims=True) + EPS)
        xh = xv * r
        err = xh * g - t_ref[...]
        e2 = (err * err).reshape(tm // 8, 8, D).sum(axis=0)
        part = e2[:, 0:128]
        for k in range(1, D // 128):
            part = part + e2[:, k * 128:(k + 1) * 128]
        loss_ref[...] += part * (0.5 / D)
        dy = err * (1.0 / D)
        dg_ref[...] += jnp.sum(dy * xh, axis=0, keepdims=True)
        dxh = dy * g
        dx = r * (dxh - xh * jnp.mean(dxh * xh, axis=-1, keepdims=True))
        dx_ref[...] = dx
        dxb_ref[...] = dx.astype(BF16)

    row = pl.BlockSpec((tm, D), lambda i: (i, 0))
    vec = pl.BlockSpec((1, D), lambda i: (0, 0))
    return pl.pallas_call(
        body, grid=(S // tm,), in_specs=[row, vec, row],
        out_specs=[row, row, pl.BlockSpec((8, 128), lambda i: (0, 0)), vec],
        out_shape=[SDS((S, D), F32), SDS((S, D), BF16), SDS((8, 128), F32), SDS((1, D), F32)],
        compiler_params=_params("arbitrary"), name=name,
    )(x, gain, target)


def _split3(x):
    hi = x.astype(BF16)
    r1 = x - hi.astype(F32)
    mid = r1.astype(BF16)
    lo = (r1 - mid.astype(F32)).astype(BF16)
    return hi, mid, lo


def _split2(x):
    hi = x.astype(BF16)
    lo = (x - hi.astype(F32)).astype(BF16)
    return hi, lo


def _fox_gate_fwd(fl_t, b_col, *, name):
    H, S = fl_t.shape
    L = 128
    tri = jnp.asarray(np.triu(np.ones((L, L), np.float32)), BF16)

    def body(fl_ref, b_ref, tri_ref, hi_ref, mid_ref, lo_ref, carry):
        @pl.when(pl.program_id(0) == 0)
        def _():
            carry[...] = jnp.zeros_like(carry)

        z = fl_ref[...] + b_ref[...]
        lf = jnp.minimum(z, 0.0) - jnp.log(1.0 + jnp.exp(-jnp.abs(z)))
        hi, mid, lo = _split3(lf)
        t = tri_ref[...]
        c = (_dot(hi, t) + _dot(mid, t)) + _dot(lo, t) + carry[...]
        carry[...] = c[:, L - 1:L]
        hi_ref[...], mid_ref[...], lo_ref[...] = _split3(c * (-LOG2E))

    blk = pl.BlockSpec((H, L), lambda i: (0, i))
    return pl.pallas_call(
        body, grid=(S // L,),
        in_specs=[blk, pl.BlockSpec((H, 1), lambda i: (0, 0)), pl.BlockSpec((L, L), lambda i: (0, 0))],
        out_specs=[blk] * 3, out_shape=[SDS((H, S), BF16)] * 3, scratch_shapes=[pltpu.VMEM((H, 1), F32)],
        compiler_params=_params("arbitrary"), name=name,
    )(fl_t, b_col, tri)


def _fox_gate_bwd(dc_row, dc_key, fl_t, b_col, *, name):
    H, S = fl_t.shape
    L = 128
    n = S // L
    tri = jnp.asarray(np.tril(np.ones((L, L), np.float32)), BF16)

    def body(dcr_ref, dck_ref, fl_ref, b_ref, tri_ref, dfl_ref, db_ref, carry):
        @pl.when(pl.program_id(0) == 0)
        def _():
            carry[...] = jnp.zeros_like(carry)
            db_ref[...] = jnp.zeros_like(db_ref)

        hi, mid, lo = _split3(dcr_ref[...] + dck_ref[...])
        t = tri_ref[...]
        dlf = (_dot(hi, t) + _dot(mid, t)) + _dot(lo, t) + carry[...]
        carry[...] = dlf[:, 0:1]
        z = fl_ref[...] + b_ref[...]
        dfl = dlf * jax.nn.sigmoid(-z)
        dfl_ref[...] = dfl
        db_ref[...] += jnp.sum(dfl, axis=1, keepdims=True)

    blk = pl.BlockSpec((H, L), lambda i: (0, n - 1 - i))
    col = pl.BlockSpec((H, 1), lambda i: (0, 0))
    return pl.pallas_call(
        body, grid=(n,), in_specs=[blk, blk, blk, col, pl.BlockSpec((L, L), lambda i: (0, 0))],
        out_specs=[blk, col], out_shape=[SDS((H, S), F32), SDS((H, 1), F32)],
        scratch_shapes=[pltpu.VMEM((H, 1), F32)], compiler_params=_params("arbitrary"), name=name,
    )(dc_row, dc_key, fl_t, b_col, tri)


AUG = HEAD_DIM


def _lane_select(cols, shape):
    lane = lax.broadcasted_iota(jnp.int32, shape, 1)
    out = jnp.zeros(shape, BF16)
    for k, c in reversed(list(enumerate(cols))):
        c = jnp.full(shape, c, BF16) if isinstance(c, (int, float)) else jnp.broadcast_to(c, shape).astype(BF16)
        out = jnp.where(lane == k, c, out)
    return out


def _fox_key_aug(b_hi, b_mid, b_lo):
    H, S = b_hi.shape
    ones = jnp.ones((H, S), BF16)
    ka = jnp.stack([b_hi, b_mid, b_lo, ones, ones, ones], axis=-1)
    ka = jnp.pad(ka, ((0, 0), (0, 0), (0, AUG - 6)))
    return jnp.transpose(ka, (1, 0, 2)).reshape(S, H * AUG)


def _fox_fwd(p0, kaug, *, H, name):
    S = p0.shape[0]
    T = min(ATT_BLOCK, S)
    nq = S // T
    dh = HEAD_DIM
    G = ATT_HEADS_PER_STEP
    assert H % G == 0

    def body(q_ref, k_ref, ka_ref, v_ref, g_ref, o_ref, y_ref, qa_ref, m_sc, acc_sc, p_sc, al_sc):
        i = pl.program_id(1)
        qaug = _lane_select([1.0, 1.0, 1.0], (T, AUG))
        ones = jnp.ones((T, dh), BF16)
        m_sc[...] = jnp.full_like(m_sc, NEG)
        acc_sc[...] = jnp.zeros_like(acc_sc)
        p_sc[...] = jnp.zeros_like(p_sc)
        al_sc[...] = jnp.ones_like(al_sc)

        def step(j, masked):
            rows = pl.ds(pl.multiple_of(j * T, T), T)
            prev = pl.ds(pl.multiple_of(jnp.maximum(j - 1, 0) * T, T), T)
            for g in range(G):
                hd = slice(g * dh, (g + 1) * dh)
                vp = jnp.concatenate([v_ref[prev, hd], ones], axis=1)
                acc_sc[g] = jnp.tile(al_sc[g], (1, 2)) * acc_sc[g] + _dot(p_sc[g], vp)
                q = jnp.concatenate([q_ref[:, hd], qaug], axis=1)
                kj = jnp.concatenate([k_ref[rows, hd], ka_ref[rows, hd]], axis=1)
                t = _dot_nt(q, kj)
                if masked:
                    row = lax.broadcasted_iota(jnp.int32, (T, T), 0)
                    col = lax.broadcasted_iota(jnp.int32, (T, T), 1)
                    t = jnp.where(row >= col, t, NEG)
                m_prev = m_sc[g]
                m_new = jnp.maximum(m_prev, jnp.max(t, axis=-1, keepdims=True))
                p_sc[g] = jnp.exp2(t - jnp.tile(m_new, (1, T // 128))).astype(BF16)
                al_sc[g] = jnp.exp2(m_prev - m_new)
                m_sc[g] = m_new

        def loop_body(j, carry):
            step(j, False)
            return carry

        lax.fori_loop(0, i, loop_body, 0)
        step(i, True)
        rows = pl.ds(pl.multiple_of(i * T, T), T)
        for g in range(G):
            hd = slice(g * dh, (g + 1) * dh)
            vp = jnp.concatenate([v_ref[rows, hd], ones], axis=1)
            acc = jnp.tile(al_sc[g], (1, 2)) * acc_sc[g] + _dot(p_sc[g], vp)
            l = acc[:, dh:]
            o = acc[:, :dh] / l
            o_ref[:, hd] = o
            y_ref[:, hd] = (o * _silu(g_ref[:, hd].astype(F32))).astype(BF16)
            hi, mid, lo = _split3(-(m_sc[g] + jnp.log2(l)))
            qa_ref[:, hd] = _lane_select([1.0, 1.0, 1.0, hi, mid, lo], (T, AUG))

    blk = lambda off: pl.BlockSpec((T, G * dh), lambda h, i: (i, off // G + h))
    full = lambda off: pl.BlockSpec((S, G * dh), lambda h, i: (0, off // G + h))
    return pl.pallas_call(
        body, grid=(H // G, nq),
        in_specs=[blk(0), full(H), full(0), full(2 * H), blk(3 * H)],
        out_specs=[blk(0), blk(0), blk(0)],
        out_shape=[SDS((S, H * dh), F32), SDS((S, H * dh), BF16), SDS((S, H * AUG), BF16)],
        scratch_shapes=[pltpu.VMEM((G, T, 128), F32), pltpu.VMEM((G, T, 2 * dh), F32),
                        pltpu.VMEM((G, T, T), BF16), pltpu.VMEM((G, T, 128), F32)],
        compiler_params=_params("parallel", "arbitrary"), name=name,
    )(p0, p0, kaug, p0, p0)


def _fox_post_bwd(dy, o, p0, *, H, name, tm=512):
    S = dy.shape[0]
    dh = HEAD_DIM
    tm = min(tm, S)
    G = POST_HEADS_PER_STEP
    assert H % G == 0

    def body(dy_ref, o_ref, g_ref, do_ref, dg_ref, da_ref):
        dyv = dy_ref[...].astype(F32)
        ov = o_ref[...]
        g = g_ref[...].astype(F32)
        do = (dyv * _silu(g)).astype(BF16)
        do_ref[...] = do
        dg_ref[...] = (dyv * ov * _dsilu(g)).astype(BF16)
        prod = do.astype(F32) * ov
        for k in range(G):
            hd = slice(k * dh, (k + 1) * dh)
            delta = jnp.sum(prod[:, hd], axis=-1, keepdims=True)
            hi, mid, lo = _split3(-jnp.broadcast_to(delta, (tm, AUG)))
            da_ref[:, hd] = _lane_select([hi, mid, lo], (tm, AUG))

    blk = pl.BlockSpec((tm, G * dh), lambda h, i: (i, h))
    return pl.pallas_call(
        body, grid=(H // G, S // tm),
        in_specs=[blk, blk, pl.BlockSpec((tm, G * dh), lambda h, i: (i, 3 * H // G + h))],
        out_specs=[blk, blk, blk],
        out_shape=[SDS((S, H * dh), BF16), SDS((S, H * dh), BF16), SDS((S, H * AUG), BF16)],
        compiler_params=_params("parallel", "parallel"), name=name,
    )(dy, o, p0)


def _fox_bwd(p0, kaug, qaug, do, doaug, *, H, name):
    S = p0.shape[0]
    T = min(ATT_BLOCK, S)
    nq = S // T
    dh = HEAD_DIM
    scale = dh ** -0.5
    G = ATT_BWD_HEADS_PER_STEP
    assert H % G == 0

    def body(q_ref, qa_ref, k_ref, ka_ref, v_ref, do_ref, da_ref, dq_ref, rs_ref, dk_ref, dv_ref, dc_ref,
             dq_sc, dk_sc, dv_sc, pt_sc, dst_sc):
        j = pl.program_id(1)
        vaug = _lane_select([1.0, 1.0, 1.0], (T, AUG))
        ones = jnp.ones((T, dh), BF16)

        @pl.when(j == 0)
        def _():
            dq_sc[...] = jnp.zeros_like(dq_sc)

        dk_sc[...] = jnp.zeros_like(dk_sc)
        dv_sc[...] = jnp.zeros_like(dv_sc)
        pt_sc[...] = jnp.zeros_like(pt_sc)
        dst_sc[...] = jnp.zeros_like(dst_sc)

        def apply(prev):
            for g in range(G):
                hd = slice(g * dh, (g + 1) * dh)
                dv_sc[g] += _dot(pt_sc[g], do_ref[prev, hd])
                dk_sc[g] += _dot(dst_sc[g], jnp.concatenate([q_ref[prev, hd], ones], axis=1))
                dq_sc[g, prev] += _dot_tn(dst_sc[g], jnp.concatenate([k_ref[:, hd], ones], axis=1))

        def step(i, masked):
            rows = pl.ds(pl.multiple_of(i * T, T), T)
            apply(pl.ds(pl.multiple_of(jnp.maximum(i - 1, j) * T, T), T))
            for g in range(G):
                hd = slice(g * dh, (g + 1) * dh)
                k = jnp.concatenate([k_ref[:, hd], ka_ref[:, hd]], axis=1)
                v = jnp.concatenate([v_ref[:, hd], vaug], axis=1)
                pt = jnp.exp2(_dot_nt(k, jnp.concatenate([q_ref[rows, hd], qa_ref[rows, hd]], axis=1)))
                if masked:
                    row = lax.broadcasted_iota(jnp.int32, (T, T), 0)
                    col = lax.broadcasted_iota(jnp.int32, (T, T), 1)
                    pt = jnp.where(col >= row, pt, 0.0)
                dst = pt * _dot_nt(v, jnp.concatenate([do_ref[rows, hd], da_ref[rows, hd]], axis=1))
                pt_sc[g] = pt.astype(BF16)
                dst_sc[g] = dst.astype(BF16)

        step(j, True)

        def loop_body(i, carry):
            step(i, False)
            return carry

        lax.fori_loop(j + 1, nq, loop_body, 0)
        apply(pl.ds((nq - 1) * T, T))
        for g in range(G):
            hd = slice(g * dh, (g + 1) * dh)
            dk_ref[:, hd] = (dk_sc[g, :, :dh] * LN2).astype(BF16)
            dv_ref[:, hd] = dv_sc[g].astype(BF16)
            dc_ref[g] = -jnp.transpose(dk_sc[g, :, dh:])[0:1]

        @pl.when(j == nq - 1)
        def _():
            for g in range(G):
                dq_ref[:, g * dh:(g + 1) * dh] = (dq_sc[g, :, :dh] * scale).astype(BF16)
                for i in range(nq):
                    rs_ref[g, :, i * T:(i + 1) * T] = jnp.transpose(dq_sc[g, i * T:(i + 1) * T, dh:])[0:1]

    blk = lambda off: pl.BlockSpec((T, G * dh), lambda h, j: (j, off // G + h))
    full = lambda off: pl.BlockSpec((S, G * dh), lambda h, j: (0, off // G + h))
    once = lambda off: pl.BlockSpec((S, G * dh), lambda h, j: (0, off // G + h), pipeline_mode=pl.Buffered(1))
    rowv = pl.BlockSpec((G, 1, T), lambda h, j: (h, 0, j))
    return pl.pallas_call(
        body, grid=(H // G, nq),
        in_specs=[once(0), once(0), blk(H), blk(0), blk(2 * H), once(0), once(0)],
        out_specs=[full(0), pl.BlockSpec((G, 1, S), lambda h, j: (h, 0, 0)), blk(0), blk(0), rowv],
        out_shape=[SDS((S, H * dh), BF16), SDS((H, 1, S), F32), SDS((S, H * dh), BF16), SDS((S, H * dh), BF16),
                   SDS((H, 1, S), F32)],
        scratch_shapes=[pltpu.VMEM((G, S, 2 * dh), F32), pltpu.VMEM((G, T, 2 * dh), F32), pltpu.VMEM((G, T, dh), F32),
                        pltpu.VMEM((G, T, T), BF16), pltpu.VMEM((G, T, T), BF16)],
        compiler_params=_params("parallel", "arbitrary"), name=name,
    )(p0, qaug, p0, kaug, p0, do, doaug)


def _hgrn_levels(C, leaf):
    levels = []
    h = C // 2
    while h >= leaf:
        levels.append(h)
        h //= 2
    return levels


def _hgrn_sum_matrix(C, leaf):
    t = np.arange(C)[:, None]
    u = np.arange(C)[None, :]
    mats = [(u <= t), (u > t)]
    for h in _hgrn_levels(C, leaf):
        start = (t // (2 * h)) * (2 * h)
        mid = start + h - 1
        second = t > mid
        m = np.where(second, (u > mid) & (u <= t), (u > t) & (u <= mid))
        mats.append(m)
    lstart = (t // leaf) * leaf
    mats.append((u >= lstart) & (u <= t))
    return np.concatenate([m.astype(np.float32) for m in mats], axis=0)


def _hgrn_chunk_terms(qr, fz, lb, msum, C, leaf):
    levels = _hgrn_levels(C, leaf)
    sq = _silu(qr)
    t = jnp.exp(-jnp.abs(fz))
    r = 1.0 / (1.0 + t)
    pos = fz >= 0.0
    sp = jnp.where(pos, r, t * r)
    sn = jnp.where(pos, t * r, r)
    f = lb + (1.0 - lb) * sp
    lf = jnp.log(f)
    k = (1.0 - lb) * sn
    hi, lo = _split2(lf)
    dsum = _dot(msum, hi) + _dot(msum, lo)
    b = dsum[0:C]
    kdec = dsum[C:2 * C]
    rowi = lax.broadcasted_iota(jnp.int32, (C, 1), 0)
    lev = []
    for n, h in enumerate(levels):
        e = jnp.exp(dsum[(2 + n) * C:(3 + n) * C])
        second = (rowi % (2 * h)) >= h
        qm = jnp.where(second, sq * e, 0.0).astype(BF16)
        km = jnp.where(second, 0.0, k * e).astype(BF16)
        lev.append((h, e, second, qm, km))
    dleaf = dsum[(2 + len(levels)) * C:(3 + len(levels)) * C]
    eq = jnp.exp(dleaf)
    ek = jnp.exp(jnp.minimum(-dleaf, EXP_CLAMP))
    return dict(sq=sq, sp=sp, sn=sn, f=f, k=k, b=b, kdec=kdec, lev=lev, eq=eq, ek=ek,
                ql=(sq * eq).astype(BF16), kl=(k * ek).astype(BF16),
                qs=(sq * jnp.exp(b)).astype(BF16), ke=(k * jnp.exp(kdec)).astype(BF16),
                e_c=jnp.exp(b[C - 1:C, :]))


def _hgrn_masks(C, leaf, transposed):
    a = lax.broadcasted_iota(jnp.int32, (C, C), 0)
    bb = lax.broadcasted_iota(jnp.int32, (C, C), 1)
    t, s = (bb, a) if transposed else (a, bb)
    lev = [None if 2 * h == C else (t // (2 * h)) == (s // (2 * h)) for h in _hgrn_levels(C, leaf)]
    if leaf == C:
        leafm = s <= t
    else:
        leafm = ((t // leaf) == (s // leaf)) & (s <= t)
    return lev, leafm


def _hgrn_fwd(p1, f1, lb, onorm, *, H, name, tb=512):
    S = p1.shape[0]
    dk = HEAD_DIM
    C = min(HGRN_CHUNK, S)
    leaf = min(HGRN_LEAF, C)
    tb = min(tb, S)
    nc = tb // C
    G = HGRN_HEADS_PER_STEP
    assert H % G == 0
    msum = jnp.asarray(_hgrn_sum_matrix(C, leaf), BF16)

    def body(q_ref, f_ref, v_ref, g_ref, lb_ref, on_ref, ms_ref, o_ref, y_ref, st_ref, st_sc):
        @pl.when(pl.program_id(1) == 0)
        def _():
            st_sc[...] = jnp.zeros_like(st_sc)

        msv = ms_ref[...]
        lmask, leafm = _hgrn_masks(C, leaf, False)

        def chunk(n, carry):
            rows = pl.ds(pl.multiple_of(n * C, C), C)
            for g in range(G):
                hd = slice(g * dk, (g + 1) * dk)
                tm = _hgrn_chunk_terms(q_ref[rows, hd].astype(F32), f_ref[rows, hd], lb_ref[:, hd], msv, C, leaf)
                v = v_ref[rows, hd]
                st = st_sc[g]
                st_ref[g, n] = st
                a = jnp.where(leafm, _dot_nt(tm["ql"], tm["kl"]), 0.0)
                for (h, e, second, qm, km), m in zip(tm["lev"], lmask):
                    al = _dot_nt(qm, km)
                    a = a + (al if m is None else jnp.where(m, al, 0.0))
                o = _dot_nt(tm["qs"], st.astype(BF16)) + _dot(a.astype(BF16), v)
                st_sc[g] = st * tm["e_c"] + _dot(v.T, tm["ke"])
                o_ref[rows, hd] = o
                rn = lax.rsqrt(jnp.mean(o * o, axis=-1, keepdims=True) + EPS)
                y = ((o * rn) * on_ref[:, hd]) * _silu(g_ref[rows, hd].astype(F32))
                y_ref[rows, hd] = y.astype(BF16)
            return carry

        lax.fori_loop(0, nc, chunk, 0)

    blk = lambda off: pl.BlockSpec((tb, G * dk), lambda h, i: (i, off // G + h))
    vec = pl.BlockSpec((1, G * dk), lambda h, i: (0, h))
    return pl.pallas_call(
        body, grid=(H // G, S // tb),
        in_specs=[blk(0), blk(0), blk(H), blk(2 * H), vec, vec,
                  pl.BlockSpec(msum.shape, lambda h, i: (0, 0))],
        out_specs=[blk(0), blk(0), pl.BlockSpec((G, nc, dk, dk), lambda h, i: (h, i, 0, 0))],
        out_shape=[SDS((S, H * dk), F32), SDS((S, H * dk), BF16), SDS((H, S // C, dk, dk), F32)],
        scratch_shapes=[pltpu.VMEM((G, dk, dk), F32)],
        compiler_params=_params("parallel", "arbitrary"), name=name,
    )(p1, f1, p1, p1, lb, onorm, msum)


def _hgrn_post_bwd(dy, o, p1, onorm, *, H, name, tm=512):
    S = dy.shape[0]
    dk = HEAD_DIM
    tm = min(tm, S)
    G = POST_HEADS_PER_STEP
    assert H % G == 0

    def body(dy_ref, o_ref, g_ref, on_ref, do_ref, dg_ref, don_ref):
        @pl.when(pl.program_id(1) == 0)
        def _():
            don_ref[...] = jnp.zeros_like(don_ref)

        for k in range(G):
            hd = slice(k * dk, (k + 1) * dk)
            dyv = dy_ref[:, hd].astype(F32)
            ov = o_ref[:, hd]
            g = g_ref[:, hd].astype(F32)
            onv = on_ref[:, hd]
            rn = lax.rsqrt(jnp.mean(ov * ov, axis=-1, keepdims=True) + EPS)
            oh = ov * rn
            dn = dyv * _silu(g)
            dg_ref[:, hd] = (dyv * (oh * onv) * _dsilu(g)).astype(BF16)
            don_ref[:, hd] += jnp.sum(dn * oh, axis=0, keepdims=True)
            doh = dn * onv
            do_ref[:, hd] = (rn * (doh - oh * jnp.mean(doh * oh, axis=-1, keepdims=True))).astype(BF16)

    blk = pl.BlockSpec((tm, G * dk), lambda h, i: (i, h))
    vec = pl.BlockSpec((1, G * dk), lambda h, i: (0, h))
    return pl.pallas_call(
        body, grid=(H // G, S // tm),
        in_specs=[blk, blk, pl.BlockSpec((tm, G * dk), lambda h, i: (i, 2 * H // G + h)), vec],
        out_specs=[blk, blk, vec],
        out_shape=[SDS((S, H * dk), BF16), SDS((S, H * dk), BF16), SDS((1, H * dk), F32)],
        compiler_params=_params("parallel", "arbitrary"), name=name,
    )(dy, o, p1, onorm)


def _hgrn_bwd(p1, f1, lb, do, states, *, H, name, tb=512):
    S = p1.shape[0]
    dk = HEAD_DIM
    C = min(HGRN_CHUNK, S)
    leaf = min(HGRN_LEAF, C)
    tb = min(tb, S)
    nc = tb // C
    nb = S // tb
    G = HGRN_HEADS_PER_STEP
    assert H % G == 0
    msum = jnp.asarray(_hgrn_sum_matrix(C, leaf), BF16)
    rtri = jnp.asarray(np.triu(np.ones((C, C), np.float32)), BF16)

    def body(q_ref, f_ref, v_ref, do_ref, st_ref, lb_ref, ms_ref, rt_ref,
             dq_ref, df_ref, dv_ref, dlb_ref, g_sc):
        @pl.when(pl.program_id(1) == 0)
        def _():
            g_sc[...] = jnp.zeros_like(g_sc)
            dlb_ref[...] = jnp.zeros_like(dlb_ref)

        msv = ms_ref[...]
        rtv = rt_ref[...]
        lmask, leafm = _hgrn_masks(C, leaf, False)
        lmask_t, leafm_t = _hgrn_masks(C, leaf, True)
        f32 = lambda z: z.astype(F32)

        def head_chunk(g, n):
            hd = slice(g * dk, (g + 1) * dk)
            rows = pl.ds(pl.multiple_of(n * C, C), C)
            lbv = lb_ref[:, hd]
            qr = q_ref[rows, hd].astype(F32)
            tm = _hgrn_chunk_terms(qr, f_ref[rows, hd], lbv, msv, C, leaf)
            v = v_ref[rows, hd]
            dov = do_ref[rows, hd]
            st0 = st_ref[g, n]
            gt = g_sc[g]
            gtb = gt.astype(BF16)
            da = _dot_nt(dov, v)
            da_t = _dot_nt(v, dov)

            dal = jnp.where(leafm, da, 0.0).astype(BF16)
            dal_t = jnp.where(leafm_t, da_t, 0.0).astype(BF16)
            dql = _dot(dal, tm["kl"])
            dkl = _dot(dal_t, tm["ql"])
            dsq = dql * tm["eq"]
            dkk = dkl * tm["ek"]
            xq = f32(tm["ql"]) * dql
            xk = f32(tm["kl"]) * dkl
            a_t = jnp.where(leafm_t, _dot_nt(tm["kl"], tm["ql"]), 0.0)
            for (h, e, second, qm, km), m, m_t in zip(tm["lev"], lmask, lmask_t):
                dl = (da if m is None else jnp.where(m, da, 0.0)).astype(BF16)
                dl_t = (da_t if m_t is None else jnp.where(m_t, da_t, 0.0)).astype(BF16)
                dqm = _dot(dl, km)
                dkm = _dot(dl_t, qm)
                dsq = dsq + jnp.where(second, dqm * e, 0.0)
                dkk = dkk + jnp.where(second, 0.0, dkm * e)
                xq = xq + f32(qm) * dqm
                xk = xk + f32(km) * dkm
                al_t = _dot_nt(km, qm)
                a_t = a_t + (al_t if m_t is None else jnp.where(m_t, al_t, 0.0))
            dqs = _dot(dov, st0.astype(BF16))
            dke = _dot(v, gtb)
            dsq = dsq + dqs * jnp.exp(tm["b"])
            dkk = dkk + dke * jnp.exp(tm["kdec"])
            xq = xq + f32(tm["qs"]) * dqs
            xk = xk + f32(tm["ke"]) * dke
            dvv = _dot(a_t.astype(BF16), dov) + _dot_nt(tm["ke"], gtb)
            r_end = jnp.sum(f32(gtb) * _dot(v.T, tm["ke"]) + gt * (st0 * tm["e_c"]), axis=0, keepdims=True)
            g_sc[g] = gt * tm["e_c"] + _dot(dov.T, tm["qs"])
            xh, xm, xl = _split3(xq - xk)
            dlf = (_dot(rtv, xh) + _dot(rtv, xm)) + _dot(rtv, xl) + r_end
            dlf_f = dlf / tm["f"]
            dsp = (1.0 - lbv) * (dlf_f - dkk)
            df_ref[rows, hd] = (dsp * (tm["sp"] * tm["sn"])).astype(BF16)
            dq_ref[rows, hd] = (dsq * _dsilu(qr)).astype(BF16)
            dv_ref[rows, hd] = dvv.astype(BF16)
            dlb_ref[:, hd] += jnp.sum(dlf_f * tm["sn"] - dkk * tm["sn"], axis=0, keepdims=True)

        def chunk(nn, carry):
            for g in range(G):
                head_chunk(g, nc - 1 - nn)
            return carry

        lax.fori_loop(0, nc, chunk, 0)

    blk = lambda off: pl.BlockSpec((tb, G * dk), lambda h, i: (nb - 1 - i, off // G + h))
    vec = pl.BlockSpec((1, G * dk), lambda h, i: (0, h))
    return pl.pallas_call(
        body, grid=(H // G, nb),
        in_specs=[blk(0), blk(0), blk(H), blk(0),
                  pl.BlockSpec((G, nc, dk, dk), lambda h, i: (h, nb - 1 - i, 0, 0)), vec,
                  pl.BlockSpec(msum.shape, lambda h, i: (0, 0)), pl.BlockSpec((C, C), lambda h, i: (0, 0))],
        out_specs=[blk(0), blk(0), blk(0), vec],
        out_shape=[SDS((S, H * dk), BF16)] * 3 + [SDS((1, H * dk), F32)],
        scratch_shapes=[pltpu.VMEM((G, dk, dk), F32)],
        compiler_params=_params("parallel", "arbitrary"), name=name,
    )(p1, f1, p1, do, states, lb, msum, rtri)


def _lb_fwd(logits, *, name):
    W = logits.shape[1]

    def body(l_ref, lb_ref):
        l = l_ref[...]
        m = jnp.max(l, axis=0, keepdims=True)
        e = jnp.exp(l - m)
        p = e / jnp.sum(e, axis=0, keepdims=True)
        lb_ref[...] = (p[0:1] + p[1:2]) - p[0:1]

    return pl.pallas_call(body, out_shape=SDS((1, W), F32), name=name)(logits)


STAT_ROWS = 8


def _stats_reduce(stats_all, logits, *, name):
    W = logits.shape[1]

    def body(s_ref, l_ref, g_ref):
        tot = s_ref[0]
        for d in range(1, N_DEV):
            tot = tot + s_ref[d]
        l = l_ref[...]
        m = jnp.max(l, axis=0, keepdims=True)
        e = jnp.exp(l - m)
        p = e / jnp.sum(e, axis=0, keepdims=True)
        dlb = tot[2:3]
        dl0 = -(p[0:1] * p[1:2]) * dlb
        dl1 = (p[1:2] * (1.0 - p[1:2])) * dlb
        g_ref[0:2] = tot[0:2]
        g_ref[2:3] = dl0
        g_ref[3:4] = dl1
        g_ref[4:7] = tot[3:6]
        g_ref[7:8] = jnp.zeros((1, W), F32)

    return pl.pallas_call(body, out_shape=SDS((STAT_ROWS, W), F32), name=name)(stats_all, logits)


def _adamw(w, m, v, g_parts, *, name, tr=128):
    R, C = w.shape
    ns = len(g_parts)
    n, Rs = g_parts[0].shape[0], g_parts[0].shape[1]
    assert all(p.shape == (n, Rs, C) for p in g_parts) and ns * Rs == R
    tr = min(tr, Rs)
    assert Rs % tr == 0
    nts = Rs // tr
    c1 = 1.0 / (1.0 - ADAM_B1 ** ADAM_STEP)
    c2 = 1.0 / (1.0 - ADAM_B2 ** ADAM_STEP)

    def body(*refs):
        w_ref, m_ref, v_ref = refs[:3]
        g_refs = refs[3:3 + ns]
        go_ref, d_ref, mo_ref, vo_ref = refs[3 + ns:]

        def update(g_ref):
            g = g_ref[0].astype(F32)
            for k in range(1, n):
                g = g + g_ref[k].astype(F32)
            mn = ADAM_B1 * m_ref[...] + (1.0 - ADAM_B1) * g
            vn = ADAM_B2 * v_ref[...] + (1.0 - ADAM_B2) * (g * g)
            d_ref[...] = -ADAM_LR * ((mn * c1) / (jnp.sqrt(vn * c2) + ADAM_EPS) + ADAM_WD * w_ref[...])
            go_ref[...] = g
            mo_ref[...] = mn
            vo_ref[...] = vn

        for s in range(ns):
            if ns == 1:
                update(g_refs[s])
            else:
                pl.when(pl.program_id(0) // nts == s)(functools.partial(update, g_refs[s]))

    def g_map(i, s):
        return (0, jnp.clip(i - s * nts, 0, nts - 1), 0)

    blk = pl.BlockSpec((tr, C), lambda i: (i, 0))
    return pl.pallas_call(
        body, grid=(R // tr,),
        in_specs=[blk, blk, blk] + [pl.BlockSpec((n, tr, C), functools.partial(g_map, s=s)) for s in range(ns)],
        out_specs=[blk] * 4, out_shape=[SDS((R, C), F32)] * 4,
        compiler_params=_params("parallel"), name=name,
    )(w, m, v, *g_parts)


ANY = pl.BlockSpec(memory_space=pl.ANY)
STAGE_BYTES = 2 * 1024 * 1024


def _stage_shape(shape, dtype):
    row_bytes = int(np.prod(shape[1:])) * jnp.dtype(dtype).itemsize
    rows = max(1, min(shape[0], STAGE_BYTES // row_bytes))
    while shape[0] % rows:
        rows -= 1
    return (rows,) + tuple(shape[1:])


def _staged_copy(frm, to, buf, sems):
    rows = buf.shape[0]
    for r0 in range(0, frm.shape[0], rows):
        cp = pltpu.make_async_copy(frm.at[pl.ds(r0, rows)], buf, sems.at[0])
        cp.start()
        cp.wait()
        cp = pltpu.make_async_copy(buf, to.at[pl.ds(r0, rows)], sems.at[1])
        cp.start()
        cp.wait()


def _all_gather(shards, out_shapes, views, *, name):
    n = len(shards)

    def body(*refs):
        ins, outs = refs[:n], refs[n:2 * n]
        send_sems, recv_sems, local_sems = refs[2 * n:2 * n + 3]
        bufs = refs[2 * n + 3:]
        x, y, c = lax.axis_index("x"), lax.axis_index("y"), lax.axis_index("c")
        me, sibling = (x, y, c), (x, y, 1 - c)
        chips = [(1 - x, y), (x, 1 - y), (1 - x, 1 - y)]

        def dev(p):
            return 4 * p[0] + 2 * p[1] + p[2]

        def copy(a, k, block, to, src=None):
            dst = views[a](outs[a], dev(block))
            return pltpu.make_async_remote_copy(
                src_ref=dst if src is None else src, dst_ref=dst,
                send_sem=send_sems.at[a, k], recv_sem=recv_sems.at[a, k],
                device_id=to, device_id_type=MESH)

        first, passed = [], []
        for a in range(n):
            first.append(copy(a, 0, me, sibling, src=ins[a]))
            first += [copy(a, 1 + j, me, (*chip, c), src=ins[a]) for j, chip in enumerate(chips)]
        for cp in first:
            cp.start()
        for a in range(n):
            _staged_copy(ins[a], views[a](outs[a], dev(me)), bufs[a], local_sems)
        for j, chip in enumerate(chips):
            for a in range(n):
                copy(a, 1 + j, (*chip, c), me).wait_recv()
                cp = copy(a, 4 + j, (*chip, c), sibling)
                cp.start()
                passed.append(cp)
        for a in range(n):
            copy(a, 0, sibling, me).wait_recv()
            for j, chip in enumerate(chips):
                copy(a, 4 + j, (*chip, 1 - c), me).wait_recv()
        for cp in first + passed:
            cp.wait_send()

    return pl.pallas_call(
        body, in_specs=[ANY] * n, out_specs=[ANY] * n, out_shape=list(out_shapes),
        scratch_shapes=[pltpu.SemaphoreType.DMA((n, 7)), pltpu.SemaphoreType.DMA((n, 7)),
                        pltpu.SemaphoreType.DMA((2,))]
        + [pltpu.VMEM(_stage_shape(s.shape, s.dtype), s.dtype) for s in shards],
        name=name,
    )(*shards)


HBM = pl.BlockSpec(memory_space=pltpu.HBM)
SEM = pl.BlockSpec(memory_space=pltpu.SEMAPHORE)
EFFECT = pltpu.SideEffectType.DATAFLOW_SIDE_EFFECTING


def _relations(x, y, c):
    for m in range(1, N_DEV):
        yield m, (1 - x if m & 4 else x, 1 - y if m & 2 else y, 1 - c if m & 1 else c)


def _dev_id(p):
    return 4 * p[0] + 2 * p[1] + p[2]


def _send_start(srcs, land_shapes, src_views, dst_views, *, name):
    n = len(srcs)

    def body(*refs):
        ins, lands = refs[:n], refs[n:2 * n]
        send_sems, recv_sems, token = refs[2 * n], refs[2 * n + 1], refs[-1]
        x, y, c = lax.axis_index("x"), lax.axis_index("y"), lax.axis_index("c")
        me = _dev_id((x, y, c))
        for m, p in _relations(x, y, c):
            for a in range(n):
                pltpu.make_async_remote_copy(
                    src_ref=src_views[a](ins[a], me, _dev_id(p), m), dst_ref=dst_views[a](lands[a], me, m),
                    send_sem=send_sems.at[a * (N_DEV - 1) + m - 1], recv_sem=recv_sems.at[a * (N_DEV - 1) + m - 1],
                    device_id=p, device_id_type=MESH).start()
        token[...] = jnp.zeros_like(token)

    lands = [pltpu.with_memory_space_constraint(lax.empty(s.shape, s.dtype), pltpu.HBM) for s in land_shapes]
    srcs = [pltpu.with_memory_space_constraint(v, pltpu.HBM) for v in srcs]
    res = pl.pallas_call(
        body, name=name,
        out_shape=[pltpu.SemaphoreType.DMA((n * (N_DEV - 1),)), pltpu.SemaphoreType.DMA((n * (N_DEV - 1),))]
        + [pltpu.HBM(v.shape, v.dtype) for v in srcs] + [pltpu.HBM(s.shape, s.dtype) for s in land_shapes]
        + [SDS((8, 128), F32)],
        in_specs=[HBM] * (2 * n), out_specs=[SEM, SEM] + [HBM] * (2 * n) + [pl.BlockSpec(memory_space=pltpu.VMEM)],
        input_output_aliases={i: 2 + i for i in range(2 * n)},
        compiler_params=pltpu.CompilerParams(has_side_effects=EFFECT),
    )(*srcs, *lands)
    return res[0], res[1], res[2:2 + n], res[2 + n:2 + 2 * n], res[-1]


def _send_wait(started, src_views, dst_views, own_views, own_shapes, after, *, name):
    send_sems, recv_sems, srcs, lands, _ = started
    n = len(srcs)

    def body(*refs):
        ins, lnd = refs[:n], refs[n:2 * n]
        send_sems, recv_sems = refs[2 * n], refs[2 * n + 1]
        got = refs[2 * n + 3 + n:2 * n + 3 + 2 * n]
        local_sems = refs[2 * n + 3 + 2 * n]
        bufs = refs[2 * n + 4 + 2 * n:]
        x, y, c = lax.axis_index("x"), lax.axis_index("y"), lax.axis_index("c")
        me = _dev_id((x, y, c))
        for m, p in _relations(x, y, c):
            for a in range(n):
                cp = pltpu.make_async_remote_copy(
                    src_ref=src_views[a](ins[a], me, _dev_id(p), m), dst_ref=dst_views[a](lnd[a], me, m),
                    send_sem=send_sems.at[a * (N_DEV - 1) + m - 1], recv_sem=recv_sems.at[a * (N_DEV - 1) + m - 1],
                    device_id=p, device_id_type=MESH)
                cp.wait_send()
                cp.wait_recv()
        for a in range(n):
            frm, to = own_views[a](ins[a], got[a], me)
            _staged_copy(frm, to, bufs[a], local_sems)

    res = pl.pallas_call(
        body, name=name,
        out_shape=[pltpu.HBM(v.shape, v.dtype) for v in srcs] + [pltpu.HBM(v.shape, v.dtype) for v in lands],
        in_specs=[HBM] * (2 * n) + [SEM, SEM, ANY], out_specs=[HBM] * (2 * n),
        input_output_aliases={i: i for i in range(2 * n)},
        scratch_shapes=[pltpu.SemaphoreType.DMA((2,))]
        + [pltpu.VMEM(_stage_shape(s, v.dtype), v.dtype) for s, v in zip(own_shapes, srcs)],
        compiler_params=pltpu.CompilerParams(has_side_effects=EFFECT),
    )(*srcs, *lands, send_sems, recv_sems, after)
    return res[n:]


def kernel(x, norm_gains, fox_w_in, fox_b_f, hgrn_w_in, hgrn_lb_logits, hgrn_onorm, w_out, final_gain, loss_target, m_norm_gains, m_fox_w_in, m_fox_b_f, m_hgrn_w_in, m_hgrn_lb_logits, m_hgrn_onorm, m_w_out, m_final_gain, v_norm_gains, v_fox_w_in, v_fox_b_f, v_hgrn_w_in, v_hgrn_lb_logits, v_hgrn_onorm, v_w_out, v_final_gain):
    _, S, D = x.shape
    H = FOX_HEADS
    W = H * HEAD_DIM
    assert HGRN_HEADS == H and w_out.shape[2] == D
    cf = fox_w_in.shape[2]
    ch = hgrn_w_in.shape[2]
    ro = w_out.shape[1]
    co = hgrn_onorm.shape[1]
    assert N_DEV * cf == 4 * W + H and N_DEV * ch == 4 * W and N_DEV * ro == W and N_DEV * co == W
    x2 = x.reshape(S, D)
    tgt = loss_target.reshape(S, D)

    col = lambda n: (lambda r, i: r.at[:, pl.ds(pl.multiple_of(i * n, n), n)])
    row = lambda n: (lambda r, i: r.at[pl.ds(pl.multiple_of(i * n, n), n), :])
    late_views = [col(ch), row(ro), col(co)]
    late = _send_start(
        [hgrn_w_in[0].astype(BF16), w_out[1].astype(BF16), hgrn_onorm],
        [SDS((D, 4 * W), BF16), SDS((W, D), BF16), SDS((1, W), F32)],
        [lambda r, me, p, m: r] * 3, [lambda r, me, m, v=v: v(r, me) for v in late_views],
        name="gather_layer1_start")
    ng0 = norm_gains[0:1] + late[4][0:1, 0:1]

    wf_g, wo0 = _all_gather(
        [fox_w_in[0].astype(BF16), w_out[0].astype(BF16)], [SDS((N_DEV, D, cf), BF16), SDS((W, D), BF16)],
        [lambda r, p: r.at[p], row(ro)], name="gather_layer0")
    wf = jnp.transpose(wf_g, (1, 0, 2)).reshape(D, N_DEV * cf)
    wf_main = jnp.concatenate([wf[:, :3 * W], wf[:, 3 * W + H:]], axis=1)
    wfl_t = wf[:, 3 * W:3 * W + H].T

    h0 = _rms_fwd(x2, ng0, name="rms0_fwd")
    p0 = _mm_nn([h0], wf_main, BF16, scale_cols=(W, LOG2E * HEAD_DIM ** -0.5), name="fox_in_proj")
    fl_t = _mm_nt_rows(wfl_t, h0, name="fox_forget_proj")
    b_col = fox_b_f.reshape(H, 1)
    kaug = _fox_key_aug(*_fox_gate_fwd(fl_t, b_col, name="fox_gate_fwd"))
    o0, y0, qaug = _fox_fwd(p0, kaug, H=H, name="fox_attn_fwd")
    x1 = _mm_nn([y0], wo0, F32, residual=x2, name="fox_out_proj")

    wh, wo1, onorm = _send_wait(
        late, [lambda r, me, p, m: r] * 3, [lambda r, me, m, v=v: v(r, me) for v in late_views],
        [lambda src, land, me, v=v: (src, v(land, me)) for v in late_views], [(D, ch), (ro, D), (1, co)],
        x1[0:8], name="gather_layer1_wait")
    wh_qig = jnp.concatenate([wh[:, :W], wh[:, 2 * W:]], axis=1)
    wh_f = wh[:, W:2 * W]
    lb = _lb_fwd(hgrn_lb_logits, name="hgrn_lower_bound")
    h1 = _rms_fwd(x1, norm_gains[1:2], name="rms1_fwd")
    p1 = _mm_nn([h1], wh_qig, BF16, name="hgrn_in_proj")
    f1 = _mm_nn([h1], wh_f, F32, name="hgrn_forget_proj")
    o1, y1, states = _hgrn_fwd(p1, f1, lb, onorm, H=H, name="hgrn_fwd")
    xo = _mm_nn([y1], wo1, F32, residual=x1, name="hgrn_out_proj")

    dx2, dx2b, loss_part, dgf = _loss_head(xo, final_gain.reshape(1, D), tgt, name="loss_head")
    loss = lax.psum(jnp.sum(loss_part), ("x", "y", "c"))

    dy1 = _mm_nn([dx2b], wo1, BF16, b_t=True, name="hgrn_out_proj_dx")
    dwo1 = _mm_tn(y1, [dx2b], BF16, name="hgrn_out_proj_dw")
    do1, dg1, donorm = _hgrn_post_bwd(dy1, o1, p1, onorm, H=H, name="hgrn_post_bwd")
    dq1, df1, di1, dlb = _hgrn_bwd(p1, f1, lb, do1, states, H=H, name="hgrn_bwd")
    segs1 = [dq1, df1, di1, dg1]
    dh1 = _mm_nn(segs1, wh, BF16, b_t=True, name="hgrn_in_proj_dx")
    dwh = _mm_tn(h1, segs1, BF16, name="hgrn_in_proj_dw")
    part_views = [col(ch), row(ro)]
    slot = lambda r, me, m: r.at[m]
    ex1 = _send_start([dwh, dwo1], [SDS((N_DEV, D, ch), BF16), SDS((N_DEV, ro, D), BF16)],
                      [lambda r, me, p, m, v=v: v(r, p) for v in part_views], [slot] * 2,
                      name="exchange_layer1_start")
    ng1 = norm_gains[1:2] + ex1[4][0:1, 0:1]
    dx1, dx1b, dng1 = _rms_bwd(x1, ng1, dh1, dx2, name="rms1_bwd")

    dy0 = _mm_nn([dx1b], wo0, BF16, b_t=True, name="fox_out_proj_dx")
    dwo0 = _mm_tn(y0, [dx1b], BF16, name="fox_out_proj_dw")
    do0, dg0, doaug = _fox_post_bwd(dy0, o0, p0, H=H, name="fox_post_bwd")
    dq0, dc_row, dk0, dv0, dc_key = _fox_bwd(p0, kaug, qaug, do0, doaug, H=H, name="fox_attn_bwd")
    dfl_t, dbf = _fox_gate_bwd(dc_row.reshape(H, S), dc_key.reshape(H, S), fl_t, b_col, name="fox_gate_bwd")
    dfl_tb = dfl_t.astype(BF16)
    dwfl_t = _mm_nn([dfl_tb], h0, BF16, name="fox_forget_proj_dw")
    segs0 = [dq0, dk0, dv0, dg0]
    dwf_main = _mm_tn(h0, segs0, BF16, name="fox_in_proj_dw")
    dwf = jnp.concatenate([dwf_main[:, :3 * W], dwfl_t.T, dwf_main[:, 3 * W:]], axis=1)
    dwf_blocks = jnp.transpose(dwf.reshape(D, N_DEV, cf), (1, 0, 2))
    ex0 = _send_start([dwf_blocks, dwo0], [SDS((N_DEV, D, cf), BF16), SDS((N_DEV, ro, D), BF16)],
                      [lambda r, me, p, m: r.at[p], lambda r, me, p, m: row(ro)(r, p)], [slot] * 2,
                      name="exchange_layer0_start")
    wfl_t0 = wfl_t + ex0[4][0:1, 0:1].astype(BF16)
    dh0_f = _mm_nn([dfl_tb.T], wfl_t0, BF16, name="fox_forget_proj_dx")
    dh0 = _mm_nn(segs0, wf_main, BF16, residual=dh0_f, b_t=True, name="fox_in_proj_dx")
    grad_x, _, dng0 = _rms_bwd(x2, norm_gains[0:1], dh0, dx1, name="rms0_bwd")

    own1 = [lambda src, land, me, v=v: (v(src, me), land.at[0]) for v in part_views]
    rh, ro1 = _send_wait(ex1, [lambda r, me, p, m, v=v: v(r, p) for v in part_views], [slot] * 2, own1,
                         [(D, ch), (ro, D)], dng0, name="exchange_layer1_wait")
    rf, ro0 = _send_wait(ex0, [lambda r, me, p, m: r.at[p], lambda r, me, p, m: row(ro)(r, p)], [slot] * 2,
                         [lambda src, land, me: (src.at[me], land.at[0]),
                          lambda src, land, me: (row(ro)(src, me), land.at[0])],
                         [(D, cf), (ro, D)], dng0, name="exchange_layer0_wait")

    pad = lambda a: jnp.pad(a, ((0, 0), (0, W - a.shape[1])))
    stats = jnp.concatenate([dng0, dng1, dlb, dgf, pad(dbf.reshape(1, H)), donorm,
                             jnp.zeros((2, W), F32)], axis=0)
    assert D == W
    (stats_all,) = _all_gather([stats], [SDS((N_DEV, STAT_ROWS, W), F32)], [lambda r, p: r.at[p]],
                               name="gather_small_grads")
    g_small = _stats_reduce(stats_all, hgrn_lb_logits, name="reduce_small_grads")
    me = 4 * lax.axis_index("x") + 2 * lax.axis_index("y") + lax.axis_index("c")
    g_onorm = lax.dynamic_slice_in_dim(g_small[6:7], me * co, co, axis=1)

    def upd(w, m, v, parts, name):
        shp = w.shape
        r2 = (-1, shp[-1])
        g, d, mn, vn = _adamw(w.reshape(r2), m.reshape(r2), v.reshape(r2), parts, name=name)
        return g.reshape(shp), d.reshape(shp), mn.reshape(shp), vn.reshape(shp)

    res = {
        "norm_gains": upd(norm_gains, m_norm_gains, v_norm_gains, [g_small[None, 0:2]], "adamw_norm_gains"),
        "fox_w_in": upd(fox_w_in, m_fox_w_in, v_fox_w_in, [rf], "adamw_fox_w_in"),
        "fox_b_f": upd(fox_b_f, m_fox_b_f, v_fox_b_f, [g_small[None, 5:6, :H]], "adamw_fox_b_f"),
        "hgrn_w_in": upd(hgrn_w_in, m_hgrn_w_in, v_hgrn_w_in, [rh], "adamw_hgrn_w_in"),
        "hgrn_lb_logits": upd(hgrn_lb_logits, m_hgrn_lb_logits, v_hgrn_lb_logits, [g_small[None, 2:4]],
                              "adamw_hgrn_lb_logits"),
        "hgrn_onorm": upd(hgrn_onorm, m_hgrn_onorm, v_hgrn_onorm, [g_onorm[None]], "adamw_hgrn_onorm"),
        "w_out": upd(w_out, m_w_out, v_w_out, [ro0, ro1], "adamw_w_out"),
        "final_gain": upd(final_gain.reshape(1, D), m_final_gain.reshape(1, D), v_final_gain.reshape(1, D),
                          [g_small[None, 4:5]], "adamw_final_gain"),
    }
    order = ["norm_gains", "fox_w_in", "fox_b_f", "hgrn_w_in", "hgrn_lb_logits", "hgrn_onorm", "w_out", "final_gain"]
    fix = lambda n, a: a.reshape(D) if n == "final_gain" else a
    outs = [loss, grad_x.reshape(1, S, D)]
    for k in range(4):
        outs += [fix(n, res[n][k]) for n in order]
    return tuple(outs)
```

```python
import functools

import numpy as np
import jax
import jax.numpy as jnp
from jax import lax
from jax.experimental import pallas as pl
from jax.experimental.pallas import tpu as pltpu

F32 = jnp.float32
BF16 = jnp.bfloat16
SDS = jax.ShapeDtypeStruct
MESH = pl.DeviceIdType.MESH

EPS = 1e-6
ADAM_LR, ADAM_B1, ADAM_B2, ADAM_EPS, ADAM_WD, ADAM_STEP = 0.001, 0.9, 0.999, 1e-08, 0.01, 10

N_DEV = 8
FOX_HEADS = 16
HGRN_HEADS = 16
HEAD_DIM = 128
HGRN_CHUNK = 128
HGRN_LEAF = 16
HGRN_HEADS_PER_STEP = 4
EXP_CLAMP = 85.0
ATT_BLOCK = 512
ATT_HEADS_PER_STEP = 4
ATT_BWD_HEADS_PER_STEP = 2
POST_HEADS_PER_STEP = 4
NEG = -1e30
LOG2E = 1.4426950408889634
LN2 = 0.6931471805599453

VMEM_LIMIT_V7X = 56 * 1024 * 1024


def _params(*sem):
    return pltpu.CompilerParams(dimension_semantics=sem, vmem_limit_bytes=VMEM_LIMIT_V7X)


def _silu(x):
    return x * jax.nn.sigmoid(x)


def _dsilu(x):
    s = jax.nn.sigmoid(x)
    return s * (1.0 + x * (1.0 - s))


def _dot(a, b):
    return jnp.dot(a, b, preferred_element_type=F32)


def _dot_nt(a, b):
    return lax.dot_general(a, b, (((1,), (1,)), ((), ())), preferred_element_type=F32)


def _dot_tn(a, b):
    return lax.dot_general(a, b, (((0,), (0,)), ((), ())), preferred_element_type=F32)


def _mm_nn(a_list, b, out_dtype, *, name, residual=None, scale_cols=None, b_t=False, tm=1024, tn=1024, tk=2048):
    ns = len(a_list)
    M, Ks = a_list[0].shape
    K, N = (b.shape[1], b.shape[0]) if b_t else b.shape
    dot = _dot_nt if b_t else _dot
    assert K == ns * Ks and all(a.shape == (M, Ks) for a in a_list)
    if ns > 1:
        tk = tk // 2
    tm, tn, tk = min(tm, M), min(tn, N), min(tk, Ks)
    assert M % tm == 0 and N % tn == 0 and Ks % tk == 0
    assert scale_cols is None or scale_cols[0] % tn == 0
    nks = Ks // tk
    nk = ns * nks
    has_res = residual is not None

    def body(*refs):
        a_refs, b_ref = refs[:ns], refs[ns]
        res_ref = refs[ns + 1] if has_res else None
        o_ref = refs[ns + 1 + has_res]

        def finish(r):
            if has_res:
                r = r + res_ref[...].astype(F32)
            if scale_cols is not None:
                r = r * jnp.where(pl.program_id(1) < scale_cols[0] // tn, scale_cols[1], 1.0)
            o_ref[...] = r.astype(out_dtype)

        if nk == 1:
            finish(dot(a_refs[0][...], b_ref[...]))
            return
        acc_ref = refs[ns + 2 + has_res]
        k = pl.program_id(2)

        @pl.when(k == 0)
        def _():
            acc_ref[...] = jnp.zeros_like(acc_ref)

        for s in range(ns):
            def step(s=s):
                acc_ref[...] += dot(a_refs[s][...], b_ref[...])

            if ns == 1:
                step()
            else:
                pl.when(k // nks == s)(step)

        @pl.when(k == nk - 1)
        def _():
            finish(acc_ref[...])

    def a_map(i, j, k, s):
        return (i, jnp.clip(k - s * nks, 0, nks - 1))

    in_specs = [pl.BlockSpec((tm, tk), functools.partial(a_map, s=s)) for s in range(ns)]
    if b_t:
        in_specs.append(pl.BlockSpec((tn, tk), lambda i, j, k: (j, k)))
    else:
        in_specs.append(pl.BlockSpec((tk, tn), lambda i, j, k: (k, j)))
    args = list(a_list) + [b]
    if has_res:
        in_specs.append(pl.BlockSpec((tm, tn), lambda i, j, k: (i, j)))
        args.append(residual)
    return pl.pallas_call(
        body, grid=(M // tm, N // tn, nk), in_specs=in_specs,
        out_specs=pl.BlockSpec((tm, tn), lambda i, j, k: (i, j)),
        out_shape=SDS((M, N), out_dtype),
        scratch_shapes=[] if nk == 1 else [pltpu.VMEM((tm, tn), F32)],
        compiler_params=_params("parallel", "parallel", "arbitrary"), name=name,
    )(*args)


def _mm_tn(a, b_list, out_dtype, *, name, tm=2048, tn=1024, tk=512):
    ns = len(b_list)
    S, M = a.shape
    Ns = b_list[0].shape[1]
    assert all(b.shape == (S, Ns) for b in b_list)
    tm, tn, tk = min(tm, M), min(tn, Ns), min(tk, S)
    assert M % tm == 0 and Ns % tn == 0 and S % tk == 0
    njs = Ns // tn
    nk = S // tk

    def body(*refs):
        a_ref, b_refs, o_ref, acc_ref = refs[0], refs[1:1 + ns], refs[1 + ns], refs[2 + ns]
        j, k = pl.program_id(1), pl.program_id(2)

        @pl.when(k == 0)
        def _():
            acc_ref[...] = jnp.zeros_like(acc_ref)

        for s in range(ns):
            def step(s=s):
                acc_ref[...] += _dot_tn(a_ref[...], b_refs[s][...])

            if ns == 1:
                step()
            else:
                pl.when(j // njs == s)(step)

        @pl.when(k == nk - 1)
        def _():
            o_ref[...] = acc_ref[...].astype(out_dtype)

    def b_map(i, j, k, s):
        return (k, jnp.clip(j - s * njs, 0, njs - 1))

    in_specs = [pl.BlockSpec((tk, tm), lambda i, j, k: (k, i))]
    in_specs += [pl.BlockSpec((tk, tn), functools.partial(b_map, s=s)) for s in range(ns)]
    return pl.pallas_call(
        body, grid=(M // tm, ns * njs, nk), in_specs=in_specs,
        out_specs=pl.BlockSpec((tm, tn), lambda i, j, k: (i, j)),
        out_shape=SDS((M, ns * Ns), out_dtype),
        scratch_shapes=[pltpu.VMEM((tm, tn), F32)],
        compiler_params=_params("parallel", "parallel", "arbitrary"), name=name,
    )(a, *b_list)


def _mm_nt_rows(w_t, h, *, name, tn=1024):
    R, K = w_t.shape
    S = h.shape[0]
    tn = min(tn, S)

    def body(w_ref, h_ref, o_ref):
        o_ref[...] = _dot_nt(w_ref[...], h_ref[...])

    return pl.pallas_call(
        body, grid=(S // tn,),
        in_specs=[pl.BlockSpec((R, K), lambda i: (0, 0)), pl.BlockSpec((tn, K), lambda i: (i, 0))],
        out_specs=pl.BlockSpec((R, tn), lambda i: (0, i)),
        out_shape=SDS((R, S), F32), compiler_params=_params("parallel"), name=name,
    )(w_t, h)


def _rms_fwd(x, gain, *, name, tm=512):
    S, D = x.shape
    tm = min(tm, S)

    def body(x_ref, g_ref, h_ref):
        xv = x_ref[...]
        r = lax.rsqrt(jnp.mean(xv * xv, axis=-1, keepdims=True) + EPS)
        h_ref[...] = ((xv * r) * g_ref[...]).astype(BF16)

    return pl.pallas_call(
        body, grid=(S // tm,),
        in_specs=[pl.BlockSpec((tm, D), lambda i: (i, 0)), pl.BlockSpec((1, D), lambda i: (0, 0))],
        out_specs=pl.BlockSpec((tm, D), lambda i: (i, 0)),
        out_shape=SDS((S, D), BF16), compiler_params=_params("parallel"), name=name,
    )(x, gain)


def _rms_bwd(x, gain, dh, dres, *, name, tm=256):
    S, D = x.shape
    tm = min(tm, S)

    def body(x_ref, g_ref, dh_ref, dres_ref, dx_ref, dxb_ref, dg_ref):
        @pl.when(pl.program_id(0) == 0)
        def _():
            dg_ref[...] = jnp.zeros_like(dg_ref)

        xv = x_ref[...]
        r = lax.rsqrt(jnp.mean(xv * xv, axis=-1, keepdims=True) + EPS)
        xh = xv * r
        dhv = dh_ref[...].astype(F32)
        dg_ref[...] += jnp.sum(dhv * xh, axis=0, keepdims=True)
        dxh = dhv * g_ref[...]
        dx = r * (dxh - xh * jnp.mean(dxh * xh, axis=-1, keepdims=True)) + dres_ref[...]
        dx_ref[...] = dx
        dxb_ref[...] = dx.astype(BF16)

    row = pl.BlockSpec((tm, D), lambda i: (i, 0))
    vec = pl.BlockSpec((1, D), lambda i: (0, 0))
    return pl.pallas_call(
        body, grid=(S // tm,), in_specs=[row, vec, row, row], out_specs=[row, row, vec],
        out_shape=[SDS((S, D), F32), SDS((S, D), BF16), SDS((1, D), F32)],
        compiler_params=_params("arbitrary"), name=name,
    )(x, gain, dh, dres)


def _loss_head(x, gain, target, *, name, tm=256):
    S, D = x.shape
    tm = min(tm, S)
    assert tm % 8 == 0 and D % 128 == 0

    def body(x_ref, g_ref, t_ref, dx_ref, dxb_ref, loss_ref, dg_ref):
        @pl.when(pl.program_id(0) == 0)
        def _():
            dg_ref[...] = jnp.zeros_like(dg_ref)
            loss_ref[...] = jnp.zeros_like(loss_ref)

        xv = x_ref[...]
        g = g_ref[...]
        r = lax.rsqrt(jnp.mean(xv * xv, axis=-1, keepdims=True) + EPS)
        xh = xv * r
        err = xh * g - t_ref[...]
        e2 = (err * err).reshape(tm // 8, 8, D).sum(axis=0)
        part = e2[:, 0:128]
        for k in range(1, D // 128):
            part = part + e2[:, k * 128:(k + 1) * 128]
        loss_ref[...] += part * (0.5 / D)
        dy = err * (1.0 / D)
        dg_ref[...] += jnp.sum(dy * xh, axis=0, keepdims=True)
        dxh = dy * g
        dx = r * (dxh - xh * jnp.mean(dxh * xh, axis=-1, keepdims=True))
        dx_ref[...] = dx
        dxb_ref[...] = dx.astype(BF16)

    row = pl.BlockSpec((tm, D), lambda i: (i, 0))
    vec = pl.BlockSpec((1, D), lambda i: (0, 0))
    return pl.pallas_call(
        body, grid=(S // tm,), in_specs=[row, vec, row],
        out_specs=[row, row, pl.BlockSpec((8, 128), lambda i: (0, 0)), vec],
        out_shape=[SDS((S, D), F32), SDS((S, D), BF16), SDS((8, 128), F32), SDS((1, D), F32)],
        compiler_params=_params("arbitrary"), name=name,
    )(x, gain, target)


def _split3(x):
    hi = x.astype(BF16)
    r1 = x - hi.astype(F32)
    mid = r1.astype(BF16)
    lo = (r1 - mid.astype(F32)).astype(BF16)
    return hi, mid, lo


def _split2(x):
    hi = x.astype(BF16)
    lo = (x - hi.astype(F32)).astype(BF16)
    return hi, lo


def _fox_gate_fwd(fl_t, b_col, *, name):
    H, S = fl_t.shape
    L = 128
    tri = jnp.asarray(np.triu(np.ones((L, L), np.float32)), BF16)

    def body(fl_ref, b_ref, tri_ref, hi_ref, mid_ref, lo_ref, carry):
        @pl.when(pl.program_id(0) == 0)
        def _():
            carry[...] = jnp.zeros_like(carry)

        z = fl_ref[...] + b_ref[...]
        lf = jnp.minimum(z, 0.0) - jnp.log(1.0 + jnp.exp(-jnp.abs(z)))
        hi, mid, lo = _split3(lf)
        t = tri_ref[...]
        c = (_dot(hi, t) + _dot(mid, t)) + _dot(lo, t) + carry[...]
        carry[...] = c[:, L - 1:L]
        hi_ref[...], mid_ref[...], lo_ref[...] = _split3(c * (-LOG2E))

    blk = pl.BlockSpec((H, L), lambda i: (0, i))
    return pl.pallas_call(
        body, grid=(S // L,),
        in_specs=[blk, pl.BlockSpec((H, 1), lambda i: (0, 0)), pl.BlockSpec((L, L), lambda i: (0, 0))],
        out_specs=[blk] * 3, out_shape=[SDS((H, S), BF16)] * 3, scratch_shapes=[pltpu.VMEM((H, 1), F32)],
        compiler_params=_params("arbitrary"), name=name,
    )(fl_t, b_col, tri)


def _fox_gate_bwd(dc_row, dc_key, fl_t, b_col, *, name):
    H, S = fl_t.shape
    L = 128
    n = S // L
    tri = jnp.asarray(np.tril(np.ones((L, L), np.float32)), BF16)

    def body(dcr_ref, dck_ref, fl_ref, b_ref, tri_ref, dfl_ref, db_ref, carry):
        @pl.when(pl.program_id(0) == 0)
        def _():
            carry[...] = jnp.zeros_like(carry)
            db_ref[...] = jnp.zeros_like(db_ref)

        hi, mid, lo = _split3(dcr_ref[...] + dck_ref[...])
        t = tri_ref[...]
        dlf = (_dot(hi, t) + _dot(mid, t)) + _dot(lo, t) + carry[...]
        carry[...] = dlf[:, 0:1]
        z = fl_ref[...] + b_ref[...]
        dfl = dlf * jax.nn.sigmoid(-z)
        dfl_ref[...] = dfl
        db_ref[...] += jnp.sum(dfl, axis=1, keepdims=True)

    blk = pl.BlockSpec((H, L), lambda i: (0, n - 1 - i))
    col = pl.BlockSpec((H, 1), lambda i: (0, 0))
    return pl.pallas_call(
        body, grid=(n,), in_specs=[blk, blk, blk, col, pl.BlockSpec((L, L), lambda i: (0, 0))],
        out_specs=[blk, col], out_shape=[SDS((H, S), F32), SDS((H, 1), F32)],
        scratch_shapes=[pltpu.VMEM((H, 1), F32)], compiler_params=_params("arbitrary"), name=name,
    )(dc_row, dc_key, fl_t, b_col, tri)


AUG = HEAD_DIM


def _lane_select(cols, shape):
    lane = lax.broadcasted_iota(jnp.int32, shape, 1)
    out = jnp.zeros(shape, BF16)
    for k, c in reversed(list(enumerate(cols))):
        c = jnp.full(shape, c, BF16) if isinstance(c, (int, float)) else jnp.broadcast_to(c, shape).astype(BF16)
        out = jnp.where(lane == k, c, out)
    return out


def _fox_key_aug(b_hi, b_mid, b_lo):
    H, S = b_hi.shape
    ones = jnp.ones((H, S), BF16)
    ka = jnp.stack([b_hi, b_mid, b_lo, ones, ones, ones], axis=-1)
    ka = jnp.pad(ka, ((0, 0), (0, 0), (0, AUG - 6)))
    return jnp.transpose(ka, (1, 0, 2)).reshape(S, H * AUG)


def _fox_fwd(p0, kaug, *, H, name):
    S = p0.shape[0]
    T = min(ATT_BLOCK, S)
    nq = S // T
    dh = HEAD_DIM
    G = ATT_HEADS_PER_STEP
    assert H % G == 0

    def body(q_ref, k_ref, ka_ref, v_ref, g_ref, o_ref, y_ref, qa_ref, m_sc, acc_sc, p_sc, al_sc):
        i = pl.program_id(1)
        qaug = _lane_select([1.0, 1.0, 1.0], (T, AUG))
        ones = jnp.ones((T, dh), BF16)
        m_sc[...] = jnp.full_like(m_sc, NEG)
        acc_sc[...] = jnp.zeros_like(acc_sc)
        p_sc[...] = jnp.zeros_like(p_sc)
        al_sc[...] = jnp.ones_like(al_sc)

        def step(j, masked):
            rows = pl.ds(pl.multiple_of(j * T, T), T)
            prev = pl.ds(pl.multiple_of(jnp.maximum(j - 1, 0) * T, T), T)
            for g in range(G):
                hd = slice(g * dh, (g + 1) * dh)
                vp = jnp.concatenate([v_ref[prev, hd], ones], axis=1)
                acc_sc[g] = jnp.tile(al_sc[g], (1, 2)) * acc_sc[g] + _dot(p_sc[g], vp)
                q = jnp.concatenate([q_ref[:, hd], qaug], axis=1)
                kj = jnp.concatenate([k_ref[rows, hd], ka_ref[rows, hd]], axis=1)
                t = _dot_nt(q, kj)
                if masked:
                    row = lax.broadcasted_iota(jnp.int32, (T, T), 0)
                    col = lax.broadcasted_iota(jnp.int32, (T, T), 1)
                    t = jnp.where(row >= col, t, NEG)
                m_prev = m_sc[g]
                m_new = jnp.maximum(m_prev, jnp.max(t, axis=-1, keepdims=True))
                p_sc[g] = jnp.exp2(t - jnp.tile(m_new, (1, T // 128))).astype(BF16)
                al_sc[g] = jnp.exp2(m_prev - m_new)
                m_sc[g] = m_new

        def loop_body(j, carry):
            step(j, False)
            return carry

        lax.fori_loop(0, i, loop_body, 0)
        step(i, True)
        rows = pl.ds(pl.multiple_of(i * T, T), T)
        for g in range(G):
            hd = slice(g * dh, (g + 1) * dh)
            vp = jnp.concatenate([v_ref[rows, hd], ones], axis=1)
            acc = jnp.tile(al_sc[g], (1, 2)) * acc_sc[g] + _dot(p_sc[g], vp)
            l = acc[:, dh:]
            o = acc[:, :dh] / l
            o_ref[:, hd] = o
            y_ref[:, hd] = (o * _silu(g_ref[:, hd].astype(F32))).astype(BF16)
            hi, mid, lo = _split3(-(m_sc[g] + jnp.log2(l)))
            qa_ref[:, hd] = _lane_select([1.0, 1.0, 1.0, hi, mid, lo], (T, AUG))

    blk = lambda off: pl.BlockSpec((T, G * dh), lambda h, i: (i, off // G + h))
    full = lambda off: pl.BlockSpec((S, G * dh), lambda h, i: (0, off // G + h), pipeline_mode=pl.Buffered(1))
    return pl.pallas_call(
        body, grid=(H // G, nq),
        in_specs=[blk(0), full(H), full(0), full(2 * H), blk(3 * H)],
        out_specs=[blk(0), blk(0), blk(0)],
        out_shape=[SDS((S, H * dh), F32), SDS((S, H * dh), BF16), SDS((S, H * AUG), BF16)],
        scratch_shapes=[pltpu.VMEM((G, T, 128), F32), pltpu.VMEM((G, T, 2 * dh), F32),
                        pltpu.VMEM((G, T, T), BF16), pltpu.VMEM((G, T, 128), F32)],
        compiler_params=_params("parallel", "arbitrary"), name=name,
    )(p0, p0, kaug, p0, p0)


def _fox_post_bwd(dy, o, p0, *, H, name, tm=512):
    S = dy.shape[0]
    dh = HEAD_DIM
    tm = min(tm, S)
    G = POST_HEADS_PER_STEP
    assert H % G == 0

    def body(dy_ref, o_ref, g_ref, do_ref, dg_ref, da_ref):
        dyv = dy_ref[...].astype(F32)
        ov = o_ref[...]
        g = g_ref[...].astype(F32)
        do = (dyv * _silu(g)).astype(BF16)
        do_ref[...] = do
        dg_ref[...] = (dyv * ov * _dsilu(g)).astype(BF16)
        prod = do.astype(F32) * ov
        for k in range(G):
            hd = slice(k * dh, (k + 1) * dh)
            delta = jnp.sum(prod[:, hd], axis=-1, keepdims=True)
            hi, mid, lo = _split3(-jnp.broadcast_to(delta, (tm, AUG)))
            da_ref[:, hd] = _lane_select([hi, mid, lo], (tm, AUG))

    blk = pl.BlockSpec((tm, G * dh), lambda h, i: (i, h))
    return pl.pallas_call(
        body, grid=(H // G, S // tm),
        in_specs=[blk, blk, pl.BlockSpec((tm, G * dh), lambda h, i: (i, 3 * H // G + h))],
        out_specs=[blk, blk, blk],
        out_shape=[SDS((S, H * dh), BF16), SDS((S, H * dh), BF16), SDS((S, H * AUG), BF16)],
        compiler_params=_params("parallel", "parallel"), name=name,
    )(dy, o, p0)


def _fox_bwd(p0, kaug, qaug, do, doaug, *, H, name):
    S = p0.shape[0]
    T = min(ATT_BLOCK, S)
    nq = S // T
    dh = HEAD_DIM
    scale = dh ** -0.5
    G = ATT_BWD_HEADS_PER_STEP
    assert H % G == 0

    def body(q_ref, qa_ref, k_ref, ka_ref, v_ref, do_ref, da_ref, dq_ref, rs_ref, dk_ref, dv_ref, dc_ref,
             dq_sc, dk_sc, dv_sc, pt_sc, dst_sc):
        j = pl.program_id(1)
        vaug = _lane_select([1.0, 1.0, 1.0], (T, AUG))
        ones = jnp.ones((T, dh), BF16)

        @pl.when(j == 0)
        def _():
            dq_sc[...] = jnp.zeros_like(dq_sc)

        dk_sc[...] = jnp.zeros_like(dk_sc)
        dv_sc[...] = jnp.zeros_like(dv_sc)
        pt_sc[...] = jnp.zeros_like(pt_sc)
        dst_sc[...] = jnp.zeros_like(dst_sc)

        def apply(prev):
            for g in range(G):
                hd = slice(g * dh, (g + 1) * dh)
                dv_sc[g] += _dot(pt_sc[g], do_ref[prev, hd])
                dk_sc[g] += _dot(dst_sc[g], jnp.concatenate([q_ref[prev, hd], ones], axis=1))
                dq_sc[g, prev] += _dot_tn(dst_sc[g], jnp.concatenate([k_ref[:, hd], ones], axis=1))

        def step(i, masked):
            rows = pl.ds(pl.multiple_of(i * T, T), T)
            apply(pl.ds(pl.multiple_of(jnp.maximum(i - 1, j) * T, T), T))
            for g in range(G):
                hd = slice(g * dh, (g + 1) * dh)
                k = jnp.concatenate([k_ref[:, hd], ka_ref[:, hd]], axis=1)
                v = jnp.concatenate([v_ref[:, hd], vaug], axis=1)
                pt = jnp.exp2(_dot_nt(k, jnp.concatenate([q_ref[rows, hd], qa_ref[rows, hd]], axis=1)))
                if masked:
                    row = lax.broadcasted_iota(jnp.int32, (T, T), 0)
                    col = lax.broadcasted_iota(jnp.int32, (T, T), 1)
                    pt = jnp.where(col >= row, pt, 0.0)
                dst = pt * _dot_nt(v, jnp.concatenate([do_ref[rows, hd], da_ref[rows, hd]], axis=1))
                pt_sc[g] = pt.astype(BF16)
                dst_sc[g] = dst.astype(BF16)

        step(j, True)

        def loop_body(i, carry):
            step(i, False)
            return carry

        lax.fori_loop(j + 1, nq, loop_body, 0)
        apply(pl.ds((nq - 1) * T, T))
        for g in range(G):
            hd = slice(g * dh, (g + 1) * dh)
            dk_ref[:, hd] = (dk_sc[g, :, :dh] * LN2).astype(BF16)
            dv_ref[:, hd] = dv_sc[g].astype(BF16)
            dc_ref[g] = -jnp.transpose(dk_sc[g, :, dh:])[0:1]

        @pl.when(j == nq - 1)
        def _():
            for g in range(G):
                dq_ref[:, g * dh:(g + 1) * dh] = (dq_sc[g, :, :dh] * scale).astype(BF16)
                for i in range(nq):
                    rs_ref[g, :, i * T:(i + 1) * T] = jnp.transpose(dq_sc[g, i * T:(i + 1) * T, dh:])[0:1]

    blk = lambda off: pl.BlockSpec((T, G * dh), lambda h, j: (j, off // G + h))
    full = lambda off: pl.BlockSpec((S, G * dh), lambda h, j: (0, off // G + h))
    once = lambda off: pl.BlockSpec((S, G * dh), lambda h, j: (0, off // G + h), pipeline_mode=pl.Buffered(1))
    rowv = pl.BlockSpec((G, 1, T), lambda h, j: (h, 0, j))
    return pl.pallas_call(
        body, grid=(H // G, nq),
        in_specs=[once(0), once(0), blk(H), blk(0), blk(2 * H), once(0), once(0)],
        out_specs=[full(0), pl.BlockSpec((G, 1, S), lambda h, j: (h, 0, 0)), blk(0), blk(0), rowv],
        out_shape=[SDS((S, H * dh), BF16), SDS((H, 1, S), F32), SDS((S, H * dh), BF16), SDS((S, H * dh), BF16),
                   SDS((H, 1, S), F32)],
        scratch_shapes=[pltpu.VMEM((G, S, 2 * dh), F32), pltpu.VMEM((G, T, 2 * dh), F32), pltpu.VMEM((G, T, dh), F32),
                        pltpu.VMEM((G, T, T), BF16), pltpu.VMEM((G, T, T), BF16)],
        compiler_params=_params("parallel", "arbitrary"), name=name,
    )(p0, qaug, p0, kaug, p0, do, doaug)


def _hgrn_levels(C, leaf):
    levels = []
    h = C // 2
    while h >= leaf:
        levels.append(h)
        h //= 2
    return levels


def _hgrn_sum_matrix(C, leaf):
    t = np.arange(C)[:, None]
    u = np.arange(C)[None, :]
    mats = [(u <= t), (u > t)]
    for h in _hgrn_levels(C, leaf):
        start = (t // (2 * h)) * (2 * h)
        mid = start + h - 1
        second = t > mid
        m = np.where(second, (u > mid) & (u <= t), (u > t) & (u <= mid))
        mats.append(m)
    lstart = (t // leaf) * leaf
    mats.append((u >= lstart) & (u <= t))
    return np.concatenate([m.astype(np.float32) for m in mats], axis=0)


def _hgrn_chunk_terms(qr, fz, lb, msum, C, leaf):
    levels = _hgrn_levels(C, leaf)
    sq = _silu(qr)
    t = jnp.exp(-jnp.abs(fz))
    r = 1.0 / (1.0 + t)
    pos = fz >= 0.0
    sp = jnp.where(pos, r, t * r)
    sn = jnp.where(pos, t * r, r)
    f = lb + (1.0 - lb) * sp
    lf = jnp.log(f)
    k = (1.0 - lb) * sn
    hi, lo = _split2(lf)
    dsum = _dot(msum, hi) + _dot(msum, lo)
    b = dsum[0:C]
    kdec = dsum[C:2 * C]
    rowi = lax.broadcasted_iota(jnp.int32, (C, 1), 0)
    lev = []
    for n, h in enumerate(levels):
        e = jnp.exp(dsum[(2 + n) * C:(3 + n) * C])
        second = (rowi % (2 * h)) >= h
        qm = jnp.where(second, sq * e, 0.0).astype(BF16)
        km = jnp.where(second, 0.0, k * e).astype(BF16)
        lev.append((h, e, second, qm, km))
    dleaf = dsum[(2 + len(levels)) * C:(3 + len(levels)) * C]
    eq = jnp.exp(dleaf)
    ek = jnp.exp(jnp.minimum(-dleaf, EXP_CLAMP))
    return dict(sq=sq, sp=sp, sn=sn, f=f, k=k, b=b, kdec=kdec, lev=lev, eq=eq, ek=ek,
                ql=(sq * eq).astype(BF16), kl=(k * ek).astype(BF16),
                qs=(sq * jnp.exp(b)).astype(BF16), ke=(k * jnp.exp(kdec)).astype(BF16),
                e_c=jnp.exp(b[C - 1:C, :]))


def _hgrn_masks(C, leaf, transposed):
    a = lax.broadcasted_iota(jnp.int32, (C, C), 0)
    bb = lax.broadcasted_iota(jnp.int32, (C, C), 1)
    t, s = (bb, a) if transposed else (a, bb)
    lev = [None if 2 * h == C else (t // (2 * h)) == (s // (2 * h)) for h in _hgrn_levels(C, leaf)]
    if leaf == C:
        leafm = s <= t
    else:
        leafm = ((t // leaf) == (s // leaf)) & (s <= t)
    return lev, leafm


def _hgrn_fwd(p1, f1, lb, onorm, *, H, name, tb=512):
    S = p1.shape[0]
    dk = HEAD_DIM
    C = min(HGRN_CHUNK, S)
    leaf = min(HGRN_LEAF, C)
    tb = min(tb, S)
    nc = tb // C
    G = HGRN_HEADS_PER_STEP
    assert H % G == 0
    msum = jnp.asarray(_hgrn_sum_matrix(C, leaf), BF16)

    def body(q_ref, f_ref, v_ref, g_ref, lb_ref, on_ref, ms_ref, o_ref, y_ref, st_ref, st_sc):
        @pl.when(pl.program_id(1) == 0)
        def _():
            st_sc[...] = jnp.zeros_like(st_sc)

        msv = ms_ref[...]
        lmask, leafm = _hgrn_masks(C, leaf, False)

        def chunk(n, carry):
            rows = pl.ds(pl.multiple_of(n * C, C), C)
            for g in range(G):
                hd = slice(g * dk, (g + 1) * dk)
                tm = _hgrn_chunk_terms(q_ref[rows, hd].astype(F32), f_ref[rows, hd], lb_ref[:, hd], msv, C, leaf)
                v = v_ref[rows, hd]
                st = st_sc[g]
                st_ref[g, n] = st
                a = jnp.where(leafm, _dot_nt(tm["ql"], tm["kl"]), 0.0)
                for (h, e, second, qm, km), m in zip(tm["lev"], lmask):
                    al = _dot_nt(qm, km)
                    a = a + (al if m is None else jnp.where(m, al, 0.0))
                o = _dot_nt(tm["qs"], st.astype(BF16)) + _dot(a.astype(BF16), v)
                st_sc[g] = st * tm["e_c"] + _dot(v.T, tm["ke"])
                o_ref[rows, hd] = o
                rn = lax.rsqrt(jnp.mean(o * o, axis=-1, keepdims=True) + EPS)
                y = ((o * rn) * on_ref[:, hd]) * _silu(g_ref[rows, hd].astype(F32))
                y_ref[rows, hd] = y.astype(BF16)
            return carry

        lax.fori_loop(0, nc, chunk, 0)

    blk = lambda off: pl.BlockSpec((tb, G * dk), lambda h, i: (i, off // G + h))
    vec = pl.BlockSpec((1, G * dk), lambda h, i: (0, h))
    return pl.pallas_call(
        body, grid=(H // G, S // tb),
        in_specs=[blk(0), blk(0), blk(H), blk(2 * H), vec, vec,
                  pl.BlockSpec(msum.shape, lambda h, i: (0, 0))],
        out_specs=[blk(0), blk(0), pl.BlockSpec((G, nc, dk, dk), lambda h, i: (h, i, 0, 0))],
        out_shape=[SDS((S, H * dk), F32), SDS((S, H * dk), BF16), SDS((H, S // C, dk, dk), F32)],
        scratch_shapes=[pltpu.VMEM((G, dk, dk), F32)],
        compiler_params=_params("parallel", "arbitrary"), name=name,
    )(p1, f1, p1, p1, lb, onorm, msum)


def _hgrn_post_bwd(dy, o, p1, onorm, *, H, name, tm=512):
    S = dy.shape[0]
    dk = HEAD_DIM
    tm = min(tm, S)
    G = POST_HEADS_PER_STEP
    assert H % G == 0

    def body(dy_ref, o_ref, g_ref, on_ref, do_ref, dg_ref, don_ref):
        @pl.when(pl.program_id(1) == 0)
        def _():
            don_ref[...] = jnp.zeros_like(don_ref)

        for k in range(G):
            hd = slice(k * dk, (k + 1) * dk)
            dyv = dy_ref[:, hd].astype(F32)
            ov = o_ref[:, hd]
            g = g_ref[:, hd].astype(F32)
            onv = on_ref[:, hd]
            rn = lax.rsqrt(jnp.mean(ov * ov, axis=-1, keepdims=True) + EPS)
            oh = ov * rn
            dn = dyv * _silu(g)
            dg_ref[:, hd] = (dyv * (oh * onv) * _dsilu(g)).astype(BF16)
            don_ref[:, hd] += jnp.sum(dn * oh, axis=0, keepdims=True)
            doh = dn * onv
            do_ref[:, hd] = (rn * (doh - oh * jnp.mean(doh * oh, axis=-1, keepdims=True))).astype(BF16)

    blk = pl.BlockSpec((tm, G * dk), lambda h, i: (i, h))
    vec = pl.BlockSpec((1, G * dk), lambda h, i: (0, h))
    return pl.pallas_call(
        body, grid=(H // G, S // tm),
        in_specs=[blk, blk, pl.BlockSpec((tm, G * dk), lambda h, i: (i, 2 * H // G + h)), vec],
        out_specs=[blk, blk, vec],
        out_shape=[SDS((S, H * dk), BF16), SDS((S, H * dk), BF16), SDS((1, H * dk), F32)],
        compiler_params=_params("parallel", "arbitrary"), name=name,
    )(dy, o, p1, onorm)


def _hgrn_bwd(p1, f1, lb, do, states, *, H, name, tb=512):
    S = p1.shape[0]
    dk = HEAD_DIM
    C = min(HGRN_CHUNK, S)
    leaf = min(HGRN_LEAF, C)
    tb = min(tb, S)
    nc = tb // C
    nb = S // tb
    G = HGRN_HEADS_PER_STEP
    assert H % G == 0
    msum = jnp.asarray(_hgrn_sum_matrix(C, leaf), BF16)
    rtri = jnp.asarray(np.triu(np.ones((C, C), np.float32)), BF16)

    def body(q_ref, f_ref, v_ref, do_ref, st_ref, lb_ref, ms_ref, rt_ref,
             dq_ref, df_ref, dv_ref, dlb_ref, g_sc):
        @pl.when(pl.program_id(1) == 0)
        def _():
            g_sc[...] = jnp.zeros_like(g_sc)
            dlb_ref[...] = jnp.zeros_like(dlb_ref)

        msv = ms_ref[...]
        rtv = rt_ref[...]
        lmask, leafm = _hgrn_masks(C, leaf, False)
        lmask_t, leafm_t = _hgrn_masks(C, leaf, True)
        f32 = lambda z: z.astype(F32)

        def head_chunk(g, n):
            hd = slice(g * dk, (g + 1) * dk)
            rows = pl.ds(pl.multiple_of(n * C, C), C)
            lbv = lb_ref[:, hd]
            qr = q_ref[rows, hd].astype(F32)
            tm = _hgrn_chunk_terms(qr, f_ref[rows, hd], lbv, msv, C, leaf)
            v = v_ref[rows, hd]
            dov = do_ref[rows, hd]
            st0 = st_ref[g, n]
            gt = g_sc[g]
            gtb = gt.astype(BF16)
            da = _dot_nt(dov, v)
            da_t = _dot_nt(v, dov)

            dal = jnp.where(leafm, da, 0.0).astype(BF16)
            dal_t = jnp.where(leafm_t, da_t, 0.0).astype(BF16)
            dql = _dot(dal, tm["kl"])
            dkl = _dot(dal_t, tm["ql"])
            dsq = dql * tm["eq"]
            dkk = dkl * tm["ek"]
            xq = f32(tm["ql"]) * dql
            xk = f32(tm["kl"]) * dkl
            a_t = jnp.where(leafm_t, _dot_nt(tm["kl"], tm["ql"]), 0.0)
            for (h, e, second, qm, km), m, m_t in zip(tm["lev"], lmask, lmask_t):
                dl = (da if m is None else jnp.where(m, da, 0.0)).astype(BF16)
                dl_t = (da_t if m_t is None else jnp.where(m_t, da_t, 0.0)).astype(BF16)
                dqm = _dot(dl, km)
                dkm = _dot(dl_t, qm)
                dsq = dsq + jnp.where(second, dqm * e, 0.0)
                dkk = dkk + jnp.where(second, 0.0, dkm * e)
                xq = xq + f32(qm) * dqm
                xk = xk + f32(km) * dkm
                al_t = _dot_nt(km, qm)
                a_t = a_t + (al_t if m_t is None else jnp.where(m_t, al_t, 0.0))
            dqs = _dot(dov, st0.astype(BF16))
            dke = _dot(v, gtb)
            dsq = dsq + dqs * jnp.exp(tm["b"])
            dkk = dkk + dke * jnp.exp(tm["kdec"])
            xq = xq + f32(tm["qs"]) * dqs
            xk = xk + f32(tm["ke"]) * dke
            dvv = _dot(a_t.astype(BF16), dov) + _dot_nt(tm["ke"], gtb)
            r_end = jnp.sum(f32(gtb) * _dot(v.T, tm["ke"]) + gt * (st0 * tm["e_c"]), axis=0, keepdims=True)
            g_sc[g] = gt * tm["e_c"] + _dot(dov.T, tm["qs"])
            xh, xm, xl = _split3(xq - xk)
            dlf = (_dot(rtv, xh) + _dot(rtv, xm)) + _dot(rtv, xl) + r_end
            dlf_f = dlf / tm["f"]
            dsp = (1.0 - lbv) * (dlf_f - dkk)
            df_ref[rows, hd] = (dsp * (tm["sp"] * tm["sn"])).astype(BF16)
            dq_ref[rows, hd] = (dsq * _dsilu(qr)).astype(BF16)
            dv_ref[rows, hd] = dvv.astype(BF16)
            dlb_ref[:, hd] += jnp.sum(dlf_f * tm["sn"] - dkk * tm["sn"], axis=0, keepdims=True)

        def chunk(nn, carry):
            for g in range(G):
                head_chunk(g, nc - 1 - nn)
            return carry

        lax.fori_loop(0, nc, chunk, 0)

    blk = lambda off: pl.BlockSpec((tb, G * dk), lambda h, i: (nb - 1 - i, off // G + h))
    vec = pl.BlockSpec((1, G * dk), lambda h, i: (0, h))
    return pl.pallas_call(
        body, grid=(H // G, nb),
        in_specs=[blk(0), blk(0), blk(H), blk(0),
                  pl.BlockSpec((G, nc, dk, dk), lambda h, i: (h, nb - 1 - i, 0, 0)), vec,
                  pl.BlockSpec(msum.shape, lambda h, i: (0, 0)), pl.BlockSpec((C, C), lambda h, i: (0, 0))],
        out_specs=[blk(0), blk(0), blk(0), vec],
        out_shape=[SDS((S, H * dk), BF16)] * 3 + [SDS((1, H * dk), F32)],
        scratch_shapes=[pltpu.VMEM((G, dk, dk), F32)],
        compiler_params=_params("parallel", "arbitrary"), name=name,
    )(p1, f1, p1, do, states, lb, msum, rtri)


def _lb_fwd(logits, *, name):
    W = logits.shape[1]

    def body(l_ref, lb_ref):
        l = l_ref[...]
        m = jnp.max(l, axis=0, keepdims=True)
        e = jnp.exp(l - m)
        p = e / jnp.sum(e, axis=0, keepdims=True)
        lb_ref[...] = (p[0:1] + p[1:2]) - p[0:1]

    return pl.pallas_call(body, out_shape=SDS((1, W), F32), name=name)(logits)


STAT_ROWS = 8


def _stats_reduce(stats_all, logits, *, name):
    W = logits.shape[1]

    def body(s_ref, l_ref, g_ref):
        tot = s_ref[0]
        for d in range(1, N_DEV):
            tot = tot + s_ref[d]
        l = l_ref[...]
        m = jnp.max(l, axis=0, keepdims=True)
        e = jnp.exp(l - m)
        p = e / jnp.sum(e, axis=0, keepdims=True)
        dlb = tot[2:3]
        dl0 = -(p[0:1] * p[1:2]) * dlb
        dl1 = (p[1:2] * (1.0 - p[1:2])) * dlb
        g_ref[0:2] = tot[0:2]
        g_ref[2:3] = dl0
        g_ref[3:4] = dl1
        g_ref[4:7] = tot[3:6]
        g_ref[7:8] = jnp.zeros((1, W), F32)

    return pl.pallas_call(body, out_shape=SDS((STAT_ROWS, W), F32), name=name)(stats_all, logits)


def _adamw(w, m, v, g_parts, *, name, tr=128):
    R, C = w.shape
    ns = len(g_parts)
    n, Rs = g_parts[0].shape[0], g_parts[0].shape[1]
    assert all(p.shape == (n, Rs, C) for p in g_parts) and ns * Rs == R
    tr = min(tr, Rs)
    assert Rs % tr == 0
    nts = Rs // tr
    c1 = 1.0 / (1.0 - ADAM_B1 ** ADAM_STEP)
    c2 = 1.0 / (1.0 - ADAM_B2 ** ADAM_STEP)

    def body(*refs):
        w_ref, m_ref, v_ref = refs[:3]
        g_refs = refs[3:3 + ns]
        go_ref, d_ref, mo_ref, vo_ref = refs[3 + ns:]

        def update(g_ref):
            g = g_ref[0].astype(F32)
            for k in range(1, n):
                g = g + g_ref[k].astype(F32)
            mn = ADAM_B1 * m_ref[...] + (1.0 - ADAM_B1) * g
            vn = ADAM_B2 * v_ref[...] + (1.0 - ADAM_B2) * (g * g)
            d_ref[...] = -ADAM_LR * ((mn * c1) / (jnp.sqrt(vn * c2) + ADAM_EPS) + ADAM_WD * w_ref[...])
            go_ref[...] = g
            mo_ref[...] = mn
            vo_ref[...] = vn

        for s in range(ns):
            if ns == 1:
                update(g_refs[s])
            else:
                pl.when(pl.program_id(0) // nts == s)(functools.partial(update, g_refs[s]))

    def g_map(i, s):
        return (0, jnp.clip(i - s * nts, 0, nts - 1), 0)

    blk = pl.BlockSpec((tr, C), lambda i: (i, 0))
    return pl.pallas_call(
        body, grid=(R // tr,),
        in_specs=[blk, blk, blk] + [pl.BlockSpec((n, tr, C), functools.partial(g_map, s=s)) for s in range(ns)],
        out_specs=[blk] * 4, out_shape=[SDS((R, C), F32)] * 4,
        compiler_params=_params("parallel"), name=name,
    )(w, m, v, *g_parts)


ANY = pl.BlockSpec(memory_space=pl.ANY)
STAGE_BYTES = 2 * 1024 * 1024


def _stage_shape(shape, dtype):
    row_bytes = int(np.prod(shape[1:])) * jnp.dtype(dtype).itemsize
    rows = max(1, min(shape[0], STAGE_BYTES // row_bytes))
    while shape[0] % rows:
        rows -= 1
    return (rows,) + tuple(shape[1:])


def _staged_copy(frm, to, buf, sems):
    rows = buf.shape[0]
    for r0 in range(0, frm.shape[0], rows):
        cp = pltpu.make_async_copy(frm.at[pl.ds(r0, rows)], buf, sems.at[0])
        cp.start()
        cp.wait()
        cp = pltpu.make_async_copy(buf, to.at[pl.ds(r0, rows)], sems.at[1])
        cp.start()
        cp.wait()


def _all_gather(shards, out_shapes, views, *, name):
    n = len(shards)

    def body(*refs):
        ins, outs = refs[:n], refs[n:2 * n]
        send_sems, recv_sems, local_sems = refs[2 * n:2 * n + 3]
        bufs = refs[2 * n + 3:]
        x, y, c = lax.axis_index("x"), lax.axis_index("y"), lax.axis_index("c")
        me, sibling = (x, y, c), (x, y, 1 - c)
        chips = [(1 - x, y), (x, 1 - y), (1 - x, 1 - y)]

        def dev(p):
            return 4 * p[0] + 2 * p[1] + p[2]

        def copy(a, k, block, to, src=None):
            dst = views[a](outs[a], dev(block))
            return pltpu.make_async_remote_copy(
                src_ref=dst if src is None else src, dst_ref=dst,
                send_sem=send_sems.at[a, k], recv_sem=recv_sems.at[a, k],
                device_id=to, device_id_type=MESH)

        first, passed = [], []
        for a in range(n):
            first.append(copy(a, 0, me, sibling, src=ins[a]))
            first += [copy(a, 1 + j, me, (*chip, c), src=ins[a]) for j, chip in enumerate(chips)]
        for cp in first:
            cp.start()
        for a in range(n):
            _staged_copy(ins[a], views[a](outs[a], dev(me)), bufs[a], local_sems)
        for j, chip in enumerate(chips):
            for a in range(n):
                copy(a, 1 + j, (*chip, c), me).wait_recv()
                cp = copy(a, 4 + j, (*chip, c), sibling)
                cp.start()
                passed.append(cp)
        for a in range(n):
            copy(a, 0, sibling, me).wait_recv()
            for j, chip in enumerate(chips):
                copy(a, 4 + j, (*chip, 1 - c), me).wait_recv()
        for cp in first + passed:
            cp.wait_send()

    return pl.pallas_call(
        body, in_specs=[ANY] * n, out_specs=[ANY] * n, out_shape=list(out_shapes),
        scratch_shapes=[pltpu.SemaphoreType.DMA((n, 7)), pltpu.SemaphoreType.DMA((n, 7)),
                        pltpu.SemaphoreType.DMA((2,))]
        + [pltpu.VMEM(_stage_shape(s.shape, s.dtype), s.dtype) for s in shards],
        name=name,
    )(*shards)


HBM = pl.BlockSpec(memory_space=pltpu.HBM)
SEM = pl.BlockSpec(memory_space=pltpu.SEMAPHORE)
EFFECT = pltpu.SideEffectType.DATAFLOW_SIDE_EFFECTING


def _relations(x, y, c):
    for m in range(1, N_DEV):
        yield m, (1 - x if m & 4 else x, 1 - y if m & 2 else y, 1 - c if m & 1 else c)


def _dev_id(p):
    return 4 * p[0] + 2 * p[1] + p[2]


def _send_start(srcs, land_shapes, src_views, dst_views, *, name):
    n = len(srcs)

    def body(*refs):
        ins, lands = refs[:n], refs[n:2 * n]
        send_sems, recv_sems, token = refs[2 * n], refs[2 * n + 1], refs[-1]
        x, y, c = lax.axis_index("x"), lax.axis_index("y"), lax.axis_index("c")
        me = _dev_id((x, y, c))
        for m, p in _relations(x, y, c):
            for a in range(n):
                pltpu.make_async_remote_copy(
                    src_ref=src_views[a](ins[a], me, _dev_id(p), m), dst_ref=dst_views[a](lands[a], me, m),
                    send_sem=send_sems.at[a * (N_DEV - 1) + m - 1], recv_sem=recv_sems.at[a * (N_DEV - 1) + m - 1],
                    device_id=p, device_id_type=MESH).start()
        token[...] = jnp.zeros_like(token)

    lands = [pltpu.with_memory_space_constraint(lax.empty(s.shape, s.dtype), pltpu.HBM) for s in land_shapes]
    srcs = [pltpu.with_memory_space_constraint(v, pltpu.HBM) for v in srcs]
    res = pl.pallas_call(
        body, name=name,
        out_shape=[pltpu.SemaphoreType.DMA((n * (N_DEV - 1),)), pltpu.SemaphoreType.DMA((n * (N_DEV - 1),))]
        + [pltpu.HBM(v.shape, v.dtype) for v in srcs] + [pltpu.HBM(s.shape, s.dtype) for s in land_shapes]
        + [SDS((8, 128), F32)],
        in_specs=[HBM] * (2 * n), out_specs=[SEM, SEM] + [HBM] * (2 * n) + [pl.BlockSpec(memory_space=pltpu.VMEM)],
        input_output_aliases={i: 2 + i for i in range(2 * n)},
        compiler_params=pltpu.CompilerParams(has_side_effects=EFFECT),
    )(*srcs, *lands)
    return res[0], res[1], res[2:2 + n], res[2 + n:2 + 2 * n], res[-1]


def _send_wait(started, src_views, dst_views, own_views, own_shapes, after, *, name):
    send_sems, recv_sems, srcs, lands, _ = started
    n = len(srcs)

    def body(*refs):
        ins, lnd = refs[:n], refs[n:2 * n]
        send_sems, recv_sems = refs[2 * n], refs[2 * n + 1]
        got = refs[2 * n + 3 + n:2 * n + 3 + 2 * n]
        local_sems = refs[2 * n + 3 + 2 * n]
        bufs = refs[2 * n + 4 + 2 * n:]
        x, y, c = lax.axis_index("x"), lax.axis_index("y"), lax.axis_index("c")
        me = _dev_id((x, y, c))
        for m, p in _relations(x, y, c):
            for a in range(n):
                cp = pltpu.make_async_remote_copy(
                    src_ref=src_views[a](ins[a], me, _dev_id(p), m), dst_ref=dst_views[a](lnd[a], me, m),
                    send_sem=send_sems.at[a * (N_DEV - 1) + m - 1], recv_sem=recv_sems.at[a * (N_DEV - 1) + m - 1],
                    device_id=p, device_id_type=MESH)
                cp.wait_send()
                cp.wait_recv()
        for a in range(n):
            frm, to = own_views[a](ins[a], got[a], me)
            _staged_copy(frm, to, bufs[a], local_sems)

    res = pl.pallas_call(
        body, name=name,
        out_shape=[pltpu.HBM(v.shape, v.dtype) for v in srcs] + [pltpu.HBM(v.shape, v.dtype) for v in lands],
        in_specs=[HBM] * (2 * n) + [SEM, SEM, ANY], out_specs=[HBM] * (2 * n),
        input_output_aliases={i: i for i in range(2 * n)},
        scratch_shapes=[pltpu.SemaphoreType.DMA((2,))]
        + [pltpu.VMEM(_stage_shape(s, v.dtype), v.dtype) for s, v in zip(own_shapes, srcs)],
        compiler_params=pltpu.CompilerParams(has_side_effects=EFFECT),
    )(*srcs, *lands, send_sems, recv_sems, after)
    return res[n:]


def kernel(x, norm_gains, fox_w_in, fox_b_f, hgrn_w_in, hgrn_lb_logits, hgrn_onorm, w_out, final_gain, loss_target, m_norm_gains, m_fox_w_in, m_fox_b_f, m_hgrn_w_in, m_hgrn_lb_logits, m_hgrn_onorm, m_w_out, m_final_gain, v_norm_gains, v_fox_w_in, v_fox_b_f, v_hgrn_w_in, v_hgrn_lb_logits, v_hgrn_onorm, v_w_out, v_final_gain):
    _, S, D = x.shape
    H = FOX_HEADS
    W = H * HEAD_DIM
    assert HGRN_HEADS == H and w_out.shape[2] == D
    cf = fox_w_in.shape[2]
    ch = hgrn_w_in.shape[2]
    ro = w_out.shape[1]
    co = hgrn_onorm.shape[1]
    assert N_DEV * cf == 4 * W + H and N_DEV * ch == 4 * W and N_DEV * ro == W and N_DEV * co == W
    x2 = x.reshape(S, D)
    tgt = loss_target.reshape(S, D)

    col = lambda n: (lambda r, i: r.at[:, pl.ds(pl.multiple_of(i * n, n), n)])
    row = lambda n: (lambda r, i: r.at[pl.ds(pl.multiple_of(i * n, n), n), :])
    late_views = [col(ch), row(ro), col(co)]
    late = _send_start(
        [hgrn_w_in[0].astype(BF16), w_out[1].astype(BF16), hgrn_onorm],
        [SDS((D, 4 * W), BF16), SDS((W, D), BF16), SDS((1, W), F32)],
        [lambda r, me, p, m: r] * 3, [lambda r, me, m, v=v: v(r, me) for v in late_views],
        name="gather_layer1_start")
    ng0 = norm_gains[0:1] + late[4][0:1, 0:1]

    wf_g, wo0 = _all_gather(
        [fox_w_in[0].astype(BF16), w_out[0].astype(BF16)], [SDS((N_DEV, D, cf), BF16), SDS((W, D), BF16)],
        [lambda r, p: r.at[p], row(ro)], name="gather_layer0")
    wf = jnp.transpose(wf_g, (1, 0, 2)).reshape(D, N_DEV * cf)
    wf_main = jnp.concatenate([wf[:, :3 * W], wf[:, 3 * W + H:]], axis=1)
    wfl_t = wf[:, 3 * W:3 * W + H].T

    h0 = _rms_fwd(x2, ng0, name="rms0_fwd")
    p0 = _mm_nn([h0], wf_main, BF16, scale_cols=(W, LOG2E * HEAD_DIM ** -0.5), name="fox_in_proj")
    fl_t = _mm_nt_rows(wfl_t, h0, name="fox_forget_proj")
    b_col = fox_b_f.reshape(H, 1)
    kaug = _fox_key_aug(*_fox_gate_fwd(fl_t, b_col, name="fox_gate_fwd"))
    o0, y0, qaug = _fox_fwd(p0, kaug, H=H, name="fox_attn_fwd")
    x1 = _mm_nn([y0], wo0, F32, residual=x2, name="fox_out_proj")

    wh, wo1, onorm = _send_wait(
        late, [lambda r, me, p, m: r] * 3, [lambda r, me, m, v=v: v(r, me) for v in late_views],
        [lambda src, land, me, v=v: (src, v(land, me)) for v in late_views], [(D, ch), (ro, D), (1, co)],
        x1[0:8], name="gather_layer1_wait")
    wh_qig = jnp.concatenate([wh[:, :W], wh[:, 2 * W:]], axis=1)
    wh_f = wh[:, W:2 * W]
    lb = _lb_fwd(hgrn_lb_logits, name="hgrn_lower_bound")
    h1 = _rms_fwd(x1, norm_gains[1:2], name="rms1_fwd")
    p1 = _mm_nn([h1], wh_qig, BF16, name="hgrn_in_proj")
    f1 = _mm_nn([h1], wh_f, F32, name="hgrn_forget_proj")
    o1, y1, states = _hgrn_fwd(p1, f1, lb, onorm, H=H, name="hgrn_fwd")
    xo = _mm_nn([y1], wo1, F32, residual=x1, name="hgrn_out_proj")

    dx2, dx2b, loss_part, dgf = _loss_head(xo, final_gain.reshape(1, D), tgt, name="loss_head")
    loss = lax.psum(jnp.sum(loss_part), ("x", "y", "c"))

    dy1 = _mm_nn([dx2b], wo1, BF16, b_t=True, name="hgrn_out_proj_dx")
    dwo1 = _mm_tn(y1, [dx2b], BF16, name="hgrn_out_proj_dw")
    do1, dg1, donorm = _hgrn_post_bwd(dy1, o1, p1, onorm, H=H, name="hgrn_post_bwd")
    dq1, df1, di1, dlb = _hgrn_bwd(p1, f1, lb, do1, states, H=H, name="hgrn_bwd")
    segs1 = [dq1, df1, di1, dg1]
    dh1 = _mm_nn(segs1, wh, BF16, b_t=True, name="hgrn_in_proj_dx")
    dwh = _mm_tn(h1, segs1, BF16, name="hgrn_in_proj_dw")
    part_views = [col(ch), row(ro)]
    slot = lambda r, me, m: r.at[m]
    ex1 = _send_start([dwh, dwo1], [SDS((N_DEV, D, ch), BF16), SDS((N_DEV, ro, D), BF16)],
                      [lambda r, me, p, m, v=v: v(r, p) for v in part_views], [slot] * 2,
                      name="exchange_layer1_start")
    ng1 = norm_gains[1:2] + ex1[4][0:1, 0:1]
    dx1, dx1b, dng1 = _rms_bwd(x1, ng1, dh1, dx2, name="rms1_bwd")

    dy0 = _mm_nn([dx1b], wo0, BF16, b_t=True, name="fox_out_proj_dx")
    dwo0 = _mm_tn(y0, [dx1b], BF16, name="fox_out_proj_dw")
    do0, dg0, doaug = _fox_post_bwd(dy0, o0, p0, H=H, name="fox_post_bwd")
    dq0, dc_row, dk0, dv0, dc_key = _fox_bwd(p0, kaug, qaug, do0, doaug, H=H, name="fox_attn_bwd")
    dfl_t, dbf = _fox_gate_bwd(dc_row.reshape(H, S), dc_key.reshape(H, S), fl_t, b_col, name="fox_gate_bwd")
    dfl_tb = dfl_t.astype(BF16)
    dwfl_t = _mm_nn([dfl_tb], h0, BF16, name="fox_forget_proj_dw")
    segs0 = [dq0, dk0, dv0, dg0]
    dwf_main = _mm_tn(h0, segs0, BF16, name="fox_in_proj_dw")
    dwf = jnp.concatenate([dwf_main[:, :3 * W], dwfl_t.T, dwf_main[:, 3 * W:]], axis=1)
    dwf_blocks = jnp.transpose(dwf.reshape(D, N_DEV, cf), (1, 0, 2))
    ex0 = _send_start([dwf_blocks, dwo0], [SDS((N_DEV, D, cf), BF16), SDS((N_DEV, ro, D), BF16)],
                      [lambda r, me, p, m: r.at[p], lambda r, me, p, m: row(ro)(r, p)], [slot] * 2,
                      name="exchange_layer0_start")
    wfl_t0 = wfl_t + ex0[4][0:1, 0:1].astype(BF16)
    dh0_f = _mm_nn([dfl_tb.T], wfl_t0, BF16, name="fox_forget_proj_dx")
    dh0 = _mm_nn(segs0, wf_main, BF16, residual=dh0_f, b_t=True, name="fox_in_proj_dx")
    grad_x, _, dng0 = _rms_bwd(x2, norm_gains[0:1], dh0, dx1, name="rms0_bwd")

    own1 = [lambda src, land, me, v=v: (v(src, me), land.at[0]) for v in part_views]
    rh, ro1 = _send_wait(ex1, [lambda r, me, p, m, v=v: v(r, p) for v in part_views], [slot] * 2, own1,
                         [(D, ch), (ro, D)], dng0, name="exchange_layer1_wait")
    rf, ro0 = _send_wait(ex0, [lambda r, me, p, m: r.at[p], lambda r, me, p, m: row(ro)(r, p)], [slot] * 2,
                         [lambda src, land, me: (src.at[me], land.at[0]),
                          lambda src, land, me: (row(ro)(src, me), land.at[0])],
                         [(D, cf), (ro, D)], dng0, name="exchange_layer0_wait")

    pad = lambda a: jnp.pad(a, ((0, 0), (0, W - a.shape[1])))
    stats = jnp.concatenate([dng0, dng1, dlb, dgf, pad(dbf.reshape(1, H)), donorm,
                             jnp.zeros((2, W), F32)], axis=0)
    assert D == W
    (stats_all,) = _all_gather([stats], [SDS((N_DEV, STAT_ROWS, W), F32)], [lambda r, p: r.at[p]],
                               name="gather_small_grads")
    g_small = _stats_reduce(stats_all, hgrn_lb_logits, name="reduce_small_grads")
    me = 4 * lax.axis_index("x") + 2 * lax.axis_index("y") + lax.axis_index("c")
    g_onorm = lax.dynamic_slice_in_dim(g_small[6:7], me * co, co, axis=1)

    def upd(w, m, v, parts, name):
        shp = w.shape
        r2 = (-1, shp[-1])
        g, d, mn, vn = _adamw(w.reshape(r2), m.reshape(r2), v.reshape(r2), parts, name=name)
        return g.reshape(shp), d.reshape(shp), mn.reshape(shp), vn.reshape(shp)

    res = {
        "norm_gains": upd(norm_gains, m_norm_gains, v_norm_gains, [g_small[None, 0:2]], "adamw_norm_gains"),
        "fox_w_in": upd(fox_w_in, m_fox_w_in, v_fox_w_in, [rf], "adamw_fox_w_in"),
        "fox_b_f": upd(fox_b_f, m_fox_b_f, v_fox_b_f, [g_small[None, 5:6, :H]], "adamw_fox_b_f"),
        "hgrn_w_in": upd(hgrn_w_in, m_hgrn_w_in, v_hgrn_w_in, [rh], "adamw_hgrn_w_in"),
        "hgrn_lb_logits": upd(hgrn_lb_logits, m_hgrn_lb_logits, v_hgrn_lb_logits, [g_small[None, 2:4]],
                              "adamw_hgrn_lb_logits"),
        "hgrn_onorm": upd(hgrn_onorm, m_hgrn_onorm, v_hgrn_onorm, [g_onorm[None]], "adamw_hgrn_onorm"),
        "w_out": upd(w_out, m_w_out, v_w_out, [ro0, ro1], "adamw_w_out"),
        "final_gain": upd(final_gain.reshape(1, D), m_final_gain.reshape(1, D), v_final_gain.reshape(1, D),
                          [g_small[None, 4:5]], "adamw_final_gain"),
    }
    order = ["norm_gains", "fox_w_in", "fox_b_f", "hgrn_w_in", "hgrn_lb_logits", "hgrn_onorm", "w_out", "final_gain"]
    fix = lambda n, a: a.reshape(D) if n == "final_gain" else a
    outs = [loss, grad_x.reshape(1, S, D)]
    for k in range(4):
        outs += [fix(n, res[n][k]) for n in order]
    return tuple(outs)
```

```python
import functools

import numpy as np
import jax
import jax.numpy as jnp
from jax import lax
from jax.experimental import pallas as pl
from jax.experimental.pallas import tpu as pltpu

F32 = jnp.float32
BF16 = jnp.bfloat16
SDS = jax.ShapeDtypeStruct
MESH = pl.DeviceIdType.MESH

EPS = 1e-6
ADAM_LR, ADAM_B1, ADAM_B2, ADAM_EPS, ADAM_WD, ADAM_STEP = 0.001, 0.9, 0.999, 1e-08, 0.01, 10

N_DEV = 8
FOX_HEADS = 16
HGRN_HEADS = 16
HEAD_DIM = 128
HGRN_CHUNK = 128
HGRN_LEAF = 16
HGRN_HEADS_PER_STEP = 4
EXP_CLAMP = 85.0
ATT_BLOCK = 512
ATT_HEADS_PER_STEP = 4
ATT_BWD_HEADS_PER_STEP = 2
POST_HEADS_PER_STEP = 4
NEG = -1e30
LOG2E = 1.4426950408889634
LN2 = 0.6931471805599453

VMEM_LIMIT_V7X = 56 * 1024 * 1024


def _params(*sem):
    return pltpu.CompilerParams(dimension_semantics=sem, vmem_limit_bytes=VMEM_LIMIT_V7X)


def _silu(x):
    return x * jax.nn.sigmoid(x)


def _dsilu(x):
    s = jax.nn.sigmoid(x)
    return s * (1.0 + x * (1.0 - s))


def _dot(a, b):
    return jnp.dot(a, b, preferred_element_type=F32)


def _dot_nt(a, b):
    return lax.dot_general(a, b, (((1,), (1,)), ((), ())), preferred_element_type=F32)


def _dot_tn(a, b):
    return lax.dot_general(a, b, (((0,), (0,)), ((), ())), preferred_element_type=F32)


def _mm_nn(a_list, b, out_dtype, *, name, residual=None, scale_cols=None, b_t=False, b_cols=None,
           tm=1024, tn=1024, tk=2048):
    ns = len(a_list)
    M, Ks = a_list[0].shape
    K, N = (b.shape[1], b.shape[0]) if b_t else b.shape
    if b_cols is None:
        b_cols = [(0, N)]
    else:
        assert not b_t
        N = sum(e - s for s, e in b_cols)
    dot = _dot_nt if b_t else _dot
    assert K == ns * Ks and all(a.shape == (M, Ks) for a in a_list)
    if ns > 1:
        tk = tk // 2
    tm, tn, tk = min(tm, M), min(tn, N), min(tk, Ks)
    assert M % tm == 0 and N % tn == 0 and Ks % tk == 0
    assert scale_cols is None or scale_cols[0] % tn == 0
    assert all(s % tn == 0 and e % tn == 0 for s, e in b_cols)
    nks = Ks // tk
    nk = ns * nks
    has_res = residual is not None

    def body(*refs):
        a_refs, b_ref = refs[:ns], refs[ns]
        res_ref = refs[ns + 1] if has_res else None
        o_ref = refs[ns + 1 + has_res]

        def finish(r):
            if has_res:
                r = r + res_ref[...].astype(F32)
            if scale_cols is not None:
                r = r * jnp.where(pl.program_id(1) < scale_cols[0] // tn, scale_cols[1], 1.0)
            o_ref[...] = r.astype(out_dtype)

        if nk == 1:
            finish(dot(a_refs[0][...], b_ref[...]))
            return
        acc_ref = refs[ns + 2 + has_res]
        k = pl.program_id(2)

        @pl.when(k == 0)
        def _():
            acc_ref[...] = jnp.zeros_like(acc_ref)

        for s in range(ns):
            def step(s=s):
                acc_ref[...] += dot(a_refs[s][...], b_ref[...])

            if ns == 1:
                step()
            else:
                pl.when(k // nks == s)(step)

        @pl.when(k == nk - 1)
        def _():
            finish(acc_ref[...])

    def a_map(i, j, k, s):
        return (i, jnp.clip(k - s * nks, 0, nks - 1))

    in_specs = [pl.BlockSpec((tm, tk), functools.partial(a_map, s=s)) for s in range(ns)]
    def b_col(j):
        src = j + b_cols[0][0] // tn
        for (_, e0), (s1, _) in zip(b_cols[:-1], b_cols[1:]):
            src = src + jnp.where(src >= e0 // tn, (s1 - e0) // tn, 0)
        return src

    if b_t:
        in_specs.append(pl.BlockSpec((tn, tk), lambda i, j, k: (j, k)))
    else:
        in_specs.append(pl.BlockSpec((tk, tn), lambda i, j, k: (k, b_col(j))))
    args = list(a_list) + [b]
    if has_res:
        in_specs.append(pl.BlockSpec((tm, tn), lambda i, j, k: (i, j)))
        args.append(residual)
    return pl.pallas_call(
        body, grid=(M // tm, N // tn, nk), in_specs=in_specs,
        out_specs=pl.BlockSpec((tm, tn), lambda i, j, k: (i, j)),
        out_shape=SDS((M, N), out_dtype),
        scratch_shapes=[] if nk == 1 else [pltpu.VMEM((tm, tn), F32)],
        compiler_params=_params("parallel", "parallel", "arbitrary"), name=name,
    )(*args)


def _mm_tn(a, b_list, out_dtype, *, name, tm=2048, tn=1024, tk=512):
    ns = len(b_list)
    S, M = a.shape
    Ns = b_list[0].shape[1]
    assert all(b.shape == (S, Ns) for b in b_list)
    tm, tn, tk = min(tm, M), min(tn, Ns), min(tk, S)
    assert M % tm == 0 and Ns % tn == 0 and S % tk == 0
    njs = Ns // tn
    nk = S // tk

    def body(*refs):
        a_ref, b_refs, o_ref, acc_ref = refs[0], refs[1:1 + ns], refs[1 + ns], refs[2 + ns]
        j, k = pl.program_id(1), pl.program_id(2)

        @pl.when(k == 0)
        def _():
            acc_ref[...] = jnp.zeros_like(acc_ref)

        for s in range(ns):
            def step(s=s):
                acc_ref[...] += _dot_tn(a_ref[...], b_refs[s][...])

            if ns == 1:
                step()
            else:
                pl.when(j // njs == s)(step)

        @pl.when(k == nk - 1)
        def _():
            o_ref[...] = acc_ref[...].astype(out_dtype)

    def b_map(i, j, k, s):
        return (k, jnp.clip(j - s * njs, 0, njs - 1))

    in_specs = [pl.BlockSpec((tk, tm), lambda i, j, k: (k, i))]
    in_specs += [pl.BlockSpec((tk, tn), functools.partial(b_map, s=s)) for s in range(ns)]
    return pl.pallas_call(
        body, grid=(M // tm, ns * njs, nk), in_specs=in_specs,
        out_specs=pl.BlockSpec((tm, tn), lambda i, j, k: (i, j)),
        out_shape=SDS((M, ns * Ns), out_dtype),
        scratch_shapes=[pltpu.VMEM((tm, tn), F32)],
        compiler_params=_params("parallel", "parallel", "arbitrary"), name=name,
    )(a, *b_list)


def _mm_nt_rows(w_t, h, *, name, tn=1024):
    R, K = w_t.shape
    S = h.shape[0]
    tn = min(tn, S)

    def body(w_ref, h_ref, o_ref):
        o_ref[...] = _dot_nt(w_ref[...], h_ref[...])

    return pl.pallas_call(
        body, grid=(S // tn,),
        in_specs=[pl.BlockSpec((R, K), lambda i: (0, 0)), pl.BlockSpec((tn, K), lambda i: (i, 0))],
        out_specs=pl.BlockSpec((R, tn), lambda i: (0, i)),
        out_shape=SDS((R, S), F32), compiler_params=_params("parallel"), name=name,
    )(w_t, h)


def _rms_fwd(x, gain, *, name, tm=512):
    S, D = x.shape
    tm = min(tm, S)

    def body(x_ref, g_ref, h_ref):
        xv = x_ref[...]
        r = lax.rsqrt(jnp.mean(xv * xv, axis=-1, keepdims=True) + EPS)
        h_ref[...] = ((xv * r) * g_ref[...]).astype(BF16)

    return pl.pallas_call(
        body, grid=(S // tm,),
        in_specs=[pl.BlockSpec((tm, D), lambda i: (i, 0)), pl.BlockSpec((1, D), lambda i: (0, 0))],
        out_specs=pl.BlockSpec((tm, D), lambda i: (i, 0)),
        out_shape=SDS((S, D), BF16), compiler_params=_params("parallel"), name=name,
    )(x, gain)


def _rms_bwd(x, gain, dh, dres, *, name, tm=256):
    S, D = x.shape
    tm = min(tm, S)

    def body(x_ref, g_ref, dh_ref, dres_ref, dx_ref, dxb_ref, dg_ref):
        @pl.when(pl.program_id(0) == 0)
        def _():
            dg_ref[...] = jnp.zeros_like(dg_ref)

        xv = x_ref[...]
        r = lax.rsqrt(jnp.mean(xv * xv, axis=-1, keepdims=True) + EPS)
        xh = xv * r
        dhv = dh_ref[...].astype(F32)
        dg_ref[...] += jnp.sum(dhv * xh, axis=0, keepdims=True)
        dxh = dhv * g_ref[...]
        dx = r * (dxh - xh * jnp.mean(dxh * xh, axis=-1, keepdims=True)) + dres_ref[...]
        dx_ref[...] = dx
        dxb_ref[...] = dx.astype(BF16)

    row = pl.BlockSpec((tm, D), lambda i: (i, 0))
    vec = pl.BlockSpec((1, D), lambda i: (0, 0))
    return pl.pallas_call(
        body, grid=(S // tm,), in_specs=[row, vec, row, row], out_specs=[row, row, vec],
        out_shape=[SDS((S, D), F32), SDS((S, D), BF16), SDS((1, D), F32)],
        compiler_params=_params("arbitrary"), name=name,
    )(x, gain, dh, dres)


def _loss_head(x, gain, target, *, name, tm=256):
    S, D = x.shape
    tm = min(tm, S)
    assert tm % 8 == 0 and D % 128 == 0

    def body(x_ref, g_ref, t_ref, dx_ref, dxb_ref, loss_ref, dg_ref):
        @pl.when(pl.program_id(0) == 0)
        def _():
            dg_ref[...] = jnp.zeros_like(dg_ref)
            loss_ref[...] = jnp.zeros_like(loss_ref)

        xv = x_ref[...]
        g = g_ref[...]
        r = lax.rsqrt(jnp.mean(xv * xv, axis=-1, keepdims=True) + EPS)
        xh = xv * r
        err = xh * g - t_ref[...]
        e2 = (err * err).reshape(tm // 8, 8, D).sum(axis=0)
        part = e2[:, 0:128]
        for k in range(1, D // 128):
            part = part + e2[:, k * 128:(k + 1) * 128]
        loss_ref[...] += part * (0.5 / D)
        dy = err * (1.0 / D)
        dg_ref[...] += jnp.sum(dy * xh, axis=0, keepdims=True)
        dxh = dy * g
        dx = r * (dxh - xh * jnp.mean(dxh * xh, axis=-1, keepdims=True))
        dx_ref[...] = dx
        dxb_ref[...] = dx.astype(BF16)

    row = pl.BlockSpec((tm, D), lambda i: (i, 0))
    vec = pl.BlockSpec((1, D), lambda i: (0, 0))
    return pl.pallas_call(
        body, grid=(S // tm,), in_specs=[row, vec, row],
        out_specs=[row, row, pl.BlockSpec((8, 128), lambda i: (0, 0)), vec],
        out_shape=[SDS((S, D), F32), SDS((S, D), BF16), SDS((8, 128), F32), SDS((1, D), F32)],
        compiler_params=_params("arbitrary"), name=name,
    )(x, gain, target)


def _split3(x):
    hi = x.astype(BF16)
    r1 = x - hi.astype(F32)
    mid = r1.astype(BF16)
    lo = (r1 - mid.astype(F32)).astype(BF16)
    return hi, mid, lo


def _split2(x):
    hi = x.astype(BF16)
    lo = (x - hi.astype(F32)).astype(BF16)
    return hi, lo


def _fox_gate_fwd(fl_t, b_col, *, name):
    H, S = fl_t.shape
    L = 128
    tri = jnp.asarray(np.triu(np.ones((L, L), np.float32)), BF16)

    def body(fl_ref, b_ref, tri_ref, hi_ref, mid_ref, lo_ref, carry):
        @pl.when(pl.program_id(0) == 0)
        def _():
            carry[...] = jnp.zeros_like(carry)

        z = fl_ref[...] + b_ref[...]
        lf = jnp.minimum(z, 0.0) - jnp.log(1.0 + jnp.exp(-jnp.abs(z)))
        hi, mid, lo = _split3(lf)
        t = tri_ref[...]
        c = (_dot(hi, t) + _dot(mid, t)) + _dot(lo, t) + carry[...]
        carry[...] = c[:, L - 1:L]
        hi_ref[...], mid_ref[...], lo_ref[...] = _split3(c * (-LOG2E))

    blk = pl.BlockSpec((H, L), lambda i: (0, i))
    return pl.pallas_call(
        body, grid=(S // L,),
        in_specs=[blk, pl.BlockSpec((H, 1), lambda i: (0, 0)), pl.BlockSpec((L, L), lambda i: (0, 0))],
        out_specs=[blk] * 3, out_shape=[SDS((H, S), BF16)] * 3, scratch_shapes=[pltpu.VMEM((H, 1), F32)],
        compiler_params=_params("arbitrary"), name=name,
    )(fl_t, b_col, tri)


def _fox_gate_bwd(dc_row, dc_key, fl_t, b_col, *, name):
    H, S = fl_t.shape
    L = 128
    n = S // L
    tri = jnp.asarray(np.tril(np.ones((L, L), np.float32)), BF16)

    def body(dcr_ref, dck_ref, fl_ref, b_ref, tri_ref, dfl_ref, db_ref, carry):
        @pl.when(pl.program_id(0) == 0)
        def _():
            carry[...] = jnp.zeros_like(carry)
            db_ref[...] = jnp.zeros_like(db_ref)

        hi, mid, lo = _split3(dcr_ref[...] + dck_ref[...])
        t = tri_ref[...]
        dlf = (_dot(hi, t) + _dot(mid, t)) + _dot(lo, t) + carry[...]
        carry[...] = dlf[:, 0:1]
        z = fl_ref[...] + b_ref[...]
        dfl = dlf * jax.nn.sigmoid(-z)
        dfl_ref[...] = dfl
        db_ref[...] += jnp.sum(dfl, axis=1, keepdims=True)

    blk = pl.BlockSpec((H, L), lambda i: (0, n - 1 - i))
    col = pl.BlockSpec((H, 1), lambda i: (0, 0))
    return pl.pallas_call(
        body, grid=(n,), in_specs=[blk, blk, blk, col, pl.BlockSpec((L, L), lambda i: (0, 0))],
        out_specs=[blk, col], out_shape=[SDS((H, S), F32), SDS((H, 1), F32)],
        scratch_shapes=[pltpu.VMEM((H, 1), F32)], compiler_params=_params("arbitrary"), name=name,
    )(dc_row, dc_key, fl_t, b_col, tri)


AUG = HEAD_DIM


def _lane_select(cols, shape):
    lane = lax.broadcasted_iota(jnp.int32, shape, 1)
    out = jnp.zeros(shape, BF16)
    for k, c in reversed(list(enumerate(cols))):
        c = jnp.full(shape, c, BF16) if isinstance(c, (int, float)) else jnp.broadcast_to(c, shape).astype(BF16)
        out = jnp.where(lane == k, c, out)
    return out


def _fox_key_aug(b_hi, b_mid, b_lo):
    H, S = b_hi.shape
    ones = jnp.ones((H, S), BF16)
    ka = jnp.stack([b_hi, b_mid, b_lo, ones, ones, ones], axis=-1)
    ka = jnp.pad(ka, ((0, 0), (0, 0), (0, AUG - 6)))
    return jnp.transpose(ka, (1, 0, 2)).reshape(S, H * AUG)


def _fox_fwd(p0, kaug, *, H, name):
    S = p0.shape[0]
    T = min(ATT_BLOCK, S)
    nq = S // T
    dh = HEAD_DIM
    G = ATT_HEADS_PER_STEP
    assert H % G == 0

    def body(q_ref, k_ref, ka_ref, v_ref, g_ref, o_ref, y_ref, qa_ref, m_sc, acc_sc, p_sc, al_sc):
        i = pl.program_id(1)
        qaug = _lane_select([1.0, 1.0, 1.0], (T, AUG))
        ones = jnp.ones((T, dh), BF16)
        m_sc[...] = jnp.full_like(m_sc, NEG)
        acc_sc[...] = jnp.zeros_like(acc_sc)
        p_sc[...] = jnp.zeros_like(p_sc)
        al_sc[...] = jnp.ones_like(al_sc)

        def step(j, masked):
            rows = pl.ds(pl.multiple_of(j * T, T), T)
            prev = pl.ds(pl.multiple_of(jnp.maximum(j - 1, 0) * T, T), T)
            for g in range(G):
                hd = slice(g * dh, (g + 1) * dh)
                vp = jnp.concatenate([v_ref[prev, hd], ones], axis=1)
                acc_sc[g] = jnp.tile(al_sc[g], (1, 2)) * acc_sc[g] + _dot(p_sc[g], vp)
                q = jnp.concatenate([q_ref[:, hd], qaug], axis=1)
                kj = jnp.concatenate([k_ref[rows, hd], ka_ref[rows, hd]], axis=1)
                t = _dot_nt(q, kj)
                if masked:
                    row = lax.broadcasted_iota(jnp.int32, (T, T), 0)
                    col = lax.broadcasted_iota(jnp.int32, (T, T), 1)
                    t = jnp.where(row >= col, t, NEG)
                m_prev = m_sc[g]
                m_new = jnp.maximum(m_prev, jnp.max(t, axis=-1, keepdims=True))
                p_sc[g] = jnp.exp2(t - jnp.tile(m_new, (1, T // 128))).astype(BF16)
                al_sc[g] = jnp.exp2(m_prev - m_new)
                m_sc[g] = m_new

        def loop_body(j, carry):
            step(j, False)
            return carry

        lax.fori_loop(0, i, loop_body, 0)
        step(i, True)
        rows = pl.ds(pl.multiple_of(i * T, T), T)
        for g in range(G):
            hd = slice(g * dh, (g + 1) * dh)
            vp = jnp.concatenate([v_ref[rows, hd], ones], axis=1)
            acc = jnp.tile(al_sc[g], (1, 2)) * acc_sc[g] + _dot(p_sc[g], vp)
            l = acc[:, dh:]
            o = acc[:, :dh] / l
            o_ref[:, hd] = o
            y_ref[:, hd] = (o * _silu(g_ref[:, hd].astype(F32))).astype(BF16)
            hi, mid, lo = _split3(-(m_sc[g] + jnp.log2(l)))
            qa_ref[:, hd] = _lane_select([1.0, 1.0, 1.0, hi, mid, lo], (T, AUG))

    blk = lambda off: pl.BlockSpec((T, G * dh), lambda h, i: (i, off // G + h))
    full = lambda off: pl.BlockSpec((S, G * dh), lambda h, i: (0, off // G + h), pipeline_mode=pl.Buffered(1))
    return pl.pallas_call(
        body, grid=(H // G, nq),
        in_specs=[blk(0), full(H), full(0), full(2 * H), blk(3 * H)],
        out_specs=[blk(0), blk(0), blk(0)],
        out_shape=[SDS((S, H * dh), F32), SDS((S, H * dh), BF16), SDS((S, H * AUG), BF16)],
        scratch_shapes=[pltpu.VMEM((G, T, 128), F32), pltpu.VMEM((G, T, 2 * dh), F32),
                        pltpu.VMEM((G, T, T), BF16), pltpu.VMEM((G, T, 128), F32)],
        compiler_params=_params("parallel", "arbitrary"), name=name,
    )(p0, p0, kaug, p0, p0)


def _fox_post_bwd(dy, o, p0, *, H, name, tm=512):
    S = dy.shape[0]
    dh = HEAD_DIM
    tm = min(tm, S)
    G = POST_HEADS_PER_STEP
    assert H % G == 0

    def body(dy_ref, o_ref, g_ref, do_ref, dg_ref, da_ref):
        dyv = dy_ref[...].astype(F32)
        ov = o_ref[...]
        g = g_ref[...].astype(F32)
        do = (dyv * _silu(g)).astype(BF16)
        do_ref[...] = do
        dg_ref[...] = (dyv * ov * _dsilu(g)).astype(BF16)
        prod = do.astype(F32) * ov
        for k in range(G):
            hd = slice(k * dh, (k + 1) * dh)
            delta = jnp.sum(prod[:, hd], axis=-1, keepdims=True)
            hi, mid, lo = _split3(-jnp.broadcast_to(delta, (tm, AUG)))
            da_ref[:, hd] = _lane_select([hi, mid, lo], (tm, AUG))

    blk = pl.BlockSpec((tm, G * dh), lambda h, i: (i, h))
    return pl.pallas_call(
        body, grid=(H // G, S // tm),
        in_specs=[blk, blk, pl.BlockSpec((tm, G * dh), lambda h, i: (i, 3 * H // G + h))],
        out_specs=[blk, blk, blk],
        out_shape=[SDS((S, H * dh), BF16), SDS((S, H * dh), BF16), SDS((S, H * AUG), BF16)],
        compiler_params=_params("parallel", "parallel"), name=name,
    )(dy, o, p0)


def _fox_bwd(p0, kaug, qaug, do, doaug, *, H, name):
    S = p0.shape[0]
    T = min(ATT_BLOCK, S)
    nq = S // T
    dh = HEAD_DIM
    scale = dh ** -0.5
    G = ATT_BWD_HEADS_PER_STEP
    assert H % G == 0

    def body(q_ref, qa_ref, k_ref, ka_ref, v_ref, do_ref, da_ref, dq_ref, rs_ref, dk_ref, dv_ref, dc_ref,
             dq_sc, dk_sc, dv_sc, pt_sc, dst_sc):
        j = pl.program_id(1)
        vaug = _lane_select([1.0, 1.0, 1.0], (T, AUG))
        ones = jnp.ones((T, dh), BF16)

        @pl.when(j == 0)
        def _():
            dq_sc[...] = jnp.zeros_like(dq_sc)

        dk_sc[...] = jnp.zeros_like(dk_sc)
        dv_sc[...] = jnp.zeros_like(dv_sc)
        pt_sc[...] = jnp.zeros_like(pt_sc)
        dst_sc[...] = jnp.zeros_like(dst_sc)

        def apply(prev):
            for g in range(G):
                hd = slice(g * dh, (g + 1) * dh)
                dv_sc[g] += _dot(pt_sc[g], do_ref[prev, hd])
                dk_sc[g] += _dot(dst_sc[g], jnp.concatenate([q_ref[prev, hd], ones], axis=1))
                dq_sc[g, prev] += _dot_tn(dst_sc[g], jnp.concatenate([k_ref[:, hd], ones], axis=1))

        def step(i, masked):
            rows = pl.ds(pl.multiple_of(i * T, T), T)
            apply(pl.ds(pl.multiple_of(jnp.maximum(i - 1, j) * T, T), T))
            for g in range(G):
                hd = slice(g * dh, (g + 1) * dh)
                k = jnp.concatenate([k_ref[:, hd], ka_ref[:, hd]], axis=1)
                v = jnp.concatenate([v_ref[:, hd], vaug], axis=1)
                pt = jnp.exp2(_dot_nt(k, jnp.concatenate([q_ref[rows, hd], qa_ref[rows, hd]], axis=1)))
                if masked:
                    row = lax.broadcasted_iota(jnp.int32, (T, T), 0)
                    col = lax.broadcasted_iota(jnp.int32, (T, T), 1)
                    pt = jnp.where(col >= row, pt, 0.0)
                dst = pt * _dot_nt(v, jnp.concatenate([do_ref[rows, hd], da_ref[rows, hd]], axis=1))
                pt_sc[g] = pt.astype(BF16)
                dst_sc[g] = dst.astype(BF16)

        step(j, True)

        def loop_body(i, carry):
            step(i, False)
            return carry

        lax.fori_loop(j + 1, nq, loop_body, 0)
        apply(pl.ds((nq - 1) * T, T))
        for g in range(G):
            hd = slice(g * dh, (g + 1) * dh)
            dk_ref[:, hd] = (dk_sc[g, :, :dh] * LN2).astype(BF16)
            dv_ref[:, hd] = dv_sc[g].astype(BF16)
            dc_ref[g] = -jnp.transpose(dk_sc[g, :, dh:])[0:1]

        @pl.when(j == nq - 1)
        def _():
            for g in range(G):
                dq_ref[:, g * dh:(g + 1) * dh] = (dq_sc[g, :, :dh] * scale).astype(BF16)
                for i in range(nq):
                    rs_ref[g, :, i * T:(i + 1) * T] = jnp.transpose(dq_sc[g, i * T:(i + 1) * T, dh:])[0:1]

    blk = lambda off: pl.BlockSpec((T, G * dh), lambda h, j: (j, off // G + h))
    full = lambda off: pl.BlockSpec((S, G * dh), lambda h, j: (0, off // G + h))
    once = lambda off: pl.BlockSpec((S, G * dh), lambda h, j: (0, off // G + h), pipeline_mode=pl.Buffered(1))
    rowv = pl.BlockSpec((G, 1, T), lambda h, j: (h, 0, j))
    return pl.pallas_call(
        body, grid=(H // G, nq),
        in_specs=[once(0), once(0), blk(H), blk(0), blk(2 * H), once(0), once(0)],
        out_specs=[full(0), pl.BlockSpec((G, 1, S), lambda h, j: (h, 0, 0)), blk(0), blk(0), rowv],
        out_shape=[SDS((S, H * dh), BF16), SDS((H, 1, S), F32), SDS((S, H * dh), BF16), SDS((S, H * dh), BF16),
                   SDS((H, 1, S), F32)],
        scratch_shapes=[pltpu.VMEM((G, S, 2 * dh), F32), pltpu.VMEM((G, T, 2 * dh), F32), pltpu.VMEM((G, T, dh), F32),
                        pltpu.VMEM((G, T, T), BF16), pltpu.VMEM((G, T, T), BF16)],
        compiler_params=_params("parallel", "arbitrary"), name=name,
    )(p0, qaug, p0, kaug, p0, do, doaug)


def _hgrn_levels(C, leaf):
    levels = []
    h = C // 2
    while h >= leaf:
        levels.append(h)
        h //= 2
    return levels


def _hgrn_sum_matrix(C, leaf):
    t = np.arange(C)[:, None]
    u = np.arange(C)[None, :]
    mats = [(u <= t), (u > t)]
    for h in _hgrn_levels(C, leaf):
        start = (t // (2 * h)) * (2 * h)
        mid = start + h - 1
        second = t > mid
        m = np.where(second, (u > mid) & (u <= t), (u > t) & (u <= mid))
        mats.append(m)
    lstart = (t // leaf) * leaf
    mats.append((u >= lstart) & (u <= t))
    return np.concatenate([m.astype(np.float32) for m in mats], axis=0)


def _hgrn_chunk_terms(qr, fz, lb, msum, C, leaf):
    levels = _hgrn_levels(C, leaf)
    sq = _silu(qr)
    sp = 1.0 / (1.0 + jnp.exp(-fz))
    sn = 1.0 / (1.0 + jnp.exp(fz))
    f = lb + (1.0 - lb) * sp
    lf = jnp.log(f)
    k = (1.0 - lb) * sn
    hi, lo = _split2(lf)
    dsum = _dot(msum, hi) + _dot(msum, lo)
    b = dsum[0:C]
    kdec = dsum[C:2 * C]
    rowi = lax.broadcasted_iota(jnp.int32, (C, 1), 0)
    lev = []
    for n, h in enumerate(levels):
        e = jnp.exp(dsum[(2 + n) * C:(3 + n) * C])
        selq = jnp.where((rowi % (2 * h)) >= h, 1.0, 0.0)
        qm = (sq * e * selq).astype(BF16)
        km = (k * e * (1.0 - selq)).astype(BF16)
        lev.append((h, e, selq, qm, km))
    dleaf = dsum[(2 + len(levels)) * C:(3 + len(levels)) * C]
    eq = jnp.exp(dleaf)
    ek = jnp.exp(jnp.minimum(-dleaf, EXP_CLAMP))
    return dict(sq=sq, sp=sp, sn=sn, f=f, k=k, b=b, kdec=kdec, lev=lev, eq=eq, ek=ek,
                ql=(sq * eq).astype(BF16), kl=(k * ek).astype(BF16),
                qs=(sq * jnp.exp(b)).astype(BF16), ke=(k * jnp.exp(kdec)).astype(BF16),
                e_c=jnp.exp(b[C - 1:C, :]))


def _hgrn_masks(C, leaf, transposed):
    a = lax.broadcasted_iota(jnp.int32, (C, C), 0)
    bb = lax.broadcasted_iota(jnp.int32, (C, C), 1)
    t, s = (bb, a) if transposed else (a, bb)
    lev = [None if 2 * h == C else (t // (2 * h)) == (s // (2 * h)) for h in _hgrn_levels(C, leaf)]
    if leaf == C:
        leafm = s <= t
    else:
        leafm = ((t // leaf) == (s // leaf)) & (s <= t)
    return lev, leafm


def _hgrn_fwd(p1, f1, lb, onorm, *, H, name, tb=512):
    S = p1.shape[0]
    dk = HEAD_DIM
    C = min(HGRN_CHUNK, S)
    leaf = min(HGRN_LEAF, C)
    tb = min(tb, S)
    nc = tb // C
    G = HGRN_HEADS_PER_STEP
    assert H % G == 0
    msum = jnp.asarray(_hgrn_sum_matrix(C, leaf), BF16)

    def body(q_ref, f_ref, v_ref, g_ref, lb_ref, on_ref, ms_ref, o_ref, y_ref, st_ref, st_sc):
        @pl.when(pl.program_id(1) == 0)
        def _():
            st_sc[...] = jnp.zeros_like(st_sc)

        msv = ms_ref[...]
        lmask, leafm = _hgrn_masks(C, leaf, False)

        def chunk(n, carry):
            rows = pl.ds(pl.multiple_of(n * C, C), C)
            for g in range(G):
                hd = slice(g * dk, (g + 1) * dk)
                tm = _hgrn_chunk_terms(q_ref[rows, hd].astype(F32), f_ref[rows, hd], lb_ref[:, hd], msv, C, leaf)
                v = v_ref[rows, hd]
                st = st_sc[g]
                st_ref[g, n] = st
                a = jnp.where(leafm, _dot_nt(tm["ql"], tm["kl"]), 0.0)
                for (h, e, selq, qm, km), m in zip(tm["lev"], lmask):
                    al = _dot_nt(qm, km)
                    a = a + (al if m is None else jnp.where(m, al, 0.0))
                o = _dot_nt(tm["qs"], st.astype(BF16)) + _dot(a.astype(BF16), v)
                st_sc[g] = st * tm["e_c"] + _dot(v.T, tm["ke"])
                o_ref[rows, hd] = o
                rn = lax.rsqrt(jnp.mean(o * o, axis=-1, keepdims=True) + EPS)
                y = ((o * rn) * on_ref[:, hd]) * _silu(g_ref[rows, hd].astype(F32))
                y_ref[rows, hd] = y.astype(BF16)
            return carry

        lax.fori_loop(0, nc, chunk, 0)

    blk = lambda off: pl.BlockSpec((tb, G * dk), lambda h, i: (i, off // G + h))
    vec = pl.BlockSpec((1, G * dk), lambda h, i: (0, h))
    return pl.pallas_call(
        body, grid=(H // G, S // tb),
        in_specs=[blk(0), blk(0), blk(H), blk(2 * H), vec, vec,
                  pl.BlockSpec(msum.shape, lambda h, i: (0, 0))],
        out_specs=[blk(0), blk(0), pl.BlockSpec((G, nc, dk, dk), lambda h, i: (h, i, 0, 0))],
        out_shape=[SDS((S, H * dk), F32), SDS((S, H * dk), BF16), SDS((H, S // C, dk, dk), F32)],
        scratch_shapes=[pltpu.VMEM((G, dk, dk), F32)],
        compiler_params=_params("parallel", "arbitrary"), name=name,
    )(p1, f1, p1, p1, lb, onorm, msum)


def _hgrn_post_bwd(dy, o, p1, onorm, *, H, name, tm=512):
    S = dy.shape[0]
    dk = HEAD_DIM
    tm = min(tm, S)
    G = POST_HEADS_PER_STEP
    assert H % G == 0

    def body(dy_ref, o_ref, g_ref, on_ref, do_ref, dg_ref, don_ref):
        @pl.when(pl.program_id(1) == 0)
        def _():
            don_ref[...] = jnp.zeros_like(don_ref)

        for k in range(G):
            hd = slice(k * dk, (k + 1) * dk)
            dyv = dy_ref[:, hd].astype(F32)
            ov = o_ref[:, hd]
            g = g_ref[:, hd].astype(F32)
            onv = on_ref[:, hd]
            rn = lax.rsqrt(jnp.mean(ov * ov, axis=-1, keepdims=True) + EPS)
            oh = ov * rn
            dn = dyv * _silu(g)
            dg_ref[:, hd] = (dyv * (oh * onv) * _dsilu(g)).astype(BF16)
            don_ref[:, hd] += jnp.sum(dn * oh, axis=0, keepdims=True)
            doh = dn * onv
            do_ref[:, hd] = (rn * (doh - oh * jnp.mean(doh * oh, axis=-1, keepdims=True))).astype(BF16)

    blk = pl.BlockSpec((tm, G * dk), lambda h, i: (i, h))
    vec = pl.BlockSpec((1, G * dk), lambda h, i: (0, h))
    return pl.pallas_call(
        body, grid=(H // G, S // tm),
        in_specs=[blk, blk, pl.BlockSpec((tm, G * dk), lambda h, i: (i, 2 * H // G + h)), vec],
        out_specs=[blk, blk, vec],
        out_shape=[SDS((S, H * dk), BF16), SDS((S, H * dk), BF16), SDS((1, H * dk), F32)],
        compiler_params=_params("parallel", "arbitrary"), name=name,
    )(dy, o, p1, onorm)


def _hgrn_bwd(p1, f1, lb, do, states, *, H, name, tb=512):
    S = p1.shape[0]
    dk = HEAD_DIM
    C = min(HGRN_CHUNK, S)
    leaf = min(HGRN_LEAF, C)
    tb = min(tb, S)
    nc = tb // C
    nb = S // tb
    G = HGRN_HEADS_PER_STEP
    assert H % G == 0
    msum = jnp.asarray(_hgrn_sum_matrix(C, leaf), BF16)
    rtri = jnp.asarray(np.triu(np.ones((C, C), np.float32)), BF16)

    def body(q_ref, f_ref, v_ref, do_ref, st_ref, lb_ref, ms_ref, rt_ref,
             dq_ref, df_ref, dv_ref, dlb_ref, g_sc):
        @pl.when(pl.program_id(1) == 0)
        def _():
            g_sc[...] = jnp.zeros_like(g_sc)
            dlb_ref[...] = jnp.zeros_like(dlb_ref)

        msv = ms_ref[...]
        rtv = rt_ref[...]
        lmask, leafm = _hgrn_masks(C, leaf, False)
        lmask_t, leafm_t = _hgrn_masks(C, leaf, True)
        f32 = lambda z: z.astype(F32)

        def head_chunk(g, n):
            hd = slice(g * dk, (g + 1) * dk)
            rows = pl.ds(pl.multiple_of(n * C, C), C)
            lbv = lb_ref[:, hd]
            qr = q_ref[rows, hd].astype(F32)
            tm = _hgrn_chunk_terms(qr, f_ref[rows, hd], lbv, msv, C, leaf)
            v = v_ref[rows, hd]
            dov = do_ref[rows, hd]
            st0 = st_ref[g, n]
            gt = g_sc[g]
            gtb = gt.astype(BF16)
            da = _dot_nt(dov, v)
            da_t = _dot_nt(v, dov)

            dal = jnp.where(leafm, da, 0.0).astype(BF16)
            dal_t = jnp.where(leafm_t, da_t, 0.0).astype(BF16)
            dql = _dot(dal, tm["kl"])
            dkl = _dot(dal_t, tm["ql"])
            dsq = dql * tm["eq"]
            dkk = dkl * tm["ek"]
            xq = f32(tm["ql"]) * dql
            xk = f32(tm["kl"]) * dkl
            a_t = jnp.where(leafm_t, _dot_nt(tm["kl"], tm["ql"]), 0.0)
            for (h, e, selq, qm, km), m, m_t in zip(tm["lev"], lmask, lmask_t):
                dl = (da if m is None else jnp.where(m, da, 0.0)).astype(BF16)
                dl_t = (da_t if m_t is None else jnp.where(m_t, da_t, 0.0)).astype(BF16)
                dqm = _dot(dl, km)
                dkm = _dot(dl_t, qm)
                dsq = dsq + dqm * (e * selq)
                dkk = dkk + dkm * (e * (1.0 - selq))
                xq = xq + f32(qm) * dqm
                xk = xk + f32(km) * dkm
                al_t = _dot_nt(km, qm)
                a_t = a_t + (al_t if m_t is None else jnp.where(m_t, al_t, 0.0))
            dqs = _dot(dov, st0.astype(BF16))
            dke = _dot(v, gtb)
            dsq = dsq + dqs * jnp.exp(tm["b"])
            dkk = dkk + dke * jnp.exp(tm["kdec"])
            xq = xq + f32(tm["qs"]) * dqs
            xk = xk + f32(tm["ke"]) * dke
            dvv = _dot(a_t.astype(BF16), dov) + _dot_nt(tm["ke"], gtb)
            r_end = jnp.sum(f32(gtb) * _dot(v.T, tm["ke"]) + gt * (st0 * tm["e_c"]), axis=0, keepdims=True)
            g_sc[g] = gt * tm["e_c"] + _dot(dov.T, tm["qs"])
            xh, xm, xl = _split3(xq - xk)
            dlf = (_dot(rtv, xh) + _dot(rtv, xm)) + _dot(rtv, xl) + r_end
            dlf_f = dlf / tm["f"]
            dsp = (1.0 - lbv) * (dlf_f - dkk)
            df_ref[rows, hd] = (dsp * (tm["sp"] * tm["sn"])).astype(BF16)
            dq_ref[rows, hd] = (dsq * _dsilu(qr)).astype(BF16)
            dv_ref[rows, hd] = dvv.astype(BF16)
            dlb_ref[:, hd] += jnp.sum(dlf_f * tm["sn"] - dkk * tm["sn"], axis=0, keepdims=True)

        def chunk(nn, carry):
            for g in range(G):
                head_chunk(g, nc - 1 - nn)
            return carry

        lax.fori_loop(0, nc, chunk, 0)

    blk = lambda off: pl.BlockSpec((tb, G * dk), lambda h, i: (nb - 1 - i, off // G + h))
    vec = pl.BlockSpec((1, G * dk), lambda h, i: (0, h))
    return pl.pallas_call(
        body, grid=(H // G, nb),
        in_specs=[blk(0), blk(0), blk(H), blk(0),
                  pl.BlockSpec((G, nc, dk, dk), lambda h, i: (h, nb - 1 - i, 0, 0)), vec,
                  pl.BlockSpec(msum.shape, lambda h, i: (0, 0)), pl.BlockSpec((C, C), lambda h, i: (0, 0))],
        out_specs=[blk(0), blk(0), blk(0), vec],
        out_shape=[SDS((S, H * dk), BF16)] * 3 + [SDS((1, H * dk), F32)],
        scratch_shapes=[pltpu.VMEM((G, dk, dk), F32)],
        compiler_params=_params("parallel", "arbitrary"), name=name,
    )(p1, f1, p1, do, states, lb, msum, rtri)


def _lb_fwd(logits, *, name):
    W = logits.shape[1]

    def body(l_ref, lb_ref):
        l = l_ref[...]
        m = jnp.max(l, axis=0, keepdims=True)
        e = jnp.exp(l - m)
        p = e / jnp.sum(e, axis=0, keepdims=True)
        lb_ref[...] = (p[0:1] + p[1:2]) - p[0:1]

    return pl.pallas_call(body, out_shape=SDS((1, W), F32), name=name)(logits)


STAT_ROWS = 8


def _stats_reduce(stats_all, logits, *, name):
    W = logits.shape[1]

    def body(s_ref, l_ref, g_ref):
        tot = s_ref[0]
        for d in range(1, N_DEV):
            tot = tot + s_ref[d]
        l = l_ref[...]
        m = jnp.max(l, axis=0, keepdims=True)
        e = jnp.exp(l - m)
        p = e / jnp.sum(e, axis=0, keepdims=True)
        dlb = tot[2:3]
        dl0 = -(p[0:1] * p[1:2]) * dlb
        dl1 = (p[1:2] * (1.0 - p[1:2])) * dlb
        g_ref[0:2] = tot[0:2]
        g_ref[2:3] = dl0
        g_ref[3:4] = dl1
        g_ref[4:7] = tot[3:6]
        g_ref[7:8] = jnp.zeros((1, W), F32)

    return pl.pallas_call(body, out_shape=SDS((STAT_ROWS, W), F32), name=name)(stats_all, logits)


def _adamw(w, m, v, g_parts, *, name, tr=128):
    R, C = w.shape
    ns = len(g_parts)
    n, Rs = g_parts[0].shape[0], g_parts[0].shape[1]
    assert all(p.shape == (n, Rs, C) for p in g_parts) and ns * Rs == R
    tr = min(tr, Rs)
    assert Rs % tr == 0
    nts = Rs // tr
    c1 = 1.0 / (1.0 - ADAM_B1 ** ADAM_STEP)
    c2 = 1.0 / (1.0 - ADAM_B2 ** ADAM_STEP)

    def body(*refs):
        w_ref, m_ref, v_ref = refs[:3]
        g_refs = refs[3:3 + ns]
        go_ref, d_ref, mo_ref, vo_ref = refs[3 + ns:]

        def update(g_ref):
            g = g_ref[0].astype(F32)
            for k in range(1, n):
                g = g + g_ref[k].astype(F32)
            mn = ADAM_B1 * m_ref[...] + (1.0 - ADAM_B1) * g
            vn = ADAM_B2 * v_ref[...] + (1.0 - ADAM_B2) * (g * g)
            d_ref[...] = -ADAM_LR * ((mn * c1) / (jnp.sqrt(vn * c2) + ADAM_EPS) + ADAM_WD * w_ref[...])
            go_ref[...] = g
            mo_ref[...] = mn
            vo_ref[...] = vn

        for s in range(ns):
            if ns == 1:
                update(g_refs[s])
            else:
                pl.when(pl.program_id(0) // nts == s)(functools.partial(update, g_refs[s]))

    def g_map(i, s):
        return (0, jnp.clip(i - s * nts, 0, nts - 1), 0)

    blk = pl.BlockSpec((tr, C), lambda i: (i, 0))
    return pl.pallas_call(
        body, grid=(R // tr,),
        in_specs=[blk, blk, blk] + [pl.BlockSpec((n, tr, C), functools.partial(g_map, s=s)) for s in range(ns)],
        out_specs=[blk] * 4, out_shape=[SDS((R, C), F32)] * 4,
        compiler_params=_params("parallel"), name=name,
    )(w, m, v, *g_parts)


ANY = pl.BlockSpec(memory_space=pl.ANY)
STAGE_BYTES = 2 * 1024 * 1024


def _stage_shape(shape, dtype):
    row_bytes = int(np.prod(shape[1:])) * jnp.dtype(dtype).itemsize
    rows = max(1, min(shape[0], STAGE_BYTES // row_bytes))
    while shape[0] % rows:
        rows -= 1
    return (rows,) + tuple(shape[1:])


def _staged_copy(frm, to, buf, sems):
    rows = buf.shape[0]
    for r0 in range(0, frm.shape[0], rows):
        cp = pltpu.make_async_copy(frm.at[pl.ds(r0, rows)], buf, sems.at[0])
        cp.start()
        cp.wait()
        cp = pltpu.make_async_copy(buf, to.at[pl.ds(r0, rows)], sems.at[1])
        cp.start()
        cp.wait()


def _all_gather(shards, out_shapes, views, *, name):
    n = len(shards)

    def body(*refs):
        ins, outs = refs[:n], refs[n:2 * n]
        send_sems, recv_sems, local_sems = refs[2 * n:2 * n + 3]
        bufs = refs[2 * n + 3:]
        x, y, c = lax.axis_index("x"), lax.axis_index("y"), lax.axis_index("c")
        me, sibling = (x, y, c), (x, y, 1 - c)
        chips = [(1 - x, y), (x, 1 - y), (1 - x, 1 - y)]

        def dev(p):
            return 4 * p[0] + 2 * p[1] + p[2]

        def copy(a, k, block, to, src=None):
            dst = views[a](outs[a], dev(block))
            return pltpu.make_async_remote_copy(
                src_ref=dst if src is None else src, dst_ref=dst,
                send_sem=send_sems.at[a, k], recv_sem=recv_sems.at[a, k],
                device_id=to, device_id_type=MESH)

        first, passed = [], []
        for a in range(n):
            first.append(copy(a, 0, me, sibling, src=ins[a]))
            first += [copy(a, 1 + j, me, (*chip, c), src=ins[a]) for j, chip in enumerate(chips)]
        for cp in first:
            cp.start()
        for a in range(n):
            _staged_copy(ins[a], views[a](outs[a], dev(me)), bufs[a], local_sems)
        for j, chip in enumerate(chips):
            for a in range(n):
                copy(a, 1 + j, (*chip, c), me).wait_recv()
                cp = copy(a, 4 + j, (*chip, c), sibling)
                cp.start()
                passed.append(cp)
        for a in range(n):
            copy(a, 0, sibling, me).wait_recv()
            for j, chip in enumerate(chips):
                copy(a, 4 + j, (*chip, 1 - c), me).wait_recv()
        for cp in first + passed:
            cp.wait_send()

    return pl.pallas_call(
        body, in_specs=[ANY] * n, out_specs=[ANY] * n, out_shape=list(out_shapes),
        scratch_shapes=[pltpu.SemaphoreType.DMA((n, 7)), pltpu.SemaphoreType.DMA((n, 7)),
                        pltpu.SemaphoreType.DMA((2,))]
        + [pltpu.VMEM(_stage_shape(s.shape, s.dtype), s.dtype) for s in shards],
        name=name,
    )(*shards)


HBM = pl.BlockSpec(memory_space=pltpu.HBM)
SEM = pl.BlockSpec(memory_space=pltpu.SEMAPHORE)
EFFECT = pltpu.SideEffectType.DATAFLOW_SIDE_EFFECTING


def _relations(x, y, c):
    for m in range(1, N_DEV):
        yield m, (1 - x if m & 4 else x, 1 - y if m & 2 else y, 1 - c if m & 1 else c)


def _dev_id(p):
    return 4 * p[0] + 2 * p[1] + p[2]


def _send_start(srcs, land_shapes, src_views, dst_views, *, name):
    n = len(srcs)

    def body(*refs):
        ins, lands = refs[:n], refs[n:2 * n]
        send_sems, recv_sems, token = refs[2 * n], refs[2 * n + 1], refs[-1]
        x, y, c = lax.axis_index("x"), lax.axis_index("y"), lax.axis_index("c")
        me = _dev_id((x, y, c))
        for m, p in _relations(x, y, c):
            for a in range(n):
                pltpu.make_async_remote_copy(
                    src_ref=src_views[a](ins[a], me, _dev_id(p), m), dst_ref=dst_views[a](lands[a], me, m),
                    send_sem=send_sems.at[a * (N_DEV - 1) + m - 1], recv_sem=recv_sems.at[a * (N_DEV - 1) + m - 1],
                    device_id=p, device_id_type=MESH).start()
        token[...] = jnp.zeros_like(token)

    lands = [pltpu.with_memory_space_constraint(lax.empty(s.shape, s.dtype), pltpu.HBM) for s in land_shapes]
    srcs = [pltpu.with_memory_space_constraint(v, pltpu.HBM) for v in srcs]
    res = pl.pallas_call(
        body, name=name,
        out_shape=[pltpu.SemaphoreType.DMA((n * (N_DEV - 1),)), pltpu.SemaphoreType.DMA((n * (N_DEV - 1),))]
        + [pltpu.HBM(v.shape, v.dtype) for v in srcs] + [pltpu.HBM(s.shape, s.dtype) for s in land_shapes]
        + [SDS((8, 128), F32)],
        in_specs=[HBM] * (2 * n), out_specs=[SEM, SEM] + [HBM] * (2 * n) + [pl.BlockSpec(memory_space=pltpu.VMEM)],
        input_output_aliases={i: 2 + i for i in range(2 * n)},
        compiler_params=pltpu.CompilerParams(has_side_effects=EFFECT),
    )(*srcs, *lands)
    return res[0], res[1], res[2:2 + n], res[2 + n:2 + 2 * n], res[-1]


def _send_wait(started, src_views, dst_views, own_views, own_shapes, after, *, name):
    send_sems, recv_sems, srcs, lands, _ = started
    n = len(srcs)

    def body(*refs):
        ins, lnd = refs[:n], refs[n:2 * n]
        send_sems, recv_sems = refs[2 * n], refs[2 * n + 1]
        got = refs[2 * n + 3 + n:2 * n + 3 + 2 * n]
        local_sems = refs[2 * n + 3 + 2 * n]
        bufs = refs[2 * n + 4 + 2 * n:]
        x, y, c = lax.axis_index("x"), lax.axis_index("y"), lax.axis_index("c")
        me = _dev_id((x, y, c))
        for m, p in _relations(x, y, c):
            for a in range(n):
                cp = pltpu.make_async_remote_copy(
                    src_ref=src_views[a](ins[a], me, _dev_id(p), m), dst_ref=dst_views[a](lnd[a], me, m),
                    send_sem=send_sems.at[a * (N_DEV - 1) + m - 1], recv_sem=recv_sems.at[a * (N_DEV - 1) + m - 1],
                    device_id=p, device_id_type=MESH)
                cp.wait_send()
                cp.wait_recv()
        for a in range(n):
            frm, to = own_views[a](ins[a], got[a], me)
            _staged_copy(frm, to, bufs[a], local_sems)

    res = pl.pallas_call(
        body, name=name,
        out_shape=[pltpu.HBM(v.shape, v.dtype) for v in srcs] + [pltpu.HBM(v.shape, v.dtype) for v in lands],
        in_specs=[HBM] * (2 * n) + [SEM, SEM, ANY], out_specs=[HBM] * (2 * n),
        input_output_aliases={i: i for i in range(2 * n)},
        scratch_shapes=[pltpu.SemaphoreType.DMA((2,))]
        + [pltpu.VMEM(_stage_shape(s, v.dtype), v.dtype) for s, v in zip(own_shapes, srcs)],
        compiler_params=pltpu.CompilerParams(has_side_effects=EFFECT),
    )(*srcs, *lands, send_sems, recv_sems, after)
    return res[n:]


def kernel(x, norm_gains, fox_w_in, fox_b_f, hgrn_w_in, hgrn_lb_logits, hgrn_onorm, w_out, final_gain, loss_target, m_norm_gains, m_fox_w_in, m_fox_b_f, m_hgrn_w_in, m_hgrn_lb_logits, m_hgrn_onorm, m_w_out, m_final_gain, v_norm_gains, v_fox_w_in, v_fox_b_f, v_hgrn_w_in, v_hgrn_lb_logits, v_hgrn_onorm, v_w_out, v_final_gain):
    _, S, D = x.shape
    H = FOX_HEADS
    W = H * HEAD_DIM
    assert HGRN_HEADS == H and w_out.shape[2] == D
    cf = fox_w_in.shape[2]
    ch = hgrn_w_in.shape[2]
    ro = w_out.shape[1]
    co = hgrn_onorm.shape[1]
    assert N_DEV * cf == 4 * W + H and N_DEV * ch == 4 * W and N_DEV * ro == W and N_DEV * co == W
    x2 = x.reshape(S, D)
    tgt = loss_target.reshape(S, D)

    col = lambda n: (lambda r, i: r.at[:, pl.ds(pl.multiple_of(i * n, n), n)])
    row = lambda n: (lambda r, i: r.at[pl.ds(pl.multiple_of(i * n, n), n), :])
    late_views = [col(ch), row(ro), col(co)]
    late = _send_start(
        [hgrn_w_in[0].astype(BF16), w_out[1].astype(BF16), hgrn_onorm],
        [SDS((D, 4 * W), BF16), SDS((W, D), BF16), SDS((1, W), F32)],
        [lambda r, me, p, m: r] * 3, [lambda r, me, m, v=v: v(r, me) for v in late_views],
        name="gather_layer1_start")
    ng0 = norm_gains[0:1] + late[4][0:1, 0:1]

    wf_g, wo0 = _all_gather(
        [fox_w_in[0].astype(BF16), w_out[0].astype(BF16)], [SDS((N_DEV, D, cf), BF16), SDS((W, D), BF16)],
        [lambda r, p: r.at[p], row(ro)], name="gather_layer0")
    wf = jnp.transpose(wf_g, (1, 0, 2)).reshape(D, N_DEV * cf)
    wf_main = jnp.concatenate([wf[:, :3 * W], wf[:, 3 * W + H:]], axis=1)
    wfl_t = wf[:, 3 * W:3 * W + H].T

    h0 = _rms_fwd(x2, ng0, name="rms0_fwd")
    p0 = _mm_nn([h0], wf_main, BF16, scale_cols=(W, LOG2E * HEAD_DIM ** -0.5), name="fox_in_proj")
    fl_t = _mm_nt_rows(wfl_t, h0, name="fox_forget_proj")
    b_col = fox_b_f.reshape(H, 1)
    kaug = _fox_key_aug(*_fox_gate_fwd(fl_t, b_col, name="fox_gate_fwd"))
    o0, y0, qaug = _fox_fwd(p0, kaug, H=H, name="fox_attn_fwd")
    x1 = _mm_nn([y0], wo0, F32, residual=x2, name="fox_out_proj")

    wh, wo1, onorm = _send_wait(
        late, [lambda r, me, p, m: r] * 3, [lambda r, me, m, v=v: v(r, me) for v in late_views],
        [lambda src, land, me, v=v: (src, v(land, me)) for v in late_views], [(D, ch), (ro, D), (1, co)],
        x1[0:8], name="gather_layer1_wait")
    lb = _lb_fwd(hgrn_lb_logits, name="hgrn_lower_bound")
    h1 = _rms_fwd(x1, norm_gains[1:2], name="rms1_fwd")
    p1 = _mm_nn([h1], wh, BF16, b_cols=[(0, W), (2 * W, 4 * W)], name="hgrn_in_proj")
    f1 = _mm_nn([h1], wh, F32, b_cols=[(W, 2 * W)], name="hgrn_forget_proj")
    o1, y1, states = _hgrn_fwd(p1, f1, lb, onorm, H=H, name="hgrn_fwd")
    xo = _mm_nn([y1], wo1, F32, residual=x1, name="hgrn_out_proj")

    dx2, dx2b, loss_part, dgf = _loss_head(xo, final_gain.reshape(1, D), tgt, name="loss_head")
    loss = lax.psum(jnp.sum(loss_part), ("x", "y", "c"))

    dy1 = _mm_nn([dx2b], wo1, BF16, b_t=True, name="hgrn_out_proj_dx")
    dwo1 = _mm_tn(y1, [dx2b], BF16, name="hgrn_out_proj_dw")
    do1, dg1, donorm = _hgrn_post_bwd(dy1, o1, p1, onorm, H=H, name="hgrn_post_bwd")
    dq1, df1, di1, dlb = _hgrn_bwd(p1, f1, lb, do1, states, H=H, name="hgrn_bwd")
    segs1 = [dq1, df1, di1, dg1]
    dh1 = _mm_nn(segs1, wh, BF16, b_t=True, tn=D, name="hgrn_in_proj_dx")
    dwh = _mm_tn(h1, segs1, BF16, name="hgrn_in_proj_dw")
    part_views = [col(ch), row(ro)]
    slot = lambda r, me, m: r.at[m]
    ex1 = _send_start([dwh, dwo1], [SDS((N_DEV, D, ch), BF16), SDS((N_DEV, ro, D), BF16)],
                      [lambda r, me, p, m, v=v: v(r, p) for v in part_views], [slot] * 2,
                      name="exchange_layer1_start")
    ng1 = norm_gains[1:2] + ex1[4][0:1, 0:1]
    dx1, dx1b, dng1 = _rms_bwd(x1, ng1, dh1, dx2, name="rms1_bwd")

    dy0 = _mm_nn([dx1b], wo0, BF16, b_t=True, name="fox_out_proj_dx")
    dwo0 = _mm_tn(y0, [dx1b], BF16, name="fox_out_proj_dw")
    do0, dg0, doaug = _fox_post_bwd(dy0, o0, p0, H=H, name="fox_post_bwd")
    dq0, dc_row, dk0, dv0, dc_key = _fox_bwd(p0, kaug, qaug, do0, doaug, H=H, name="fox_attn_bwd")
    dfl_t, dbf = _fox_gate_bwd(dc_row.reshape(H, S), dc_key.reshape(H, S), fl_t, b_col, name="fox_gate_bwd")
    dfl_tb = dfl_t.astype(BF16)
    dwfl_t = _mm_nn([dfl_tb], h0, BF16, name="fox_forget_proj_dw")
    segs0 = [dq0, dk0, dv0, dg0]
    dwf_main = _mm_tn(h0, segs0, BF16, name="fox_in_proj_dw")
    dwf = jnp.concatenate([dwf_main[:, :3 * W], dwfl_t.T, dwf_main[:, 3 * W:]], axis=1)
    dwf_blocks = jnp.transpose(dwf.reshape(D, N_DEV, cf), (1, 0, 2))
    ex0 = _send_start([dwf_blocks, dwo0], [SDS((N_DEV, D, cf), BF16), SDS((N_DEV, ro, D), BF16)],
                      [lambda r, me, p, m: r.at[p], lambda r, me, p, m: row(ro)(r, p)], [slot] * 2,
                      name="exchange_layer0_start")
    wfl_t0 = wfl_t + ex0[4][0:1, 0:1].astype(BF16)
    dh0_f = _mm_nn([dfl_tb.T], wfl_t0, BF16, name="fox_forget_proj_dx")
    dh0 = _mm_nn(segs0, wf_main, BF16, residual=dh0_f, b_t=True, tn=D, name="fox_in_proj_dx")
    grad_x, _, dng0 = _rms_bwd(x2, norm_gains[0:1], dh0, dx1, name="rms0_bwd")

    own1 = [lambda src, land, me, v=v: (v(src, me), land.at[0]) for v in part_views]
    rh, ro1 = _send_wait(ex1, [lambda r, me, p, m, v=v: v(r, p) for v in part_views], [slot] * 2, own1,
                         [(D, ch), (ro, D)], dng0, name="exchange_layer1_wait")
    rf, ro0 = _send_wait(ex0, [lambda r, me, p, m: r.at[p], lambda r, me, p, m: row(ro)(r, p)], [slot] * 2,
                         [lambda src, land, me: (src.at[me], land.at[0]),
                          lambda src, land, me: (row(ro)(src, me), land.at[0])],
                         [(D, cf), (ro, D)], dng0, name="exchange_layer0_wait")

    pad = lambda a: jnp.pad(a, ((0, 0), (0, W - a.shape[1])))
    stats = jnp.concatenate([dng0, dng1, dlb, dgf, pad(dbf.reshape(1, H)), donorm,
                             jnp.zeros((2, W), F32)], axis=0)
    assert D == W
    (stats_all,) = _all_gather([stats], [SDS((N_DEV, STAT_ROWS, W), F32)], [lambda r, p: r.at[p]],
                               name="gather_small_grads")
    g_small = _stats_reduce(stats_all, hgrn_lb_logits, name="reduce_small_grads")
    me = 4 * lax.axis_index("x") + 2 * lax.axis_index("y") + lax.axis_index("c")
    g_onorm = lax.dynamic_slice_in_dim(g_small[6:7], me * co, co, axis=1)

    def upd(w, m, v, parts, name):
        shp = w.shape
        r2 = (-1, shp[-1])
        g, d, mn, vn = _adamw(w.reshape(r2), m.reshape(r2), v.reshape(r2), parts, name=name)
        return g.reshape(shp), d.reshape(shp), mn.reshape(shp), vn.reshape(shp)

    res = {
        "norm_gains": upd(norm_gains, m_norm_gains, v_norm_gains, [g_small[None, 0:2]], "adamw_norm_gains"),
        "fox_w_in": upd(fox_w_in, m_fox_w_in, v_fox_w_in, [rf], "adamw_fox_w_in"),
        "fox_b_f": upd(fox_b_f, m_fox_b_f, v_fox_b_f, [g_small[None, 5:6, :H]], "adamw_fox_b_f"),
        "hgrn_w_in": upd(hgrn_w_in, m_hgrn_w_in, v_hgrn_w_in, [rh], "adamw_hgrn_w_in"),
        "hgrn_lb_logits": upd(hgrn_lb_logits, m_hgrn_lb_logits, v_hgrn_lb_logits, [g_small[None, 2:4]],
                              "adamw_hgrn_lb_logits"),
        "hgrn_onorm": upd(hgrn_onorm, m_hgrn_onorm, v_hgrn_onorm, [g_onorm[None]], "adamw_hgrn_onorm"),
        "w_out": upd(w_out, m_w_out, v_w_out, [ro0, ro1], "adamw_w_out"),
        "final_gain": upd(final_gain.reshape(1, D), m_final_gain.reshape(1, D), v_final_gain.reshape(1, D),
                          [g_small[None, 4:5]], "adamw_final_gain"),
    }
    order = ["norm_gains", "fox_w_in", "fox_b_f", "hgrn_w_in", "hgrn_lb_logits", "hgrn_onorm", "w_out", "final_gain"]
    fix = lambda n, a: a.reshape(D) if n == "final_gain" else a
    outs = [loss, grad_x.reshape(1, S, D)]
    for k in range(4):
        outs += [fix(n, res[n][k]) for n in order]
    return tuple(outs)
```

```python
import functools

import numpy as np
import jax
import jax.numpy as jnp
from jax import lax
from jax.experimental import pallas as pl
from jax.experimental.pallas import tpu as pltpu

F32 = jnp.float32
BF16 = jnp.bfloat16
SDS = jax.ShapeDtypeStruct
MESH = pl.DeviceIdType.MESH

EPS = 1e-6
ADAM_LR, ADAM_B1, ADAM_B2, ADAM_EPS, ADAM_WD, ADAM_STEP = 0.001, 0.9, 0.999, 1e-08, 0.01, 10

N_DEV = 8
FOX_HEADS = 16
HGRN_HEADS = 16
HEAD_DIM = 128
HGRN_CHUNK = 128
HGRN_LEAF = 16
HGRN_HEADS_PER_STEP = 4
EXP_CLAMP = 85.0
ATT_BLOCK = 512
ATT_HEADS_PER_STEP = 4
ATT_BWD_HEADS_PER_STEP = 2
POST_HEADS_PER_STEP = 4
NEG = -1e30
LOG2E = 1.4426950408889634
LN2 = 0.6931471805599453

VMEM_LIMIT_V7X = 56 * 1024 * 1024


def _params(*sem):
    return pltpu.CompilerParams(dimension_semantics=sem, vmem_limit_bytes=VMEM_LIMIT_V7X)


def _silu(x):
    return x * jax.nn.sigmoid(x)


def _dsilu(x):
    s = jax.nn.sigmoid(x)
    return s * (1.0 + x * (1.0 - s))


def _dot(a, b):
    return jnp.dot(a, b, preferred_element_type=F32)


def _dot_nt(a, b):
    return lax.dot_general(a, b, (((1,), (1,)), ((), ())), preferred_element_type=F32)


def _dot_tn(a, b):
    return lax.dot_general(a, b, (((0,), (0,)), ((), ())), preferred_element_type=F32)


def _mm_nn(a_list, b, out_dtype, *, name, residual=None, scale_cols=None, b_t=False, b_cols=None,
           tm=1024, tn=1024, tk=2048):
    ns = len(a_list)
    M, Ks = a_list[0].shape
    K, N = (b.shape[1], b.shape[0]) if b_t else b.shape
    if b_cols is None:
        b_cols = [(0, N)]
    else:
        assert not b_t
        N = sum(e - s for s, e in b_cols)
    dot = _dot_nt if b_t else _dot
    assert K == ns * Ks and all(a.shape == (M, Ks) for a in a_list)
    if ns > 1:
        tk = tk // 2
    tm, tn, tk = min(tm, M), min(tn, N), min(tk, Ks)
    assert M % tm == 0 and N % tn == 0 and Ks % tk == 0
    assert scale_cols is None or scale_cols[0] % tn == 0
    assert all(s % tn == 0 and e % tn == 0 for s, e in b_cols)
    nks = Ks // tk
    nk = ns * nks
    has_res = residual is not None

    def body(*refs):
        a_refs, b_ref = refs[:ns], refs[ns]
        res_ref = refs[ns + 1] if has_res else None
        o_ref = refs[ns + 1 + has_res]

        def finish(r):
            if has_res:
                r = r + res_ref[...].astype(F32)
            if scale_cols is not None:
                r = r * jnp.where(pl.program_id(1) < scale_cols[0] // tn, scale_cols[1], 1.0)
            o_ref[...] = r.astype(out_dtype)

        if nk == 1:
            finish(dot(a_refs[0][...], b_ref[...]))
            return
        acc_ref = refs[ns + 2 + has_res]
        k = pl.program_id(2)

        @pl.when(k == 0)
        def _():
            acc_ref[...] = jnp.zeros_like(acc_ref)

        for s in range(ns):
            def step(s=s):
                acc_ref[...] += dot(a_refs[s][...], b_ref[...])

            if ns == 1:
                step()
            else:
                pl.when(k // nks == s)(step)

        @pl.when(k == nk - 1)
        def _():
            finish(acc_ref[...])

    def a_map(i, j, k, s):
        return (i, jnp.clip(k - s * nks, 0, nks - 1))

    in_specs = [pl.BlockSpec((tm, tk), functools.partial(a_map, s=s)) for s in range(ns)]
    def b_col(j):
        src = j + b_cols[0][0] // tn
        for (_, e0), (s1, _) in zip(b_cols[:-1], b_cols[1:]):
            src = src + jnp.where(src >= e0 // tn, (s1 - e0) // tn, 0)
        return src

    if b_t:
        in_specs.append(pl.BlockSpec((tn, tk), lambda i, j, k: (j, k)))
    else:
        in_specs.append(pl.BlockSpec((tk, tn), lambda i, j, k: (k, b_col(j))))
    args = list(a_list) + [b]
    if has_res:
        in_specs.append(pl.BlockSpec((tm, tn), lambda i, j, k: (i, j)))
        args.append(residual)
    return pl.pallas_call(
        body, grid=(M // tm, N // tn, nk), in_specs=in_specs,
        out_specs=pl.BlockSpec((tm, tn), lambda i, j, k: (i, j)),
        out_shape=SDS((M, N), out_dtype),
        scratch_shapes=[] if nk == 1 else [pltpu.VMEM((tm, tn), F32)],
        compiler_params=_params("parallel", "parallel", "arbitrary"), name=name,
    )(*args)


def _mm_tn(a, b_list, out_dtype, *, name, tm=2048, tn=1024, tk=512):
    ns = len(b_list)
    S, M = a.shape
    Ns = b_list[0].shape[1]
    assert all(b.shape == (S, Ns) for b in b_list)
    tm, tn, tk = min(tm, M), min(tn, Ns), min(tk, S)
    assert M % tm == 0 and Ns % tn == 0 and S % tk == 0
    njs = Ns // tn
    nk = S // tk

    def body(*refs):
        a_ref, b_refs, o_ref, acc_ref = refs[0], refs[1:1 + ns], refs[1 + ns], refs[2 + ns]
        j, k = pl.program_id(1), pl.program_id(2)

        @pl.when(k == 0)
        def _():
            acc_ref[...] = jnp.zeros_like(acc_ref)

        for s in range(ns):
            def step(s=s):
                acc_ref[...] += _dot_tn(a_ref[...], b_refs[s][...])

            if ns == 1:
                step()
            else:
                pl.when(j // njs == s)(step)

        @pl.when(k == nk - 1)
        def _():
            o_ref[...] = acc_ref[...].astype(out_dtype)

    def b_map(i, j, k, s):
        return (k, jnp.clip(j - s * njs, 0, njs - 1))

    in_specs = [pl.BlockSpec((tk, tm), lambda i, j, k: (k, i))]
    in_specs += [pl.BlockSpec((tk, tn), functools.partial(b_map, s=s)) for s in range(ns)]
    return pl.pallas_call(
        body, grid=(M // tm, ns * njs, nk), in_specs=in_specs,
        out_specs=pl.BlockSpec((tm, tn), lambda i, j, k: (i, j)),
        out_shape=SDS((M, ns * Ns), out_dtype),
        scratch_shapes=[pltpu.VMEM((tm, tn), F32)],
        compiler_params=_params("parallel", "parallel", "arbitrary"), name=name,
    )(a, *b_list)


def _mm_nt_rows(w_t, h, *, name, tn=1024):
    R, K = w_t.shape
    S = h.shape[0]
    tn = min(tn, S)

    def body(w_ref, h_ref, o_ref):
        o_ref[...] = _dot_nt(w_ref[...], h_ref[...])

    return pl.pallas_call(
        body, grid=(S // tn,),
        in_specs=[pl.BlockSpec((R, K), lambda i: (0, 0)), pl.BlockSpec((tn, K), lambda i: (i, 0))],
        out_specs=pl.BlockSpec((R, tn), lambda i: (0, i)),
        out_shape=SDS((R, S), F32), compiler_params=_params("parallel"), name=name,
    )(w_t, h)


def _rms_fwd(x, gain, *, name, tm=512):
    S, D = x.shape
    tm = min(tm, S)

    def body(x_ref, g_ref, h_ref):
        xv = x_ref[...]
        r = lax.rsqrt(jnp.mean(xv * xv, axis=-1, keepdims=True) + EPS)
        h_ref[...] = ((xv * r) * g_ref[...]).astype(BF16)

    return pl.pallas_call(
        body, grid=(S // tm,),
        in_specs=[pl.BlockSpec((tm, D), lambda i: (i, 0)), pl.BlockSpec((1, D), lambda i: (0, 0))],
        out_specs=pl.BlockSpec((tm, D), lambda i: (i, 0)),
        out_shape=SDS((S, D), BF16), compiler_params=_params("parallel"), name=name,
    )(x, gain)


def _rms_bwd(x, gain, dh, dres, *, name, tm=256):
    S, D = x.shape
    tm = min(tm, S)

    def body(x_ref, g_ref, dh_ref, dres_ref, dx_ref, dxb_ref, dg_ref):
        @pl.when(pl.program_id(0) == 0)
        def _():
            dg_ref[...] = jnp.zeros_like(dg_ref)

        xv = x_ref[...]
        r = lax.rsqrt(jnp.mean(xv * xv, axis=-1, keepdims=True) + EPS)
        xh = xv * r
        dhv = dh_ref[...].astype(F32)
        dg_ref[...] += jnp.sum(dhv * xh, axis=0, keepdims=True)
        dxh = dhv * g_ref[...]
        dx = r * (dxh - xh * jnp.mean(dxh * xh, axis=-1, keepdims=True)) + dres_ref[...]
        dx_ref[...] = dx
        dxb_ref[...] = dx.astype(BF16)

    row = pl.BlockSpec((tm, D), lambda i: (i, 0))
    vec = pl.BlockSpec((1, D), lambda i: (0, 0))
    return pl.pallas_call(
        body, grid=(S // tm,), in_specs=[row, vec, row, row], out_specs=[row, row, vec],
        out_shape=[SDS((S, D), F32), SDS((S, D), BF16), SDS((1, D), F32)],
        compiler_params=_params("arbitrary"), name=name,
    )(x, gain, dh, dres)


def _loss_head(x, gain, target, *, name, tm=256):
    S, D = x.shape
    tm = min(tm, S)
    assert tm % 8 == 0 and D % 128 == 0

    def body(x_ref, g_ref, t_ref, dx_ref, dxb_ref, loss_ref, dg_ref):
        @pl.when(pl.program_id(0) == 0)
        def _():
            dg_ref[...] = jnp.zeros_like(dg_ref)
            loss_ref[...] = jnp.zeros_like(loss_ref)

        xv = x_ref[...]
        g = g_ref[...]
        r = lax.rsqrt(jnp.mean(xv * xv, axis=-1, keepdims=True) + EPS)
        xh = xv * r
        err = xh * g - t_ref[...]
        e2 = (err * err).reshape(tm // 8, 8, D).sum(axis=0)
        part = e2[:, 0:128]
        for k in range(1, D // 128):
            part = part + e2[:, k * 128:(k + 1) * 128]
        loss_ref[...] += part * (0.5 / D)
        dy = err * (1.0 / D)
        dg_ref[...] += jnp.sum(dy * xh, axis=0, keepdims=True)
        dxh = dy * g
        dx = r * (dxh - xh * jnp.mean(dxh * xh, axis=-1, keepdims=True))
        dx_ref[...] = dx
        dxb_ref[...] = dx.astype(BF16)

    row = pl.BlockSpec((tm, D), lambda i: (i, 0))
    vec = pl.BlockSpec((1, D), lambda i: (0, 0))
    return pl.pallas_call(
        body, grid=(S // tm,), in_specs=[row, vec, row],
        out_specs=[row, row, pl.BlockSpec((8, 128), lambda i: (0, 0)), vec],
        out_shape=[SDS((S, D), F32), SDS((S, D), BF16), SDS((8, 128), F32), SDS((1, D), F32)],
        compiler_params=_params("arbitrary"), name=name,
    )(x, gain, target)


def _split3(x):
    hi = x.astype(BF16)
    r1 = x - hi.astype(F32)
    mid = r1.astype(BF16)
    lo = (r1 - mid.astype(F32)).astype(BF16)
    return hi, mid, lo


def _split2(x):
    hi = x.astype(BF16)
    lo = (x - hi.astype(F32)).astype(BF16)
    return hi, lo


def _fox_gate_fwd(fl_t, b_col, *, name):
    H, S = fl_t.shape
    L = 128
    tri = jnp.asarray(np.triu(np.ones((L, L), np.float32)), BF16)

    def body(fl_ref, b_ref, tri_ref, hi_ref, mid_ref, lo_ref, carry):
        @pl.when(pl.program_id(0) == 0)
        def _():
            carry[...] = jnp.zeros_like(carry)

        z = fl_ref[...] + b_ref[...]
        lf = jnp.minimum(z, 0.0) - jnp.log(1.0 + jnp.exp(-jnp.abs(z)))
        hi, mid, lo = _split3(lf)
        t = tri_ref[...]
        c = (_dot(hi, t) + _dot(mid, t)) + _dot(lo, t) + carry[...]
        carry[...] = c[:, L - 1:L]
        hi_ref[...], mid_ref[...], lo_ref[...] = _split3(c * (-LOG2E))

    blk = pl.BlockSpec((H, L), lambda i: (0, i))
    return pl.pallas_call(
        body, grid=(S // L,),
        in_specs=[blk, pl.BlockSpec((H, 1), lambda i: (0, 0)), pl.BlockSpec((L, L), lambda i: (0, 0))],
        out_specs=[blk] * 3, out_shape=[SDS((H, S), BF16)] * 3, scratch_shapes=[pltpu.VMEM((H, 1), F32)],
        compiler_params=_params("arbitrary"), name=name,
    )(fl_t, b_col, tri)


def _fox_gate_bwd(dc_row, dc_key, fl_t, b_col, *, name):
    H, S = fl_t.shape
    L = 128
    n = S // L
    tri = jnp.asarray(np.tril(np.ones((L, L), np.float32)), BF16)

    def body(dcr_ref, dck_ref, fl_ref, b_ref, tri_ref, dfl_ref, db_ref, carry):
        @pl.when(pl.program_id(0) == 0)
        def _():
            carry[...] = jnp.zeros_like(carry)
            db_ref[...] = jnp.zeros_like(db_ref)

        hi, mid, lo = _split3(dcr_ref[...] + dck_ref[...])
        t = tri_ref[...]
        dlf = (_dot(hi, t) + _dot(mid, t)) + _dot(lo, t) + carry[...]
        carry[...] = dlf[:, 0:1]
        z = fl_ref[...] + b_ref[...]
        dfl = dlf * jax.nn.sigmoid(-z)
        dfl_ref[...] = dfl
        db_ref[...] += jnp.sum(dfl, axis=1, keepdims=True)

    blk = pl.BlockSpec((H, L), lambda i: (0, n - 1 - i))
    col = pl.BlockSpec((H, 1), lambda i: (0, 0))
    return pl.pallas_call(
        body, grid=(n,), in_specs=[blk, blk, blk, col, pl.BlockSpec((L, L), lambda i: (0, 0))],
        out_specs=[blk, col], out_shape=[SDS((H, S), F32), SDS((H, 1), F32)],
        scratch_shapes=[pltpu.VMEM((H, 1), F32)], compiler_params=_params("arbitrary"), name=name,
    )(dc_row, dc_key, fl_t, b_col, tri)


AUG = HEAD_DIM


def _lane_select(cols, shape):
    lane = lax.broadcasted_iota(jnp.int32, shape, 1)
    out = jnp.zeros(shape, BF16)
    for k, c in reversed(list(enumerate(cols))):
        c = jnp.full(shape, c, BF16) if isinstance(c, (int, float)) else jnp.broadcast_to(c, shape).astype(BF16)
        out = jnp.where(lane == k, c, out)
    return out


def _fox_key_aug(b_hi, b_mid, b_lo):
    H, S = b_hi.shape
    ones = jnp.ones((H, S), BF16)
    ka = jnp.stack([b_hi, b_mid, b_lo, ones, ones, ones], axis=-1)
    ka = jnp.pad(ka, ((0, 0), (0, 0), (0, AUG - 6)))
    return jnp.transpose(ka, (1, 0, 2)).reshape(S, H * AUG)


def _fox_fwd(p0, kaug, *, H, name):
    S = p0.shape[0]
    T = min(ATT_BLOCK, S)
    nq = S // T
    dh = HEAD_DIM
    G = ATT_HEADS_PER_STEP
    assert H % G == 0

    def body(q_ref, k_ref, ka_ref, v_ref, g_ref, o_ref, y_ref, qa_ref, m_sc, acc_sc, p_sc, al_sc):
        i = pl.program_id(1)
        qaug = _lane_select([1.0, 1.0, 1.0], (T, AUG))
        ones = jnp.ones((T, dh), BF16)
        m_sc[...] = jnp.full_like(m_sc, NEG)
        acc_sc[...] = jnp.zeros_like(acc_sc)

        def step(j, before, masked):
            rows = pl.ds(pl.multiple_of(j * T, T), T)
            for g in range(G):
                hd = slice(g * dh, (g + 1) * dh)
                if not masked:
                    prev = pl.ds(pl.multiple_of(before * T, T), T)
                    vp = jnp.concatenate([v_ref[prev, hd], ones], axis=1)
                    acc_sc[g] = jnp.tile(al_sc[g], (1, 2)) * acc_sc[g] + _dot(p_sc[g], vp)
                q = jnp.concatenate([q_ref[:, hd], qaug], axis=1)
                kj = jnp.concatenate([k_ref[rows, hd], ka_ref[rows, hd]], axis=1)
                t = _dot_nt(q, kj)
                if masked:
                    row = lax.broadcasted_iota(jnp.int32, (T, T), 0)
                    col = lax.broadcasted_iota(jnp.int32, (T, T), 1)
                    t = jnp.where(row >= col, t, NEG)
                m_prev = m_sc[g]
                m_new = jnp.maximum(m_prev, jnp.max(t, axis=-1, keepdims=True))
                p_sc[g] = jnp.exp2(t - jnp.tile(m_new, (1, T // 128))).astype(BF16)
                al_sc[g] = jnp.exp2(m_prev - m_new)
                m_sc[g] = m_new

        step(i, None, True)

        def loop_body(j, carry):
            step(j, jnp.where(j == 0, i, j - 1), False)
            return carry

        lax.fori_loop(0, i, loop_body, 0)
        rows = pl.ds(pl.multiple_of(jnp.where(i == 0, 0, i - 1) * T, T), T)
        for g in range(G):
            hd = slice(g * dh, (g + 1) * dh)
            vp = jnp.concatenate([v_ref[rows, hd], ones], axis=1)
            acc = jnp.tile(al_sc[g], (1, 2)) * acc_sc[g] + _dot(p_sc[g], vp)
            l = acc[:, dh:]
            o = acc[:, :dh] / l
            o_ref[:, hd] = o
            y_ref[:, hd] = (o * _silu(g_ref[:, hd].astype(F32))).astype(BF16)
            hi, mid, lo = _split3(-(m_sc[g] + jnp.log2(l)))
            qa_ref[:, hd] = _lane_select([1.0, 1.0, 1.0, hi, mid, lo], (T, AUG))

    blk = lambda off: pl.BlockSpec((T, G * dh), lambda h, i: (i, off // G + h))
    full = lambda off: pl.BlockSpec((S, G * dh), lambda h, i: (0, off // G + h), pipeline_mode=pl.Buffered(1))
    return pl.pallas_call(
        body, grid=(H // G, nq),
        in_specs=[blk(0), full(H), full(0), full(2 * H), blk(3 * H)],
        out_specs=[blk(0), blk(0), blk(0)],
        out_shape=[SDS((S, H * dh), F32), SDS((S, H * dh), BF16), SDS((S, H * AUG), BF16)],
        scratch_shapes=[pltpu.VMEM((G, T, 128), F32), pltpu.VMEM((G, T, 2 * dh), F32),
                        pltpu.VMEM((G, T, T), BF16), pltpu.VMEM((G, T, 128), F32)],
        compiler_params=_params("parallel", "arbitrary"), name=name,
    )(p0, p0, kaug, p0, p0)


def _fox_post_bwd(dy, o, p0, *, H, name, tm=512):
    S = dy.shape[0]
    dh = HEAD_DIM
    tm = min(tm, S)
    G = POST_HEADS_PER_STEP
    assert H % G == 0

    def body(dy_ref, o_ref, g_ref, do_ref, dg_ref, da_ref):
        dyv = dy_ref[...].astype(F32)
        ov = o_ref[...]
        g = g_ref[...].astype(F32)
        do = (dyv * _silu(g)).astype(BF16)
        do_ref[...] = do
        dg_ref[...] = (dyv * ov * _dsilu(g)).astype(BF16)
        prod = do.astype(F32) * ov
        for k in range(G):
            hd = slice(k * dh, (k + 1) * dh)
            delta = jnp.sum(prod[:, hd], axis=-1, keepdims=True)
            hi, mid, lo = _split3(-jnp.broadcast_to(delta, (tm, AUG)))
            da_ref[:, hd] = _lane_select([hi, mid, lo], (tm, AUG))

    blk = pl.BlockSpec((tm, G * dh), lambda h, i: (i, h))
    return pl.pallas_call(
        body, grid=(H // G, S // tm),
        in_specs=[blk, blk, pl.BlockSpec((tm, G * dh), lambda h, i: (i, 3 * H // G + h))],
        out_specs=[blk, blk, blk],
        out_shape=[SDS((S, H * dh), BF16), SDS((S, H * dh), BF16), SDS((S, H * AUG), BF16)],
        compiler_params=_params("parallel", "parallel"), name=name,
    )(dy, o, p0)


def _fox_bwd(p0, kaug, qaug, do, doaug, *, H, name):
    S = p0.shape[0]
    T = min(ATT_BLOCK, S)
    nq = S // T
    dh = HEAD_DIM
    scale = dh ** -0.5
    G = ATT_BWD_HEADS_PER_STEP
    assert H % G == 0

    def body(q_ref, qa_ref, k_ref, ka_ref, v_ref, do_ref, da_ref, dq_ref, rs_ref, dk_ref, dv_ref, dc_ref,
             dq_sc, dk_sc, dv_sc, pt_sc, dst_sc):
        j = pl.program_id(1)
        vaug = _lane_select([1.0, 1.0, 1.0], (T, AUG))
        ones = jnp.ones((T, dh), BF16)

        @pl.when(j == 0)
        def _():
            dq_sc[...] = jnp.zeros_like(dq_sc)

        dk_sc[...] = jnp.zeros_like(dk_sc)
        dv_sc[...] = jnp.zeros_like(dv_sc)

        def apply(prev):
            for g in range(G):
                hd = slice(g * dh, (g + 1) * dh)
                dv_sc[g] += _dot(pt_sc[g], do_ref[prev, hd])
                dk_sc[g] += _dot(dst_sc[g], jnp.concatenate([q_ref[prev, hd], ones], axis=1))
                dq_sc[g, prev] += _dot_tn(dst_sc[g], jnp.concatenate([k_ref[:, hd], ones], axis=1))

        def step(i, masked):
            rows = pl.ds(pl.multiple_of(i * T, T), T)
            if not masked:
                apply(pl.ds(pl.multiple_of((i - 1) * T, T), T))
            for g in range(G):
                hd = slice(g * dh, (g + 1) * dh)
                k = jnp.concatenate([k_ref[:, hd], ka_ref[:, hd]], axis=1)
                v = jnp.concatenate([v_ref[:, hd], vaug], axis=1)
                pt = jnp.exp2(_dot_nt(k, jnp.concatenate([q_ref[rows, hd], qa_ref[rows, hd]], axis=1)))
                if masked:
                    row = lax.broadcasted_iota(jnp.int32, (T, T), 0)
                    col = lax.broadcasted_iota(jnp.int32, (T, T), 1)
                    pt = jnp.where(col >= row, pt, 0.0)
                dst = pt * _dot_nt(v, jnp.concatenate([do_ref[rows, hd], da_ref[rows, hd]], axis=1))
                pt_sc[g] = pt.astype(BF16)
                dst_sc[g] = dst.astype(BF16)

        step(j, True)

        def loop_body(i, carry):
            step(i, False)
            return carry

        lax.fori_loop(j + 1, nq, loop_body, 0)
        apply(pl.ds((nq - 1) * T, T))
        for g in range(G):
            hd = slice(g * dh, (g + 1) * dh)
            dk_ref[:, hd] = (dk_sc[g, :, :dh] * LN2).astype(BF16)
            dv_ref[:, hd] = dv_sc[g].astype(BF16)
            dc_ref[g] = -jnp.transpose(dk_sc[g, :, dh:])[0:1]

        @pl.when(j == nq - 1)
        def _():
            for g in range(G):
                dq_ref[:, g * dh:(g + 1) * dh] = (dq_sc[g, :, :dh] * scale).astype(BF16)
                for i in range(nq):
                    rs_ref[g, :, i * T:(i + 1) * T] = jnp.transpose(dq_sc[g, i * T:(i + 1) * T, dh:])[0:1]

    blk = lambda off: pl.BlockSpec((T, G * dh), lambda h, j: (j, off // G + h))
    full = lambda off: pl.BlockSpec((S, G * dh), lambda h, j: (0, off // G + h))
    once = lambda off: pl.BlockSpec((S, G * dh), lambda h, j: (0, off // G + h), pipeline_mode=pl.Buffered(1))
    rowv = pl.BlockSpec((G, 1, T), lambda h, j: (h, 0, j))
    return pl.pallas_call(
        body, grid=(H // G, nq),
        in_specs=[once(0), once(0), blk(H), blk(0), blk(2 * H), once(0), once(0)],
        out_specs=[full(0), pl.BlockSpec((G, 1, S), lambda h, j: (h, 0, 0)), blk(0), blk(0), rowv],
        out_shape=[SDS((S, H * dh), BF16), SDS((H, 1, S), F32), SDS((S, H * dh), BF16), SDS((S, H * dh), BF16),
                   SDS((H, 1, S), F32)],
        scratch_shapes=[pltpu.VMEM((G, S, 2 * dh), F32), pltpu.VMEM((G, T, 2 * dh), F32), pltpu.VMEM((G, T, dh), F32),
                        pltpu.VMEM((G, T, T), BF16), pltpu.VMEM((G, T, T), BF16)],
        compiler_params=_params("parallel", "arbitrary"), name=name,
    )(p0, qaug, p0, kaug, p0, do, doaug)


def _hgrn_levels(C, leaf):
    levels = []
    h = C // 2
    while h >= leaf:
        levels.append(h)
        h //= 2
    return levels


def _hgrn_sum_matrix(C, leaf):
    t = np.arange(C)[:, None]
    u = np.arange(C)[None, :]
    mats = [(u <= t), (u > t)]
    for h in _hgrn_levels(C, leaf):
        start = (t // (2 * h)) * (2 * h)
        mid = start + h - 1
        second = t > mid
        m = np.where(second, (u > mid) & (u <= t), (u > t) & (u <= mid))
        mats.append(m)
    lstart = (t // leaf) * leaf
    mats.append((u >= lstart) & (u <= t))
    return np.concatenate([m.astype(np.float32) for m in mats], axis=0)


def _hgrn_chunk_terms(qr, fz, lb, msum, C, leaf):
    levels = _hgrn_levels(C, leaf)
    sq = _silu(qr)
    sp = 1.0 / (1.0 + jnp.exp(-fz))
    sn = 1.0 / (1.0 + jnp.exp(fz))
    f = lb + (1.0 - lb) * sp
    lf = jnp.log(f)
    k = (1.0 - lb) * sn
    hi, lo = _split2(lf)
    dsum = _dot(msum, hi) + _dot(msum, lo)
    b = dsum[0:C]
    kdec = dsum[C:2 * C]
    rowi = lax.broadcasted_iota(jnp.int32, (C, 1), 0)
    lev = []
    for n, h in enumerate(levels):
        e = jnp.exp(dsum[(2 + n) * C:(3 + n) * C])
        selq = jnp.where((rowi % (2 * h)) >= h, 1.0, 0.0)
        qm = (sq * e * selq).astype(BF16)
        km = (k * e * (1.0 - selq)).astype(BF16)
        lev.append((h, e, selq, qm, km))
    dleaf = dsum[(2 + len(levels)) * C:(3 + len(levels)) * C]
    eq = jnp.exp(dleaf)
    ek = jnp.exp(jnp.minimum(-dleaf, EXP_CLAMP))
    return dict(sq=sq, sp=sp, sn=sn, f=f, k=k, b=b, kdec=kdec, lev=lev, eq=eq, ek=ek,
                ql=(sq * eq).astype(BF16), kl=(k * ek).astype(BF16),
                qs=(sq * jnp.exp(b)).astype(BF16), ke=(k * jnp.exp(kdec)).astype(BF16),
                e_c=jnp.exp(b[C - 1:C, :]))


def _hgrn_masks(C, leaf, transposed):
    a = lax.broadcasted_iota(jnp.int32, (C, C), 0)
    bb = lax.broadcasted_iota(jnp.int32, (C, C), 1)
    t, s = (bb, a) if transposed else (a, bb)
    lev = [None if 2 * h == C else (t // (2 * h)) == (s // (2 * h)) for h in _hgrn_levels(C, leaf)]
    if leaf == C:
        leafm = s <= t
    else:
        leafm = ((t // leaf) == (s // leaf)) & (s <= t)
    return lev, leafm


def _hgrn_fwd(p1, f1, lb, onorm, *, H, name, tb=512):
    S = p1.shape[0]
    dk = HEAD_DIM
    C = min(HGRN_CHUNK, S)
    leaf = min(HGRN_LEAF, C)
    tb = min(tb, S)
    nc = tb // C
    G = HGRN_HEADS_PER_STEP
    assert H % G == 0
    msum = jnp.asarray(_hgrn_sum_matrix(C, leaf), BF16)

    def body(q_ref, f_ref, v_ref, g_ref, lb_ref, on_ref, ms_ref, o_ref, y_ref, st_ref, st_sc):
        @pl.when(pl.program_id(1) == 0)
        def _():
            st_sc[...] = jnp.zeros_like(st_sc)

        msv = ms_ref[...]
        lmask, leafm = _hgrn_masks(C, leaf, False)

        def chunk(n, carry):
            rows = pl.ds(pl.multiple_of(n * C, C), C)
            for g in range(G):
                hd = slice(g * dk, (g + 1) * dk)
                tm = _hgrn_chunk_terms(q_ref[rows, hd].astype(F32), f_ref[rows, hd], lb_ref[:, hd], msv, C, leaf)
                v = v_ref[rows, hd]
                st = st_sc[g]
                st_ref[g, n] = st
                a = jnp.where(leafm, _dot_nt(tm["ql"], tm["kl"]), 0.0)
                for (h, e, selq, qm, km), m in zip(tm["lev"], lmask):
                    al = _dot_nt(qm, km)
                    a = a + (al if m is None else jnp.where(m, al, 0.0))
                o = _dot_nt(tm["qs"], st.astype(BF16)) + _dot(a.astype(BF16), v)
                st_sc[g] = st * tm["e_c"] + _dot(v.T, tm["ke"])
                o_ref[rows, hd] = o
                rn = lax.rsqrt(jnp.mean(o * o, axis=-1, keepdims=True) + EPS)
                y = ((o * rn) * on_ref[:, hd]) * _silu(g_ref[rows, hd].astype(F32))
                y_ref[rows, hd] = y.astype(BF16)
            return carry

        lax.fori_loop(0, nc, chunk, 0)

    blk = lambda off: pl.BlockSpec((tb, G * dk), lambda h, i: (i, off // G + h))
    vec = pl.BlockSpec((1, G * dk), lambda h, i: (0, h))
    return pl.pallas_call(
        body, grid=(H // G, S // tb),
        in_specs=[blk(0), blk(0), blk(H), blk(2 * H), vec, vec,
                  pl.BlockSpec(msum.shape, lambda h, i: (0, 0))],
        out_specs=[blk(0), blk(0), pl.BlockSpec((G, nc, dk, dk), lambda h, i: (h, i, 0, 0))],
        out_shape=[SDS((S, H * dk), F32), SDS((S, H * dk), BF16), SDS((H, S // C, dk, dk), F32)],
        scratch_shapes=[pltpu.VMEM((G, dk, dk), F32)],
        compiler_params=_params("parallel", "arbitrary"), name=name,
    )(p1, f1, p1, p1, lb, onorm, msum)


def _hgrn_post_bwd(dy, o, p1, onorm, *, H, name, tm=512):
    S = dy.shape[0]
    dk = HEAD_DIM
    tm = min(tm, S)
    G = POST_HEADS_PER_STEP
    assert H % G == 0

    def body(dy_ref, o_ref, g_ref, on_ref, do_ref, dg_ref, don_ref):
        @pl.when(pl.program_id(1) == 0)
        def _():
            don_ref[...] = jnp.zeros_like(don_ref)

        for k in range(G):
            hd = slice(k * dk, (k + 1) * dk)
            dyv = dy_ref[:, hd].astype(F32)
            ov = o_ref[:, hd]
            g = g_ref[:, hd].astype(F32)
            onv = on_ref[:, hd]
            rn = lax.rsqrt(jnp.mean(ov * ov, axis=-1, keepdims=True) + EPS)
            oh = ov * rn
            dn = dyv * _silu(g)
            dg_ref[:, hd] = (dyv * (oh * onv) * _dsilu(g)).astype(BF16)
            don_ref[:, hd] += jnp.sum(dn * oh, axis=0, keepdims=True)
            doh = dn * onv
            do_ref[:, hd] = (rn * (doh - oh * jnp.mean(doh * oh, axis=-1, keepdims=True))).astype(BF16)

    blk = pl.BlockSpec((tm, G * dk), lambda h, i: (i, h))
    vec = pl.BlockSpec((1, G * dk), lambda h, i: (0, h))
    return pl.pallas_call(
        body, grid=(H // G, S // tm),
        in_specs=[blk, blk, pl.BlockSpec((tm, G * dk), lambda h, i: (i, 2 * H // G + h)), vec],
        out_specs=[blk, blk, vec],
        out_shape=[SDS((S, H * dk), BF16), SDS((S, H * dk), BF16), SDS((1, H * dk), F32)],
        compiler_params=_params("parallel", "arbitrary"), name=name,
    )(dy, o, p1, onorm)


def _hgrn_bwd(p1, f1, lb, do, states, *, H, name, tb=512):
    S = p1.shape[0]
    dk = HEAD_DIM
    C = min(HGRN_CHUNK, S)
    leaf = min(HGRN_LEAF, C)
    tb = min(tb, S)
    nc = tb // C
    nb = S // tb
    G = HGRN_HEADS_PER_STEP
    assert H % G == 0
    msum = jnp.asarray(_hgrn_sum_matrix(C, leaf), BF16)
    rtri = jnp.asarray(np.triu(np.ones((C, C), np.float32)), BF16)

    def body(q_ref, f_ref, v_ref, do_ref, st_ref, lb_ref, ms_ref, rt_ref,
             dq_ref, df_ref, dv_ref, dlb_ref, g_sc):
        @pl.when(pl.program_id(1) == 0)
        def _():
            g_sc[...] = jnp.zeros_like(g_sc)
            dlb_ref[...] = jnp.zeros_like(dlb_ref)

        msv = ms_ref[...]
        rtv = rt_ref[...]
        lmask, leafm = _hgrn_masks(C, leaf, False)
        lmask_t, leafm_t = _hgrn_masks(C, leaf, True)
        f32 = lambda z: z.astype(F32)

        def head_chunk(g, n):
            hd = slice(g * dk, (g + 1) * dk)
            rows = pl.ds(pl.multiple_of(n * C, C), C)
            lbv = lb_ref[:, hd]
            qr = q_ref[rows, hd].astype(F32)
            tm = _hgrn_chunk_terms(qr, f_ref[rows, hd], lbv, msv, C, leaf)
            v = v_ref[rows, hd]
            dov = do_ref[rows, hd]
            st0 = st_ref[g, n]
            gt = g_sc[g]
            gtb = gt.astype(BF16)
            da = _dot_nt(dov, v)
            da_t = _dot_nt(v, dov)

            dal = jnp.where(leafm, da, 0.0).astype(BF16)
            dal_t = jnp.where(leafm_t, da_t, 0.0).astype(BF16)
            dql = _dot(dal, tm["kl"])
            dkl = _dot(dal_t, tm["ql"])
            dsq = dql * tm["eq"]
            dkk = dkl * tm["ek"]
            xq = f32(tm["ql"]) * dql
            xk = f32(tm["kl"]) * dkl
            a_t = jnp.where(leafm_t, _dot_nt(tm["kl"], tm["ql"]), 0.0)
            for (h, e, selq, qm, km), m, m_t in zip(tm["lev"], lmask, lmask_t):
                dl = (da if m is None else jnp.where(m, da, 0.0)).astype(BF16)
                dl_t = (da_t if m_t is None else jnp.where(m_t, da_t, 0.0)).astype(BF16)
                dqm = _dot(dl, km)
                dkm = _dot(dl_t, qm)
                dsq = dsq + dqm * (e * selq)
                dkk = dkk + dkm * (e * (1.0 - selq))
                xq = xq + f32(qm) * dqm
                xk = xk + f32(km) * dkm
                al_t = _dot_nt(km, qm)
                a_t = a_t + (al_t if m_t is None else jnp.where(m_t, al_t, 0.0))
            dqs = _dot(dov, st0.astype(BF16))
            dke = _dot(v, gtb)
            dsq = dsq + dqs * jnp.exp(tm["b"])
            dkk = dkk + dke * jnp.exp(tm["kdec"])
            xq = xq + f32(tm["qs"]) * dqs
            xk = xk + f32(tm["ke"]) * dke
            dvv = _dot(a_t.astype(BF16), dov) + _dot_nt(tm["ke"], gtb)
            r_end = jnp.sum(f32(gtb) * _dot(v.T, tm["ke"]) + gt * (st0 * tm["e_c"]), axis=0, keepdims=True)
            g_sc[g] = gt * tm["e_c"] + _dot(dov.T, tm["qs"])
            xh, xm, xl = _split3(xq - xk)
            dlf = (_dot(rtv, xh) + _dot(rtv, xm)) + _dot(rtv, xl) + r_end
            dlf_f = dlf / tm["f"]
            dsp = (1.0 - lbv) * (dlf_f - dkk)
            df_ref[rows, hd] = (dsp * (tm["sp"] * tm["sn"])).astype(BF16)
            dq_ref[rows, hd] = (dsq * _dsilu(qr)).astype(BF16)
            dv_ref[rows, hd] = dvv.astype(BF16)
            dlb_ref[:, hd] += jnp.sum(dlf_f * tm["sn"] - dkk * tm["sn"], axis=0, keepdims=True)

        def chunk(nn, carry):
            for g in range(G):
                head_chunk(g, nc - 1 - nn)
            return carry

        lax.fori_loop(0, nc, chunk, 0)

    blk = lambda off: pl.BlockSpec((tb, G * dk), lambda h, i: (nb - 1 - i, off // G + h))
    vec = pl.BlockSpec((1, G * dk), lambda h, i: (0, h))
    return pl.pallas_call(
        body, grid=(H // G, nb),
        in_specs=[blk(0), blk(0), blk(H), blk(0),
                  pl.BlockSpec((G, nc, dk, dk), lambda h, i: (h, nb - 1 - i, 0, 0)), vec,
                  pl.BlockSpec(msum.shape, lambda h, i: (0, 0)), pl.BlockSpec((C, C), lambda h, i: (0, 0))],
        out_specs=[blk(0), blk(0), blk(0), vec],
        out_shape=[SDS((S, H * dk), BF16)] * 3 + [SDS((1, H * dk), F32)],
        scratch_shapes=[pltpu.VMEM((G, dk, dk), F32)],
        compiler_params=_params("parallel", "arbitrary"), name=name,
    )(p1, f1, p1, do, states, lb, msum, rtri)


def _lb_fwd(logits, *, name):
    W = logits.shape[1]

    def body(l_ref, lb_ref):
        l = l_ref[...]
        m = jnp.max(l, axis=0, keepdims=True)
        e = jnp.exp(l - m)
        p = e / jnp.sum(e, axis=0, keepdims=True)
        lb_ref[...] = (p[0:1] + p[1:2]) - p[0:1]

    return pl.pallas_call(body, out_shape=SDS((1, W), F32), name=name)(logits)


STAT_ROWS = 8


def _stats_reduce(stats_all, logits, *, name):
    W = logits.shape[1]

    def body(s_ref, l_ref, g_ref):
        tot = s_ref[0]
        for d in range(1, N_DEV):
            tot = tot + s_ref[d]
        l = l_ref[...]
        m = jnp.max(l, axis=0, keepdims=True)
        e = jnp.exp(l - m)
        p = e / jnp.sum(e, axis=0, keepdims=True)
        dlb = tot[2:3]
        dl0 = -(p[0:1] * p[1:2]) * dlb
        dl1 = (p[1:2] * (1.0 - p[1:2])) * dlb
        g_ref[0:2] = tot[0:2]
        g_ref[2:3] = dl0
        g_ref[3:4] = dl1
        g_ref[4:7] = tot[3:6]
        g_ref[7:8] = jnp.zeros((1, W), F32)

    return pl.pallas_call(body, out_shape=SDS((STAT_ROWS, W), F32), name=name)(stats_all, logits)


def _adamw(w, m, v, g_parts, *, name, tr=128):
    R, C = w.shape
    ns = len(g_parts)
    n, Rs = g_parts[0].shape[0], g_parts[0].shape[1]
    assert all(p.shape == (n, Rs, C) for p in g_parts) and ns * Rs == R
    tr = min(tr, Rs)
    assert Rs % tr == 0
    nts = Rs // tr
    c1 = 1.0 / (1.0 - ADAM_B1 ** ADAM_STEP)
    c2 = 1.0 / (1.0 - ADAM_B2 ** ADAM_STEP)

    def body(*refs):
        w_ref, m_ref, v_ref = refs[:3]
        g_refs = refs[3:3 + ns]
        go_ref, d_ref, mo_ref, vo_ref = refs[3 + ns:]

        def update(g_ref):
            g = g_ref[0].astype(F32)
            for k in range(1, n):
                g = g + g_ref[k].astype(F32)
            mn = ADAM_B1 * m_ref[...] + (1.0 - ADAM_B1) * g
            vn = ADAM_B2 * v_ref[...] + (1.0 - ADAM_B2) * (g * g)
            d_ref[...] = -ADAM_LR * ((mn * c1) / (jnp.sqrt(vn * c2) + ADAM_EPS) + ADAM_WD * w_ref[...])
            go_ref[...] = g
            mo_ref[...] = mn
            vo_ref[...] = vn

        for s in range(ns):
            if ns == 1:
                update(g_refs[s])
            else:
                pl.when(pl.program_id(0) // nts == s)(functools.partial(update, g_refs[s]))

    def g_map(i, s):
        return (0, jnp.clip(i - s * nts, 0, nts - 1), 0)

    blk = pl.BlockSpec((tr, C), lambda i: (i, 0))
    return pl.pallas_call(
        body, grid=(R // tr,),
        in_specs=[blk, blk, blk] + [pl.BlockSpec((n, tr, C), functools.partial(g_map, s=s)) for s in range(ns)],
        out_specs=[blk] * 4, out_shape=[SDS((R, C), F32)] * 4,
        compiler_params=_params("parallel"), name=name,
    )(w, m, v, *g_parts)


ANY = pl.BlockSpec(memory_space=pl.ANY)
STAGE_BYTES = 2 * 1024 * 1024


def _stage_shape(shape, dtype):
    row_bytes = int(np.prod(shape[1:])) * jnp.dtype(dtype).itemsize
    rows = max(1, min(shape[0], STAGE_BYTES // row_bytes))
    while shape[0] % rows:
        rows -= 1
    return (rows,) + tuple(shape[1:])


def _staged_copy(frm, to, buf, sems):
    rows = buf.shape[0]
    for r0 in range(0, frm.shape[0], rows):
        cp = pltpu.make_async_copy(frm.at[pl.ds(r0, rows)], buf, sems.at[0])
        cp.start()
        cp.wait()
        cp = pltpu.make_async_copy(buf, to.at[pl.ds(r0, rows)], sems.at[1])
        cp.start()
        cp.wait()


def _all_gather(shards, out_shapes, views, *, name):
    n = len(shards)

    def body(*refs):
        ins, outs = refs[:n], refs[n:2 * n]
        send_sems, recv_sems, local_sems = refs[2 * n:2 * n + 3]
        bufs = refs[2 * n + 3:]
        x, y, c = lax.axis_index("x"), lax.axis_index("y"), lax.axis_index("c")
        me, sibling = (x, y, c), (x, y, 1 - c)
        chips = [(1 - x, y), (x, 1 - y), (1 - x, 1 - y)]

        def dev(p):
            return 4 * p[0] + 2 * p[1] + p[2]

        def copy(a, k, block, to, src=None):
            dst = views[a](outs[a], dev(block))
            return pltpu.make_async_remote_copy(
                src_ref=dst if src is None else src, dst_ref=dst,
                send_sem=send_sems.at[a, k], recv_sem=recv_sems.at[a, k],
                device_id=to, device_id_type=MESH)

        first, passed = [], []
        for a in range(n):
            first.append(copy(a, 0, me, sibling, src=ins[a]))
            first += [copy(a, 1 + j, me, (*chip, c), src=ins[a]) for j, chip in enumerate(chips)]
        for cp in first:
            cp.start()
        for a in range(n):
            _staged_copy(ins[a], views[a](outs[a], dev(me)), bufs[a], local_sems)
        for j, chip in enumerate(chips):
            for a in range(n):
                copy(a, 1 + j, (*chip, c), me).wait_recv()
                cp = copy(a, 4 + j, (*chip, c), sibling)
                cp.start()
                passed.append(cp)
        for a in range(n):
            copy(a, 0, sibling, me).wait_recv()
            for j, chip in enumerate(chips):
                copy(a, 4 + j, (*chip, 1 - c), me).wait_recv()
        for cp in first + passed:
            cp.wait_send()

    return pl.pallas_call(
        body, in_specs=[ANY] * n, out_specs=[ANY] * n, out_shape=list(out_shapes),
        scratch_shapes=[pltpu.SemaphoreType.DMA((n, 7)), pltpu.SemaphoreType.DMA((n, 7)),
                        pltpu.SemaphoreType.DMA((2,))]
        + [pltpu.VMEM(_stage_shape(s.shape, s.dtype), s.dtype) for s in shards],
        name=name,
    )(*shards)


HBM = pl.BlockSpec(memory_space=pltpu.HBM)
SEM = pl.BlockSpec(memory_space=pltpu.SEMAPHORE)
EFFECT = pltpu.SideEffectType.DATAFLOW_SIDE_EFFECTING


def _relations(x, y, c):
    for m in range(1, N_DEV):
        yield m, (1 - x if m & 4 else x, 1 - y if m & 2 else y, 1 - c if m & 1 else c)


def _dev_id(p):
    return 4 * p[0] + 2 * p[1] + p[2]


def _send_start(srcs, land_shapes, src_views, dst_views, *, name):
    n = len(srcs)

    def body(*refs):
        ins, lands = refs[:n], refs[n:2 * n]
        send_sems, recv_sems, token = refs[2 * n], refs[2 * n + 1], refs[-1]
        x, y, c = lax.axis_index("x"), lax.axis_index("y"), lax.axis_index("c")
        me = _dev_id((x, y, c))
        for m, p in _relations(x, y, c):
            for a in range(n):
                pltpu.make_async_remote_copy(
                    src_ref=src_views[a](ins[a], me, _dev_id(p), m), dst_ref=dst_views[a](lands[a], me, m),
                    send_sem=send_sems.at[a * (N_DEV - 1) + m - 1], recv_sem=recv_sems.at[a * (N_DEV - 1) + m - 1],
                    device_id=p, device_id_type=MESH).start()
        token[...] = jnp.zeros_like(token)

    lands = [pltpu.with_memory_space_constraint(lax.empty(s.shape, s.dtype), pltpu.HBM) for s in land_shapes]
    srcs = [pltpu.with_memory_space_constraint(v, pltpu.HBM) for v in srcs]
    res = pl.pallas_call(
        body, name=name,
        out_shape=[pltpu.SemaphoreType.DMA((n * (N_DEV - 1),)), pltpu.SemaphoreType.DMA((n * (N_DEV - 1),))]
        + [pltpu.HBM(v.shape, v.dtype) for v in srcs] + [pltpu.HBM(s.shape, s.dtype) for s in land_shapes]
        + [SDS((8, 128), F32)],
        in_specs=[HBM] * (2 * n), out_specs=[SEM, SEM] + [HBM] * (2 * n) + [pl.BlockSpec(memory_space=pltpu.VMEM)],
        input_output_aliases={i: 2 + i for i in range(2 * n)},
        compiler_params=pltpu.CompilerParams(has_side_effects=EFFECT),
    )(*srcs, *lands)
    return res[0], res[1], res[2:2 + n], res[2 + n:2 + 2 * n], res[-1]


def _send_wait(started, src_views, dst_views, own_views, own_shapes, after, *, name):
    send_sems, recv_sems, srcs, lands, _ = started
    n = len(srcs)

    def body(*refs):
        ins, lnd = refs[:n], refs[n:2 * n]
        send_sems, recv_sems = refs[2 * n], refs[2 * n + 1]
        got = refs[2 * n + 3 + n:2 * n + 3 + 2 * n]
        local_sems = refs[2 * n + 3 + 2 * n]
        bufs = refs[2 * n + 4 + 2 * n:]
        x, y, c = lax.axis_index("x"), lax.axis_index("y"), lax.axis_index("c")
        me = _dev_id((x, y, c))
        for m, p in _relations(x, y, c):
            for a in range(n):
                cp = pltpu.make_async_remote_copy(
                    src_ref=src_views[a](ins[a], me, _dev_id(p), m), dst_ref=dst_views[a](lnd[a], me, m),
                    send_sem=send_sems.at[a * (N_DEV - 1) + m - 1], recv_sem=recv_sems.at[a * (N_DEV - 1) + m - 1],
                    device_id=p, device_id_type=MESH)
                cp.wait_send()
                cp.wait_recv()
        for a in range(n):
            frm, to = own_views[a](ins[a], got[a], me)
            _staged_copy(frm, to, bufs[a], local_sems)

    res = pl.pallas_call(
        body, name=name,
        out_shape=[pltpu.HBM(v.shape, v.dtype) for v in srcs] + [pltpu.HBM(v.shape, v.dtype) for v in lands],
        in_specs=[HBM] * (2 * n) + [SEM, SEM, ANY], out_specs=[HBM] * (2 * n),
        input_output_aliases={i: i for i in range(2 * n)},
        scratch_shapes=[pltpu.SemaphoreType.DMA((2,))]
        + [pltpu.VMEM(_stage_shape(s, v.dtype), v.dtype) for s, v in zip(own_shapes, srcs)],
        compiler_params=pltpu.CompilerParams(has_side_effects=EFFECT),
    )(*srcs, *lands, send_sems, recv_sems, after)
    return res[n:]


def kernel(x, norm_gains, fox_w_in, fox_b_f, hgrn_w_in, hgrn_lb_logits, hgrn_onorm, w_out, final_gain, loss_target, m_norm_gains, m_fox_w_in, m_fox_b_f, m_hgrn_w_in, m_hgrn_lb_logits, m_hgrn_onorm, m_w_out, m_final_gain, v_norm_gains, v_fox_w_in, v_fox_b_f, v_hgrn_w_in, v_hgrn_lb_logits, v_hgrn_onorm, v_w_out, v_final_gain):
    _, S, D = x.shape
    H = FOX_HEADS
    W = H * HEAD_DIM
    assert HGRN_HEADS == H and w_out.shape[2] == D
    cf = fox_w_in.shape[2]
    ch = hgrn_w_in.shape[2]
    ro = w_out.shape[1]
    co = hgrn_onorm.shape[1]
    assert N_DEV * cf == 4 * W + H and N_DEV * ch == 4 * W and N_DEV * ro == W and N_DEV * co == W
    x2 = x.reshape(S, D)
    tgt = loss_target.reshape(S, D)

    col = lambda n: (lambda r, i: r.at[:, pl.ds(pl.multiple_of(i * n, n), n)])
    row = lambda n: (lambda r, i: r.at[pl.ds(pl.multiple_of(i * n, n), n), :])
    late_views = [col(ch), row(ro), col(co)]
    late = _send_start(
        [hgrn_w_in[0].astype(BF16), w_out[1].astype(BF16), hgrn_onorm],
        [SDS((D, 4 * W), BF16), SDS((W, D), BF16), SDS((1, W), F32)],
        [lambda r, me, p, m: r] * 3, [lambda r, me, m, v=v: v(r, me) for v in late_views],
        name="gather_layer1_start")
    ng0 = norm_gains[0:1] + late[4][0:1, 0:1]

    wf_g, wo0 = _all_gather(
        [fox_w_in[0].astype(BF16), w_out[0].astype(BF16)], [SDS((N_DEV, D, cf), BF16), SDS((W, D), BF16)],
        [lambda r, p: r.at[p], row(ro)], name="gather_layer0")
    wf = jnp.transpose(wf_g, (1, 0, 2)).reshape(D, N_DEV * cf)
    wf_main = jnp.concatenate([wf[:, :3 * W], wf[:, 3 * W + H:]], axis=1)
    wfl_t = wf[:, 3 * W:3 * W + H].T

    h0 = _rms_fwd(x2, ng0, name="rms0_fwd")
    p0 = _mm_nn([h0], wf_main, BF16, scale_cols=(W, LOG2E * HEAD_DIM ** -0.5), name="fox_in_proj")
    fl_t = _mm_nt_rows(wfl_t, h0, name="fox_forget_proj")
    b_col = fox_b_f.reshape(H, 1)
    kaug = _fox_key_aug(*_fox_gate_fwd(fl_t, b_col, name="fox_gate_fwd"))
    o0, y0, qaug = _fox_fwd(p0, kaug, H=H, name="fox_attn_fwd")
    x1 = _mm_nn([y0], wo0, F32, residual=x2, name="fox_out_proj")

    wh, wo1, onorm = _send_wait(
        late, [lambda r, me, p, m: r] * 3, [lambda r, me, m, v=v: v(r, me) for v in late_views],
        [lambda src, land, me, v=v: (src, v(land, me)) for v in late_views], [(D, ch), (ro, D), (1, co)],
        x1[0:8], name="gather_layer1_wait")
    lb = _lb_fwd(hgrn_lb_logits, name="hgrn_lower_bound")
    h1 = _rms_fwd(x1, norm_gains[1:2], name="rms1_fwd")
    p1 = _mm_nn([h1], wh, BF16, b_cols=[(0, W), (2 * W, 4 * W)], name="hgrn_in_proj")
    f1 = _mm_nn([h1], wh, F32, b_cols=[(W, 2 * W)], name="hgrn_forget_proj")
    o1, y1, states = _hgrn_fwd(p1, f1, lb, onorm, H=H, name="hgrn_fwd")
    xo = _mm_nn([y1], wo1, F32, residual=x1, name="hgrn_out_proj")

    dx2, dx2b, loss_part, dgf = _loss_head(xo, final_gain.reshape(1, D), tgt, name="loss_head")
    loss = lax.psum(jnp.sum(loss_part), ("x", "y", "c"))

    dy1 = _mm_nn([dx2b], wo1, BF16, b_t=True, name="hgrn_out_proj_dx")
    dwo1 = _mm_tn(y1, [dx2b], BF16, name="hgrn_out_proj_dw")
    do1, dg1, donorm = _hgrn_post_bwd(dy1, o1, p1, onorm, H=H, name="hgrn_post_bwd")
    dq1, df1, di1, dlb = _hgrn_bwd(p1, f1, lb, do1, states, H=H, name="hgrn_bwd")
    segs1 = [dq1, df1, di1, dg1]
    dh1 = _mm_nn(segs1, wh, BF16, b_t=True, tn=D, name="hgrn_in_proj_dx")
    dwh = _mm_tn(h1, segs1, BF16, name="hgrn_in_proj_dw")
    part_views = [col(ch), row(ro)]
    slot = lambda r, me, m: r.at[m]
    ex1 = _send_start([dwh, dwo1], [SDS((N_DEV, D, ch), BF16), SDS((N_DEV, ro, D), BF16)],
                      [lambda r, me, p, m, v=v: v(r, p) for v in part_views], [slot] * 2,
                      name="exchange_layer1_start")
    ng1 = norm_gains[1:2] + ex1[4][0:1, 0:1]
    dx1, dx1b, dng1 = _rms_bwd(x1, ng1, dh1, dx2, name="rms1_bwd")

    dy0 = _mm_nn([dx1b], wo0, BF16, b_t=True, name="fox_out_proj_dx")
    dwo0 = _mm_tn(y0, [dx1b], BF16, name="fox_out_proj_dw")
    do0, dg0, doaug = _fox_post_bwd(dy0, o0, p0, H=H, name="fox_post_bwd")
    dq0, dc_row, dk0, dv0, dc_key = _fox_bwd(p0, kaug, qaug, do0, doaug, H=H, name="fox_attn_bwd")
    dfl_t, dbf = _fox_gate_bwd(dc_row.reshape(H, S), dc_key.reshape(H, S), fl_t, b_col, name="fox_gate_bwd")
    dfl_tb = dfl_t.astype(BF16)
    dwfl_t = _mm_nn([dfl_tb], h0, BF16, name="fox_forget_proj_dw")
    segs0 = [dq0, dk0, dv0, dg0]
    dwf_main = _mm_tn(h0, segs0, BF16, name="fox_in_proj_dw")
    dwf = jnp.concatenate([dwf_main[:, :3 * W], dwfl_t.T, dwf_main[:, 3 * W:]], axis=1)
    dwf_blocks = jnp.transpose(dwf.reshape(D, N_DEV, cf), (1, 0, 2))
    ex0 = _send_start([dwf_blocks, dwo0], [SDS((N_DEV, D, cf), BF16), SDS((N_DEV, ro, D), BF16)],
                      [lambda r, me, p, m: r.at[p], lambda r, me, p, m: row(ro)(r, p)], [slot] * 2,
                      name="exchange_layer0_start")
    wfl_t0 = wfl_t + ex0[4][0:1, 0:1].astype(BF16)
    dh0_f = _mm_nn([dfl_tb.T], wfl_t0, BF16, name="fox_forget_proj_dx")
    dh0 = _mm_nn(segs0, wf_main, BF16, residual=dh0_f, b_t=True, tn=D, name="fox_in_proj_dx")
    grad_x, _, dng0 = _rms_bwd(x2, norm_gains[0:1], dh0, dx1, name="rms0_bwd")

    own1 = [lambda src, land, me, v=v: (v(src, me), land.at[0]) for v in part_views]
    rh, ro1 = _send_wait(ex1, [lambda r, me, p, m, v=v: v(r, p) for v in part_views], [slot] * 2, own1,
                         [(D, ch), (ro, D)], dng0, name="exchange_layer1_wait")
    rf, ro0 = _send_wait(ex0, [lambda r, me, p, m: r.at[p], lambda r, me, p, m: row(ro)(r, p)], [slot] * 2,
                         [lambda src, land, me: (src.at[me], land.at[0]),
                          lambda src, land, me: (row(ro)(src, me), land.at[0])],
                         [(D, cf), (ro, D)], dng0, name="exchange_layer0_wait")

    pad = lambda a: jnp.pad(a, ((0, 0), (0, W - a.shape[1])))
    stats = jnp.concatenate([dng0, dng1, dlb, dgf, pad(dbf.reshape(1, H)), donorm,
                             jnp.zeros((2, W), F32)], axis=0)
    assert D == W
    (stats_all,) = _all_gather([stats], [SDS((N_DEV, STAT_ROWS, W), F32)], [lambda r, p: r.at[p]],
                               name="gather_small_grads")
    g_small = _stats_reduce(stats_all, hgrn_lb_logits, name="reduce_small_grads")
    me = 4 * lax.axis_index("x") + 2 * lax.axis_index("y") + lax.axis_index("c")
    g_onorm = lax.dynamic_slice_in_dim(g_small[6:7], me * co, co, axis=1)

    def upd(w, m, v, parts, name):
        shp = w.shape
        r2 = (-1, shp[-1])
        g, d, mn, vn = _adamw(w.reshape(r2), m.reshape(r2), v.reshape(r2), parts, name=name)
        return g.reshape(shp), d.reshape(shp), mn.reshape(shp), vn.reshape(shp)

    res = {
        "norm_gains": upd(norm_gains, m_norm_gains, v_norm_gains, [g_small[None, 0:2]], "adamw_norm_gains"),
        "fox_w_in": upd(fox_w_in, m_fox_w_in, v_fox_w_in, [rf], "adamw_fox_w_in"),
        "fox_b_f": upd(fox_b_f, m_fox_b_f, v_fox_b_f, [g_small[None, 5:6, :H]], "adamw_fox_b_f"),
        "hgrn_w_in": upd(hgrn_w_in, m_hgrn_w_in, v_hgrn_w_in, [rh], "adamw_hgrn_w_in"),
        "hgrn_lb_logits": upd(hgrn_lb_logits, m_hgrn_lb_logits, v_hgrn_lb_logits, [g_small[None, 2:4]],
                              "adamw_hgrn_lb_logits"),
        "hgrn_onorm": upd(hgrn_onorm, m_hgrn_onorm, v_hgrn_onorm, [g_onorm[None]], "adamw_hgrn_onorm"),
        "w_out": upd(w_out, m_w_out, v_w_out, [ro0, ro1], "adamw_w_out"),
        "final_gain": upd(final_gain.reshape(1, D), m_final_gain.reshape(1, D), v_final_gain.reshape(1, D),
                          [g_small[None, 4:5]], "adamw_final_gain"),
    }
    order = ["norm_gains", "fox_w_in", "fox_b_f", "hgrn_w_in", "hgrn_lb_logits", "hgrn_onorm", "w_out", "final_gain"]
    fix = lambda n, a: a.reshape(D) if n == "final_gain" else a
    outs = [loss, grad_x.reshape(1, S, D)]
    for k in range(4):
        outs += [fix(n, res[n][k]) for n in order]
    return tuple(outs)
```

```python
import functools

import numpy as np
import jax
import jax.numpy as jnp
from jax import lax
from jax.experimental import pallas as pl
from jax.experimental.pallas import tpu as pltpu

F32 = jnp.float32
BF16 = jnp.bfloat16
SDS = jax.ShapeDtypeStruct
MESH = pl.DeviceIdType.MESH

EPS = 1e-6
ADAM_LR, ADAM_B1, ADAM_B2, ADAM_EPS, ADAM_WD, ADAM_STEP = 0.001, 0.9, 0.999, 1e-08, 0.01, 10

N_DEV = 8
FOX_HEADS = 16
HGRN_HEADS = 16
HEAD_DIM = 128
HGRN_CHUNK = 128
HGRN_LEAF = 16
HGRN_HEADS_PER_STEP = 4
EXP_CLAMP = 85.0
ATT_BLOCK = 512
ATT_HEADS_PER_STEP = 4
ATT_BWD_HEADS_PER_STEP = 2
POST_HEADS_PER_STEP = 4
NEG = -1e30
LOG2E = 1.4426950408889634
LN2 = 0.6931471805599453

VMEM_LIMIT_V7X = 56 * 1024 * 1024


def _params(*sem):
    return pltpu.CompilerParams(dimension_semantics=sem, vmem_limit_bytes=VMEM_LIMIT_V7X)


def _silu(x):
    return x * jax.nn.sigmoid(x)


def _dsilu(x):
    s = jax.nn.sigmoid(x)
    return s * (1.0 + x * (1.0 - s))


def _dot(a, b):
    return jnp.dot(a, b, preferred_element_type=F32)


def _dot_nt(a, b):
    return lax.dot_general(a, b, (((1,), (1,)), ((), ())), preferred_element_type=F32)


def _dot_tn(a, b):
    return lax.dot_general(a, b, (((0,), (0,)), ((), ())), preferred_element_type=F32)


def _mm_nn(a_list, b, out_dtype, *, name, residual=None, scale_cols=None, b_t=False, b_cols=None,
           tm=1024, tn=1024, tk=2048):
    ns = len(a_list)
    M, Ks = a_list[0].shape
    K, N = (b.shape[1], b.shape[0]) if b_t else b.shape
    if b_cols is None:
        b_cols = [(0, N)]
    else:
        assert not b_t
        N = sum(e - s for s, e in b_cols)
    dot = _dot_nt if b_t else _dot
    assert K == ns * Ks and all(a.shape == (M, Ks) for a in a_list)
    if ns > 1:
        tk = tk // 2
    tm, tn, tk = min(tm, M), min(tn, N), min(tk, Ks)
    assert M % tm == 0 and N % tn == 0 and Ks % tk == 0
    assert scale_cols is None or scale_cols[0] % tn == 0
    assert all(s % tn == 0 and e % tn == 0 for s, e in b_cols)
    nks = Ks // tk
    nk = ns * nks
    has_res = residual is not None

    def body(*refs):
        a_refs, b_ref = refs[:ns], refs[ns]
        res_ref = refs[ns + 1] if has_res else None
        o_ref = refs[ns + 1 + has_res]

        def finish(r):
            if has_res:
                r = r + res_ref[...].astype(F32)
            if scale_cols is not None:
                r = r * jnp.where(pl.program_id(1) < scale_cols[0] // tn, scale_cols[1], 1.0)
            o_ref[...] = r.astype(out_dtype)

        if nk == 1:
            finish(dot(a_refs[0][...], b_ref[...]))
            return
        acc_ref = refs[ns + 2 + has_res]
        k = pl.program_id(2)

        @pl.when(k == 0)
        def _():
            acc_ref[...] = jnp.zeros_like(acc_ref)

        for s in range(ns):
            def step(s=s):
                acc_ref[...] += dot(a_refs[s][...], b_ref[...])

            if ns == 1:
                step()
            else:
                pl.when(k // nks == s)(step)

        @pl.when(k == nk - 1)
        def _():
            finish(acc_ref[...])

    def a_map(i, j, k, s):
        return (i, jnp.clip(k - s * nks, 0, nks - 1))

    in_specs = [pl.BlockSpec((tm, tk), functools.partial(a_map, s=s)) for s in range(ns)]
    def b_col(j):
        src = j + b_cols[0][0] // tn
        for (_, e0), (s1, _) in zip(b_cols[:-1], b_cols[1:]):
            src = src + jnp.where(src >= e0 // tn, (s1 - e0) // tn, 0)
        return src

    if b_t:
        in_specs.append(pl.BlockSpec((tn, tk), lambda i, j, k: (j, k)))
    else:
        in_specs.append(pl.BlockSpec((tk, tn), lambda i, j, k: (k, b_col(j))))
    args = list(a_list) + [b]
    if has_res:
        in_specs.append(pl.BlockSpec((tm, tn), lambda i, j, k: (i, j)))
        args.append(residual)
    return pl.pallas_call(
        body, grid=(M // tm, N // tn, nk), in_specs=in_specs,
        out_specs=pl.BlockSpec((tm, tn), lambda i, j, k: (i, j)),
        out_shape=SDS((M, N), out_dtype),
        scratch_shapes=[] if nk == 1 else [pltpu.VMEM((tm, tn), F32)],
        compiler_params=_params("parallel", "parallel", "arbitrary"), name=name,
    )(*args)


def _mm_tn(a, b_list, out_dtype, *, name, tm=2048, tn=1024, tk=512):
    ns = len(b_list)
    S, M = a.shape
    Ns = b_list[0].shape[1]
    assert all(b.shape == (S, Ns) for b in b_list)
    tm, tn, tk = min(tm, M), min(tn, Ns), min(tk, S)
    assert M % tm == 0 and Ns % tn == 0 and S % tk == 0
    njs = Ns // tn
    nk = S // tk

    def body(*refs):
        a_ref, b_refs, o_ref, acc_ref = refs[0], refs[1:1 + ns], refs[1 + ns], refs[2 + ns]
        j, k = pl.program_id(1), pl.program_id(2)

        @pl.when(k == 0)
        def _():
            acc_ref[...] = jnp.zeros_like(acc_ref)

        for s in range(ns):
            def step(s=s):
                acc_ref[...] += _dot_tn(a_ref[...], b_refs[s][...])

            if ns == 1:
                step()
            else:
                pl.when(j // njs == s)(step)

        @pl.when(k == nk - 1)
        def _():
            o_ref[...] = acc_ref[...].astype(out_dtype)

    def b_map(i, j, k, s):
        return (k, jnp.clip(j - s * njs, 0, njs - 1))

    in_specs = [pl.BlockSpec((tk, tm), lambda i, j, k: (k, i))]
    in_specs += [pl.BlockSpec((tk, tn), functools.partial(b_map, s=s)) for s in range(ns)]
    return pl.pallas_call(
        body, grid=(M // tm, ns * njs, nk), in_specs=in_specs,
        out_specs=pl.BlockSpec((tm, tn), lambda i, j, k: (i, j)),
        out_shape=SDS((M, ns * Ns), out_dtype),
        scratch_shapes=[pltpu.VMEM((tm, tn), F32)],
        compiler_params=_params("parallel", "parallel", "arbitrary"), name=name,
    )(a, *b_list)


def _mm_nt_rows(w_t, h, *, name, tn=1024):
    R, K = w_t.shape
    S = h.shape[0]
    tn = min(tn, S)

    def body(w_ref, h_ref, o_ref):
        o_ref[...] = _dot_nt(w_ref[...], h_ref[...])

    return pl.pallas_call(
        body, grid=(S // tn,),
        in_specs=[pl.BlockSpec((R, K), lambda i: (0, 0)), pl.BlockSpec((tn, K), lambda i: (i, 0))],
        out_specs=pl.BlockSpec((R, tn), lambda i: (0, i)),
        out_shape=SDS((R, S), F32), compiler_params=_params("parallel"), name=name,
    )(w_t, h)


def _rms_fwd(x, gain, *, name, tm=512):
    S, D = x.shape
    tm = min(tm, S)

    def body(x_ref, g_ref, h_ref):
        xv = x_ref[...]
        r = lax.rsqrt(jnp.mean(xv * xv, axis=-1, keepdims=True) + EPS)
        h_ref[...] = ((xv * r) * g_ref[...]).astype(BF16)

    return pl.pallas_call(
        body, grid=(S // tm,),
        in_specs=[pl.BlockSpec((tm, D), lambda i: (i, 0)), pl.BlockSpec((1, D), lambda i: (0, 0))],
        out_specs=pl.BlockSpec((tm, D), lambda i: (i, 0)),
        out_shape=SDS((S, D), BF16), compiler_params=_params("parallel"), name=name,
    )(x, gain)


def _rms_bwd(x, gain, dh, dres, *, name, tm=256):
    S, D = x.shape
    tm = min(tm, S)

    def body(x_ref, g_ref, dh_ref, dres_ref, dx_ref, dxb_ref, dg_ref):
        @pl.when(pl.program_id(0) == 0)
        def _():
            dg_ref[...] = jnp.zeros_like(dg_ref)

        xv = x_ref[...]
        r = lax.rsqrt(jnp.mean(xv * xv, axis=-1, keepdims=True) + EPS)
        xh = xv * r
        dhv = dh_ref[...].astype(F32)
        dg_ref[...] += jnp.sum(dhv * xh, axis=0, keepdims=True)
        dxh = dhv * g_ref[...]
        dx = r * (dxh - xh * jnp.mean(dxh * xh, axis=-1, keepdims=True)) + dres_ref[...]
        dx_ref[...] = dx
        dxb_ref[...] = dx.astype(BF16)

    row = pl.BlockSpec((tm, D), lambda i: (i, 0))
    vec = pl.BlockSpec((1, D), lambda i: (0, 0))
    return pl.pallas_call(
        body, grid=(S // tm,), in_specs=[row, vec, row, row], out_specs=[row, row, vec],
        out_shape=[SDS((S, D), F32), SDS((S, D), BF16), SDS((1, D), F32)],
        compiler_params=_params("arbitrary"), name=name,
    )(x, gain, dh, dres)


def _loss_head(x, gain, target, *, name, tm=256):
    S, D = x.shape
    tm = min(tm, S)
    assert tm % 8 == 0 and D % 128 == 0

    def body(x_ref, g_ref, t_ref, dx_ref, dxb_ref, loss_ref, dg_ref):
        @pl.when(pl.program_id(0) == 0)
        def _():
            dg_ref[...] = jnp.zeros_like(dg_ref)
            loss_ref[...] = jnp.zeros_like(loss_ref)

        xv = x_ref[...]
        g = g_ref[...]
        r = lax.rsqrt(jnp.mean(xv * xv, axis=-1, keepdims=True) + EPS)
        xh = xv * r
        err = xh * g - t_ref[...]
        e2 = (err * err).reshape(tm // 8, 8, D).sum(axis=0)
        part = e2[:, 0:128]
        for k in range(1, D // 128):
            part = part + e2[:, k * 128:(k + 1) * 128]
        loss_ref[...] += part * (0.5 / D)
        dy = err * (1.0 / D)
        dg_ref[...] += jnp.sum(dy * xh, axis=0, keepdims=True)
        dxh = dy * g
        dx = r * (dxh - xh * jnp.mean(dxh * xh, axis=-1, keepdims=True))
        dx_ref[...] = dx
        dxb_ref[...] = dx.astype(BF16)

    row = pl.BlockSpec((tm, D), lambda i: (i, 0))
    vec = pl.BlockSpec((1, D), lambda i: (0, 0))
    return pl.pallas_call(
        body, grid=(S // tm,), in_specs=[row, vec, row],
        out_specs=[row, row, pl.BlockSpec((8, 128), lambda i: (0, 0)), vec],
        out_shape=[SDS((S, D), F32), SDS((S, D), BF16), SDS((8, 128), F32), SDS((1, D), F32)],
        compiler_params=_params("arbitrary"), name=name,
    )(x, gain, target)


def _split3(x):
    hi = x.astype(BF16)
    r1 = x - hi.astype(F32)
    mid = r1.astype(BF16)
    lo = (r1 - mid.astype(F32)).astype(BF16)
    return hi, mid, lo


def _split2(x):
    hi = x.astype(BF16)
    lo = (x - hi.astype(F32)).astype(BF16)
    return hi, lo


def _fox_gate_fwd(fl_t, b_col, *, name):
    H, S = fl_t.shape
    L = 128
    tri = jnp.asarray(np.triu(np.ones((L, L), np.float32)), BF16)

    def body(fl_ref, b_ref, tri_ref, hi_ref, mid_ref, lo_ref, carry):
        @pl.when(pl.program_id(0) == 0)
        def _():
            carry[...] = jnp.zeros_like(carry)

        z = fl_ref[...] + b_ref[...]
        lf = jnp.minimum(z, 0.0) - jnp.log(1.0 + jnp.exp(-jnp.abs(z)))
        hi, mid, lo = _split3(lf)
        t = tri_ref[...]
        c = (_dot(hi, t) + _dot(mid, t)) + _dot(lo, t) + carry[...]
        carry[...] = c[:, L - 1:L]
        hi_ref[...], mid_ref[...], lo_ref[...] = _split3(c * (-LOG2E))

    blk = pl.BlockSpec((H, L), lambda i: (0, i))
    return pl.pallas_call(
        body, grid=(S // L,),
        in_specs=[blk, pl.BlockSpec((H, 1), lambda i: (0, 0)), pl.BlockSpec((L, L), lambda i: (0, 0))],
        out_specs=[blk] * 3, out_shape=[SDS((H, S), BF16)] * 3, scratch_shapes=[pltpu.VMEM((H, 1), F32)],
        compiler_params=_params("arbitrary"), name=name,
    )(fl_t, b_col, tri)


def _fox_gate_bwd(dc_row, dc_key, fl_t, b_col, *, name):
    H, S = fl_t.shape
    L = 128
    n = S // L
    tri = jnp.asarray(np.tril(np.ones((L, L), np.float32)), BF16)

    def body(dcr_ref, dck_ref, fl_ref, b_ref, tri_ref, dfl_ref, db_ref, carry):
        @pl.when(pl.program_id(0) == 0)
        def _():
            carry[...] = jnp.zeros_like(carry)
            db_ref[...] = jnp.zeros_like(db_ref)

        hi, mid, lo = _split3(dcr_ref[...] + dck_ref[...])
        t = tri_ref[...]
        dlf = (_dot(hi, t) + _dot(mid, t)) + _dot(lo, t) + carry[...]
        carry[...] = dlf[:, 0:1]
        z = fl_ref[...] + b_ref[...]
        dfl = dlf * jax.nn.sigmoid(-z)
        dfl_ref[...] = dfl
        db_ref[...] += jnp.sum(dfl, axis=1, keepdims=True)

    blk = pl.BlockSpec((H, L), lambda i: (0, n - 1 - i))
    col = pl.BlockSpec((H, 1), lambda i: (0, 0))
    return pl.pallas_call(
        body, grid=(n,), in_specs=[blk, blk, blk, col, pl.BlockSpec((L, L), lambda i: (0, 0))],
        out_specs=[blk, col], out_shape=[SDS((H, S), F32), SDS((H, 1), F32)],
        scratch_shapes=[pltpu.VMEM((H, 1), F32)], compiler_params=_params("arbitrary"), name=name,
    )(dc_row, dc_key, fl_t, b_col, tri)


AUG = HEAD_DIM


def _lane_select(cols, shape):
    lane = lax.broadcasted_iota(jnp.int32, shape, 1)
    out = jnp.zeros(shape, BF16)
    for k, c in reversed(list(enumerate(cols))):
        c = jnp.full(shape, c, BF16) if isinstance(c, (int, float)) else jnp.broadcast_to(c, shape).astype(BF16)
        out = jnp.where(lane == k, c, out)
    return out


def _fox_key_aug(b_hi, b_mid, b_lo):
    H, S = b_hi.shape
    ones = jnp.ones((H, S), BF16)
    ka = jnp.stack([b_hi, b_mid, b_lo, ones, ones, ones], axis=-1)
    ka = jnp.pad(ka, ((0, 0), (0, 0), (0, AUG - 6)))
    return jnp.transpose(ka, (1, 0, 2)).reshape(S, H * AUG)


def _fox_fwd(p0, kaug, *, H, name):
    S = p0.shape[0]
    T = min(ATT_BLOCK, S)
    nq = S // T
    dh = HEAD_DIM
    G = ATT_HEADS_PER_STEP
    assert H % G == 0

    def body(q_ref, k_ref, ka_ref, v_ref, g_ref, o_ref, y_ref, qa_ref, m_sc, acc_sc, p_sc, al_sc):
        i = pl.program_id(1)
        qaug = _lane_select([1.0, 1.0, 1.0], (T, AUG))
        ones = jnp.ones((T, dh), BF16)
        m_sc[...] = jnp.full_like(m_sc, NEG)
        acc_sc[...] = jnp.zeros_like(acc_sc)

        def step(j, before, masked):
            rows = pl.ds(pl.multiple_of(j * T, T), T)
            for g in range(G):
                hd = slice(g * dh, (g + 1) * dh)
                if not masked:
                    prev = pl.ds(pl.multiple_of(before * T, T), T)
                    vp = jnp.concatenate([v_ref[prev, hd], ones], axis=1)
                    acc_sc[g] = jnp.tile(al_sc[g], (1, 2)) * acc_sc[g] + _dot(p_sc[g], vp)
                q = jnp.concatenate([q_ref[:, hd], qaug], axis=1)
                kj = jnp.concatenate([k_ref[rows, hd], ka_ref[rows, hd]], axis=1)
                t = _dot_nt(q, kj)
                if masked:
                    row = lax.broadcasted_iota(jnp.int32, (T, T), 0)
                    col = lax.broadcasted_iota(jnp.int32, (T, T), 1)
                    t = jnp.where(row >= col, t, NEG)
                m_prev = m_sc[g]
                m_new = jnp.maximum(m_prev, jnp.max(t, axis=-1, keepdims=True))
                p_sc[g] = jnp.exp2(t - jnp.tile(m_new, (1, T // 128))).astype(BF16)
                al_sc[g] = jnp.exp2(m_prev - m_new)
                m_sc[g] = m_new

        step(i, None, True)

        def loop_body(j, carry):
            step(j, jnp.where(j == 0, i, j - 1), False)
            return carry

        lax.fori_loop(0, i, loop_body, 0)
        rows = pl.ds(pl.multiple_of(jnp.where(i == 0, 0, i - 1) * T, T), T)
        for g in range(G):
            hd = slice(g * dh, (g + 1) * dh)
            vp = jnp.concatenate([v_ref[rows, hd], ones], axis=1)
            acc = jnp.tile(al_sc[g], (1, 2)) * acc_sc[g] + _dot(p_sc[g], vp)
            l = acc[:, dh:]
            o = acc[:, :dh] / l
            o_ref[:, hd] = o
            y_ref[:, hd] = (o * _silu(g_ref[:, hd].astype(F32))).astype(BF16)
            hi, mid, lo = _split3(-(m_sc[g] + jnp.log2(l)))
            qa_ref[:, hd] = _lane_select([1.0, 1.0, 1.0, hi, mid, lo], (T, AUG))

    blk = lambda off: pl.BlockSpec((T, G * dh), lambda h, i: (i, off // G + h))
    full = lambda off: pl.BlockSpec((S, G * dh), lambda h, i: (0, off // G + h), pipeline_mode=pl.Buffered(1))
    return pl.pallas_call(
        body, grid=(H // G, nq),
        in_specs=[blk(0), full(H), full(0), full(2 * H), blk(3 * H)],
        out_specs=[blk(0), blk(0), blk(0)],
        out_shape=[SDS((S, H * dh), F32), SDS((S, H * dh), BF16), SDS((S, H * AUG), BF16)],
        scratch_shapes=[pltpu.VMEM((G, T, 128), F32), pltpu.VMEM((G, T, 2 * dh), F32),
                        pltpu.VMEM((G, T, T), BF16), pltpu.VMEM((G, T, 128), F32)],
        compiler_params=_params("parallel", "arbitrary"), name=name,
    )(p0, p0, kaug, p0, p0)


def _fox_post_bwd(dy, o, p0, *, H, name, tm=512):
    S = dy.shape[0]
    dh = HEAD_DIM
    tm = min(tm, S)
    G = POST_HEADS_PER_STEP
    assert H % G == 0

    def body(dy_ref, o_ref, g_ref, do_ref, dg_ref, da_ref):
        dyv = dy_ref[...].astype(F32)
        ov = o_ref[...]
        g = g_ref[...].astype(F32)
        do = (dyv * _silu(g)).astype(BF16)
        do_ref[...] = do
        dg_ref[...] = (dyv * ov * _dsilu(g)).astype(BF16)
        prod = do.astype(F32) * ov
        for k in range(G):
            hd = slice(k * dh, (k + 1) * dh)
            delta = jnp.sum(prod[:, hd], axis=-1, keepdims=True)
            hi, mid, lo = _split3(-jnp.broadcast_to(delta, (tm, AUG)))
            da_ref[:, hd] = _lane_select([hi, mid, lo], (tm, AUG))

    blk = pl.BlockSpec((tm, G * dh), lambda h, i: (i, h))
    return pl.pallas_call(
        body, grid=(H // G, S // tm),
        in_specs=[blk, blk, pl.BlockSpec((tm, G * dh), lambda h, i: (i, 3 * H // G + h))],
        out_specs=[blk, blk, blk],
        out_shape=[SDS((S, H * dh), BF16), SDS((S, H * dh), BF16), SDS((S, H * AUG), BF16)],
        compiler_params=_params("parallel", "parallel"), name=name,
    )(dy, o, p0)


def _fox_bwd(p0, kaug, qaug, do, doaug, *, H, name):
    S = p0.shape[0]
    T = min(ATT_BLOCK, S)
    nq = S // T
    dh = HEAD_DIM
    scale = dh ** -0.5
    G = ATT_BWD_HEADS_PER_STEP
    assert H % G == 0

    def body(q_ref, qa_ref, k_ref, ka_ref, v_ref, do_ref, da_ref, dq_ref, rs_ref, dk_ref, dv_ref, dc_ref,
             dq_sc, dk_sc, dv_sc, pt_sc, dst_sc):
        j = pl.program_id(1)
        vaug = _lane_select([1.0, 1.0, 1.0], (T, AUG))
        ones = jnp.ones((T, dh), BF16)

        @pl.when(j == 0)
        def _():
            dq_sc[...] = jnp.zeros_like(dq_sc)

        dk_sc[...] = jnp.zeros_like(dk_sc)
        dv_sc[...] = jnp.zeros_like(dv_sc)

        def apply(prev):
            for g in range(G):
                hd = slice(g * dh, (g + 1) * dh)
                dv_sc[g] += _dot(pt_sc[g], do_ref[prev, hd])
                dk_sc[g] += _dot(dst_sc[g], jnp.concatenate([q_ref[prev, hd], ones], axis=1))
                dq_sc[g, prev] += _dot_tn(dst_sc[g], jnp.concatenate([k_ref[:, hd], ones], axis=1))

        def step(i, masked):
            rows = pl.ds(pl.multiple_of(i * T, T), T)
            if not masked:
                apply(pl.ds(pl.multiple_of((i - 1) * T, T), T))
            for g in range(G):
                hd = slice(g * dh, (g + 1) * dh)
                k = jnp.concatenate([k_ref[:, hd], ka_ref[:, hd]], axis=1)
                v = jnp.concatenate([v_ref[:, hd], vaug], axis=1)
                pt = jnp.exp2(_dot_nt(k, jnp.concatenate([q_ref[rows, hd], qa_ref[rows, hd]], axis=1)))
                if masked:
                    row = lax.broadcasted_iota(jnp.int32, (T, T), 0)
                    col = lax.broadcasted_iota(jnp.int32, (T, T), 1)
                    pt = jnp.where(col >= row, pt, 0.0)
                dst = pt * _dot_nt(v, jnp.concatenate([do_ref[rows, hd], da_ref[rows, hd]], axis=1))
                pt_sc[g] = pt.astype(BF16)
                dst_sc[g] = dst.astype(BF16)

        step(j, True)

        def loop_body(i, carry):
            step(i, False)
            return carry

        lax.fori_loop(j + 1, nq, loop_body, 0)
        apply(pl.ds((nq - 1) * T, T))
        for g in range(G):
            hd = slice(g * dh, (g + 1) * dh)
            dk_ref[:, hd] = (dk_sc[g, :, :dh] * LN2).astype(BF16)
            dv_ref[:, hd] = dv_sc[g].astype(BF16)
            dc_ref[g] = -jnp.transpose(dk_sc[g, :, dh:])[0:1]

        @pl.when(j == nq - 1)
        def _():
            for g in range(G):
                dq_ref[:, g * dh:(g + 1) * dh] = (dq_sc[g, :, :dh] * scale).astype(BF16)
                for i in range(nq):
                    rs_ref[g, :, i * T:(i + 1) * T] = jnp.transpose(dq_sc[g, i * T:(i + 1) * T, dh:])[0:1]

    blk = lambda off: pl.BlockSpec((T, G * dh), lambda h, j: (j, off // G + h))
    full = lambda off: pl.BlockSpec((S, G * dh), lambda h, j: (0, off // G + h))
    once = lambda off: pl.BlockSpec((S, G * dh), lambda h, j: (0, off // G + h), pipeline_mode=pl.Buffered(1))
    rowv = pl.BlockSpec((G, 1, T), lambda h, j: (h, 0, j))
    return pl.pallas_call(
        body, grid=(H // G, nq),
        in_specs=[once(0), once(0), blk(H), blk(0), blk(2 * H), once(0), once(0)],
        out_specs=[full(0), pl.BlockSpec((G, 1, S), lambda h, j: (h, 0, 0)), blk(0), blk(0), rowv],
        out_shape=[SDS((S, H * dh), BF16), SDS((H, 1, S), F32), SDS((S, H * dh), BF16), SDS((S, H * dh), BF16),
                   SDS((H, 1, S), F32)],
        scratch_shapes=[pltpu.VMEM((G, S, 2 * dh), F32), pltpu.VMEM((G, T, 2 * dh), F32), pltpu.VMEM((G, T, dh), F32),
                        pltpu.VMEM((G, T, T), BF16), pltpu.VMEM((G, T, T), BF16)],
        compiler_params=_params("parallel", "arbitrary"), name=name,
    )(p0, qaug, p0, kaug, p0, do, doaug)


def _hgrn_levels(C, leaf):
    levels = []
    h = C // 2
    while h >= leaf:
        levels.append(h)
        h //= 2
    return levels


def _hgrn_sum_matrix(C, leaf):
    t = np.arange(C)[:, None]
    u = np.arange(C)[None, :]
    mats = [(u <= t), (u > t)]
    for h in _hgrn_levels(C, leaf):
        start = (t // (2 * h)) * (2 * h)
        mid = start + h - 1
        second = t > mid
        m = np.where(second, (u > mid) & (u <= t), (u > t) & (u <= mid))
        mats.append(m)
    lstart = (t // leaf) * leaf
    mats.append((u >= lstart) & (u <= t))
    return np.concatenate([m.astype(np.float32) for m in mats], axis=0)


def _hgrn_chunk_terms(qr, fz, lb, msum, C, leaf):
    levels = _hgrn_levels(C, leaf)
    sq = _silu(qr)
    sp = 1.0 / (1.0 + jnp.exp(-fz))
    sn = 1.0 / (1.0 + jnp.exp(fz))
    f = lb + (1.0 - lb) * sp
    lf = jnp.log(f)
    k = (1.0 - lb) * sn
    hi, lo = _split2(lf)
    dsum = _dot(msum, hi) + _dot(msum, lo)
    b = dsum[0:C]
    kdec = dsum[C:2 * C]
    rowi = lax.broadcasted_iota(jnp.int32, (C, 1), 0)
    lev = []
    for n, h in enumerate(levels):
        e = jnp.exp(dsum[(2 + n) * C:(3 + n) * C])
        selq = jnp.where((rowi % (2 * h)) >= h, 1.0, 0.0)
        qm = (sq * e * selq).astype(BF16)
        km = (k * e * (1.0 - selq)).astype(BF16)
        lev.append((h, e, selq, qm, km))
    dleaf = dsum[(2 + len(levels)) * C:(3 + len(levels)) * C]
    eq = jnp.exp(dleaf)
    ek = jnp.exp(jnp.minimum(-dleaf, EXP_CLAMP))
    return dict(sq=sq, sp=sp, sn=sn, f=f, k=k, b=b, kdec=kdec, lev=lev, eq=eq, ek=ek,
                ql=(sq * eq).astype(BF16), kl=(k * ek).astype(BF16),
                qs=(sq * jnp.exp(b)).astype(BF16), ke=(k * jnp.exp(kdec)).astype(BF16),
                e_c=jnp.exp(b[C - 1:C, :]))


def _hgrn_masks(C, leaf, transposed):
    a = lax.broadcasted_iota(jnp.int32, (C, C), 0)
    bb = lax.broadcasted_iota(jnp.int32, (C, C), 1)
    t, s = (bb, a) if transposed else (a, bb)
    lev = [None if 2 * h == C else (t // (2 * h)) == (s // (2 * h)) for h in _hgrn_levels(C, leaf)]
    if leaf == C:
        leafm = s <= t
    else:
        leafm = ((t // leaf) == (s // leaf)) & (s <= t)
    return lev, leafm


def _hgrn_fwd(p1, f1, lb, onorm, *, H, name, tb=512):
    S = p1.shape[0]
    dk = HEAD_DIM
    C = min(HGRN_CHUNK, S)
    leaf = min(HGRN_LEAF, C)
    tb = min(tb, S)
    nc = tb // C
    G = HGRN_HEADS_PER_STEP
    assert H % G == 0
    msum = jnp.asarray(_hgrn_sum_matrix(C, leaf), BF16)

    def body(q_ref, f_ref, v_ref, g_ref, lb_ref, on_ref, ms_ref, o_ref, y_ref, st_ref, at_ref, st_sc):
        @pl.when(pl.program_id(1) == 0)
        def _():
            st_sc[...] = jnp.zeros_like(st_sc)

        msv = ms_ref[...]
        lmask, leafm = _hgrn_masks(C, leaf, False)

        def chunk(n, carry):
            rows = pl.ds(pl.multiple_of(n * C, C), C)
            for g in range(G):
                hd = slice(g * dk, (g + 1) * dk)
                tm = _hgrn_chunk_terms(q_ref[rows, hd].astype(F32), f_ref[rows, hd], lb_ref[:, hd], msv, C, leaf)
                v = v_ref[rows, hd]
                st = st_sc[g]
                st_ref[g, n] = st
                a = jnp.where(leafm, _dot_nt(tm["ql"], tm["kl"]), 0.0)
                for (h, e, selq, qm, km), m in zip(tm["lev"], lmask):
                    al = _dot_nt(qm, km)
                    a = a + (al if m is None else jnp.where(m, al, 0.0))
                at_ref[g, n] = jnp.transpose(a).astype(BF16)
                o = _dot_nt(tm["qs"], st.astype(BF16)) + _dot(a.astype(BF16), v)
                st_sc[g] = st * tm["e_c"] + _dot(v.T, tm["ke"])
                o_ref[rows, hd] = o
                rn = lax.rsqrt(jnp.mean(o * o, axis=-1, keepdims=True) + EPS)
                y = ((o * rn) * on_ref[:, hd]) * _silu(g_ref[rows, hd].astype(F32))
                y_ref[rows, hd] = y.astype(BF16)
            return carry

        lax.fori_loop(0, nc, chunk, 0)

    blk = lambda off: pl.BlockSpec((tb, G * dk), lambda h, i: (i, off // G + h))
    vec = pl.BlockSpec((1, G * dk), lambda h, i: (0, h))
    return pl.pallas_call(
        body, grid=(H // G, S // tb),
        in_specs=[blk(0), blk(0), blk(H), blk(2 * H), vec, vec,
                  pl.BlockSpec(msum.shape, lambda h, i: (0, 0))],
        out_specs=[blk(0), blk(0), pl.BlockSpec((G, nc, dk, dk), lambda h, i: (h, i, 0, 0)),
                   pl.BlockSpec((G, nc, C, C), lambda h, i: (h, i, 0, 0))],
        out_shape=[SDS((S, H * dk), F32), SDS((S, H * dk), BF16), SDS((H, S // C, dk, dk), F32),
                   SDS((H, S // C, C, C), BF16)],
        scratch_shapes=[pltpu.VMEM((G, dk, dk), F32)],
        compiler_params=_params("parallel", "arbitrary"), name=name,
    )(p1, f1, p1, p1, lb, onorm, msum)


def _hgrn_post_bwd(dy, o, p1, onorm, *, H, name, tm=512):
    S = dy.shape[0]
    dk = HEAD_DIM
    tm = min(tm, S)
    G = POST_HEADS_PER_STEP
    assert H % G == 0

    def body(dy_ref, o_ref, g_ref, on_ref, do_ref, dg_ref, don_ref):
        @pl.when(pl.program_id(1) == 0)
        def _():
            don_ref[...] = jnp.zeros_like(don_ref)

        for k in range(G):
            hd = slice(k * dk, (k + 1) * dk)
            dyv = dy_ref[:, hd].astype(F32)
            ov = o_ref[:, hd]
            g = g_ref[:, hd].astype(F32)
            onv = on_ref[:, hd]
            rn = lax.rsqrt(jnp.mean(ov * ov, axis=-1, keepdims=True) + EPS)
            oh = ov * rn
            dn = dyv * _silu(g)
            dg_ref[:, hd] = (dyv * (oh * onv) * _dsilu(g)).astype(BF16)
            don_ref[:, hd] += jnp.sum(dn * oh, axis=0, keepdims=True)
            doh = dn * onv
            do_ref[:, hd] = (rn * (doh - oh * jnp.mean(doh * oh, axis=-1, keepdims=True))).astype(BF16)

    blk = pl.BlockSpec((tm, G * dk), lambda h, i: (i, h))
    vec = pl.BlockSpec((1, G * dk), lambda h, i: (0, h))
    return pl.pallas_call(
        body, grid=(H // G, S // tm),
        in_specs=[blk, blk, pl.BlockSpec((tm, G * dk), lambda h, i: (i, 2 * H // G + h)), vec],
        out_specs=[blk, blk, vec],
        out_shape=[SDS((S, H * dk), BF16), SDS((S, H * dk), BF16), SDS((1, H * dk), F32)],
        compiler_params=_params("parallel", "arbitrary"), name=name,
    )(dy, o, p1, onorm)


def _hgrn_bwd(p1, f1, lb, do, states, a_t, *, H, name, tb=512):
    S = p1.shape[0]
    dk = HEAD_DIM
    C = min(HGRN_CHUNK, S)
    leaf = min(HGRN_LEAF, C)
    tb = min(tb, S)
    nc = tb // C
    nb = S // tb
    G = HGRN_HEADS_PER_STEP
    assert H % G == 0
    msum = jnp.asarray(_hgrn_sum_matrix(C, leaf), BF16)
    rtri = jnp.asarray(np.triu(np.ones((C, C), np.float32)), BF16)

    def body(q_ref, f_ref, v_ref, do_ref, st_ref, at_ref, lb_ref, ms_ref, rt_ref,
             dq_ref, df_ref, dv_ref, dlb_ref, g_sc):
        @pl.when(pl.program_id(1) == 0)
        def _():
            g_sc[...] = jnp.zeros_like(g_sc)
            dlb_ref[...] = jnp.zeros_like(dlb_ref)

        msv = ms_ref[...]
        rtv = rt_ref[...]
        lmask, leafm = _hgrn_masks(C, leaf, False)
        lmask_t, leafm_t = _hgrn_masks(C, leaf, True)
        f32 = lambda z: z.astype(F32)

        def head_chunk(g, n):
            hd = slice(g * dk, (g + 1) * dk)
            rows = pl.ds(pl.multiple_of(n * C, C), C)
            lbv = lb_ref[:, hd]
            qr = q_ref[rows, hd].astype(F32)
            tm = _hgrn_chunk_terms(qr, f_ref[rows, hd], lbv, msv, C, leaf)
            v = v_ref[rows, hd]
            dov = do_ref[rows, hd]
            st0 = st_ref[g, n]
            gt = g_sc[g]
            gtb = gt.astype(BF16)
            da = _dot_nt(dov, v)
            da_t = _dot_nt(v, dov)

            dal = jnp.where(leafm, da, 0.0).astype(BF16)
            dal_t = jnp.where(leafm_t, da_t, 0.0).astype(BF16)
            dql = _dot(dal, tm["kl"])
            dkl = _dot(dal_t, tm["ql"])
            dsq = dql * tm["eq"]
            dkk = dkl * tm["ek"]
            xq = f32(tm["ql"]) * dql
            xk = f32(tm["kl"]) * dkl
            for (h, e, selq, qm, km), m, m_t in zip(tm["lev"], lmask, lmask_t):
                dl = (da if m is None else jnp.where(m, da, 0.0)).astype(BF16)
                dl_t = (da_t if m_t is None else jnp.where(m_t, da_t, 0.0)).astype(BF16)
                dqm = _dot(dl, km)
                dkm = _dot(dl_t, qm)
                dsq = dsq + dqm * (e * selq)
                dkk = dkk + dkm * (e * (1.0 - selq))
                xq = xq + f32(qm) * dqm
                xk = xk + f32(km) * dkm
            dqs = _dot(dov, st0.astype(BF16))
            dke = _dot(v, gtb)
            dsq = dsq + dqs * jnp.exp(tm["b"])
            dkk = dkk + dke * jnp.exp(tm["kdec"])
            xq = xq + f32(tm["qs"]) * dqs
            xk = xk + f32(tm["ke"]) * dke
            dvv = _dot(at_ref[g, n], dov) + _dot_nt(tm["ke"], gtb)
            r_end = jnp.sum(f32(gtb) * _dot(v.T, tm["ke"]) + gt * (st0 * tm["e_c"]), axis=0, keepdims=True)
            g_sc[g] = gt * tm["e_c"] + _dot(dov.T, tm["qs"])
            xh, xm, xl = _split3(xq - xk)
            dlf = (_dot(rtv, xh) + _dot(rtv, xm)) + _dot(rtv, xl) + r_end
            dlf_f = dlf / tm["f"]
            dsp = (1.0 - lbv) * (dlf_f - dkk)
            df_ref[rows, hd] = (dsp * (tm["sp"] * tm["sn"])).astype(BF16)
            dq_ref[rows, hd] = (dsq * _dsilu(qr)).astype(BF16)
            dv_ref[rows, hd] = dvv.astype(BF16)
            dlb_ref[:, hd] += jnp.sum(dlf_f * tm["sn"] - dkk * tm["sn"], axis=0, keepdims=True)

        def chunk(nn, carry):
            for g in range(G):
                head_chunk(g, nc - 1 - nn)
            return carry

        lax.fori_loop(0, nc, chunk, 0)

    blk = lambda off: pl.BlockSpec((tb, G * dk), lambda h, i: (nb - 1 - i, off // G + h))
    vec = pl.BlockSpec((1, G * dk), lambda h, i: (0, h))
    return pl.pallas_call(
        body, grid=(H // G, nb),
        in_specs=[blk(0), blk(0), blk(H), blk(0),
                  pl.BlockSpec((G, nc, dk, dk), lambda h, i: (h, nb - 1 - i, 0, 0)),
                  pl.BlockSpec((G, nc, C, C), lambda h, i: (h, nb - 1 - i, 0, 0)), vec,
                  pl.BlockSpec(msum.shape, lambda h, i: (0, 0)), pl.BlockSpec((C, C), lambda h, i: (0, 0))],
        out_specs=[blk(0), blk(0), blk(0), vec],
        out_shape=[SDS((S, H * dk), BF16)] * 3 + [SDS((1, H * dk), F32)],
        scratch_shapes=[pltpu.VMEM((G, dk, dk), F32)],
        compiler_params=_params("parallel", "arbitrary"), name=name,
    )(p1, f1, p1, do, states, a_t, lb, msum, rtri)


def _lb_fwd(logits, *, name):
    W = logits.shape[1]

    def body(l_ref, lb_ref):
        l = l_ref[...]
        m = jnp.max(l, axis=0, keepdims=True)
        e = jnp.exp(l - m)
        p = e / jnp.sum(e, axis=0, keepdims=True)
        lb_ref[...] = (p[0:1] + p[1:2]) - p[0:1]

    return pl.pallas_call(body, out_shape=SDS((1, W), F32), name=name)(logits)


STAT_ROWS = 8


def _stats_reduce(stats_all, logits, *, name):
    W = logits.shape[1]

    def body(s_ref, l_ref, g_ref):
        tot = s_ref[0]
        for d in range(1, N_DEV):
            tot = tot + s_ref[d]
        l = l_ref[...]
        m = jnp.max(l, axis=0, keepdims=True)
        e = jnp.exp(l - m)
        p = e / jnp.sum(e, axis=0, keepdims=True)
        dlb = tot[2:3]
        dl0 = -(p[0:1] * p[1:2]) * dlb
        dl1 = (p[1:2] * (1.0 - p[1:2])) * dlb
        g_ref[0:2] = tot[0:2]
        g_ref[2:3] = dl0
        g_ref[3:4] = dl1
        g_ref[4:7] = tot[3:6]
        g_ref[7:8] = jnp.zeros((1, W), F32)

    return pl.pallas_call(body, out_shape=SDS((STAT_ROWS, W), F32), name=name)(stats_all, logits)


def _adamw(w, m, v, g_parts, *, name, tr=128):
    R, C = w.shape
    ns = len(g_parts)
    n, Rs = g_parts[0].shape[0], g_parts[0].shape[1]
    assert all(p.shape == (n, Rs, C) for p in g_parts) and ns * Rs == R
    tr = min(tr, Rs)
    assert Rs % tr == 0
    nts = Rs // tr
    c1 = 1.0 / (1.0 - ADAM_B1 ** ADAM_STEP)
    c2 = 1.0 / (1.0 - ADAM_B2 ** ADAM_STEP)

    def body(*refs):
        w_ref, m_ref, v_ref = refs[:3]
        g_refs = refs[3:3 + ns]
        go_ref, d_ref, mo_ref, vo_ref = refs[3 + ns:]

        def update(g_ref):
            g = g_ref[0].astype(F32)
            for k in range(1, n):
                g = g + g_ref[k].astype(F32)
            mn = ADAM_B1 * m_ref[...] + (1.0 - ADAM_B1) * g
            vn = ADAM_B2 * v_ref[...] + (1.0 - ADAM_B2) * (g * g)
            d_ref[...] = -ADAM_LR * ((mn * c1) / (jnp.sqrt(vn * c2) + ADAM_EPS) + ADAM_WD * w_ref[...])
            go_ref[...] = g
            mo_ref[...] = mn
            vo_ref[...] = vn

        for s in range(ns):
            if ns == 1:
                update(g_refs[s])
            else:
                pl.when(pl.program_id(0) // nts == s)(functools.partial(update, g_refs[s]))

    def g_map(i, s):
        return (0, jnp.clip(i - s * nts, 0, nts - 1), 0)

    blk = pl.BlockSpec((tr, C), lambda i: (i, 0))
    return pl.pallas_call(
        body, grid=(R // tr,),
        in_specs=[blk, blk, blk] + [pl.BlockSpec((n, tr, C), functools.partial(g_map, s=s)) for s in range(ns)],
        out_specs=[blk] * 4, out_shape=[SDS((R, C), F32)] * 4,
        compiler_params=_params("parallel"), name=name,
    )(w, m, v, *g_parts)


ANY = pl.BlockSpec(memory_space=pl.ANY)
STAGE_BYTES = 2 * 1024 * 1024


def _stage_shape(shape, dtype):
    row_bytes = int(np.prod(shape[1:])) * jnp.dtype(dtype).itemsize
    rows = max(1, min(shape[0], STAGE_BYTES // row_bytes))
    while shape[0] % rows:
        rows -= 1
    return (rows,) + tuple(shape[1:])


def _staged_copy(frm, to, buf, sems):
    rows = buf.shape[0]
    for r0 in range(0, frm.shape[0], rows):
        cp = pltpu.make_async_copy(frm.at[pl.ds(r0, rows)], buf, sems.at[0])
        cp.start()
        cp.wait()
        cp = pltpu.make_async_copy(buf, to.at[pl.ds(r0, rows)], sems.at[1])
        cp.start()
        cp.wait()


def _all_gather(shards, out_shapes, views, *, name):
    n = len(shards)

    def body(*refs):
        ins, outs = refs[:n], refs[n:2 * n]
        send_sems, recv_sems, local_sems = refs[2 * n:2 * n + 3]
        bufs = refs[2 * n + 3:]
        x, y, c = lax.axis_index("x"), lax.axis_index("y"), lax.axis_index("c")
        me, sibling = (x, y, c), (x, y, 1 - c)
        chips = [(1 - x, y), (x, 1 - y), (1 - x, 1 - y)]

        def dev(p):
            return 4 * p[0] + 2 * p[1] + p[2]

        def copy(a, k, block, to, src=None):
            dst = views[a](outs[a], dev(block))
            return pltpu.make_async_remote_copy(
                src_ref=dst if src is None else src, dst_ref=dst,
                send_sem=send_sems.at[a, k], recv_sem=recv_sems.at[a, k],
                device_id=to, device_id_type=MESH)

        first, passed = [], []
        for a in range(n):
            first.append(copy(a, 0, me, sibling, src=ins[a]))
            first += [copy(a, 1 + j, me, (*chip, c), src=ins[a]) for j, chip in enumerate(chips)]
        for cp in first:
            cp.start()
        for a in range(n):
            _staged_copy(ins[a], views[a](outs[a], dev(me)), bufs[a], local_sems)
        for j, chip in enumerate(chips):
            for a in range(n):
                copy(a, 1 + j, (*chip, c), me).wait_recv()
                cp = copy(a, 4 + j, (*chip, c), sibling)
                cp.start()
                passed.append(cp)
        for a in range(n):
            copy(a, 0, sibling, me).wait_recv()
            for j, chip in enumerate(chips):
                copy(a, 4 + j, (*chip, 1 - c), me).wait_recv()
        for cp in first + passed:
            cp.wait_send()

    return pl.pallas_call(
        body, in_specs=[ANY] * n, out_specs=[ANY] * n, out_shape=list(out_shapes),
        scratch_shapes=[pltpu.SemaphoreType.DMA((n, 7)), pltpu.SemaphoreType.DMA((n, 7)),
                        pltpu.SemaphoreType.DMA((2,))]
        + [pltpu.VMEM(_stage_shape(s.shape, s.dtype), s.dtype) for s in shards],
        name=name,
    )(*shards)


HBM = pl.BlockSpec(memory_space=pltpu.HBM)
SEM = pl.BlockSpec(memory_space=pltpu.SEMAPHORE)
EFFECT = pltpu.SideEffectType.DATAFLOW_SIDE_EFFECTING


def _relations(x, y, c):
    for m in range(1, N_DEV):
        yield m, (1 - x if m & 4 else x, 1 - y if m & 2 else y, 1 - c if m & 1 else c)


def _dev_id(p):
    return 4 * p[0] + 2 * p[1] + p[2]


def _send_start(srcs, land_shapes, src_views, dst_views, *, name):
    n = len(srcs)

    def body(*refs):
        ins, lands = refs[:n], refs[n:2 * n]
        send_sems, recv_sems, token = refs[2 * n], refs[2 * n + 1], refs[-1]
        x, y, c = lax.axis_index("x"), lax.axis_index("y"), lax.axis_index("c")
        me = _dev_id((x, y, c))
        for m, p in _relations(x, y, c):
            for a in range(n):
                pltpu.make_async_remote_copy(
                    src_ref=src_views[a](ins[a], me, _dev_id(p), m), dst_ref=dst_views[a](lands[a], me, m),
                    send_sem=send_sems.at[a * (N_DEV - 1) + m - 1], recv_sem=recv_sems.at[a * (N_DEV - 1) + m - 1],
                    device_id=p, device_id_type=MESH).start()
        token[...] = jnp.zeros_like(token)

    lands = [pltpu.with_memory_space_constraint(lax.empty(s.shape, s.dtype), pltpu.HBM) for s in land_shapes]
    srcs = [pltpu.with_memory_space_constraint(v, pltpu.HBM) for v in srcs]
    res = pl.pallas_call(
        body, name=name,
        out_shape=[pltpu.SemaphoreType.DMA((n * (N_DEV - 1),)), pltpu.SemaphoreType.DMA((n * (N_DEV - 1),))]
        + [pltpu.HBM(v.shape, v.dtype) for v in srcs] + [pltpu.HBM(s.shape, s.dtype) for s in land_shapes]
        + [SDS((8, 128), F32)],
        in_specs=[HBM] * (2 * n), out_specs=[SEM, SEM] + [HBM] * (2 * n) + [pl.BlockSpec(memory_space=pltpu.VMEM)],
        input_output_aliases={i: 2 + i for i in range(2 * n)},
        compiler_params=pltpu.CompilerParams(has_side_effects=EFFECT),
    )(*srcs, *lands)
    return res[0], res[1], res[2:2 + n], res[2 + n:2 + 2 * n], res[-1]


def _send_wait(started, src_views, dst_views, own_views, own_shapes, after, *, name):
    send_sems, recv_sems, srcs, lands, _ = started
    n = len(srcs)

    def body(*refs):
        ins, lnd = refs[:n], refs[n:2 * n]
        send_sems, recv_sems = refs[2 * n], refs[2 * n + 1]
        got = refs[2 * n + 3 + n:2 * n + 3 + 2 * n]
        local_sems = refs[2 * n + 3 + 2 * n]
        bufs = refs[2 * n + 4 + 2 * n:]
        x, y, c = lax.axis_index("x"), lax.axis_index("y"), lax.axis_index("c")
        me = _dev_id((x, y, c))
        for m, p in _relations(x, y, c):
            for a in range(n):
                cp = pltpu.make_async_remote_copy(
                    src_ref=src_views[a](ins[a], me, _dev_id(p), m), dst_ref=dst_views[a](lnd[a], me, m),
                    send_sem=send_sems.at[a * (N_DEV - 1) + m - 1], recv_sem=recv_sems.at[a * (N_DEV - 1) + m - 1],
                    device_id=p, device_id_type=MESH)
                cp.wait_send()
                cp.wait_recv()
        for a in range(n):
            frm, to = own_views[a](ins[a], got[a], me)
            _staged_copy(frm, to, bufs[a], local_sems)

    res = pl.pallas_call(
        body, name=name,
        out_shape=[pltpu.HBM(v.shape, v.dtype) for v in srcs] + [pltpu.HBM(v.shape, v.dtype) for v in lands],
        in_specs=[HBM] * (2 * n) + [SEM, SEM, ANY], out_specs=[HBM] * (2 * n),
        input_output_aliases={i: i for i in range(2 * n)},
        scratch_shapes=[pltpu.SemaphoreType.DMA((2,))]
        + [pltpu.VMEM(_stage_shape(s, v.dtype), v.dtype) for s, v in zip(own_shapes, srcs)],
        compiler_params=pltpu.CompilerParams(has_side_effects=EFFECT),
    )(*srcs, *lands, send_sems, recv_sems, after)
    return res[n:]


def kernel(x, norm_gains, fox_w_in, fox_b_f, hgrn_w_in, hgrn_lb_logits, hgrn_onorm, w_out, final_gain, loss_target, m_norm_gains, m_fox_w_in, m_fox_b_f, m_hgrn_w_in, m_hgrn_lb_logits, m_hgrn_onorm, m_w_out, m_final_gain, v_norm_gains, v_fox_w_in, v_fox_b_f, v_hgrn_w_in, v_hgrn_lb_logits, v_hgrn_onorm, v_w_out, v_final_gain):
    _, S, D = x.shape
    H = FOX_HEADS
    W = H * HEAD_DIM
    assert HGRN_HEADS == H and w_out.shape[2] == D
    cf = fox_w_in.shape[2]
    ch = hgrn_w_in.shape[2]
    ro = w_out.shape[1]
    co = hgrn_onorm.shape[1]
    assert N_DEV * cf == 4 * W + H and N_DEV * ch == 4 * W and N_DEV * ro == W and N_DEV * co == W
    x2 = x.reshape(S, D)
    tgt = loss_target.reshape(S, D)

    col = lambda n: (lambda r, i: r.at[:, pl.ds(pl.multiple_of(i * n, n), n)])
    row = lambda n: (lambda r, i: r.at[pl.ds(pl.multiple_of(i * n, n), n), :])
    late_views = [col(ch), row(ro), row(ro), col(co)]
    late = _send_start(
        [hgrn_w_in[0].astype(BF16), w_out[0].astype(BF16), w_out[1].astype(BF16), hgrn_onorm],
        [SDS((D, 4 * W), BF16), SDS((W, D), BF16), SDS((W, D), BF16), SDS((1, W), F32)],
        [lambda r, me, p, m: r] * 4, [lambda r, me, m, v=v: v(r, me) for v in late_views],
        name="gather_later_start")
    ng0 = norm_gains[0:1] + late[4][0:1, 0:1]

    (wf_g,) = _all_gather([fox_w_in[0].astype(BF16)], [SDS((N_DEV, D, cf), BF16)], [lambda r, p: r.at[p]],
                          name="gather_fox_w_in")
    wf = jnp.transpose(wf_g, (1, 0, 2)).reshape(D, N_DEV * cf)
    wf_main = jnp.concatenate([wf[:, :3 * W], wf[:, 3 * W + H:]], axis=1)
    wfl_t = wf[:, 3 * W:3 * W + H].T

    h0 = _rms_fwd(x2, ng0, name="rms0_fwd")
    p0 = _mm_nn([h0], wf_main, BF16, scale_cols=(W, LOG2E * HEAD_DIM ** -0.5), name="fox_in_proj")
    fl_t = _mm_nt_rows(wfl_t, h0, name="fox_forget_proj")
    b_col = fox_b_f.reshape(H, 1)
    kaug = _fox_key_aug(*_fox_gate_fwd(fl_t, b_col, name="fox_gate_fwd"))
    o0, y0, qaug = _fox_fwd(p0, kaug, H=H, name="fox_attn_fwd")
    wh, wo0, wo1, onorm = _send_wait(
        late, [lambda r, me, p, m: r] * 4, [lambda r, me, m, v=v: v(r, me) for v in late_views],
        [lambda src, land, me, v=v: (src, v(land, me)) for v in late_views],
        [(D, ch), (ro, D), (ro, D), (1, co)], y0[0:16], name="gather_later_wait")
    x1 = _mm_nn([y0], wo0, F32, residual=x2, name="fox_out_proj")

    lb = _lb_fwd(hgrn_lb_logits, name="hgrn_lower_bound")
    h1 = _rms_fwd(x1, norm_gains[1:2], name="rms1_fwd")
    p1 = _mm_nn([h1], wh, BF16, b_cols=[(0, W), (2 * W, 4 * W)], name="hgrn_in_proj")
    f1 = _mm_nn([h1], wh, F32, b_cols=[(W, 2 * W)], name="hgrn_forget_proj")
    o1, y1, states, a_t1 = _hgrn_fwd(p1, f1, lb, onorm, H=H, name="hgrn_fwd")
    xo = _mm_nn([y1], wo1, F32, residual=x1, name="hgrn_out_proj")

    dx2, dx2b, loss_part, dgf = _loss_head(xo, final_gain.reshape(1, D), tgt, name="loss_head")
    loss = lax.psum(jnp.sum(loss_part), ("x", "y", "c"))

    dy1 = _mm_nn([dx2b], wo1, BF16, b_t=True, name="hgrn_out_proj_dx")
    dwo1 = _mm_tn(y1, [dx2b], BF16, name="hgrn_out_proj_dw")
    do1, dg1, donorm = _hgrn_post_bwd(dy1, o1, p1, onorm, H=H, name="hgrn_post_bwd")
    dq1, df1, di1, dlb = _hgrn_bwd(p1, f1, lb, do1, states, a_t1, H=H, name="hgrn_bwd")
    segs1 = [dq1, df1, di1, dg1]
    dh1 = _mm_nn(segs1, wh, BF16, b_t=True, tn=D, name="hgrn_in_proj_dx")
    dwh = _mm_tn(h1, segs1, BF16, name="hgrn_in_proj_dw")
    part_views = [col(ch), row(ro)]
    slot = lambda r, me, m: r.at[m]
    ex1 = _send_start([dwh, dwo1], [SDS((N_DEV, D, ch), BF16), SDS((N_DEV, ro, D), BF16)],
                      [lambda r, me, p, m, v=v: v(r, p) for v in part_views], [slot] * 2,
                      name="exchange_layer1_start")
    ng1 = norm_gains[1:2] + ex1[4][0:1, 0:1]
    dx1, dx1b, dng1 = _rms_bwd(x1, ng1, dh1, dx2, name="rms1_bwd")

    dy0 = _mm_nn([dx1b], wo0, BF16, b_t=True, name="fox_out_proj_dx")
    dwo0 = _mm_tn(y0, [dx1b], BF16, name="fox_out_proj_dw")
    do0, dg0, doaug = _fox_post_bwd(dy0, o0, p0, H=H, name="fox_post_bwd")
    dq0, dc_row, dk0, dv0, dc_key = _fox_bwd(p0, kaug, qaug, do0, doaug, H=H, name="fox_attn_bwd")
    dfl_t, dbf = _fox_gate_bwd(dc_row.reshape(H, S), dc_key.reshape(H, S), fl_t, b_col, name="fox_gate_bwd")
    dfl_tb = dfl_t.astype(BF16)
    dwfl_t = _mm_nn([dfl_tb], h0, BF16, name="fox_forget_proj_dw")
    segs0 = [dq0, dk0, dv0, dg0]
    dwf_main = _mm_tn(h0, segs0, BF16, name="fox_in_proj_dw")
    dwf = jnp.concatenate([dwf_main[:, :3 * W], dwfl_t.T, dwf_main[:, 3 * W:]], axis=1)
    dwf_blocks = jnp.transpose(dwf.reshape(D, N_DEV, cf), (1, 0, 2))
    ex0 = _send_start([dwf_blocks, dwo0], [SDS((N_DEV, D, cf), BF16), SDS((N_DEV, ro, D), BF16)],
                      [lambda r, me, p, m: r.at[p], lambda r, me, p, m: row(ro)(r, p)], [slot] * 2,
                      name="exchange_layer0_start")
    wfl_t0 = wfl_t + ex0[4][0:1, 0:1].astype(BF16)
    dh0_f = _mm_nn([dfl_tb.T], wfl_t0, BF16, name="fox_forget_proj_dx")
    dh0 = _mm_nn(segs0, wf_main, BF16, residual=dh0_f, b_t=True, tn=D, name="fox_in_proj_dx")
    grad_x, _, dng0 = _rms_bwd(x2, norm_gains[0:1], dh0, dx1, name="rms0_bwd")

    own1 = [lambda src, land, me, v=v: (v(src, me), land.at[0]) for v in part_views]
    rh, ro1 = _send_wait(ex1, [lambda r, me, p, m, v=v: v(r, p) for v in part_views], [slot] * 2, own1,
                         [(D, ch), (ro, D)], dng0, name="exchange_layer1_wait")
    rf, ro0 = _send_wait(ex0, [lambda r, me, p, m: r.at[p], lambda r, me, p, m: row(ro)(r, p)], [slot] * 2,
                         [lambda src, land, me: (src.at[me], land.at[0]),
                          lambda src, land, me: (row(ro)(src, me), land.at[0])],
                         [(D, cf), (ro, D)], dng0, name="exchange_layer0_wait")

    pad = lambda a: jnp.pad(a, ((0, 0), (0, W - a.shape[1])))
    stats = jnp.concatenate([dng0, dng1, dlb, dgf, pad(dbf.reshape(1, H)), donorm,
                             jnp.zeros((2, W), F32)], axis=0)
    assert D == W
    (stats_all,) = _all_gather([stats], [SDS((N_DEV, STAT_ROWS, W), F32)], [lambda r, p: r.at[p]],
                               name="gather_small_grads")
    g_small = _stats_reduce(stats_all, hgrn_lb_logits, name="reduce_small_grads")
    me = 4 * lax.axis_index("x") + 2 * lax.axis_index("y") + lax.axis_index("c")
    g_onorm = lax.dynamic_slice_in_dim(g_small[6:7], me * co, co, axis=1)

    def upd(w, m, v, parts, name):
        shp = w.shape
        r2 = (-1, shp[-1])
        g, d, mn, vn = _adamw(w.reshape(r2), m.reshape(r2), v.reshape(r2), parts, name=name)
        return g.reshape(shp), d.reshape(shp), mn.reshape(shp), vn.reshape(shp)

    res = {
        "norm_gains": upd(norm_gains, m_norm_gains, v_norm_gains, [g_small[None, 0:2]], "adamw_norm_gains"),
        "fox_w_in": upd(fox_w_in, m_fox_w_in, v_fox_w_in, [rf], "adamw_fox_w_in"),
        "fox_b_f": upd(fox_b_f, m_fox_b_f, v_fox_b_f, [g_small[None, 5:6, :H]], "adamw_fox_b_f"),
        "hgrn_w_in": upd(hgrn_w_in, m_hgrn_w_in, v_hgrn_w_in, [rh], "adamw_hgrn_w_in"),
        "hgrn_lb_logits": upd(hgrn_lb_logits, m_hgrn_lb_logits, v_hgrn_lb_logits, [g_small[None, 2:4]],
                              "adamw_hgrn_lb_logits"),
        "hgrn_onorm": upd(hgrn_onorm, m_hgrn_onorm, v_hgrn_onorm, [g_onorm[None]], "adamw_hgrn_onorm"),
        "w_out": upd(w_out, m_w_out, v_w_out, [ro0, ro1], "adamw_w_out"),
        "final_gain": upd(final_gain.reshape(1, D), m_final_gain.reshape(1, D), v_final_gain.reshape(1, D),
                          [g_small[None, 4:5]], "adamw_final_gain"),
    }
    order = ["norm_gains", "fox_w_in", "fox_b_f", "hgrn_w_in", "hgrn_lb_logits", "hgrn_onorm", "w_out", "final_gain"]
    fix = lambda n, a: a.reshape(D) if n == "final_gain" else a
    outs = [loss, grad_x.reshape(1, S, D)]
    for k in range(4):
        outs += [fix(n, res[n][k]) for n in order]
    return tuple(outs)
```

```python
import functools

import numpy as np
import jax
import jax.numpy as jnp
from jax import lax
from jax.experimental import pallas as pl
from jax.experimental.pallas import tpu as pltpu

F32 = jnp.float32
BF16 = jnp.bfloat16
SDS = jax.ShapeDtypeStruct
MESH = pl.DeviceIdType.MESH

EPS = 1e-6
ADAM_LR, ADAM_B1, ADAM_B2, ADAM_EPS, ADAM_WD, ADAM_STEP = 0.001, 0.9, 0.999, 1e-08, 0.01, 10

N_DEV = 8
FOX_HEADS = 16
HGRN_HEADS = 16
HEAD_DIM = 128
HGRN_CHUNK = 128
HGRN_LEAF = 16
HGRN_HEADS_PER_STEP = 8
EXP_CLAMP = 85.0
ATT_BLOCK = 512
ATT_HEADS_PER_STEP = 4
ATT_BWD_HEADS_PER_STEP = 2
POST_HEADS_PER_STEP = 4
NEG = -1e30
LOG2E = 1.4426950408889634
LN2 = 0.6931471805599453

VMEM_LIMIT_V7X = 56 * 1024 * 1024


def _params(*sem):
    return pltpu.CompilerParams(dimension_semantics=sem, vmem_limit_bytes=VMEM_LIMIT_V7X)


def _silu(x):
    return x * jax.nn.sigmoid(x)


def _dsilu(x):
    s = jax.nn.sigmoid(x)
    return s * (1.0 + x * (1.0 - s))


def _dot(a, b):
    return jnp.dot(a, b, preferred_element_type=F32)


def _dot_nt(a, b):
    return lax.dot_general(a, b, (((1,), (1,)), ((), ())), preferred_element_type=F32)


def _dot_tn(a, b):
    return lax.dot_general(a, b, (((0,), (0,)), ((), ())), preferred_element_type=F32)


def _mm_nn(a_list, b, out_dtype, *, name, residual=None, scale_cols=None, b_t=False, b_cols=None,
           tm=1024, tn=1024, tk=2048):
    ns = len(a_list)
    M, Ks = a_list[0].shape
    K, N = (b.shape[1], b.shape[0]) if b_t else b.shape
    if b_cols is None:
        b_cols = [(0, N)]
    else:
        assert not b_t
        N = sum(e - s for s, e in b_cols)
    dot = _dot_nt if b_t else _dot
    assert K == ns * Ks and all(a.shape == (M, Ks) for a in a_list)
    if ns > 1:
        tk = tk // 2
    tm, tn, tk = min(tm, M), min(tn, N), min(tk, Ks)
    assert M % tm == 0 and N % tn == 0 and Ks % tk == 0
    assert scale_cols is None or scale_cols[0] % tn == 0
    assert all(s % tn == 0 and e % tn == 0 for s, e in b_cols)
    nks = Ks // tk
    nk = ns * nks
    has_res = residual is not None

    def body(*refs):
        a_refs, b_ref = refs[:ns], refs[ns]
        res_ref = refs[ns + 1] if has_res else None
        o_ref = refs[ns + 1 + has_res]

        def finish(r):
            if has_res:
                r = r + res_ref[...].astype(F32)
            if scale_cols is not None:
                r = r * jnp.where(pl.program_id(1) < scale_cols[0] // tn, scale_cols[1], 1.0)
            o_ref[...] = r.astype(out_dtype)

        if nk == 1:
            finish(dot(a_refs[0][...], b_ref[...]))
            return
        acc_ref = refs[ns + 2 + has_res]
        k = pl.program_id(2)

        @pl.when(k == 0)
        def _():
            acc_ref[...] = jnp.zeros_like(acc_ref)

        for s in range(ns):
            def step(s=s):
                acc_ref[...] += dot(a_refs[s][...], b_ref[...])

            if ns == 1:
                step()
            else:
                pl.when(k // nks == s)(step)

        @pl.when(k == nk - 1)
        def _():
            finish(acc_ref[...])

    def a_map(i, j, k, s):
        return (i, jnp.clip(k - s * nks, 0, nks - 1))

    in_specs = [pl.BlockSpec((tm, tk), functools.partial(a_map, s=s)) for s in range(ns)]
    def b_col(j):
        src = j + b_cols[0][0] // tn
        for (_, e0), (s1, _) in zip(b_cols[:-1], b_cols[1:]):
            src = src + jnp.where(src >= e0 // tn, (s1 - e0) // tn, 0)
        return src

    if b_t:
        in_specs.append(pl.BlockSpec((tn, tk), lambda i, j, k: (j, k)))
    else:
        in_specs.append(pl.BlockSpec((tk, tn), lambda i, j, k: (k, b_col(j))))
    args = list(a_list) + [b]
    if has_res:
        in_specs.append(pl.BlockSpec((tm, tn), lambda i, j, k: (i, j)))
        args.append(residual)
    return pl.pallas_call(
        body, grid=(M // tm, N // tn, nk), in_specs=in_specs,
        out_specs=pl.BlockSpec((tm, tn), lambda i, j, k: (i, j)),
        out_shape=SDS((M, N), out_dtype),
        scratch_shapes=[] if nk == 1 else [pltpu.VMEM((tm, tn), F32)],
        compiler_params=_params("parallel", "parallel", "arbitrary"), name=name,
    )(*args)


def _mm_tn(a, b_list, out_dtype, *, name, tm=2048, tn=1024, tk=512):
    ns = len(b_list)
    S, M = a.shape
    Ns = b_list[0].shape[1]
    assert all(b.shape == (S, Ns) for b in b_list)
    tm, tn, tk = min(tm, M), min(tn, Ns), min(tk, S)
    assert M % tm == 0 and Ns % tn == 0 and S % tk == 0
    njs = Ns // tn
    nk = S // tk

    def body(*refs):
        a_ref, b_refs, o_ref, acc_ref = refs[0], refs[1:1 + ns], refs[1 + ns], refs[2 + ns]
        j, k = pl.program_id(1), pl.program_id(2)

        @pl.when(k == 0)
        def _():
            acc_ref[...] = jnp.zeros_like(acc_ref)

        for s in range(ns):
            def step(s=s):
                acc_ref[...] += _dot_tn(a_ref[...], b_refs[s][...])

            if ns == 1:
                step()
            else:
                pl.when(j // njs == s)(step)

        @pl.when(k == nk - 1)
        def _():
            o_ref[...] = acc_ref[...].astype(out_dtype)

    def b_map(i, j, k, s):
        return (k, jnp.clip(j - s * njs, 0, njs - 1))

    in_specs = [pl.BlockSpec((tk, tm), lambda i, j, k: (k, i))]
    in_specs += [pl.BlockSpec((tk, tn), functools.partial(b_map, s=s)) for s in range(ns)]
    return pl.pallas_call(
        body, grid=(M // tm, ns * njs, nk), in_specs=in_specs,
        out_specs=pl.BlockSpec((tm, tn), lambda i, j, k: (i, j)),
        out_shape=SDS((M, ns * Ns), out_dtype),
        scratch_shapes=[pltpu.VMEM((tm, tn), F32)],
        compiler_params=_params("parallel", "parallel", "arbitrary"), name=name,
    )(a, *b_list)


def _mm_nt_rows(w_t, h, *, name, tn=1024):
    R, K = w_t.shape
    S = h.shape[0]
    tn = min(tn, S)

    def body(w_ref, h_ref, o_ref):
        o_ref[...] = _dot_nt(w_ref[...], h_ref[...])

    return pl.pallas_call(
        body, grid=(S // tn,),
        in_specs=[pl.BlockSpec((R, K), lambda i: (0, 0)), pl.BlockSpec((tn, K), lambda i: (i, 0))],
        out_specs=pl.BlockSpec((R, tn), lambda i: (0, i)),
        out_shape=SDS((R, S), F32), compiler_params=_params("parallel"), name=name,
    )(w_t, h)


def _rms_fwd(x, gain, *, name, tm=512):
    S, D = x.shape
    tm = min(tm, S)

    def body(x_ref, g_ref, h_ref):
        xv = x_ref[...]
        r = lax.rsqrt(jnp.mean(xv * xv, axis=-1, keepdims=True) + EPS)
        h_ref[...] = ((xv * r) * g_ref[...]).astype(BF16)

    return pl.pallas_call(
        body, grid=(S // tm,),
        in_specs=[pl.BlockSpec((tm, D), lambda i: (i, 0)), pl.BlockSpec((1, D), lambda i: (0, 0))],
        out_specs=pl.BlockSpec((tm, D), lambda i: (i, 0)),
        out_shape=SDS((S, D), BF16), compiler_params=_params("parallel"), name=name,
    )(x, gain)


def _rms_bwd(x, gain, dh, dres, *, name, tm=256):
    S, D = x.shape
    tm = min(tm, S)

    def body(x_ref, g_ref, dh_ref, dres_ref, dx_ref, dxb_ref, dg_ref):
        @pl.when(pl.program_id(0) == 0)
        def _():
            dg_ref[...] = jnp.zeros_like(dg_ref)

        xv = x_ref[...]
        r = lax.rsqrt(jnp.mean(xv * xv, axis=-1, keepdims=True) + EPS)
        xh = xv * r
        dhv = dh_ref[...].astype(F32)
        dg_ref[...] += jnp.sum(dhv * xh, axis=0, keepdims=True)
        dxh = dhv * g_ref[...]
        dx = r * (dxh - xh * jnp.mean(dxh * xh, axis=-1, keepdims=True)) + dres_ref[...]
        dx_ref[...] = dx
        dxb_ref[...] = dx.astype(BF16)

    row = pl.BlockSpec((tm, D), lambda i: (i, 0))
    vec = pl.BlockSpec((1, D), lambda i: (0, 0))
    return pl.pallas_call(
        body, grid=(S // tm,), in_specs=[row, vec, row, row], out_specs=[row, row, vec],
        out_shape=[SDS((S, D), F32), SDS((S, D), BF16), SDS((1, D), F32)],
        compiler_params=_params("arbitrary"), name=name,
    )(x, gain, dh, dres)


def _loss_head(x, gain, target, *, name, tm=256):
    S, D = x.shape
    tm = min(tm, S)
    assert tm % 8 == 0 and D % 128 == 0

    def body(x_ref, g_ref, t_ref, dx_ref, dxb_ref, loss_ref, dg_ref):
        @pl.when(pl.program_id(0) == 0)
        def _():
            dg_ref[...] = jnp.zeros_like(dg_ref)
            loss_ref[...] = jnp.zeros_like(loss_ref)

        xv = x_ref[...]
        g = g_ref[...]
        r = lax.rsqrt(jnp.mean(xv * xv, axis=-1, keepdims=True) + EPS)
        xh = xv * r
        err = xh * g - t_ref[...]
        e2 = (err * err).reshape(tm // 8, 8, D).sum(axis=0)
        part = e2[:, 0:128]
        for k in range(1, D // 128):
            part = part + e2[:, k * 128:(k + 1) * 128]
        loss_ref[...] += part * (0.5 / D)
        dy = err * (1.0 / D)
        dg_ref[...] += jnp.sum(dy * xh, axis=0, keepdims=True)
        dxh = dy * g
        dx = r * (dxh - xh * jnp.mean(dxh * xh, axis=-1, keepdims=True))
        dx_ref[...] = dx
        dxb_ref[...] = dx.astype(BF16)

    row = pl.BlockSpec((tm, D), lambda i: (i, 0))
    vec = pl.BlockSpec((1, D), lambda i: (0, 0))
    return pl.pallas_call(
        body, grid=(S // tm,), in_specs=[row, vec, row],
        out_specs=[row, row, pl.BlockSpec((8, 128), lambda i: (0, 0)), vec],
        out_shape=[SDS((S, D), F32), SDS((S, D), BF16), SDS((8, 128), F32), SDS((1, D), F32)],
        compiler_params=_params("arbitrary"), name=name,
    )(x, gain, target)


def _split3(x):
    hi = x.astype(BF16)
    r1 = x - hi.astype(F32)
    mid = r1.astype(BF16)
    lo = (r1 - mid.astype(F32)).astype(BF16)
    return hi, mid, lo


def _split2(x):
    hi = x.astype(BF16)
    lo = (x - hi.astype(F32)).astype(BF16)
    return hi, lo


def _fox_gate_fwd(fl_t, b_col, *, name):
    H, S = fl_t.shape
    L = 128
    tri = jnp.asarray(np.triu(np.ones((L, L), np.float32)), BF16)

    def body(fl_ref, b_ref, tri_ref, hi_ref, mid_ref, lo_ref, carry):
        @pl.when(pl.program_id(0) == 0)
        def _():
            carry[...] = jnp.zeros_like(carry)

        z = fl_ref[...] + b_ref[...]
        lf = jnp.minimum(z, 0.0) - jnp.log(1.0 + jnp.exp(-jnp.abs(z)))
        hi, mid, lo = _split3(lf)
        t = tri_ref[...]
        c = (_dot(hi, t) + _dot(mid, t)) + _dot(lo, t) + carry[...]
        carry[...] = c[:, L - 1:L]
        hi_ref[...], mid_ref[...], lo_ref[...] = _split3(c * (-LOG2E))

    blk = pl.BlockSpec((H, L), lambda i: (0, i))
    return pl.pallas_call(
        body, grid=(S // L,),
        in_specs=[blk, pl.BlockSpec((H, 1), lambda i: (0, 0)), pl.BlockSpec((L, L), lambda i: (0, 0))],
        out_specs=[blk] * 3, out_shape=[SDS((H, S), BF16)] * 3, scratch_shapes=[pltpu.VMEM((H, 1), F32)],
        compiler_params=_params("arbitrary"), name=name,
    )(fl_t, b_col, tri)


def _fox_gate_bwd(dc_row, dc_key, fl_t, b_col, *, name):
    H, S = fl_t.shape
    L = 128
    n = S // L
    tri = jnp.asarray(np.tril(np.ones((L, L), np.float32)), BF16)

    def body(dcr_ref, dck_ref, fl_ref, b_ref, tri_ref, dfl_ref, db_ref, carry):
        @pl.when(pl.program_id(0) == 0)
        def _():
            carry[...] = jnp.zeros_like(carry)
            db_ref[...] = jnp.zeros_like(db_ref)

        hi, mid, lo = _split3(dcr_ref[...] + dck_ref[...])
        t = tri_ref[...]
        dlf = (_dot(hi, t) + _dot(mid, t)) + _dot(lo, t) + carry[...]
        carry[...] = dlf[:, 0:1]
        z = fl_ref[...] + b_ref[...]
        dfl = dlf * jax.nn.sigmoid(-z)
        dfl_ref[...] = dfl
        db_ref[...] += jnp.sum(dfl, axis=1, keepdims=True)

    blk = pl.BlockSpec((H, L), lambda i: (0, n - 1 - i))
    col = pl.BlockSpec((H, 1), lambda i: (0, 0))
    return pl.pallas_call(
        body, grid=(n,), in_specs=[blk, blk, blk, col, pl.BlockSpec((L, L), lambda i: (0, 0))],
        out_specs=[blk, col], out_shape=[SDS((H, S), F32), SDS((H, 1), F32)],
        scratch_shapes=[pltpu.VMEM((H, 1), F32)], compiler_params=_params("arbitrary"), name=name,
    )(dc_row, dc_key, fl_t, b_col, tri)


AUG = HEAD_DIM


def _lane_select(cols, shape):
    lane = lax.broadcasted_iota(jnp.int32, shape, 1)
    out = jnp.zeros(shape, BF16)
    for k, c in reversed(list(enumerate(cols))):
        c = jnp.full(shape, c, BF16) if isinstance(c, (int, float)) else jnp.broadcast_to(c, shape).astype(BF16)
        out = jnp.where(lane == k, c, out)
    return out


def _fox_key_aug(b_hi, b_mid, b_lo):
    H, S = b_hi.shape
    ones = jnp.ones((H, S), BF16)
    ka = jnp.stack([b_hi, b_mid, b_lo, ones, ones, ones], axis=-1)
    ka = jnp.pad(ka, ((0, 0), (0, 0), (0, AUG - 6)))
    return jnp.transpose(ka, (1, 0, 2)).reshape(S, H * AUG)


def _fox_fwd(p0, kaug, *, H, name):
    S = p0.shape[0]
    T = min(ATT_BLOCK, S)
    nq = S // T
    dh = HEAD_DIM
    G = ATT_HEADS_PER_STEP
    assert H % G == 0

    def body(q_ref, k_ref, ka_ref, v_ref, g_ref, o_ref, y_ref, qa_ref, m_sc, acc_sc, p_sc, al_sc):
        i = pl.program_id(1)
        qaug = _lane_select([1.0, 1.0, 1.0], (T, AUG))
        ones = jnp.ones((T, dh), BF16)
        m_sc[...] = jnp.full_like(m_sc, NEG)
        acc_sc[...] = jnp.zeros_like(acc_sc)

        def step(j, before, masked):
            rows = pl.ds(pl.multiple_of(j * T, T), T)
            for g in range(G):
                hd = slice(g * dh, (g + 1) * dh)
                if not masked:
                    prev = pl.ds(pl.multiple_of(before * T, T), T)
                    vp = jnp.concatenate([v_ref[prev, hd], ones], axis=1)
                    acc_sc[g] = jnp.tile(al_sc[g], (1, 2)) * acc_sc[g] + _dot(p_sc[g], vp)
                q = jnp.concatenate([q_ref[:, hd], qaug], axis=1)
                kj = jnp.concatenate([k_ref[rows, hd], ka_ref[rows, hd]], axis=1)
                t = _dot_nt(q, kj)
                if masked:
                    row = lax.broadcasted_iota(jnp.int32, (T, T), 0)
                    col = lax.broadcasted_iota(jnp.int32, (T, T), 1)
                    t = jnp.where(row >= col, t, NEG)
                m_prev = m_sc[g]
                m_new = jnp.maximum(m_prev, jnp.max(t, axis=-1, keepdims=True))
                p_sc[g] = jnp.exp2(t - jnp.tile(m_new, (1, T // 128))).astype(BF16)
                al_sc[g] = jnp.exp2(m_prev - m_new)
                m_sc[g] = m_new

        step(i, None, True)

        def loop_body(j, carry):
            step(j, jnp.where(j == 0, i, j - 1), False)
            return carry

        lax.fori_loop(0, i, loop_body, 0)
        rows = pl.ds(pl.multiple_of(jnp.where(i == 0, 0, i - 1) * T, T), T)
        for g in range(G):
            hd = slice(g * dh, (g + 1) * dh)
            vp = jnp.concatenate([v_ref[rows, hd], ones], axis=1)
            acc = jnp.tile(al_sc[g], (1, 2)) * acc_sc[g] + _dot(p_sc[g], vp)
            l = acc[:, dh:]
            o = acc[:, :dh] / l
            o_ref[:, hd] = o
            y_ref[:, hd] = (o * _silu(g_ref[:, hd].astype(F32))).astype(BF16)
            hi, mid, lo = _split3(-(m_sc[g] + jnp.log2(l)))
            qa_ref[:, hd] = _lane_select([1.0, 1.0, 1.0, hi, mid, lo], (T, AUG))

    blk = lambda off: pl.BlockSpec((T, G * dh), lambda h, i: (i, off // G + h))
    full = lambda off: pl.BlockSpec((S, G * dh), lambda h, i: (0, off // G + h), pipeline_mode=pl.Buffered(1))
    return pl.pallas_call(
        body, grid=(H // G, nq),
        in_specs=[blk(0), full(H), full(0), full(2 * H), blk(3 * H)],
        out_specs=[blk(0), blk(0), blk(0)],
        out_shape=[SDS((S, H * dh), F32), SDS((S, H * dh), BF16), SDS((S, H * AUG), BF16)],
        scratch_shapes=[pltpu.VMEM((G, T, 128), F32), pltpu.VMEM((G, T, 2 * dh), F32),
                        pltpu.VMEM((G, T, T), BF16), pltpu.VMEM((G, T, 128), F32)],
        compiler_params=_params("parallel", "arbitrary"), name=name,
    )(p0, p0, kaug, p0, p0)


def _fox_post_bwd(dy, o, p0, *, H, name, tm=512):
    S = dy.shape[0]
    dh = HEAD_DIM
    tm = min(tm, S)
    G = POST_HEADS_PER_STEP
    assert H % G == 0

    def body(dy_ref, o_ref, g_ref, do_ref, dg_ref, da_ref):
        dyv = dy_ref[...].astype(F32)
        ov = o_ref[...]
        g = g_ref[...].astype(F32)
        do = (dyv * _silu(g)).astype(BF16)
        do_ref[...] = do
        dg_ref[...] = (dyv * ov * _dsilu(g)).astype(BF16)
        prod = do.astype(F32) * ov
        for k in range(G):
            hd = slice(k * dh, (k + 1) * dh)
            delta = jnp.sum(prod[:, hd], axis=-1, keepdims=True)
            hi, mid, lo = _split3(-jnp.broadcast_to(delta, (tm, AUG)))
            da_ref[:, hd] = _lane_select([hi, mid, lo], (tm, AUG))

    blk = pl.BlockSpec((tm, G * dh), lambda h, i: (i, h))
    return pl.pallas_call(
        body, grid=(H // G, S // tm),
        in_specs=[blk, blk, pl.BlockSpec((tm, G * dh), lambda h, i: (i, 3 * H // G + h))],
        out_specs=[blk, blk, blk],
        out_shape=[SDS((S, H * dh), BF16), SDS((S, H * dh), BF16), SDS((S, H * AUG), BF16)],
        compiler_params=_params("parallel", "parallel"), name=name,
    )(dy, o, p0)


def _fox_bwd(p0, kaug, qaug, do, doaug, *, H, name):
    S = p0.shape[0]
    T = min(ATT_BLOCK, S)
    nq = S // T
    dh = HEAD_DIM
    scale = dh ** -0.5
    G = ATT_BWD_HEADS_PER_STEP
    assert H % G == 0

    def body(q_ref, qa_ref, k_ref, ka_ref, v_ref, do_ref, da_ref, dq_ref, rs_ref, dk_ref, dv_ref, dc_ref,
             dq_sc, dk_sc, dv_sc, pt_sc, dst_sc):
        j = pl.program_id(1)
        vaug = _lane_select([1.0, 1.0, 1.0], (T, AUG))
        ones = jnp.ones((T, dh), BF16)

        @pl.when(j == 0)
        def _():
            dq_sc[...] = jnp.zeros_like(dq_sc)

        dk_sc[...] = jnp.zeros_like(dk_sc)
        dv_sc[...] = jnp.zeros_like(dv_sc)

        def apply(prev):
            for g in range(G):
                hd = slice(g * dh, (g + 1) * dh)
                dv_sc[g] += _dot(pt_sc[g], do_ref[prev, hd])
                dk_sc[g] += _dot(dst_sc[g], jnp.concatenate([q_ref[prev, hd], ones], axis=1))
                dq_sc[g, prev] += _dot_tn(dst_sc[g], jnp.concatenate([k_ref[:, hd], ones], axis=1))

        def step(i, masked):
            rows = pl.ds(pl.multiple_of(i * T, T), T)
            if not masked:
                apply(pl.ds(pl.multiple_of((i - 1) * T, T), T))
            for g in range(G):
                hd = slice(g * dh, (g + 1) * dh)
                k = jnp.concatenate([k_ref[:, hd], ka_ref[:, hd]], axis=1)
                v = jnp.concatenate([v_ref[:, hd], vaug], axis=1)
                pt = jnp.exp2(_dot_nt(k, jnp.concatenate([q_ref[rows, hd], qa_ref[rows, hd]], axis=1)))
                if masked:
                    row = lax.broadcasted_iota(jnp.int32, (T, T), 0)
                    col = lax.broadcasted_iota(jnp.int32, (T, T), 1)
                    pt = jnp.where(col >= row, pt, 0.0)
                dst = pt * _dot_nt(v, jnp.concatenate([do_ref[rows, hd], da_ref[rows, hd]], axis=1))
                pt_sc[g] = pt.astype(BF16)
                dst_sc[g] = dst.astype(BF16)

        step(j, True)

        def loop_body(i, carry):
            step(i, False)
            return carry

        lax.fori_loop(j + 1, nq, loop_body, 0)
        apply(pl.ds((nq - 1) * T, T))
        for g in range(G):
            hd = slice(g * dh, (g + 1) * dh)
            dk_ref[:, hd] = (dk_sc[g, :, :dh] * LN2).astype(BF16)
            dv_ref[:, hd] = dv_sc[g].astype(BF16)
            dc_ref[g] = -jnp.transpose(dk_sc[g, :, dh:])[0:1]

        @pl.when(j == nq - 1)
        def _():
            for g in range(G):
                dq_ref[:, g * dh:(g + 1) * dh] = (dq_sc[g, :, :dh] * scale).astype(BF16)
                for i in range(nq):
                    rs_ref[g, :, i * T:(i + 1) * T] = jnp.transpose(dq_sc[g, i * T:(i + 1) * T, dh:])[0:1]

    blk = lambda off: pl.BlockSpec((T, G * dh), lambda h, j: (j, off // G + h))
    full = lambda off: pl.BlockSpec((S, G * dh), lambda h, j: (0, off // G + h))
    once = lambda off: pl.BlockSpec((S, G * dh), lambda h, j: (0, off // G + h), pipeline_mode=pl.Buffered(1))
    rowv = pl.BlockSpec((G, 1, T), lambda h, j: (h, 0, j))
    return pl.pallas_call(
        body, grid=(H // G, nq),
        in_specs=[once(0), once(0), blk(H), blk(0), blk(2 * H), once(0), once(0)],
        out_specs=[full(0), pl.BlockSpec((G, 1, S), lambda h, j: (h, 0, 0)), blk(0), blk(0), rowv],
        out_shape=[SDS((S, H * dh), BF16), SDS((H, 1, S), F32), SDS((S, H * dh), BF16), SDS((S, H * dh), BF16),
                   SDS((H, 1, S), F32)],
        scratch_shapes=[pltpu.VMEM((G, S, 2 * dh), F32), pltpu.VMEM((G, T, 2 * dh), F32), pltpu.VMEM((G, T, dh), F32),
                        pltpu.VMEM((G, T, T), BF16), pltpu.VMEM((G, T, T), BF16)],
        compiler_params=_params("parallel", "arbitrary"), name=name,
    )(p0, qaug, p0, kaug, p0, do, doaug)


def _hgrn_levels(C, leaf):
    levels = []
    h = C // 2
    while h >= leaf:
        levels.append(h)
        h //= 2
    return levels


def _hgrn_sum_matrix(C, leaf):
    t = np.arange(C)[:, None]
    u = np.arange(C)[None, :]
    mats = [(u <= t), (u > t)]
    for h in _hgrn_levels(C, leaf):
        start = (t // (2 * h)) * (2 * h)
        mid = start + h - 1
        second = t > mid
        m = np.where(second, (u > mid) & (u <= t), (u > t) & (u <= mid))
        mats.append(m)
    lstart = (t // leaf) * leaf
    mats.append((u >= lstart) & (u <= t))
    return np.concatenate([m.astype(np.float32) for m in mats], axis=0)


def _hgrn_chunk_terms(qr, fz, lb, msum, C, leaf):
    levels = _hgrn_levels(C, leaf)
    sq = _silu(qr)
    sp = 1.0 / (1.0 + jnp.exp(-fz))
    sn = 1.0 / (1.0 + jnp.exp(fz))
    f = lb + (1.0 - lb) * sp
    lf = jnp.log(f)
    k = (1.0 - lb) * sn
    hi, lo = _split2(lf)
    dsum = _dot(msum, hi) + _dot(msum, lo)
    b = dsum[0:C]
    kdec = dsum[C:2 * C]
    rowi = lax.broadcasted_iota(jnp.int32, (C, 1), 0)
    lev = []
    for n, h in enumerate(levels):
        e = jnp.exp(dsum[(2 + n) * C:(3 + n) * C])
        selq = jnp.where((rowi % (2 * h)) >= h, 1.0, 0.0)
        qm = (sq * e * selq).astype(BF16)
        km = (k * e * (1.0 - selq)).astype(BF16)
        lev.append((h, e, selq, qm, km))
    dleaf = dsum[(2 + len(levels)) * C:(3 + len(levels)) * C]
    eq = jnp.exp(dleaf)
    ek = jnp.exp(jnp.minimum(-dleaf, EXP_CLAMP))
    return dict(sq=sq, sp=sp, sn=sn, f=f, k=k, b=b, kdec=kdec, lev=lev, eq=eq, ek=ek,
                ql=(sq * eq).astype(BF16), kl=(k * ek).astype(BF16),
                qs=(sq * jnp.exp(b)).astype(BF16), ke=(k * jnp.exp(kdec)).astype(BF16),
                e_c=jnp.exp(b[C - 1:C, :]))


def _hgrn_masks(C, leaf, transposed):
    a = lax.broadcasted_iota(jnp.int32, (C, C), 0)
    bb = lax.broadcasted_iota(jnp.int32, (C, C), 1)
    t, s = (bb, a) if transposed else (a, bb)
    lev = [None if 2 * h == C else (t // (2 * h)) == (s // (2 * h)) for h in _hgrn_levels(C, leaf)]
    if leaf == C:
        leafm = s <= t
    else:
        leafm = ((t // leaf) == (s // leaf)) & (s <= t)
    return lev, leafm


def _hgrn_fwd(p1, f1, lb, onorm, *, H, name, tb=512):
    S = p1.shape[0]
    dk = HEAD_DIM
    C = min(HGRN_CHUNK, S)
    leaf = min(HGRN_LEAF, C)
    tb = min(tb, S)
    nc = tb // C
    G = HGRN_HEADS_PER_STEP
    assert H % G == 0
    msum = jnp.asarray(_hgrn_sum_matrix(C, leaf), BF16)

    def body(q_ref, f_ref, v_ref, g_ref, lb_ref, on_ref, ms_ref, o_ref, y_ref, st_ref, at_ref, st_sc):
        @pl.when(pl.program_id(1) == 0)
        def _():
            st_sc[...] = jnp.zeros_like(st_sc)

        msv = ms_ref[...]
        lmask, leafm = _hgrn_masks(C, leaf, False)

        def chunk(n, carry):
            rows = pl.ds(pl.multiple_of(n * C, C), C)
            for g in range(G):
                hd = slice(g * dk, (g + 1) * dk)
                tm = _hgrn_chunk_terms(q_ref[rows, hd].astype(F32), f_ref[rows, hd], lb_ref[:, hd], msv, C, leaf)
                v = v_ref[rows, hd]
                st = st_sc[g]
                st_ref[g, n] = st
                a = jnp.where(leafm, _dot_nt(tm["ql"], tm["kl"]), 0.0)
                for (h, e, selq, qm, km), m in zip(tm["lev"], lmask):
                    al = _dot_nt(qm, km)
                    a = a + (al if m is None else jnp.where(m, al, 0.0))
                at_ref[g, n] = jnp.transpose(a).astype(BF16)
                o = _dot_nt(tm["qs"], st.astype(BF16)) + _dot(a.astype(BF16), v)
                st_sc[g] = st * tm["e_c"] + _dot(v.T, tm["ke"])
                o_ref[rows, hd] = o
                rn = lax.rsqrt(jnp.mean(o * o, axis=-1, keepdims=True) + EPS)
                y = ((o * rn) * on_ref[:, hd]) * _silu(g_ref[rows, hd].astype(F32))
                y_ref[rows, hd] = y.astype(BF16)
            return carry

        lax.fori_loop(0, nc, chunk, 0)

    blk = lambda off: pl.BlockSpec((tb, G * dk), lambda h, i: (i, off // G + h))
    vec = pl.BlockSpec((1, G * dk), lambda h, i: (0, h))
    return pl.pallas_call(
        body, grid=(H // G, S // tb),
        in_specs=[blk(0), blk(0), blk(H), blk(2 * H), vec, vec,
                  pl.BlockSpec(msum.shape, lambda h, i: (0, 0))],
        out_specs=[blk(0), blk(0), pl.BlockSpec((G, nc, dk, dk), lambda h, i: (h, i, 0, 0)),
                   pl.BlockSpec((G, nc, C, C), lambda h, i: (h, i, 0, 0))],
        out_shape=[SDS((S, H * dk), F32), SDS((S, H * dk), BF16), SDS((H, S // C, dk, dk), F32),
                   SDS((H, S // C, C, C), BF16)],
        scratch_shapes=[pltpu.VMEM((G, dk, dk), F32)],
        compiler_params=_params("parallel", "arbitrary"), name=name,
    )(p1, f1, p1, p1, lb, onorm, msum)


def _hgrn_post_bwd(dy, o, p1, onorm, *, H, name, tm=512):
    S = dy.shape[0]
    dk = HEAD_DIM
    tm = min(tm, S)
    G = POST_HEADS_PER_STEP
    assert H % G == 0

    def body(dy_ref, o_ref, g_ref, on_ref, do_ref, dg_ref, don_ref):
        @pl.when(pl.program_id(1) == 0)
        def _():
            don_ref[...] = jnp.zeros_like(don_ref)

        for k in range(G):
            hd = slice(k * dk, (k + 1) * dk)
            dyv = dy_ref[:, hd].astype(F32)
            ov = o_ref[:, hd]
            g = g_ref[:, hd].astype(F32)
            onv = on_ref[:, hd]
            rn = lax.rsqrt(jnp.mean(ov * ov, axis=-1, keepdims=True) + EPS)
            oh = ov * rn
            dn = dyv * _silu(g)
            dg_ref[:, hd] = (dyv * (oh * onv) * _dsilu(g)).astype(BF16)
            don_ref[:, hd] += jnp.sum(dn * oh, axis=0, keepdims=True)
            doh = dn * onv
            do_ref[:, hd] = (rn * (doh - oh * jnp.mean(doh * oh, axis=-1, keepdims=True))).astype(BF16)

    blk = pl.BlockSpec((tm, G * dk), lambda h, i: (i, h))
    vec = pl.BlockSpec((1, G * dk), lambda h, i: (0, h))
    return pl.pallas_call(
        body, grid=(H // G, S // tm),
        in_specs=[blk, blk, pl.BlockSpec((tm, G * dk), lambda h, i: (i, 2 * H // G + h)), vec],
        out_specs=[blk, blk, vec],
        out_shape=[SDS((S, H * dk), BF16), SDS((S, H * dk), BF16), SDS((1, H * dk), F32)],
        compiler_params=_params("parallel", "arbitrary"), name=name,
    )(dy, o, p1, onorm)


def _hgrn_bwd(p1, f1, lb, do, states, a_t, *, H, name, tb=512):
    S = p1.shape[0]
    dk = HEAD_DIM
    C = min(HGRN_CHUNK, S)
    leaf = min(HGRN_LEAF, C)
    tb = min(tb, S)
    nc = tb // C
    nb = S // tb
    G = HGRN_HEADS_PER_STEP
    assert H % G == 0
    msum = jnp.asarray(_hgrn_sum_matrix(C, leaf), BF16)
    rtri = jnp.asarray(np.triu(np.ones((C, C), np.float32)), BF16)

    def body(q_ref, f_ref, v_ref, do_ref, st_ref, at_ref, lb_ref, ms_ref, rt_ref,
             dq_ref, df_ref, dv_ref, dlb_ref, g_sc):
        @pl.when(pl.program_id(1) == 0)
        def _():
            g_sc[...] = jnp.zeros_like(g_sc)
            dlb_ref[...] = jnp.zeros_like(dlb_ref)

        msv = ms_ref[...]
        rtv = rt_ref[...]
        lmask, leafm = _hgrn_masks(C, leaf, False)
        lmask_t, leafm_t = _hgrn_masks(C, leaf, True)
        f32 = lambda z: z.astype(F32)

        def head_chunk(g, n):
            hd = slice(g * dk, (g + 1) * dk)
            rows = pl.ds(pl.multiple_of(n * C, C), C)
            lbv = lb_ref[:, hd]
            qr = q_ref[rows, hd].astype(F32)
            tm = _hgrn_chunk_terms(qr, f_ref[rows, hd], lbv, msv, C, leaf)
            v = v_ref[rows, hd]
            dov = do_ref[rows, hd]
            st0 = st_ref[g, n]
            gt = g_sc[g]
            gtb = gt.astype(BF16)
            da = _dot_nt(dov, v)
            da_t = _dot_nt(v, dov)

            dal = jnp.where(leafm, da, 0.0).astype(BF16)
            dal_t = jnp.where(leafm_t, da_t, 0.0).astype(BF16)
            dql = _dot(dal, tm["kl"])
            dkl = _dot(dal_t, tm["ql"])
            dsq = dql * tm["eq"]
            dkk = dkl * tm["ek"]
            xq = f32(tm["ql"]) * dql
            xk = f32(tm["kl"]) * dkl
            for (h, e, selq, qm, km), m, m_t in zip(tm["lev"], lmask, lmask_t):
                dl = (da if m is None else jnp.where(m, da, 0.0)).astype(BF16)
                dl_t = (da_t if m_t is None else jnp.where(m_t, da_t, 0.0)).astype(BF16)
                dqm = _dot(dl, km)
                dkm = _dot(dl_t, qm)
                dsq = dsq + dqm * (e * selq)
                dkk = dkk + dkm * (e * (1.0 - selq))
                xq = xq + f32(qm) * dqm
                xk = xk + f32(km) * dkm
            dqs = _dot(dov, st0.astype(BF16))
            dke = _dot(v, gtb)
            dsq = dsq + dqs * jnp.exp(tm["b"])
            dkk = dkk + dke * jnp.exp(tm["kdec"])
            xq = xq + f32(tm["qs"]) * dqs
            xk = xk + f32(tm["ke"]) * dke
            dvv = _dot(at_ref[g, n], dov) + _dot_nt(tm["ke"], gtb)
            r_end = jnp.sum(f32(gtb) * _dot(v.T, tm["ke"]) + gt * (st0 * tm["e_c"]), axis=0, keepdims=True)
            g_sc[g] = gt * tm["e_c"] + _dot(dov.T, tm["qs"])
            xh, xm, xl = _split3(xq - xk)
            dlf = (_dot(rtv, xh) + _dot(rtv, xm)) + _dot(rtv, xl) + r_end
            dlf_f = dlf / tm["f"]
            dsp = (1.0 - lbv) * (dlf_f - dkk)
            df_ref[rows, hd] = (dsp * (tm["sp"] * tm["sn"])).astype(BF16)
            dq_ref[rows, hd] = (dsq * _dsilu(qr)).astype(BF16)
            dv_ref[rows, hd] = dvv.astype(BF16)
            dlb_ref[:, hd] += jnp.sum(dlf_f * tm["sn"] - dkk * tm["sn"], axis=0, keepdims=True)

        def chunk(nn, carry):
            for g in range(G):
                head_chunk(g, nc - 1 - nn)
            return carry

        lax.fori_loop(0, nc, chunk, 0)

    blk = lambda off: pl.BlockSpec((tb, G * dk), lambda h, i: (nb - 1 - i, off // G + h))
    vec = pl.BlockSpec((1, G * dk), lambda h, i: (0, h))
    return pl.pallas_call(
        body, grid=(H // G, nb),
        in_specs=[blk(0), blk(0), blk(H), blk(0),
                  pl.BlockSpec((G, nc, dk, dk), lambda h, i: (h, nb - 1 - i, 0, 0)),
                  pl.BlockSpec((G, nc, C, C), lambda h, i: (h, nb - 1 - i, 0, 0)), vec,
                  pl.BlockSpec(msum.shape, lambda h, i: (0, 0)), pl.BlockSpec((C, C), lambda h, i: (0, 0))],
        out_specs=[blk(0), blk(0), blk(0), vec],
        out_shape=[SDS((S, H * dk), BF16)] * 3 + [SDS((1, H * dk), F32)],
        scratch_shapes=[pltpu.VMEM((G, dk, dk), F32)],
        compiler_params=_params("parallel", "arbitrary"), name=name,
    )(p1, f1, p1, do, states, a_t, lb, msum, rtri)


def _lb_fwd(logits, *, name):
    W = logits.shape[1]

    def body(l_ref, lb_ref):
        l = l_ref[...]
        m = jnp.max(l, axis=0, keepdims=True)
        e = jnp.exp(l - m)
        p = e / jnp.sum(e, axis=0, keepdims=True)
        lb_ref[...] = (p[0:1] + p[1:2]) - p[0:1]

    return pl.pallas_call(body, out_shape=SDS((1, W), F32), name=name)(logits)


STAT_ROWS = 8


def _stats_reduce(stats_all, logits, *, name):
    W = logits.shape[1]

    def body(s_ref, l_ref, g_ref):
        tot = s_ref[0]
        for d in range(1, N_DEV):
            tot = tot + s_ref[d]
        l = l_ref[...]
        m = jnp.max(l, axis=0, keepdims=True)
        e = jnp.exp(l - m)
        p = e / jnp.sum(e, axis=0, keepdims=True)
        dlb = tot[2:3]
        dl0 = -(p[0:1] * p[1:2]) * dlb
        dl1 = (p[1:2] * (1.0 - p[1:2])) * dlb
        g_ref[0:2] = tot[0:2]
        g_ref[2:3] = dl0
        g_ref[3:4] = dl1
        g_ref[4:7] = tot[3:6]
        g_ref[7:8] = jnp.zeros((1, W), F32)

    return pl.pallas_call(body, out_shape=SDS((STAT_ROWS, W), F32), name=name)(stats_all, logits)


def _adamw(w, m, v, g_parts, *, name, tr=128):
    R, C = w.shape
    ns = len(g_parts)
    n, Rs = g_parts[0].shape[0], g_parts[0].shape[1]
    assert all(p.shape == (n, Rs, C) for p in g_parts) and ns * Rs == R
    tr = min(tr, Rs)
    assert Rs % tr == 0
    nts = Rs // tr
    c1 = 1.0 / (1.0 - ADAM_B1 ** ADAM_STEP)
    c2 = 1.0 / (1.0 - ADAM_B2 ** ADAM_STEP)

    def body(*refs):
        w_ref, m_ref, v_ref = refs[:3]
        g_refs = refs[3:3 + ns]
        go_ref, d_ref, mo_ref, vo_ref = refs[3 + ns:]

        def update(g_ref):
            g = g_ref[0].astype(F32)
            for k in range(1, n):
                g = g + g_ref[k].astype(F32)
            mn = ADAM_B1 * m_ref[...] + (1.0 - ADAM_B1) * g
            vn = ADAM_B2 * v_ref[...] + (1.0 - ADAM_B2) * (g * g)
            d_ref[...] = -ADAM_LR * ((mn * c1) / (jnp.sqrt(vn * c2) + ADAM_EPS) + ADAM_WD * w_ref[...])
            go_ref[...] = g
            mo_ref[...] = mn
            vo_ref[...] = vn

        for s in range(ns):
            if ns == 1:
                update(g_refs[s])
            else:
                pl.when(pl.program_id(0) // nts == s)(functools.partial(update, g_refs[s]))

    def g_map(i, s):
        return (0, jnp.clip(i - s * nts, 0, nts - 1), 0)

    blk = pl.BlockSpec((tr, C), lambda i: (i, 0))
    return pl.pallas_call(
        body, grid=(R // tr,),
        in_specs=[blk, blk, blk] + [pl.BlockSpec((n, tr, C), functools.partial(g_map, s=s)) for s in range(ns)],
        out_specs=[blk] * 4, out_shape=[SDS((R, C), F32)] * 4,
        compiler_params=_params("parallel"), name=name,
    )(w, m, v, *g_parts)


ANY = pl.BlockSpec(memory_space=pl.ANY)
STAGE_BYTES = 2 * 1024 * 1024


def _stage_shape(shape, dtype):
    row_bytes = int(np.prod(shape[1:])) * jnp.dtype(dtype).itemsize
    rows = max(1, min(shape[0], STAGE_BYTES // row_bytes))
    while shape[0] % rows:
        rows -= 1
    return (rows,) + tuple(shape[1:])


def _staged_copy(frm, to, buf, sems):
    rows = buf.shape[0]
    for r0 in range(0, frm.shape[0], rows):
        cp = pltpu.make_async_copy(frm.at[pl.ds(r0, rows)], buf, sems.at[0])
        cp.start()
        cp.wait()
        cp = pltpu.make_async_copy(buf, to.at[pl.ds(r0, rows)], sems.at[1])
        cp.start()
        cp.wait()


def _all_gather(shards, out_shapes, views, *, name):
    n = len(shards)

    def body(*refs):
        ins, outs = refs[:n], refs[n:2 * n]
        send_sems, recv_sems, local_sems = refs[2 * n:2 * n + 3]
        bufs = refs[2 * n + 3:]
        x, y, c = lax.axis_index("x"), lax.axis_index("y"), lax.axis_index("c")
        me, sibling = (x, y, c), (x, y, 1 - c)
        chips = [(1 - x, y), (x, 1 - y), (1 - x, 1 - y)]

        def dev(p):
            return 4 * p[0] + 2 * p[1] + p[2]

        def copy(a, k, block, to, src=None):
            dst = views[a](outs[a], dev(block))
            return pltpu.make_async_remote_copy(
                src_ref=dst if src is None else src, dst_ref=dst,
                send_sem=send_sems.at[a, k], recv_sem=recv_sems.at[a, k],
                device_id=to, device_id_type=MESH)

        first, passed = [], []
        for a in range(n):
            first.append(copy(a, 0, me, sibling, src=ins[a]))
            first += [copy(a, 1 + j, me, (*chip, c), src=ins[a]) for j, chip in enumerate(chips)]
        for cp in first:
            cp.start()
        for a in range(n):
            _staged_copy(ins[a], views[a](outs[a], dev(me)), bufs[a], local_sems)
        for j, chip in enumerate(chips):
            for a in range(n):
                copy(a, 1 + j, (*chip, c), me).wait_recv()
                cp = copy(a, 4 + j, (*chip, c), sibling)
                cp.start()
                passed.append(cp)
        for a in range(n):
            copy(a, 0, sibling, me).wait_recv()
            for j, chip in enumerate(chips):
                copy(a, 4 + j, (*chip, 1 - c), me).wait_recv()
        for cp in first + passed:
            cp.wait_send()

    return pl.pallas_call(
        body, in_specs=[ANY] * n, out_specs=[ANY] * n, out_shape=list(out_shapes),
        scratch_shapes=[pltpu.SemaphoreType.DMA((n, 7)), pltpu.SemaphoreType.DMA((n, 7)),
                        pltpu.SemaphoreType.DMA((2,))]
        + [pltpu.VMEM(_stage_shape(s.shape, s.dtype), s.dtype) for s in shards],
        name=name,
    )(*shards)


HBM = pl.BlockSpec(memory_space=pltpu.HBM)
SEM = pl.BlockSpec(memory_space=pltpu.SEMAPHORE)
EFFECT = pltpu.SideEffectType.DATAFLOW_SIDE_EFFECTING


def _relations(x, y, c):
    for m in range(1, N_DEV):
        yield m, (1 - x if m & 4 else x, 1 - y if m & 2 else y, 1 - c if m & 1 else c)


def _dev_id(p):
    return 4 * p[0] + 2 * p[1] + p[2]


def _send_start(srcs, land_shapes, src_views, dst_views, *, name):
    n = len(srcs)

    def body(*refs):
        ins, lands = refs[:n], refs[n:2 * n]
        send_sems, recv_sems, token = refs[2 * n], refs[2 * n + 1], refs[-1]
        x, y, c = lax.axis_index("x"), lax.axis_index("y"), lax.axis_index("c")
        me = _dev_id((x, y, c))
        for m, p in _relations(x, y, c):
            for a in range(n):
                pltpu.make_async_remote_copy(
                    src_ref=src_views[a](ins[a], me, _dev_id(p), m), dst_ref=dst_views[a](lands[a], me, m),
                    send_sem=send_sems.at[a * (N_DEV - 1) + m - 1], recv_sem=recv_sems.at[a * (N_DEV - 1) + m - 1],
                    device_id=p, device_id_type=MESH).start()
        token[...] = jnp.zeros_like(token)

    lands = [pltpu.with_memory_space_constraint(lax.empty(s.shape, s.dtype), pltpu.HBM) for s in land_shapes]
    srcs = [pltpu.with_memory_space_constraint(v, pltpu.HBM) for v in srcs]
    res = pl.pallas_call(
        body, name=name,
        out_shape=[pltpu.SemaphoreType.DMA((n * (N_DEV - 1),)), pltpu.SemaphoreType.DMA((n * (N_DEV - 1),))]
        + [pltpu.HBM(v.shape, v.dtype) for v in srcs] + [pltpu.HBM(s.shape, s.dtype) for s in land_shapes]
        + [SDS((8, 128), F32)],
        in_specs=[HBM] * (2 * n), out_specs=[SEM, SEM] + [HBM] * (2 * n) + [pl.BlockSpec(memory_space=pltpu.VMEM)],
        input_output_aliases={i: 2 + i for i in range(2 * n)},
        compiler_params=pltpu.CompilerParams(has_side_effects=EFFECT),
    )(*srcs, *lands)
    return res[0], res[1], res[2:2 + n], res[2 + n:2 + 2 * n], res[-1]


def _send_wait(started, src_views, dst_views, own_views, own_shapes, after, *, name):
    send_sems, recv_sems, srcs, lands, _ = started
    n = len(srcs)

    def body(*refs):
        ins, lnd = refs[:n], refs[n:2 * n]
        send_sems, recv_sems = refs[2 * n], refs[2 * n + 1]
        got = refs[2 * n + 3 + n:2 * n + 3 + 2 * n]
        local_sems = refs[2 * n + 3 + 2 * n]
        bufs = refs[2 * n + 4 + 2 * n:]
        x, y, c = lax.axis_index("x"), lax.axis_index("y"), lax.axis_index("c")
        me = _dev_id((x, y, c))
        for m, p in _relations(x, y, c):
            for a in range(n):
                cp = pltpu.make_async_remote_copy(
                    src_ref=src_views[a](ins[a], me, _dev_id(p), m), dst_ref=dst_views[a](lnd[a], me, m),
                    send_sem=send_sems.at[a * (N_DEV - 1) + m - 1], recv_sem=recv_sems.at[a * (N_DEV - 1) + m - 1],
                    device_id=p, device_id_type=MESH)
                cp.wait_send()
                cp.wait_recv()
        for a in range(n):
            frm, to = own_views[a](ins[a], got[a], me)
            _staged_copy(frm, to, bufs[a], local_sems)

    res = pl.pallas_call(
        body, name=name,
        out_shape=[pltpu.HBM(v.shape, v.dtype) for v in srcs] + [pltpu.HBM(v.shape, v.dtype) for v in lands],
        in_specs=[HBM] * (2 * n) + [SEM, SEM, ANY], out_specs=[HBM] * (2 * n),
        input_output_aliases={i: i for i in range(2 * n)},
        scratch_shapes=[pltpu.SemaphoreType.DMA((2,))]
        + [pltpu.VMEM(_stage_shape(s, v.dtype), v.dtype) for s, v in zip(own_shapes, srcs)],
        compiler_params=pltpu.CompilerParams(has_side_effects=EFFECT),
    )(*srcs, *lands, send_sems, recv_sems, after)
    return res[n:]


def kernel(x, norm_gains, fox_w_in, fox_b_f, hgrn_w_in, hgrn_lb_logits, hgrn_onorm, w_out, final_gain, loss_target, m_norm_gains, m_fox_w_in, m_fox_b_f, m_hgrn_w_in, m_hgrn_lb_logits, m_hgrn_onorm, m_w_out, m_final_gain, v_norm_gains, v_fox_w_in, v_fox_b_f, v_hgrn_w_in, v_hgrn_lb_logits, v_hgrn_onorm, v_w_out, v_final_gain):
    _, S, D = x.shape
    H = FOX_HEADS
    W = H * HEAD_DIM
    assert HGRN_HEADS == H and w_out.shape[2] == D
    cf = fox_w_in.shape[2]
    ch = hgrn_w_in.shape[2]
    ro = w_out.shape[1]
    co = hgrn_onorm.shape[1]
    assert N_DEV * cf == 4 * W + H and N_DEV * ch == 4 * W and N_DEV * ro == W and N_DEV * co == W
    x2 = x.reshape(S, D)
    tgt = loss_target.reshape(S, D)

    col = lambda n: (lambda r, i: r.at[:, pl.ds(pl.multiple_of(i * n, n), n)])
    row = lambda n: (lambda r, i: r.at[pl.ds(pl.multiple_of(i * n, n), n), :])
    late_views = [col(ch), row(ro), row(ro), col(co)]
    late = _send_start(
        [hgrn_w_in[0].astype(BF16), w_out[0].astype(BF16), w_out[1].astype(BF16), hgrn_onorm],
        [SDS((D, 4 * W), BF16), SDS((W, D), BF16), SDS((W, D), BF16), SDS((1, W), F32)],
        [lambda r, me, p, m: r] * 4, [lambda r, me, m, v=v: v(r, me) for v in late_views],
        name="gather_later_start")
    ng0 = norm_gains[0:1] + late[4][0:1, 0:1]

    (wf_g,) = _all_gather([fox_w_in[0].astype(BF16)], [SDS((N_DEV, D, cf), BF16)], [lambda r, p: r.at[p]],
                          name="gather_fox_w_in")
    wf = jnp.transpose(wf_g, (1, 0, 2)).reshape(D, N_DEV * cf)
    wf_main = jnp.concatenate([wf[:, :3 * W], wf[:, 3 * W + H:]], axis=1)
    wfl_t = wf[:, 3 * W:3 * W + H].T

    h0 = _rms_fwd(x2, ng0, name="rms0_fwd")
    p0 = _mm_nn([h0], wf_main, BF16, scale_cols=(W, LOG2E * HEAD_DIM ** -0.5), name="fox_in_proj")
    fl_t = _mm_nt_rows(wfl_t, h0, name="fox_forget_proj")
    b_col = fox_b_f.reshape(H, 1)
    kaug = _fox_key_aug(*_fox_gate_fwd(fl_t, b_col, name="fox_gate_fwd"))
    o0, y0, qaug = _fox_fwd(p0, kaug, H=H, name="fox_attn_fwd")
    wh, wo0, wo1, onorm = _send_wait(
        late, [lambda r, me, p, m: r] * 4, [lambda r, me, m, v=v: v(r, me) for v in late_views],
        [lambda src, land, me, v=v: (src, v(land, me)) for v in late_views],
        [(D, ch), (ro, D), (ro, D), (1, co)], y0[0:16], name="gather_later_wait")
    x1 = _mm_nn([y0], wo0, F32, residual=x2, name="fox_out_proj")

    lb = _lb_fwd(hgrn_lb_logits, name="hgrn_lower_bound")
    h1 = _rms_fwd(x1, norm_gains[1:2], name="rms1_fwd")
    p1 = _mm_nn([h1], wh, BF16, b_cols=[(0, W), (2 * W, 4 * W)], name="hgrn_in_proj")
    f1 = _mm_nn([h1], wh, F32, b_cols=[(W, 2 * W)], name="hgrn_forget_proj")
    o1, y1, states, a_t1 = _hgrn_fwd(p1, f1, lb, onorm, H=H, name="hgrn_fwd")
    xo = _mm_nn([y1], wo1, F32, residual=x1, name="hgrn_out_proj")

    dx2, dx2b, loss_part, dgf = _loss_head(xo, final_gain.reshape(1, D), tgt, name="loss_head")
    loss = lax.psum(jnp.sum(loss_part), ("x", "y", "c"))

    dy1 = _mm_nn([dx2b], wo1, BF16, b_t=True, name="hgrn_out_proj_dx")
    dwo1 = _mm_tn(y1, [dx2b], BF16, name="hgrn_out_proj_dw")
    do1, dg1, donorm = _hgrn_post_bwd(dy1, o1, p1, onorm, H=H, name="hgrn_post_bwd")
    dq1, df1, di1, dlb = _hgrn_bwd(p1, f1, lb, do1, states, a_t1, H=H, name="hgrn_bwd")
    segs1 = [dq1, df1, di1, dg1]
    dh1 = _mm_nn(segs1, wh, BF16, b_t=True, tn=D, name="hgrn_in_proj_dx")
    dwh = _mm_tn(h1, segs1, BF16, name="hgrn_in_proj_dw")
    part_views = [col(ch), row(ro)]
    slot = lambda r, me, m: r.at[m]
    ex1 = _send_start([dwh, dwo1], [SDS((N_DEV, D, ch), BF16), SDS((N_DEV, ro, D), BF16)],
                      [lambda r, me, p, m, v=v: v(r, p) for v in part_views], [slot] * 2,
                      name="exchange_layer1_start")
    ng1 = norm_gains[1:2] + ex1[4][0:1, 0:1]
    dx1, dx1b, dng1 = _rms_bwd(x1, ng1, dh1, dx2, name="rms1_bwd")

    dy0 = _mm_nn([dx1b], wo0, BF16, b_t=True, name="fox_out_proj_dx")
    dwo0 = _mm_tn(y0, [dx1b], BF16, name="fox_out_proj_dw")
    do0, dg0, doaug = _fox_post_bwd(dy0, o0, p0, H=H, name="fox_post_bwd")
    dq0, dc_row, dk0, dv0, dc_key = _fox_bwd(p0, kaug, qaug, do0, doaug, H=H, name="fox_attn_bwd")
    dfl_t, dbf = _fox_gate_bwd(dc_row.reshape(H, S), dc_key.reshape(H, S), fl_t, b_col, name="fox_gate_bwd")
    dfl_tb = dfl_t.astype(BF16)
    dwfl_t = _mm_nn([dfl_tb], h0, BF16, name="fox_forget_proj_dw")
    segs0 = [dq0, dk0, dv0, dg0]
    dwf_main = _mm_tn(h0, segs0, BF16, name="fox_in_proj_dw")
    dwf = jnp.concatenate([dwf_main[:, :3 * W], dwfl_t.T, dwf_main[:, 3 * W:]], axis=1)
    dwf_blocks = jnp.transpose(dwf.reshape(D, N_DEV, cf), (1, 0, 2))
    ex0 = _send_start([dwf_blocks, dwo0], [SDS((N_DEV, D, cf), BF16), SDS((N_DEV, ro, D), BF16)],
                      [lambda r, me, p, m: r.at[p], lambda r, me, p, m: row(ro)(r, p)], [slot] * 2,
                      name="exchange_layer0_start")
    wfl_t0 = wfl_t + ex0[4][0:1, 0:1].astype(BF16)
    dh0_f = _mm_nn([dfl_tb.T], wfl_t0, BF16, name="fox_forget_proj_dx")
    dh0 = _mm_nn(segs0, wf_main, BF16, residual=dh0_f, b_t=True, tn=D, name="fox_in_proj_dx")
    grad_x, _, dng0 = _rms_bwd(x2, norm_gains[0:1], dh0, dx1, name="rms0_bwd")

    own1 = [lambda src, land, me, v=v: (v(src, me), land.at[0]) for v in part_views]
    rh, ro1 = _send_wait(ex1, [lambda r, me, p, m, v=v: v(r, p) for v in part_views], [slot] * 2, own1,
                         [(D, ch), (ro, D)], dng0, name="exchange_layer1_wait")
    rf, ro0 = _send_wait(ex0, [lambda r, me, p, m: r.at[p], lambda r, me, p, m: row(ro)(r, p)], [slot] * 2,
                         [lambda src, land, me: (src.at[me], land.at[0]),
                          lambda src, land, me: (row(ro)(src, me), land.at[0])],
                         [(D, cf), (ro, D)], dng0, name="exchange_layer0_wait")

    pad = lambda a: jnp.pad(a, ((0, 0), (0, W - a.shape[1])))
    stats = jnp.concatenate([dng0, dng1, dlb, dgf, pad(dbf.reshape(1, H)), donorm,
                             jnp.zeros((2, W), F32)], axis=0)
    assert D == W
    (stats_all,) = _all_gather([stats], [SDS((N_DEV, STAT_ROWS, W), F32)], [lambda r, p: r.at[p]],
                               name="gather_small_grads")
    g_small = _stats_reduce(stats_all, hgrn_lb_logits, name="reduce_small_grads")
    me = 4 * lax.axis_index("x") + 2 * lax.axis_index("y") + lax.axis_index("c")
    g_onorm = lax.dynamic_slice_in_dim(g_small[6:7], me * co, co, axis=1)

    def upd(w, m, v, parts, name):
        shp = w.shape
        r2 = (-1, shp[-1])
        g, d, mn, vn = _adamw(w.reshape(r2), m.reshape(r2), v.reshape(r2), parts, name=name)
        return g.reshape(shp), d.reshape(shp), mn.reshape(shp), vn.reshape(shp)

    res = {
        "norm_gains": upd(norm_gains, m_norm_gains, v_norm_gains, [g_small[None, 0:2]], "adamw_norm_gains"),
        "fox_w_in": upd(fox_w_in, m_fox_w_in, v_fox_w_in, [rf], "adamw_fox_w_in"),
        "fox_b_f": upd(fox_b_f, m_fox_b_f, v_fox_b_f, [g_small[None, 5:6, :H]], "adamw_fox_b_f"),
        "hgrn_w_in": upd(hgrn_w_in, m_hgrn_w_in, v_hgrn_w_in, [rh], "adamw_hgrn_w_in"),
        "hgrn_lb_logits": upd(hgrn_lb_logits, m_hgrn_lb_logits, v_hgrn_lb_logits, [g_small[None, 2:4]],
                              "adamw_hgrn_lb_logits"),
        "hgrn_onorm": upd(hgrn_onorm, m_hgrn_onorm, v_hgrn_onorm, [g_onorm[None]], "adamw_hgrn_onorm"),
        "w_out": upd(w_out, m_w_out, v_w_out, [ro0, ro1], "adamw_w_out"),
        "final_gain": upd(final_gain.reshape(1, D), m_final_gain.reshape(1, D), v_final_gain.reshape(1, D),
                          [g_small[None, 4:5]], "adamw_final_gain"),
    }
    order = ["norm_gains", "fox_w_in", "fox_b_f", "hgrn_w_in", "hgrn_lb_logits", "hgrn_onorm", "w_out", "final_gain"]
    fix = lambda n, a: a.reshape(D) if n == "final_gain" else a
    outs = [loss, grad_x.reshape(1, S, D)]
    for k in range(4):
        outs += [fix(n, res[n][k]) for n in order]
    return tuple(outs)
```

```python
import functools

import numpy as np
import jax
import jax.numpy as jnp
from jax import lax
from jax.experimental import pallas as pl
from jax.experimental.pallas import tpu as pltpu

F32 = jnp.float32
BF16 = jnp.bfloat16
SDS = jax.ShapeDtypeStruct
MESH = pl.DeviceIdType.MESH

EPS = 1e-6
ADAM_LR, ADAM_B1, ADAM_B2, ADAM_EPS, ADAM_WD, ADAM_STEP = 0.001, 0.9, 0.999, 1e-08, 0.01, 10

N_DEV = 8
FOX_HEADS = 16
HGRN_HEADS = 16
HEAD_DIM = 128
HGRN_CHUNK = 128
HGRN_LEAF = 16
HGRN_HEADS_PER_STEP = 16
EXP_CLAMP = 85.0
ATT_BLOCK = 512
ATT_HEADS_PER_STEP = 4
ATT_BWD_HEADS_PER_STEP = 2
POST_HEADS_PER_STEP = 4
NEG = -1e30
LOG2E = 1.4426950408889634
LN2 = 0.6931471805599453

VMEM_LIMIT_V7X = 56 * 1024 * 1024


def _params(*sem):
    return pltpu.CompilerParams(dimension_semantics=sem, vmem_limit_bytes=VMEM_LIMIT_V7X)


def _silu(x):
    return x * jax.nn.sigmoid(x)


def _dsilu(x):
    s = jax.nn.sigmoid(x)
    return s * (1.0 + x * (1.0 - s))


def _dot(a, b):
    return jnp.dot(a, b, preferred_element_type=F32)


def _dot_nt(a, b):
    return lax.dot_general(a, b, (((1,), (1,)), ((), ())), preferred_element_type=F32)


def _dot_tn(a, b):
    return lax.dot_general(a, b, (((0,), (0,)), ((), ())), preferred_element_type=F32)


def _mm_nn(a_list, b, out_dtype, *, name, residual=None, scale_cols=None, b_t=False, b_cols=None,
           tm=1024, tn=1024, tk=2048):
    ns = len(a_list)
    M, Ks = a_list[0].shape
    K, N = (b.shape[1], b.shape[0]) if b_t else b.shape
    if b_cols is None:
        b_cols = [(0, N)]
    else:
        assert not b_t
        N = sum(e - s for s, e in b_cols)
    dot = _dot_nt if b_t else _dot
    assert K == ns * Ks and all(a.shape == (M, Ks) for a in a_list)
    if ns > 1:
        tk = tk // 2
    tm, tn, tk = min(tm, M), min(tn, N), min(tk, Ks)
    assert M % tm == 0 and N % tn == 0 and Ks % tk == 0
    assert scale_cols is None or scale_cols[0] % tn == 0
    assert all(s % tn == 0 and e % tn == 0 for s, e in b_cols)
    nks = Ks // tk
    nk = ns * nks
    has_res = residual is not None

    def body(*refs):
        a_refs, b_ref = refs[:ns], refs[ns]
        res_ref = refs[ns + 1] if has_res else None
        o_ref = refs[ns + 1 + has_res]

        def finish(r):
            if has_res:
                r = r + res_ref[...].astype(F32)
            if scale_cols is not None:
                r = r * jnp.where(pl.program_id(1) < scale_cols[0] // tn, scale_cols[1], 1.0)
            o_ref[...] = r.astype(out_dtype)

        if nk == 1:
            finish(dot(a_refs[0][...], b_ref[...]))
            return
        acc_ref = refs[ns + 2 + has_res]
        k = pl.program_id(2)

        @pl.when(k == 0)
        def _():
            acc_ref[...] = jnp.zeros_like(acc_ref)

        for s in range(ns):
            def step(s=s):
                acc_ref[...] += dot(a_refs[s][...], b_ref[...])

            if ns == 1:
                step()
            else:
                pl.when(k // nks == s)(step)

        @pl.when(k == nk - 1)
        def _():
            finish(acc_ref[...])

    def a_map(i, j, k, s):
        return (i, jnp.clip(k - s * nks, 0, nks - 1))

    in_specs = [pl.BlockSpec((tm, tk), functools.partial(a_map, s=s)) for s in range(ns)]
    def b_col(j):
        src = j + b_cols[0][0] // tn
        for (_, e0), (s1, _) in zip(b_cols[:-1], b_cols[1:]):
            src = src + jnp.where(src >= e0 // tn, (s1 - e0) // tn, 0)
        return src

    if b_t:
        in_specs.append(pl.BlockSpec((tn, tk), lambda i, j, k: (j, k)))
    else:
        in_specs.append(pl.BlockSpec((tk, tn), lambda i, j, k: (k, b_col(j))))
    args = list(a_list) + [b]
    if has_res:
        in_specs.append(pl.BlockSpec((tm, tn), lambda i, j, k: (i, j)))
        args.append(residual)
    return pl.pallas_call(
        body, grid=(M // tm, N // tn, nk), in_specs=in_specs,
        out_specs=pl.BlockSpec((tm, tn), lambda i, j, k: (i, j)),
        out_shape=SDS((M, N), out_dtype),
        scratch_shapes=[] if nk == 1 else [pltpu.VMEM((tm, tn), F32)],
        compiler_params=_params("parallel", "parallel", "arbitrary"), name=name,
    )(*args)


def _mm_tn(a, b_list, out_dtype, *, name, tm=2048, tn=1024, tk=512):
    ns = len(b_list)
    S, M = a.shape
    Ns = b_list[0].shape[1]
    assert all(b.shape == (S, Ns) for b in b_list)
    tm, tn, tk = min(tm, M), min(tn, Ns), min(tk, S)
    assert M % tm == 0 and Ns % tn == 0 and S % tk == 0
    njs = Ns // tn
    nk = S // tk

    def body(*refs):
        a_ref, b_refs, o_ref, acc_ref = refs[0], refs[1:1 + ns], refs[1 + ns], refs[2 + ns]
        j, k = pl.program_id(1), pl.program_id(2)

        @pl.when(k == 0)
        def _():
            acc_ref[...] = jnp.zeros_like(acc_ref)

        for s in range(ns):
            def step(s=s):
                acc_ref[...] += _dot_tn(a_ref[...], b_refs[s][...])

            if ns == 1:
                step()
            else:
                pl.when(j // njs == s)(step)

        @pl.when(k == nk - 1)
        def _():
            o_ref[...] = acc_ref[...].astype(out_dtype)

    def b_map(i, j, k, s):
        return (k, jnp.clip(j - s * njs, 0, njs - 1))

    in_specs = [pl.BlockSpec((tk, tm), lambda i, j, k: (k, i))]
    in_specs += [pl.BlockSpec((tk, tn), functools.partial(b_map, s=s)) for s in range(ns)]
    return pl.pallas_call(
        body, grid=(M // tm, ns * njs, nk), in_specs=in_specs,
        out_specs=pl.BlockSpec((tm, tn), lambda i, j, k: (i, j)),
        out_shape=SDS((M, ns * Ns), out_dtype),
        scratch_shapes=[pltpu.VMEM((tm, tn), F32)],
        compiler_params=_params("parallel", "parallel", "arbitrary"), name=name,
    )(a, *b_list)


def _mm_nt_rows(w_t, h, *, name, tn=1024):
    R, K = w_t.shape
    S = h.shape[0]
    tn = min(tn, S)

    def body(w_ref, h_ref, o_ref):
        o_ref[...] = _dot_nt(w_ref[...], h_ref[...])

    return pl.pallas_call(
        body, grid=(S // tn,),
        in_specs=[pl.BlockSpec((R, K), lambda i: (0, 0)), pl.BlockSpec((tn, K), lambda i: (i, 0))],
        out_specs=pl.BlockSpec((R, tn), lambda i: (0, i)),
        out_shape=SDS((R, S), F32), compiler_params=_params("parallel"), name=name,
    )(w_t, h)


def _rms_fwd(x, gain, *, name, tm=512):
    S, D = x.shape
    tm = min(tm, S)

    def body(x_ref, g_ref, h_ref):
        xv = x_ref[...]
        r = lax.rsqrt(jnp.mean(xv * xv, axis=-1, keepdims=True) + EPS)
        h_ref[...] = ((xv * r) * g_ref[...]).astype(BF16)

    return pl.pallas_call(
        body, grid=(S // tm,),
        in_specs=[pl.BlockSpec((tm, D), lambda i: (i, 0)), pl.BlockSpec((1, D), lambda i: (0, 0))],
        out_specs=pl.BlockSpec((tm, D), lambda i: (i, 0)),
        out_shape=SDS((S, D), BF16), compiler_params=_params("parallel"), name=name,
    )(x, gain)


def _rms_bwd(x, gain, dh, dres, *, name, tm=512):
    S, D = x.shape
    tm = min(tm, S)

    def body(x_ref, g_ref, dh_ref, dres_ref, dx_ref, dxb_ref, dg_ref):
        @pl.when(pl.program_id(0) == 0)
        def _():
            dg_ref[...] = jnp.zeros_like(dg_ref)

        xv = x_ref[...]
        r = lax.rsqrt(jnp.mean(xv * xv, axis=-1, keepdims=True) + EPS)
        xh = xv * r
        dhv = dh_ref[...].astype(F32)
        dg_ref[...] += jnp.sum(dhv * xh, axis=0, keepdims=True)
        dxh = dhv * g_ref[...]
        dx = r * (dxh - xh * jnp.mean(dxh * xh, axis=-1, keepdims=True)) + dres_ref[...]
        dx_ref[...] = dx
        dxb_ref[...] = dx.astype(BF16)

    row = pl.BlockSpec((tm, D), lambda i: (i, 0))
    vec = pl.BlockSpec((1, D), lambda i: (0, 0))
    return pl.pallas_call(
        body, grid=(S // tm,), in_specs=[row, vec, row, row], out_specs=[row, row, vec],
        out_shape=[SDS((S, D), F32), SDS((S, D), BF16), SDS((1, D), F32)],
        compiler_params=_params("arbitrary"), name=name,
    )(x, gain, dh, dres)


def _loss_head(x, gain, target, *, name, tm=512):
    S, D = x.shape
    tm = min(tm, S)
    assert tm % 8 == 0 and D % 128 == 0

    def body(x_ref, g_ref, t_ref, dx_ref, dxb_ref, loss_ref, dg_ref):
        @pl.when(pl.program_id(0) == 0)
        def _():
            dg_ref[...] = jnp.zeros_like(dg_ref)
            loss_ref[...] = jnp.zeros_like(loss_ref)

        xv = x_ref[...]
        g = g_ref[...]
        r = lax.rsqrt(jnp.mean(xv * xv, axis=-1, keepdims=True) + EPS)
        xh = xv * r
        err = xh * g - t_ref[...]
        e2 = (err * err).reshape(tm // 8, 8, D).sum(axis=0)
        part = e2[:, 0:128]
        for k in range(1, D // 128):
            part = part + e2[:, k * 128:(k + 1) * 128]
        loss_ref[...] += part * (0.5 / D)
        dy = err * (1.0 / D)
        dg_ref[...] += jnp.sum(dy * xh, axis=0, keepdims=True)
        dxh = dy * g
        dx = r * (dxh - xh * jnp.mean(dxh * xh, axis=-1, keepdims=True))
        dx_ref[...] = dx
        dxb_ref[...] = dx.astype(BF16)

    row = pl.BlockSpec((tm, D), lambda i: (i, 0))
    vec = pl.BlockSpec((1, D), lambda i: (0, 0))
    return pl.pallas_call(
        body, grid=(S // tm,), in_specs=[row, vec, row],
        out_specs=[row, row, pl.BlockSpec((8, 128), lambda i: (0, 0)), vec],
        out_shape=[SDS((S, D), F32), SDS((S, D), BF16), SDS((8, 128), F32), SDS((1, D), F32)],
        compiler_params=_params("arbitrary"), name=name,
    )(x, gain, target)


def _split3(x):
    hi = x.astype(BF16)
    r1 = x - hi.astype(F32)
    mid = r1.astype(BF16)
    lo = (r1 - mid.astype(F32)).astype(BF16)
    return hi, mid, lo


def _split2(x):
    hi = x.astype(BF16)
    lo = (x - hi.astype(F32)).astype(BF16)
    return hi, lo


def _fox_gate_fwd(fl_t, b_col, *, name):
    H, S = fl_t.shape
    L = 128
    tri = jnp.asarray(np.triu(np.ones((L, L), np.float32)), BF16)

    def body(fl_ref, b_ref, tri_ref, hi_ref, mid_ref, lo_ref, carry):
        @pl.when(pl.program_id(0) == 0)
        def _():
            carry[...] = jnp.zeros_like(carry)

        z = fl_ref[...] + b_ref[...]
        lf = jnp.minimum(z, 0.0) - jnp.log(1.0 + jnp.exp(-jnp.abs(z)))
        hi, mid, lo = _split3(lf)
        t = tri_ref[...]
        c = (_dot(hi, t) + _dot(mid, t)) + _dot(lo, t) + carry[...]
        carry[...] = c[:, L - 1:L]
        hi_ref[...], mid_ref[...], lo_ref[...] = _split3(c * (-LOG2E))

    blk = pl.BlockSpec((H, L), lambda i: (0, i))
    return pl.pallas_call(
        body, grid=(S // L,),
        in_specs=[blk, pl.BlockSpec((H, 1), lambda i: (0, 0)), pl.BlockSpec((L, L), lambda i: (0, 0))],
        out_specs=[blk] * 3, out_shape=[SDS((H, S), BF16)] * 3, scratch_shapes=[pltpu.VMEM((H, 1), F32)],
        compiler_params=_params("arbitrary"), name=name,
    )(fl_t, b_col, tri)


def _fox_gate_bwd(dc_row, dc_key, fl_t, b_col, *, name):
    H, S = fl_t.shape
    L = 128
    n = S // L
    tri = jnp.asarray(np.tril(np.ones((L, L), np.float32)), BF16)

    def body(dcr_ref, dck_ref, fl_ref, b_ref, tri_ref, dfl_ref, db_ref, carry):
        @pl.when(pl.program_id(0) == 0)
        def _():
            carry[...] = jnp.zeros_like(carry)
            db_ref[...] = jnp.zeros_like(db_ref)

        hi, mid, lo = _split3(dcr_ref[...] + dck_ref[...])
        t = tri_ref[...]
        dlf = (_dot(hi, t) + _dot(mid, t)) + _dot(lo, t) + carry[...]
        carry[...] = dlf[:, 0:1]
        z = fl_ref[...] + b_ref[...]
        dfl = dlf * jax.nn.sigmoid(-z)
        dfl_ref[...] = dfl
        db_ref[...] += jnp.sum(dfl, axis=1, keepdims=True)

    blk = pl.BlockSpec((H, L), lambda i: (0, n - 1 - i))
    col = pl.BlockSpec((H, 1), lambda i: (0, 0))
    return pl.pallas_call(
        body, grid=(n,), in_specs=[blk, blk, blk, col, pl.BlockSpec((L, L), lambda i: (0, 0))],
        out_specs=[blk, col], out_shape=[SDS((H, S), F32), SDS((H, 1), F32)],
        scratch_shapes=[pltpu.VMEM((H, 1), F32)], compiler_params=_params("arbitrary"), name=name,
    )(dc_row, dc_key, fl_t, b_col, tri)


AUG = HEAD_DIM


def _lane_select(cols, shape):
    lane = lax.broadcasted_iota(jnp.int32, shape, 1)
    out = jnp.zeros(shape, BF16)
    for k, c in reversed(list(enumerate(cols))):
        c = jnp.full(shape, c, BF16) if isinstance(c, (int, float)) else jnp.broadcast_to(c, shape).astype(BF16)
        out = jnp.where(lane == k, c, out)
    return out


def _fox_key_aug(b_hi, b_mid, b_lo):
    H, S = b_hi.shape
    ones = jnp.ones((H, S), BF16)
    ka = jnp.stack([b_hi, b_mid, b_lo, ones, ones, ones], axis=-1)
    ka = jnp.pad(ka, ((0, 0), (0, 0), (0, AUG - 6)))
    return jnp.transpose(ka, (1, 0, 2)).reshape(S, H * AUG)


def _fox_fwd(p0, kaug, *, H, name):
    S = p0.shape[0]
    T = min(ATT_BLOCK, S)
    nq = S // T
    dh = HEAD_DIM
    G = ATT_HEADS_PER_STEP
    assert H % G == 0

    def body(q_ref, k_ref, ka_ref, v_ref, g_ref, o_ref, y_ref, qa_ref, m_sc, acc_sc, p_sc, al_sc):
        i = pl.program_id(1)
        qaug = _lane_select([1.0, 1.0, 1.0], (T, AUG))
        ones = jnp.ones((T, dh), BF16)
        m_sc[...] = jnp.full_like(m_sc, NEG)
        acc_sc[...] = jnp.zeros_like(acc_sc)

        def step(j, before, masked):
            rows = pl.ds(pl.multiple_of(j * T, T), T)
            for g in range(G):
                hd = slice(g * dh, (g + 1) * dh)
                if not masked:
                    prev = pl.ds(pl.multiple_of(before * T, T), T)
                    vp = jnp.concatenate([v_ref[prev, hd], ones], axis=1)
                    acc_sc[g] = jnp.tile(al_sc[g], (1, 2)) * acc_sc[g] + _dot(p_sc[g], vp)
                q = jnp.concatenate([q_ref[:, hd], qaug], axis=1)
                kj = jnp.concatenate([k_ref[rows, hd], ka_ref[rows, hd]], axis=1)
                t = _dot_nt(q, kj)
                if masked:
                    row = lax.broadcasted_iota(jnp.int32, (T, T), 0)
                    col = lax.broadcasted_iota(jnp.int32, (T, T), 1)
                    t = jnp.where(row >= col, t, NEG)
                m_prev = m_sc[g]
                m_new = jnp.maximum(m_prev, jnp.max(t, axis=-1, keepdims=True))
                p_sc[g] = jnp.exp2(t - jnp.tile(m_new, (1, T // 128))).astype(BF16)
                al_sc[g] = jnp.exp2(m_prev - m_new)
                m_sc[g] = m_new

        step(i, None, True)

        def loop_body(j, carry):
            step(j, jnp.where(j == 0, i, j - 1), False)
            return carry

        lax.fori_loop(0, i, loop_body, 0)
        rows = pl.ds(pl.multiple_of(jnp.where(i == 0, 0, i - 1) * T, T), T)
        for g in range(G):
            hd = slice(g * dh, (g + 1) * dh)
            vp = jnp.concatenate([v_ref[rows, hd], ones], axis=1)
            acc = jnp.tile(al_sc[g], (1, 2)) * acc_sc[g] + _dot(p_sc[g], vp)
            l = acc[:, dh:]
            o = acc[:, :dh] / l
            o_ref[:, hd] = o
            y_ref[:, hd] = (o * _silu(g_ref[:, hd].astype(F32))).astype(BF16)
            hi, mid, lo = _split3(-(m_sc[g] + jnp.log2(l)))
            qa_ref[:, hd] = _lane_select([1.0, 1.0, 1.0, hi, mid, lo], (T, AUG))

    blk = lambda off: pl.BlockSpec((T, G * dh), lambda h, i: (i, off // G + h))
    full = lambda off: pl.BlockSpec((S, G * dh), lambda h, i: (0, off // G + h), pipeline_mode=pl.Buffered(1))
    return pl.pallas_call(
        body, grid=(H // G, nq),
        in_specs=[blk(0), full(H), full(0), full(2 * H), blk(3 * H)],
        out_specs=[blk(0), blk(0), blk(0)],
        out_shape=[SDS((S, H * dh), F32), SDS((S, H * dh), BF16), SDS((S, H * AUG), BF16)],
        scratch_shapes=[pltpu.VMEM((G, T, 128), F32), pltpu.VMEM((G, T, 2 * dh), F32),
                        pltpu.VMEM((G, T, T), BF16), pltpu.VMEM((G, T, 128), F32)],
        compiler_params=_params("parallel", "arbitrary"), name=name,
    )(p0, p0, kaug, p0, p0)


def _fox_post_bwd(dy, o, p0, *, H, name, tm=512):
    S = dy.shape[0]
    dh = HEAD_DIM
    tm = min(tm, S)
    G = POST_HEADS_PER_STEP
    assert H % G == 0

    def body(dy_ref, o_ref, g_ref, do_ref, dg_ref, da_ref):
        dyv = dy_ref[...].astype(F32)
        ov = o_ref[...]
        g = g_ref[...].astype(F32)
        do = (dyv * _silu(g)).astype(BF16)
        do_ref[...] = do
        dg_ref[...] = (dyv * ov * _dsilu(g)).astype(BF16)
        prod = do.astype(F32) * ov
        for k in range(G):
            hd = slice(k * dh, (k + 1) * dh)
            delta = jnp.sum(prod[:, hd], axis=-1, keepdims=True)
            hi, mid, lo = _split3(-jnp.broadcast_to(delta, (tm, AUG)))
            da_ref[:, hd] = _lane_select([hi, mid, lo], (tm, AUG))

    blk = pl.BlockSpec((tm, G * dh), lambda h, i: (i, h))
    return pl.pallas_call(
        body, grid=(H // G, S // tm),
        in_specs=[blk, blk, pl.BlockSpec((tm, G * dh), lambda h, i: (i, 3 * H // G + h))],
        out_specs=[blk, blk, blk],
        out_shape=[SDS((S, H * dh), BF16), SDS((S, H * dh), BF16), SDS((S, H * AUG), BF16)],
        compiler_params=_params("parallel", "parallel"), name=name,
    )(dy, o, p0)


def _fox_bwd(p0, kaug, qaug, do, doaug, *, H, name):
    S = p0.shape[0]
    T = min(ATT_BLOCK, S)
    nq = S // T
    dh = HEAD_DIM
    scale = dh ** -0.5
    G = ATT_BWD_HEADS_PER_STEP
    assert H % G == 0

    def body(q_ref, qa_ref, k_ref, ka_ref, v_ref, do_ref, da_ref, dq_ref, rs_ref, dk_ref, dv_ref, dc_ref,
             dq_sc, dk_sc, dv_sc, pt_sc, dst_sc):
        j = pl.program_id(1)
        vaug = _lane_select([1.0, 1.0, 1.0], (T, AUG))
        ones = jnp.ones((T, dh), BF16)

        @pl.when(j == 0)
        def _():
            dq_sc[...] = jnp.zeros_like(dq_sc)

        dk_sc[...] = jnp.zeros_like(dk_sc)
        dv_sc[...] = jnp.zeros_like(dv_sc)

        def apply(prev):
            for g in range(G):
                hd = slice(g * dh, (g + 1) * dh)
                dv_sc[g] += _dot(pt_sc[g], do_ref[prev, hd])
                dk_sc[g] += _dot(dst_sc[g], jnp.concatenate([q_ref[prev, hd], ones], axis=1))
                dq_sc[g, prev] += _dot_tn(dst_sc[g], jnp.concatenate([k_ref[:, hd], ones], axis=1))

        def step(i, masked):
            rows = pl.ds(pl.multiple_of(i * T, T), T)
            if not masked:
                apply(pl.ds(pl.multiple_of((i - 1) * T, T), T))
            for g in range(G):
                hd = slice(g * dh, (g + 1) * dh)
                k = jnp.concatenate([k_ref[:, hd], ka_ref[:, hd]], axis=1)
                v = jnp.concatenate([v_ref[:, hd], vaug], axis=1)
                pt = jnp.exp2(_dot_nt(k, jnp.concatenate([q_ref[rows, hd], qa_ref[rows, hd]], axis=1)))
                if masked:
                    row = lax.broadcasted_iota(jnp.int32, (T, T), 0)
                    col = lax.broadcasted_iota(jnp.int32, (T, T), 1)
                    pt = jnp.where(col >= row, pt, 0.0)
                dst = pt * _dot_nt(v, jnp.concatenate([do_ref[rows, hd], da_ref[rows, hd]], axis=1))
                pt_sc[g] = pt.astype(BF16)
                dst_sc[g] = dst.astype(BF16)

        step(j, True)

        def loop_body(i, carry):
            step(i, False)
            return carry

        lax.fori_loop(j + 1, nq, loop_body, 0)
        apply(pl.ds((nq - 1) * T, T))
        for g in range(G):
            hd = slice(g * dh, (g + 1) * dh)
            dk_ref[:, hd] = (dk_sc[g, :, :dh] * LN2).astype(BF16)
            dv_ref[:, hd] = dv_sc[g].astype(BF16)
            dc_ref[g] = -jnp.transpose(dk_sc[g, :, dh:])[0:1]

        @pl.when(j == nq - 1)
        def _():
            for g in range(G):
                dq_ref[:, g * dh:(g + 1) * dh] = (dq_sc[g, :, :dh] * scale).astype(BF16)
                for i in range(nq):
                    rs_ref[g, :, i * T:(i + 1) * T] = jnp.transpose(dq_sc[g, i * T:(i + 1) * T, dh:])[0:1]

    blk = lambda off: pl.BlockSpec((T, G * dh), lambda h, j: (j, off // G + h))
    full = lambda off: pl.BlockSpec((S, G * dh), lambda h, j: (0, off // G + h))
    once = lambda off: pl.BlockSpec((S, G * dh), lambda h, j: (0, off // G + h), pipeline_mode=pl.Buffered(1))
    rowv = pl.BlockSpec((G, 1, T), lambda h, j: (h, 0, j))
    return pl.pallas_call(
        body, grid=(H // G, nq),
        in_specs=[once(0), once(0), blk(H), blk(0), blk(2 * H), once(0), once(0)],
        out_specs=[full(0), pl.BlockSpec((G, 1, S), lambda h, j: (h, 0, 0)), blk(0), blk(0), rowv],
        out_shape=[SDS((S, H * dh), BF16), SDS((H, 1, S), F32), SDS((S, H * dh), BF16), SDS((S, H * dh), BF16),
                   SDS((H, 1, S), F32)],
        scratch_shapes=[pltpu.VMEM((G, S, 2 * dh), F32), pltpu.VMEM((G, T, 2 * dh), F32), pltpu.VMEM((G, T, dh), F32),
                        pltpu.VMEM((G, T, T), BF16), pltpu.VMEM((G, T, T), BF16)],
        compiler_params=_params("parallel", "arbitrary"), name=name,
    )(p0, qaug, p0, kaug, p0, do, doaug)


def _hgrn_levels(C, leaf):
    levels = []
    h = C // 2
    while h >= leaf:
        levels.append(h)
        h //= 2
    return levels


def _hgrn_sum_matrix(C, leaf):
    t = np.arange(C)[:, None]
    u = np.arange(C)[None, :]
    mats = [(u <= t), (u > t)]
    for h in _hgrn_levels(C, leaf):
        start = (t // (2 * h)) * (2 * h)
        mid = start + h - 1
        second = t > mid
        m = np.where(second, (u > mid) & (u <= t), (u > t) & (u <= mid))
        mats.append(m)
    lstart = (t // leaf) * leaf
    mats.append((u >= lstart) & (u <= t))
    return np.concatenate([m.astype(np.float32) for m in mats], axis=0)


def _hgrn_chunk_terms(qr, fz, lb, msum, C, leaf):
    levels = _hgrn_levels(C, leaf)
    sq = _silu(qr)
    sp = 1.0 / (1.0 + jnp.exp(-fz))
    sn = 1.0 / (1.0 + jnp.exp(fz))
    f = lb + (1.0 - lb) * sp
    lf = jnp.log(f)
    k = (1.0 - lb) * sn
    hi, lo = _split2(lf)
    dsum = _dot(msum, hi) + _dot(msum, lo)
    b = dsum[0:C]
    kdec = dsum[C:2 * C]
    rowi = lax.broadcasted_iota(jnp.int32, (C, 1), 0)
    lev = []
    for n, h in enumerate(levels):
        e = jnp.exp(dsum[(2 + n) * C:(3 + n) * C])
        selq = jnp.where((rowi % (2 * h)) >= h, 1.0, 0.0)
        qm = (sq * e * selq).astype(BF16)
        km = (k * e * (1.0 - selq)).astype(BF16)
        lev.append((h, e, selq, qm, km))
    dleaf = dsum[(2 + len(levels)) * C:(3 + len(levels)) * C]
    eq = jnp.exp(dleaf)
    ek = jnp.exp(jnp.minimum(-dleaf, EXP_CLAMP))
    return dict(sq=sq, sp=sp, sn=sn, f=f, k=k, b=b, kdec=kdec, lev=lev, eq=eq, ek=ek,
                ql=(sq * eq).astype(BF16), kl=(k * ek).astype(BF16),
                qs=(sq * jnp.exp(b)).astype(BF16), ke=(k * jnp.exp(kdec)).astype(BF16),
                e_c=jnp.exp(b[C - 1:C, :]))


def _hgrn_masks(C, leaf, transposed):
    a = lax.broadcasted_iota(jnp.int32, (C, C), 0)
    bb = lax.broadcasted_iota(jnp.int32, (C, C), 1)
    t, s = (bb, a) if transposed else (a, bb)
    lev = [None if 2 * h == C else (t // (2 * h)) == (s // (2 * h)) for h in _hgrn_levels(C, leaf)]
    if leaf == C:
        leafm = s <= t
    else:
        leafm = ((t // leaf) == (s // leaf)) & (s <= t)
    return lev, leafm


def _hgrn_fwd(p1, f1, lb, onorm, *, H, name, tb=512):
    S = p1.shape[0]
    dk = HEAD_DIM
    C = min(HGRN_CHUNK, S)
    leaf = min(HGRN_LEAF, C)
    tb = min(tb, S)
    nc = tb // C
    G = HGRN_HEADS_PER_STEP
    assert H % G == 0
    msum = jnp.asarray(_hgrn_sum_matrix(C, leaf), BF16)

    def body(q_ref, f_ref, v_ref, g_ref, lb_ref, on_ref, ms_ref, o_ref, y_ref, st_ref, at_ref, st_sc):
        @pl.when(pl.program_id(1) == 0)
        def _():
            st_sc[...] = jnp.zeros_like(st_sc)

        msv = ms_ref[...]
        lmask, leafm = _hgrn_masks(C, leaf, False)

        def chunk(n, carry):
            rows = pl.ds(pl.multiple_of(n * C, C), C)
            for g in range(G):
                hd = slice(g * dk, (g + 1) * dk)
                tm = _hgrn_chunk_terms(q_ref[rows, hd].astype(F32), f_ref[rows, hd], lb_ref[:, hd], msv, C, leaf)
                v = v_ref[rows, hd]
                st = st_sc[g]
                st_ref[g, n] = st
                a = jnp.where(leafm, _dot_nt(tm["ql"], tm["kl"]), 0.0)
                for (h, e, selq, qm, km), m in zip(tm["lev"], lmask):
                    al = _dot_nt(qm, km)
                    a = a + (al if m is None else jnp.where(m, al, 0.0))
                at_ref[g, n] = jnp.transpose(a).astype(BF16)
                o = _dot_nt(tm["qs"], st.astype(BF16)) + _dot(a.astype(BF16), v)
                st_sc[g] = st * tm["e_c"] + _dot(v.T, tm["ke"])
                o_ref[rows, hd] = o
                rn = lax.rsqrt(jnp.mean(o * o, axis=-1, keepdims=True) + EPS)
                y = ((o * rn) * on_ref[:, hd]) * _silu(g_ref[rows, hd].astype(F32))
                y_ref[rows, hd] = y.astype(BF16)
            return carry

        lax.fori_loop(0, nc, chunk, 0)

    blk = lambda off: pl.BlockSpec((tb, G * dk), lambda h, i: (i, off // G + h))
    vec = pl.BlockSpec((1, G * dk), lambda h, i: (0, h))
    return pl.pallas_call(
        body, grid=(H // G, S // tb),
        in_specs=[blk(0), blk(0), blk(H), blk(2 * H), vec, vec,
                  pl.BlockSpec(msum.shape, lambda h, i: (0, 0))],
        out_specs=[blk(0), blk(0), pl.BlockSpec((G, nc, dk, dk), lambda h, i: (h, i, 0, 0)),
                   pl.BlockSpec((G, nc, C, C), lambda h, i: (h, i, 0, 0))],
        out_shape=[SDS((S, H * dk), F32), SDS((S, H * dk), BF16), SDS((H, S // C, dk, dk), F32),
                   SDS((H, S // C, C, C), BF16)],
        scratch_shapes=[pltpu.VMEM((G, dk, dk), F32)],
        compiler_params=_params("parallel", "arbitrary"), name=name,
    )(p1, f1, p1, p1, lb, onorm, msum)


def _hgrn_post_bwd(dy, o, p1, onorm, *, H, name, tm=512):
    S = dy.shape[0]
    dk = HEAD_DIM
    tm = min(tm, S)
    G = POST_HEADS_PER_STEP
    assert H % G == 0

    def body(dy_ref, o_ref, g_ref, on_ref, do_ref, dg_ref, don_ref):
        @pl.when(pl.program_id(1) == 0)
        def _():
            don_ref[...] = jnp.zeros_like(don_ref)

        for k in range(G):
            hd = slice(k * dk, (k + 1) * dk)
            dyv = dy_ref[:, hd].astype(F32)
            ov = o_ref[:, hd]
            g = g_ref[:, hd].astype(F32)
            onv = on_ref[:, hd]
            rn = lax.rsqrt(jnp.mean(ov * ov, axis=-1, keepdims=True) + EPS)
            oh = ov * rn
            dn = dyv * _silu(g)
            dg_ref[:, hd] = (dyv * (oh * onv) * _dsilu(g)).astype(BF16)
            don_ref[:, hd] += jnp.sum(dn * oh, axis=0, keepdims=True)
            doh = dn * onv
            do_ref[:, hd] = (rn * (doh - oh * jnp.mean(doh * oh, axis=-1, keepdims=True))).astype(BF16)

    blk = pl.BlockSpec((tm, G * dk), lambda h, i: (i, h))
    vec = pl.BlockSpec((1, G * dk), lambda h, i: (0, h))
    return pl.pallas_call(
        body, grid=(H // G, S // tm),
        in_specs=[blk, blk, pl.BlockSpec((tm, G * dk), lambda h, i: (i, 2 * H // G + h)), vec],
        out_specs=[blk, blk, vec],
        out_shape=[SDS((S, H * dk), BF16), SDS((S, H * dk), BF16), SDS((1, H * dk), F32)],
        compiler_params=_params("parallel", "arbitrary"), name=name,
    )(dy, o, p1, onorm)


def _hgrn_bwd(p1, f1, lb, do, states, a_t, *, H, name, tb=512):
    S = p1.shape[0]
    dk = HEAD_DIM
    C = min(HGRN_CHUNK, S)
    leaf = min(HGRN_LEAF, C)
    tb = min(tb, S)
    nc = tb // C
    nb = S // tb
    G = HGRN_HEADS_PER_STEP
    assert H % G == 0
    msum = jnp.asarray(_hgrn_sum_matrix(C, leaf), BF16)
    rtri = jnp.asarray(np.triu(np.ones((C, C), np.float32)), BF16)

    def body(q_ref, f_ref, v_ref, do_ref, st_ref, at_ref, lb_ref, ms_ref, rt_ref,
             dq_ref, df_ref, dv_ref, dlb_ref, g_sc):
        @pl.when(pl.program_id(1) == 0)
        def _():
            g_sc[...] = jnp.zeros_like(g_sc)
            dlb_ref[...] = jnp.zeros_like(dlb_ref)

        msv = ms_ref[...]
        rtv = rt_ref[...]
        lmask, leafm = _hgrn_masks(C, leaf, False)
        lmask_t, leafm_t = _hgrn_masks(C, leaf, True)
        f32 = lambda z: z.astype(F32)

        def head_chunk(g, n):
            hd = slice(g * dk, (g + 1) * dk)
            rows = pl.ds(pl.multiple_of(n * C, C), C)
            lbv = lb_ref[:, hd]
            qr = q_ref[rows, hd].astype(F32)
            tm = _hgrn_chunk_terms(qr, f_ref[rows, hd], lbv, msv, C, leaf)
            v = v_ref[rows, hd]
            dov = do_ref[rows, hd]
            st0 = st_ref[g, n]
            gt = g_sc[g]
            gtb = gt.astype(BF16)
            da = _dot_nt(dov, v)
            da_t = _dot_nt(v, dov)

            dal = jnp.where(leafm, da, 0.0).astype(BF16)
            dal_t = jnp.where(leafm_t, da_t, 0.0).astype(BF16)
            dql = _dot(dal, tm["kl"])
            dkl = _dot(dal_t, tm["ql"])
            dsq = dql * tm["eq"]
            dkk = dkl * tm["ek"]
            xq = f32(tm["ql"]) * dql
            xk = f32(tm["kl"]) * dkl
            for (h, e, selq, qm, km), m, m_t in zip(tm["lev"], lmask, lmask_t):
                dl = (da if m is None else jnp.where(m, da, 0.0)).astype(BF16)
                dl_t = (da_t if m_t is None else jnp.where(m_t, da_t, 0.0)).astype(BF16)
                dqm = _dot(dl, km)
                dkm = _dot(dl_t, qm)
                dsq = dsq + dqm * (e * selq)
                dkk = dkk + dkm * (e * (1.0 - selq))
                xq = xq + f32(qm) * dqm
                xk = xk + f32(km) * dkm
            dqs = _dot(dov, st0.astype(BF16))
            dke = _dot(v, gtb)
            dsq = dsq + dqs * jnp.exp(tm["b"])
            dkk = dkk + dke * jnp.exp(tm["kdec"])
            xq = xq + f32(tm["qs"]) * dqs
            xk = xk + f32(tm["ke"]) * dke
            dvv = _dot(at_ref[g, n], dov) + _dot_nt(tm["ke"], gtb)
            r_end = jnp.sum(f32(gtb) * _dot(v.T, tm["ke"]) + gt * (st0 * tm["e_c"]), axis=0, keepdims=True)
            g_sc[g] = gt * tm["e_c"] + _dot(dov.T, tm["qs"])
            xh, xm, xl = _split3(xq - xk)
            dlf = (_dot(rtv, xh) + _dot(rtv, xm)) + _dot(rtv, xl) + r_end
            dlf_f = dlf / tm["f"]
            dsp = (1.0 - lbv) * (dlf_f - dkk)
            df_ref[rows, hd] = (dsp * (tm["sp"] * tm["sn"])).astype(BF16)
            dq_ref[rows, hd] = (dsq * _dsilu(qr)).astype(BF16)
            dv_ref[rows, hd] = dvv.astype(BF16)
            dlb_ref[:, hd] += jnp.sum(dlf_f * tm["sn"] - dkk * tm["sn"], axis=0, keepdims=True)

        def chunk(nn, carry):
            for g in range(G):
                head_chunk(g, nc - 1 - nn)
            return carry

        lax.fori_loop(0, nc, chunk, 0)

    blk = lambda off: pl.BlockSpec((tb, G * dk), lambda h, i: (nb - 1 - i, off // G + h))
    vec = pl.BlockSpec((1, G * dk), lambda h, i: (0, h))
    return pl.pallas_call(
        body, grid=(H // G, nb),
        in_specs=[blk(0), blk(0), blk(H), blk(0),
                  pl.BlockSpec((G, nc, dk, dk), lambda h, i: (h, nb - 1 - i, 0, 0)),
                  pl.BlockSpec((G, nc, C, C), lambda h, i: (h, nb - 1 - i, 0, 0)), vec,
                  pl.BlockSpec(msum.shape, lambda h, i: (0, 0)), pl.BlockSpec((C, C), lambda h, i: (0, 0))],
        out_specs=[blk(0), blk(0), blk(0), vec],
        out_shape=[SDS((S, H * dk), BF16)] * 3 + [SDS((1, H * dk), F32)],
        scratch_shapes=[pltpu.VMEM((G, dk, dk), F32)],
        compiler_params=_params("parallel", "arbitrary"), name=name,
    )(p1, f1, p1, do, states, a_t, lb, msum, rtri)


def _lb_fwd(logits, *, name):
    W = logits.shape[1]

    def body(l_ref, lb_ref):
        l = l_ref[...]
        m = jnp.max(l, axis=0, keepdims=True)
        e = jnp.exp(l - m)
        p = e / jnp.sum(e, axis=0, keepdims=True)
        lb_ref[...] = (p[0:1] + p[1:2]) - p[0:1]

    return pl.pallas_call(body, out_shape=SDS((1, W), F32), name=name)(logits)


STAT_ROWS = 8


def _stats_reduce(stats_all, logits, *, name):
    W = logits.shape[1]

    def body(s_ref, l_ref, g_ref):
        tot = s_ref[0]
        for d in range(1, N_DEV):
            tot = tot + s_ref[d]
        l = l_ref[...]
        m = jnp.max(l, axis=0, keepdims=True)
        e = jnp.exp(l - m)
        p = e / jnp.sum(e, axis=0, keepdims=True)
        dlb = tot[2:3]
        dl0 = -(p[0:1] * p[1:2]) * dlb
        dl1 = (p[1:2] * (1.0 - p[1:2])) * dlb
        g_ref[0:2] = tot[0:2]
        g_ref[2:3] = dl0
        g_ref[3:4] = dl1
        g_ref[4:7] = tot[3:6]
        g_ref[7:8] = jnp.zeros((1, W), F32)

    return pl.pallas_call(body, out_shape=SDS((STAT_ROWS, W), F32), name=name)(stats_all, logits)


def _adamw(w, m, v, g_parts, *, name, tr=128):
    R, C = w.shape
    ns = len(g_parts)
    n, Rs = g_parts[0].shape[0], g_parts[0].shape[1]
    assert all(p.shape == (n, Rs, C) for p in g_parts) and ns * Rs == R
    tr = min(tr, Rs)
    assert Rs % tr == 0
    nts = Rs // tr
    c1 = 1.0 / (1.0 - ADAM_B1 ** ADAM_STEP)
    c2 = 1.0 / (1.0 - ADAM_B2 ** ADAM_STEP)

    def body(*refs):
        w_ref, m_ref, v_ref = refs[:3]
        g_refs = refs[3:3 + ns]
        go_ref, d_ref, mo_ref, vo_ref = refs[3 + ns:]

        def update(g_ref):
            g = g_ref[0].astype(F32)
            for k in range(1, n):
                g = g + g_ref[k].astype(F32)
            mn = ADAM_B1 * m_ref[...] + (1.0 - ADAM_B1) * g
            vn = ADAM_B2 * v_ref[...] + (1.0 - ADAM_B2) * (g * g)
            d_ref[...] = -ADAM_LR * ((mn * c1) / (jnp.sqrt(vn * c2) + ADAM_EPS) + ADAM_WD * w_ref[...])
            go_ref[...] = g
            mo_ref[...] = mn
            vo_ref[...] = vn

        for s in range(ns):
            if ns == 1:
                update(g_refs[s])
            else:
                pl.when(pl.program_id(0) // nts == s)(functools.partial(update, g_refs[s]))

    def g_map(i, s):
        return (0, jnp.clip(i - s * nts, 0, nts - 1), 0)

    blk = pl.BlockSpec((tr, C), lambda i: (i, 0))
    return pl.pallas_call(
        body, grid=(R // tr,),
        in_specs=[blk, blk, blk] + [pl.BlockSpec((n, tr, C), functools.partial(g_map, s=s)) for s in range(ns)],
        out_specs=[blk] * 4, out_shape=[SDS((R, C), F32)] * 4,
        compiler_params=_params("parallel"), name=name,
    )(w, m, v, *g_parts)


ANY = pl.BlockSpec(memory_space=pl.ANY)
STAGE_BYTES = 2 * 1024 * 1024


def _stage_shape(shape, dtype):
    row_bytes = int(np.prod(shape[1:])) * jnp.dtype(dtype).itemsize
    rows = max(1, min(shape[0], STAGE_BYTES // row_bytes))
    while shape[0] % rows:
        rows -= 1
    return (rows,) + tuple(shape[1:])


def _staged_copy(frm, to, buf, sems):
    rows = buf.shape[0]
    for r0 in range(0, frm.shape[0], rows):
        cp = pltpu.make_async_copy(frm.at[pl.ds(r0, rows)], buf, sems.at[0])
        cp.start()
        cp.wait()
        cp = pltpu.make_async_copy(buf, to.at[pl.ds(r0, rows)], sems.at[1])
        cp.start()
        cp.wait()


def _all_gather(shards, out_shapes, views, *, name):
    n = len(shards)

    def body(*refs):
        ins, outs = refs[:n], refs[n:2 * n]
        send_sems, recv_sems, local_sems = refs[2 * n:2 * n + 3]
        bufs = refs[2 * n + 3:]
        x, y, c = lax.axis_index("x"), lax.axis_index("y"), lax.axis_index("c")
        me, sibling = (x, y, c), (x, y, 1 - c)
        chips = [(1 - x, y), (x, 1 - y), (1 - x, 1 - y)]

        def dev(p):
            return 4 * p[0] + 2 * p[1] + p[2]

        def copy(a, k, block, to, src=None):
            dst = views[a](outs[a], dev(block))
            return pltpu.make_async_remote_copy(
                src_ref=dst if src is None else src, dst_ref=dst,
                send_sem=send_sems.at[a, k], recv_sem=recv_sems.at[a, k],
                device_id=to, device_id_type=MESH)

        first, passed = [], []
        for a in range(n):
            first.append(copy(a, 0, me, sibling, src=ins[a]))
            first += [copy(a, 1 + j, me, (*chip, c), src=ins[a]) for j, chip in enumerate(chips)]
        for cp in first:
            cp.start()
        for a in range(n):
            _staged_copy(ins[a], views[a](outs[a], dev(me)), bufs[a], local_sems)
        for j, chip in enumerate(chips):
            for a in range(n):
                copy(a, 1 + j, (*chip, c), me).wait_recv()
                cp = copy(a, 4 + j, (*chip, c), sibling)
                cp.start()
                passed.append(cp)
        for a in range(n):
            copy(a, 0, sibling, me).wait_recv()
            for j, chip in enumerate(chips):
                copy(a, 4 + j, (*chip, 1 - c), me).wait_recv()
        for cp in first + passed:
            cp.wait_send()

    return pl.pallas_call(
        body, in_specs=[ANY] * n, out_specs=[ANY] * n, out_shape=list(out_shapes),
        scratch_shapes=[pltpu.SemaphoreType.DMA((n, 7)), pltpu.SemaphoreType.DMA((n, 7)),
                        pltpu.SemaphoreType.DMA((2,))]
        + [pltpu.VMEM(_stage_shape(s.shape, s.dtype), s.dtype) for s in shards],
        name=name,
    )(*shards)


HBM = pl.BlockSpec(memory_space=pltpu.HBM)
SEM = pl.BlockSpec(memory_space=pltpu.SEMAPHORE)
EFFECT = pltpu.SideEffectType.DATAFLOW_SIDE_EFFECTING


def _relations(x, y, c):
    for m in range(1, N_DEV):
        yield m, (1 - x if m & 4 else x, 1 - y if m & 2 else y, 1 - c if m & 1 else c)


def _dev_id(p):
    return 4 * p[0] + 2 * p[1] + p[2]


def _send_start(srcs, land_shapes, src_views, dst_views, *, name):
    n = len(srcs)

    def body(*refs):
        ins, lands = refs[:n], refs[n:2 * n]
        send_sems, recv_sems, token = refs[2 * n], refs[2 * n + 1], refs[-1]
        x, y, c = lax.axis_index("x"), lax.axis_index("y"), lax.axis_index("c")
        me = _dev_id((x, y, c))
        for m, p in _relations(x, y, c):
            for a in range(n):
                pltpu.make_async_remote_copy(
                    src_ref=src_views[a](ins[a], me, _dev_id(p), m), dst_ref=dst_views[a](lands[a], me, m),
                    send_sem=send_sems.at[a * (N_DEV - 1) + m - 1], recv_sem=recv_sems.at[a * (N_DEV - 1) + m - 1],
                    device_id=p, device_id_type=MESH).start()
        token[...] = jnp.zeros_like(token)

    lands = [pltpu.with_memory_space_constraint(lax.empty(s.shape, s.dtype), pltpu.HBM) for s in land_shapes]
    srcs = [pltpu.with_memory_space_constraint(v, pltpu.HBM) for v in srcs]
    res = pl.pallas_call(
        body, name=name,
        out_shape=[pltpu.SemaphoreType.DMA((n * (N_DEV - 1),)), pltpu.SemaphoreType.DMA((n * (N_DEV - 1),))]
        + [pltpu.HBM(v.shape, v.dtype) for v in srcs] + [pltpu.HBM(s.shape, s.dtype) for s in land_shapes]
        + [SDS((8, 128), F32)],
        in_specs=[HBM] * (2 * n), out_specs=[SEM, SEM] + [HBM] * (2 * n) + [pl.BlockSpec(memory_space=pltpu.VMEM)],
        input_output_aliases={i: 2 + i for i in range(2 * n)},
        compiler_params=pltpu.CompilerParams(has_side_effects=EFFECT),
    )(*srcs, *lands)
    return res[0], res[1], res[2:2 + n], res[2 + n:2 + 2 * n], res[-1]


def _send_wait(started, src_views, dst_views, own_views, own_shapes, after, *, name):
    send_sems, recv_sems, srcs, lands, _ = started
    n = len(srcs)

    def body(*refs):
        ins, lnd = refs[:n], refs[n:2 * n]
        send_sems, recv_sems = refs[2 * n], refs[2 * n + 1]
        got = refs[2 * n + 3 + n:2 * n + 3 + 2 * n]
        local_sems = refs[2 * n + 3 + 2 * n]
        bufs = refs[2 * n + 4 + 2 * n:]
        x, y, c = lax.axis_index("x"), lax.axis_index("y"), lax.axis_index("c")
        me = _dev_id((x, y, c))
        for m, p in _relations(x, y, c):
            for a in range(n):
                cp = pltpu.make_async_remote_copy(
                    src_ref=src_views[a](ins[a], me, _dev_id(p), m), dst_ref=dst_views[a](lnd[a], me, m),
                    send_sem=send_sems.at[a * (N_DEV - 1) + m - 1], recv_sem=recv_sems.at[a * (N_DEV - 1) + m - 1],
                    device_id=p, device_id_type=MESH)
                cp.wait_send()
                cp.wait_recv()
        for a in range(n):
            frm, to = own_views[a](ins[a], got[a], me)
            _staged_copy(frm, to, bufs[a], local_sems)

    res = pl.pallas_call(
        body, name=name,
        out_shape=[pltpu.HBM(v.shape, v.dtype) for v in srcs] + [pltpu.HBM(v.shape, v.dtype) for v in lands],
        in_specs=[HBM] * (2 * n) + [SEM, SEM, ANY], out_specs=[HBM] * (2 * n),
        input_output_aliases={i: i for i in range(2 * n)},
        scratch_shapes=[pltpu.SemaphoreType.DMA((2,))]
        + [pltpu.VMEM(_stage_shape(s, v.dtype), v.dtype) for s, v in zip(own_shapes, srcs)],
        compiler_params=pltpu.CompilerParams(has_side_effects=EFFECT),
    )(*srcs, *lands, send_sems, recv_sems, after)
    return res[n:]


def kernel(x, norm_gains, fox_w_in, fox_b_f, hgrn_w_in, hgrn_lb_logits, hgrn_onorm, w_out, final_gain, loss_target, m_norm_gains, m_fox_w_in, m_fox_b_f, m_hgrn_w_in, m_hgrn_lb_logits, m_hgrn_onorm, m_w_out, m_final_gain, v_norm_gains, v_fox_w_in, v_fox_b_f, v_hgrn_w_in, v_hgrn_lb_logits, v_hgrn_onorm, v_w_out, v_final_gain):
    _, S, D = x.shape
    H = FOX_HEADS
    W = H * HEAD_DIM
    assert HGRN_HEADS == H and w_out.shape[2] == D
    cf = fox_w_in.shape[2]
    ch = hgrn_w_in.shape[2]
    ro = w_out.shape[1]
    co = hgrn_onorm.shape[1]
    assert N_DEV * cf == 4 * W + H and N_DEV * ch == 4 * W and N_DEV * ro == W and N_DEV * co == W
    x2 = x.reshape(S, D)
    tgt = loss_target.reshape(S, D)

    col = lambda n: (lambda r, i: r.at[:, pl.ds(pl.multiple_of(i * n, n), n)])
    row = lambda n: (lambda r, i: r.at[pl.ds(pl.multiple_of(i * n, n), n), :])
    late_views = [col(ch), row(ro), row(ro), col(co)]
    late = _send_start(
        [hgrn_w_in[0].astype(BF16), w_out[0].astype(BF16), w_out[1].astype(BF16), hgrn_onorm],
        [SDS((D, 4 * W), BF16), SDS((W, D), BF16), SDS((W, D), BF16), SDS((1, W), F32)],
        [lambda r, me, p, m: r] * 4, [lambda r, me, m, v=v: v(r, me) for v in late_views],
        name="gather_later_start")
    ng0 = norm_gains[0:1] + late[4][0:1, 0:1]

    (wf_g,) = _all_gather([fox_w_in[0].astype(BF16)], [SDS((N_DEV, D, cf), BF16)], [lambda r, p: r.at[p]],
                          name="gather_fox_w_in")
    wf = jnp.transpose(wf_g, (1, 0, 2)).reshape(D, N_DEV * cf)
    wf_main = jnp.concatenate([wf[:, :3 * W], wf[:, 3 * W + H:]], axis=1)
    wfl_t = wf[:, 3 * W:3 * W + H].T

    h0 = _rms_fwd(x2, ng0, name="rms0_fwd")
    p0 = _mm_nn([h0], wf_main, BF16, scale_cols=(W, LOG2E * HEAD_DIM ** -0.5), name="fox_in_proj")
    fl_t = _mm_nt_rows(wfl_t, h0, name="fox_forget_proj")
    b_col = fox_b_f.reshape(H, 1)
    kaug = _fox_key_aug(*_fox_gate_fwd(fl_t, b_col, name="fox_gate_fwd"))
    o0, y0, qaug = _fox_fwd(p0, kaug, H=H, name="fox_attn_fwd")
    wh, wo0, wo1, onorm = _send_wait(
        late, [lambda r, me, p, m: r] * 4, [lambda r, me, m, v=v: v(r, me) for v in late_views],
        [lambda src, land, me, v=v: (src, v(land, me)) for v in late_views],
        [(D, ch), (ro, D), (ro, D), (1, co)], y0[0:16], name="gather_later_wait")
    x1 = _mm_nn([y0], wo0, F32, residual=x2, name="fox_out_proj")

    lb = _lb_fwd(hgrn_lb_logits, name="hgrn_lower_bound")
    h1 = _rms_fwd(x1, norm_gains[1:2], name="rms1_fwd")
    p1 = _mm_nn([h1], wh, BF16, b_cols=[(0, W), (2 * W, 4 * W)], name="hgrn_in_proj")
    f1 = _mm_nn([h1], wh, F32, b_cols=[(W, 2 * W)], name="hgrn_forget_proj")
    o1, y1, states, a_t1 = _hgrn_fwd(p1, f1, lb, onorm, H=H, name="hgrn_fwd")
    xo = _mm_nn([y1], wo1, F32, residual=x1, name="hgrn_out_proj")

    dx2, dx2b, loss_part, dgf = _loss_head(xo, final_gain.reshape(1, D), tgt, name="loss_head")
    loss = lax.psum(jnp.sum(loss_part), ("x", "y", "c"))

    dy1 = _mm_nn([dx2b], wo1, BF16, b_t=True, name="hgrn_out_proj_dx")
    dwo1 = _mm_tn(y1, [dx2b], BF16, name="hgrn_out_proj_dw")
    do1, dg1, donorm = _hgrn_post_bwd(dy1, o1, p1, onorm, H=H, name="hgrn_post_bwd")
    dq1, df1, di1, dlb = _hgrn_bwd(p1, f1, lb, do1, states, a_t1, H=H, name="hgrn_bwd")
    segs1 = [dq1, df1, di1, dg1]
    dh1 = _mm_nn(segs1, wh, BF16, b_t=True, tn=D, name="hgrn_in_proj_dx")
    dwh = _mm_tn(h1, segs1, BF16, name="hgrn_in_proj_dw")
    part_views = [col(ch), row(ro)]
    slot = lambda r, me, m: r.at[m]
    ex1 = _send_start([dwh, dwo1], [SDS((N_DEV, D, ch), BF16), SDS((N_DEV, ro, D), BF16)],
                      [lambda r, me, p, m, v=v: v(r, p) for v in part_views], [slot] * 2,
                      name="exchange_layer1_start")
    ng1 = norm_gains[1:2] + ex1[4][0:1, 0:1]
    dx1, dx1b, dng1 = _rms_bwd(x1, ng1, dh1, dx2, name="rms1_bwd")

    dy0 = _mm_nn([dx1b], wo0, BF16, b_t=True, name="fox_out_proj_dx")
    dwo0 = _mm_tn(y0, [dx1b], BF16, name="fox_out_proj_dw")
    do0, dg0, doaug = _fox_post_bwd(dy0, o0, p0, H=H, name="fox_post_bwd")
    dq0, dc_row, dk0, dv0, dc_key = _fox_bwd(p0, kaug, qaug, do0, doaug, H=H, name="fox_attn_bwd")
    dfl_t, dbf = _fox_gate_bwd(dc_row.reshape(H, S), dc_key.reshape(H, S), fl_t, b_col, name="fox_gate_bwd")
    dfl_tb = dfl_t.astype(BF16)
    dwfl_t = _mm_nn([dfl_tb], h0, BF16, name="fox_forget_proj_dw")
    segs0 = [dq0, dk0, dv0, dg0]
    dwf_main = _mm_tn(h0, segs0, BF16, name="fox_in_proj_dw")
    dwf = jnp.concatenate([dwf_main[:, :3 * W], dwfl_t.T, dwf_main[:, 3 * W:]], axis=1)
    dwf_blocks = jnp.transpose(dwf.reshape(D, N_DEV, cf), (1, 0, 2))
    ex0 = _send_start([dwf_blocks, dwo0], [SDS((N_DEV, D, cf), BF16), SDS((N_DEV, ro, D), BF16)],
                      [lambda r, me, p, m: r.at[p], lambda r, me, p, m: row(ro)(r, p)], [slot] * 2,
                      name="exchange_layer0_start")
    wfl_t0 = wfl_t + ex0[4][0:1, 0:1].astype(BF16)
    dh0_f = _mm_nn([dfl_tb.T], wfl_t0, BF16, name="fox_forget_proj_dx")
    dh0 = _mm_nn(segs0, wf_main, BF16, residual=dh0_f, b_t=True, tn=D, name="fox_in_proj_dx")
    grad_x, _, dng0 = _rms_bwd(x2, norm_gains[0:1], dh0, dx1, name="rms0_bwd")

    own1 = [lambda src, land, me, v=v: (v(src, me), land.at[0]) for v in part_views]
    rh, ro1 = _send_wait(ex1, [lambda r, me, p, m, v=v: v(r, p) for v in part_views], [slot] * 2, own1,
                         [(D, ch), (ro, D)], dng0, name="exchange_layer1_wait")
    rf, ro0 = _send_wait(ex0, [lambda r, me, p, m: r.at[p], lambda r, me, p, m: row(ro)(r, p)], [slot] * 2,
                         [lambda src, land, me: (src.at[me], land.at[0]),
                          lambda src, land, me: (row(ro)(src, me), land.at[0])],
                         [(D, cf), (ro, D)], dng0, name="exchange_layer0_wait")

    pad = lambda a: jnp.pad(a, ((0, 0), (0, W - a.shape[1])))
    stats = jnp.concatenate([dng0, dng1, dlb, dgf, pad(dbf.reshape(1, H)), donorm,
                             jnp.zeros((2, W), F32)], axis=0)
    assert D == W
    (stats_all,) = _all_gather([stats], [SDS((N_DEV, STAT_ROWS, W), F32)], [lambda r, p: r.at[p]],
                               name="gather_small_grads")
    g_small = _stats_reduce(stats_all, hgrn_lb_logits, name="reduce_small_grads")
    me = 4 * lax.axis_index("x") + 2 * lax.axis_index("y") + lax.axis_index("c")
    g_onorm = lax.dynamic_slice_in_dim(g_small[6:7], me * co, co, axis=1)

    def upd(w, m, v, parts, name):
        shp = w.shape
        r2 = (-1, shp[-1])
        g, d, mn, vn = _adamw(w.reshape(r2), m.reshape(r2), v.reshape(r2), parts, name=name)
        return g.reshape(shp), d.reshape(shp), mn.reshape(shp), vn.reshape(shp)

    res = {
        "norm_gains": upd(norm_gains, m_norm_gains, v_norm_gains, [g_small[None, 0:2]], "adamw_norm_gains"),
        "fox_w_in": upd(fox_w_in, m_fox_w_in, v_fox_w_in, [rf], "adamw_fox_w_in"),
        "fox_b_f": upd(fox_b_f, m_fox_b_f, v_fox_b_f, [g_small[None, 5:6, :H]], "adamw_fox_b_f"),
        "hgrn_w_in": upd(hgrn_w_in, m_hgrn_w_in, v_hgrn_w_in, [rh], "adamw_hgrn_w_in"),
        "hgrn_lb_logits": upd(hgrn_lb_logits, m_hgrn_lb_logits, v_hgrn_lb_logits, [g_small[None, 2:4]],
                              "adamw_hgrn_lb_logits"),
        "hgrn_onorm": upd(hgrn_onorm, m_hgrn_onorm, v_hgrn_onorm, [g_onorm[None]], "adamw_hgrn_onorm"),
        "w_out": upd(w_out, m_w_out, v_w_out, [ro0, ro1], "adamw_w_out"),
        "final_gain": upd(final_gain.reshape(1, D), m_final_gain.reshape(1, D), v_final_gain.reshape(1, D),
                          [g_small[None, 4:5]], "adamw_final_gain"),
    }
    order = ["norm_gains", "fox_w_in", "fox_b_f", "hgrn_w_in", "hgrn_lb_logits", "hgrn_onorm", "w_out", "final_gain"]
    fix = lambda n, a: a.reshape(D) if n == "final_gain" else a
    outs = [loss, grad_x.reshape(1, S, D)]
    for k in range(4):
        outs += [fix(n, res[n][k]) for n in order]
    return tuple(outs)
```

```python
import functools

import numpy as np
import jax
import jax.numpy as jnp
from jax import lax
from jax.experimental import pallas as pl
from jax.experimental.pallas import tpu as pltpu

F32 = jnp.float32
BF16 = jnp.bfloat16
SDS = jax.ShapeDtypeStruct
MESH = pl.DeviceIdType.MESH

EPS = 1e-6
ADAM_LR, ADAM_B1, ADAM_B2, ADAM_EPS, ADAM_WD, ADAM_STEP = 0.001, 0.9, 0.999, 1e-08, 0.01, 10

N_DEV = 8
FOX_HEADS = 16
HGRN_HEADS = 16
HEAD_DIM = 128
HGRN_CHUNK = 128
HGRN_LEAF = 16
HGRN_HEADS_PER_STEP = 16
EXP_CLAMP = 85.0
ATT_BLOCK = 512
ATT_HEADS_PER_STEP = 4
ATT_BWD_HEADS_PER_STEP = 2
POST_HEADS_PER_STEP = 4
NEG = -1e30
LOG2E = 1.4426950408889634
LN2 = 0.6931471805599453

VMEM_LIMIT_V7X = 56 * 1024 * 1024


def _params(*sem):
    return pltpu.CompilerParams(dimension_semantics=sem, vmem_limit_bytes=VMEM_LIMIT_V7X)


def _silu(x):
    return x * jax.nn.sigmoid(x)


def _dsilu(x):
    s = jax.nn.sigmoid(x)
    return s * (1.0 + x * (1.0 - s))


def _dot(a, b):
    return jnp.dot(a, b, preferred_element_type=F32)


def _dot_nt(a, b):
    return lax.dot_general(a, b, (((1,), (1,)), ((), ())), preferred_element_type=F32)


def _dot_tn(a, b):
    return lax.dot_general(a, b, (((0,), (0,)), ((), ())), preferred_element_type=F32)


def _mm_nn(a_list, b, out_dtype, *, name, residual=None, scale_cols=None, b_t=False, b_cols=None,
           tm=1024, tn=1024, tk=2048):
    ns = len(a_list)
    M, Ks = a_list[0].shape
    K, N = (b.shape[1], b.shape[0]) if b_t else b.shape
    if b_cols is None:
        b_cols = [(0, N)]
    else:
        assert not b_t
        N = sum(e - s for s, e in b_cols)
    dot = _dot_nt if b_t else _dot
    assert K == ns * Ks and all(a.shape == (M, Ks) for a in a_list)
    if ns > 1:
        tk = tk // 2
    tm, tn, tk = min(tm, M), min(tn, N), min(tk, Ks)
    assert M % tm == 0 and N % tn == 0 and Ks % tk == 0
    assert scale_cols is None or scale_cols[0] % tn == 0
    assert all(s % tn == 0 and e % tn == 0 for s, e in b_cols)
    nks = Ks // tk
    nk = ns * nks
    has_res = residual is not None

    def body(*refs):
        a_refs, b_ref = refs[:ns], refs[ns]
        res_ref = refs[ns + 1] if has_res else None
        o_ref = refs[ns + 1 + has_res]

        def finish(r):
            if has_res:
                r = r + res_ref[...].astype(F32)
            if scale_cols is not None:
                r = r * jnp.where(pl.program_id(1) < scale_cols[0] // tn, scale_cols[1], 1.0)
            o_ref[...] = r.astype(out_dtype)

        if nk == 1:
            finish(dot(a_refs[0][...], b_ref[...]))
            return
        acc_ref = refs[ns + 2 + has_res]
        k = pl.program_id(2)

        @pl.when(k == 0)
        def _():
            acc_ref[...] = jnp.zeros_like(acc_ref)

        for s in range(ns):
            def step(s=s):
                acc_ref[...] += dot(a_refs[s][...], b_ref[...])

            if ns == 1:
                step()
            else:
                pl.when(k // nks == s)(step)

        @pl.when(k == nk - 1)
        def _():
            finish(acc_ref[...])

    def a_map(i, j, k, s):
        return (i, jnp.clip(k - s * nks, 0, nks - 1))

    in_specs = [pl.BlockSpec((tm, tk), functools.partial(a_map, s=s)) for s in range(ns)]
    def b_col(j):
        src = j + b_cols[0][0] // tn
        for (_, e0), (s1, _) in zip(b_cols[:-1], b_cols[1:]):
            src = src + jnp.where(src >= e0 // tn, (s1 - e0) // tn, 0)
        return src

    if b_t:
        in_specs.append(pl.BlockSpec((tn, tk), lambda i, j, k: (j, k)))
    else:
        in_specs.append(pl.BlockSpec((tk, tn), lambda i, j, k: (k, b_col(j))))
    args = list(a_list) + [b]
    if has_res:
        in_specs.append(pl.BlockSpec((tm, tn), lambda i, j, k: (i, j)))
        args.append(residual)
    return pl.pallas_call(
        body, grid=(M // tm, N // tn, nk), in_specs=in_specs,
        out_specs=pl.BlockSpec((tm, tn), lambda i, j, k: (i, j)),
        out_shape=SDS((M, N), out_dtype),
        scratch_shapes=[] if nk == 1 else [pltpu.VMEM((tm, tn), F32)],
        compiler_params=_params("parallel", "parallel", "arbitrary"), name=name,
    )(*args)


def _mm_tn(a, b_list, out_dtype, *, name, tm=2048, tn=1024, tk=512):
    ns = len(b_list)
    S, M = a.shape
    Ns = b_list[0].shape[1]
    assert all(b.shape == (S, Ns) for b in b_list)
    tm, tn, tk = min(tm, M), min(tn, Ns), min(tk, S)
    assert M % tm == 0 and Ns % tn == 0 and S % tk == 0
    njs = Ns // tn
    nk = S // tk

    def body(*refs):
        a_ref, b_refs, o_ref, acc_ref = refs[0], refs[1:1 + ns], refs[1 + ns], refs[2 + ns]
        j, k = pl.program_id(1), pl.program_id(2)

        @pl.when(k == 0)
        def _():
            acc_ref[...] = jnp.zeros_like(acc_ref)

        for s in range(ns):
            def step(s=s):
                acc_ref[...] += _dot_tn(a_ref[...], b_refs[s][...])

            if ns == 1:
                step()
            else:
                pl.when(j // njs == s)(step)

        @pl.when(k == nk - 1)
        def _():
            o_ref[...] = acc_ref[...].astype(out_dtype)

    def b_map(i, j, k, s):
        return (k, jnp.clip(j - s * njs, 0, njs - 1))

    in_specs = [pl.BlockSpec((tk, tm), lambda i, j, k: (k, i))]
    in_specs += [pl.BlockSpec((tk, tn), functools.partial(b_map, s=s)) for s in range(ns)]
    return pl.pallas_call(
        body, grid=(M // tm, ns * njs, nk), in_specs=in_specs,
        out_specs=pl.BlockSpec((tm, tn), lambda i, j, k: (i, j)),
        out_shape=SDS((M, ns * Ns), out_dtype),
        scratch_shapes=[pltpu.VMEM((tm, tn), F32)],
        compiler_params=_params("parallel", "parallel", "arbitrary"), name=name,
    )(a, *b_list)


def _mm_nt_rows(w_t, h, *, name, tn=1024):
    R, K = w_t.shape
    S = h.shape[0]
    tn = min(tn, S)

    def body(w_ref, h_ref, o_ref):
        o_ref[...] = _dot_nt(w_ref[...], h_ref[...])

    return pl.pallas_call(
        body, grid=(S // tn,),
        in_specs=[pl.BlockSpec((R, K), lambda i: (0, 0)), pl.BlockSpec((tn, K), lambda i: (i, 0))],
        out_specs=pl.BlockSpec((R, tn), lambda i: (0, i)),
        out_shape=SDS((R, S), F32), compiler_params=_params("parallel"), name=name,
    )(w_t, h)


def _rms_fwd(x, gain, *, name, tm=512):
    S, D = x.shape
    tm = min(tm, S)

    def body(x_ref, g_ref, h_ref):
        xv = x_ref[...]
        r = lax.rsqrt(jnp.mean(xv * xv, axis=-1, keepdims=True) + EPS)
        h_ref[...] = ((xv * r) * g_ref[...]).astype(BF16)

    return pl.pallas_call(
        body, grid=(S // tm,),
        in_specs=[pl.BlockSpec((tm, D), lambda i: (i, 0)), pl.BlockSpec((1, D), lambda i: (0, 0))],
        out_specs=pl.BlockSpec((tm, D), lambda i: (i, 0)),
        out_shape=SDS((S, D), BF16), compiler_params=_params("parallel"), name=name,
    )(x, gain)


def _rms_bwd(x, gain, dh, dres, *, name, tm=512):
    S, D = x.shape
    tm = min(tm, S)

    def body(x_ref, g_ref, dh_ref, dres_ref, dx_ref, dxb_ref, dg_ref):
        @pl.when(pl.program_id(0) == 0)
        def _():
            dg_ref[...] = jnp.zeros_like(dg_ref)

        xv = x_ref[...]
        r = lax.rsqrt(jnp.mean(xv * xv, axis=-1, keepdims=True) + EPS)
        xh = xv * r
        dhv = dh_ref[...].astype(F32)
        dg_ref[...] += jnp.sum(dhv * xh, axis=0, keepdims=True)
        dxh = dhv * g_ref[...]
        dx = r * (dxh - xh * jnp.mean(dxh * xh, axis=-1, keepdims=True)) + dres_ref[...]
        dx_ref[...] = dx
        dxb_ref[...] = dx.astype(BF16)

    row = pl.BlockSpec((tm, D), lambda i: (i, 0))
    vec = pl.BlockSpec((1, D), lambda i: (0, 0))
    return pl.pallas_call(
        body, grid=(S // tm,), in_specs=[row, vec, row, row], out_specs=[row, row, vec],
        out_shape=[SDS((S, D), F32), SDS((S, D), BF16), SDS((1, D), F32)],
        compiler_params=_params("arbitrary"), name=name,
    )(x, gain, dh, dres)


def _loss_head(x, gain, target, *, name, tm=512):
    S, D = x.shape
    tm = min(tm, S)
    assert tm % 8 == 0 and D % 128 == 0

    def body(x_ref, g_ref, t_ref, dx_ref, dxb_ref, loss_ref, dg_ref):
        @pl.when(pl.program_id(0) == 0)
        def _():
            dg_ref[...] = jnp.zeros_like(dg_ref)
            loss_ref[...] = jnp.zeros_like(loss_ref)

        xv = x_ref[...]
        g = g_ref[...]
        r = lax.rsqrt(jnp.mean(xv * xv, axis=-1, keepdims=True) + EPS)
        xh = xv * r
        err = xh * g - t_ref[...]
        e2 = (err * err).reshape(tm // 8, 8, D).sum(axis=0)
        part = e2[:, 0:128]
        for k in range(1, D // 128):
            part = part + e2[:, k * 128:(k + 1) * 128]
        loss_ref[...] += part * (0.5 / D)
        dy = err * (1.0 / D)
        dg_ref[...] += jnp.sum(dy * xh, axis=0, keepdims=True)
        dxh = dy * g
        dx = r * (dxh - xh * jnp.mean(dxh * xh, axis=-1, keepdims=True))
        dx_ref[...] = dx
        dxb_ref[...] = dx.astype(BF16)

    row = pl.BlockSpec((tm, D), lambda i: (i, 0))
    vec = pl.BlockSpec((1, D), lambda i: (0, 0))
    return pl.pallas_call(
        body, grid=(S // tm,), in_specs=[row, vec, row],
        out_specs=[row, row, pl.BlockSpec((8, 128), lambda i: (0, 0)), vec],
        out_shape=[SDS((S, D), F32), SDS((S, D), BF16), SDS((8, 128), F32), SDS((1, D), F32)],
        compiler_params=_params("arbitrary"), name=name,
    )(x, gain, target)


def _split3(x):
    hi = x.astype(BF16)
    r1 = x - hi.astype(F32)
    mid = r1.astype(BF16)
    lo = (r1 - mid.astype(F32)).astype(BF16)
    return hi, mid, lo


def _split2(x):
    hi = x.astype(BF16)
    lo = (x - hi.astype(F32)).astype(BF16)
    return hi, lo


def _fox_gate_fwd(fl_t, b_col, *, name):
    H, S = fl_t.shape
    L = 128
    tri = jnp.asarray(np.triu(np.ones((L, L), np.float32)), BF16)

    def body(fl_ref, b_ref, tri_ref, hi_ref, mid_ref, lo_ref, carry):
        @pl.when(pl.program_id(0) == 0)
        def _():
            carry[...] = jnp.zeros_like(carry)

        z = fl_ref[...] + b_ref[...]
        lf = jnp.minimum(z, 0.0) - jnp.log(1.0 + jnp.exp(-jnp.abs(z)))
        hi, mid, lo = _split3(lf)
        t = tri_ref[...]
        c = (_dot(hi, t) + _dot(mid, t)) + _dot(lo, t) + carry[...]
        carry[...] = c[:, L - 1:L]
        hi_ref[...], mid_ref[...], lo_ref[...] = _split3(c * (-LOG2E))

    blk = pl.BlockSpec((H, L), lambda i: (0, i))
    return pl.pallas_call(
        body, grid=(S // L,),
        in_specs=[blk, pl.BlockSpec((H, 1), lambda i: (0, 0)), pl.BlockSpec((L, L), lambda i: (0, 0))],
        out_specs=[blk] * 3, out_shape=[SDS((H, S), BF16)] * 3, scratch_shapes=[pltpu.VMEM((H, 1), F32)],
        compiler_params=_params("arbitrary"), name=name,
    )(fl_t, b_col, tri)


def _fox_gate_bwd(dc_row, dc_key, fl_t, b_col, *, name):
    H, S = fl_t.shape
    L = 128
    n = S // L
    tri = jnp.asarray(np.tril(np.ones((L, L), np.float32)), BF16)

    def body(dcr_ref, dck_ref, fl_ref, b_ref, tri_ref, dfl_ref, db_ref, carry):
        @pl.when(pl.program_id(0) == 0)
        def _():
            carry[...] = jnp.zeros_like(carry)
            db_ref[...] = jnp.zeros_like(db_ref)

        hi, mid, lo = _split3(dcr_ref[...] + dck_ref[...])
        t = tri_ref[...]
        dlf = (_dot(hi, t) + _dot(mid, t)) + _dot(lo, t) + carry[...]
        carry[...] = dlf[:, 0:1]
        z = fl_ref[...] + b_ref[...]
        dfl = dlf * jax.nn.sigmoid(-z)
        dfl_ref[...] = dfl
        db_ref[...] += jnp.sum(dfl, axis=1, keepdims=True)

    blk = pl.BlockSpec((H, L), lambda i: (0, n - 1 - i))
    col = pl.BlockSpec((H, 1), lambda i: (0, 0))
    return pl.pallas_call(
        body, grid=(n,), in_specs=[blk, blk, blk, col, pl.BlockSpec((L, L), lambda i: (0, 0))],
        out_specs=[blk, col], out_shape=[SDS((H, S), F32), SDS((H, 1), F32)],
        scratch_shapes=[pltpu.VMEM((H, 1), F32)], compiler_params=_params("arbitrary"), name=name,
    )(dc_row, dc_key, fl_t, b_col, tri)


AUG = HEAD_DIM


def _lane_select(cols, shape):
    lane = lax.broadcasted_iota(jnp.int32, shape, 1)
    out = jnp.zeros(shape, BF16)
    for k, c in reversed(list(enumerate(cols))):
        c = jnp.full(shape, c, BF16) if isinstance(c, (int, float)) else jnp.broadcast_to(c, shape).astype(BF16)
        out = jnp.where(lane == k, c, out)
    return out


def _fox_key_aug(b_hi, b_mid, b_lo):
    H, S = b_hi.shape
    ones = jnp.ones((H, S), BF16)
    ka = jnp.stack([b_hi, b_mid, b_lo, ones, ones, ones], axis=-1)
    ka = jnp.pad(ka, ((0, 0), (0, 0), (0, AUG - 6)))
    return jnp.transpose(ka, (1, 0, 2)).reshape(S, H * AUG)


def _fox_fwd(p0, kaug, *, H, name):
    S = p0.shape[0]
    T = min(ATT_BLOCK, S)
    nq = S // T
    dh = HEAD_DIM
    G = ATT_HEADS_PER_STEP
    assert H % G == 0

    def body(q_ref, k_ref, ka_ref, v_ref, g_ref, o_ref, y_ref, qa_ref, m_sc, acc_sc, p_sc, al_sc):
        i = pl.program_id(1)
        qaug = _lane_select([1.0, 1.0, 1.0], (T, AUG))
        ones = jnp.ones((T, dh), BF16)
        m_sc[...] = jnp.full_like(m_sc, NEG)
        acc_sc[...] = jnp.zeros_like(acc_sc)

        def step(j, before, masked):
            rows = pl.ds(pl.multiple_of(j * T, T), T)
            for g in range(G):
                hd = slice(g * dh, (g + 1) * dh)
                if not masked:
                    prev = pl.ds(pl.multiple_of(before * T, T), T)
                    vp = jnp.concatenate([v_ref[prev, hd], ones], axis=1)
                    acc_sc[g] = jnp.tile(al_sc[g], (1, 2)) * acc_sc[g] + _dot(p_sc[g], vp)
                q = jnp.concatenate([q_ref[:, hd], qaug], axis=1)
                kj = jnp.concatenate([k_ref[rows, hd], ka_ref[rows, hd]], axis=1)
                t = _dot_nt(q, kj)
                if masked:
                    row = lax.broadcasted_iota(jnp.int32, (T, T), 0)
                    col = lax.broadcasted_iota(jnp.int32, (T, T), 1)
                    t = jnp.where(row >= col, t, NEG)
                m_prev = m_sc[g]
                m_new = jnp.maximum(m_prev, jnp.max(t, axis=-1, keepdims=True))
                p_sc[g] = jnp.exp2(t - jnp.tile(m_new, (1, T // 128))).astype(BF16)
                al_sc[g] = jnp.exp2(m_prev - m_new)
                m_sc[g] = m_new

        step(i, None, True)

        def loop_body(j, carry):
            step(j, jnp.where(j == 0, i, j - 1), False)
            return carry

        lax.fori_loop(0, i, loop_body, 0)
        rows = pl.ds(pl.multiple_of(jnp.where(i == 0, 0, i - 1) * T, T), T)
        for g in range(G):
            hd = slice(g * dh, (g + 1) * dh)
            vp = jnp.concatenate([v_ref[rows, hd], ones], axis=1)
            acc = jnp.tile(al_sc[g], (1, 2)) * acc_sc[g] + _dot(p_sc[g], vp)
            l = acc[:, dh:]
            o = acc[:, :dh] / l
            o_ref[:, hd] = o
            y_ref[:, hd] = (o * _silu(g_ref[:, hd].astype(F32))).astype(BF16)
            hi, mid, lo = _split3(-(m_sc[g] + jnp.log2(l)))
            qa_ref[:, hd] = _lane_select([1.0, 1.0, 1.0, hi, mid, lo], (T, AUG))

    blk = lambda off: pl.BlockSpec((T, G * dh), lambda h, i: (i, off // G + h))
    full = lambda off: pl.BlockSpec((S, G * dh), lambda h, i: (0, off // G + h), pipeline_mode=pl.Buffered(1))
    return pl.pallas_call(
        body, grid=(H // G, nq),
        in_specs=[blk(0), full(H), full(0), full(2 * H), blk(3 * H)],
        out_specs=[blk(0), blk(0), blk(0)],
        out_shape=[SDS((S, H * dh), F32), SDS((S, H * dh), BF16), SDS((S, H * AUG), BF16)],
        scratch_shapes=[pltpu.VMEM((G, T, 128), F32), pltpu.VMEM((G, T, 2 * dh), F32),
                        pltpu.VMEM((G, T, T), BF16), pltpu.VMEM((G, T, 128), F32)],
        compiler_params=_params("parallel", "arbitrary"), name=name,
    )(p0, p0, kaug, p0, p0)


def _fox_post_bwd(dy, o, p0, *, H, name, tm=512):
    S = dy.shape[0]
    dh = HEAD_DIM
    tm = min(tm, S)
    G = POST_HEADS_PER_STEP
    assert H % G == 0

    def body(dy_ref, o_ref, g_ref, do_ref, dg_ref, da_ref):
        dyv = dy_ref[...].astype(F32)
        ov = o_ref[...]
        g = g_ref[...].astype(F32)
        do = (dyv * _silu(g)).astype(BF16)
        do_ref[...] = do
        dg_ref[...] = (dyv * ov * _dsilu(g)).astype(BF16)
        prod = do.astype(F32) * ov
        for k in range(G):
            hd = slice(k * dh, (k + 1) * dh)
            delta = jnp.sum(prod[:, hd], axis=-1, keepdims=True)
            hi, mid, lo = _split3(-jnp.broadcast_to(delta, (tm, AUG)))
            da_ref[:, hd] = _lane_select([hi, mid, lo], (tm, AUG))

    blk = pl.BlockSpec((tm, G * dh), lambda h, i: (i, h))
    return pl.pallas_call(
        body, grid=(H // G, S // tm),
        in_specs=[blk, blk, pl.BlockSpec((tm, G * dh), lambda h, i: (i, 3 * H // G + h))],
        out_specs=[blk, blk, blk],
        out_shape=[SDS((S, H * dh), BF16), SDS((S, H * dh), BF16), SDS((S, H * AUG), BF16)],
        compiler_params=_params("parallel", "parallel"), name=name,
    )(dy, o, p0)


def _fox_bwd(p0, kaug, qaug, do, doaug, *, H, name):
    S = p0.shape[0]
    T = min(ATT_BLOCK, S)
    nq = S // T
    dh = HEAD_DIM
    scale = dh ** -0.5
    G = ATT_BWD_HEADS_PER_STEP
    assert H % G == 0

    def body(q_ref, qa_ref, k_ref, ka_ref, v_ref, do_ref, da_ref, dq_ref, rs_ref, dk_ref, dv_ref, dc_ref,
             dq_sc, dk_sc, dv_sc, pt_sc, dst_sc):
        j = pl.program_id(1)
        vaug = _lane_select([1.0, 1.0, 1.0], (T, AUG))
        ones = jnp.ones((T, dh), BF16)

        @pl.when(j == 0)
        def _():
            dq_sc[...] = jnp.zeros_like(dq_sc)

        dk_sc[...] = jnp.zeros_like(dk_sc)
        dv_sc[...] = jnp.zeros_like(dv_sc)

        def apply(prev):
            for g in range(G):
                hd = slice(g * dh, (g + 1) * dh)
                dv_sc[g] += _dot(pt_sc[g], do_ref[prev, hd])
                dk_sc[g] += _dot(dst_sc[g], jnp.concatenate([q_ref[prev, hd], ones], axis=1))
                dq_sc[g, prev] += _dot_tn(dst_sc[g], jnp.concatenate([k_ref[:, hd], ones], axis=1))

        def step(i, masked):
            rows = pl.ds(pl.multiple_of(i * T, T), T)
            if not masked:
                apply(pl.ds(pl.multiple_of((i - 1) * T, T), T))
            for g in range(G):
                hd = slice(g * dh, (g + 1) * dh)
                k = jnp.concatenate([k_ref[:, hd], ka_ref[:, hd]], axis=1)
                v = jnp.concatenate([v_ref[:, hd], vaug], axis=1)
                pt = jnp.exp2(_dot_nt(k, jnp.concatenate([q_ref[rows, hd], qa_ref[rows, hd]], axis=1)))
                if masked:
                    row = lax.broadcasted_iota(jnp.int32, (T, T), 0)
                    col = lax.broadcasted_iota(jnp.int32, (T, T), 1)
                    pt = jnp.where(col >= row, pt, 0.0)
                dst = pt * _dot_nt(v, jnp.concatenate([do_ref[rows, hd], da_ref[rows, hd]], axis=1))
                pt_sc[g] = pt.astype(BF16)
                dst_sc[g] = dst.astype(BF16)

        step(j, True)

        def loop_body(i, carry):
            step(i, False)
            return carry

        lax.fori_loop(j + 1, nq, loop_body, 0)
        apply(pl.ds((nq - 1) * T, T))
        for g in range(G):
            hd = slice(g * dh, (g + 1) * dh)
            dk_ref[:, hd] = (dk_sc[g, :, :dh] * LN2).astype(BF16)
            dv_ref[:, hd] = dv_sc[g].astype(BF16)
            dc_ref[g] = -jnp.transpose(dk_sc[g, :, dh:])[0:1]

        @pl.when(j == nq - 1)
        def _():
            for g in range(G):
                dq_ref[:, g * dh:(g + 1) * dh] = (dq_sc[g, :, :dh] * scale).astype(BF16)
                for i in range(nq):
                    rs_ref[g, :, i * T:(i + 1) * T] = jnp.transpose(dq_sc[g, i * T:(i + 1) * T, dh:])[0:1]

    blk = lambda off: pl.BlockSpec((T, G * dh), lambda h, j: (j, off // G + h))
    full = lambda off: pl.BlockSpec((S, G * dh), lambda h, j: (0, off // G + h))
    once = lambda off: pl.BlockSpec((S, G * dh), lambda h, j: (0, off // G + h), pipeline_mode=pl.Buffered(1))
    rowv = pl.BlockSpec((G, 1, T), lambda h, j: (h, 0, j))
    return pl.pallas_call(
        body, grid=(H // G, nq),
        in_specs=[once(0), once(0), blk(H), blk(0), blk(2 * H), once(0), once(0)],
        out_specs=[full(0), pl.BlockSpec((G, 1, S), lambda h, j: (h, 0, 0)), blk(0), blk(0), rowv],
        out_shape=[SDS((S, H * dh), BF16), SDS((H, 1, S), F32), SDS((S, H * dh), BF16), SDS((S, H * dh), BF16),
                   SDS((H, 1, S), F32)],
        scratch_shapes=[pltpu.VMEM((G, S, 2 * dh), F32), pltpu.VMEM((G, T, 2 * dh), F32), pltpu.VMEM((G, T, dh), F32),
                        pltpu.VMEM((G, T, T), BF16), pltpu.VMEM((G, T, T), BF16)],
        compiler_params=_params("parallel", "arbitrary"), name=name,
    )(p0, qaug, p0, kaug, p0, do, doaug)


def _hgrn_levels(C, leaf):
    levels = []
    h = C // 2
    while h >= leaf:
        levels.append(h)
        h //= 2
    return levels


def _hgrn_sum_matrix(C, leaf):
    t = np.arange(C)[:, None]
    u = np.arange(C)[None, :]
    mats = [(u <= t), (u > t)]
    for h in _hgrn_levels(C, leaf):
        start = (t // (2 * h)) * (2 * h)
        mid = start + h - 1
        second = t > mid
        m = np.where(second, (u > mid) & (u <= t), (u > t) & (u <= mid))
        mats.append(m)
    lstart = (t // leaf) * leaf
    mats.append((u >= lstart) & (u <= t))
    return np.concatenate([m.astype(np.float32) for m in mats], axis=0)


def _hgrn_chunk_terms(qr, fz, lb, msum, C, leaf):
    levels = _hgrn_levels(C, leaf)
    sq = _silu(qr)
    sp = 1.0 / (1.0 + jnp.exp(-fz))
    sn = 1.0 / (1.0 + jnp.exp(fz))
    f = lb + (1.0 - lb) * sp
    lf = jnp.log(f)
    k = (1.0 - lb) * sn
    hi, lo = _split2(lf)
    dsum = _dot(msum, hi) + _dot(msum, lo)
    b = dsum[0:C]
    kdec = dsum[C:2 * C]
    rowi = lax.broadcasted_iota(jnp.int32, (C, 1), 0)
    lev = []
    for n, h in enumerate(levels):
        e = jnp.exp(dsum[(2 + n) * C:(3 + n) * C])
        selq = jnp.where((rowi % (2 * h)) >= h, 1.0, 0.0)
        qm = (sq * e * selq).astype(BF16)
        km = (k * e * (1.0 - selq)).astype(BF16)
        lev.append((h, e, selq, qm, km))
    dleaf = dsum[(2 + len(levels)) * C:(3 + len(levels)) * C]
    eq = jnp.exp(dleaf)
    ek = jnp.exp(jnp.minimum(-dleaf, EXP_CLAMP))
    return dict(sq=sq, sp=sp, sn=sn, f=f, k=k, b=b, kdec=kdec, lev=lev, eq=eq, ek=ek,
                ql=(sq * eq).astype(BF16), kl=(k * ek).astype(BF16),
                qs=(sq * jnp.exp(b)).astype(BF16), ke=(k * jnp.exp(kdec)).astype(BF16),
                e_c=jnp.exp(b[C - 1:C, :]))


def _hgrn_masks(C, leaf, transposed):
    a = lax.broadcasted_iota(jnp.int32, (C, C), 0)
    bb = lax.broadcasted_iota(jnp.int32, (C, C), 1)
    t, s = (bb, a) if transposed else (a, bb)
    lev = [None if 2 * h == C else (t // (2 * h)) == (s // (2 * h)) for h in _hgrn_levels(C, leaf)]
    if leaf == C:
        leafm = s <= t
    else:
        leafm = ((t // leaf) == (s // leaf)) & (s <= t)
    return lev, leafm


def _hgrn_fwd(p1, f1, lb, onorm, *, H, name, tb=512):
    S = p1.shape[0]
    dk = HEAD_DIM
    C = min(HGRN_CHUNK, S)
    leaf = min(HGRN_LEAF, C)
    tb = min(tb, S)
    nc = tb // C
    G = HGRN_HEADS_PER_STEP
    assert H % G == 0
    msum = jnp.asarray(_hgrn_sum_matrix(C, leaf), BF16)

    def body(q_ref, f_ref, v_ref, g_ref, lb_ref, on_ref, ms_ref, o_ref, y_ref, st_ref, at_ref, st_sc):
        @pl.when(pl.program_id(1) == 0)
        def _():
            st_sc[...] = jnp.zeros_like(st_sc)

        msv = ms_ref[...]
        lmask, leafm = _hgrn_masks(C, leaf, False)

        def chunk(n, carry):
            rows = pl.ds(pl.multiple_of(n * C, C), C)
            for g in range(G):
                hd = slice(g * dk, (g + 1) * dk)
                tm = _hgrn_chunk_terms(q_ref[rows, hd].astype(F32), f_ref[rows, hd], lb_ref[:, hd], msv, C, leaf)
                v = v_ref[rows, hd]
                st = st_sc[g]
                st_ref[g, n] = st
                a = jnp.where(leafm, _dot_nt(tm["ql"], tm["kl"]), 0.0)
                for (h, e, selq, qm, km), m in zip(tm["lev"], lmask):
                    al = _dot_nt(qm, km)
                    a = a + (al if m is None else jnp.where(m, al, 0.0))
                at_ref[g, n] = jnp.transpose(a).astype(BF16)
                o = _dot_nt(tm["qs"], st.astype(BF16)) + _dot(a.astype(BF16), v)
                st_sc[g] = st * tm["e_c"] + _dot(v.T, tm["ke"])
                o_ref[rows, hd] = o
                rn = lax.rsqrt(jnp.mean(o * o, axis=-1, keepdims=True) + EPS)
                y = ((o * rn) * on_ref[:, hd]) * _silu(g_ref[rows, hd].astype(F32))
                y_ref[rows, hd] = y.astype(BF16)
            return carry

        lax.fori_loop(0, nc, chunk, 0)

    blk = lambda off: pl.BlockSpec((tb, G * dk), lambda h, i: (i, off // G + h))
    vec = pl.BlockSpec((1, G * dk), lambda h, i: (0, h))
    return pl.pallas_call(
        body, grid=(H // G, S // tb),
        in_specs=[blk(0), blk(0), blk(H), blk(2 * H), vec, vec,
                  pl.BlockSpec(msum.shape, lambda h, i: (0, 0))],
        out_specs=[blk(0), blk(0), pl.BlockSpec((G, nc, dk, dk), lambda h, i: (h, i, 0, 0)),
                   pl.BlockSpec((G, nc, C, C), lambda h, i: (h, i, 0, 0))],
        out_shape=[SDS((S, H * dk), F32), SDS((S, H * dk), BF16), SDS((H, S // C, dk, dk), F32),
                   SDS((H, S // C, C, C), BF16)],
        scratch_shapes=[pltpu.VMEM((G, dk, dk), F32)],
        compiler_params=_params("parallel", "arbitrary"), name=name,
    )(p1, f1, p1, p1, lb, onorm, msum)


def _hgrn_post_bwd(dy, o, p1, onorm, *, H, name, tm=512):
    S = dy.shape[0]
    dk = HEAD_DIM
    tm = min(tm, S)
    G = POST_HEADS_PER_STEP
    assert H % G == 0

    def body(dy_ref, o_ref, g_ref, on_ref, do_ref, dg_ref, don_ref):
        @pl.when(pl.program_id(1) == 0)
        def _():
            don_ref[...] = jnp.zeros_like(don_ref)

        for k in range(G):
            hd = slice(k * dk, (k + 1) * dk)
            dyv = dy_ref[:, hd].astype(F32)
            ov = o_ref[:, hd]
            g = g_ref[:, hd].astype(F32)
            onv = on_ref[:, hd]
            rn = lax.rsqrt(jnp.mean(ov * ov, axis=-1, keepdims=True) + EPS)
            oh = ov * rn
            dn = dyv * _silu(g)
            dg_ref[:, hd] = (dyv * (oh * onv) * _dsilu(g)).astype(BF16)
            don_ref[:, hd] += jnp.sum(dn * oh, axis=0, keepdims=True)
            doh = dn * onv
            do_ref[:, hd] = (rn * (doh - oh * jnp.mean(doh * oh, axis=-1, keepdims=True))).astype(BF16)

    blk = pl.BlockSpec((tm, G * dk), lambda h, i: (i, h))
    vec = pl.BlockSpec((1, G * dk), lambda h, i: (0, h))
    return pl.pallas_call(
        body, grid=(H // G, S // tm),
        in_specs=[blk, blk, pl.BlockSpec((tm, G * dk), lambda h, i: (i, 2 * H // G + h)), vec],
        out_specs=[blk, blk, vec],
        out_shape=[SDS((S, H * dk), BF16), SDS((S, H * dk), BF16), SDS((1, H * dk), F32)],
        compiler_params=_params("parallel", "arbitrary"), name=name,
    )(dy, o, p1, onorm)


def _hgrn_bwd(p1, f1, lb, do, states, a_t, *, H, name, tb=512):
    S = p1.shape[0]
    dk = HEAD_DIM
    C = min(HGRN_CHUNK, S)
    leaf = min(HGRN_LEAF, C)
    tb = min(tb, S)
    nc = tb // C
    nb = S // tb
    G = HGRN_HEADS_PER_STEP
    assert H % G == 0
    msum = jnp.asarray(_hgrn_sum_matrix(C, leaf), BF16)
    rtri = jnp.asarray(np.triu(np.ones((C, C), np.float32)), BF16)

    def body(q_ref, f_ref, v_ref, do_ref, st_ref, at_ref, lb_ref, ms_ref, rt_ref,
             dq_ref, df_ref, dv_ref, dlb_ref, g_sc):
        @pl.when(pl.program_id(1) == 0)
        def _():
            g_sc[...] = jnp.zeros_like(g_sc)
            dlb_ref[...] = jnp.zeros_like(dlb_ref)

        msv = ms_ref[...]
        rtv = rt_ref[...]
        lmask, leafm = _hgrn_masks(C, leaf, False)
        lmask_t, leafm_t = _hgrn_masks(C, leaf, True)
        f32 = lambda z: z.astype(F32)

        def head_chunk(g, n):
            hd = slice(g * dk, (g + 1) * dk)
            rows = pl.ds(pl.multiple_of(n * C, C), C)
            lbv = lb_ref[:, hd]
            qr = q_ref[rows, hd].astype(F32)
            tm = _hgrn_chunk_terms(qr, f_ref[rows, hd], lbv, msv, C, leaf)
            v = v_ref[rows, hd]
            dov = do_ref[rows, hd]
            st0 = st_ref[g, n]
            gt = g_sc[g]
            gtb = gt.astype(BF16)
            da = _dot_nt(dov, v)
            da_t = _dot_nt(v, dov)

            dal = jnp.where(leafm, da, 0.0).astype(BF16)
            dal_t = jnp.where(leafm_t, da_t, 0.0).astype(BF16)
            dql = _dot(dal, tm["kl"])
            dkl = _dot(dal_t, tm["ql"])
            dsq = dql * tm["eq"]
            dkk = dkl * tm["ek"]
            xq = f32(tm["ql"]) * dql
            xk = f32(tm["kl"]) * dkl
            for (h, e, selq, qm, km), m, m_t in zip(tm["lev"], lmask, lmask_t):
                dl = (da if m is None else jnp.where(m, da, 0.0)).astype(BF16)
                dl_t = (da_t if m_t is None else jnp.where(m_t, da_t, 0.0)).astype(BF16)
                dqm = _dot(dl, km)
                dkm = _dot(dl_t, qm)
                dsq = dsq + dqm * (e * selq)
                dkk = dkk + dkm * (e * (1.0 - selq))
                xq = xq + f32(qm) * dqm
                xk = xk + f32(km) * dkm
            dqs = _dot(dov, st0.astype(BF16))
            dke = _dot(v, gtb)
            dsq = dsq + dqs * jnp.exp(tm["b"])
            dkk = dkk + dke * jnp.exp(tm["kdec"])
            xq = xq + f32(tm["qs"]) * dqs
            xk = xk + f32(tm["ke"]) * dke
            dvv = _dot(at_ref[g, n], dov) + _dot_nt(tm["ke"], gtb)
            r_end = jnp.sum(f32(gtb) * _dot(v.T, tm["ke"]) + gt * (st0 * tm["e_c"]), axis=0, keepdims=True)
            g_sc[g] = gt * tm["e_c"] + _dot(dov.T, tm["qs"])
            xh, xm, xl = _split3(xq - xk)
            dlf = (_dot(rtv, xh) + _dot(rtv, xm)) + _dot(rtv, xl) + r_end
            dlf_f = dlf / tm["f"]
            dsp = (1.0 - lbv) * (dlf_f - dkk)
            df_ref[rows, hd] = (dsp * (tm["sp"] * tm["sn"])).astype(BF16)
            dq_ref[rows, hd] = (dsq * _dsilu(qr)).astype(BF16)
            dv_ref[rows, hd] = dvv.astype(BF16)
            dlb_ref[:, hd] += jnp.sum(dlf_f * tm["sn"] - dkk * tm["sn"], axis=0, keepdims=True)

        def chunk(nn, carry):
            for g in range(G):
                head_chunk(g, nc - 1 - nn)
            return carry

        lax.fori_loop(0, nc, chunk, 0)

    blk = lambda off: pl.BlockSpec((tb, G * dk), lambda h, i: (nb - 1 - i, off // G + h))
    vec = pl.BlockSpec((1, G * dk), lambda h, i: (0, h))
    return pl.pallas_call(
        body, grid=(H // G, nb),
        in_specs=[blk(0), blk(0), blk(H), blk(0),
                  pl.BlockSpec((G, nc, dk, dk), lambda h, i: (h, nb - 1 - i, 0, 0)),
                  pl.BlockSpec((G, nc, C, C), lambda h, i: (h, nb - 1 - i, 0, 0)), vec,
                  pl.BlockSpec(msum.shape, lambda h, i: (0, 0)), pl.BlockSpec((C, C), lambda h, i: (0, 0))],
        out_specs=[blk(0), blk(0), blk(0), vec],
        out_shape=[SDS((S, H * dk), BF16)] * 3 + [SDS((1, H * dk), F32)],
        scratch_shapes=[pltpu.VMEM((G, dk, dk), F32)],
        compiler_params=_params("parallel", "arbitrary"), name=name,
    )(p1, f1, p1, do, states, a_t, lb, msum, rtri)


def _lb_fwd(logits, *, name):
    W = logits.shape[1]

    def body(l_ref, lb_ref):
        l = l_ref[...]
        m = jnp.max(l, axis=0, keepdims=True)
        e = jnp.exp(l - m)
        p = e / jnp.sum(e, axis=0, keepdims=True)
        lb_ref[...] = (p[0:1] + p[1:2]) - p[0:1]

    return pl.pallas_call(body, out_shape=SDS((1, W), F32), name=name)(logits)


STAT_ROWS = 8


def _stats_reduce(stats_all, logits, *, name):
    W = logits.shape[1]

    def body(s_ref, l_ref, g_ref):
        tot = s_ref[0]
        for d in range(1, N_DEV):
            tot = tot + s_ref[d]
        l = l_ref[...]
        m = jnp.max(l, axis=0, keepdims=True)
        e = jnp.exp(l - m)
        p = e / jnp.sum(e, axis=0, keepdims=True)
        dlb = tot[2:3]
        dl0 = -(p[0:1] * p[1:2]) * dlb
        dl1 = (p[1:2] * (1.0 - p[1:2])) * dlb
        g_ref[0:2] = tot[0:2]
        g_ref[2:3] = dl0
        g_ref[3:4] = dl1
        g_ref[4:7] = tot[3:6]
        g_ref[7:8] = jnp.zeros((1, W), F32)

    return pl.pallas_call(body, out_shape=SDS((STAT_ROWS, W), F32), name=name)(stats_all, logits)


def _adamw(w, m, v, g_parts, *, name, tr=128):
    R, C = w.shape
    ns = len(g_parts)
    n, Rs = g_parts[0].shape[0], g_parts[0].shape[1]
    assert all(p.shape == (n, Rs, C) for p in g_parts) and ns * Rs == R
    tr = min(tr, Rs)
    assert Rs % tr == 0
    nts = Rs // tr
    c1 = 1.0 / (1.0 - ADAM_B1 ** ADAM_STEP)
    c2 = 1.0 / (1.0 - ADAM_B2 ** ADAM_STEP)

    def body(*refs):
        w_ref, m_ref, v_ref = refs[:3]
        g_refs = refs[3:3 + ns]
        go_ref, d_ref, mo_ref, vo_ref = refs[3 + ns:]

        def update(g_ref):
            g = g_ref[0].astype(F32)
            for k in range(1, n):
                g = g + g_ref[k].astype(F32)
            mn = ADAM_B1 * m_ref[...] + (1.0 - ADAM_B1) * g
            vn = ADAM_B2 * v_ref[...] + (1.0 - ADAM_B2) * (g * g)
            d_ref[...] = -ADAM_LR * ((mn * c1) / (jnp.sqrt(vn * c2) + ADAM_EPS) + ADAM_WD * w_ref[...])
            go_ref[...] = g
            mo_ref[...] = mn
            vo_ref[...] = vn

        for s in range(ns):
            if ns == 1:
                update(g_refs[s])
            else:
                pl.when(pl.program_id(0) // nts == s)(functools.partial(update, g_refs[s]))

    def g_map(i, s):
        return (0, jnp.clip(i - s * nts, 0, nts - 1), 0)

    blk = pl.BlockSpec((tr, C), lambda i: (i, 0))
    return pl.pallas_call(
        body, grid=(R // tr,),
        in_specs=[blk, blk, blk] + [pl.BlockSpec((n, tr, C), functools.partial(g_map, s=s)) for s in range(ns)],
        out_specs=[blk] * 4, out_shape=[SDS((R, C), F32)] * 4,
        compiler_params=_params("parallel"), name=name,
    )(w, m, v, *g_parts)


ANY = pl.BlockSpec(memory_space=pl.ANY)
STAGE_BYTES = 2 * 1024 * 1024


def _stage_shape(shape, dtype):
    row_bytes = int(np.prod(shape[1:])) * jnp.dtype(dtype).itemsize
    rows = max(1, min(shape[0], STAGE_BYTES // row_bytes))
    while shape[0] % rows:
        rows -= 1
    return (rows,) + tuple(shape[1:])


def _staged_copy(frm, to, buf, sems):
    rows = buf.shape[0]
    for r0 in range(0, frm.shape[0], rows):
        cp = pltpu.make_async_copy(frm.at[pl.ds(r0, rows)], buf, sems.at[0])
        cp.start()
        cp.wait()
        cp = pltpu.make_async_copy(buf, to.at[pl.ds(r0, rows)], sems.at[1])
        cp.start()
        cp.wait()


def _all_gather(shards, out_shapes, views, *, name):
    n = len(shards)

    def body(*refs):
        ins, outs = refs[:n], refs[n:2 * n]
        send_sems, recv_sems, local_sems = refs[2 * n:2 * n + 3]
        bufs = refs[2 * n + 3:]
        x, y, c = lax.axis_index("x"), lax.axis_index("y"), lax.axis_index("c")
        me, sibling = (x, y, c), (x, y, 1 - c)
        chips = [(1 - x, y), (x, 1 - y), (1 - x, 1 - y)]

        def dev(p):
            return 4 * p[0] + 2 * p[1] + p[2]

        def copy(a, k, block, to, src=None):
            dst = views[a](outs[a], dev(block))
            return pltpu.make_async_remote_copy(
                src_ref=dst if src is None else src, dst_ref=dst,
                send_sem=send_sems.at[a, k], recv_sem=recv_sems.at[a, k],
                device_id=to, device_id_type=MESH)

        first, passed = [], []
        for a in range(n):
            first.append(copy(a, 0, me, sibling, src=ins[a]))
            first += [copy(a, 1 + j, me, (*chip, c), src=ins[a]) for j, chip in enumerate(chips)]
        for cp in first:
            cp.start()
        for a in range(n):
            _staged_copy(ins[a], views[a](outs[a], dev(me)), bufs[a], local_sems)
        for j, chip in enumerate(chips):
            for a in range(n):
                copy(a, 1 + j, (*chip, c), me).wait_recv()
                cp = copy(a, 4 + j, (*chip, c), sibling)
                cp.start()
                passed.append(cp)
        for a in range(n):
            copy(a, 0, sibling, me).wait_recv()
            for j, chip in enumerate(chips):
                copy(a, 4 + j, (*chip, 1 - c), me).wait_recv()
        for cp in first + passed:
            cp.wait_send()

    return pl.pallas_call(
        body, in_specs=[ANY] * n, out_specs=[ANY] * n, out_shape=list(out_shapes),
        scratch_shapes=[pltpu.SemaphoreType.DMA((n, 7)), pltpu.SemaphoreType.DMA((n, 7)),
                        pltpu.SemaphoreType.DMA((2,))]
        + [pltpu.VMEM(_stage_shape(s.shape, s.dtype), s.dtype) for s in shards],
        name=name,
    )(*shards)


HBM = pl.BlockSpec(memory_space=pltpu.HBM)
SEM = pl.BlockSpec(memory_space=pltpu.SEMAPHORE)
EFFECT = pltpu.SideEffectType.DATAFLOW_SIDE_EFFECTING


def _relations(x, y, c):
    for m in range(1, N_DEV):
        yield m, (1 - x if m & 4 else x, 1 - y if m & 2 else y, 1 - c if m & 1 else c)


def _dev_id(p):
    return 4 * p[0] + 2 * p[1] + p[2]


def _send_start(srcs, land_shapes, src_views, dst_views, after, *, name):
    n = len(srcs)

    def body(*refs):
        ins, lands = refs[:n], refs[n:2 * n]
        send_sems, recv_sems, token = refs[2 * n + 1], refs[2 * n + 2], refs[-1]
        x, y, c = lax.axis_index("x"), lax.axis_index("y"), lax.axis_index("c")
        me = _dev_id((x, y, c))
        for m, p in _relations(x, y, c):
            for a in range(n):
                pltpu.make_async_remote_copy(
                    src_ref=src_views[a](ins[a], me, _dev_id(p), m), dst_ref=dst_views[a](lands[a], me, m),
                    send_sem=send_sems.at[a * (N_DEV - 1) + m - 1], recv_sem=recv_sems.at[a * (N_DEV - 1) + m - 1],
                    device_id=p, device_id_type=MESH).start()
        token[...] = jnp.zeros_like(token)

    lands = [pltpu.with_memory_space_constraint(lax.empty(s.shape, s.dtype), pltpu.HBM) for s in land_shapes]
    srcs = [pltpu.with_memory_space_constraint(v, pltpu.HBM) for v in srcs]
    res = pl.pallas_call(
        body, name=name,
        out_shape=[pltpu.SemaphoreType.DMA((n * (N_DEV - 1),)), pltpu.SemaphoreType.DMA((n * (N_DEV - 1),))]
        + [pltpu.HBM(v.shape, v.dtype) for v in srcs] + [pltpu.HBM(s.shape, s.dtype) for s in land_shapes]
        + [SDS((8, 128), F32)],
        in_specs=[HBM] * (2 * n) + [ANY],
        out_specs=[SEM, SEM] + [HBM] * (2 * n) + [pl.BlockSpec(memory_space=pltpu.VMEM)],
        input_output_aliases={i: 2 + i for i in range(2 * n)},
        compiler_params=pltpu.CompilerParams(has_side_effects=EFFECT),
    )(*srcs, *lands, after)
    return res[0], res[1], res[2:2 + n], res[2 + n:2 + 2 * n], res[-1]


def _send_wait(started, src_views, dst_views, own_views, own_shapes, after, *, name):
    send_sems, recv_sems, srcs, lands, _ = started
    n = len(srcs)

    def body(*refs):
        ins, lnd = refs[:n], refs[n:2 * n]
        send_sems, recv_sems = refs[2 * n], refs[2 * n + 1]
        got = refs[2 * n + 3 + n:2 * n + 3 + 2 * n]
        local_sems = refs[2 * n + 3 + 2 * n]
        bufs = refs[2 * n + 4 + 2 * n:]
        x, y, c = lax.axis_index("x"), lax.axis_index("y"), lax.axis_index("c")
        me = _dev_id((x, y, c))
        for m, p in _relations(x, y, c):
            for a in range(n):
                cp = pltpu.make_async_remote_copy(
                    src_ref=src_views[a](ins[a], me, _dev_id(p), m), dst_ref=dst_views[a](lnd[a], me, m),
                    send_sem=send_sems.at[a * (N_DEV - 1) + m - 1], recv_sem=recv_sems.at[a * (N_DEV - 1) + m - 1],
                    device_id=p, device_id_type=MESH)
                cp.wait_send()
                cp.wait_recv()
        for a in range(n):
            frm, to = own_views[a](ins[a], got[a], me)
            _staged_copy(frm, to, bufs[a], local_sems)

    res = pl.pallas_call(
        body, name=name,
        out_shape=[pltpu.HBM(v.shape, v.dtype) for v in srcs] + [pltpu.HBM(v.shape, v.dtype) for v in lands],
        in_specs=[HBM] * (2 * n) + [SEM, SEM, ANY], out_specs=[HBM] * (2 * n),
        input_output_aliases={i: i for i in range(2 * n)},
        scratch_shapes=[pltpu.SemaphoreType.DMA((2,))]
        + [pltpu.VMEM(_stage_shape(s, v.dtype), v.dtype) for s, v in zip(own_shapes, srcs)],
        compiler_params=pltpu.CompilerParams(has_side_effects=EFFECT),
    )(*srcs, *lands, send_sems, recv_sems, after)
    return res[n:]


def kernel(x, norm_gains, fox_w_in, fox_b_f, hgrn_w_in, hgrn_lb_logits, hgrn_onorm, w_out, final_gain, loss_target, m_norm_gains, m_fox_w_in, m_fox_b_f, m_hgrn_w_in, m_hgrn_lb_logits, m_hgrn_onorm, m_w_out, m_final_gain, v_norm_gains, v_fox_w_in, v_fox_b_f, v_hgrn_w_in, v_hgrn_lb_logits, v_hgrn_onorm, v_w_out, v_final_gain):
    _, S, D = x.shape
    H = FOX_HEADS
    W = H * HEAD_DIM
    assert HGRN_HEADS == H and w_out.shape[2] == D
    cf = fox_w_in.shape[2]
    ch = hgrn_w_in.shape[2]
    ro = w_out.shape[1]
    co = hgrn_onorm.shape[1]
    assert N_DEV * cf == 4 * W + H and N_DEV * ch == 4 * W and N_DEV * ro == W and N_DEV * co == W
    x2 = x.reshape(S, D)
    tgt = loss_target.reshape(S, D)

    col = lambda n: (lambda r, i: r.at[:, pl.ds(pl.multiple_of(i * n, n), n)])
    row = lambda n: (lambda r, i: r.at[pl.ds(pl.multiple_of(i * n, n), n), :])

    (wf_g,) = _all_gather([fox_w_in[0].astype(BF16)], [SDS((N_DEV, D, cf), BF16)], [lambda r, p: r.at[p]],
                          name="gather_fox_w_in")

    late_views = [col(ch), row(ro), row(ro), col(co)]
    late = _send_start(
        [hgrn_w_in[0].astype(BF16), w_out[0].astype(BF16), w_out[1].astype(BF16), hgrn_onorm],
        [SDS((D, 4 * W), BF16), SDS((W, D), BF16), SDS((W, D), BF16), SDS((1, W), F32)],
        [lambda r, me, p, m: r] * 4, [lambda r, me, m, v=v: v(r, me) for v in late_views], wf_g[0, 0:8],
        name="gather_later_start")
    ng0 = norm_gains[0:1] + late[4][0:1, 0:1]
    wf = jnp.transpose(wf_g, (1, 0, 2)).reshape(D, N_DEV * cf)
    wf_main = jnp.concatenate([wf[:, :3 * W], wf[:, 3 * W + H:]], axis=1)
    wfl_t = wf[:, 3 * W:3 * W + H].T

    h0 = _rms_fwd(x2, ng0, name="rms0_fwd")
    p0 = _mm_nn([h0], wf_main, BF16, scale_cols=(W, LOG2E * HEAD_DIM ** -0.5), name="fox_in_proj")
    fl_t = _mm_nt_rows(wfl_t, h0, name="fox_forget_proj")
    b_col = fox_b_f.reshape(H, 1)
    kaug = _fox_key_aug(*_fox_gate_fwd(fl_t, b_col, name="fox_gate_fwd"))
    o0, y0, qaug = _fox_fwd(p0, kaug, H=H, name="fox_attn_fwd")
    wh, wo0, wo1, onorm = _send_wait(
        late, [lambda r, me, p, m: r] * 4, [lambda r, me, m, v=v: v(r, me) for v in late_views],
        [lambda src, land, me, v=v: (src, v(land, me)) for v in late_views],
        [(D, ch), (ro, D), (ro, D), (1, co)], y0[0:16], name="gather_later_wait")
    x1 = _mm_nn([y0], wo0, F32, residual=x2, name="fox_out_proj")

    lb = _lb_fwd(hgrn_lb_logits, name="hgrn_lower_bound")
    h1 = _rms_fwd(x1, norm_gains[1:2], name="rms1_fwd")
    p1 = _mm_nn([h1], wh, BF16, b_cols=[(0, W), (2 * W, 4 * W)], name="hgrn_in_proj")
    f1 = _mm_nn([h1], wh, F32, b_cols=[(W, 2 * W)], name="hgrn_forget_proj")
    o1, y1, states, a_t1 = _hgrn_fwd(p1, f1, lb, onorm, H=H, name="hgrn_fwd")
    xo = _mm_nn([y1], wo1, F32, residual=x1, name="hgrn_out_proj")

    dx2, dx2b, loss_part, dgf = _loss_head(xo, final_gain.reshape(1, D), tgt, name="loss_head")
    loss = lax.psum(jnp.sum(loss_part), ("x", "y", "c"))

    dy1 = _mm_nn([dx2b], wo1, BF16, b_t=True, name="hgrn_out_proj_dx")
    dwo1 = _mm_tn(y1, [dx2b], BF16, name="hgrn_out_proj_dw")
    do1, dg1, donorm = _hgrn_post_bwd(dy1, o1, p1, onorm, H=H, name="hgrn_post_bwd")
    dq1, df1, di1, dlb = _hgrn_bwd(p1, f1, lb, do1, states, a_t1, H=H, name="hgrn_bwd")
    segs1 = [dq1, df1, di1, dg1]
    dh1 = _mm_nn(segs1, wh, BF16, b_t=True, tn=D, name="hgrn_in_proj_dx")
    dwh = _mm_tn(h1, segs1, BF16, name="hgrn_in_proj_dw")
    part_views = [col(ch), row(ro)]
    slot = lambda r, me, m: r.at[m]
    ex1 = _send_start([dwh, dwo1], [SDS((N_DEV, D, ch), BF16), SDS((N_DEV, ro, D), BF16)],
                      [lambda r, me, p, m, v=v: v(r, p) for v in part_views], [slot] * 2, dwo1[0:8],
                      name="exchange_layer1_start")
    ng1 = norm_gains[1:2] + ex1[4][0:1, 0:1]
    dx1, dx1b, dng1 = _rms_bwd(x1, ng1, dh1, dx2, name="rms1_bwd")

    dy0 = _mm_nn([dx1b], wo0, BF16, b_t=True, name="fox_out_proj_dx")
    dwo0 = _mm_tn(y0, [dx1b], BF16, name="fox_out_proj_dw")
    do0, dg0, doaug = _fox_post_bwd(dy0, o0, p0, H=H, name="fox_post_bwd")
    dq0, dc_row, dk0, dv0, dc_key = _fox_bwd(p0, kaug, qaug, do0, doaug, H=H, name="fox_attn_bwd")
    dfl_t, dbf = _fox_gate_bwd(dc_row.reshape(H, S), dc_key.reshape(H, S), fl_t, b_col, name="fox_gate_bwd")
    dfl_tb = dfl_t.astype(BF16)
    dwfl_t = _mm_nn([dfl_tb], h0, BF16, name="fox_forget_proj_dw")
    segs0 = [dq0, dk0, dv0, dg0]
    dwf_main = _mm_tn(h0, segs0, BF16, name="fox_in_proj_dw")
    dwf = jnp.concatenate([dwf_main[:, :3 * W], dwfl_t.T, dwf_main[:, 3 * W:]], axis=1)
    dwf_blocks = jnp.transpose(dwf.reshape(D, N_DEV, cf), (1, 0, 2))
    ex0 = _send_start([dwf_blocks, dwo0], [SDS((N_DEV, D, cf), BF16), SDS((N_DEV, ro, D), BF16)],
                      [lambda r, me, p, m: r.at[p], lambda r, me, p, m: row(ro)(r, p)], [slot] * 2, dwo0[0:8],
                      name="exchange_layer0_start")
    wfl_t0 = wfl_t + ex0[4][0:1, 0:1].astype(BF16)
    dh0_f = _mm_nn([dfl_tb.T], wfl_t0, BF16, name="fox_forget_proj_dx")
    dh0 = _mm_nn(segs0, wf_main, BF16, residual=dh0_f, b_t=True, tn=D, name="fox_in_proj_dx")
    grad_x, _, dng0 = _rms_bwd(x2, norm_gains[0:1], dh0, dx1, name="rms0_bwd")

    own1 = [lambda src, land, me, v=v: (v(src, me), land.at[0]) for v in part_views]
    rh, ro1 = _send_wait(ex1, [lambda r, me, p, m, v=v: v(r, p) for v in part_views], [slot] * 2, own1,
                         [(D, ch), (ro, D)], dng0, name="exchange_layer1_wait")
    rf, ro0 = _send_wait(ex0, [lambda r, me, p, m: r.at[p], lambda r, me, p, m: row(ro)(r, p)], [slot] * 2,
                         [lambda src, land, me: (src.at[me], land.at[0]),
                          lambda src, land, me: (row(ro)(src, me), land.at[0])],
                         [(D, cf), (ro, D)], dng0, name="exchange_layer0_wait")

    pad = lambda a: jnp.pad(a, ((0, 0), (0, W - a.shape[1])))
    stats = jnp.concatenate([dng0, dng1, dlb, dgf, pad(dbf.reshape(1, H)), donorm,
                             jnp.zeros((2, W), F32)], axis=0)
    assert D == W
    (stats_all,) = _all_gather([stats], [SDS((N_DEV, STAT_ROWS, W), F32)], [lambda r, p: r.at[p]],
                               name="gather_small_grads")
    g_small = _stats_reduce(stats_all, hgrn_lb_logits, name="reduce_small_grads")
    me = 4 * lax.axis_index("x") + 2 * lax.axis_index("y") + lax.axis_index("c")
    g_onorm = lax.dynamic_slice_in_dim(g_small[6:7], me * co, co, axis=1)

    def upd(w, m, v, parts, name):
        shp = w.shape
        r2 = (-1, shp[-1])
        g, d, mn, vn = _adamw(w.reshape(r2), m.reshape(r2), v.reshape(r2), parts, name=name)
        return g.reshape(shp), d.reshape(shp), mn.reshape(shp), vn.reshape(shp)

    res = {
        "norm_gains": upd(norm_gains, m_norm_gains, v_norm_gains, [g_small[None, 0:2]], "adamw_norm_gains"),
        "fox_w_in": upd(fox_w_in, m_fox_w_in, v_fox_w_in, [rf], "adamw_fox_w_in"),
        "fox_b_f": upd(fox_b_f, m_fox_b_f, v_fox_b_f, [g_small[None, 5:6, :H]], "adamw_fox_b_f"),
        "hgrn_w_in": upd(hgrn_w_in, m_hgrn_w_in, v_hgrn_w_in, [rh], "adamw_hgrn_w_in"),
        "hgrn_lb_logits": upd(hgrn_lb_logits, m_hgrn_lb_logits, v_hgrn_lb_logits, [g_small[None, 2:4]],
                              "adamw_hgrn_lb_logits"),
        "hgrn_onorm": upd(hgrn_onorm, m_hgrn_onorm, v_hgrn_onorm, [g_onorm[None]], "adamw_hgrn_onorm"),
        "w_out": upd(w_out, m_w_out, v_w_out, [ro0, ro1], "adamw_w_out"),
        "final_gain": upd(final_gain.reshape(1, D), m_final_gain.reshape(1, D), v_final_gain.reshape(1, D),
                          [g_small[None, 4:5]], "adamw_final_gain"),
    }
    order = ["norm_gains", "fox_w_in", "fox_b_f", "hgrn_w_in", "hgrn_lb_logits", "hgrn_onorm", "w_out", "final_gain"]
    fix = lambda n, a: a.reshape(D) if n == "final_gain" else a
    outs = [loss, grad_x.reshape(1, S, D)]
    for k in range(4):
        outs += [fix(n, res[n][k]) for n in order]
    return tuple(outs)
```

```python
import functools

import numpy as np
import jax
import jax.numpy as jnp
from jax import lax
from jax.experimental import pallas as pl
from jax.experimental.pallas import tpu as pltpu

F32 = jnp.float32
BF16 = jnp.bfloat16
SDS = jax.ShapeDtypeStruct
MESH = pl.DeviceIdType.MESH

EPS = 1e-6
ADAM_LR, ADAM_B1, ADAM_B2, ADAM_EPS, ADAM_WD, ADAM_STEP = 0.001, 0.9, 0.999, 1e-08, 0.01, 10

N_DEV = 8
FOX_HEADS = 16
HGRN_HEADS = 16
HEAD_DIM = 128
HGRN_CHUNK = 128
HGRN_LEAF = 16
HGRN_HEADS_PER_STEP = 16
EXP_CLAMP = 85.0
ATT_BLOCK = 512
ATT_HEADS_PER_STEP = 4
ATT_BWD_HEADS_PER_STEP = 2
POST_HEADS_PER_STEP = 4
NEG = -1e30
LOG2E = 1.4426950408889634
LN2 = 0.6931471805599453

VMEM_LIMIT_V7X = 56 * 1024 * 1024


def _params(*sem):
    return pltpu.CompilerParams(dimension_semantics=sem, vmem_limit_bytes=VMEM_LIMIT_V7X)


def _silu(x):
    return x * jax.nn.sigmoid(x)


def _dsilu(x):
    s = jax.nn.sigmoid(x)
    return s * (1.0 + x * (1.0 - s))


def _dot(a, b):
    return jnp.dot(a, b, preferred_element_type=F32)


def _dot_nt(a, b):
    return lax.dot_general(a, b, (((1,), (1,)), ((), ())), preferred_element_type=F32)


def _dot_tn(a, b):
    return lax.dot_general(a, b, (((0,), (0,)), ((), ())), preferred_element_type=F32)


def _mm_nn(a_list, b, out_dtype, *, name, residual=None, scale_cols=None, b_t=False, b_cols=None,
           tm=1024, tn=1024, tk=2048):
    ns = len(a_list)
    M, Ks = a_list[0].shape
    K, N = (b.shape[1], b.shape[0]) if b_t else b.shape
    if b_cols is None:
        b_cols = [(0, N)]
    else:
        assert not b_t
        N = sum(e - s for s, e in b_cols)
    dot = _dot_nt if b_t else _dot
    assert K == ns * Ks and all(a.shape == (M, Ks) for a in a_list)
    if ns > 1:
        tk = tk // 2
    tm, tn, tk = min(tm, M), min(tn, N), min(tk, Ks)
    assert M % tm == 0 and N % tn == 0 and Ks % tk == 0
    assert scale_cols is None or scale_cols[0] % tn == 0
    assert all(s % tn == 0 and e % tn == 0 for s, e in b_cols)
    nks = Ks // tk
    nk = ns * nks
    has_res = residual is not None

    def body(*refs):
        a_refs, b_ref = refs[:ns], refs[ns]
        res_ref = refs[ns + 1] if has_res else None
        o_ref = refs[ns + 1 + has_res]

        def finish(r):
            if has_res:
                r = r + res_ref[...].astype(F32)
            if scale_cols is not None:
                r = r * jnp.where(pl.program_id(1) < scale_cols[0] // tn, scale_cols[1], 1.0)
            o_ref[...] = r.astype(out_dtype)

        if nk == 1:
            finish(dot(a_refs[0][...], b_ref[...]))
            return
        acc_ref = refs[ns + 2 + has_res]
        k = pl.program_id(2)

        @pl.when(k == 0)
        def _():
            acc_ref[...] = jnp.zeros_like(acc_ref)

        for s in range(ns):
            def step(s=s):
                acc_ref[...] += dot(a_refs[s][...], b_ref[...])

            if ns == 1:
                step()
            else:
                pl.when(k // nks == s)(step)

        @pl.when(k == nk - 1)
        def _():
            finish(acc_ref[...])

    def a_map(i, j, k, s):
        return (i, jnp.clip(k - s * nks, 0, nks - 1))

    in_specs = [pl.BlockSpec((tm, tk), functools.partial(a_map, s=s)) for s in range(ns)]
    def b_col(j):
        src = j + b_cols[0][0] // tn
        for (_, e0), (s1, _) in zip(b_cols[:-1], b_cols[1:]):
            src = src + jnp.where(src >= e0 // tn, (s1 - e0) // tn, 0)
        return src

    if b_t:
        in_specs.append(pl.BlockSpec((tn, tk), lambda i, j, k: (j, k)))
    else:
        in_specs.append(pl.BlockSpec((tk, tn), lambda i, j, k: (k, b_col(j))))
    args = list(a_list) + [b]
    if has_res:
        in_specs.append(pl.BlockSpec((tm, tn), lambda i, j, k: (i, j)))
        args.append(residual)
    return pl.pallas_call(
        body, grid=(M // tm, N // tn, nk), in_specs=in_specs,
        out_specs=pl.BlockSpec((tm, tn), lambda i, j, k: (i, j)),
        out_shape=SDS((M, N), out_dtype),
        scratch_shapes=[] if nk == 1 else [pltpu.VMEM((tm, tn), F32)],
        compiler_params=_params("parallel", "parallel", "arbitrary"), name=name,
    )(*args)


def _mm_tn(a, b_list, out_dtype, *, name, tm=2048, tn=1024, tk=512):
    ns = len(b_list)
    S, M = a.shape
    Ns = b_list[0].shape[1]
    assert all(b.shape == (S, Ns) for b in b_list)
    tm, tn, tk = min(tm, M), min(tn, Ns), min(tk, S)
    assert M % tm == 0 and Ns % tn == 0 and S % tk == 0
    njs = Ns // tn
    nk = S // tk

    def body(*refs):
        a_ref, b_refs, o_ref, acc_ref = refs[0], refs[1:1 + ns], refs[1 + ns], refs[2 + ns]
        j, k = pl.program_id(1), pl.program_id(2)

        @pl.when(k == 0)
        def _():
            acc_ref[...] = jnp.zeros_like(acc_ref)

        for s in range(ns):
            def step(s=s):
                acc_ref[...] += _dot_tn(a_ref[...], b_refs[s][...])

            if ns == 1:
                step()
            else:
                pl.when(j // njs == s)(step)

        @pl.when(k == nk - 1)
        def _():
            o_ref[...] = acc_ref[...].astype(out_dtype)

    def b_map(i, j, k, s):
        return (k, jnp.clip(j - s * njs, 0, njs - 1))

    in_specs = [pl.BlockSpec((tk, tm), lambda i, j, k: (k, i))]
    in_specs += [pl.BlockSpec((tk, tn), functools.partial(b_map, s=s)) for s in range(ns)]
    return pl.pallas_call(
        body, grid=(M // tm, ns * njs, nk), in_specs=in_specs,
        out_specs=pl.BlockSpec((tm, tn), lambda i, j, k: (i, j)),
        out_shape=SDS((M, ns * Ns), out_dtype),
        scratch_shapes=[pltpu.VMEM((tm, tn), F32)],
        compiler_params=_params("parallel", "parallel", "arbitrary"), name=name,
    )(a, *b_list)


def _mm_nt_rows(w_t, h, *, name, tn=1024):
    R, K = w_t.shape
    S = h.shape[0]
    tn = min(tn, S)

    def body(w_ref, h_ref, o_ref):
        o_ref[...] = _dot_nt(w_ref[...], h_ref[...])

    return pl.pallas_call(
        body, grid=(S // tn,),
        in_specs=[pl.BlockSpec((R, K), lambda i: (0, 0)), pl.BlockSpec((tn, K), lambda i: (i, 0))],
        out_specs=pl.BlockSpec((R, tn), lambda i: (0, i)),
        out_shape=SDS((R, S), F32), compiler_params=_params("parallel"), name=name,
    )(w_t, h)


def _rms_fwd(x, gain, *, name, tm=512):
    S, D = x.shape
    tm = min(tm, S)

    def body(x_ref, g_ref, h_ref):
        xv = x_ref[...]
        r = lax.rsqrt(jnp.mean(xv * xv, axis=-1, keepdims=True) + EPS)
        h_ref[...] = ((xv * r) * g_ref[...]).astype(BF16)

    return pl.pallas_call(
        body, grid=(S // tm,),
        in_specs=[pl.BlockSpec((tm, D), lambda i: (i, 0)), pl.BlockSpec((1, D), lambda i: (0, 0))],
        out_specs=pl.BlockSpec((tm, D), lambda i: (i, 0)),
        out_shape=SDS((S, D), BF16), compiler_params=_params("parallel"), name=name,
    )(x, gain)


def _rms_bwd(x, gain, dh, dres, out_dtype, *, name, tm=512):
    S, D = x.shape
    tm = min(tm, S)

    def body(x_ref, g_ref, dh_ref, dres_ref, dx_ref, dg_ref):
        @pl.when(pl.program_id(0) == 0)
        def _():
            dg_ref[...] = jnp.zeros_like(dg_ref)

        xv = x_ref[...]
        r = lax.rsqrt(jnp.mean(xv * xv, axis=-1, keepdims=True) + EPS)
        xh = xv * r
        dhv = dh_ref[...].astype(F32)
        dg_ref[...] += jnp.sum(dhv * xh, axis=0, keepdims=True)
        dxh = dhv * g_ref[...]
        dx = r * (dxh - xh * jnp.mean(dxh * xh, axis=-1, keepdims=True)) + dres_ref[...].astype(F32)
        dx_ref[...] = dx.astype(out_dtype)

    row = pl.BlockSpec((tm, D), lambda i: (i, 0))
    vec = pl.BlockSpec((1, D), lambda i: (0, 0))
    return pl.pallas_call(
        body, grid=(S // tm,), in_specs=[row, vec, row, row], out_specs=[row, vec],
        out_shape=[SDS((S, D), out_dtype), SDS((1, D), F32)],
        compiler_params=_params("arbitrary"), name=name,
    )(x, gain, dh, dres)


def _loss_head(x, gain, target, *, name, tm=512):
    S, D = x.shape
    tm = min(tm, S)
    assert tm % 8 == 0 and D % 128 == 0

    def body(x_ref, g_ref, t_ref, dxb_ref, loss_ref, dg_ref):
        @pl.when(pl.program_id(0) == 0)
        def _():
            dg_ref[...] = jnp.zeros_like(dg_ref)
            loss_ref[...] = jnp.zeros_like(loss_ref)

        xv = x_ref[...]
        g = g_ref[...]
        r = lax.rsqrt(jnp.mean(xv * xv, axis=-1, keepdims=True) + EPS)
        xh = xv * r
        err = xh * g - t_ref[...]
        e2 = (err * err).reshape(tm // 8, 8, D).sum(axis=0)
        part = e2[:, 0:128]
        for k in range(1, D // 128):
            part = part + e2[:, k * 128:(k + 1) * 128]
        loss_ref[...] += part * (0.5 / D)
        dy = err * (1.0 / D)
        dg_ref[...] += jnp.sum(dy * xh, axis=0, keepdims=True)
        dxh = dy * g
        dx = r * (dxh - xh * jnp.mean(dxh * xh, axis=-1, keepdims=True))
        dxb_ref[...] = dx.astype(BF16)

    row = pl.BlockSpec((tm, D), lambda i: (i, 0))
    vec = pl.BlockSpec((1, D), lambda i: (0, 0))
    return pl.pallas_call(
        body, grid=(S // tm,), in_specs=[row, vec, row],
        out_specs=[row, pl.BlockSpec((8, 128), lambda i: (0, 0)), vec],
        out_shape=[SDS((S, D), BF16), SDS((8, 128), F32), SDS((1, D), F32)],
        compiler_params=_params("arbitrary"), name=name,
    )(x, gain, target)


def _split3(x):
    hi = x.astype(BF16)
    r1 = x - hi.astype(F32)
    mid = r1.astype(BF16)
    lo = (r1 - mid.astype(F32)).astype(BF16)
    return hi, mid, lo


def _split2(x):
    hi = x.astype(BF16)
    lo = (x - hi.astype(F32)).astype(BF16)
    return hi, lo


def _fox_gate_fwd(fl_t, b_col, *, name):
    H, S = fl_t.shape
    L = 128
    tri = jnp.asarray(np.triu(np.ones((L, L), np.float32)), BF16)

    def body(fl_ref, b_ref, tri_ref, hi_ref, mid_ref, lo_ref, carry):
        @pl.when(pl.program_id(0) == 0)
        def _():
            carry[...] = jnp.zeros_like(carry)

        z = fl_ref[...] + b_ref[...]
        lf = jnp.minimum(z, 0.0) - jnp.log(1.0 + jnp.exp(-jnp.abs(z)))
        hi, mid, lo = _split3(lf)
        t = tri_ref[...]
        c = (_dot(hi, t) + _dot(mid, t)) + _dot(lo, t) + carry[...]
        carry[...] = c[:, L - 1:L]
        hi_ref[...], mid_ref[...], lo_ref[...] = _split3(c * (-LOG2E))

    blk = pl.BlockSpec((H, L), lambda i: (0, i))
    return pl.pallas_call(
        body, grid=(S // L,),
        in_specs=[blk, pl.BlockSpec((H, 1), lambda i: (0, 0)), pl.BlockSpec((L, L), lambda i: (0, 0))],
        out_specs=[blk] * 3, out_shape=[SDS((H, S), BF16)] * 3, scratch_shapes=[pltpu.VMEM((H, 1), F32)],
        compiler_params=_params("arbitrary"), name=name,
    )(fl_t, b_col, tri)


def _fox_gate_bwd(dc_row, dc_key, fl_t, b_col, *, name):
    H, S = fl_t.shape
    L = 128
    n = S // L
    tri = jnp.asarray(np.tril(np.ones((L, L), np.float32)), BF16)

    def body(dcr_ref, dck_ref, fl_ref, b_ref, tri_ref, dfl_ref, db_ref, carry):
        @pl.when(pl.program_id(0) == 0)
        def _():
            carry[...] = jnp.zeros_like(carry)
            db_ref[...] = jnp.zeros_like(db_ref)

        hi, mid, lo = _split3(dcr_ref[...] + dck_ref[...])
        t = tri_ref[...]
        dlf = (_dot(hi, t) + _dot(mid, t)) + _dot(lo, t) + carry[...]
        carry[...] = dlf[:, 0:1]
        z = fl_ref[...] + b_ref[...]
        dfl = dlf * jax.nn.sigmoid(-z)
        dfl_ref[...] = dfl
        db_ref[...] += jnp.sum(dfl, axis=1, keepdims=True)

    blk = pl.BlockSpec((H, L), lambda i: (0, n - 1 - i))
    col = pl.BlockSpec((H, 1), lambda i: (0, 0))
    return pl.pallas_call(
        body, grid=(n,), in_specs=[blk, blk, blk, col, pl.BlockSpec((L, L), lambda i: (0, 0))],
        out_specs=[blk, col], out_shape=[SDS((H, S), F32), SDS((H, 1), F32)],
        scratch_shapes=[pltpu.VMEM((H, 1), F32)], compiler_params=_params("arbitrary"), name=name,
    )(dc_row, dc_key, fl_t, b_col, tri)


AUG = HEAD_DIM


def _lane_select(cols, shape):
    lane = lax.broadcasted_iota(jnp.int32, shape, 1)
    out = jnp.zeros(shape, BF16)
    for k, c in reversed(list(enumerate(cols))):
        c = jnp.full(shape, c, BF16) if isinstance(c, (int, float)) else jnp.broadcast_to(c, shape).astype(BF16)
        out = jnp.where(lane == k, c, out)
    return out


def _fox_key_aug(b_hi, b_mid, b_lo):
    H, S = b_hi.shape
    ones = jnp.ones((H, S), BF16)
    ka = jnp.stack([b_hi, b_mid, b_lo, ones, ones, ones], axis=-1)
    ka = jnp.pad(ka, ((0, 0), (0, 0), (0, AUG - 6)))
    return jnp.transpose(ka, (1, 0, 2)).reshape(S, H * AUG)


def _fox_fwd(p0, kaug, *, H, name):
    S = p0.shape[0]
    T = min(ATT_BLOCK, S)
    nq = S // T
    dh = HEAD_DIM
    G = ATT_HEADS_PER_STEP
    assert H % G == 0

    def body(q_ref, k_ref, ka_ref, v_ref, g_ref, o_ref, y_ref, qa_ref, m_sc, acc_sc, p_sc, al_sc):
        i = pl.program_id(1)
        qaug = _lane_select([1.0, 1.0, 1.0], (T, AUG))
        ones = jnp.ones((T, dh), BF16)
        m_sc[...] = jnp.full_like(m_sc, NEG)
        acc_sc[...] = jnp.zeros_like(acc_sc)

        def step(j, before, masked):
            rows = pl.ds(pl.multiple_of(j * T, T), T)
            for g in range(G):
                hd = slice(g * dh, (g + 1) * dh)
                if not masked:
                    prev = pl.ds(pl.multiple_of(before * T, T), T)
                    vp = jnp.concatenate([v_ref[prev, hd], ones], axis=1)
                    acc_sc[g] = jnp.tile(al_sc[g], (1, 2)) * acc_sc[g] + _dot(p_sc[g], vp)
                q = jnp.concatenate([q_ref[:, hd], qaug], axis=1)
                kj = jnp.concatenate([k_ref[rows, hd], ka_ref[rows, hd]], axis=1)
                t = _dot_nt(q, kj)
                if masked:
                    row = lax.broadcasted_iota(jnp.int32, (T, T), 0)
                    col = lax.broadcasted_iota(jnp.int32, (T, T), 1)
                    t = jnp.where(row >= col, t, NEG)
                m_prev = m_sc[g]
                m_new = jnp.maximum(m_prev, jnp.max(t, axis=-1, keepdims=True))
                p_sc[g] = jnp.exp2(t - jnp.tile(m_new, (1, T // 128))).astype(BF16)
                al_sc[g] = jnp.exp2(m_prev - m_new)
                m_sc[g] = m_new

        step(i, None, True)

        def loop_body(j, carry):
            step(j, jnp.where(j == 0, i, j - 1), False)
            return carry

        lax.fori_loop(0, i, loop_body, 0)
        rows = pl.ds(pl.multiple_of(jnp.where(i == 0, 0, i - 1) * T, T), T)
        for g in range(G):
            hd = slice(g * dh, (g + 1) * dh)
            vp = jnp.concatenate([v_ref[rows, hd], ones], axis=1)
            acc = jnp.tile(al_sc[g], (1, 2)) * acc_sc[g] + _dot(p_sc[g], vp)
            l = acc[:, dh:]
            o = acc[:, :dh] / l
            o_ref[:, hd] = o
            y_ref[:, hd] = (o * _silu(g_ref[:, hd].astype(F32))).astype(BF16)
            hi, mid, lo = _split3(-(m_sc[g] + jnp.log2(l)))
            qa_ref[:, hd] = _lane_select([1.0, 1.0, 1.0, hi, mid, lo], (T, AUG))

    blk = lambda off: pl.BlockSpec((T, G * dh), lambda h, i: (i, off // G + h))
    full = lambda off: pl.BlockSpec((S, G * dh), lambda h, i: (0, off // G + h), pipeline_mode=pl.Buffered(1))
    return pl.pallas_call(
        body, grid=(H // G, nq),
        in_specs=[blk(0), full(H), full(0), full(2 * H), blk(3 * H)],
        out_specs=[blk(0), blk(0), blk(0)],
        out_shape=[SDS((S, H * dh), F32), SDS((S, H * dh), BF16), SDS((S, H * AUG), BF16)],
        scratch_shapes=[pltpu.VMEM((G, T, 128), F32), pltpu.VMEM((G, T, 2 * dh), F32),
                        pltpu.VMEM((G, T, T), BF16), pltpu.VMEM((G, T, 128), F32)],
        compiler_params=_params("parallel", "arbitrary"), name=name,
    )(p0, p0, kaug, p0, p0)


def _fox_post_bwd(dy, o, p0, *, H, name, tm=512):
    S = dy.shape[0]
    dh = HEAD_DIM
    tm = min(tm, S)
    G = POST_HEADS_PER_STEP
    assert H % G == 0

    def body(dy_ref, o_ref, g_ref, do_ref, dg_ref, da_ref):
        dyv = dy_ref[...].astype(F32)
        ov = o_ref[...]
        g = g_ref[...].astype(F32)
        do = (dyv * _silu(g)).astype(BF16)
        do_ref[...] = do
        dg_ref[...] = (dyv * ov * _dsilu(g)).astype(BF16)
        prod = do.astype(F32) * ov
        for k in range(G):
            hd = slice(k * dh, (k + 1) * dh)
            delta = jnp.sum(prod[:, hd], axis=-1, keepdims=True)
            hi, mid, lo = _split3(-jnp.broadcast_to(delta, (tm, AUG)))
            da_ref[:, hd] = _lane_select([hi, mid, lo], (tm, AUG))

    blk = pl.BlockSpec((tm, G * dh), lambda h, i: (i, h))
    return pl.pallas_call(
        body, grid=(H // G, S // tm),
        in_specs=[blk, blk, pl.BlockSpec((tm, G * dh), lambda h, i: (i, 3 * H // G + h))],
        out_specs=[blk, blk, blk],
        out_shape=[SDS((S, H * dh), BF16), SDS((S, H * dh), BF16), SDS((S, H * AUG), BF16)],
        compiler_params=_params("parallel", "parallel"), name=name,
    )(dy, o, p0)


def _fox_bwd(p0, kaug, qaug, do, doaug, *, H, name):
    S = p0.shape[0]
    T = min(ATT_BLOCK, S)
    nq = S // T
    dh = HEAD_DIM
    scale = dh ** -0.5
    G = ATT_BWD_HEADS_PER_STEP
    assert H % G == 0

    def body(q_ref, qa_ref, k_ref, ka_ref, v_ref, do_ref, da_ref, dq_ref, rs_ref, dk_ref, dv_ref, dc_ref,
             dq_sc, dk_sc, dv_sc, pt_sc, dst_sc):
        j = pl.program_id(1)
        vaug = _lane_select([1.0, 1.0, 1.0], (T, AUG))
        ones = jnp.ones((T, dh), BF16)

        @pl.when(j == 0)
        def _():
            dq_sc[...] = jnp.zeros_like(dq_sc)

        dk_sc[...] = jnp.zeros_like(dk_sc)
        dv_sc[...] = jnp.zeros_like(dv_sc)

        def apply(prev):
            for g in range(G):
                hd = slice(g * dh, (g + 1) * dh)
                dv_sc[g] += _dot(pt_sc[g], do_ref[prev, hd])
                dk_sc[g] += _dot(dst_sc[g], jnp.concatenate([q_ref[prev, hd], ones], axis=1))
                dq_sc[g, prev] += _dot_tn(dst_sc[g], jnp.concatenate([k_ref[:, hd], ones], axis=1))

        def step(i, masked):
            rows = pl.ds(pl.multiple_of(i * T, T), T)
            if not masked:
                apply(pl.ds(pl.multiple_of((i - 1) * T, T), T))
            for g in range(G):
                hd = slice(g * dh, (g + 1) * dh)
                k = jnp.concatenate([k_ref[:, hd], ka_ref[:, hd]], axis=1)
                v = jnp.concatenate([v_ref[:, hd], vaug], axis=1)
                pt = jnp.exp2(_dot_nt(k, jnp.concatenate([q_ref[rows, hd], qa_ref[rows, hd]], axis=1)))
                if masked:
                    row = lax.broadcasted_iota(jnp.int32, (T, T), 0)
                    col = lax.broadcasted_iota(jnp.int32, (T, T), 1)
                    pt = jnp.where(col >= row, pt, 0.0)
                dst = pt * _dot_nt(v, jnp.concatenate([do_ref[rows, hd], da_ref[rows, hd]], axis=1))
                pt_sc[g] = pt.astype(BF16)
                dst_sc[g] = dst.astype(BF16)

        step(j, True)

        def loop_body(i, carry):
            step(i, False)
            return carry

        lax.fori_loop(j + 1, nq, loop_body, 0)
        apply(pl.ds((nq - 1) * T, T))
        for g in range(G):
            hd = slice(g * dh, (g + 1) * dh)
            dk_ref[:, hd] = (dk_sc[g, :, :dh] * LN2).astype(BF16)
            dv_ref[:, hd] = dv_sc[g].astype(BF16)
            dc_ref[g] = -jnp.transpose(dk_sc[g, :, dh:])[0:1]

        @pl.when(j == nq - 1)
        def _():
            for g in range(G):
                dq_ref[:, g * dh:(g + 1) * dh] = (dq_sc[g, :, :dh] * scale).astype(BF16)
                for i in range(nq):
                    rs_ref[g, :, i * T:(i + 1) * T] = jnp.transpose(dq_sc[g, i * T:(i + 1) * T, dh:])[0:1]

    blk = lambda off: pl.BlockSpec((T, G * dh), lambda h, j: (j, off // G + h))
    full = lambda off: pl.BlockSpec((S, G * dh), lambda h, j: (0, off // G + h))
    once = lambda off: pl.BlockSpec((S, G * dh), lambda h, j: (0, off // G + h), pipeline_mode=pl.Buffered(1))
    rowv = pl.BlockSpec((G, 1, T), lambda h, j: (h, 0, j))
    return pl.pallas_call(
        body, grid=(H // G, nq),
        in_specs=[once(0), once(0), blk(H), blk(0), blk(2 * H), once(0), once(0)],
        out_specs=[full(0), pl.BlockSpec((G, 1, S), lambda h, j: (h, 0, 0)), blk(0), blk(0), rowv],
        out_shape=[SDS((S, H * dh), BF16), SDS((H, 1, S), F32), SDS((S, H * dh), BF16), SDS((S, H * dh), BF16),
                   SDS((H, 1, S), F32)],
        scratch_shapes=[pltpu.VMEM((G, S, 2 * dh), F32), pltpu.VMEM((G, T, 2 * dh), F32), pltpu.VMEM((G, T, dh), F32),
                        pltpu.VMEM((G, T, T), BF16), pltpu.VMEM((G, T, T), BF16)],
        compiler_params=_params("parallel", "arbitrary"), name=name,
    )(p0, qaug, p0, kaug, p0, do, doaug)


def _hgrn_levels(C, leaf):
    levels = []
    h = C // 2
    while h >= leaf:
        levels.append(h)
        h //= 2
    return levels


def _hgrn_sum_matrix(C, leaf):
    t = np.arange(C)[:, None]
    u = np.arange(C)[None, :]
    mats = [(u <= t), (u > t)]
    for h in _hgrn_levels(C, leaf):
        start = (t // (2 * h)) * (2 * h)
        mid = start + h - 1
        second = t > mid
        m = np.where(second, (u > mid) & (u <= t), (u > t) & (u <= mid))
        mats.append(m)
    lstart = (t // leaf) * leaf
    mats.append((u >= lstart) & (u <= t))
    return np.concatenate([m.astype(np.float32) for m in mats], axis=0)


def _hgrn_chunk_terms(qr, fz, lb, msum, C, leaf):
    levels = _hgrn_levels(C, leaf)
    sq = _silu(qr)
    sp = 1.0 / (1.0 + jnp.exp(-fz))
    sn = 1.0 / (1.0 + jnp.exp(fz))
    f = lb + (1.0 - lb) * sp
    lf = jnp.log(f)
    k = (1.0 - lb) * sn
    hi, lo = _split2(lf)
    dsum = _dot(msum, hi) + _dot(msum, lo)
    b = dsum[0:C]
    kdec = dsum[C:2 * C]
    rowi = lax.broadcasted_iota(jnp.int32, (C, 1), 0)
    lev = []
    for n, h in enumerate(levels):
        e = jnp.exp(dsum[(2 + n) * C:(3 + n) * C])
        selq = jnp.where((rowi % (2 * h)) >= h, 1.0, 0.0)
        qm = (sq * e * selq).astype(BF16)
        km = (k * e * (1.0 - selq)).astype(BF16)
        lev.append((h, e, selq, qm, km))
    dleaf = dsum[(2 + len(levels)) * C:(3 + len(levels)) * C]
    eq = jnp.exp(dleaf)
    ek = jnp.exp(jnp.minimum(-dleaf, EXP_CLAMP))
    return dict(sq=sq, sp=sp, sn=sn, f=f, k=k, b=b, kdec=kdec, lev=lev, eq=eq, ek=ek,
                ql=(sq * eq).astype(BF16), kl=(k * ek).astype(BF16),
                qs=(sq * jnp.exp(b)).astype(BF16), ke=(k * jnp.exp(kdec)).astype(BF16),
                e_c=jnp.exp(b[C - 1:C, :]))


def _hgrn_masks(C, leaf, transposed):
    a = lax.broadcasted_iota(jnp.int32, (C, C), 0)
    bb = lax.broadcasted_iota(jnp.int32, (C, C), 1)
    t, s = (bb, a) if transposed else (a, bb)
    lev = [None if 2 * h == C else (t // (2 * h)) == (s // (2 * h)) for h in _hgrn_levels(C, leaf)]
    if leaf == C:
        leafm = s <= t
    else:
        leafm = ((t // leaf) == (s // leaf)) & (s <= t)
    return lev, leafm


def _hgrn_fwd(p1, f1, lb, onorm, *, H, name, tb=512):
    S = p1.shape[0]
    dk = HEAD_DIM
    C = min(HGRN_CHUNK, S)
    leaf = min(HGRN_LEAF, C)
    tb = min(tb, S)
    nc = tb // C
    G = HGRN_HEADS_PER_STEP
    assert H % G == 0
    msum = jnp.asarray(_hgrn_sum_matrix(C, leaf), BF16)

    def body(q_ref, f_ref, v_ref, g_ref, lb_ref, on_ref, ms_ref, o_ref, y_ref, st_ref, at_ref, st_sc):
        @pl.when(pl.program_id(1) == 0)
        def _():
            st_sc[...] = jnp.zeros_like(st_sc)

        msv = ms_ref[...]
        lmask, leafm = _hgrn_masks(C, leaf, False)

        def chunk(n, carry):
            rows = pl.ds(pl.multiple_of(n * C, C), C)
            for g in range(G):
                hd = slice(g * dk, (g + 1) * dk)
                tm = _hgrn_chunk_terms(q_ref[rows, hd].astype(F32), f_ref[rows, hd], lb_ref[:, hd], msv, C, leaf)
                v = v_ref[rows, hd]
                st = st_sc[g]
                st_ref[g, n] = st
                a = jnp.where(leafm, _dot_nt(tm["ql"], tm["kl"]), 0.0)
                for (h, e, selq, qm, km), m in zip(tm["lev"], lmask):
                    al = _dot_nt(qm, km)
                    a = a + (al if m is None else jnp.where(m, al, 0.0))
                at_ref[g, n] = jnp.transpose(a).astype(BF16)
                o = _dot_nt(tm["qs"], st.astype(BF16)) + _dot(a.astype(BF16), v)
                st_sc[g] = st * tm["e_c"] + _dot(v.T, tm["ke"])
                o_ref[rows, hd] = o
                rn = lax.rsqrt(jnp.mean(o * o, axis=-1, keepdims=True) + EPS)
                y = ((o * rn) * on_ref[:, hd]) * _silu(g_ref[rows, hd].astype(F32))
                y_ref[rows, hd] = y.astype(BF16)
            return carry

        lax.fori_loop(0, nc, chunk, 0)

    blk = lambda off: pl.BlockSpec((tb, G * dk), lambda h, i: (i, off // G + h))
    vec = pl.BlockSpec((1, G * dk), lambda h, i: (0, h))
    return pl.pallas_call(
        body, grid=(H // G, S // tb),
        in_specs=[blk(0), blk(0), blk(H), blk(2 * H), vec, vec,
                  pl.BlockSpec(msum.shape, lambda h, i: (0, 0))],
        out_specs=[blk(0), blk(0), pl.BlockSpec((G, nc, dk, dk), lambda h, i: (h, i, 0, 0)),
                   pl.BlockSpec((G, nc, C, C), lambda h, i: (h, i, 0, 0))],
        out_shape=[SDS((S, H * dk), F32), SDS((S, H * dk), BF16), SDS((H, S // C, dk, dk), F32),
                   SDS((H, S // C, C, C), BF16)],
        scratch_shapes=[pltpu.VMEM((G, dk, dk), F32)],
        compiler_params=_params("parallel", "arbitrary"), name=name,
    )(p1, f1, p1, p1, lb, onorm, msum)


def _hgrn_post_bwd(dy, o, p1, onorm, *, H, name, tm=512):
    S = dy.shape[0]
    dk = HEAD_DIM
    tm = min(tm, S)
    G = POST_HEADS_PER_STEP
    assert H % G == 0

    def body(dy_ref, o_ref, g_ref, on_ref, do_ref, dg_ref, don_ref):
        @pl.when(pl.program_id(1) == 0)
        def _():
            don_ref[...] = jnp.zeros_like(don_ref)

        for k in range(G):
            hd = slice(k * dk, (k + 1) * dk)
            dyv = dy_ref[:, hd].astype(F32)
            ov = o_ref[:, hd]
            g = g_ref[:, hd].astype(F32)
            onv = on_ref[:, hd]
            rn = lax.rsqrt(jnp.mean(ov * ov, axis=-1, keepdims=True) + EPS)
            oh = ov * rn
            dn = dyv * _silu(g)
            dg_ref[:, hd] = (dyv * (oh * onv) * _dsilu(g)).astype(BF16)
            don_ref[:, hd] += jnp.sum(dn * oh, axis=0, keepdims=True)
            doh = dn * onv
            do_ref[:, hd] = (rn * (doh - oh * jnp.mean(doh * oh, axis=-1, keepdims=True))).astype(BF16)

    blk = pl.BlockSpec((tm, G * dk), lambda h, i: (i, h))
    vec = pl.BlockSpec((1, G * dk), lambda h, i: (0, h))
    return pl.pallas_call(
        body, grid=(H // G, S // tm),
        in_specs=[blk, blk, pl.BlockSpec((tm, G * dk), lambda h, i: (i, 2 * H // G + h)), vec],
        out_specs=[blk, blk, vec],
        out_shape=[SDS((S, H * dk), BF16), SDS((S, H * dk), BF16), SDS((1, H * dk), F32)],
        compiler_params=_params("parallel", "arbitrary"), name=name,
    )(dy, o, p1, onorm)


def _hgrn_bwd(p1, f1, lb, do, states, a_t, *, H, name, tb=512):
    S = p1.shape[0]
    dk = HEAD_DIM
    C = min(HGRN_CHUNK, S)
    leaf = min(HGRN_LEAF, C)
    tb = min(tb, S)
    nc = tb // C
    nb = S // tb
    G = HGRN_HEADS_PER_STEP
    assert H % G == 0
    msum = jnp.asarray(_hgrn_sum_matrix(C, leaf), BF16)
    rtri = jnp.asarray(np.triu(np.ones((C, C), np.float32)), BF16)

    def body(q_ref, f_ref, v_ref, do_ref, st_ref, at_ref, lb_ref, ms_ref, rt_ref,
             dq_ref, df_ref, dv_ref, dlb_ref, g_sc):
        @pl.when(pl.program_id(1) == 0)
        def _():
            g_sc[...] = jnp.zeros_like(g_sc)
            dlb_ref[...] = jnp.zeros_like(dlb_ref)

        msv = ms_ref[...]
        rtv = rt_ref[...]
        lmask, leafm = _hgrn_masks(C, leaf, False)
        lmask_t, leafm_t = _hgrn_masks(C, leaf, True)
        f32 = lambda z: z.astype(F32)

        def head_chunk(g, n):
            hd = slice(g * dk, (g + 1) * dk)
            rows = pl.ds(pl.multiple_of(n * C, C), C)
            lbv = lb_ref[:, hd]
            qr = q_ref[rows, hd].astype(F32)
            tm = _hgrn_chunk_terms(qr, f_ref[rows, hd], lbv, msv, C, leaf)
            v = v_ref[rows, hd]
            dov = do_ref[rows, hd]
            st0 = st_ref[g, n]
            gt = g_sc[g]
            gtb = gt.astype(BF16)
            da = _dot_nt(dov, v)
            da_t = _dot_nt(v, dov)

            dal = jnp.where(leafm, da, 0.0).astype(BF16)
            dal_t = jnp.where(leafm_t, da_t, 0.0).astype(BF16)
            dql = _dot(dal, tm["kl"])
            dkl = _dot(dal_t, tm["ql"])
            dsq = dql * tm["eq"]
            dkk = dkl * tm["ek"]
            xq = f32(tm["ql"]) * dql
            xk = f32(tm["kl"]) * dkl
            for (h, e, selq, qm, km), m, m_t in zip(tm["lev"], lmask, lmask_t):
                dl = (da if m is None else jnp.where(m, da, 0.0)).astype(BF16)
                dl_t = (da_t if m_t is None else jnp.where(m_t, da_t, 0.0)).astype(BF16)
                dqm = _dot(dl, km)
                dkm = _dot(dl_t, qm)
                dsq = dsq + dqm * (e * selq)
                dkk = dkk + dkm * (e * (1.0 - selq))
                xq = xq + f32(qm) * dqm
                xk = xk + f32(km) * dkm
            dqs = _dot(dov, st0.astype(BF16))
            dke = _dot(v, gtb)
            dsq = dsq + dqs * jnp.exp(tm["b"])
            dkk = dkk + dke * jnp.exp(tm["kdec"])
            xq = xq + f32(tm["qs"]) * dqs
            xk = xk + f32(tm["ke"]) * dke
            dvv = _dot(at_ref[g, n], dov) + _dot_nt(tm["ke"], gtb)
            r_end = jnp.sum(f32(gtb) * _dot(v.T, tm["ke"]) + gt * (st0 * tm["e_c"]), axis=0, keepdims=True)
            g_sc[g] = gt * tm["e_c"] + _dot(dov.T, tm["qs"])
            xh, xm, xl = _split3(xq - xk)
            dlf = (_dot(rtv, xh) + _dot(rtv, xm)) + _dot(rtv, xl) + r_end
            dlf_f = dlf / tm["f"]
            dsp = (1.0 - lbv) * (dlf_f - dkk)
            df_ref[rows, hd] = (dsp * (tm["sp"] * tm["sn"])).astype(BF16)
            dq_ref[rows, hd] = (dsq * _dsilu(qr)).astype(BF16)
            dv_ref[rows, hd] = dvv.astype(BF16)
            dlb_ref[:, hd] += jnp.sum(dlf_f * tm["sn"] - dkk * tm["sn"], axis=0, keepdims=True)

        def chunk(nn, carry):
            for g in range(G):
                head_chunk(g, nc - 1 - nn)
            return carry

        lax.fori_loop(0, nc, chunk, 0)

    blk = lambda off: pl.BlockSpec((tb, G * dk), lambda h, i: (nb - 1 - i, off // G + h))
    vec = pl.BlockSpec((1, G * dk), lambda h, i: (0, h))
    return pl.pallas_call(
        body, grid=(H // G, nb),
        in_specs=[blk(0), blk(0), blk(H), blk(0),
                  pl.BlockSpec((G, nc, dk, dk), lambda h, i: (h, nb - 1 - i, 0, 0)),
                  pl.BlockSpec((G, nc, C, C), lambda h, i: (h, nb - 1 - i, 0, 0)), vec,
                  pl.BlockSpec(msum.shape, lambda h, i: (0, 0)), pl.BlockSpec((C, C), lambda h, i: (0, 0))],
        out_specs=[blk(0), blk(0), blk(0), vec],
        out_shape=[SDS((S, H * dk), BF16)] * 3 + [SDS((1, H * dk), F32)],
        scratch_shapes=[pltpu.VMEM((G, dk, dk), F32)],
        compiler_params=_params("parallel", "arbitrary"), name=name,
    )(p1, f1, p1, do, states, a_t, lb, msum, rtri)


def _lb_fwd(logits, *, name):
    W = logits.shape[1]

    def body(l_ref, lb_ref):
        l = l_ref[...]
        m = jnp.max(l, axis=0, keepdims=True)
        e = jnp.exp(l - m)
        p = e / jnp.sum(e, axis=0, keepdims=True)
        lb_ref[...] = (p[0:1] + p[1:2]) - p[0:1]

    return pl.pallas_call(body, out_shape=SDS((1, W), F32), name=name)(logits)


STAT_ROWS = 8


def _stats_reduce(stats_all, logits, *, name):
    W = logits.shape[1]

    def body(s_ref, l_ref, g_ref):
        tot = s_ref[0]
        for d in range(1, N_DEV):
            tot = tot + s_ref[d]
        l = l_ref[...]
        m = jnp.max(l, axis=0, keepdims=True)
        e = jnp.exp(l - m)
        p = e / jnp.sum(e, axis=0, keepdims=True)
        dlb = tot[2:3]
        dl0 = -(p[0:1] * p[1:2]) * dlb
        dl1 = (p[1:2] * (1.0 - p[1:2])) * dlb
        g_ref[0:2] = tot[0:2]
        g_ref[2:3] = dl0
        g_ref[3:4] = dl1
        g_ref[4:7] = tot[3:6]
        g_ref[7:8] = jnp.zeros((1, W), F32)

    return pl.pallas_call(body, out_shape=SDS((STAT_ROWS, W), F32), name=name)(stats_all, logits)


def _adamw(w, m, v, g_parts, *, name, tr=128):
    R, C = w.shape
    ns = len(g_parts)
    n, Rs = g_parts[0].shape[0], g_parts[0].shape[1]
    assert all(p.shape == (n, Rs, C) for p in g_parts) and ns * Rs == R
    tr = min(tr, Rs)
    assert Rs % tr == 0
    nts = Rs // tr
    c1 = 1.0 / (1.0 - ADAM_B1 ** ADAM_STEP)
    c2 = 1.0 / (1.0 - ADAM_B2 ** ADAM_STEP)

    def body(*refs):
        w_ref, m_ref, v_ref = refs[:3]
        g_refs = refs[3:3 + ns]
        go_ref, d_ref, mo_ref, vo_ref = refs[3 + ns:]

        def update(g_ref):
            g = g_ref[0].astype(F32)
            for k in range(1, n):
                g = g + g_ref[k].astype(F32)
            mn = ADAM_B1 * m_ref[...] + (1.0 - ADAM_B1) * g
            vn = ADAM_B2 * v_ref[...] + (1.0 - ADAM_B2) * (g * g)
            d_ref[...] = -ADAM_LR * ((mn * c1) / (jnp.sqrt(vn * c2) + ADAM_EPS) + ADAM_WD * w_ref[...])
            go_ref[...] = g
            mo_ref[...] = mn
            vo_ref[...] = vn

        for s in range(ns):
            if ns == 1:
                update(g_refs[s])
            else:
                pl.when(pl.program_id(0) // nts == s)(functools.partial(update, g_refs[s]))

    def g_map(i, s):
        return (0, jnp.clip(i - s * nts, 0, nts - 1), 0)

    blk = pl.BlockSpec((tr, C), lambda i: (i, 0))
    return pl.pallas_call(
        body, grid=(R // tr,),
        in_specs=[blk, blk, blk] + [pl.BlockSpec((n, tr, C), functools.partial(g_map, s=s)) for s in range(ns)],
        out_specs=[blk] * 4, out_shape=[SDS((R, C), F32)] * 4,
        compiler_params=_params("parallel"), name=name,
    )(w, m, v, *g_parts)


ANY = pl.BlockSpec(memory_space=pl.ANY)
STAGE_BYTES = 2 * 1024 * 1024


def _stage_shape(shape, dtype):
    row_bytes = int(np.prod(shape[1:])) * jnp.dtype(dtype).itemsize
    rows = max(1, min(shape[0], STAGE_BYTES // row_bytes))
    while shape[0] % rows:
        rows -= 1
    return (rows,) + tuple(shape[1:])


def _staged_copy(frm, to, buf, sems):
    rows = buf.shape[0]
    for r0 in range(0, frm.shape[0], rows):
        cp = pltpu.make_async_copy(frm.at[pl.ds(r0, rows)], buf, sems.at[0])
        cp.start()
        cp.wait()
        cp = pltpu.make_async_copy(buf, to.at[pl.ds(r0, rows)], sems.at[1])
        cp.start()
        cp.wait()


def _all_gather(shards, out_shapes, views, *, name):
    n = len(shards)

    def body(*refs):
        ins, outs = refs[:n], refs[n:2 * n]
        send_sems, recv_sems, local_sems = refs[2 * n:2 * n + 3]
        bufs = refs[2 * n + 3:]
        x, y, c = lax.axis_index("x"), lax.axis_index("y"), lax.axis_index("c")
        me, sibling = (x, y, c), (x, y, 1 - c)
        chips = [(1 - x, y), (x, 1 - y), (1 - x, 1 - y)]

        def dev(p):
            return 4 * p[0] + 2 * p[1] + p[2]

        def copy(a, k, block, to, src=None):
            dst = views[a](outs[a], dev(block))
            return pltpu.make_async_remote_copy(
                src_ref=dst if src is None else src, dst_ref=dst,
                send_sem=send_sems.at[a, k], recv_sem=recv_sems.at[a, k],
                device_id=to, device_id_type=MESH)

        first, passed = [], []
        for a in range(n):
            first.append(copy(a, 0, me, sibling, src=ins[a]))
            first += [copy(a, 1 + j, me, (*chip, c), src=ins[a]) for j, chip in enumerate(chips)]
        for cp in first:
            cp.start()
        for a in range(n):
            _staged_copy(ins[a], views[a](outs[a], dev(me)), bufs[a], local_sems)
        for j, chip in enumerate(chips):
            for a in range(n):
                copy(a, 1 + j, (*chip, c), me).wait_recv()
                cp = copy(a, 4 + j, (*chip, c), sibling)
                cp.start()
                passed.append(cp)
        for a in range(n):
            copy(a, 0, sibling, me).wait_recv()
            for j, chip in enumerate(chips):
                copy(a, 4 + j, (*chip, 1 - c), me).wait_recv()
        for cp in first + passed:
            cp.wait_send()

    return pl.pallas_call(
        body, in_specs=[ANY] * n, out_specs=[ANY] * n, out_shape=list(out_shapes),
        scratch_shapes=[pltpu.SemaphoreType.DMA((n, 7)), pltpu.SemaphoreType.DMA((n, 7)),
                        pltpu.SemaphoreType.DMA((2,))]
        + [pltpu.VMEM(_stage_shape(s.shape, s.dtype), s.dtype) for s in shards],
        name=name,
    )(*shards)


HBM = pl.BlockSpec(memory_space=pltpu.HBM)
SEM = pl.BlockSpec(memory_space=pltpu.SEMAPHORE)
EFFECT = pltpu.SideEffectType.DATAFLOW_SIDE_EFFECTING


def _relations(x, y, c):
    for m in range(1, N_DEV):
        yield m, (1 - x if m & 4 else x, 1 - y if m & 2 else y, 1 - c if m & 1 else c)


def _dev_id(p):
    return 4 * p[0] + 2 * p[1] + p[2]


def _send_start(srcs, land_shapes, src_views, dst_views, after, *, name):
    n = len(srcs)

    def body(*refs):
        ins, lands = refs[:n], refs[n:2 * n]
        send_sems, recv_sems, token = refs[2 * n + 1], refs[2 * n + 2], refs[-1]
        x, y, c = lax.axis_index("x"), lax.axis_index("y"), lax.axis_index("c")
        me = _dev_id((x, y, c))
        for m, p in _relations(x, y, c):
            for a in range(n):
                pltpu.make_async_remote_copy(
                    src_ref=src_views[a](ins[a], me, _dev_id(p), m), dst_ref=dst_views[a](lands[a], me, m),
                    send_sem=send_sems.at[a * (N_DEV - 1) + m - 1], recv_sem=recv_sems.at[a * (N_DEV - 1) + m - 1],
                    device_id=p, device_id_type=MESH).start()
        token[...] = jnp.zeros_like(token)

    lands = [pltpu.with_memory_space_constraint(lax.empty(s.shape, s.dtype), pltpu.HBM) for s in land_shapes]
    srcs = [pltpu.with_memory_space_constraint(v, pltpu.HBM) for v in srcs]
    res = pl.pallas_call(
        body, name=name,
        out_shape=[pltpu.SemaphoreType.DMA((n * (N_DEV - 1),)), pltpu.SemaphoreType.DMA((n * (N_DEV - 1),))]
        + [pltpu.HBM(v.shape, v.dtype) for v in srcs] + [pltpu.HBM(s.shape, s.dtype) for s in land_shapes]
        + [SDS((8, 128), F32)],
        in_specs=[HBM] * (2 * n) + [ANY],
        out_specs=[SEM, SEM] + [HBM] * (2 * n) + [pl.BlockSpec(memory_space=pltpu.VMEM)],
        input_output_aliases={i: 2 + i for i in range(2 * n)},
        compiler_params=pltpu.CompilerParams(has_side_effects=EFFECT),
    )(*srcs, *lands, after)
    return res[0], res[1], res[2:2 + n], res[2 + n:2 + 2 * n], res[-1]


def _send_wait(started, src_views, dst_views, own_views, own_shapes, after, *, name):
    send_sems, recv_sems, srcs, lands, _ = started
    n = len(srcs)

    def body(*refs):
        ins, lnd = refs[:n], refs[n:2 * n]
        send_sems, recv_sems = refs[2 * n], refs[2 * n + 1]
        got = refs[2 * n + 3 + n:2 * n + 3 + 2 * n]
        local_sems = refs[2 * n + 3 + 2 * n]
        bufs = refs[2 * n + 4 + 2 * n:]
        x, y, c = lax.axis_index("x"), lax.axis_index("y"), lax.axis_index("c")
        me = _dev_id((x, y, c))
        for m, p in _relations(x, y, c):
            for a in range(n):
                cp = pltpu.make_async_remote_copy(
                    src_ref=src_views[a](ins[a], me, _dev_id(p), m), dst_ref=dst_views[a](lnd[a], me, m),
                    send_sem=send_sems.at[a * (N_DEV - 1) + m - 1], recv_sem=recv_sems.at[a * (N_DEV - 1) + m - 1],
                    device_id=p, device_id_type=MESH)
                cp.wait_send()
                cp.wait_recv()
        for a in range(n):
            frm, to = own_views[a](ins[a], got[a], me)
            _staged_copy(frm, to, bufs[a], local_sems)

    res = pl.pallas_call(
        body, name=name,
        out_shape=[pltpu.HBM(v.shape, v.dtype) for v in srcs] + [pltpu.HBM(v.shape, v.dtype) for v in lands],
        in_specs=[HBM] * (2 * n) + [SEM, SEM, ANY], out_specs=[HBM] * (2 * n),
        input_output_aliases={i: i for i in range(2 * n)},
        scratch_shapes=[pltpu.SemaphoreType.DMA((2,))]
        + [pltpu.VMEM(_stage_shape(s, v.dtype), v.dtype) for s, v in zip(own_shapes, srcs)],
        compiler_params=pltpu.CompilerParams(has_side_effects=EFFECT),
    )(*srcs, *lands, send_sems, recv_sems, after)
    return res[n:]


def kernel(x, norm_gains, fox_w_in, fox_b_f, hgrn_w_in, hgrn_lb_logits, hgrn_onorm, w_out, final_gain, loss_target, m_norm_gains, m_fox_w_in, m_fox_b_f, m_hgrn_w_in, m_hgrn_lb_logits, m_hgrn_onorm, m_w_out, m_final_gain, v_norm_gains, v_fox_w_in, v_fox_b_f, v_hgrn_w_in, v_hgrn_lb_logits, v_hgrn_onorm, v_w_out, v_final_gain):
    _, S, D = x.shape
    H = FOX_HEADS
    W = H * HEAD_DIM
    assert HGRN_HEADS == H and w_out.shape[2] == D
    cf = fox_w_in.shape[2]
    ch = hgrn_w_in.shape[2]
    ro = w_out.shape[1]
    co = hgrn_onorm.shape[1]
    assert N_DEV * cf == 4 * W + H and N_DEV * ch == 4 * W and N_DEV * ro == W and N_DEV * co == W
    x2 = x.reshape(S, D)
    tgt = loss_target.reshape(S, D)

    col = lambda n: (lambda r, i: r.at[:, pl.ds(pl.multiple_of(i * n, n), n)])
    row = lambda n: (lambda r, i: r.at[pl.ds(pl.multiple_of(i * n, n), n), :])

    (wf_g,) = _all_gather([fox_w_in[0].astype(BF16)], [SDS((N_DEV, D, cf), BF16)], [lambda r, p: r.at[p]],
                          name="gather_fox_w_in")

    late_views = [col(ch), row(ro), row(ro), col(co)]
    late = _send_start(
        [hgrn_w_in[0].astype(BF16), w_out[0].astype(BF16), w_out[1].astype(BF16), hgrn_onorm],
        [SDS((D, 4 * W), BF16), SDS((W, D), BF16), SDS((W, D), BF16), SDS((1, W), F32)],
        [lambda r, me, p, m: r] * 4, [lambda r, me, m, v=v: v(r, me) for v in late_views], wf_g[0, 0:8],
        name="gather_later_start")
    ng0 = norm_gains[0:1] + late[4][0:1, 0:1]
    wf = jnp.transpose(wf_g, (1, 0, 2)).reshape(D, N_DEV * cf)
    wf_main = jnp.concatenate([wf[:, :3 * W], wf[:, 3 * W + H:]], axis=1)
    wfl_t = wf[:, 3 * W:3 * W + H].T

    h0 = _rms_fwd(x2, ng0, name="rms0_fwd")
    p0 = _mm_nn([h0], wf_main, BF16, scale_cols=(W, LOG2E * HEAD_DIM ** -0.5), name="fox_in_proj")
    fl_t = _mm_nt_rows(wfl_t, h0, name="fox_forget_proj")
    b_col = fox_b_f.reshape(H, 1)
    kaug = _fox_key_aug(*_fox_gate_fwd(fl_t, b_col, name="fox_gate_fwd"))
    o0, y0, qaug = _fox_fwd(p0, kaug, H=H, name="fox_attn_fwd")
    wh, wo0, wo1, onorm = _send_wait(
        late, [lambda r, me, p, m: r] * 4, [lambda r, me, m, v=v: v(r, me) for v in late_views],
        [lambda src, land, me, v=v: (src, v(land, me)) for v in late_views],
        [(D, ch), (ro, D), (ro, D), (1, co)], y0[0:16], name="gather_later_wait")
    x1 = _mm_nn([y0], wo0, F32, residual=x2, name="fox_out_proj")

    lb = _lb_fwd(hgrn_lb_logits, name="hgrn_lower_bound")
    h1 = _rms_fwd(x1, norm_gains[1:2], name="rms1_fwd")
    p1 = _mm_nn([h1], wh, BF16, b_cols=[(0, W), (2 * W, 4 * W)], name="hgrn_in_proj")
    f1 = _mm_nn([h1], wh, F32, b_cols=[(W, 2 * W)], name="hgrn_forget_proj")
    o1, y1, states, a_t1 = _hgrn_fwd(p1, f1, lb, onorm, H=H, name="hgrn_fwd")
    xo = _mm_nn([y1], wo1, F32, residual=x1, name="hgrn_out_proj")

    dx2b, loss_part, dgf = _loss_head(xo, final_gain.reshape(1, D), tgt, name="loss_head")
    loss = lax.psum(jnp.sum(loss_part), ("x", "y", "c"))

    dy1 = _mm_nn([dx2b], wo1, BF16, b_t=True, name="hgrn_out_proj_dx")
    dwo1 = _mm_tn(y1, [dx2b], BF16, name="hgrn_out_proj_dw")
    do1, dg1, donorm = _hgrn_post_bwd(dy1, o1, p1, onorm, H=H, name="hgrn_post_bwd")
    dq1, df1, di1, dlb = _hgrn_bwd(p1, f1, lb, do1, states, a_t1, H=H, name="hgrn_bwd")
    segs1 = [dq1, df1, di1, dg1]
    dh1 = _mm_nn(segs1, wh, BF16, b_t=True, tn=D, name="hgrn_in_proj_dx")
    dwh = _mm_tn(h1, segs1, BF16, name="hgrn_in_proj_dw")
    part_views = [col(ch), row(ro)]
    slot = lambda r, me, m: r.at[m]
    ex1 = _send_start([dwh, dwo1], [SDS((N_DEV, D, ch), BF16), SDS((N_DEV, ro, D), BF16)],
                      [lambda r, me, p, m, v=v: v(r, p) for v in part_views], [slot] * 2, dwo1[0:8],
                      name="exchange_layer1_start")
    ng1 = norm_gains[1:2] + ex1[4][0:1, 0:1]
    dx1b, dng1 = _rms_bwd(x1, ng1, dh1, dx2b, BF16, name="rms1_bwd")

    dy0 = _mm_nn([dx1b], wo0, BF16, b_t=True, name="fox_out_proj_dx")
    dwo0 = _mm_tn(y0, [dx1b], BF16, name="fox_out_proj_dw")
    do0, dg0, doaug = _fox_post_bwd(dy0, o0, p0, H=H, name="fox_post_bwd")
    dq0, dc_row, dk0, dv0, dc_key = _fox_bwd(p0, kaug, qaug, do0, doaug, H=H, name="fox_attn_bwd")
    dfl_t, dbf = _fox_gate_bwd(dc_row.reshape(H, S), dc_key.reshape(H, S), fl_t, b_col, name="fox_gate_bwd")
    dfl_tb = dfl_t.astype(BF16)
    dwfl_t = _mm_nn([dfl_tb], h0, BF16, name="fox_forget_proj_dw")
    segs0 = [dq0, dk0, dv0, dg0]
    dwf_main = _mm_tn(h0, segs0, BF16, name="fox_in_proj_dw")
    dwf = jnp.concatenate([dwf_main[:, :3 * W], dwfl_t.T, dwf_main[:, 3 * W:]], axis=1)
    dwf_blocks = jnp.transpose(dwf.reshape(D, N_DEV, cf), (1, 0, 2))
    ex0 = _send_start([dwf_blocks, dwo0], [SDS((N_DEV, D, cf), BF16), SDS((N_DEV, ro, D), BF16)],
                      [lambda r, me, p, m: r.at[p], lambda r, me, p, m: row(ro)(r, p)], [slot] * 2, dwo0[0:8],
                      name="exchange_layer0_start")
    wfl_t0 = wfl_t + ex0[4][0:1, 0:1].astype(BF16)
    dh0_f = _mm_nn([dfl_tb.T], wfl_t0, BF16, name="fox_forget_proj_dx")
    dh0 = _mm_nn(segs0, wf_main, BF16, residual=dh0_f, b_t=True, tn=D, name="fox_in_proj_dx")
    grad_x, dng0 = _rms_bwd(x2, norm_gains[0:1], dh0, dx1b, F32, name="rms0_bwd")

    own1 = [lambda src, land, me, v=v: (v(src, me), land.at[0]) for v in part_views]
    rh, ro1 = _send_wait(ex1, [lambda r, me, p, m, v=v: v(r, p) for v in part_views], [slot] * 2, own1,
                         [(D, ch), (ro, D)], dng0, name="exchange_layer1_wait")
    rf, ro0 = _send_wait(ex0, [lambda r, me, p, m: r.at[p], lambda r, me, p, m: row(ro)(r, p)], [slot] * 2,
                         [lambda src, land, me: (src.at[me], land.at[0]),
                          lambda src, land, me: (row(ro)(src, me), land.at[0])],
                         [(D, cf), (ro, D)], dng0, name="exchange_layer0_wait")

    pad = lambda a: jnp.pad(a, ((0, 0), (0, W - a.shape[1])))
    stats = jnp.concatenate([dng0, dng1, dlb, dgf, pad(dbf.reshape(1, H)), donorm,
                             jnp.zeros((2, W), F32)], axis=0)
    assert D == W
    (stats_all,) = _all_gather([stats], [SDS((N_DEV, STAT_ROWS, W), F32)], [lambda r, p: r.at[p]],
                               name="gather_small_grads")
    g_small = _stats_reduce(stats_all, hgrn_lb_logits, name="reduce_small_grads")
    me = 4 * lax.axis_index("x") + 2 * lax.axis_index("y") + lax.axis_index("c")
    g_onorm = lax.dynamic_slice_in_dim(g_small[6:7], me * co, co, axis=1)

    def upd(w, m, v, parts, name):
        shp = w.shape
        r2 = (-1, shp[-1])
        g, d, mn, vn = _adamw(w.reshape(r2), m.reshape(r2), v.reshape(r2), parts, name=name)
        return g.reshape(shp), d.reshape(shp), mn.reshape(shp), vn.reshape(shp)

    res = {
        "norm_gains": upd(norm_gains, m_norm_gains, v_norm_gains, [g_small[None, 0:2]], "adamw_norm_gains"),
        "fox_w_in": upd(fox_w_in, m_fox_w_in, v_fox_w_in, [rf], "adamw_fox_w_in"),
        "fox_b_f": upd(fox_b_f, m_fox_b_f, v_fox_b_f, [g_small[None, 5:6, :H]], "adamw_fox_b_f"),
        "hgrn_w_in": upd(hgrn_w_in, m_hgrn_w_in, v_hgrn_w_in, [rh], "adamw_hgrn_w_in"),
        "hgrn_lb_logits": upd(hgrn_lb_logits, m_hgrn_lb_logits, v_hgrn_lb_logits, [g_small[None, 2:4]],
                              "adamw_hgrn_lb_logits"),
        "hgrn_onorm": upd(hgrn_onorm, m_hgrn_onorm, v_hgrn_onorm, [g_onorm[None]], "adamw_hgrn_onorm"),
        "w_out": upd(w_out, m_w_out, v_w_out, [ro0, ro1], "adamw_w_out"),
        "final_gain": upd(final_gain.reshape(1, D), m_final_gain.reshape(1, D), v_final_gain.reshape(1, D),
                          [g_small[None, 4:5]], "adamw_final_gain"),
    }
    order = ["norm_gains", "fox_w_in", "fox_b_f", "hgrn_w_in", "hgrn_lb_logits", "hgrn_onorm", "w_out", "final_gain"]
    fix = lambda n, a: a.reshape(D) if n == "final_gain" else a
    outs = [loss, grad_x.reshape(1, S, D)]
    for k in range(4):
        outs += [fix(n, res[n][k]) for n in order]
    return tuple(outs)
```

```python
import functools

import numpy as np
import jax
import jax.numpy as jnp
from jax import lax
from jax.experimental import pallas as pl
from jax.experimental.pallas import tpu as pltpu

F32 = jnp.float32
BF16 = jnp.bfloat16
SDS = jax.ShapeDtypeStruct
MESH = pl.DeviceIdType.MESH

EPS = 1e-6
ADAM_LR, ADAM_B1, ADAM_B2, ADAM_EPS, ADAM_WD, ADAM_STEP = 0.001, 0.9, 0.999, 1e-08, 0.01, 10

N_DEV = 8
FOX_HEADS = 16
HGRN_HEADS = 16
HEAD_DIM = 128
HGRN_CHUNK = 128
HGRN_LEAF = 16
HGRN_HEADS_PER_STEP = 16
EXP_CLAMP = 85.0
ATT_BLOCK = 512
ATT_HEADS_PER_STEP = 4
ATT_BWD_HEADS_PER_STEP = 2
POST_HEADS_PER_STEP = 4
NEG = -1e30
LOG2E = 1.4426950408889634
LN2 = 0.6931471805599453

VMEM_LIMIT_V7X = 56 * 1024 * 1024


def _params(*sem):
    return pltpu.CompilerParams(dimension_semantics=sem, vmem_limit_bytes=VMEM_LIMIT_V7X)


def _silu(x):
    return x * jax.nn.sigmoid(x)


def _dsilu(x):
    s = jax.nn.sigmoid(x)
    return s * (1.0 + x * (1.0 - s))


def _dot(a, b):
    return jnp.dot(a, b, preferred_element_type=F32)


def _dot_nt(a, b):
    return lax.dot_general(a, b, (((1,), (1,)), ((), ())), preferred_element_type=F32)


def _dot_tn(a, b):
    return lax.dot_general(a, b, (((0,), (0,)), ((), ())), preferred_element_type=F32)


def _mm_nn(a_list, b, out_dtype, *, name, residual=None, scale_cols=None, b_t=False, b_cols=None,
           tm=1024, tn=1024, tk=2048):
    ns = len(a_list)
    M, Ks = a_list[0].shape
    K, N = (b.shape[1], b.shape[0]) if b_t else b.shape
    if b_cols is None:
        b_cols = [(0, N)]
    else:
        assert not b_t
        N = sum(e - s for s, e in b_cols)
    dot = _dot_nt if b_t else _dot
    assert K == ns * Ks and all(a.shape == (M, Ks) for a in a_list)
    if ns > 1:
        tk = tk // 2
    tm, tn, tk = min(tm, M), min(tn, N), min(tk, Ks)
    assert M % tm == 0 and N % tn == 0 and Ks % tk == 0
    assert scale_cols is None or scale_cols[0] % tn == 0
    assert all(s % tn == 0 and e % tn == 0 for s, e in b_cols)
    nks = Ks // tk
    nk = ns * nks
    has_res = residual is not None

    def body(*refs):
        a_refs, b_ref = refs[:ns], refs[ns]
        res_ref = refs[ns + 1] if has_res else None
        o_ref = refs[ns + 1 + has_res]

        def finish(r):
            if has_res:
                r = r + res_ref[...].astype(F32)
            if scale_cols is not None:
                r = r * jnp.where(pl.program_id(1) < scale_cols[0] // tn, scale_cols[1], 1.0)
            o_ref[...] = r.astype(out_dtype)

        if nk == 1:
            finish(dot(a_refs[0][...], b_ref[...]))
            return
        acc_ref = refs[ns + 2 + has_res]
        k = pl.program_id(2)

        @pl.when(k == 0)
        def _():
            acc_ref[...] = jnp.zeros_like(acc_ref)

        for s in range(ns):
            def step(s=s):
                acc_ref[...] += dot(a_refs[s][...], b_ref[...])

            if ns == 1:
                step()
            else:
                pl.when(k // nks == s)(step)

        @pl.when(k == nk - 1)
        def _():
            finish(acc_ref[...])

    def a_map(i, j, k, s):
        return (i, jnp.clip(k - s * nks, 0, nks - 1))

    in_specs = [pl.BlockSpec((tm, tk), functools.partial(a_map, s=s)) for s in range(ns)]
    def b_col(j):
        src = j + b_cols[0][0] // tn
        for (_, e0), (s1, _) in zip(b_cols[:-1], b_cols[1:]):
            src = src + jnp.where(src >= e0 // tn, (s1 - e0) // tn, 0)
        return src

    if b_t:
        in_specs.append(pl.BlockSpec((tn, tk), lambda i, j, k: (j, k)))
    else:
        in_specs.append(pl.BlockSpec((tk, tn), lambda i, j, k: (k, b_col(j))))
    args = list(a_list) + [b]
    if has_res:
        in_specs.append(pl.BlockSpec((tm, tn), lambda i, j, k: (i, j)))
        args.append(residual)
    return pl.pallas_call(
        body, grid=(M // tm, N // tn, nk), in_specs=in_specs,
        out_specs=pl.BlockSpec((tm, tn), lambda i, j, k: (i, j)),
        out_shape=SDS((M, N), out_dtype),
        scratch_shapes=[] if nk == 1 else [pltpu.VMEM((tm, tn), F32)],
        compiler_params=_params("parallel", "parallel", "arbitrary"), name=name,
    )(*args)


def _mm_tn(a, b_list, out_dtype, *, name, tm=2048, tn=1024, tk=512):
    ns = len(b_list)
    S, M = a.shape
    Ns = b_list[0].shape[1]
    assert all(b.shape == (S, Ns) for b in b_list)
    tm, tn, tk = min(tm, M), min(tn, Ns), min(tk, S)
    assert M % tm == 0 and Ns % tn == 0 and S % tk == 0
    njs = Ns // tn
    nk = S // tk

    def body(*refs):
        a_ref, b_refs, o_ref, acc_ref = refs[0], refs[1:1 + ns], refs[1 + ns], refs[2 + ns]
        j, k = pl.program_id(1), pl.program_id(2)

        @pl.when(k == 0)
        def _():
            acc_ref[...] = jnp.zeros_like(acc_ref)

        for s in range(ns):
            def step(s=s):
                acc_ref[...] += _dot_tn(a_ref[...], b_refs[s][...])

            if ns == 1:
                step()
            else:
                pl.when(j // njs == s)(step)

        @pl.when(k == nk - 1)
        def _():
            o_ref[...] = acc_ref[...].astype(out_dtype)

    def b_map(i, j, k, s):
        return (k, jnp.clip(j - s * njs, 0, njs - 1))

    in_specs = [pl.BlockSpec((tk, tm), lambda i, j, k: (k, i))]
    in_specs += [pl.BlockSpec((tk, tn), functools.partial(b_map, s=s)) for s in range(ns)]
    return pl.pallas_call(
        body, grid=(M // tm, ns * njs, nk), in_specs=in_specs,
        out_specs=pl.BlockSpec((tm, tn), lambda i, j, k: (i, j)),
        out_shape=SDS((M, ns * Ns), out_dtype),
        scratch_shapes=[pltpu.VMEM((tm, tn), F32)],
        compiler_params=_params("parallel", "parallel", "arbitrary"), name=name,
    )(a, *b_list)


def _mm_nt_rows(w_t, h, *, name, tn=1024):
    R, K = w_t.shape
    S = h.shape[0]
    tn = min(tn, S)

    def body(w_ref, h_ref, o_ref):
        o_ref[...] = _dot_nt(w_ref[...], h_ref[...])

    return pl.pallas_call(
        body, grid=(S // tn,),
        in_specs=[pl.BlockSpec((R, K), lambda i: (0, 0)), pl.BlockSpec((tn, K), lambda i: (i, 0))],
        out_specs=pl.BlockSpec((R, tn), lambda i: (0, i)),
        out_shape=SDS((R, S), F32), compiler_params=_params("parallel"), name=name,
    )(w_t, h)


def _rms_fwd(x, gain, *, name, tm=512):
    S, D = x.shape
    tm = min(tm, S)

    def body(x_ref, g_ref, h_ref):
        xv = x_ref[...]
        r = lax.rsqrt(jnp.mean(xv * xv, axis=-1, keepdims=True) + EPS)
        h_ref[...] = ((xv * r) * g_ref[...]).astype(BF16)

    return pl.pallas_call(
        body, grid=(S // tm,),
        in_specs=[pl.BlockSpec((tm, D), lambda i: (i, 0)), pl.BlockSpec((1, D), lambda i: (0, 0))],
        out_specs=pl.BlockSpec((tm, D), lambda i: (i, 0)),
        out_shape=SDS((S, D), BF16), compiler_params=_params("parallel"), name=name,
    )(x, gain)


def _rms_bwd(x, gain, dh, dres, out_dtype, *, name, tm=512):
    S, D = x.shape
    tm = min(tm, S)

    def body(x_ref, g_ref, dh_ref, dres_ref, dx_ref, dg_ref):
        @pl.when(pl.program_id(0) == 0)
        def _():
            dg_ref[...] = jnp.zeros_like(dg_ref)

        xv = x_ref[...]
        r = lax.rsqrt(jnp.mean(xv * xv, axis=-1, keepdims=True) + EPS)
        xh = xv * r
        dhv = dh_ref[...].astype(F32)
        dg_ref[...] += jnp.sum(dhv * xh, axis=0, keepdims=True)
        dxh = dhv * g_ref[...]
        dx = r * (dxh - xh * jnp.mean(dxh * xh, axis=-1, keepdims=True)) + dres_ref[...].astype(F32)
        dx_ref[...] = dx.astype(out_dtype)

    row = pl.BlockSpec((tm, D), lambda i: (i, 0))
    vec = pl.BlockSpec((1, D), lambda i: (0, 0))
    return pl.pallas_call(
        body, grid=(S // tm,), in_specs=[row, vec, row, row], out_specs=[row, vec],
        out_shape=[SDS((S, D), out_dtype), SDS((1, D), F32)],
        compiler_params=_params("arbitrary"), name=name,
    )(x, gain, dh, dres)


def _loss_head(x, gain, target, *, name, tm=512):
    S, D = x.shape
    tm = min(tm, S)
    assert tm % 8 == 0 and D % 128 == 0

    def body(x_ref, g_ref, t_ref, dxb_ref, loss_ref, dg_ref):
        @pl.when(pl.program_id(0) == 0)
        def _():
            dg_ref[...] = jnp.zeros_like(dg_ref)
            loss_ref[...] = jnp.zeros_like(loss_ref)

        xv = x_ref[...]
        g = g_ref[...]
        r = lax.rsqrt(jnp.mean(xv * xv, axis=-1, keepdims=True) + EPS)
        xh = xv * r
        err = xh * g - t_ref[...]
        e2 = (err * err).reshape(tm // 8, 8, D).sum(axis=0)
        part = e2[:, 0:128]
        for k in range(1, D // 128):
            part = part + e2[:, k * 128:(k + 1) * 128]
        loss_ref[...] += part * (0.5 / D)
        dy = err * (1.0 / D)
        dg_ref[...] += jnp.sum(dy * xh, axis=0, keepdims=True)
        dxh = dy * g
        dx = r * (dxh - xh * jnp.mean(dxh * xh, axis=-1, keepdims=True))
        dxb_ref[...] = dx.astype(BF16)

    row = pl.BlockSpec((tm, D), lambda i: (i, 0))
    vec = pl.BlockSpec((1, D), lambda i: (0, 0))
    return pl.pallas_call(
        body, grid=(S // tm,), in_specs=[row, vec, row],
        out_specs=[row, pl.BlockSpec((8, 128), lambda i: (0, 0)), vec],
        out_shape=[SDS((S, D), BF16), SDS((8, 128), F32), SDS((1, D), F32)],
        compiler_params=_params("arbitrary"), name=name,
    )(x, gain, target)


def _split3(x):
    hi = x.astype(BF16)
    r1 = x - hi.astype(F32)
    mid = r1.astype(BF16)
    lo = (r1 - mid.astype(F32)).astype(BF16)
    return hi, mid, lo


def _split2(x):
    hi = x.astype(BF16)
    lo = (x - hi.astype(F32)).astype(BF16)
    return hi, lo


def _fox_gate_fwd(fl_t, b_col, *, name):
    H, S = fl_t.shape
    L = 128
    tri = jnp.asarray(np.triu(np.ones((L, L), np.float32)), BF16)

    def body(fl_ref, b_ref, tri_ref, hi_ref, mid_ref, lo_ref, carry):
        @pl.when(pl.program_id(0) == 0)
        def _():
            carry[...] = jnp.zeros_like(carry)

        z = fl_ref[...] + b_ref[...]
        lf = jnp.minimum(z, 0.0) - jnp.log(1.0 + jnp.exp(-jnp.abs(z)))
        hi, mid, lo = _split3(lf)
        t = tri_ref[...]
        c = (_dot(hi, t) + _dot(mid, t)) + _dot(lo, t) + carry[...]
        carry[...] = c[:, L - 1:L]
        hi_ref[...], mid_ref[...], lo_ref[...] = _split3(c * (-LOG2E))

    blk = pl.BlockSpec((H, L), lambda i: (0, i))
    return pl.pallas_call(
        body, grid=(S // L,),
        in_specs=[blk, pl.BlockSpec((H, 1), lambda i: (0, 0)), pl.BlockSpec((L, L), lambda i: (0, 0))],
        out_specs=[blk] * 3, out_shape=[SDS((H, S), BF16)] * 3, scratch_shapes=[pltpu.VMEM((H, 1), F32)],
        compiler_params=_params("arbitrary"), name=name,
    )(fl_t, b_col, tri)


def _fox_gate_bwd(dc_row, dc_key, fl_t, b_col, *, name):
    H, S = fl_t.shape
    L = 128
    n = S // L
    tri = jnp.asarray(np.tril(np.ones((L, L), np.float32)), BF16)

    def body(dcr_ref, dck_ref, fl_ref, b_ref, tri_ref, dfl_ref, db_ref, carry):
        @pl.when(pl.program_id(0) == 0)
        def _():
            carry[...] = jnp.zeros_like(carry)
            db_ref[...] = jnp.zeros_like(db_ref)

        hi, mid, lo = _split3(dcr_ref[...] + dck_ref[...])
        t = tri_ref[...]
        dlf = (_dot(hi, t) + _dot(mid, t)) + _dot(lo, t) + carry[...]
        carry[...] = dlf[:, 0:1]
        z = fl_ref[...] + b_ref[...]
        dfl = dlf * jax.nn.sigmoid(-z)
        dfl_ref[...] = dfl
        db_ref[...] += jnp.sum(dfl, axis=1, keepdims=True)

    blk = pl.BlockSpec((H, L), lambda i: (0, n - 1 - i))
    col = pl.BlockSpec((H, 1), lambda i: (0, 0))
    return pl.pallas_call(
        body, grid=(n,), in_specs=[blk, blk, blk, col, pl.BlockSpec((L, L), lambda i: (0, 0))],
        out_specs=[blk, col], out_shape=[SDS((H, S), F32), SDS((H, 1), F32)],
        scratch_shapes=[pltpu.VMEM((H, 1), F32)], compiler_params=_params("arbitrary"), name=name,
    )(dc_row, dc_key, fl_t, b_col, tri)


AUG = HEAD_DIM


def _lane_select(cols, shape):
    lane = lax.broadcasted_iota(jnp.int32, shape, 1)
    out = jnp.zeros(shape, BF16)
    for k, c in reversed(list(enumerate(cols))):
        c = jnp.full(shape, c, BF16) if isinstance(c, (int, float)) else jnp.broadcast_to(c, shape).astype(BF16)
        out = jnp.where(lane == k, c, out)
    return out


def _fox_key_aug(b_hi, b_mid, b_lo):
    H, S = b_hi.shape
    ones = jnp.ones((H, S), BF16)
    ka = jnp.stack([b_hi, b_mid, b_lo, ones, ones, ones], axis=-1)
    ka = jnp.pad(ka, ((0, 0), (0, 0), (0, AUG - 6)))
    return jnp.transpose(ka, (1, 0, 2)).reshape(S, H * AUG)


def _fox_fwd(p0, kaug, *, H, name):
    S = p0.shape[0]
    T = min(ATT_BLOCK, S)
    nq = S // T
    dh = HEAD_DIM
    G = ATT_HEADS_PER_STEP
    assert H % G == 0

    def body(q_ref, k_ref, ka_ref, v_ref, g_ref, o_ref, y_ref, qa_ref, m_sc, acc_sc, p_sc, al_sc):
        i = pl.program_id(1)
        qaug = _lane_select([1.0, 1.0, 1.0], (T, AUG))
        ones = jnp.ones((T, dh), BF16)
        m_sc[...] = jnp.full_like(m_sc, NEG)
        acc_sc[...] = jnp.zeros_like(acc_sc)

        def step(j, before, masked):
            rows = pl.ds(pl.multiple_of(j * T, T), T)
            for g in range(G):
                hd = slice(g * dh, (g + 1) * dh)
                if not masked:
                    prev = pl.ds(pl.multiple_of(before * T, T), T)
                    vp = jnp.concatenate([v_ref[prev, hd], ones], axis=1)
                    acc_sc[g] = jnp.tile(al_sc[g], (1, 2)) * acc_sc[g] + _dot(p_sc[g], vp)
                q = jnp.concatenate([q_ref[:, hd], qaug], axis=1)
                kj = jnp.concatenate([k_ref[rows, hd], ka_ref[rows, hd]], axis=1)
                t = _dot_nt(q, kj)
                if masked:
                    row = lax.broadcasted_iota(jnp.int32, (T, T), 0)
                    col = lax.broadcasted_iota(jnp.int32, (T, T), 1)
                    t = jnp.where(row >= col, t, NEG)
                m_prev = m_sc[g]
                m_new = jnp.maximum(m_prev, jnp.max(t, axis=-1, keepdims=True))
                p_sc[g] = jnp.exp2(t - jnp.tile(m_new, (1, T // 128))).astype(BF16)
                al_sc[g] = jnp.exp2(m_prev - m_new)
                m_sc[g] = m_new

        step(i, None, True)

        def loop_body(j, carry):
            step(j, jnp.where(j == 0, i, j - 1), False)
            return carry

        lax.fori_loop(0, i, loop_body, 0)
        rows = pl.ds(pl.multiple_of(jnp.where(i == 0, 0, i - 1) * T, T), T)
        for g in range(G):
            hd = slice(g * dh, (g + 1) * dh)
            vp = jnp.concatenate([v_ref[rows, hd], ones], axis=1)
            acc = jnp.tile(al_sc[g], (1, 2)) * acc_sc[g] + _dot(p_sc[g], vp)
            l = acc[:, dh:]
            o = acc[:, :dh] / l
            o_ref[:, hd] = o
            y_ref[:, hd] = (o * _silu(g_ref[:, hd].astype(F32))).astype(BF16)
            hi, mid, lo = _split3(-(m_sc[g] + jnp.log2(l)))
            qa_ref[:, hd] = _lane_select([1.0, 1.0, 1.0, hi, mid, lo], (T, AUG))

    blk = lambda off: pl.BlockSpec((T, G * dh), lambda h, i: (i, off // G + h))
    full = lambda off: pl.BlockSpec((S, G * dh), lambda h, i: (0, off // G + h), pipeline_mode=pl.Buffered(1))
    return pl.pallas_call(
        body, grid=(H // G, nq),
        in_specs=[blk(0), full(H), full(0), full(2 * H), blk(3 * H)],
        out_specs=[blk(0), blk(0), blk(0)],
        out_shape=[SDS((S, H * dh), F32), SDS((S, H * dh), BF16), SDS((S, H * AUG), BF16)],
        scratch_shapes=[pltpu.VMEM((G, T, 128), F32), pltpu.VMEM((G, T, 2 * dh), F32),
                        pltpu.VMEM((G, T, T), BF16), pltpu.VMEM((G, T, 128), F32)],
        compiler_params=_params("parallel", "arbitrary"), name=name,
    )(p0, p0, kaug, p0, p0)


def _fox_post_bwd(dy, o, p0, *, H, name, tm=512):
    S = dy.shape[0]
    dh = HEAD_DIM
    tm = min(tm, S)
    G = POST_HEADS_PER_STEP
    assert H % G == 0

    def body(dy_ref, o_ref, g_ref, do_ref, dg_ref, da_ref):
        dyv = dy_ref[...].astype(F32)
        ov = o_ref[...]
        g = g_ref[...].astype(F32)
        do = (dyv * _silu(g)).astype(BF16)
        do_ref[...] = do
        dg_ref[...] = (dyv * ov * _dsilu(g)).astype(BF16)
        prod = do.astype(F32) * ov
        for k in range(G):
            hd = slice(k * dh, (k + 1) * dh)
            delta = jnp.sum(prod[:, hd], axis=-1, keepdims=True)
            hi, mid, lo = _split3(-jnp.broadcast_to(delta, (tm, AUG)))
            da_ref[:, hd] = _lane_select([hi, mid, lo], (tm, AUG))

    blk = pl.BlockSpec((tm, G * dh), lambda h, i: (i, h))
    return pl.pallas_call(
        body, grid=(H // G, S // tm),
        in_specs=[blk, blk, pl.BlockSpec((tm, G * dh), lambda h, i: (i, 3 * H // G + h))],
        out_specs=[blk, blk, blk],
        out_shape=[SDS((S, H * dh), BF16), SDS((S, H * dh), BF16), SDS((S, H * AUG), BF16)],
        compiler_params=_params("parallel", "parallel"), name=name,
    )(dy, o, p0)


def _fox_bwd(p0, kaug, qaug, do, doaug, *, H, name):
    S = p0.shape[0]
    T = min(ATT_BLOCK, S)
    nq = S // T
    dh = HEAD_DIM
    scale = dh ** -0.5
    G = ATT_BWD_HEADS_PER_STEP
    assert H % G == 0

    def body(q_ref, qa_ref, k_ref, ka_ref, v_ref, do_ref, da_ref, dq_ref, rs_ref, dk_ref, dv_ref, dc_ref,
             dq_sc, dk_sc, dv_sc, pt_sc, dst_sc):
        j = pl.program_id(1)
        vaug = _lane_select([1.0, 1.0, 1.0], (T, AUG))
        ones = jnp.ones((T, dh), BF16)

        @pl.when(j == 0)
        def _():
            dq_sc[...] = jnp.zeros_like(dq_sc)

        dk_sc[...] = jnp.zeros_like(dk_sc)
        dv_sc[...] = jnp.zeros_like(dv_sc)

        def apply(prev):
            for g in range(G):
                hd = slice(g * dh, (g + 1) * dh)
                dv_sc[g] += _dot(pt_sc[g], do_ref[prev, hd])
                dk_sc[g] += _dot(dst_sc[g], jnp.concatenate([q_ref[prev, hd], ones], axis=1))
                dq_sc[g, prev] += _dot_tn(dst_sc[g], jnp.concatenate([k_ref[:, hd], ones], axis=1))

        def step(i, masked):
            rows = pl.ds(pl.multiple_of(i * T, T), T)
            if not masked:
                apply(pl.ds(pl.multiple_of((i - 1) * T, T), T))
            for g in range(G):
                hd = slice(g * dh, (g + 1) * dh)
                k = jnp.concatenate([k_ref[:, hd], ka_ref[:, hd]], axis=1)
                v = jnp.concatenate([v_ref[:, hd], vaug], axis=1)
                pt = jnp.exp2(_dot_nt(k, jnp.concatenate([q_ref[rows, hd], qa_ref[rows, hd]], axis=1)))
                if masked:
                    row = lax.broadcasted_iota(jnp.int32, (T, T), 0)
                    col = lax.broadcasted_iota(jnp.int32, (T, T), 1)
                    pt = jnp.where(col >= row, pt, 0.0)
                dst = pt * _dot_nt(v, jnp.concatenate([do_ref[rows, hd], da_ref[rows, hd]], axis=1))
                pt_sc[g] = pt.astype(BF16)
                dst_sc[g] = dst.astype(BF16)

        step(j, True)

        def loop_body(i, carry):
            step(i, False)
            return carry

        lax.fori_loop(j + 1, nq, loop_body, 0)
        apply(pl.ds((nq - 1) * T, T))
        for g in range(G):
            hd = slice(g * dh, (g + 1) * dh)
            dk_ref[:, hd] = (dk_sc[g, :, :dh] * LN2).astype(BF16)
            dv_ref[:, hd] = dv_sc[g].astype(BF16)
            dc_ref[g] = -jnp.transpose(dk_sc[g, :, dh:])[0:1]

        @pl.when(j == nq - 1)
        def _():
            for g in range(G):
                dq_ref[:, g * dh:(g + 1) * dh] = (dq_sc[g, :, :dh] * scale).astype(BF16)
                for i in range(nq):
                    rs_ref[g, :, i * T:(i + 1) * T] = jnp.transpose(dq_sc[g, i * T:(i + 1) * T, dh:])[0:1]

    blk = lambda off: pl.BlockSpec((T, G * dh), lambda h, j: (j, off // G + h))
    full = lambda off: pl.BlockSpec((S, G * dh), lambda h, j: (0, off // G + h))
    once = lambda off: pl.BlockSpec((S, G * dh), lambda h, j: (0, off // G + h), pipeline_mode=pl.Buffered(1))
    rowv = pl.BlockSpec((G, 1, T), lambda h, j: (h, 0, j))
    return pl.pallas_call(
        body, grid=(H // G, nq),
        in_specs=[once(0), once(0), blk(H), blk(0), blk(2 * H), once(0), once(0)],
        out_specs=[full(0), pl.BlockSpec((G, 1, S), lambda h, j: (h, 0, 0)), blk(0), blk(0), rowv],
        out_shape=[SDS((S, H * dh), BF16), SDS((H, 1, S), F32), SDS((S, H * dh), BF16), SDS((S, H * dh), BF16),
                   SDS((H, 1, S), F32)],
        scratch_shapes=[pltpu.VMEM((G, S, 2 * dh), F32), pltpu.VMEM((G, T, 2 * dh), F32), pltpu.VMEM((G, T, dh), F32),
                        pltpu.VMEM((G, T, T), BF16), pltpu.VMEM((G, T, T), BF16)],
        compiler_params=_params("parallel", "arbitrary"), name=name,
    )(p0, qaug, p0, kaug, p0, do, doaug)


def _hgrn_levels(C, leaf):
    levels = []
    h = C // 2
    while h >= leaf:
        levels.append(h)
        h //= 2
    return levels


def _hgrn_sum_matrix(C, leaf):
    t = np.arange(C)[:, None]
    u = np.arange(C)[None, :]
    mats = [(u <= t), (u > t)]
    for h in _hgrn_levels(C, leaf):
        start = (t // (2 * h)) * (2 * h)
        mid = start + h - 1
        second = t > mid
        m = np.where(second, (u > mid) & (u <= t), (u > t) & (u <= mid))
        mats.append(m)
    lstart = (t // leaf) * leaf
    mats.append((u >= lstart) & (u <= t))
    return np.concatenate([m.astype(np.float32) for m in mats], axis=0)


def _hgrn_chunk_terms(qr, fz, lb, msum, C, leaf):
    levels = _hgrn_levels(C, leaf)
    sq = _silu(qr)
    sp = 1.0 / (1.0 + jnp.exp(-fz))
    sn = 1.0 / (1.0 + jnp.exp(fz))
    f = lb + (1.0 - lb) * sp
    lf = jnp.log(f)
    k = (1.0 - lb) * sn
    hi, lo = _split2(lf)
    dsum = _dot(msum, hi) + _dot(msum, lo)
    b = dsum[0:C]
    kdec = dsum[C:2 * C]
    rowi = lax.broadcasted_iota(jnp.int32, (C, 1), 0)
    lev = []
    for n, h in enumerate(levels):
        e = jnp.exp(dsum[(2 + n) * C:(3 + n) * C])
        selq = jnp.where((rowi % (2 * h)) >= h, 1.0, 0.0)
        qm = (sq * e * selq).astype(BF16)
        km = (k * e * (1.0 - selq)).astype(BF16)
        lev.append((h, e, selq, qm, km))
    dleaf = dsum[(2 + len(levels)) * C:(3 + len(levels)) * C]
    eq = jnp.exp(dleaf)
    ek = jnp.exp(jnp.minimum(-dleaf, EXP_CLAMP))
    return dict(sq=sq, sp=sp, sn=sn, f=f, k=k, b=b, kdec=kdec, lev=lev, eq=eq, ek=ek,
                ql=(sq * eq).astype(BF16), kl=(k * ek).astype(BF16),
                qs=(sq * jnp.exp(b)).astype(BF16), ke=(k * jnp.exp(kdec)).astype(BF16),
                e_c=jnp.exp(b[C - 1:C, :]))


def _hgrn_masks(C, leaf, transposed):
    a = lax.broadcasted_iota(jnp.int32, (C, C), 0)
    bb = lax.broadcasted_iota(jnp.int32, (C, C), 1)
    t, s = (bb, a) if transposed else (a, bb)
    lev = [None if 2 * h == C else (t // (2 * h)) == (s // (2 * h)) for h in _hgrn_levels(C, leaf)]
    if leaf == C:
        leafm = s <= t
    else:
        leafm = ((t // leaf) == (s // leaf)) & (s <= t)
    return lev, leafm


def _hgrn_fwd(p1, f1, lb, onorm, *, H, name, tb=512):
    S = p1.shape[0]
    dk = HEAD_DIM
    C = min(HGRN_CHUNK, S)
    leaf = min(HGRN_LEAF, C)
    tb = min(tb, S)
    nc = tb // C
    G = HGRN_HEADS_PER_STEP
    assert H % G == 0
    msum = jnp.asarray(_hgrn_sum_matrix(C, leaf), BF16)

    def body(q_ref, f_ref, v_ref, g_ref, lb_ref, on_ref, ms_ref, o_ref, y_ref, st_ref, at_ref, st_sc):
        @pl.when(pl.program_id(1) == 0)
        def _():
            st_sc[...] = jnp.zeros_like(st_sc)

        msv = ms_ref[...]
        lmask, leafm = _hgrn_masks(C, leaf, False)

        def chunk(n, carry):
            rows = pl.ds(pl.multiple_of(n * C, C), C)
            for g in range(G):
                hd = slice(g * dk, (g + 1) * dk)
                tm = _hgrn_chunk_terms(q_ref[rows, hd].astype(F32), f_ref[rows, hd], lb_ref[:, hd], msv, C, leaf)
                v = v_ref[rows, hd]
                st = st_sc[g]
                st_ref[g, n] = st
                a = jnp.where(leafm, _dot_nt(tm["ql"], tm["kl"]), 0.0)
                for (h, e, selq, qm, km), m in zip(tm["lev"], lmask):
                    al = _dot_nt(qm, km)
                    a = a + (al if m is None else jnp.where(m, al, 0.0))
                at_ref[g, n] = jnp.transpose(a).astype(BF16)
                o = _dot_nt(tm["qs"], st.astype(BF16)) + _dot(a.astype(BF16), v)
                st_sc[g] = st * tm["e_c"] + _dot(v.T, tm["ke"])
                o_ref[rows, hd] = o
                rn = lax.rsqrt(jnp.mean(o * o, axis=-1, keepdims=True) + EPS)
                y = ((o * rn) * on_ref[:, hd]) * _silu(g_ref[rows, hd].astype(F32))
                y_ref[rows, hd] = y.astype(BF16)
            return carry

        lax.fori_loop(0, nc, chunk, 0)

    blk = lambda off: pl.BlockSpec((tb, G * dk), lambda h, i: (i, off // G + h))
    vec = pl.BlockSpec((1, G * dk), lambda h, i: (0, h))
    return pl.pallas_call(
        body, grid=(H // G, S // tb),
        in_specs=[blk(0), blk(0), blk(H), blk(2 * H), vec, vec,
                  pl.BlockSpec(msum.shape, lambda h, i: (0, 0))],
        out_specs=[blk(0), blk(0), pl.BlockSpec((G, nc, dk, dk), lambda h, i: (h, i, 0, 0)),
                   pl.BlockSpec((G, nc, C, C), lambda h, i: (h, i, 0, 0))],
        out_shape=[SDS((S, H * dk), F32), SDS((S, H * dk), BF16), SDS((H, S // C, dk, dk), F32),
                   SDS((H, S // C, C, C), BF16)],
        scratch_shapes=[pltpu.VMEM((G, dk, dk), F32)],
        compiler_params=_params("parallel", "arbitrary"), name=name,
    )(p1, f1, p1, p1, lb, onorm, msum)


def _hgrn_post_bwd(dy, o, p1, onorm, *, H, name, tm=512):
    S = dy.shape[0]
    dk = HEAD_DIM
    tm = min(tm, S)
    G = POST_HEADS_PER_STEP
    assert H % G == 0

    def body(dy_ref, o_ref, g_ref, on_ref, do_ref, dg_ref, don_ref):
        @pl.when(pl.program_id(1) == 0)
        def _():
            don_ref[...] = jnp.zeros_like(don_ref)

        for k in range(G):
            hd = slice(k * dk, (k + 1) * dk)
            dyv = dy_ref[:, hd].astype(F32)
            ov = o_ref[:, hd]
            g = g_ref[:, hd].astype(F32)
            onv = on_ref[:, hd]
            rn = lax.rsqrt(jnp.mean(ov * ov, axis=-1, keepdims=True) + EPS)
            oh = ov * rn
            dn = dyv * _silu(g)
            dg_ref[:, hd] = (dyv * (oh * onv) * _dsilu(g)).astype(BF16)
            don_ref[:, hd] += jnp.sum(dn * oh, axis=0, keepdims=True)
            doh = dn * onv
            do_ref[:, hd] = (rn * (doh - oh * jnp.mean(doh * oh, axis=-1, keepdims=True))).astype(BF16)

    blk = pl.BlockSpec((tm, G * dk), lambda h, i: (i, h))
    vec = pl.BlockSpec((1, G * dk), lambda h, i: (0, h))
    return pl.pallas_call(
        body, grid=(H // G, S // tm),
        in_specs=[blk, blk, pl.BlockSpec((tm, G * dk), lambda h, i: (i, 2 * H // G + h)), vec],
        out_specs=[blk, blk, vec],
        out_shape=[SDS((S, H * dk), BF16), SDS((S, H * dk), BF16), SDS((1, H * dk), F32)],
        compiler_params=_params("parallel", "arbitrary"), name=name,
    )(dy, o, p1, onorm)


def _hgrn_bwd(p1, f1, lb, do, states, a_t, *, H, name, tb=512):
    S = p1.shape[0]
    dk = HEAD_DIM
    C = min(HGRN_CHUNK, S)
    leaf = min(HGRN_LEAF, C)
    tb = min(tb, S)
    nc = tb // C
    nb = S // tb
    G = HGRN_HEADS_PER_STEP
    assert H % G == 0
    msum = jnp.asarray(_hgrn_sum_matrix(C, leaf), BF16)
    rtri = jnp.asarray(np.triu(np.ones((C, C), np.float32)), BF16)

    def body(q_ref, f_ref, v_ref, do_ref, st_ref, at_ref, lb_ref, ms_ref, rt_ref,
             dq_ref, df_ref, dv_ref, dlb_ref, g_sc):
        @pl.when(pl.program_id(1) == 0)
        def _():
            g_sc[...] = jnp.zeros_like(g_sc)
            dlb_ref[...] = jnp.zeros_like(dlb_ref)

        msv = ms_ref[...]
        rtv = rt_ref[...]
        lmask, leafm = _hgrn_masks(C, leaf, False)
        lmask_t, leafm_t = _hgrn_masks(C, leaf, True)
        f32 = lambda z: z.astype(F32)

        def head_chunk(g, n):
            hd = slice(g * dk, (g + 1) * dk)
            rows = pl.ds(pl.multiple_of(n * C, C), C)
            lbv = lb_ref[:, hd]
            qr = q_ref[rows, hd].astype(F32)
            tm = _hgrn_chunk_terms(qr, f_ref[rows, hd], lbv, msv, C, leaf)
            v = v_ref[rows, hd]
            dov = do_ref[rows, hd]
            st0 = st_ref[g, n]
            gt = g_sc[g]
            gtb = gt.astype(BF16)
            da = _dot_nt(dov, v)
            da_t = _dot_nt(v, dov)

            dal = jnp.where(leafm, da, 0.0).astype(BF16)
            dal_t = jnp.where(leafm_t, da_t, 0.0).astype(BF16)
            dql = _dot(dal, tm["kl"])
            dkl = _dot(dal_t, tm["ql"])
            dsq = dql * tm["eq"]
            dkk = dkl * tm["ek"]
            xq = f32(tm["ql"]) * dql
            xk = f32(tm["kl"]) * dkl
            for (h, e, selq, qm, km), m, m_t in zip(tm["lev"], lmask, lmask_t):
                dl = (da if m is None else jnp.where(m, da, 0.0)).astype(BF16)
                dl_t = (da_t if m_t is None else jnp.where(m_t, da_t, 0.0)).astype(BF16)
                dqm = _dot(dl, km)
                dkm = _dot(dl_t, qm)
                dsq = dsq + dqm * (e * selq)
                dkk = dkk + dkm * (e * (1.0 - selq))
                xq = xq + f32(qm) * dqm
                xk = xk + f32(km) * dkm
            dqs = _dot(dov, st0.astype(BF16))
            dke = _dot(v, gtb)
            dsq = dsq + dqs * jnp.exp(tm["b"])
            dkk = dkk + dke * jnp.exp(tm["kdec"])
            xq = xq + f32(tm["qs"]) * dqs
            xk = xk + f32(tm["ke"]) * dke
            dvv = _dot(at_ref[g, n], dov) + _dot_nt(tm["ke"], gtb)
            r_end = jnp.sum(f32(gtb) * _dot(v.T, tm["ke"]) + gt * (st0 * tm["e_c"]), axis=0, keepdims=True)
            g_sc[g] = gt * tm["e_c"] + _dot(dov.T, tm["qs"])
            xh, xm, xl = _split3(xq - xk)
            dlf = (_dot(rtv, xh) + _dot(rtv, xm)) + _dot(rtv, xl) + r_end
            dlf_f = dlf / tm["f"]
            dsp = (1.0 - lbv) * (dlf_f - dkk)
            df_ref[rows, hd] = (dsp * (tm["sp"] * tm["sn"])).astype(BF16)
            dq_ref[rows, hd] = (dsq * _dsilu(qr)).astype(BF16)
            dv_ref[rows, hd] = dvv.astype(BF16)
            dlb_ref[:, hd] += jnp.sum(dlf_f * tm["sn"] - dkk * tm["sn"], axis=0, keepdims=True)

        def chunk(nn, carry):
            for g in range(G):
                head_chunk(g, nc - 1 - nn)
            return carry

        lax.fori_loop(0, nc, chunk, 0)

    blk = lambda off: pl.BlockSpec((tb, G * dk), lambda h, i: (nb - 1 - i, off // G + h))
    vec = pl.BlockSpec((1, G * dk), lambda h, i: (0, h))
    return pl.pallas_call(
        body, grid=(H // G, nb),
        in_specs=[blk(0), blk(0), blk(H), blk(0),
                  pl.BlockSpec((G, nc, dk, dk), lambda h, i: (h, nb - 1 - i, 0, 0)),
                  pl.BlockSpec((G, nc, C, C), lambda h, i: (h, nb - 1 - i, 0, 0)), vec,
                  pl.BlockSpec(msum.shape, lambda h, i: (0, 0)), pl.BlockSpec((C, C), lambda h, i: (0, 0))],
        out_specs=[blk(0), blk(0), blk(0), vec],
        out_shape=[SDS((S, H * dk), BF16)] * 3 + [SDS((1, H * dk), F32)],
        scratch_shapes=[pltpu.VMEM((G, dk, dk), F32)],
        compiler_params=_params("parallel", "arbitrary"), name=name,
    )(p1, f1, p1, do, states, a_t, lb, msum, rtri)


def _lb_fwd(logits, *, name):
    W = logits.shape[1]

    def body(l_ref, lb_ref):
        l = l_ref[...]
        m = jnp.max(l, axis=0, keepdims=True)
        e = jnp.exp(l - m)
        p = e / jnp.sum(e, axis=0, keepdims=True)
        lb_ref[...] = (p[0:1] + p[1:2]) - p[0:1]

    return pl.pallas_call(body, out_shape=SDS((1, W), F32), name=name)(logits)


STAT_ROWS = 8


def _stats_reduce(stats_all, logits, *, name):
    W = logits.shape[1]

    def body(s_ref, l_ref, g_ref):
        tot = s_ref[0]
        for d in range(1, N_DEV):
            tot = tot + s_ref[d]
        l = l_ref[...]
        m = jnp.max(l, axis=0, keepdims=True)
        e = jnp.exp(l - m)
        p = e / jnp.sum(e, axis=0, keepdims=True)
        dlb = tot[2:3]
        dl0 = -(p[0:1] * p[1:2]) * dlb
        dl1 = (p[1:2] * (1.0 - p[1:2])) * dlb
        g_ref[0:2] = tot[0:2]
        g_ref[2:3] = dl0
        g_ref[3:4] = dl1
        g_ref[4:7] = tot[3:6]
        g_ref[7:8] = jnp.zeros((1, W), F32)

    return pl.pallas_call(body, out_shape=SDS((STAT_ROWS, W), F32), name=name)(stats_all, logits)


def _adamw(w, m, v, g_parts, *, name, tr=128):
    R, C = w.shape
    ns = len(g_parts)
    n, Rs = g_parts[0].shape[0], g_parts[0].shape[1]
    assert all(p.shape == (n, Rs, C) for p in g_parts) and ns * Rs == R
    tr = min(tr, Rs)
    assert Rs % tr == 0
    nts = Rs // tr
    c1 = 1.0 / (1.0 - ADAM_B1 ** ADAM_STEP)
    c2 = 1.0 / (1.0 - ADAM_B2 ** ADAM_STEP)

    def body(*refs):
        w_ref, m_ref, v_ref = refs[:3]
        g_refs = refs[3:3 + ns]
        go_ref, d_ref, mo_ref, vo_ref = refs[3 + ns:]

        def update(g_ref):
            g = g_ref[0].astype(F32)
            for k in range(1, n):
                g = g + g_ref[k].astype(F32)
            mn = ADAM_B1 * m_ref[...] + (1.0 - ADAM_B1) * g
            vn = ADAM_B2 * v_ref[...] + (1.0 - ADAM_B2) * (g * g)
            d_ref[...] = -ADAM_LR * ((mn * c1) / (jnp.sqrt(vn * c2) + ADAM_EPS) + ADAM_WD * w_ref[...])
            go_ref[...] = g
            mo_ref[...] = mn
            vo_ref[...] = vn

        for s in range(ns):
            if ns == 1:
                update(g_refs[s])
            else:
                pl.when(pl.program_id(0) // nts == s)(functools.partial(update, g_refs[s]))

    def g_map(i, s):
        return (0, jnp.clip(i - s * nts, 0, nts - 1), 0)

    blk = pl.BlockSpec((tr, C), lambda i: (i, 0))
    return pl.pallas_call(
        body, grid=(R // tr,),
        in_specs=[blk, blk, blk] + [pl.BlockSpec((n, tr, C), functools.partial(g_map, s=s)) for s in range(ns)],
        out_specs=[blk] * 4, out_shape=[SDS((R, C), F32)] * 4,
        compiler_params=_params("parallel"), name=name,
    )(w, m, v, *g_parts)


ANY = pl.BlockSpec(memory_space=pl.ANY)
STAGE_BYTES = 2 * 1024 * 1024


def _stage_shape(shape, dtype):
    row_bytes = int(np.prod(shape[1:])) * jnp.dtype(dtype).itemsize
    rows = max(1, min(shape[0], STAGE_BYTES // row_bytes))
    while shape[0] % rows:
        rows -= 1
    return (rows,) + tuple(shape[1:])


def _staged_copy(frm, to, buf, sems):
    rows = buf.shape[0]
    for r0 in range(0, frm.shape[0], rows):
        cp = pltpu.make_async_copy(frm.at[pl.ds(r0, rows)], buf, sems.at[0])
        cp.start()
        cp.wait()
        cp = pltpu.make_async_copy(buf, to.at[pl.ds(r0, rows)], sems.at[1])
        cp.start()
        cp.wait()


def _all_gather(shards, out_shapes, views, *, name):
    n = len(shards)

    def body(*refs):
        ins, outs = refs[:n], refs[n:2 * n]
        send_sems, recv_sems, local_sems = refs[2 * n:2 * n + 3]
        bufs = refs[2 * n + 3:]
        x, y, c = lax.axis_index("x"), lax.axis_index("y"), lax.axis_index("c")
        me, sibling = (x, y, c), (x, y, 1 - c)
        chips = [(1 - x, y), (x, 1 - y), (1 - x, 1 - y)]

        def dev(p):
            return 4 * p[0] + 2 * p[1] + p[2]

        def copy(a, k, block, to, src=None):
            dst = views[a](outs[a], dev(block))
            return pltpu.make_async_remote_copy(
                src_ref=dst if src is None else src, dst_ref=dst,
                send_sem=send_sems.at[a, k], recv_sem=recv_sems.at[a, k],
                device_id=to, device_id_type=MESH)

        first, passed = [], []
        for a in range(n):
            first.append(copy(a, 0, me, sibling, src=ins[a]))
            first += [copy(a, 1 + j, me, (*chip, c), src=ins[a]) for j, chip in enumerate(chips)]
        for cp in first:
            cp.start()
        for a in range(n):
            _staged_copy(ins[a], views[a](outs[a], dev(me)), bufs[a], local_sems)
        for j, chip in enumerate(chips):
            for a in range(n):
                copy(a, 1 + j, (*chip, c), me).wait_recv()
                cp = copy(a, 4 + j, (*chip, c), sibling)
                cp.start()
                passed.append(cp)
        for a in range(n):
            copy(a, 0, sibling, me).wait_recv()
            for j, chip in enumerate(chips):
                copy(a, 4 + j, (*chip, 1 - c), me).wait_recv()
        for cp in first + passed:
            cp.wait_send()

    return pl.pallas_call(
        body, in_specs=[ANY] * n, out_specs=[ANY] * n, out_shape=list(out_shapes),
        scratch_shapes=[pltpu.SemaphoreType.DMA((n, 7)), pltpu.SemaphoreType.DMA((n, 7)),
                        pltpu.SemaphoreType.DMA((2,))]
        + [pltpu.VMEM(_stage_shape(s.shape, s.dtype), s.dtype) for s in shards],
        name=name,
    )(*shards)


HBM = pl.BlockSpec(memory_space=pltpu.HBM)
SEM = pl.BlockSpec(memory_space=pltpu.SEMAPHORE)
EFFECT = pltpu.SideEffectType.DATAFLOW_SIDE_EFFECTING


def _relations(x, y, c):
    for m in range(1, N_DEV):
        yield m, (1 - x if m & 4 else x, 1 - y if m & 2 else y, 1 - c if m & 1 else c)


def _dev_id(p):
    return 4 * p[0] + 2 * p[1] + p[2]


def _send_start(srcs, land_shapes, src_views, dst_views, after, *, name):
    n = len(srcs)

    def body(*refs):
        ins, lands = refs[:n], refs[n:2 * n]
        send_sems, recv_sems, token = refs[2 * n + 1], refs[2 * n + 2], refs[-1]
        x, y, c = lax.axis_index("x"), lax.axis_index("y"), lax.axis_index("c")
        me = _dev_id((x, y, c))
        for m, p in _relations(x, y, c):
            for a in range(n):
                pltpu.make_async_remote_copy(
                    src_ref=src_views[a](ins[a], me, _dev_id(p), m), dst_ref=dst_views[a](lands[a], me, m),
                    send_sem=send_sems.at[a * (N_DEV - 1) + m - 1], recv_sem=recv_sems.at[a * (N_DEV - 1) + m - 1],
                    device_id=p, device_id_type=MESH).start()
        token[...] = jnp.zeros_like(token)

    lands = [pltpu.with_memory_space_constraint(lax.empty(s.shape, s.dtype), pltpu.HBM) for s in land_shapes]
    srcs = [pltpu.with_memory_space_constraint(v, pltpu.HBM) for v in srcs]
    res = pl.pallas_call(
        body, name=name,
        out_shape=[pltpu.SemaphoreType.DMA((n * (N_DEV - 1),)), pltpu.SemaphoreType.DMA((n * (N_DEV - 1),))]
        + [pltpu.HBM(v.shape, v.dtype) for v in srcs] + [pltpu.HBM(s.shape, s.dtype) for s in land_shapes]
        + [SDS((8, 128), F32)],
        in_specs=[HBM] * (2 * n) + [ANY],
        out_specs=[SEM, SEM] + [HBM] * (2 * n) + [pl.BlockSpec(memory_space=pltpu.VMEM)],
        input_output_aliases={i: 2 + i for i in range(2 * n)},
        compiler_params=pltpu.CompilerParams(has_side_effects=EFFECT),
    )(*srcs, *lands, after)
    return res[0], res[1], res[2:2 + n], res[2 + n:2 + 2 * n], res[-1]


def _send_wait(started, src_views, dst_views, own_views, own_shapes, after, *, name):
    send_sems, recv_sems, srcs, lands, _ = started
    n = len(srcs)

    def body(*refs):
        ins, lnd = refs[:n], refs[n:2 * n]
        send_sems, recv_sems = refs[2 * n], refs[2 * n + 1]
        got = refs[2 * n + 3 + n:2 * n + 3 + 2 * n]
        local_sems = refs[2 * n + 3 + 2 * n]
        bufs = refs[2 * n + 4 + 2 * n:]
        x, y, c = lax.axis_index("x"), lax.axis_index("y"), lax.axis_index("c")
        me = _dev_id((x, y, c))
        for m, p in _relations(x, y, c):
            for a in range(n):
                cp = pltpu.make_async_remote_copy(
                    src_ref=src_views[a](ins[a], me, _dev_id(p), m), dst_ref=dst_views[a](lnd[a], me, m),
                    send_sem=send_sems.at[a * (N_DEV - 1) + m - 1], recv_sem=recv_sems.at[a * (N_DEV - 1) + m - 1],
                    device_id=p, device_id_type=MESH)
                cp.wait_send()
                cp.wait_recv()
        for a in range(n):
            frm, to = own_views[a](ins[a], got[a], me)
            _staged_copy(frm, to, bufs[a], local_sems)

    res = pl.pallas_call(
        body, name=name,
        out_shape=[pltpu.HBM(v.shape, v.dtype) for v in srcs] + [pltpu.HBM(v.shape, v.dtype) for v in lands],
        in_specs=[HBM] * (2 * n) + [SEM, SEM, ANY], out_specs=[HBM] * (2 * n),
        input_output_aliases={i: i for i in range(2 * n)},
        scratch_shapes=[pltpu.SemaphoreType.DMA((2,))]
        + [pltpu.VMEM(_stage_shape(s, v.dtype), v.dtype) for s, v in zip(own_shapes, srcs)],
        compiler_params=pltpu.CompilerParams(has_side_effects=EFFECT),
    )(*srcs, *lands, send_sems, recv_sems, after)
    return res[n:]


def kernel(x, norm_gains, fox_w_in, fox_b_f, hgrn_w_in, hgrn_lb_logits, hgrn_onorm, w_out, final_gain, loss_target, m_norm_gains, m_fox_w_in, m_fox_b_f, m_hgrn_w_in, m_hgrn_lb_logits, m_hgrn_onorm, m_w_out, m_final_gain, v_norm_gains, v_fox_w_in, v_fox_b_f, v_hgrn_w_in, v_hgrn_lb_logits, v_hgrn_onorm, v_w_out, v_final_gain):
    _, S, D = x.shape
    H = FOX_HEADS
    W = H * HEAD_DIM
    assert HGRN_HEADS == H and w_out.shape[2] == D
    cf = fox_w_in.shape[2]
    ch = hgrn_w_in.shape[2]
    ro = w_out.shape[1]
    co = hgrn_onorm.shape[1]
    assert N_DEV * cf == 4 * W + H and N_DEV * ch == 4 * W and N_DEV * ro == W and N_DEV * co == W
    x2 = x.reshape(S, D)
    tgt = loss_target.reshape(S, D)

    col = lambda n: (lambda r, i: r.at[:, pl.ds(pl.multiple_of(i * n, n), n)])
    row = lambda n: (lambda r, i: r.at[pl.ds(pl.multiple_of(i * n, n), n), :])

    (wf_g,) = _all_gather([fox_w_in[0].astype(BF16)], [SDS((N_DEV, D, cf), BF16)], [lambda r, p: r.at[p]],
                          name="gather_fox_w_in")

    late_views = [col(ch), row(ro), row(ro), col(co)]
    late = _send_start(
        [hgrn_w_in[0].astype(BF16), w_out[0].astype(BF16), w_out[1].astype(BF16), hgrn_onorm],
        [SDS((D, 4 * W), BF16), SDS((W, D), BF16), SDS((W, D), BF16), SDS((1, W), F32)],
        [lambda r, me, p, m: r] * 4, [lambda r, me, m, v=v: v(r, me) for v in late_views], wf_g[0, 0:8],
        name="gather_later_start")
    ng0 = norm_gains[0:1] + late[4][0:1, 0:1]
    wf = jnp.transpose(wf_g, (1, 0, 2)).reshape(D, N_DEV * cf)
    wf_main = jnp.concatenate([wf[:, :3 * W], wf[:, 3 * W + H:]], axis=1)
    wfl_t = wf[:, 3 * W:3 * W + H].T

    h0 = _rms_fwd(x2, ng0, name="rms0_fwd")
    p0 = _mm_nn([h0], wf_main, BF16, scale_cols=(W, LOG2E * HEAD_DIM ** -0.5), name="fox_in_proj")
    fl_t = _mm_nt_rows(wfl_t, h0, name="fox_forget_proj")
    b_col = fox_b_f.reshape(H, 1)
    kaug = _fox_key_aug(*_fox_gate_fwd(fl_t, b_col, name="fox_gate_fwd"))
    o0, y0, qaug = _fox_fwd(p0, kaug, H=H, name="fox_attn_fwd")
    wh, wo0, wo1, onorm = _send_wait(
        late, [lambda r, me, p, m: r] * 4, [lambda r, me, m, v=v: v(r, me) for v in late_views],
        [lambda src, land, me, v=v: (src, v(land, me)) for v in late_views],
        [(D, ch), (ro, D), (ro, D), (1, co)], y0[0:16], name="gather_later_wait")
    x1 = _mm_nn([y0], wo0, F32, residual=x2, name="fox_out_proj")

    lb = _lb_fwd(hgrn_lb_logits, name="hgrn_lower_bound")
    h1 = _rms_fwd(x1, norm_gains[1:2], name="rms1_fwd")
    p1 = _mm_nn([h1], wh, BF16, b_cols=[(0, W), (2 * W, 4 * W)], name="hgrn_in_proj")
    f1 = _mm_nn([h1], wh, F32, b_cols=[(W, 2 * W)], name="hgrn_forget_proj")
    o1, y1, states, a_t1 = _hgrn_fwd(p1, f1, lb, onorm, H=H, name="hgrn_fwd")
    xo = _mm_nn([y1], wo1, F32, residual=x1, name="hgrn_out_proj")

    dx2b, loss_part, dgf = _loss_head(xo, final_gain.reshape(1, D), tgt, name="loss_head")
    loss = lax.psum(jnp.sum(loss_part), ("x", "y", "c"))

    dy1 = _mm_nn([dx2b], wo1, BF16, b_t=True, name="hgrn_out_proj_dx")
    dwo1 = _mm_tn(y1, [dx2b], BF16, name="hgrn_out_proj_dw")
    do1, dg1, donorm = _hgrn_post_bwd(dy1, o1, p1, onorm, H=H, name="hgrn_post_bwd")
    dq1, df1, di1, dlb = _hgrn_bwd(p1, f1, lb, do1, states, a_t1, H=H, name="hgrn_bwd")
    segs1 = [dq1, df1, di1, dg1]
    dh1 = _mm_nn(segs1, wh, BF16, b_t=True, tn=D, name="hgrn_in_proj_dx")
    dwh = _mm_tn(h1, segs1, BF16, name="hgrn_in_proj_dw")
    part_views = [col(ch), row(ro)]
    slot = lambda r, me, m: r.at[m]
    ex1 = _send_start([dwh, dwo1], [SDS((N_DEV, D, ch), BF16), SDS((N_DEV, ro, D), BF16)],
                      [lambda r, me, p, m, v=v: v(r, p) for v in part_views], [slot] * 2, dwo1[0:8],
                      name="exchange_layer1_start")
    ng1 = norm_gains[1:2] + ex1[4][0:1, 0:1]
    dx1b, dng1 = _rms_bwd(x1, ng1, dh1, dx2b, BF16, name="rms1_bwd")

    dy0 = _mm_nn([dx1b], wo0, BF16, b_t=True, name="fox_out_proj_dx")
    dwo0 = _mm_tn(y0, [dx1b], BF16, name="fox_out_proj_dw")
    do0, dg0, doaug = _fox_post_bwd(dy0, o0, p0, H=H, name="fox_post_bwd")
    dq0, dc_row, dk0, dv0, dc_key = _fox_bwd(p0, kaug, qaug, do0, doaug, H=H, name="fox_attn_bwd")
    dfl_t, dbf = _fox_gate_bwd(dc_row.reshape(H, S), dc_key.reshape(H, S), fl_t, b_col, name="fox_gate_bwd")
    dfl_tb = dfl_t.astype(BF16)
    dwfl_t = _mm_nn([dfl_tb], h0, BF16, name="fox_forget_proj_dw")
    segs0 = [dq0, dk0, dv0, dg0]
    dwf_main = _mm_tn(h0, segs0, BF16, name="fox_in_proj_dw")
    dwf = jnp.concatenate([dwf_main[:, :3 * W], dwfl_t.T, dwf_main[:, 3 * W:]], axis=1)
    dwf_blocks = jnp.transpose(dwf.reshape(D, N_DEV, cf), (1, 0, 2))
    ex0 = _send_start([dwf_blocks, dwo0], [SDS((N_DEV, D, cf), BF16), SDS((N_DEV, ro, D), BF16)],
                      [lambda r, me, p, m: r.at[p], lambda r, me, p, m: row(ro)(r, p)], [slot] * 2, dwo0[0:8],
                      name="exchange_layer0_start")
    wfl_t0 = wfl_t + ex0[4][0:1, 0:1].astype(BF16)
    dh0_f = _mm_nn([dfl_tb.T], wfl_t0, BF16, name="fox_forget_proj_dx")
    dh0 = _mm_nn(segs0, wf_main, BF16, residual=dh0_f, b_t=True, tn=D, name="fox_in_proj_dx")
    grad_x, dng0 = _rms_bwd(x2, norm_gains[0:1], dh0, dx1b, F32, name="rms0_bwd")

    own1 = [lambda src, land, me, v=v: (v(src, me), land.at[0]) for v in part_views]
    rh, ro1 = _send_wait(ex1, [lambda r, me, p, m, v=v: v(r, p) for v in part_views], [slot] * 2, own1,
                         [(D, ch), (ro, D)], dng0, name="exchange_layer1_wait")

    pad = lambda a: jnp.pad(a, ((0, 0), (0, W - a.shape[1])))
    stats = jnp.concatenate([dng0, dng1, dlb, dgf, pad(dbf.reshape(1, H)), donorm,
                             jnp.zeros((2, W), F32)], axis=0)
    assert D == W
    (stats_all,) = _all_gather([stats], [SDS((N_DEV, STAT_ROWS, W), F32)], [lambda r, p: r.at[p]],
                               name="gather_small_grads")
    g_small = _stats_reduce(stats_all, hgrn_lb_logits, name="reduce_small_grads")
    me = 4 * lax.axis_index("x") + 2 * lax.axis_index("y") + lax.axis_index("c")
    g_onorm = lax.dynamic_slice_in_dim(g_small[6:7], me * co, co, axis=1)

    def upd(w, m, v, parts, name):
        shp = w.shape
        r2 = (-1, shp[-1])
        g, d, mn, vn = _adamw(w.reshape(r2), m.reshape(r2), v.reshape(r2), parts, name=name)
        return g.reshape(shp), d.reshape(shp), mn.reshape(shp), vn.reshape(shp)

    res = {
        "norm_gains": upd(norm_gains, m_norm_gains, v_norm_gains, [g_small[None, 0:2]], "adamw_norm_gains"),
        "fox_b_f": upd(fox_b_f, m_fox_b_f, v_fox_b_f, [g_small[None, 5:6, :H]], "adamw_fox_b_f"),
        "hgrn_w_in": upd(hgrn_w_in, m_hgrn_w_in, v_hgrn_w_in, [rh], "adamw_hgrn_w_in"),
        "hgrn_lb_logits": upd(hgrn_lb_logits, m_hgrn_lb_logits, v_hgrn_lb_logits, [g_small[None, 2:4]],
                              "adamw_hgrn_lb_logits"),
        "hgrn_onorm": upd(hgrn_onorm, m_hgrn_onorm, v_hgrn_onorm, [g_onorm[None]], "adamw_hgrn_onorm"),
        "final_gain": upd(final_gain.reshape(1, D), m_final_gain.reshape(1, D), v_final_gain.reshape(1, D),
                          [g_small[None, 4:5]], "adamw_final_gain"),
    }
    rf, ro0 = _send_wait(ex0, [lambda r, me, p, m: r.at[p], lambda r, me, p, m: row(ro)(r, p)], [slot] * 2,
                         [lambda src, land, me: (src.at[me], land.at[0]),
                          lambda src, land, me: (row(ro)(src, me), land.at[0])],
                         [(D, cf), (ro, D)], res["hgrn_w_in"][0][0, 0:8], name="exchange_layer0_wait")
    res["fox_w_in"] = upd(fox_w_in, m_fox_w_in, v_fox_w_in, [rf], "adamw_fox_w_in")
    res["w_out"] = upd(w_out, m_w_out, v_w_out, [ro0, ro1], "adamw_w_out")
    order = ["norm_gains", "fox_w_in", "fox_b_f", "hgrn_w_in", "hgrn_lb_logits", "hgrn_onorm", "w_out", "final_gain"]
    fix = lambda n, a: a.reshape(D) if n == "final_gain" else a
    outs = [loss, grad_x.reshape(1, S, D)]
    for k in range(4):
        outs += [fix(n, res[n][k]) for n in order]
    return tuple(outs)
```

```python
import functools

import numpy as np
import jax
import jax.numpy as jnp
from jax import lax
from jax.experimental import pallas as pl
from jax.experimental.pallas import tpu as pltpu

F32 = jnp.float32
BF16 = jnp.bfloat16
SDS = jax.ShapeDtypeStruct
MESH = pl.DeviceIdType.MESH

EPS = 1e-6
ADAM_LR, ADAM_B1, ADAM_B2, ADAM_EPS, ADAM_WD, ADAM_STEP = 0.001, 0.9, 0.999, 1e-08, 0.01, 10

N_DEV = 8
FOX_HEADS = 16
HGRN_HEADS = 16
HEAD_DIM = 128
HGRN_CHUNK = 128
HGRN_LEAF = 16
HGRN_HEADS_PER_STEP = 16
EXP_CLAMP = 85.0
ATT_BLOCK = 512
ATT_HEADS_PER_STEP = 4
ATT_BWD_HEADS_PER_STEP = 2
POST_HEADS_PER_STEP = 8
NEG = -1e30
LOG2E = 1.4426950408889634
LN2 = 0.6931471805599453

VMEM_LIMIT_V7X = 56 * 1024 * 1024


def _params(*sem):
    return pltpu.CompilerParams(dimension_semantics=sem, vmem_limit_bytes=VMEM_LIMIT_V7X)


def _silu(x):
    return x * jax.nn.sigmoid(x)


def _dsilu(x):
    s = jax.nn.sigmoid(x)
    return s * (1.0 + x * (1.0 - s))


def _dot(a, b):
    return jnp.dot(a, b, preferred_element_type=F32)


def _dot_nt(a, b):
    return lax.dot_general(a, b, (((1,), (1,)), ((), ())), preferred_element_type=F32)


def _dot_tn(a, b):
    return lax.dot_general(a, b, (((0,), (0,)), ((), ())), preferred_element_type=F32)


def _mm_nn(a_list, b, out_dtype, *, name, residual=None, scale_cols=None, b_t=False, b_cols=None,
           tm=1024, tn=1024, tk=2048):
    ns = len(a_list)
    M, Ks = a_list[0].shape
    K, N = (b.shape[1], b.shape[0]) if b_t else b.shape
    if b_cols is None:
        b_cols = [(0, N)]
    else:
        assert not b_t
        N = sum(e - s for s, e in b_cols)
    dot = _dot_nt if b_t else _dot
    assert K == ns * Ks and all(a.shape == (M, Ks) for a in a_list)
    if ns > 1:
        tk = tk // 2
    tm, tn, tk = min(tm, M), min(tn, N), min(tk, Ks)
    assert M % tm == 0 and N % tn == 0 and Ks % tk == 0
    assert scale_cols is None or scale_cols[0] % tn == 0
    assert all(s % tn == 0 and e % tn == 0 for s, e in b_cols)
    nks = Ks // tk
    nk = ns * nks
    has_res = residual is not None

    def body(*refs):
        a_refs, b_ref = refs[:ns], refs[ns]
        res_ref = refs[ns + 1] if has_res else None
        o_ref = refs[ns + 1 + has_res]

        def finish(r):
            if has_res:
                r = r + res_ref[...].astype(F32)
            if scale_cols is not None:
                r = r * jnp.where(pl.program_id(1) < scale_cols[0] // tn, scale_cols[1], 1.0)
            o_ref[...] = r.astype(out_dtype)

        if nk == 1:
            finish(dot(a_refs[0][...], b_ref[...]))
            return
        acc_ref = refs[ns + 2 + has_res]
        k = pl.program_id(2)

        @pl.when(k == 0)
        def _():
            acc_ref[...] = jnp.zeros_like(acc_ref)

        for s in range(ns):
            def step(s=s):
                acc_ref[...] += dot(a_refs[s][...], b_ref[...])

            if ns == 1:
                step()
            else:
                pl.when(k // nks == s)(step)

        @pl.when(k == nk - 1)
        def _():
            finish(acc_ref[...])

    def a_map(i, j, k, s):
        return (i, jnp.clip(k - s * nks, 0, nks - 1))

    in_specs = [pl.BlockSpec((tm, tk), functools.partial(a_map, s=s)) for s in range(ns)]
    def b_col(j):
        src = j + b_cols[0][0] // tn
        for (_, e0), (s1, _) in zip(b_cols[:-1], b_cols[1:]):
            src = src + jnp.where(src >= e0 // tn, (s1 - e0) // tn, 0)
        return src

    if b_t:
        in_specs.append(pl.BlockSpec((tn, tk), lambda i, j, k: (j, k)))
    else:
        in_specs.append(pl.BlockSpec((tk, tn), lambda i, j, k: (k, b_col(j))))
    args = list(a_list) + [b]
    if has_res:
        in_specs.append(pl.BlockSpec((tm, tn), lambda i, j, k: (i, j)))
        args.append(residual)
    return pl.pallas_call(
        body, grid=(M // tm, N // tn, nk), in_specs=in_specs,
        out_specs=pl.BlockSpec((tm, tn), lambda i, j, k: (i, j)),
        out_shape=SDS((M, N), out_dtype),
        scratch_shapes=[] if nk == 1 else [pltpu.VMEM((tm, tn), F32)],
        compiler_params=_params("parallel", "parallel", "arbitrary"), name=name,
    )(*args)


def _mm_tn(a, b_list, out_dtype, *, name, tm=2048, tn=1024, tk=512):
    ns = len(b_list)
    S, M = a.shape
    Ns = b_list[0].shape[1]
    assert all(b.shape == (S, Ns) for b in b_list)
    tm, tn, tk = min(tm, M), min(tn, Ns), min(tk, S)
    assert M % tm == 0 and Ns % tn == 0 and S % tk == 0
    njs = Ns // tn
    nk = S // tk

    def body(*refs):
        a_ref, b_refs, o_ref, acc_ref = refs[0], refs[1:1 + ns], refs[1 + ns], refs[2 + ns]
        j, k = pl.program_id(1), pl.program_id(2)

        @pl.when(k == 0)
        def _():
            acc_ref[...] = jnp.zeros_like(acc_ref)

        for s in range(ns):
            def step(s=s):
                acc_ref[...] += _dot_tn(a_ref[...], b_refs[s][...])

            if ns == 1:
                step()
            else:
                pl.when(j // njs == s)(step)

        @pl.when(k == nk - 1)
        def _():
            o_ref[...] = acc_ref[...].astype(out_dtype)

    def b_map(i, j, k, s):
        return (k, jnp.clip(j - s * njs, 0, njs - 1))

    in_specs = [pl.BlockSpec((tk, tm), lambda i, j, k: (k, i))]
    in_specs += [pl.BlockSpec((tk, tn), functools.partial(b_map, s=s)) for s in range(ns)]
    return pl.pallas_call(
        body, grid=(M // tm, ns * njs, nk), in_specs=in_specs,
        out_specs=pl.BlockSpec((tm, tn), lambda i, j, k: (i, j)),
        out_shape=SDS((M, ns * Ns), out_dtype),
        scratch_shapes=[pltpu.VMEM((tm, tn), F32)],
        compiler_params=_params("parallel", "parallel", "arbitrary"), name=name,
    )(a, *b_list)


def _mm_nt_rows(w_t, h, *, name, tn=1024):
    R, K = w_t.shape
    S = h.shape[0]
    tn = min(tn, S)

    def body(w_ref, h_ref, o_ref):
        o_ref[...] = _dot_nt(w_ref[...], h_ref[...])

    return pl.pallas_call(
        body, grid=(S // tn,),
        in_specs=[pl.BlockSpec((R, K), lambda i: (0, 0)), pl.BlockSpec((tn, K), lambda i: (i, 0))],
        out_specs=pl.BlockSpec((R, tn), lambda i: (0, i)),
        out_shape=SDS((R, S), F32), compiler_params=_params("parallel"), name=name,
    )(w_t, h)


def _rms_fwd(x, gain, *, name, tm=1024):
    S, D = x.shape
    tm = min(tm, S)

    def body(x_ref, g_ref, h_ref):
        xv = x_ref[...]
        r = lax.rsqrt(jnp.mean(xv * xv, axis=-1, keepdims=True) + EPS)
        h_ref[...] = ((xv * r) * g_ref[...]).astype(BF16)

    return pl.pallas_call(
        body, grid=(S // tm,),
        in_specs=[pl.BlockSpec((tm, D), lambda i: (i, 0)), pl.BlockSpec((1, D), lambda i: (0, 0))],
        out_specs=pl.BlockSpec((tm, D), lambda i: (i, 0)),
        out_shape=SDS((S, D), BF16), compiler_params=_params("parallel"), name=name,
    )(x, gain)


def _rms_bwd(x, gain, dh, dres, out_dtype, *, name, tm=512):
    S, D = x.shape
    tm = min(tm, S)

    def body(x_ref, g_ref, dh_ref, dres_ref, dx_ref, dg_ref):
        @pl.when(pl.program_id(0) == 0)
        def _():
            dg_ref[...] = jnp.zeros_like(dg_ref)

        xv = x_ref[...]
        r = lax.rsqrt(jnp.mean(xv * xv, axis=-1, keepdims=True) + EPS)
        xh = xv * r
        dhv = dh_ref[...].astype(F32)
        dg_ref[...] += jnp.sum(dhv * xh, axis=0, keepdims=True)
        dxh = dhv * g_ref[...]
        dx = r * (dxh - xh * jnp.mean(dxh * xh, axis=-1, keepdims=True)) + dres_ref[...].astype(F32)
        dx_ref[...] = dx.astype(out_dtype)

    row = pl.BlockSpec((tm, D), lambda i: (i, 0))
    vec = pl.BlockSpec((1, D), lambda i: (0, 0))
    return pl.pallas_call(
        body, grid=(S // tm,), in_specs=[row, vec, row, row], out_specs=[row, vec],
        out_shape=[SDS((S, D), out_dtype), SDS((1, D), F32)],
        compiler_params=_params("arbitrary"), name=name,
    )(x, gain, dh, dres)


def _loss_head(x, gain, target, *, name, tm=512):
    S, D = x.shape
    tm = min(tm, S)
    assert tm % 8 == 0 and D % 128 == 0

    def body(x_ref, g_ref, t_ref, dxb_ref, loss_ref, dg_ref):
        @pl.when(pl.program_id(0) == 0)
        def _():
            dg_ref[...] = jnp.zeros_like(dg_ref)
            loss_ref[...] = jnp.zeros_like(loss_ref)

        xv = x_ref[...]
        g = g_ref[...]
        r = lax.rsqrt(jnp.mean(xv * xv, axis=-1, keepdims=True) + EPS)
        xh = xv * r
        err = xh * g - t_ref[...]
        e2 = (err * err).reshape(tm // 8, 8, D).sum(axis=0)
        part = e2[:, 0:128]
        for k in range(1, D // 128):
            part = part + e2[:, k * 128:(k + 1) * 128]
        loss_ref[...] += part * (0.5 / D)
        dy = err * (1.0 / D)
        dg_ref[...] += jnp.sum(dy * xh, axis=0, keepdims=True)
        dxh = dy * g
        dx = r * (dxh - xh * jnp.mean(dxh * xh, axis=-1, keepdims=True))
        dxb_ref[...] = dx.astype(BF16)

    row = pl.BlockSpec((tm, D), lambda i: (i, 0))
    vec = pl.BlockSpec((1, D), lambda i: (0, 0))
    return pl.pallas_call(
        body, grid=(S // tm,), in_specs=[row, vec, row],
        out_specs=[row, pl.BlockSpec((8, 128), lambda i: (0, 0)), vec],
        out_shape=[SDS((S, D), BF16), SDS((8, 128), F32), SDS((1, D), F32)],
        compiler_params=_params("arbitrary"), name=name,
    )(x, gain, target)


def _split3(x):
    hi = x.astype(BF16)
    r1 = x - hi.astype(F32)
    mid = r1.astype(BF16)
    lo = (r1 - mid.astype(F32)).astype(BF16)
    return hi, mid, lo


def _split2(x):
    hi = x.astype(BF16)
    lo = (x - hi.astype(F32)).astype(BF16)
    return hi, lo


def _fox_gate_fwd(fl_t, b_col, *, name):
    H, S = fl_t.shape
    L = 128
    tri = jnp.asarray(np.triu(np.ones((L, L), np.float32)), BF16)

    def body(fl_ref, b_ref, tri_ref, hi_ref, mid_ref, lo_ref, carry):
        @pl.when(pl.program_id(0) == 0)
        def _():
            carry[...] = jnp.zeros_like(carry)

        z = fl_ref[...] + b_ref[...]
        lf = jnp.minimum(z, 0.0) - jnp.log(1.0 + jnp.exp(-jnp.abs(z)))
        hi, mid, lo = _split3(lf)
        t = tri_ref[...]
        c = (_dot(hi, t) + _dot(mid, t)) + _dot(lo, t) + carry[...]
        carry[...] = c[:, L - 1:L]
        hi_ref[...], mid_ref[...], lo_ref[...] = _split3(c * (-LOG2E))

    blk = pl.BlockSpec((H, L), lambda i: (0, i))
    return pl.pallas_call(
        body, grid=(S // L,),
        in_specs=[blk, pl.BlockSpec((H, 1), lambda i: (0, 0)), pl.BlockSpec((L, L), lambda i: (0, 0))],
        out_specs=[blk] * 3, out_shape=[SDS((H, S), BF16)] * 3, scratch_shapes=[pltpu.VMEM((H, 1), F32)],
        compiler_params=_params("arbitrary"), name=name,
    )(fl_t, b_col, tri)


def _fox_gate_bwd(dc_row, dc_key, fl_t, b_col, *, name):
    H, S = fl_t.shape
    L = 128
    n = S // L
    tri = jnp.asarray(np.tril(np.ones((L, L), np.float32)), BF16)

    def body(dcr_ref, dck_ref, fl_ref, b_ref, tri_ref, dfl_ref, db_ref, carry):
        @pl.when(pl.program_id(0) == 0)
        def _():
            carry[...] = jnp.zeros_like(carry)
            db_ref[...] = jnp.zeros_like(db_ref)

        hi, mid, lo = _split3(dcr_ref[...] + dck_ref[...])
        t = tri_ref[...]
        dlf = (_dot(hi, t) + _dot(mid, t)) + _dot(lo, t) + carry[...]
        carry[...] = dlf[:, 0:1]
        z = fl_ref[...] + b_ref[...]
        dfl = dlf * jax.nn.sigmoid(-z)
        dfl_ref[...] = dfl
        db_ref[...] += jnp.sum(dfl, axis=1, keepdims=True)

    blk = pl.BlockSpec((H, L), lambda i: (0, n - 1 - i))
    col = pl.BlockSpec((H, 1), lambda i: (0, 0))
    return pl.pallas_call(
        body, grid=(n,), in_specs=[blk, blk, blk, col, pl.BlockSpec((L, L), lambda i: (0, 0))],
        out_specs=[blk, col], out_shape=[SDS((H, S), F32), SDS((H, 1), F32)],
        scratch_shapes=[pltpu.VMEM((H, 1), F32)], compiler_params=_params("arbitrary"), name=name,
    )(dc_row, dc_key, fl_t, b_col, tri)


AUG = HEAD_DIM


def _lane_select(cols, shape):
    lane = lax.broadcasted_iota(jnp.int32, shape, 1)
    out = jnp.zeros(shape, BF16)
    for k, c in reversed(list(enumerate(cols))):
        c = jnp.full(shape, c, BF16) if isinstance(c, (int, float)) else jnp.broadcast_to(c, shape).astype(BF16)
        out = jnp.where(lane == k, c, out)
    return out


def _fox_key_aug(b_hi, b_mid, b_lo):
    H, S = b_hi.shape
    ones = jnp.ones((H, S), BF16)
    ka = jnp.stack([b_hi, b_mid, b_lo, ones, ones, ones], axis=-1)
    ka = jnp.pad(ka, ((0, 0), (0, 0), (0, AUG - 6)))
    return jnp.transpose(ka, (1, 0, 2)).reshape(S, H * AUG)


def _fox_fwd(p0, kaug, *, H, name):
    S = p0.shape[0]
    T = min(ATT_BLOCK, S)
    nq = S // T
    dh = HEAD_DIM
    G = ATT_HEADS_PER_STEP
    assert H % G == 0

    def body(q_ref, k_ref, ka_ref, v_ref, g_ref, o_ref, y_ref, qa_ref, m_sc, acc_sc, p_sc, al_sc):
        i = pl.program_id(1)
        qaug = _lane_select([1.0, 1.0, 1.0], (T, AUG))
        ones = jnp.ones((T, dh), BF16)
        m_sc[...] = jnp.full_like(m_sc, NEG)
        acc_sc[...] = jnp.zeros_like(acc_sc)

        def step(j, before, masked):
            rows = pl.ds(pl.multiple_of(j * T, T), T)
            for g in range(G):
                hd = slice(g * dh, (g + 1) * dh)
                if not masked:
                    prev = pl.ds(pl.multiple_of(before * T, T), T)
                    vp = jnp.concatenate([v_ref[prev, hd], ones], axis=1)
                    acc_sc[g] = jnp.tile(al_sc[g], (1, 2)) * acc_sc[g] + _dot(p_sc[g], vp)
                q = jnp.concatenate([q_ref[:, hd], qaug], axis=1)
                kj = jnp.concatenate([k_ref[rows, hd], ka_ref[rows, hd]], axis=1)
                t = _dot_nt(q, kj)
                if masked:
                    row = lax.broadcasted_iota(jnp.int32, (T, T), 0)
                    col = lax.broadcasted_iota(jnp.int32, (T, T), 1)
                    t = jnp.where(row >= col, t, NEG)
                m_prev = m_sc[g]
                m_new = jnp.maximum(m_prev, jnp.max(t, axis=-1, keepdims=True))
                p_sc[g] = jnp.exp2(t - jnp.tile(m_new, (1, T // 128))).astype(BF16)
                al_sc[g] = jnp.exp2(m_prev - m_new)
                m_sc[g] = m_new

        step(i, None, True)

        def loop_body(j, carry):
            step(j, jnp.where(j == 0, i, j - 1), False)
            return carry

        lax.fori_loop(0, i, loop_body, 0)
        rows = pl.ds(pl.multiple_of(jnp.where(i == 0, 0, i - 1) * T, T), T)
        for g in range(G):
            hd = slice(g * dh, (g + 1) * dh)
            vp = jnp.concatenate([v_ref[rows, hd], ones], axis=1)
            acc = jnp.tile(al_sc[g], (1, 2)) * acc_sc[g] + _dot(p_sc[g], vp)
            l = acc[:, dh:]
            o = acc[:, :dh] / l
            o_ref[:, hd] = o
            y_ref[:, hd] = (o * _silu(g_ref[:, hd].astype(F32))).astype(BF16)
            hi, mid, lo = _split3(-(m_sc[g] + jnp.log2(l)))
            qa_ref[:, hd] = _lane_select([1.0, 1.0, 1.0, hi, mid, lo], (T, AUG))

    blk = lambda off: pl.BlockSpec((T, G * dh), lambda h, i: (i, off // G + h))
    full = lambda off: pl.BlockSpec((S, G * dh), lambda h, i: (0, off // G + h), pipeline_mode=pl.Buffered(1))
    return pl.pallas_call(
        body, grid=(H // G, nq),
        in_specs=[blk(0), full(H), full(0), full(2 * H), blk(3 * H)],
        out_specs=[blk(0), blk(0), blk(0)],
        out_shape=[SDS((S, H * dh), F32), SDS((S, H * dh), BF16), SDS((S, H * AUG), BF16)],
        scratch_shapes=[pltpu.VMEM((G, T, 128), F32), pltpu.VMEM((G, T, 2 * dh), F32),
                        pltpu.VMEM((G, T, T), BF16), pltpu.VMEM((G, T, 128), F32)],
        compiler_params=_params("parallel", "arbitrary"), name=name,
    )(p0, p0, kaug, p0, p0)


def _fox_post_bwd(dy, o, p0, *, H, name, tm=512):
    S = dy.shape[0]
    dh = HEAD_DIM
    tm = min(tm, S)
    G = POST_HEADS_PER_STEP
    assert H % G == 0

    def body(dy_ref, o_ref, g_ref, do_ref, dg_ref, da_ref):
        dyv = dy_ref[...].astype(F32)
        ov = o_ref[...]
        g = g_ref[...].astype(F32)
        do = (dyv * _silu(g)).astype(BF16)
        do_ref[...] = do
        dg_ref[...] = (dyv * ov * _dsilu(g)).astype(BF16)
        prod = do.astype(F32) * ov
        for k in range(G):
            hd = slice(k * dh, (k + 1) * dh)
            delta = jnp.sum(prod[:, hd], axis=-1, keepdims=True)
            hi, mid, lo = _split3(-jnp.broadcast_to(delta, (tm, AUG)))
            da_ref[:, hd] = _lane_select([hi, mid, lo], (tm, AUG))

    blk = pl.BlockSpec((tm, G * dh), lambda h, i: (i, h))
    return pl.pallas_call(
        body, grid=(H // G, S // tm),
        in_specs=[blk, blk, pl.BlockSpec((tm, G * dh), lambda h, i: (i, 3 * H // G + h))],
        out_specs=[blk, blk, blk],
        out_shape=[SDS((S, H * dh), BF16), SDS((S, H * dh), BF16), SDS((S, H * AUG), BF16)],
        compiler_params=_params("parallel", "parallel"), name=name,
    )(dy, o, p0)


def _fox_bwd(p0, kaug, qaug, do, doaug, *, H, name):
    S = p0.shape[0]
    T = min(ATT_BLOCK, S)
    nq = S // T
    dh = HEAD_DIM
    scale = dh ** -0.5
    G = ATT_BWD_HEADS_PER_STEP
    assert H % G == 0

    def body(q_ref, qa_ref, k_ref, ka_ref, v_ref, do_ref, da_ref, dq_ref, rs_ref, dk_ref, dv_ref, dc_ref,
             dq_sc, dk_sc, dv_sc, pt_sc, dst_sc):
        j = pl.program_id(1)
        vaug = _lane_select([1.0, 1.0, 1.0], (T, AUG))
        ones = jnp.ones((T, dh), BF16)

        @pl.when(j == 0)
        def _():
            dq_sc[...] = jnp.zeros_like(dq_sc)

        dk_sc[...] = jnp.zeros_like(dk_sc)
        dv_sc[...] = jnp.zeros_like(dv_sc)

        def apply(prev):
            for g in range(G):
                hd = slice(g * dh, (g + 1) * dh)
                dv_sc[g] += _dot(pt_sc[g], do_ref[prev, hd])
                dk_sc[g] += _dot(dst_sc[g], jnp.concatenate([q_ref[prev, hd], ones], axis=1))
                dq_sc[g, prev] += _dot_tn(dst_sc[g], jnp.concatenate([k_ref[:, hd], ones], axis=1))

        def step(i, masked):
            rows = pl.ds(pl.multiple_of(i * T, T), T)
            if not masked:
                apply(pl.ds(pl.multiple_of((i - 1) * T, T), T))
            for g in range(G):
                hd = slice(g * dh, (g + 1) * dh)
                k = jnp.concatenate([k_ref[:, hd], ka_ref[:, hd]], axis=1)
                v = jnp.concatenate([v_ref[:, hd], vaug], axis=1)
                pt = jnp.exp2(_dot_nt(k, jnp.concatenate([q_ref[rows, hd], qa_ref[rows, hd]], axis=1)))
                if masked:
                    row = lax.broadcasted_iota(jnp.int32, (T, T), 0)
                    col = lax.broadcasted_iota(jnp.int32, (T, T), 1)
                    pt = jnp.where(col >= row, pt, 0.0)
                dst = pt * _dot_nt(v, jnp.concatenate([do_ref[rows, hd], da_ref[rows, hd]], axis=1))
                pt_sc[g] = pt.astype(BF16)
                dst_sc[g] = dst.astype(BF16)

        step(j, True)

        def loop_body(i, carry):
            step(i, False)
            return carry

        lax.fori_loop(j + 1, nq, loop_body, 0)
        apply(pl.ds((nq - 1) * T, T))
        for g in range(G):
            hd = slice(g * dh, (g + 1) * dh)
            dk_ref[:, hd] = (dk_sc[g, :, :dh] * LN2).astype(BF16)
            dv_ref[:, hd] = dv_sc[g].astype(BF16)
            dc_ref[g] = -jnp.transpose(dk_sc[g, :, dh:])[0:1]

        @pl.when(j == nq - 1)
        def _():
            for g in range(G):
                dq_ref[:, g * dh:(g + 1) * dh] = (dq_sc[g, :, :dh] * scale).astype(BF16)
                for i in range(nq):
                    rs_ref[g, :, i * T:(i + 1) * T] = jnp.transpose(dq_sc[g, i * T:(i + 1) * T, dh:])[0:1]

    blk = lambda off: pl.BlockSpec((T, G * dh), lambda h, j: (j, off // G + h))
    full = lambda off: pl.BlockSpec((S, G * dh), lambda h, j: (0, off // G + h))
    once = lambda off: pl.BlockSpec((S, G * dh), lambda h, j: (0, off // G + h), pipeline_mode=pl.Buffered(1))
    rowv = pl.BlockSpec((G, 1, T), lambda h, j: (h, 0, j))
    return pl.pallas_call(
        body, grid=(H // G, nq),
        in_specs=[once(0), once(0), blk(H), blk(0), blk(2 * H), once(0), once(0)],
        out_specs=[full(0), pl.BlockSpec((G, 1, S), lambda h, j: (h, 0, 0)), blk(0), blk(0), rowv],
        out_shape=[SDS((S, H * dh), BF16), SDS((H, 1, S), F32), SDS((S, H * dh), BF16), SDS((S, H * dh), BF16),
                   SDS((H, 1, S), F32)],
        scratch_shapes=[pltpu.VMEM((G, S, 2 * dh), F32), pltpu.VMEM((G, T, 2 * dh), F32), pltpu.VMEM((G, T, dh), F32),
                        pltpu.VMEM((G, T, T), BF16), pltpu.VMEM((G, T, T), BF16)],
        compiler_params=_params("parallel", "arbitrary"), name=name,
    )(p0, qaug, p0, kaug, p0, do, doaug)


def _hgrn_levels(C, leaf):
    levels = []
    h = C // 2
    while h >= leaf:
        levels.append(h)
        h //= 2
    return levels


def _hgrn_sum_matrix(C, leaf):
    t = np.arange(C)[:, None]
    u = np.arange(C)[None, :]
    mats = [(u <= t), (u > t)]
    for h in _hgrn_levels(C, leaf):
        start = (t // (2 * h)) * (2 * h)
        mid = start + h - 1
        second = t > mid
        m = np.where(second, (u > mid) & (u <= t), (u > t) & (u <= mid))
        mats.append(m)
    lstart = (t // leaf) * leaf
    mats.append((u >= lstart) & (u <= t))
    return np.concatenate([m.astype(np.float32) for m in mats], axis=0)


def _hgrn_chunk_terms(qr, fz, lb, msum, C, leaf):
    levels = _hgrn_levels(C, leaf)
    sq = _silu(qr)
    sp = 1.0 / (1.0 + jnp.exp(-fz))
    sn = 1.0 / (1.0 + jnp.exp(fz))
    f = lb + (1.0 - lb) * sp
    lf = jnp.log(f)
    k = (1.0 - lb) * sn
    hi, lo = _split2(lf)
    dsum = _dot(msum, hi) + _dot(msum, lo)
    b = dsum[0:C]
    kdec = dsum[C:2 * C]
    rowi = lax.broadcasted_iota(jnp.int32, (C, 1), 0)
    lev = []
    for n, h in enumerate(levels):
        e = jnp.exp(dsum[(2 + n) * C:(3 + n) * C])
        selq = jnp.where((rowi % (2 * h)) >= h, 1.0, 0.0)
        qm = (sq * e * selq).astype(BF16)
        km = (k * e * (1.0 - selq)).astype(BF16)
        lev.append((h, e, selq, qm, km))
    dleaf = dsum[(2 + len(levels)) * C:(3 + len(levels)) * C]
    eq = jnp.exp(dleaf)
    ek = jnp.exp(jnp.minimum(-dleaf, EXP_CLAMP))
    return dict(sq=sq, sp=sp, sn=sn, f=f, k=k, b=b, kdec=kdec, lev=lev, eq=eq, ek=ek,
                ql=(sq * eq).astype(BF16), kl=(k * ek).astype(BF16),
                qs=(sq * jnp.exp(b)).astype(BF16), ke=(k * jnp.exp(kdec)).astype(BF16),
                e_c=jnp.exp(b[C - 1:C, :]))


def _hgrn_masks(C, leaf, transposed):
    a = lax.broadcasted_iota(jnp.int32, (C, C), 0)
    bb = lax.broadcasted_iota(jnp.int32, (C, C), 1)
    t, s = (bb, a) if transposed else (a, bb)
    lev = [None if 2 * h == C else (t // (2 * h)) == (s // (2 * h)) for h in _hgrn_levels(C, leaf)]
    if leaf == C:
        leafm = s <= t
    else:
        leafm = ((t // leaf) == (s // leaf)) & (s <= t)
    return lev, leafm


def _hgrn_fwd(p1, f1, lb, onorm, *, H, name, tb=512):
    S = p1.shape[0]
    dk = HEAD_DIM
    C = min(HGRN_CHUNK, S)
    leaf = min(HGRN_LEAF, C)
    tb = min(tb, S)
    nc = tb // C
    G = HGRN_HEADS_PER_STEP
    assert H % G == 0
    msum = jnp.asarray(_hgrn_sum_matrix(C, leaf), BF16)

    def body(q_ref, f_ref, v_ref, g_ref, lb_ref, on_ref, ms_ref, o_ref, y_ref, st_ref, at_ref, st_sc):
        @pl.when(pl.program_id(1) == 0)
        def _():
            st_sc[...] = jnp.zeros_like(st_sc)

        msv = ms_ref[...]
        lmask, leafm = _hgrn_masks(C, leaf, False)

        def chunk(n, carry):
            rows = pl.ds(pl.multiple_of(n * C, C), C)
            for g in range(G):
                hd = slice(g * dk, (g + 1) * dk)
                tm = _hgrn_chunk_terms(q_ref[rows, hd].astype(F32), f_ref[rows, hd], lb_ref[:, hd], msv, C, leaf)
                v = v_ref[rows, hd]
                st = st_sc[g]
                st_ref[g, n] = st
                a = jnp.where(leafm, _dot_nt(tm["ql"], tm["kl"]), 0.0)
                for (h, e, selq, qm, km), m in zip(tm["lev"], lmask):
                    al = _dot_nt(qm, km)
                    a = a + (al if m is None else jnp.where(m, al, 0.0))
                at_ref[g, n] = jnp.transpose(a).astype(BF16)
                o = _dot_nt(tm["qs"], st.astype(BF16)) + _dot(a.astype(BF16), v)
                st_sc[g] = st * tm["e_c"] + _dot(v.T, tm["ke"])
                o_ref[rows, hd] = o
                rn = lax.rsqrt(jnp.mean(o * o, axis=-1, keepdims=True) + EPS)
                y = ((o * rn) * on_ref[:, hd]) * _silu(g_ref[rows, hd].astype(F32))
                y_ref[rows, hd] = y.astype(BF16)
            return carry

        lax.fori_loop(0, nc, chunk, 0)

    blk = lambda off: pl.BlockSpec((tb, G * dk), lambda h, i: (i, off // G + h))
    vec = pl.BlockSpec((1, G * dk), lambda h, i: (0, h))
    return pl.pallas_call(
        body, grid=(H // G, S // tb),
        in_specs=[blk(0), blk(0), blk(H), blk(2 * H), vec, vec,
                  pl.BlockSpec(msum.shape, lambda h, i: (0, 0))],
        out_specs=[blk(0), blk(0), pl.BlockSpec((G, nc, dk, dk), lambda h, i: (h, i, 0, 0)),
                   pl.BlockSpec((G, nc, C, C), lambda h, i: (h, i, 0, 0))],
        out_shape=[SDS((S, H * dk), F32), SDS((S, H * dk), BF16), SDS((H, S // C, dk, dk), F32),
                   SDS((H, S // C, C, C), BF16)],
        scratch_shapes=[pltpu.VMEM((G, dk, dk), F32)],
        compiler_params=_params("parallel", "arbitrary"), name=name,
    )(p1, f1, p1, p1, lb, onorm, msum)


def _hgrn_post_bwd(dy, o, p1, onorm, *, H, name, tm=512):
    S = dy.shape[0]
    dk = HEAD_DIM
    tm = min(tm, S)
    G = POST_HEADS_PER_STEP
    assert H % G == 0

    def body(dy_ref, o_ref, g_ref, on_ref, do_ref, dg_ref, don_ref):
        @pl.when(pl.program_id(1) == 0)
        def _():
            don_ref[...] = jnp.zeros_like(don_ref)

        for k in range(G):
            hd = slice(k * dk, (k + 1) * dk)
            dyv = dy_ref[:, hd].astype(F32)
            ov = o_ref[:, hd]
            g = g_ref[:, hd].astype(F32)
            onv = on_ref[:, hd]
            rn = lax.rsqrt(jnp.mean(ov * ov, axis=-1, keepdims=True) + EPS)
            oh = ov * rn
            dn = dyv * _silu(g)
            dg_ref[:, hd] = (dyv * (oh * onv) * _dsilu(g)).astype(BF16)
            don_ref[:, hd] += jnp.sum(dn * oh, axis=0, keepdims=True)
            doh = dn * onv
            do_ref[:, hd] = (rn * (doh - oh * jnp.mean(doh * oh, axis=-1, keepdims=True))).astype(BF16)

    blk = pl.BlockSpec((tm, G * dk), lambda h, i: (i, h))
    vec = pl.BlockSpec((1, G * dk), lambda h, i: (0, h))
    return pl.pallas_call(
        body, grid=(H // G, S // tm),
        in_specs=[blk, blk, pl.BlockSpec((tm, G * dk), lambda h, i: (i, 2 * H // G + h)), vec],
        out_specs=[blk, blk, vec],
        out_shape=[SDS((S, H * dk), BF16), SDS((S, H * dk), BF16), SDS((1, H * dk), F32)],
        compiler_params=_params("parallel", "arbitrary"), name=name,
    )(dy, o, p1, onorm)


def _hgrn_bwd(p1, f1, lb, do, states, a_t, *, H, name, tb=512):
    S = p1.shape[0]
    dk = HEAD_DIM
    C = min(HGRN_CHUNK, S)
    leaf = min(HGRN_LEAF, C)
    tb = min(tb, S)
    nc = tb // C
    nb = S // tb
    G = HGRN_HEADS_PER_STEP
    assert H % G == 0
    msum = jnp.asarray(_hgrn_sum_matrix(C, leaf), BF16)
    rtri = jnp.asarray(np.triu(np.ones((C, C), np.float32)), BF16)

    def body(q_ref, f_ref, v_ref, do_ref, st_ref, at_ref, lb_ref, ms_ref, rt_ref,
             dq_ref, df_ref, dv_ref, dlb_ref, g_sc):
        @pl.when(pl.program_id(1) == 0)
        def _():
            g_sc[...] = jnp.zeros_like(g_sc)
            dlb_ref[...] = jnp.zeros_like(dlb_ref)

        msv = ms_ref[...]
        rtv = rt_ref[...]
        lmask, leafm = _hgrn_masks(C, leaf, False)
        lmask_t, leafm_t = _hgrn_masks(C, leaf, True)
        f32 = lambda z: z.astype(F32)

        def head_chunk(g, n):
            hd = slice(g * dk, (g + 1) * dk)
            rows = pl.ds(pl.multiple_of(n * C, C), C)
            lbv = lb_ref[:, hd]
            qr = q_ref[rows, hd].astype(F32)
            tm = _hgrn_chunk_terms(qr, f_ref[rows, hd], lbv, msv, C, leaf)
            v = v_ref[rows, hd]
            dov = do_ref[rows, hd]
            st0 = st_ref[g, n]
            gt = g_sc[g]
            gtb = gt.astype(BF16)
            da = _dot_nt(dov, v)
            da_t = _dot_nt(v, dov)

            dal = jnp.where(leafm, da, 0.0).astype(BF16)
            dal_t = jnp.where(leafm_t, da_t, 0.0).astype(BF16)
            dql = _dot(dal, tm["kl"])
            dkl = _dot(dal_t, tm["ql"])
            dsq = dql * tm["eq"]
            dkk = dkl * tm["ek"]
            xq = f32(tm["ql"]) * dql
            xk = f32(tm["kl"]) * dkl
            for (h, e, selq, qm, km), m, m_t in zip(tm["lev"], lmask, lmask_t):
                dl = (da if m is None else jnp.where(m, da, 0.0)).astype(BF16)
                dl_t = (da_t if m_t is None else jnp.where(m_t, da_t, 0.0)).astype(BF16)
                dqm = _dot(dl, km)
                dkm = _dot(dl_t, qm)
                dsq = dsq + dqm * (e * selq)
                dkk = dkk + dkm * (e * (1.0 - selq))
                xq = xq + f32(qm) * dqm
                xk = xk + f32(km) * dkm
            dqs = _dot(dov, st0.astype(BF16))
            dke = _dot(v, gtb)
            dsq = dsq + dqs * jnp.exp(tm["b"])
            dkk = dkk + dke * jnp.exp(tm["kdec"])
            xq = xq + f32(tm["qs"]) * dqs
            xk = xk + f32(tm["ke"]) * dke
            dvv = _dot(at_ref[g, n], dov) + _dot_nt(tm["ke"], gtb)
            r_end = jnp.sum(f32(gtb) * _dot(v.T, tm["ke"]) + gt * (st0 * tm["e_c"]), axis=0, keepdims=True)
            g_sc[g] = gt * tm["e_c"] + _dot(dov.T, tm["qs"])
            xh, xm, xl = _split3(xq - xk)
            dlf = (_dot(rtv, xh) + _dot(rtv, xm)) + _dot(rtv, xl) + r_end
            dlf_f = dlf / tm["f"]
            dsp = (1.0 - lbv) * (dlf_f - dkk)
            df_ref[rows, hd] = (dsp * (tm["sp"] * tm["sn"])).astype(BF16)
            dq_ref[rows, hd] = (dsq * _dsilu(qr)).astype(BF16)
            dv_ref[rows, hd] = dvv.astype(BF16)
            dlb_ref[:, hd] += jnp.sum(dlf_f * tm["sn"] - dkk * tm["sn"], axis=0, keepdims=True)

        def chunk(nn, carry):
            for g in range(G):
                head_chunk(g, nc - 1 - nn)
            return carry

        lax.fori_loop(0, nc, chunk, 0)

    blk = lambda off: pl.BlockSpec((tb, G * dk), lambda h, i: (nb - 1 - i, off // G + h))
    vec = pl.BlockSpec((1, G * dk), lambda h, i: (0, h))
    return pl.pallas_call(
        body, grid=(H // G, nb),
        in_specs=[blk(0), blk(0), blk(H), blk(0),
                  pl.BlockSpec((G, nc, dk, dk), lambda h, i: (h, nb - 1 - i, 0, 0)),
                  pl.BlockSpec((G, nc, C, C), lambda h, i: (h, nb - 1 - i, 0, 0)), vec,
                  pl.BlockSpec(msum.shape, lambda h, i: (0, 0)), pl.BlockSpec((C, C), lambda h, i: (0, 0))],
        out_specs=[blk(0), blk(0), blk(0), vec],
        out_shape=[SDS((S, H * dk), BF16)] * 3 + [SDS((1, H * dk), F32)],
        scratch_shapes=[pltpu.VMEM((G, dk, dk), F32)],
        compiler_params=_params("parallel", "arbitrary"), name=name,
    )(p1, f1, p1, do, states, a_t, lb, msum, rtri)


def _lb_fwd(logits, *, name):
    W = logits.shape[1]

    def body(l_ref, lb_ref):
        l = l_ref[...]
        m = jnp.max(l, axis=0, keepdims=True)
        e = jnp.exp(l - m)
        p = e / jnp.sum(e, axis=0, keepdims=True)
        lb_ref[...] = (p[0:1] + p[1:2]) - p[0:1]

    return pl.pallas_call(body, out_shape=SDS((1, W), F32), name=name)(logits)


STAT_ROWS = 8


def _stats_reduce(stats_all, logits, *, name):
    W = logits.shape[1]

    def body(s_ref, l_ref, g_ref):
        tot = s_ref[0]
        for d in range(1, N_DEV):
            tot = tot + s_ref[d]
        l = l_ref[...]
        m = jnp.max(l, axis=0, keepdims=True)
        e = jnp.exp(l - m)
        p = e / jnp.sum(e, axis=0, keepdims=True)
        dlb = tot[2:3]
        dl0 = -(p[0:1] * p[1:2]) * dlb
        dl1 = (p[1:2] * (1.0 - p[1:2])) * dlb
        g_ref[0:2] = tot[0:2]
        g_ref[2:3] = dl0
        g_ref[3:4] = dl1
        g_ref[4:7] = tot[3:6]
        g_ref[7:8] = jnp.zeros((1, W), F32)

    return pl.pallas_call(body, out_shape=SDS((STAT_ROWS, W), F32), name=name)(stats_all, logits)


def _adamw(w, m, v, g_parts, *, name, tr=128):
    R, C = w.shape
    ns = len(g_parts)
    n, Rs = g_parts[0].shape[0], g_parts[0].shape[1]
    assert all(p.shape == (n, Rs, C) for p in g_parts) and ns * Rs == R
    tr = min(tr, Rs)
    assert Rs % tr == 0
    nts = Rs // tr
    c1 = 1.0 / (1.0 - ADAM_B1 ** ADAM_STEP)
    c2 = 1.0 / (1.0 - ADAM_B2 ** ADAM_STEP)

    def body(*refs):
        w_ref, m_ref, v_ref = refs[:3]
        g_refs = refs[3:3 + ns]
        go_ref, d_ref, mo_ref, vo_ref = refs[3 + ns:]

        def update(g_ref):
            g = g_ref[0].astype(F32)
            for k in range(1, n):
                g = g + g_ref[k].astype(F32)
            mn = ADAM_B1 * m_ref[...] + (1.0 - ADAM_B1) * g
            vn = ADAM_B2 * v_ref[...] + (1.0 - ADAM_B2) * (g * g)
            d_ref[...] = -ADAM_LR * ((mn * c1) / (jnp.sqrt(vn * c2) + ADAM_EPS) + ADAM_WD * w_ref[...])
            go_ref[...] = g
            mo_ref[...] = mn
            vo_ref[...] = vn

        for s in range(ns):
            if ns == 1:
                update(g_refs[s])
            else:
                pl.when(pl.program_id(0) // nts == s)(functools.partial(update, g_refs[s]))

    def g_map(i, s):
        return (0, jnp.clip(i - s * nts, 0, nts - 1), 0)

    blk = pl.BlockSpec((tr, C), lambda i: (i, 0))
    return pl.pallas_call(
        body, grid=(R // tr,),
        in_specs=[blk, blk, blk] + [pl.BlockSpec((n, tr, C), functools.partial(g_map, s=s)) for s in range(ns)],
        out_specs=[blk] * 4, out_shape=[SDS((R, C), F32)] * 4,
        compiler_params=_params("parallel"), name=name,
    )(w, m, v, *g_parts)


ANY = pl.BlockSpec(memory_space=pl.ANY)
STAGE_BYTES = 2 * 1024 * 1024


def _stage_shape(shape, dtype):
    row_bytes = int(np.prod(shape[1:])) * jnp.dtype(dtype).itemsize
    rows = max(1, min(shape[0], STAGE_BYTES // row_bytes))
    while shape[0] % rows:
        rows -= 1
    return (rows,) + tuple(shape[1:])


def _staged_copy(frm, to, buf, sems):
    rows = buf.shape[0]
    for r0 in range(0, frm.shape[0], rows):
        cp = pltpu.make_async_copy(frm.at[pl.ds(r0, rows)], buf, sems.at[0])
        cp.start()
        cp.wait()
        cp = pltpu.make_async_copy(buf, to.at[pl.ds(r0, rows)], sems.at[1])
        cp.start()
        cp.wait()


def _all_gather(shards, out_shapes, views, *, name):
    n = len(shards)

    def body(*refs):
        ins, outs = refs[:n], refs[n:2 * n]
        send_sems, recv_sems, local_sems = refs[2 * n:2 * n + 3]
        bufs = refs[2 * n + 3:]
        x, y, c = lax.axis_index("x"), lax.axis_index("y"), lax.axis_index("c")
        me, sibling = (x, y, c), (x, y, 1 - c)
        chips = [(1 - x, y), (x, 1 - y), (1 - x, 1 - y)]

        def dev(p):
            return 4 * p[0] + 2 * p[1] + p[2]

        def copy(a, k, block, to, src=None):
            dst = views[a](outs[a], dev(block))
            return pltpu.make_async_remote_copy(
                src_ref=dst if src is None else src, dst_ref=dst,
                send_sem=send_sems.at[a, k], recv_sem=recv_sems.at[a, k],
                device_id=to, device_id_type=MESH)

        first, passed = [], []
        for a in range(n):
            first.append(copy(a, 0, me, sibling, src=ins[a]))
            first += [copy(a, 1 + j, me, (*chip, c), src=ins[a]) for j, chip in enumerate(chips)]
        for cp in first:
            cp.start()
        for a in range(n):
            _staged_copy(ins[a], views[a](outs[a], dev(me)), bufs[a], local_sems)
        for j, chip in enumerate(chips):
            for a in range(n):
                copy(a, 1 + j, (*chip, c), me).wait_recv()
                cp = copy(a, 4 + j, (*chip, c), sibling)
                cp.start()
                passed.append(cp)
        for a in range(n):
            copy(a, 0, sibling, me).wait_recv()
            for j, chip in enumerate(chips):
                copy(a, 4 + j, (*chip, 1 - c), me).wait_recv()
        for cp in first + passed:
            cp.wait_send()

    return pl.pallas_call(
        body, in_specs=[ANY] * n, out_specs=[ANY] * n, out_shape=list(out_shapes),
        scratch_shapes=[pltpu.SemaphoreType.DMA((n, 7)), pltpu.SemaphoreType.DMA((n, 7)),
                        pltpu.SemaphoreType.DMA((2,))]
        + [pltpu.VMEM(_stage_shape(s.shape, s.dtype), s.dtype) for s in shards],
        name=name,
    )(*shards)


HBM = pl.BlockSpec(memory_space=pltpu.HBM)
SEM = pl.BlockSpec(memory_space=pltpu.SEMAPHORE)
EFFECT = pltpu.SideEffectType.DATAFLOW_SIDE_EFFECTING


def _relations(x, y, c):
    for m in range(1, N_DEV):
        yield m, (1 - x if m & 4 else x, 1 - y if m & 2 else y, 1 - c if m & 1 else c)


def _dev_id(p):
    return 4 * p[0] + 2 * p[1] + p[2]


def _send_start(srcs, land_shapes, src_views, dst_views, after, *, name):
    n = len(srcs)

    def body(*refs):
        ins, lands = refs[:n], refs[n:2 * n]
        send_sems, recv_sems, token = refs[2 * n + 1], refs[2 * n + 2], refs[-1]
        x, y, c = lax.axis_index("x"), lax.axis_index("y"), lax.axis_index("c")
        me = _dev_id((x, y, c))
        for m, p in _relations(x, y, c):
            for a in range(n):
                pltpu.make_async_remote_copy(
                    src_ref=src_views[a](ins[a], me, _dev_id(p), m), dst_ref=dst_views[a](lands[a], me, m),
                    send_sem=send_sems.at[a * (N_DEV - 1) + m - 1], recv_sem=recv_sems.at[a * (N_DEV - 1) + m - 1],
                    device_id=p, device_id_type=MESH).start()
        token[...] = jnp.zeros_like(token)

    lands = [pltpu.with_memory_space_constraint(lax.empty(s.shape, s.dtype), pltpu.HBM) for s in land_shapes]
    srcs = [pltpu.with_memory_space_constraint(v, pltpu.HBM) for v in srcs]
    res = pl.pallas_call(
        body, name=name,
        out_shape=[pltpu.SemaphoreType.DMA((n * (N_DEV - 1),)), pltpu.SemaphoreType.DMA((n * (N_DEV - 1),))]
        + [pltpu.HBM(v.shape, v.dtype) for v in srcs] + [pltpu.HBM(s.shape, s.dtype) for s in land_shapes]
        + [SDS((8, 128), F32)],
        in_specs=[HBM] * (2 * n) + [ANY],
        out_specs=[SEM, SEM] + [HBM] * (2 * n) + [pl.BlockSpec(memory_space=pltpu.VMEM)],
        input_output_aliases={i: 2 + i for i in range(2 * n)},
        compiler_params=pltpu.CompilerParams(has_side_effects=EFFECT),
    )(*srcs, *lands, after)
    return res[0], res[1], res[2:2 + n], res[2 + n:2 + 2 * n], res[-1]


def _send_wait(started, src_views, dst_views, own_views, own_shapes, after, *, name):
    send_sems, recv_sems, srcs, lands, _ = started
    n = len(srcs)

    def body(*refs):
        ins, lnd = refs[:n], refs[n:2 * n]
        send_sems, recv_sems = refs[2 * n], refs[2 * n + 1]
        got = refs[2 * n + 3 + n:2 * n + 3 + 2 * n]
        local_sems = refs[2 * n + 3 + 2 * n]
        bufs = refs[2 * n + 4 + 2 * n:]
        x, y, c = lax.axis_index("x"), lax.axis_index("y"), lax.axis_index("c")
        me = _dev_id((x, y, c))
        for m, p in _relations(x, y, c):
            for a in range(n):
                cp = pltpu.make_async_remote_copy(
                    src_ref=src_views[a](ins[a], me, _dev_id(p), m), dst_ref=dst_views[a](lnd[a], me, m),
                    send_sem=send_sems.at[a * (N_DEV - 1) + m - 1], recv_sem=recv_sems.at[a * (N_DEV - 1) + m - 1],
                    device_id=p, device_id_type=MESH)
                cp.wait_send()
                cp.wait_recv()
        for a in range(n):
            frm, to = own_views[a](ins[a], got[a], me)
            _staged_copy(frm, to, bufs[a], local_sems)

    res = pl.pallas_call(
        body, name=name,
        out_shape=[pltpu.HBM(v.shape, v.dtype) for v in srcs] + [pltpu.HBM(v.shape, v.dtype) for v in lands],
        in_specs=[HBM] * (2 * n) + [SEM, SEM, ANY], out_specs=[HBM] * (2 * n),
        input_output_aliases={i: i for i in range(2 * n)},
        scratch_shapes=[pltpu.SemaphoreType.DMA((2,))]
        + [pltpu.VMEM(_stage_shape(s, v.dtype), v.dtype) for s, v in zip(own_shapes, srcs)],
        compiler_params=pltpu.CompilerParams(has_side_effects=EFFECT),
    )(*srcs, *lands, send_sems, recv_sems, after)
    return res[n:]


def kernel(x, norm_gains, fox_w_in, fox_b_f, hgrn_w_in, hgrn_lb_logits, hgrn_onorm, w_out, final_gain, loss_target, m_norm_gains, m_fox_w_in, m_fox_b_f, m_hgrn_w_in, m_hgrn_lb_logits, m_hgrn_onorm, m_w_out, m_final_gain, v_norm_gains, v_fox_w_in, v_fox_b_f, v_hgrn_w_in, v_hgrn_lb_logits, v_hgrn_onorm, v_w_out, v_final_gain):
    _, S, D = x.shape
    H = FOX_HEADS
    W = H * HEAD_DIM
    assert HGRN_HEADS == H and w_out.shape[2] == D
    cf = fox_w_in.shape[2]
    ch = hgrn_w_in.shape[2]
    ro = w_out.shape[1]
    co = hgrn_onorm.shape[1]
    assert N_DEV * cf == 4 * W + H and N_DEV * ch == 4 * W and N_DEV * ro == W and N_DEV * co == W
    x2 = x.reshape(S, D)
    tgt = loss_target.reshape(S, D)

    col = lambda n: (lambda r, i: r.at[:, pl.ds(pl.multiple_of(i * n, n), n)])
    row = lambda n: (lambda r, i: r.at[pl.ds(pl.multiple_of(i * n, n), n), :])

    (wf_g,) = _all_gather([fox_w_in[0].astype(BF16)], [SDS((N_DEV, D, cf), BF16)], [lambda r, p: r.at[p]],
                          name="gather_fox_w_in")

    late_views = [col(ch), row(ro), row(ro), col(co)]
    late = _send_start(
        [hgrn_w_in[0].astype(BF16), w_out[0].astype(BF16), w_out[1].astype(BF16), hgrn_onorm],
        [SDS((D, 4 * W), BF16), SDS((W, D), BF16), SDS((W, D), BF16), SDS((1, W), F32)],
        [lambda r, me, p, m: r] * 4, [lambda r, me, m, v=v: v(r, me) for v in late_views], wf_g[0, 0:8],
        name="gather_later_start")
    ng0 = norm_gains[0:1] + late[4][0:1, 0:1]
    wf = jnp.transpose(wf_g, (1, 0, 2)).reshape(D, N_DEV * cf)
    wf_main = jnp.concatenate([wf[:, :3 * W], wf[:, 3 * W + H:]], axis=1)
    wfl_t = wf[:, 3 * W:3 * W + H].T

    h0 = _rms_fwd(x2, ng0, name="rms0_fwd")
    p0 = _mm_nn([h0], wf_main, BF16, scale_cols=(W, LOG2E * HEAD_DIM ** -0.5), name="fox_in_proj")
    fl_t = _mm_nt_rows(wfl_t, h0, name="fox_forget_proj")
    b_col = fox_b_f.reshape(H, 1)
    kaug = _fox_key_aug(*_fox_gate_fwd(fl_t, b_col, name="fox_gate_fwd"))
    o0, y0, qaug = _fox_fwd(p0, kaug, H=H, name="fox_attn_fwd")
    wh, wo0, wo1, onorm = _send_wait(
        late, [lambda r, me, p, m: r] * 4, [lambda r, me, m, v=v: v(r, me) for v in late_views],
        [lambda src, land, me, v=v: (src, v(land, me)) for v in late_views],
        [(D, ch), (ro, D), (ro, D), (1, co)], y0[0:16], name="gather_later_wait")
    x1 = _mm_nn([y0], wo0, F32, residual=x2, name="fox_out_proj")

    lb = _lb_fwd(hgrn_lb_logits, name="hgrn_lower_bound")
    h1 = _rms_fwd(x1, norm_gains[1:2], name="rms1_fwd")
    p1 = _mm_nn([h1], wh, BF16, b_cols=[(0, W), (2 * W, 4 * W)], name="hgrn_in_proj")
    f1 = _mm_nn([h1], wh, F32, b_cols=[(W, 2 * W)], name="hgrn_forget_proj")
    o1, y1, states, a_t1 = _hgrn_fwd(p1, f1, lb, onorm, H=H, name="hgrn_fwd")
    xo = _mm_nn([y1], wo1, F32, residual=x1, name="hgrn_out_proj")

    dx2b, loss_part, dgf = _loss_head(xo, final_gain.reshape(1, D), tgt, name="loss_head")
    loss = lax.psum(jnp.sum(loss_part), ("x", "y", "c"))

    dy1 = _mm_nn([dx2b], wo1, BF16, b_t=True, name="hgrn_out_proj_dx")
    dwo1 = _mm_tn(y1, [dx2b], BF16, name="hgrn_out_proj_dw")
    do1, dg1, donorm = _hgrn_post_bwd(dy1, o1, p1, onorm, H=H, name="hgrn_post_bwd")
    dq1, df1, di1, dlb = _hgrn_bwd(p1, f1, lb, do1, states, a_t1, H=H, name="hgrn_bwd")
    segs1 = [dq1, df1, di1, dg1]
    dh1 = _mm_nn(segs1, wh, BF16, b_t=True, tn=D, name="hgrn_in_proj_dx")
    dwh = _mm_tn(h1, segs1, BF16, name="hgrn_in_proj_dw")
    part_views = [col(ch), row(ro)]
    slot = lambda r, me, m: r.at[m]
    ex1 = _send_start([dwh, dwo1], [SDS((N_DEV, D, ch), BF16), SDS((N_DEV, ro, D), BF16)],
                      [lambda r, me, p, m, v=v: v(r, p) for v in part_views], [slot] * 2, dwo1[0:8],
                      name="exchange_layer1_start")
    ng1 = norm_gains[1:2] + ex1[4][0:1, 0:1]
    dx1b, dng1 = _rms_bwd(x1, ng1, dh1, dx2b, BF16, name="rms1_bwd")

    dy0 = _mm_nn([dx1b], wo0, BF16, b_t=True, name="fox_out_proj_dx")
    dwo0 = _mm_tn(y0, [dx1b], BF16, name="fox_out_proj_dw")
    do0, dg0, doaug = _fox_post_bwd(dy0, o0, p0, H=H, name="fox_post_bwd")
    dq0, dc_row, dk0, dv0, dc_key = _fox_bwd(p0, kaug, qaug, do0, doaug, H=H, name="fox_attn_bwd")
    dfl_t, dbf = _fox_gate_bwd(dc_row.reshape(H, S), dc_key.reshape(H, S), fl_t, b_col, name="fox_gate_bwd")
    dfl_tb = dfl_t.astype(BF16)
    dwfl_t = _mm_nn([dfl_tb], h0, BF16, name="fox_forget_proj_dw")
    segs0 = [dq0, dk0, dv0, dg0]
    dwf_main = _mm_tn(h0, segs0, BF16, name="fox_in_proj_dw")
    dwf = jnp.concatenate([dwf_main[:, :3 * W], dwfl_t.T, dwf_main[:, 3 * W:]], axis=1)
    dwf_blocks = jnp.transpose(dwf.reshape(D, N_DEV, cf), (1, 0, 2))
    ex0 = _send_start([dwf_blocks, dwo0], [SDS((N_DEV, D, cf), BF16), SDS((N_DEV, ro, D), BF16)],
                      [lambda r, me, p, m: r.at[p], lambda r, me, p, m: row(ro)(r, p)], [slot] * 2, dwo0[0:8],
                      name="exchange_layer0_start")
    wfl_t0 = wfl_t + ex0[4][0:1, 0:1].astype(BF16)
    dh0_f = _mm_nn([dfl_tb.T], wfl_t0, BF16, name="fox_forget_proj_dx")
    dh0 = _mm_nn(segs0, wf_main, BF16, residual=dh0_f, b_t=True, tn=D, name="fox_in_proj_dx")
    grad_x, dng0 = _rms_bwd(x2, norm_gains[0:1], dh0, dx1b, F32, name="rms0_bwd")

    own1 = [lambda src, land, me, v=v: (v(src, me), land.at[0]) for v in part_views]
    rh, ro1 = _send_wait(ex1, [lambda r, me, p, m, v=v: v(r, p) for v in part_views], [slot] * 2, own1,
                         [(D, ch), (ro, D)], dng0, name="exchange_layer1_wait")

    pad = lambda a: jnp.pad(a, ((0, 0), (0, W - a.shape[1])))
    stats = jnp.concatenate([dng0, dng1, dlb, dgf, pad(dbf.reshape(1, H)), donorm,
                             jnp.zeros((2, W), F32)], axis=0)
    assert D == W
    (stats_all,) = _all_gather([stats], [SDS((N_DEV, STAT_ROWS, W), F32)], [lambda r, p: r.at[p]],
                               name="gather_small_grads")
    g_small = _stats_reduce(stats_all, hgrn_lb_logits, name="reduce_small_grads")
    me = 4 * lax.axis_index("x") + 2 * lax.axis_index("y") + lax.axis_index("c")
    g_onorm = lax.dynamic_slice_in_dim(g_small[6:7], me * co, co, axis=1)

    def upd(w, m, v, parts, name):
        shp = w.shape
        r2 = (-1, shp[-1])
        g, d, mn, vn = _adamw(w.reshape(r2), m.reshape(r2), v.reshape(r2), parts, name=name)
        return g.reshape(shp), d.reshape(shp), mn.reshape(shp), vn.reshape(shp)

    res = {
        "norm_gains": upd(norm_gains, m_norm_gains, v_norm_gains, [g_small[None, 0:2]], "adamw_norm_gains"),
        "fox_b_f": upd(fox_b_f, m_fox_b_f, v_fox_b_f, [g_small[None, 5:6, :H]], "adamw_fox_b_f"),
        "hgrn_w_in": upd(hgrn_w_in, m_hgrn_w_in, v_hgrn_w_in, [rh], "adamw_hgrn_w_in"),
        "hgrn_lb_logits": upd(hgrn_lb_logits, m_hgrn_lb_logits, v_hgrn_lb_logits, [g_small[None, 2:4]],
                              "adamw_hgrn_lb_logits"),
        "hgrn_onorm": upd(hgrn_onorm, m_hgrn_onorm, v_hgrn_onorm, [g_onorm[None]], "adamw_hgrn_onorm"),
        "final_gain": upd(final_gain.reshape(1, D), m_final_gain.reshape(1, D), v_final_gain.reshape(1, D),
                          [g_small[None, 4:5]], "adamw_final_gain"),
    }
    rf, ro0 = _send_wait(ex0, [lambda r, me, p, m: r.at[p], lambda r, me, p, m: row(ro)(r, p)], [slot] * 2,
                         [lambda src, land, me: (src.at[me], land.at[0]),
                          lambda src, land, me: (row(ro)(src, me), land.at[0])],
                         [(D, cf), (ro, D)], res["hgrn_w_in"][0][0, 0:8], name="exchange_layer0_wait")
    res["fox_w_in"] = upd(fox_w_in, m_fox_w_in, v_fox_w_in, [rf], "adamw_fox_w_in")
    res["w_out"] = upd(w_out, m_w_out, v_w_out, [ro0, ro1], "adamw_w_out")
    order = ["norm_gains", "fox_w_in", "fox_b_f", "hgrn_w_in", "hgrn_lb_logits", "hgrn_onorm", "w_out", "final_gain"]
    fix = lambda n, a: a.reshape(D) if n == "final_gain" else a
    outs = [loss, grad_x.reshape(1, S, D)]
    for k in range(4):
        outs += [fix(n, res[n][k]) for n in order]
    return tuple(outs)
```

```python
import functools

import numpy as np
import jax
import jax.numpy as jnp
from jax import lax
from jax.experimental import pallas as pl
from jax.experimental.pallas import tpu as pltpu

F32 = jnp.float32
BF16 = jnp.bfloat16
SDS = jax.ShapeDtypeStruct
MESH = pl.DeviceIdType.MESH

EPS = 1e-6
ADAM_LR, ADAM_B1, ADAM_B2, ADAM_EPS, ADAM_WD, ADAM_STEP = 0.001, 0.9, 0.999, 1e-08, 0.01, 10

N_DEV = 8
FOX_HEADS = 16
HGRN_HEADS = 16
HEAD_DIM = 128
HGRN_CHUNK = 128
HGRN_LEAF = 16
HGRN_HEADS_PER_STEP = 16
EXP_CLAMP = 85.0
ATT_BLOCK = 512
ATT_HEADS_PER_STEP = 4
ATT_BWD_HEADS_PER_STEP = 2
POST_HEADS_PER_STEP = 16
NEG = -1e30
LOG2E = 1.4426950408889634
LN2 = 0.6931471805599453

VMEM_LIMIT_V7X = 56 * 1024 * 1024


def _params(*sem):
    return pltpu.CompilerParams(dimension_semantics=sem, vmem_limit_bytes=VMEM_LIMIT_V7X)


def _silu(x):
    return x * jax.nn.sigmoid(x)


def _dsilu(x):
    s = jax.nn.sigmoid(x)
    return s * (1.0 + x * (1.0 - s))


def _dot(a, b):
    return jnp.dot(a, b, preferred_element_type=F32)


def _dot_nt(a, b):
    return lax.dot_general(a, b, (((1,), (1,)), ((), ())), preferred_element_type=F32)


def _dot_tn(a, b):
    return lax.dot_general(a, b, (((0,), (0,)), ((), ())), preferred_element_type=F32)


def _mm_nn(a_list, b, out_dtype, *, name, residual=None, scale_cols=None, b_t=False, b_cols=None,
           tm=1024, tn=1024, tk=2048):
    ns = len(a_list)
    M, Ks = a_list[0].shape
    K, N = (b.shape[1], b.shape[0]) if b_t else b.shape
    if b_cols is None:
        b_cols = [(0, N)]
    else:
        assert not b_t
        N = sum(e - s for s, e in b_cols)
    dot = _dot_nt if b_t else _dot
    assert K == ns * Ks and all(a.shape == (M, Ks) for a in a_list)
    if ns > 1:
        tk = tk // 2
    tm, tn, tk = min(tm, M), min(tn, N), min(tk, Ks)
    assert M % tm == 0 and N % tn == 0 and Ks % tk == 0
    assert scale_cols is None or scale_cols[0] % tn == 0
    assert all(s % tn == 0 and e % tn == 0 for s, e in b_cols)
    nks = Ks // tk
    nk = ns * nks
    has_res = residual is not None

    def body(*refs):
        a_refs, b_ref = refs[:ns], refs[ns]
        res_ref = refs[ns + 1] if has_res else None
        o_ref = refs[ns + 1 + has_res]

        def finish(r):
            if has_res:
                r = r + res_ref[...].astype(F32)
            if scale_cols is not None:
                r = r * jnp.where(pl.program_id(1) < scale_cols[0] // tn, scale_cols[1], 1.0)
            o_ref[...] = r.astype(out_dtype)

        if nk == 1:
            finish(dot(a_refs[0][...], b_ref[...]))
            return
        acc_ref = refs[ns + 2 + has_res]
        k = pl.program_id(2)

        @pl.when(k == 0)
        def _():
            acc_ref[...] = jnp.zeros_like(acc_ref)

        for s in range(ns):
            def step(s=s):
                acc_ref[...] += dot(a_refs[s][...], b_ref[...])

            if ns == 1:
                step()
            else:
                pl.when(k // nks == s)(step)

        @pl.when(k == nk - 1)
        def _():
            finish(acc_ref[...])

    def a_map(i, j, k, s):
        return (i, jnp.clip(k - s * nks, 0, nks - 1))

    in_specs = [pl.BlockSpec((tm, tk), functools.partial(a_map, s=s)) for s in range(ns)]
    def b_col(j):
        src = j + b_cols[0][0] // tn
        for (_, e0), (s1, _) in zip(b_cols[:-1], b_cols[1:]):
            src = src + jnp.where(src >= e0 // tn, (s1 - e0) // tn, 0)
        return src

    if b_t:
        in_specs.append(pl.BlockSpec((tn, tk), lambda i, j, k: (j, k)))
    else:
        in_specs.append(pl.BlockSpec((tk, tn), lambda i, j, k: (k, b_col(j))))
    args = list(a_list) + [b]
    if has_res:
        in_specs.append(pl.BlockSpec((tm, tn), lambda i, j, k: (i, j)))
        args.append(residual)
    return pl.pallas_call(
        body, grid=(M // tm, N // tn, nk), in_specs=in_specs,
        out_specs=pl.BlockSpec((tm, tn), lambda i, j, k: (i, j)),
        out_shape=SDS((M, N), out_dtype),
        scratch_shapes=[] if nk == 1 else [pltpu.VMEM((tm, tn), F32)],
        compiler_params=_params("parallel", "parallel", "arbitrary"), name=name,
    )(*args)


def _mm_tn(a, b_list, out_dtype, *, name, tm=2048, tn=1024, tk=512):
    ns = len(b_list)
    S, M = a.shape
    Ns = b_list[0].shape[1]
    assert all(b.shape == (S, Ns) for b in b_list)
    tm, tn, tk = min(tm, M), min(tn, Ns), min(tk, S)
    assert M % tm == 0 and Ns % tn == 0 and S % tk == 0
    njs = Ns // tn
    nk = S // tk

    def body(*refs):
        a_ref, b_refs, o_ref, acc_ref = refs[0], refs[1:1 + ns], refs[1 + ns], refs[2 + ns]
        j, k = pl.program_id(1), pl.program_id(2)

        @pl.when(k == 0)
        def _():
            acc_ref[...] = jnp.zeros_like(acc_ref)

        for s in range(ns):
            def step(s=s):
                acc_ref[...] += _dot_tn(a_ref[...], b_refs[s][...])

            if ns == 1:
                step()
            else:
                pl.when(j // njs == s)(step)

        @pl.when(k == nk - 1)
        def _():
            o_ref[...] = acc_ref[...].astype(out_dtype)

    def b_map(i, j, k, s):
        return (k, jnp.clip(j - s * njs, 0, njs - 1))

    in_specs = [pl.BlockSpec((tk, tm), lambda i, j, k: (k, i))]
    in_specs += [pl.BlockSpec((tk, tn), functools.partial(b_map, s=s)) for s in range(ns)]
    return pl.pallas_call(
        body, grid=(M // tm, ns * njs, nk), in_specs=in_specs,
        out_specs=pl.BlockSpec((tm, tn), lambda i, j, k: (i, j)),
        out_shape=SDS((M, ns * Ns), out_dtype),
        scratch_shapes=[pltpu.VMEM((tm, tn), F32)],
        compiler_params=_params("parallel", "parallel", "arbitrary"), name=name,
    )(a, *b_list)


def _mm_nt_rows(w_t, h, *, name, tn=1024):
    R, K = w_t.shape
    S = h.shape[0]
    tn = min(tn, S)

    def body(w_ref, h_ref, o_ref):
        o_ref[...] = _dot_nt(w_ref[...], h_ref[...])

    return pl.pallas_call(
        body, grid=(S // tn,),
        in_specs=[pl.BlockSpec((R, K), lambda i: (0, 0)), pl.BlockSpec((tn, K), lambda i: (i, 0))],
        out_specs=pl.BlockSpec((R, tn), lambda i: (0, i)),
        out_shape=SDS((R, S), F32), compiler_params=_params("parallel"), name=name,
    )(w_t, h)


def _rms_fwd(x, gain, *, name, tm=1024):
    S, D = x.shape
    tm = min(tm, S)

    def body(x_ref, g_ref, h_ref):
        xv = x_ref[...]
        r = lax.rsqrt(jnp.mean(xv * xv, axis=-1, keepdims=True) + EPS)
        h_ref[...] = ((xv * r) * g_ref[...]).astype(BF16)

    return pl.pallas_call(
        body, grid=(S // tm,),
        in_specs=[pl.BlockSpec((tm, D), lambda i: (i, 0)), pl.BlockSpec((1, D), lambda i: (0, 0))],
        out_specs=pl.BlockSpec((tm, D), lambda i: (i, 0)),
        out_shape=SDS((S, D), BF16), compiler_params=_params("parallel"), name=name,
    )(x, gain)


def _rms_bwd(x, gain, dh, dres, out_dtype, *, name, tm=512):
    S, D = x.shape
    tm = min(tm, S)

    def body(x_ref, g_ref, dh_ref, dres_ref, dx_ref, dg_ref):
        @pl.when(pl.program_id(0) == 0)
        def _():
            dg_ref[...] = jnp.zeros_like(dg_ref)

        xv = x_ref[...]
        r = lax.rsqrt(jnp.mean(xv * xv, axis=-1, keepdims=True) + EPS)
        xh = xv * r
        dhv = dh_ref[...].astype(F32)
        dg_ref[...] += jnp.sum(dhv * xh, axis=0, keepdims=True)
        dxh = dhv * g_ref[...]
        dx = r * (dxh - xh * jnp.mean(dxh * xh, axis=-1, keepdims=True)) + dres_ref[...].astype(F32)
        dx_ref[...] = dx.astype(out_dtype)

    row = pl.BlockSpec((tm, D), lambda i: (i, 0))
    vec = pl.BlockSpec((1, D), lambda i: (0, 0))
    return pl.pallas_call(
        body, grid=(S // tm,), in_specs=[row, vec, row, row], out_specs=[row, vec],
        out_shape=[SDS((S, D), out_dtype), SDS((1, D), F32)],
        compiler_params=_params("arbitrary"), name=name,
    )(x, gain, dh, dres)


def _loss_head(x, gain, target, *, name, tm=512):
    S, D = x.shape
    tm = min(tm, S)
    assert tm % 8 == 0 and D % 128 == 0

    def body(x_ref, g_ref, t_ref, dxb_ref, loss_ref, dg_ref):
        @pl.when(pl.program_id(0) == 0)
        def _():
            dg_ref[...] = jnp.zeros_like(dg_ref)
            loss_ref[...] = jnp.zeros_like(loss_ref)

        xv = x_ref[...]
        g = g_ref[...]
        r = lax.rsqrt(jnp.mean(xv * xv, axis=-1, keepdims=True) + EPS)
        xh = xv * r
        err = xh * g - t_ref[...]
        e2 = (err * err).reshape(tm // 8, 8, D).sum(axis=0)
        part = e2[:, 0:128]
        for k in range(1, D // 128):
            part = part + e2[:, k * 128:(k + 1) * 128]
        loss_ref[...] += part * (0.5 / D)
        dy = err * (1.0 / D)
        dg_ref[...] += jnp.sum(dy * xh, axis=0, keepdims=True)
        dxh = dy * g
        dx = r * (dxh - xh * jnp.mean(dxh * xh, axis=-1, keepdims=True))
        dxb_ref[...] = dx.astype(BF16)

    row = pl.BlockSpec((tm, D), lambda i: (i, 0))
    vec = pl.BlockSpec((1, D), lambda i: (0, 0))
    return pl.pallas_call(
        body, grid=(S // tm,), in_specs=[row, vec, row],
        out_specs=[row, pl.BlockSpec((8, 128), lambda i: (0, 0)), vec],
        out_shape=[SDS((S, D), BF16), SDS((8, 128), F32), SDS((1, D), F32)],
        compiler_params=_params("arbitrary"), name=name,
    )(x, gain, target)


def _split3(x):
    hi = x.astype(BF16)
    r1 = x - hi.astype(F32)
    mid = r1.astype(BF16)
    lo = (r1 - mid.astype(F32)).astype(BF16)
    return hi, mid, lo


def _split2(x):
    hi = x.astype(BF16)
    lo = (x - hi.astype(F32)).astype(BF16)
    return hi, lo


def _fox_gate_fwd(fl_t, b_col, *, name):
    H, S = fl_t.shape
    L = 128
    tri = jnp.asarray(np.triu(np.ones((L, L), np.float32)), BF16)

    def body(fl_ref, b_ref, tri_ref, hi_ref, mid_ref, lo_ref, carry):
        @pl.when(pl.program_id(0) == 0)
        def _():
            carry[...] = jnp.zeros_like(carry)

        z = fl_ref[...] + b_ref[...]
        lf = jnp.minimum(z, 0.0) - jnp.log(1.0 + jnp.exp(-jnp.abs(z)))
        hi, mid, lo = _split3(lf)
        t = tri_ref[...]
        c = (_dot(hi, t) + _dot(mid, t)) + _dot(lo, t) + carry[...]
        carry[...] = c[:, L - 1:L]
        hi_ref[...], mid_ref[...], lo_ref[...] = _split3(c * (-LOG2E))

    blk = pl.BlockSpec((H, L), lambda i: (0, i))
    return pl.pallas_call(
        body, grid=(S // L,),
        in_specs=[blk, pl.BlockSpec((H, 1), lambda i: (0, 0)), pl.BlockSpec((L, L), lambda i: (0, 0))],
        out_specs=[blk] * 3, out_shape=[SDS((H, S), BF16)] * 3, scratch_shapes=[pltpu.VMEM((H, 1), F32)],
        compiler_params=_params("arbitrary"), name=name,
    )(fl_t, b_col, tri)


def _fox_gate_bwd(dc_row, dc_key, fl_t, b_col, *, name):
    H, S = fl_t.shape
    L = 128
    n = S // L
    tri = jnp.asarray(np.tril(np.ones((L, L), np.float32)), BF16)

    def body(dcr_ref, dck_ref, fl_ref, b_ref, tri_ref, dfl_ref, db_ref, carry):
        @pl.when(pl.program_id(0) == 0)
        def _():
            carry[...] = jnp.zeros_like(carry)
            db_ref[...] = jnp.zeros_like(db_ref)

        hi, mid, lo = _split3(dcr_ref[...] + dck_ref[...])
        t = tri_ref[...]
        dlf = (_dot(hi, t) + _dot(mid, t)) + _dot(lo, t) + carry[...]
        carry[...] = dlf[:, 0:1]
        z = fl_ref[...] + b_ref[...]
        dfl = dlf * jax.nn.sigmoid(-z)
        dfl_ref[...] = dfl
        db_ref[...] += jnp.sum(dfl, axis=1, keepdims=True)

    blk = pl.BlockSpec((H, L), lambda i: (0, n - 1 - i))
    col = pl.BlockSpec((H, 1), lambda i: (0, 0))
    return pl.pallas_call(
        body, grid=(n,), in_specs=[blk, blk, blk, col, pl.BlockSpec((L, L), lambda i: (0, 0))],
        out_specs=[blk, col], out_shape=[SDS((H, S), F32), SDS((H, 1), F32)],
        scratch_shapes=[pltpu.VMEM((H, 1), F32)], compiler_params=_params("arbitrary"), name=name,
    )(dc_row, dc_key, fl_t, b_col, tri)


AUG = HEAD_DIM


def _lane_select(cols, shape):
    lane = lax.broadcasted_iota(jnp.int32, shape, 1)
    out = jnp.zeros(shape, BF16)
    for k, c in reversed(list(enumerate(cols))):
        c = jnp.full(shape, c, BF16) if isinstance(c, (int, float)) else jnp.broadcast_to(c, shape).astype(BF16)
        out = jnp.where(lane == k, c, out)
    return out


def _fox_key_aug(b_hi, b_mid, b_lo):
    H, S = b_hi.shape
    ones = jnp.ones((H, S), BF16)
    ka = jnp.stack([b_hi, b_mid, b_lo, ones, ones, ones], axis=-1)
    ka = jnp.pad(ka, ((0, 0), (0, 0), (0, AUG - 6)))
    return jnp.transpose(ka, (1, 0, 2)).reshape(S, H * AUG)


def _fox_fwd(p0, kaug, *, H, name):
    S = p0.shape[0]
    T = min(ATT_BLOCK, S)
    nq = S // T
    dh = HEAD_DIM
    G = ATT_HEADS_PER_STEP
    assert H % G == 0

    def body(q_ref, k_ref, ka_ref, v_ref, g_ref, o_ref, y_ref, qa_ref, m_sc, acc_sc, p_sc, al_sc):
        i = pl.program_id(1)
        qaug = _lane_select([1.0, 1.0, 1.0], (T, AUG))
        ones = jnp.ones((T, dh), BF16)
        m_sc[...] = jnp.full_like(m_sc, NEG)
        acc_sc[...] = jnp.zeros_like(acc_sc)

        def step(j, before, masked):
            rows = pl.ds(pl.multiple_of(j * T, T), T)
            for g in range(G):
                hd = slice(g * dh, (g + 1) * dh)
                if not masked:
                    prev = pl.ds(pl.multiple_of(before * T, T), T)
                    vp = jnp.concatenate([v_ref[prev, hd], ones], axis=1)
                    acc_sc[g] = jnp.tile(al_sc[g], (1, 2)) * acc_sc[g] + _dot(p_sc[g], vp)
                q = jnp.concatenate([q_ref[:, hd], qaug], axis=1)
                kj = jnp.concatenate([k_ref[rows, hd], ka_ref[rows, hd]], axis=1)
                t = _dot_nt(q, kj)
                if masked:
                    row = lax.broadcasted_iota(jnp.int32, (T, T), 0)
                    col = lax.broadcasted_iota(jnp.int32, (T, T), 1)
                    t = jnp.where(row >= col, t, NEG)
                m_prev = m_sc[g]
                m_new = jnp.maximum(m_prev, jnp.max(t, axis=-1, keepdims=True))
                p_sc[g] = jnp.exp2(t - jnp.tile(m_new, (1, T // 128))).astype(BF16)
                al_sc[g] = jnp.exp2(m_prev - m_new)
                m_sc[g] = m_new

        step(i, None, True)

        def loop_body(j, carry):
            step(j, jnp.where(j == 0, i, j - 1), False)
            return carry

        lax.fori_loop(0, i, loop_body, 0)
        rows = pl.ds(pl.multiple_of(jnp.where(i == 0, 0, i - 1) * T, T), T)
        for g in range(G):
            hd = slice(g * dh, (g + 1) * dh)
            vp = jnp.concatenate([v_ref[rows, hd], ones], axis=1)
            acc = jnp.tile(al_sc[g], (1, 2)) * acc_sc[g] + _dot(p_sc[g], vp)
            l = acc[:, dh:]
            o = acc[:, :dh] / l
            o_ref[:, hd] = o
            y_ref[:, hd] = (o * _silu(g_ref[:, hd].astype(F32))).astype(BF16)
            hi, mid, lo = _split3(-(m_sc[g] + jnp.log2(l)))
            qa_ref[:, hd] = _lane_select([1.0, 1.0, 1.0, hi, mid, lo], (T, AUG))

    blk = lambda off: pl.BlockSpec((T, G * dh), lambda h, i: (i, off // G + h))
    full = lambda off: pl.BlockSpec((S, G * dh), lambda h, i: (0, off // G + h), pipeline_mode=pl.Buffered(1))
    return pl.pallas_call(
        body, grid=(H // G, nq),
        in_specs=[blk(0), full(H), full(0), full(2 * H), blk(3 * H)],
        out_specs=[blk(0), blk(0), blk(0)],
        out_shape=[SDS((S, H * dh), F32), SDS((S, H * dh), BF16), SDS((S, H * AUG), BF16)],
        scratch_shapes=[pltpu.VMEM((G, T, 128), F32), pltpu.VMEM((G, T, 2 * dh), F32),
                        pltpu.VMEM((G, T, T), BF16), pltpu.VMEM((G, T, 128), F32)],
        compiler_params=_params("parallel", "arbitrary"), name=name,
    )(p0, p0, kaug, p0, p0)


def _fox_post_bwd(dy, o, p0, *, H, name, tm=512):
    S = dy.shape[0]
    dh = HEAD_DIM
    tm = min(tm, S)
    G = POST_HEADS_PER_STEP
    assert H % G == 0

    def body(dy_ref, o_ref, g_ref, do_ref, dg_ref, da_ref):
        dyv = dy_ref[...].astype(F32)
        ov = o_ref[...]
        g = g_ref[...].astype(F32)
        do = (dyv * _silu(g)).astype(BF16)
        do_ref[...] = do
        dg_ref[...] = (dyv * ov * _dsilu(g)).astype(BF16)
        prod = do.astype(F32) * ov
        for k in range(G):
            hd = slice(k * dh, (k + 1) * dh)
            delta = jnp.sum(prod[:, hd], axis=-1, keepdims=True)
            hi, mid, lo = _split3(-jnp.broadcast_to(delta, (tm, AUG)))
            da_ref[:, hd] = _lane_select([hi, mid, lo], (tm, AUG))

    blk = pl.BlockSpec((tm, G * dh), lambda h, i: (i, h))
    return pl.pallas_call(
        body, grid=(H // G, S // tm),
        in_specs=[blk, blk, pl.BlockSpec((tm, G * dh), lambda h, i: (i, 3 * H // G + h))],
        out_specs=[blk, blk, blk],
        out_shape=[SDS((S, H * dh), BF16), SDS((S, H * dh), BF16), SDS((S, H * AUG), BF16)],
        compiler_params=_params("parallel", "parallel"), name=name,
    )(dy, o, p0)


def _fox_bwd(p0, kaug, qaug, do, doaug, *, H, name):
    S = p0.shape[0]
    T = min(ATT_BLOCK, S)
    nq = S // T
    dh = HEAD_DIM
    scale = dh ** -0.5
    G = ATT_BWD_HEADS_PER_STEP
    assert H % G == 0

    def body(q_ref, qa_ref, k_ref, ka_ref, v_ref, do_ref, da_ref, dq_ref, rs_ref, dk_ref, dv_ref, dc_ref,
             dq_sc, dk_sc, dv_sc, pt_sc, dst_sc):
        j = pl.program_id(1)
        vaug = _lane_select([1.0, 1.0, 1.0], (T, AUG))
        ones = jnp.ones((T, dh), BF16)

        @pl.when(j == 0)
        def _():
            dq_sc[...] = jnp.zeros_like(dq_sc)

        dk_sc[...] = jnp.zeros_like(dk_sc)
        dv_sc[...] = jnp.zeros_like(dv_sc)

        def apply(prev):
            for g in range(G):
                hd = slice(g * dh, (g + 1) * dh)
                dv_sc[g] += _dot(pt_sc[g], do_ref[prev, hd])
                dk_sc[g] += _dot(dst_sc[g], jnp.concatenate([q_ref[prev, hd], ones], axis=1))
                dq_sc[g, prev] += _dot_tn(dst_sc[g], jnp.concatenate([k_ref[:, hd], ones], axis=1))

        def step(i, masked):
            rows = pl.ds(pl.multiple_of(i * T, T), T)
            if not masked:
                apply(pl.ds(pl.multiple_of((i - 1) * T, T), T))
            for g in range(G):
                hd = slice(g * dh, (g + 1) * dh)
                k = jnp.concatenate([k_ref[:, hd], ka_ref[:, hd]], axis=1)
                v = jnp.concatenate([v_ref[:, hd], vaug], axis=1)
                pt = jnp.exp2(_dot_nt(k, jnp.concatenate([q_ref[rows, hd], qa_ref[rows, hd]], axis=1)))
                if masked:
                    row = lax.broadcasted_iota(jnp.int32, (T, T), 0)
                    col = lax.broadcasted_iota(jnp.int32, (T, T), 1)
                    pt = jnp.where(col >= row, pt, 0.0)
                dst = pt * _dot_nt(v, jnp.concatenate([do_ref[rows, hd], da_ref[rows, hd]], axis=1))
                pt_sc[g] = pt.astype(BF16)
                dst_sc[g] = dst.astype(BF16)

        step(j, True)

        def loop_body(i, carry):
            step(i, False)
            return carry

        lax.fori_loop(j + 1, nq, loop_body, 0)
        apply(pl.ds((nq - 1) * T, T))
        for g in range(G):
            hd = slice(g * dh, (g + 1) * dh)
            dk_ref[:, hd] = (dk_sc[g, :, :dh] * LN2).astype(BF16)
            dv_ref[:, hd] = dv_sc[g].astype(BF16)
            dc_ref[g] = -jnp.transpose(dk_sc[g, :, dh:])[0:1]

        @pl.when(j == nq - 1)
        def _():
            for g in range(G):
                dq_ref[:, g * dh:(g + 1) * dh] = (dq_sc[g, :, :dh] * scale).astype(BF16)
                for i in range(nq):
                    rs_ref[g, :, i * T:(i + 1) * T] = jnp.transpose(dq_sc[g, i * T:(i + 1) * T, dh:])[0:1]

    blk = lambda off: pl.BlockSpec((T, G * dh), lambda h, j: (j, off // G + h))
    full = lambda off: pl.BlockSpec((S, G * dh), lambda h, j: (0, off // G + h))
    once = lambda off: pl.BlockSpec((S, G * dh), lambda h, j: (0, off // G + h), pipeline_mode=pl.Buffered(1))
    rowv = pl.BlockSpec((G, 1, T), lambda h, j: (h, 0, j))
    return pl.pallas_call(
        body, grid=(H // G, nq),
        in_specs=[once(0), once(0), blk(H), blk(0), blk(2 * H), once(0), once(0)],
        out_specs=[full(0), pl.BlockSpec((G, 1, S), lambda h, j: (h, 0, 0)), blk(0), blk(0), rowv],
        out_shape=[SDS((S, H * dh), BF16), SDS((H, 1, S), F32), SDS((S, H * dh), BF16), SDS((S, H * dh), BF16),
                   SDS((H, 1, S), F32)],
        scratch_shapes=[pltpu.VMEM((G, S, 2 * dh), F32), pltpu.VMEM((G, T, 2 * dh), F32), pltpu.VMEM((G, T, dh), F32),
                        pltpu.VMEM((G, T, T), BF16), pltpu.VMEM((G, T, T), BF16)],
        compiler_params=_params("parallel", "arbitrary"), name=name,
    )(p0, qaug, p0, kaug, p0, do, doaug)


def _hgrn_levels(C, leaf):
    levels = []
    h = C // 2
    while h >= leaf:
        levels.append(h)
        h //= 2
    return levels


def _hgrn_sum_matrix(C, leaf):
    t = np.arange(C)[:, None]
    u = np.arange(C)[None, :]
    mats = [(u <= t), (u > t)]
    for h in _hgrn_levels(C, leaf):
        start = (t // (2 * h)) * (2 * h)
        mid = start + h - 1
        second = t > mid
        m = np.where(second, (u > mid) & (u <= t), (u > t) & (u <= mid))
        mats.append(m)
    lstart = (t // leaf) * leaf
    mats.append((u >= lstart) & (u <= t))
    return np.concatenate([m.astype(np.float32) for m in mats], axis=0)


def _hgrn_chunk_terms(qr, fz, lb, msum, C, leaf):
    levels = _hgrn_levels(C, leaf)
    sq = _silu(qr)
    sp = 1.0 / (1.0 + jnp.exp(-fz))
    sn = 1.0 / (1.0 + jnp.exp(fz))
    f = lb + (1.0 - lb) * sp
    lf = jnp.log(f)
    k = (1.0 - lb) * sn
    hi, lo = _split2(lf)
    dsum = _dot(msum, hi) + _dot(msum, lo)
    b = dsum[0:C]
    kdec = dsum[C:2 * C]
    rowi = lax.broadcasted_iota(jnp.int32, (C, 1), 0)
    lev = []
    for n, h in enumerate(levels):
        e = jnp.exp(dsum[(2 + n) * C:(3 + n) * C])
        selq = jnp.where((rowi % (2 * h)) >= h, 1.0, 0.0)
        qm = (sq * e * selq).astype(BF16)
        km = (k * e * (1.0 - selq)).astype(BF16)
        lev.append((h, e, selq, qm, km))
    dleaf = dsum[(2 + len(levels)) * C:(3 + len(levels)) * C]
    eq = jnp.exp(dleaf)
    ek = jnp.exp(jnp.minimum(-dleaf, EXP_CLAMP))
    return dict(sq=sq, sp=sp, sn=sn, f=f, k=k, b=b, kdec=kdec, lev=lev, eq=eq, ek=ek,
                ql=(sq * eq).astype(BF16), kl=(k * ek).astype(BF16),
                qs=(sq * jnp.exp(b)).astype(BF16), ke=(k * jnp.exp(kdec)).astype(BF16),
                e_c=jnp.exp(b[C - 1:C, :]))


def _hgrn_masks(C, leaf, transposed):
    a = lax.broadcasted_iota(jnp.int32, (C, C), 0)
    bb = lax.broadcasted_iota(jnp.int32, (C, C), 1)
    t, s = (bb, a) if transposed else (a, bb)
    lev = [None if 2 * h == C else (t // (2 * h)) == (s // (2 * h)) for h in _hgrn_levels(C, leaf)]
    if leaf == C:
        leafm = s <= t
    else:
        leafm = ((t // leaf) == (s // leaf)) & (s <= t)
    return lev, leafm


def _hgrn_fwd(p1, f1, lb, onorm, *, H, name, tb=512):
    S = p1.shape[0]
    dk = HEAD_DIM
    C = min(HGRN_CHUNK, S)
    leaf = min(HGRN_LEAF, C)
    tb = min(tb, S)
    nc = tb // C
    G = HGRN_HEADS_PER_STEP
    assert H % G == 0
    msum = jnp.asarray(_hgrn_sum_matrix(C, leaf), BF16)

    def body(q_ref, f_ref, v_ref, g_ref, lb_ref, on_ref, ms_ref, o_ref, y_ref, st_ref, at_ref, st_sc):
        @pl.when(pl.program_id(1) == 0)
        def _():
            st_sc[...] = jnp.zeros_like(st_sc)

        msv = ms_ref[...]
        lmask, leafm = _hgrn_masks(C, leaf, False)

        def chunk(n, carry):
            rows = pl.ds(pl.multiple_of(n * C, C), C)
            for g in range(G):
                hd = slice(g * dk, (g + 1) * dk)
                tm = _hgrn_chunk_terms(q_ref[rows, hd].astype(F32), f_ref[rows, hd], lb_ref[:, hd], msv, C, leaf)
                v = v_ref[rows, hd]
                st = st_sc[g]
                st_ref[g, n] = st
                a = jnp.where(leafm, _dot_nt(tm["ql"], tm["kl"]), 0.0)
                for (h, e, selq, qm, km), m in zip(tm["lev"], lmask):
                    al = _dot_nt(qm, km)
                    a = a + (al if m is None else jnp.where(m, al, 0.0))
                at_ref[g, n] = jnp.transpose(a).astype(BF16)
                o = _dot_nt(tm["qs"], st.astype(BF16)) + _dot(a.astype(BF16), v)
                st_sc[g] = st * tm["e_c"] + _dot(v.T, tm["ke"])
                o_ref[rows, hd] = o
                rn = lax.rsqrt(jnp.mean(o * o, axis=-1, keepdims=True) + EPS)
                y = ((o * rn) * on_ref[:, hd]) * _silu(g_ref[rows, hd].astype(F32))
                y_ref[rows, hd] = y.astype(BF16)
            return carry

        lax.fori_loop(0, nc, chunk, 0)

    blk = lambda off: pl.BlockSpec((tb, G * dk), lambda h, i: (i, off // G + h))
    vec = pl.BlockSpec((1, G * dk), lambda h, i: (0, h))
    return pl.pallas_call(
        body, grid=(H // G, S // tb),
        in_specs=[blk(0), blk(0), blk(H), blk(2 * H), vec, vec,
                  pl.BlockSpec(msum.shape, lambda h, i: (0, 0))],
        out_specs=[blk(0), blk(0), pl.BlockSpec((G, nc, dk, dk), lambda h, i: (h, i, 0, 0)),
                   pl.BlockSpec((G, nc, C, C), lambda h, i: (h, i, 0, 0))],
        out_shape=[SDS((S, H * dk), F32), SDS((S, H * dk), BF16), SDS((H, S // C, dk, dk), F32),
                   SDS((H, S // C, C, C), BF16)],
        scratch_shapes=[pltpu.VMEM((G, dk, dk), F32)],
        compiler_params=_params("parallel", "arbitrary"), name=name,
    )(p1, f1, p1, p1, lb, onorm, msum)


def _hgrn_post_bwd(dy, o, p1, onorm, *, H, name, tm=512):
    S = dy.shape[0]
    dk = HEAD_DIM
    tm = min(tm, S)
    G = POST_HEADS_PER_STEP
    assert H % G == 0

    def body(dy_ref, o_ref, g_ref, on_ref, do_ref, dg_ref, don_ref):
        @pl.when(pl.program_id(1) == 0)
        def _():
            don_ref[...] = jnp.zeros_like(don_ref)

        for k in range(G):
            hd = slice(k * dk, (k + 1) * dk)
            dyv = dy_ref[:, hd].astype(F32)
            ov = o_ref[:, hd]
            g = g_ref[:, hd].astype(F32)
            onv = on_ref[:, hd]
            rn = lax.rsqrt(jnp.mean(ov * ov, axis=-1, keepdims=True) + EPS)
            oh = ov * rn
            dn = dyv * _silu(g)
            dg_ref[:, hd] = (dyv * (oh * onv) * _dsilu(g)).astype(BF16)
            don_ref[:, hd] += jnp.sum(dn * oh, axis=0, keepdims=True)
            doh = dn * onv
            do_ref[:, hd] = (rn * (doh - oh * jnp.mean(doh * oh, axis=-1, keepdims=True))).astype(BF16)

    blk = pl.BlockSpec((tm, G * dk), lambda h, i: (i, h))
    vec = pl.BlockSpec((1, G * dk), lambda h, i: (0, h))
    return pl.pallas_call(
        body, grid=(H // G, S // tm),
        in_specs=[blk, blk, pl.BlockSpec((tm, G * dk), lambda h, i: (i, 2 * H // G + h)), vec],
        out_specs=[blk, blk, vec],
        out_shape=[SDS((S, H * dk), BF16), SDS((S, H * dk), BF16), SDS((1, H * dk), F32)],
        compiler_params=_params("parallel", "arbitrary"), name=name,
    )(dy, o, p1, onorm)


def _hgrn_bwd(p1, f1, lb, do, states, a_t, *, H, name, tb=512):
    S = p1.shape[0]
    dk = HEAD_DIM
    C = min(HGRN_CHUNK, S)
    leaf = min(HGRN_LEAF, C)
    tb = min(tb, S)
    nc = tb // C
    nb = S // tb
    G = HGRN_HEADS_PER_STEP
    assert H % G == 0
    msum = jnp.asarray(_hgrn_sum_matrix(C, leaf), BF16)
    rtri = jnp.asarray(np.triu(np.ones((C, C), np.float32)), BF16)

    def body(q_ref, f_ref, v_ref, do_ref, st_ref, at_ref, lb_ref, ms_ref, rt_ref,
             dq_ref, df_ref, dv_ref, dlb_ref, g_sc):
        @pl.when(pl.program_id(1) == 0)
        def _():
            g_sc[...] = jnp.zeros_like(g_sc)
            dlb_ref[...] = jnp.zeros_like(dlb_ref)

        msv = ms_ref[...]
        rtv = rt_ref[...]
        lmask, leafm = _hgrn_masks(C, leaf, False)
        lmask_t, leafm_t = _hgrn_masks(C, leaf, True)
        f32 = lambda z: z.astype(F32)

        def head_chunk(g, n):
            hd = slice(g * dk, (g + 1) * dk)
            rows = pl.ds(pl.multiple_of(n * C, C), C)
            lbv = lb_ref[:, hd]
            qr = q_ref[rows, hd].astype(F32)
            tm = _hgrn_chunk_terms(qr, f_ref[rows, hd], lbv, msv, C, leaf)
            v = v_ref[rows, hd]
            dov = do_ref[rows, hd]
            st0 = st_ref[g, n]
            gt = g_sc[g]
            gtb = gt.astype(BF16)
            da = _dot_nt(dov, v)
            da_t = _dot_nt(v, dov)

            dal = jnp.where(leafm, da, 0.0).astype(BF16)
            dal_t = jnp.where(leafm_t, da_t, 0.0).astype(BF16)
            dql = _dot(dal, tm["kl"])
            dkl = _dot(dal_t, tm["ql"])
            dsq = dql * tm["eq"]
            dkk = dkl * tm["ek"]
            xq = f32(tm["ql"]) * dql
            xk = f32(tm["kl"]) * dkl
            for (h, e, selq, qm, km), m, m_t in zip(tm["lev"], lmask, lmask_t):
                dl = (da if m is None else jnp.where(m, da, 0.0)).astype(BF16)
                dl_t = (da_t if m_t is None else jnp.where(m_t, da_t, 0.0)).astype(BF16)
                dqm = _dot(dl, km)
                dkm = _dot(dl_t, qm)
                dsq = dsq + dqm * (e * selq)
                dkk = dkk + dkm * (e * (1.0 - selq))
                xq = xq + f32(qm) * dqm
                xk = xk + f32(km) * dkm
            dqs = _dot(dov, st0.astype(BF16))
            dke = _dot(v, gtb)
            dsq = dsq + dqs * jnp.exp(tm["b"])
            dkk = dkk + dke * jnp.exp(tm["kdec"])
            xq = xq + f32(tm["qs"]) * dqs
            xk = xk + f32(tm["ke"]) * dke
            dvv = _dot(at_ref[g, n], dov) + _dot_nt(tm["ke"], gtb)
            r_end = jnp.sum(f32(gtb) * _dot(v.T, tm["ke"]) + gt * (st0 * tm["e_c"]), axis=0, keepdims=True)
            g_sc[g] = gt * tm["e_c"] + _dot(dov.T, tm["qs"])
            xh, xm, xl = _split3(xq - xk)
            dlf = (_dot(rtv, xh) + _dot(rtv, xm)) + _dot(rtv, xl) + r_end
            dlf_f = dlf / tm["f"]
            dsp = (1.0 - lbv) * (dlf_f - dkk)
            df_ref[rows, hd] = (dsp * (tm["sp"] * tm["sn"])).astype(BF16)
            dq_ref[rows, hd] = (dsq * _dsilu(qr)).astype(BF16)
            dv_ref[rows, hd] = dvv.astype(BF16)
            dlb_ref[:, hd] += jnp.sum(dlf_f * tm["sn"] - dkk * tm["sn"], axis=0, keepdims=True)

        def chunk(nn, carry):
            for g in range(G):
                head_chunk(g, nc - 1 - nn)
            return carry

        lax.fori_loop(0, nc, chunk, 0)

    blk = lambda off: pl.BlockSpec((tb, G * dk), lambda h, i: (nb - 1 - i, off // G + h))
    vec = pl.BlockSpec((1, G * dk), lambda h, i: (0, h))
    return pl.pallas_call(
        body, grid=(H // G, nb),
        in_specs=[blk(0), blk(0), blk(H), blk(0),
                  pl.BlockSpec((G, nc, dk, dk), lambda h, i: (h, nb - 1 - i, 0, 0)),
                  pl.BlockSpec((G, nc, C, C), lambda h, i: (h, nb - 1 - i, 0, 0)), vec,
                  pl.BlockSpec(msum.shape, lambda h, i: (0, 0)), pl.BlockSpec((C, C), lambda h, i: (0, 0))],
        out_specs=[blk(0), blk(0), blk(0), vec],
        out_shape=[SDS((S, H * dk), BF16)] * 3 + [SDS((1, H * dk), F32)],
        scratch_shapes=[pltpu.VMEM((G, dk, dk), F32)],
        compiler_params=_params("parallel", "arbitrary"), name=name,
    )(p1, f1, p1, do, states, a_t, lb, msum, rtri)


def _lb_fwd(logits, *, name):
    W = logits.shape[1]

    def body(l_ref, lb_ref):
        l = l_ref[...]
        m = jnp.max(l, axis=0, keepdims=True)
        e = jnp.exp(l - m)
        p = e / jnp.sum(e, axis=0, keepdims=True)
        lb_ref[...] = (p[0:1] + p[1:2]) - p[0:1]

    return pl.pallas_call(body, out_shape=SDS((1, W), F32), name=name)(logits)


STAT_ROWS = 8


def _stats_reduce(stats_all, logits, *, name):
    W = logits.shape[1]

    def body(s_ref, l_ref, g_ref):
        tot = s_ref[0]
        for d in range(1, N_DEV):
            tot = tot + s_ref[d]
        l = l_ref[...]
        m = jnp.max(l, axis=0, keepdims=True)
        e = jnp.exp(l - m)
        p = e / jnp.sum(e, axis=0, keepdims=True)
        dlb = tot[2:3]
        dl0 = -(p[0:1] * p[1:2]) * dlb
        dl1 = (p[1:2] * (1.0 - p[1:2])) * dlb
        g_ref[0:2] = tot[0:2]
        g_ref[2:3] = dl0
        g_ref[3:4] = dl1
        g_ref[4:7] = tot[3:6]
        g_ref[7:8] = jnp.zeros((1, W), F32)

    return pl.pallas_call(body, out_shape=SDS((STAT_ROWS, W), F32), name=name)(stats_all, logits)


def _adamw(w, m, v, g_parts, *, name, tr=128):
    R, C = w.shape
    ns = len(g_parts)
    n, Rs = g_parts[0].shape[0], g_parts[0].shape[1]
    assert all(p.shape == (n, Rs, C) for p in g_parts) and ns * Rs == R
    tr = min(tr, Rs)
    assert Rs % tr == 0
    nts = Rs // tr
    c1 = 1.0 / (1.0 - ADAM_B1 ** ADAM_STEP)
    c2 = 1.0 / (1.0 - ADAM_B2 ** ADAM_STEP)

    def body(*refs):
        w_ref, m_ref, v_ref = refs[:3]
        g_refs = refs[3:3 + ns]
        go_ref, d_ref, mo_ref, vo_ref = refs[3 + ns:]

        def update(g_ref):
            g = g_ref[0].astype(F32)
            for k in range(1, n):
                g = g + g_ref[k].astype(F32)
            mn = ADAM_B1 * m_ref[...] + (1.0 - ADAM_B1) * g
            vn = ADAM_B2 * v_ref[...] + (1.0 - ADAM_B2) * (g * g)
            d_ref[...] = -ADAM_LR * ((mn * c1) / (jnp.sqrt(vn * c2) + ADAM_EPS) + ADAM_WD * w_ref[...])
            go_ref[...] = g
            mo_ref[...] = mn
            vo_ref[...] = vn

        for s in range(ns):
            if ns == 1:
                update(g_refs[s])
            else:
                pl.when(pl.program_id(0) // nts == s)(functools.partial(update, g_refs[s]))

    def g_map(i, s):
        return (0, jnp.clip(i - s * nts, 0, nts - 1), 0)

    blk = pl.BlockSpec((tr, C), lambda i: (i, 0))
    return pl.pallas_call(
        body, grid=(R // tr,),
        in_specs=[blk, blk, blk] + [pl.BlockSpec((n, tr, C), functools.partial(g_map, s=s)) for s in range(ns)],
        out_specs=[blk] * 4, out_shape=[SDS((R, C), F32)] * 4,
        compiler_params=_params("parallel"), name=name,
    )(w, m, v, *g_parts)


ANY = pl.BlockSpec(memory_space=pl.ANY)
STAGE_BYTES = 2 * 1024 * 1024


def _stage_shape(shape, dtype):
    row_bytes = int(np.prod(shape[1:])) * jnp.dtype(dtype).itemsize
    rows = max(1, min(shape[0], STAGE_BYTES // row_bytes))
    while shape[0] % rows:
        rows -= 1
    return (rows,) + tuple(shape[1:])


def _staged_copy(frm, to, buf, sems):
    rows = buf.shape[0]
    for r0 in range(0, frm.shape[0], rows):
        cp = pltpu.make_async_copy(frm.at[pl.ds(r0, rows)], buf, sems.at[0])
        cp.start()
        cp.wait()
        cp = pltpu.make_async_copy(buf, to.at[pl.ds(r0, rows)], sems.at[1])
        cp.start()
        cp.wait()


def _all_gather(shards, out_shapes, views, *, name):
    n = len(shards)

    def body(*refs):
        ins, outs = refs[:n], refs[n:2 * n]
        send_sems, recv_sems, local_sems = refs[2 * n:2 * n + 3]
        bufs = refs[2 * n + 3:]
        x, y, c = lax.axis_index("x"), lax.axis_index("y"), lax.axis_index("c")
        me, sibling = (x, y, c), (x, y, 1 - c)
        chips = [(1 - x, y), (x, 1 - y), (1 - x, 1 - y)]

        def dev(p):
            return 4 * p[0] + 2 * p[1] + p[2]

        def copy(a, k, block, to, src=None):
            dst = views[a](outs[a], dev(block))
            return pltpu.make_async_remote_copy(
                src_ref=dst if src is None else src, dst_ref=dst,
                send_sem=send_sems.at[a, k], recv_sem=recv_sems.at[a, k],
                device_id=to, device_id_type=MESH)

        first, passed = [], []
        for a in range(n):
            first.append(copy(a, 0, me, sibling, src=ins[a]))
            first += [copy(a, 1 + j, me, (*chip, c), src=ins[a]) for j, chip in enumerate(chips)]
        for cp in first:
            cp.start()
        for a in range(n):
            _staged_copy(ins[a], views[a](outs[a], dev(me)), bufs[a], local_sems)
        for j, chip in enumerate(chips):
            for a in range(n):
                copy(a, 1 + j, (*chip, c), me).wait_recv()
                cp = copy(a, 4 + j, (*chip, c), sibling)
                cp.start()
                passed.append(cp)
        for a in range(n):
            copy(a, 0, sibling, me).wait_recv()
            for j, chip in enumerate(chips):
                copy(a, 4 + j, (*chip, 1 - c), me).wait_recv()
        for cp in first + passed:
            cp.wait_send()

    return pl.pallas_call(
        body, in_specs=[ANY] * n, out_specs=[ANY] * n, out_shape=list(out_shapes),
        scratch_shapes=[pltpu.SemaphoreType.DMA((n, 7)), pltpu.SemaphoreType.DMA((n, 7)),
                        pltpu.SemaphoreType.DMA((2,))]
        + [pltpu.VMEM(_stage_shape(s.shape, s.dtype), s.dtype) for s in shards],
        name=name,
    )(*shards)


HBM = pl.BlockSpec(memory_space=pltpu.HBM)
SEM = pl.BlockSpec(memory_space=pltpu.SEMAPHORE)
EFFECT = pltpu.SideEffectType.DATAFLOW_SIDE_EFFECTING


def _relations(x, y, c):
    for m in range(1, N_DEV):
        yield m, (1 - x if m & 4 else x, 1 - y if m & 2 else y, 1 - c if m & 1 else c)


def _dev_id(p):
    return 4 * p[0] + 2 * p[1] + p[2]


def _send_start(srcs, land_shapes, src_views, dst_views, after, *, name):
    n = len(srcs)

    def body(*refs):
        ins, lands = refs[:n], refs[n:2 * n]
        send_sems, recv_sems, token = refs[2 * n + 1], refs[2 * n + 2], refs[-1]
        x, y, c = lax.axis_index("x"), lax.axis_index("y"), lax.axis_index("c")
        me = _dev_id((x, y, c))
        for m, p in _relations(x, y, c):
            for a in range(n):
                pltpu.make_async_remote_copy(
                    src_ref=src_views[a](ins[a], me, _dev_id(p), m), dst_ref=dst_views[a](lands[a], me, m),
                    send_sem=send_sems.at[a * (N_DEV - 1) + m - 1], recv_sem=recv_sems.at[a * (N_DEV - 1) + m - 1],
                    device_id=p, device_id_type=MESH).start()
        token[...] = jnp.zeros_like(token)

    lands = [pltpu.with_memory_space_constraint(lax.empty(s.shape, s.dtype), pltpu.HBM) for s in land_shapes]
    srcs = [pltpu.with_memory_space_constraint(v, pltpu.HBM) for v in srcs]
    res = pl.pallas_call(
        body, name=name,
        out_shape=[pltpu.SemaphoreType.DMA((n * (N_DEV - 1),)), pltpu.SemaphoreType.DMA((n * (N_DEV - 1),))]
        + [pltpu.HBM(v.shape, v.dtype) for v in srcs] + [pltpu.HBM(s.shape, s.dtype) for s in land_shapes]
        + [SDS((8, 128), F32)],
        in_specs=[HBM] * (2 * n) + [ANY],
        out_specs=[SEM, SEM] + [HBM] * (2 * n) + [pl.BlockSpec(memory_space=pltpu.VMEM)],
        input_output_aliases={i: 2 + i for i in range(2 * n)},
        compiler_params=pltpu.CompilerParams(has_side_effects=EFFECT),
    )(*srcs, *lands, after)
    return res[0], res[1], res[2:2 + n], res[2 + n:2 + 2 * n], res[-1]


def _send_wait(started, src_views, dst_views, own_views, own_shapes, after, *, name):
    send_sems, recv_sems, srcs, lands, _ = started
    n = len(srcs)

    def body(*refs):
        ins, lnd = refs[:n], refs[n:2 * n]
        send_sems, recv_sems = refs[2 * n], refs[2 * n + 1]
        got = refs[2 * n + 3 + n:2 * n + 3 + 2 * n]
        local_sems = refs[2 * n + 3 + 2 * n]
        bufs = refs[2 * n + 4 + 2 * n:]
        x, y, c = lax.axis_index("x"), lax.axis_index("y"), lax.axis_index("c")
        me = _dev_id((x, y, c))
        for m, p in _relations(x, y, c):
            for a in range(n):
                cp = pltpu.make_async_remote_copy(
                    src_ref=src_views[a](ins[a], me, _dev_id(p), m), dst_ref=dst_views[a](lnd[a], me, m),
                    send_sem=send_sems.at[a * (N_DEV - 1) + m - 1], recv_sem=recv_sems.at[a * (N_DEV - 1) + m - 1],
                    device_id=p, device_id_type=MESH)
                cp.wait_send()
                cp.wait_recv()
        for a in range(n):
            frm, to = own_views[a](ins[a], got[a], me)
            _staged_copy(frm, to, bufs[a], local_sems)

    res = pl.pallas_call(
        body, name=name,
        out_shape=[pltpu.HBM(v.shape, v.dtype) for v in srcs] + [pltpu.HBM(v.shape, v.dtype) for v in lands],
        in_specs=[HBM] * (2 * n) + [SEM, SEM, ANY], out_specs=[HBM] * (2 * n),
        input_output_aliases={i: i for i in range(2 * n)},
        scratch_shapes=[pltpu.SemaphoreType.DMA((2,))]
        + [pltpu.VMEM(_stage_shape(s, v.dtype), v.dtype) for s, v in zip(own_shapes, srcs)],
        compiler_params=pltpu.CompilerParams(has_side_effects=EFFECT),
    )(*srcs, *lands, send_sems, recv_sems, after)
    return res[n:]


def kernel(x, norm_gains, fox_w_in, fox_b_f, hgrn_w_in, hgrn_lb_logits, hgrn_onorm, w_out, final_gain, loss_target, m_norm_gains, m_fox_w_in, m_fox_b_f, m_hgrn_w_in, m_hgrn_lb_logits, m_hgrn_onorm, m_w_out, m_final_gain, v_norm_gains, v_fox_w_in, v_fox_b_f, v_hgrn_w_in, v_hgrn_lb_logits, v_hgrn_onorm, v_w_out, v_final_gain):
    _, S, D = x.shape
    H = FOX_HEADS
    W = H * HEAD_DIM
    assert HGRN_HEADS == H and w_out.shape[2] == D
    cf = fox_w_in.shape[2]
    ch = hgrn_w_in.shape[2]
    ro = w_out.shape[1]
    co = hgrn_onorm.shape[1]
    assert N_DEV * cf == 4 * W + H and N_DEV * ch == 4 * W and N_DEV * ro == W and N_DEV * co == W
    x2 = x.reshape(S, D)
    tgt = loss_target.reshape(S, D)

    col = lambda n: (lambda r, i: r.at[:, pl.ds(pl.multiple_of(i * n, n), n)])
    row = lambda n: (lambda r, i: r.at[pl.ds(pl.multiple_of(i * n, n), n), :])

    (wf_g,) = _all_gather([fox_w_in[0].astype(BF16)], [SDS((N_DEV, D, cf), BF16)], [lambda r, p: r.at[p]],
                          name="gather_fox_w_in")

    late_views = [col(ch), row(ro), row(ro), col(co)]
    late = _send_start(
        [hgrn_w_in[0].astype(BF16), w_out[0].astype(BF16), w_out[1].astype(BF16), hgrn_onorm],
        [SDS((D, 4 * W), BF16), SDS((W, D), BF16), SDS((W, D), BF16), SDS((1, W), F32)],
        [lambda r, me, p, m: r] * 4, [lambda r, me, m, v=v: v(r, me) for v in late_views], wf_g[0, 0:8],
        name="gather_later_start")
    ng0 = norm_gains[0:1] + late[4][0:1, 0:1]
    wf = jnp.transpose(wf_g, (1, 0, 2)).reshape(D, N_DEV * cf)
    wf_main = jnp.concatenate([wf[:, :3 * W], wf[:, 3 * W + H:]], axis=1)
    wfl_t = wf[:, 3 * W:3 * W + H].T

    h0 = _rms_fwd(x2, ng0, name="rms0_fwd")
    p0 = _mm_nn([h0], wf_main, BF16, scale_cols=(W, LOG2E * HEAD_DIM ** -0.5), name="fox_in_proj")
    fl_t = _mm_nt_rows(wfl_t, h0, name="fox_forget_proj")
    b_col = fox_b_f.reshape(H, 1)
    kaug = _fox_key_aug(*_fox_gate_fwd(fl_t, b_col, name="fox_gate_fwd"))
    o0, y0, qaug = _fox_fwd(p0, kaug, H=H, name="fox_attn_fwd")
    wh, wo0, wo1, onorm = _send_wait(
        late, [lambda r, me, p, m: r] * 4, [lambda r, me, m, v=v: v(r, me) for v in late_views],
        [lambda src, land, me, v=v: (src, v(land, me)) for v in late_views],
        [(D, ch), (ro, D), (ro, D), (1, co)], y0[0:16], name="gather_later_wait")
    x1 = _mm_nn([y0], wo0, F32, residual=x2, name="fox_out_proj")

    lb = _lb_fwd(hgrn_lb_logits, name="hgrn_lower_bound")
    h1 = _rms_fwd(x1, norm_gains[1:2], name="rms1_fwd")
    p1 = _mm_nn([h1], wh, BF16, b_cols=[(0, W), (2 * W, 4 * W)], name="hgrn_in_proj")
    f1 = _mm_nn([h1], wh, F32, b_cols=[(W, 2 * W)], name="hgrn_forget_proj")
    o1, y1, states, a_t1 = _hgrn_fwd(p1, f1, lb, onorm, H=H, name="hgrn_fwd")
    xo = _mm_nn([y1], wo1, F32, residual=x1, name="hgrn_out_proj")

    dx2b, loss_part, dgf = _loss_head(xo, final_gain.reshape(1, D), tgt, name="loss_head")
    loss = lax.psum(jnp.sum(loss_part), ("x", "y", "c"))

    dy1 = _mm_nn([dx2b], wo1, BF16, b_t=True, name="hgrn_out_proj_dx")
    dwo1 = _mm_tn(y1, [dx2b], BF16, name="hgrn_out_proj_dw")
    do1, dg1, donorm = _hgrn_post_bwd(dy1, o1, p1, onorm, H=H, name="hgrn_post_bwd")
    dq1, df1, di1, dlb = _hgrn_bwd(p1, f1, lb, do1, states, a_t1, H=H, name="hgrn_bwd")
    segs1 = [dq1, df1, di1, dg1]
    dh1 = _mm_nn(segs1, wh, BF16, b_t=True, tn=D, name="hgrn_in_proj_dx")
    dwh = _mm_tn(h1, segs1, BF16, name="hgrn_in_proj_dw")
    part_views = [col(ch), row(ro)]
    slot = lambda r, me, m: r.at[m]
    ex1 = _send_start([dwh, dwo1], [SDS((N_DEV, D, ch), BF16), SDS((N_DEV, ro, D), BF16)],
                      [lambda r, me, p, m, v=v: v(r, p) for v in part_views], [slot] * 2, dwo1[0:8],
                      name="exchange_layer1_start")
    ng1 = norm_gains[1:2] + ex1[4][0:1, 0:1]
    dx1b, dng1 = _rms_bwd(x1, ng1, dh1, dx2b, BF16, name="rms1_bwd")

    dy0 = _mm_nn([dx1b], wo0, BF16, b_t=True, name="fox_out_proj_dx")
    dwo0 = _mm_tn(y0, [dx1b], BF16, name="fox_out_proj_dw")
    do0, dg0, doaug = _fox_post_bwd(dy0, o0, p0, H=H, name="fox_post_bwd")
    dq0, dc_row, dk0, dv0, dc_key = _fox_bwd(p0, kaug, qaug, do0, doaug, H=H, name="fox_attn_bwd")
    dfl_t, dbf = _fox_gate_bwd(dc_row.reshape(H, S), dc_key.reshape(H, S), fl_t, b_col, name="fox_gate_bwd")
    dfl_tb = dfl_t.astype(BF16)
    dwfl_t = _mm_nn([dfl_tb], h0, BF16, name="fox_forget_proj_dw")
    segs0 = [dq0, dk0, dv0, dg0]
    dwf_main = _mm_tn(h0, segs0, BF16, name="fox_in_proj_dw")
    dwf = jnp.concatenate([dwf_main[:, :3 * W], dwfl_t.T, dwf_main[:, 3 * W:]], axis=1)
    dwf_blocks = jnp.transpose(dwf.reshape(D, N_DEV, cf), (1, 0, 2))
    ex0 = _send_start([dwf_blocks, dwo0], [SDS((N_DEV, D, cf), BF16), SDS((N_DEV, ro, D), BF16)],
                      [lambda r, me, p, m: r.at[p], lambda r, me, p, m: row(ro)(r, p)], [slot] * 2, dwo0[0:8],
                      name="exchange_layer0_start")
    wfl_t0 = wfl_t + ex0[4][0:1, 0:1].astype(BF16)
    dh0_f = _mm_nn([dfl_tb.T], wfl_t0, BF16, name="fox_forget_proj_dx")
    dh0 = _mm_nn(segs0, wf_main, BF16, residual=dh0_f, b_t=True, tn=D, name="fox_in_proj_dx")
    grad_x, dng0 = _rms_bwd(x2, norm_gains[0:1], dh0, dx1b, F32, name="rms0_bwd")

    own1 = [lambda src, land, me, v=v: (v(src, me), land.at[0]) for v in part_views]
    rh, ro1 = _send_wait(ex1, [lambda r, me, p, m, v=v: v(r, p) for v in part_views], [slot] * 2, own1,
                         [(D, ch), (ro, D)], dng0, name="exchange_layer1_wait")

    pad = lambda a: jnp.pad(a, ((0, 0), (0, W - a.shape[1])))
    stats = jnp.concatenate([dng0, dng1, dlb, dgf, pad(dbf.reshape(1, H)), donorm,
                             jnp.zeros((2, W), F32)], axis=0)
    assert D == W
    (stats_all,) = _all_gather([stats], [SDS((N_DEV, STAT_ROWS, W), F32)], [lambda r, p: r.at[p]],
                               name="gather_small_grads")
    g_small = _stats_reduce(stats_all, hgrn_lb_logits, name="reduce_small_grads")
    me = 4 * lax.axis_index("x") + 2 * lax.axis_index("y") + lax.axis_index("c")
    g_onorm = lax.dynamic_slice_in_dim(g_small[6:7], me * co, co, axis=1)

    def upd(w, m, v, parts, name):
        shp = w.shape
        r2 = (-1, shp[-1])
        g, d, mn, vn = _adamw(w.reshape(r2), m.reshape(r2), v.reshape(r2), parts, name=name)
        return g.reshape(shp), d.reshape(shp), mn.reshape(shp), vn.reshape(shp)

    res = {
        "norm_gains": upd(norm_gains, m_norm_gains, v_norm_gains, [g_small[None, 0:2]], "adamw_norm_gains"),
        "fox_b_f": upd(fox_b_f, m_fox_b_f, v_fox_b_f, [g_small[None, 5:6, :H]], "adamw_fox_b_f"),
        "hgrn_w_in": upd(hgrn_w_in, m_hgrn_w_in, v_hgrn_w_in, [rh], "adamw_hgrn_w_in"),
        "hgrn_lb_logits": upd(hgrn_lb_logits, m_hgrn_lb_logits, v_hgrn_lb_logits, [g_small[None, 2:4]],
                              "adamw_hgrn_lb_logits"),
        "hgrn_onorm": upd(hgrn_onorm, m_hgrn_onorm, v_hgrn_onorm, [g_onorm[None]], "adamw_hgrn_onorm"),
        "final_gain": upd(final_gain.reshape(1, D), m_final_gain.reshape(1, D), v_final_gain.reshape(1, D),
                          [g_small[None, 4:5]], "adamw_final_gain"),
    }
    rf, ro0 = _send_wait(ex0, [lambda r, me, p, m: r.at[p], lambda r, me, p, m: row(ro)(r, p)], [slot] * 2,
                         [lambda src, land, me: (src.at[me], land.at[0]),
                          lambda src, land, me: (row(ro)(src, me), land.at[0])],
                         [(D, cf), (ro, D)], res["hgrn_w_in"][0][0, 0:8], name="exchange_layer0_wait")
    res["fox_w_in"] = upd(fox_w_in, m_fox_w_in, v_fox_w_in, [rf], "adamw_fox_w_in")
    res["w_out"] = upd(w_out, m_w_out, v_w_out, [ro0, ro1], "adamw_w_out")
    order = ["norm_gains", "fox_w_in", "fox_b_f", "hgrn_w_in", "hgrn_lb_logits", "hgrn_onorm", "w_out", "final_gain"]
    fix = lambda n, a: a.reshape(D) if n == "final_gain" else a
    outs = [loss, grad_x.reshape(1, S, D)]
    for k in range(4):
        outs += [fix(n, res[n][k]) for n in order]
    return tuple(outs)
```

```python
import functools

import numpy as np
import jax
import jax.numpy as jnp
from jax import lax
from jax.experimental import pallas as pl
from jax.experimental.pallas import tpu as pltpu

F32 = jnp.float32
BF16 = jnp.bfloat16
SDS = jax.ShapeDtypeStruct
MESH = pl.DeviceIdType.MESH

EPS = 1e-6
ADAM_LR, ADAM_B1, ADAM_B2, ADAM_EPS, ADAM_WD, ADAM_STEP = 0.001, 0.9, 0.999, 1e-08, 0.01, 10

N_DEV = 8
FOX_HEADS = 16
HGRN_HEADS = 16
HEAD_DIM = 128
HGRN_CHUNK = 128
HGRN_LEAF = 16
HGRN_HEADS_PER_STEP = 16
EXP_CLAMP = 85.0
ATT_BLOCK = 512
ATT_HEADS_PER_STEP = 4
ATT_BWD_HEADS_PER_STEP = 2
POST_HEADS_PER_STEP = 16
NEG = -1e30
LOG2E = 1.4426950408889634
LN2 = 0.6931471805599453

VMEM_LIMIT_V7X = 56 * 1024 * 1024


def _params(*sem):
    return pltpu.CompilerParams(dimension_semantics=sem, vmem_limit_bytes=VMEM_LIMIT_V7X)


def _silu(x):
    return x * jax.nn.sigmoid(x)


def _dsilu(x):
    s = jax.nn.sigmoid(x)
    return s * (1.0 + x * (1.0 - s))


def _dot(a, b):
    return jnp.dot(a, b, preferred_element_type=F32)


def _dot_nt(a, b):
    return lax.dot_general(a, b, (((1,), (1,)), ((), ())), preferred_element_type=F32)


def _dot_tn(a, b):
    return lax.dot_general(a, b, (((0,), (0,)), ((), ())), preferred_element_type=F32)


def _mm_nn(a_list, b, out_dtype, *, name, residual=None, scale_cols=None, b_t=False, b_cols=None,
           tm=1024, tn=1024, tk=2048):
    ns = len(a_list)
    M, Ks = a_list[0].shape
    K, N = (b.shape[1], b.shape[0]) if b_t else b.shape
    if b_cols is None:
        b_cols = [(0, N)]
    else:
        assert not b_t
        N = sum(e - s for s, e in b_cols)
    dot = _dot_nt if b_t else _dot
    assert K == ns * Ks and all(a.shape == (M, Ks) for a in a_list)
    if ns > 1:
        tk = tk // 2
    tm, tn, tk = min(tm, M), min(tn, N), min(tk, Ks)
    assert M % tm == 0 and N % tn == 0 and Ks % tk == 0
    assert scale_cols is None or scale_cols[0] % tn == 0
    assert all(s % tn == 0 and e % tn == 0 for s, e in b_cols)
    nks = Ks // tk
    nk = ns * nks
    has_res = residual is not None

    def body(*refs):
        a_refs, b_ref = refs[:ns], refs[ns]
        res_ref = refs[ns + 1] if has_res else None
        o_ref = refs[ns + 1 + has_res]

        def finish(r):
            if has_res:
                r = r + res_ref[...].astype(F32)
            if scale_cols is not None:
                r = r * jnp.where(pl.program_id(1) < scale_cols[0] // tn, scale_cols[1], 1.0)
            o_ref[...] = r.astype(out_dtype)

        if nk == 1:
            finish(dot(a_refs[0][...], b_ref[...]))
            return
        acc_ref = refs[ns + 2 + has_res]
        k = pl.program_id(2)

        @pl.when(k == 0)
        def _():
            acc_ref[...] = jnp.zeros_like(acc_ref)

        for s in range(ns):
            def step(s=s):
                acc_ref[...] += dot(a_refs[s][...], b_ref[...])

            if ns == 1:
                step()
            else:
                pl.when(k // nks == s)(step)

        @pl.when(k == nk - 1)
        def _():
            finish(acc_ref[...])

    def a_map(i, j, k, s):
        return (i, jnp.clip(k - s * nks, 0, nks - 1))

    in_specs = [pl.BlockSpec((tm, tk), functools.partial(a_map, s=s)) for s in range(ns)]
    def b_col(j):
        src = j + b_cols[0][0] // tn
        for (_, e0), (s1, _) in zip(b_cols[:-1], b_cols[1:]):
            src = src + jnp.where(src >= e0 // tn, (s1 - e0) // tn, 0)
        return src

    if b_t:
        in_specs.append(pl.BlockSpec((tn, tk), lambda i, j, k: (j, k)))
    else:
        in_specs.append(pl.BlockSpec((tk, tn), lambda i, j, k: (k, b_col(j))))
    args = list(a_list) + [b]
    if has_res:
        in_specs.append(pl.BlockSpec((tm, tn), lambda i, j, k: (i, j)))
        args.append(residual)
    return pl.pallas_call(
        body, grid=(M // tm, N // tn, nk), in_specs=in_specs,
        out_specs=pl.BlockSpec((tm, tn), lambda i, j, k: (i, j)),
        out_shape=SDS((M, N), out_dtype),
        scratch_shapes=[] if nk == 1 else [pltpu.VMEM((tm, tn), F32)],
        compiler_params=_params("parallel", "parallel", "arbitrary"), name=name,
    )(*args)


def _mm_tn(a, b_list, out_dtype, *, name, tm=2048, tn=1024, tk=512):
    ns = len(b_list)
    S, M = a.shape
    Ns = b_list[0].shape[1]
    assert all(b.shape == (S, Ns) for b in b_list)
    tm, tn, tk = min(tm, M), min(tn, Ns), min(tk, S)
    assert M % tm == 0 and Ns % tn == 0 and S % tk == 0
    njs = Ns // tn
    nk = S // tk

    def body(*refs):
        a_ref, b_refs, o_ref, acc_ref = refs[0], refs[1:1 + ns], refs[1 + ns], refs[2 + ns]
        j, k = pl.program_id(1), pl.program_id(2)

        @pl.when(k == 0)
        def _():
            acc_ref[...] = jnp.zeros_like(acc_ref)

        for s in range(ns):
            def step(s=s):
                acc_ref[...] += _dot_tn(a_ref[...], b_refs[s][...])

            if ns == 1:
                step()
            else:
                pl.when(j // njs == s)(step)

        @pl.when(k == nk - 1)
        def _():
            o_ref[...] = acc_ref[...].astype(out_dtype)

    def b_map(i, j, k, s):
        return (k, jnp.clip(j - s * njs, 0, njs - 1))

    in_specs = [pl.BlockSpec((tk, tm), lambda i, j, k: (k, i))]
    in_specs += [pl.BlockSpec((tk, tn), functools.partial(b_map, s=s)) for s in range(ns)]
    return pl.pallas_call(
        body, grid=(M // tm, ns * njs, nk), in_specs=in_specs,
        out_specs=pl.BlockSpec((tm, tn), lambda i, j, k: (i, j)),
        out_shape=SDS((M, ns * Ns), out_dtype),
        scratch_shapes=[pltpu.VMEM((tm, tn), F32)],
        compiler_params=_params("parallel", "parallel", "arbitrary"), name=name,
    )(a, *b_list)


def _mm_nt_rows(w_t, h, *, name, tn=1024):
    R, K = w_t.shape
    S = h.shape[0]
    tn = min(tn, S)

    def body(w_ref, h_ref, o_ref):
        o_ref[...] = _dot_nt(w_ref[...], h_ref[...])

    return pl.pallas_call(
        body, grid=(S // tn,),
        in_specs=[pl.BlockSpec((R, K), lambda i: (0, 0)), pl.BlockSpec((tn, K), lambda i: (i, 0))],
        out_specs=pl.BlockSpec((R, tn), lambda i: (0, i)),
        out_shape=SDS((R, S), F32), compiler_params=_params("parallel"), name=name,
    )(w_t, h)


def _rms_fwd(x, gain, *, name, tm=1024):
    S, D = x.shape
    tm = min(tm, S)

    def body(x_ref, g_ref, h_ref):
        xv = x_ref[...]
        r = lax.rsqrt(jnp.mean(xv * xv, axis=-1, keepdims=True) + EPS)
        h_ref[...] = ((xv * r) * g_ref[...]).astype(BF16)

    return pl.pallas_call(
        body, grid=(S // tm,),
        in_specs=[pl.BlockSpec((tm, D), lambda i: (i, 0)), pl.BlockSpec((1, D), lambda i: (0, 0))],
        out_specs=pl.BlockSpec((tm, D), lambda i: (i, 0)),
        out_shape=SDS((S, D), BF16), compiler_params=_params("parallel"), name=name,
    )(x, gain)


def _rms_bwd(x, gain, dh, dres, out_dtype, *, name, tm=512):
    S, D = x.shape
    tm = min(tm, S)

    def body(x_ref, g_ref, dh_ref, dres_ref, dx_ref, dg_ref):
        @pl.when(pl.program_id(0) == 0)
        def _():
            dg_ref[...] = jnp.zeros_like(dg_ref)

        xv = x_ref[...]
        r = lax.rsqrt(jnp.mean(xv * xv, axis=-1, keepdims=True) + EPS)
        xh = xv * r
        dhv = dh_ref[...].astype(F32)
        dg_ref[...] += jnp.sum(dhv * xh, axis=0, keepdims=True)
        dxh = dhv * g_ref[...]
        dx = r * (dxh - xh * jnp.mean(dxh * xh, axis=-1, keepdims=True)) + dres_ref[...].astype(F32)
        dx_ref[...] = dx.astype(out_dtype)

    row = pl.BlockSpec((tm, D), lambda i: (i, 0))
    vec = pl.BlockSpec((1, D), lambda i: (0, 0))
    return pl.pallas_call(
        body, grid=(S // tm,), in_specs=[row, vec, row, row], out_specs=[row, vec],
        out_shape=[SDS((S, D), out_dtype), SDS((1, D), F32)],
        compiler_params=_params("arbitrary"), name=name,
    )(x, gain, dh, dres)


def _loss_head(x, gain, target, *, name, tm=512):
    S, D = x.shape
    tm = min(tm, S)
    assert tm % 8 == 0 and D % 128 == 0

    def body(x_ref, g_ref, t_ref, dxb_ref, loss_ref, dg_ref):
        @pl.when(pl.program_id(0) == 0)
        def _():
            dg_ref[...] = jnp.zeros_like(dg_ref)
            loss_ref[...] = jnp.zeros_like(loss_ref)

        xv = x_ref[...]
        g = g_ref[...]
        r = lax.rsqrt(jnp.mean(xv * xv, axis=-1, keepdims=True) + EPS)
        xh = xv * r
        err = xh * g - t_ref[...]
        e2 = (err * err).reshape(tm // 8, 8, D).sum(axis=0)
        part = e2[:, 0:128]
        for k in range(1, D // 128):
            part = part + e2[:, k * 128:(k + 1) * 128]
        loss_ref[...] += part * (0.5 / D)
        dy = err * (1.0 / D)
        dg_ref[...] += jnp.sum(dy * xh, axis=0, keepdims=True)
        dxh = dy * g
        dx = r * (dxh - xh * jnp.mean(dxh * xh, axis=-1, keepdims=True))
        dxb_ref[...] = dx.astype(BF16)

    row = pl.BlockSpec((tm, D), lambda i: (i, 0))
    vec = pl.BlockSpec((1, D), lambda i: (0, 0))
    return pl.pallas_call(
        body, grid=(S // tm,), in_specs=[row, vec, row],
        out_specs=[row, pl.BlockSpec((8, 128), lambda i: (0, 0)), vec],
        out_shape=[SDS((S, D), BF16), SDS((8, 128), F32), SDS((1, D), F32)],
        compiler_params=_params("arbitrary"), name=name,
    )(x, gain, target)


def _split3(x):
    hi = x.astype(BF16)
    r1 = x - hi.astype(F32)
    mid = r1.astype(BF16)
    lo = (r1 - mid.astype(F32)).astype(BF16)
    return hi, mid, lo


def _split2(x):
    hi = x.astype(BF16)
    lo = (x - hi.astype(F32)).astype(BF16)
    return hi, lo


def _fox_gate_fwd(fl_t, b_col, *, name):
    H, S = fl_t.shape
    L = 128
    tri = jnp.asarray(np.triu(np.ones((L, L), np.float32)), BF16)

    def body(fl_ref, b_ref, tri_ref, hi_ref, mid_ref, lo_ref, carry):
        @pl.when(pl.program_id(0) == 0)
        def _():
            carry[...] = jnp.zeros_like(carry)

        z = fl_ref[...] + b_ref[...]
        lf = jnp.minimum(z, 0.0) - jnp.log(1.0 + jnp.exp(-jnp.abs(z)))
        hi, mid, lo = _split3(lf)
        t = tri_ref[...]
        c = (_dot(hi, t) + _dot(mid, t)) + _dot(lo, t) + carry[...]
        carry[...] = c[:, L - 1:L]
        hi_ref[...], mid_ref[...], lo_ref[...] = _split3(c * (-LOG2E))

    blk = pl.BlockSpec((H, L), lambda i: (0, i))
    return pl.pallas_call(
        body, grid=(S // L,),
        in_specs=[blk, pl.BlockSpec((H, 1), lambda i: (0, 0)), pl.BlockSpec((L, L), lambda i: (0, 0))],
        out_specs=[blk] * 3, out_shape=[SDS((H, S), BF16)] * 3, scratch_shapes=[pltpu.VMEM((H, 1), F32)],
        compiler_params=_params("arbitrary"), name=name,
    )(fl_t, b_col, tri)


def _fox_gate_bwd(dc_row, dc_key, fl_t, b_col, *, name):
    H, S = fl_t.shape
    L = 128
    n = S // L
    tri = jnp.asarray(np.tril(np.ones((L, L), np.float32)), BF16)

    def body(dcr_ref, dck_ref, fl_ref, b_ref, tri_ref, dfl_ref, db_ref, carry):
        @pl.when(pl.program_id(0) == 0)
        def _():
            carry[...] = jnp.zeros_like(carry)
            db_ref[...] = jnp.zeros_like(db_ref)

        hi, mid, lo = _split3(dcr_ref[...] + dck_ref[...])
        t = tri_ref[...]
        dlf = (_dot(hi, t) + _dot(mid, t)) + _dot(lo, t) + carry[...]
        carry[...] = dlf[:, 0:1]
        z = fl_ref[...] + b_ref[...]
        dfl = dlf * jax.nn.sigmoid(-z)
        dfl_ref[...] = dfl
        db_ref[...] += jnp.sum(dfl, axis=1, keepdims=True)

    blk = pl.BlockSpec((H, L), lambda i: (0, n - 1 - i))
    col = pl.BlockSpec((H, 1), lambda i: (0, 0))
    return pl.pallas_call(
        body, grid=(n,), in_specs=[blk, blk, blk, col, pl.BlockSpec((L, L), lambda i: (0, 0))],
        out_specs=[blk, col], out_shape=[SDS((H, S), F32), SDS((H, 1), F32)],
        scratch_shapes=[pltpu.VMEM((H, 1), F32)], compiler_params=_params("arbitrary"), name=name,
    )(dc_row, dc_key, fl_t, b_col, tri)


AUG = HEAD_DIM


def _lane_select(cols, shape):
    lane = lax.broadcasted_iota(jnp.int32, shape, 1)
    out = jnp.zeros(shape, BF16)
    for k, c in reversed(list(enumerate(cols))):
        c = jnp.full(shape, c, BF16) if isinstance(c, (int, float)) else jnp.broadcast_to(c, shape).astype(BF16)
        out = jnp.where(lane == k, c, out)
    return out


def _fox_key_aug(b_hi, b_mid, b_lo):
    H, S = b_hi.shape
    ones = jnp.ones((H, S), BF16)
    ka = jnp.stack([b_hi, b_mid, b_lo, ones, ones, ones], axis=-1)
    ka = jnp.pad(ka, ((0, 0), (0, 0), (0, AUG - 6)))
    return jnp.transpose(ka, (1, 0, 2)).reshape(S, H * AUG)


def _fox_fwd(p0, kaug, *, H, name):
    S = p0.shape[0]
    T = min(ATT_BLOCK, S)
    nq = S // T
    dh = HEAD_DIM
    G = ATT_HEADS_PER_STEP
    assert H % G == 0

    def body(q_ref, k_ref, ka_ref, v_ref, g_ref, o_ref, y_ref, qa_ref, m_sc, acc_sc, p_sc, al_sc):
        i = pl.program_id(1)
        qaug = _lane_select([1.0, 1.0, 1.0], (T, AUG))
        ones = jnp.ones((T, dh), BF16)
        m_sc[...] = jnp.full_like(m_sc, NEG)
        acc_sc[...] = jnp.zeros_like(acc_sc)

        def step(j, before, masked):
            rows = pl.ds(pl.multiple_of(j * T, T), T)
            for g in range(G):
                hd = slice(g * dh, (g + 1) * dh)
                if not masked:
                    prev = pl.ds(pl.multiple_of(before * T, T), T)
                    vp = jnp.concatenate([v_ref[prev, hd], ones], axis=1)
                    acc_sc[g] = jnp.tile(al_sc[g], (1, 2)) * acc_sc[g] + _dot(p_sc[g], vp)
                q = jnp.concatenate([q_ref[:, hd], qaug], axis=1)
                kj = jnp.concatenate([k_ref[rows, hd], ka_ref[rows, hd]], axis=1)
                t = _dot_nt(q, kj)
                if masked:
                    row = lax.broadcasted_iota(jnp.int32, (T, T), 0)
                    col = lax.broadcasted_iota(jnp.int32, (T, T), 1)
                    t = jnp.where(row >= col, t, NEG)
                m_prev = m_sc[g]
                m_new = jnp.maximum(m_prev, jnp.max(t, axis=-1, keepdims=True))
                p_sc[g] = jnp.exp2(t - jnp.tile(m_new, (1, T // 128))).astype(BF16)
                al_sc[g] = jnp.exp2(m_prev - m_new)
                m_sc[g] = m_new

        step(i, None, True)

        def loop_body(j, carry):
            step(j, jnp.where(j == 0, i, j - 1), False)
            return carry

        lax.fori_loop(0, i, loop_body, 0)
        rows = pl.ds(pl.multiple_of(jnp.where(i == 0, 0, i - 1) * T, T), T)
        for g in range(G):
            hd = slice(g * dh, (g + 1) * dh)
            vp = jnp.concatenate([v_ref[rows, hd], ones], axis=1)
            acc = jnp.tile(al_sc[g], (1, 2)) * acc_sc[g] + _dot(p_sc[g], vp)
            l = acc[:, dh:]
            o = acc[:, :dh] / l
            o_ref[:, hd] = o
            y_ref[:, hd] = (o * _silu(g_ref[:, hd].astype(F32))).astype(BF16)
            hi, mid, lo = _split3(-(m_sc[g] + jnp.log2(l)))
            qa_ref[:, hd] = _lane_select([1.0, 1.0, 1.0, hi, mid, lo], (T, AUG))

    blk = lambda off: pl.BlockSpec((T, G * dh), lambda h, i: (i, off // G + h))
    full = lambda off: pl.BlockSpec((S, G * dh), lambda h, i: (0, off // G + h), pipeline_mode=pl.Buffered(1))
    return pl.pallas_call(
        body, grid=(H // G, nq),
        in_specs=[blk(0), full(H), full(0), full(2 * H), blk(3 * H)],
        out_specs=[blk(0), blk(0), blk(0)],
        out_shape=[SDS((S, H * dh), F32), SDS((S, H * dh), BF16), SDS((S, H * AUG), BF16)],
        scratch_shapes=[pltpu.VMEM((G, T, 128), F32), pltpu.VMEM((G, T, 2 * dh), F32),
                        pltpu.VMEM((G, T, T), BF16), pltpu.VMEM((G, T, 128), F32)],
        compiler_params=_params("parallel", "arbitrary"), name=name,
    )(p0, p0, kaug, p0, p0)


def _fox_post_bwd(dy, o, p0, *, H, name, tm=512):
    S = dy.shape[0]
    dh = HEAD_DIM
    tm = min(tm, S)
    G = POST_HEADS_PER_STEP
    assert H % G == 0

    def body(dy_ref, o_ref, g_ref, do_ref, dg_ref, da_ref):
        dyv = dy_ref[...].astype(F32)
        ov = o_ref[...]
        g = g_ref[...].astype(F32)
        do = (dyv * _silu(g)).astype(BF16)
        do_ref[...] = do
        dg_ref[...] = (dyv * ov * _dsilu(g)).astype(BF16)
        prod = do.astype(F32) * ov
        for k in range(G):
            hd = slice(k * dh, (k + 1) * dh)
            delta = jnp.sum(prod[:, hd], axis=-1, keepdims=True)
            hi, mid, lo = _split3(-jnp.broadcast_to(delta, (tm, AUG)))
            da_ref[:, hd] = _lane_select([hi, mid, lo], (tm, AUG))

    blk = pl.BlockSpec((tm, G * dh), lambda h, i: (i, h))
    return pl.pallas_call(
        body, grid=(H // G, S // tm),
        in_specs=[blk, blk, pl.BlockSpec((tm, G * dh), lambda h, i: (i, 3 * H // G + h))],
        out_specs=[blk, blk, blk],
        out_shape=[SDS((S, H * dh), BF16), SDS((S, H * dh), BF16), SDS((S, H * AUG), BF16)],
        compiler_params=_params("parallel", "parallel"), name=name,
    )(dy, o, p0)


def _fox_bwd(p0, kaug, qaug, do, doaug, *, H, name):
    S = p0.shape[0]
    T = min(ATT_BLOCK, S)
    nq = S // T
    dh = HEAD_DIM
    scale = dh ** -0.5
    G = ATT_BWD_HEADS_PER_STEP
    assert H % G == 0

    def body(q_ref, qa_ref, k_ref, ka_ref, v_ref, do_ref, da_ref, dq_ref, rs_ref, dk_ref, dv_ref, dc_ref,
             dq_sc, dk_sc, dv_sc, pt_sc, dst_sc):
        j = pl.program_id(1)
        vaug = _lane_select([1.0, 1.0, 1.0], (T, AUG))
        ones = jnp.ones((T, dh), BF16)

        @pl.when(j == 0)
        def _():
            dq_sc[...] = jnp.zeros_like(dq_sc)

        dk_sc[...] = jnp.zeros_like(dk_sc)
        dv_sc[...] = jnp.zeros_like(dv_sc)

        def apply(prev):
            for g in range(G):
                hd = slice(g * dh, (g + 1) * dh)
                dv_sc[g] += _dot(pt_sc[g], do_ref[prev, hd])
                dk_sc[g] += _dot(dst_sc[g], jnp.concatenate([q_ref[prev, hd], ones], axis=1))
                dq_sc[g, prev] += _dot_tn(dst_sc[g], jnp.concatenate([k_ref[:, hd], ones], axis=1))

        def step(i, masked):
            rows = pl.ds(pl.multiple_of(i * T, T), T)
            if not masked:
                apply(pl.ds(pl.multiple_of((i - 1) * T, T), T))
            for g in range(G):
                hd = slice(g * dh, (g + 1) * dh)
                k = jnp.concatenate([k_ref[:, hd], ka_ref[:, hd]], axis=1)
                v = jnp.concatenate([v_ref[:, hd], vaug], axis=1)
                pt = jnp.exp2(_dot_nt(k, jnp.concatenate([q_ref[rows, hd], qa_ref[rows, hd]], axis=1)))
                if masked:
                    row = lax.broadcasted_iota(jnp.int32, (T, T), 0)
                    col = lax.broadcasted_iota(jnp.int32, (T, T), 1)
                    pt = jnp.where(col >= row, pt, 0.0)
                dst = pt * _dot_nt(v, jnp.concatenate([do_ref[rows, hd], da_ref[rows, hd]], axis=1))
                pt_sc[g] = pt.astype(BF16)
                dst_sc[g] = dst.astype(BF16)

        step(j, True)

        def loop_body(i, carry):
            step(i, False)
            return carry

        lax.fori_loop(j + 1, nq, loop_body, 0)
        apply(pl.ds((nq - 1) * T, T))
        for g in range(G):
            hd = slice(g * dh, (g + 1) * dh)
            dk_ref[:, hd] = (dk_sc[g, :, :dh] * LN2).astype(BF16)
            dv_ref[:, hd] = dv_sc[g].astype(BF16)
            dc_ref[g] = -jnp.transpose(dk_sc[g, :, dh:])[0:1]

        @pl.when(j == nq - 1)
        def _():
            for g in range(G):
                dq_ref[:, g * dh:(g + 1) * dh] = (dq_sc[g, :, :dh] * scale).astype(BF16)
                for i in range(nq):
                    rs_ref[g, :, i * T:(i + 1) * T] = jnp.transpose(dq_sc[g, i * T:(i + 1) * T, dh:])[0:1]

    blk = lambda off: pl.BlockSpec((T, G * dh), lambda h, j: (j, off // G + h))
    full = lambda off: pl.BlockSpec((S, G * dh), lambda h, j: (0, off // G + h))
    once = lambda off: pl.BlockSpec((S, G * dh), lambda h, j: (0, off // G + h), pipeline_mode=pl.Buffered(1))
    rowv = pl.BlockSpec((G, 1, T), lambda h, j: (h, 0, j))
    return pl.pallas_call(
        body, grid=(H // G, nq),
        in_specs=[once(0), once(0), blk(H), blk(0), blk(2 * H), once(0), once(0)],
        out_specs=[full(0), pl.BlockSpec((G, 1, S), lambda h, j: (h, 0, 0)), blk(0), blk(0), rowv],
        out_shape=[SDS((S, H * dh), BF16), SDS((H, 1, S), F32), SDS((S, H * dh), BF16), SDS((S, H * dh), BF16),
                   SDS((H, 1, S), F32)],
        scratch_shapes=[pltpu.VMEM((G, S, 2 * dh), F32), pltpu.VMEM((G, T, 2 * dh), F32), pltpu.VMEM((G, T, dh), F32),
                        pltpu.VMEM((G, T, T), BF16), pltpu.VMEM((G, T, T), BF16)],
        compiler_params=_params("parallel", "arbitrary"), name=name,
    )(p0, qaug, p0, kaug, p0, do, doaug)


def _hgrn_levels(C, leaf):
    levels = []
    h = C // 2
    while h >= leaf:
        levels.append(h)
        h //= 2
    return levels


def _hgrn_sum_matrix(C, leaf):
    t = np.arange(C)[:, None]
    u = np.arange(C)[None, :]
    mats = [(u <= t), (u > t)]
    for h in _hgrn_levels(C, leaf):
        start = (t // (2 * h)) * (2 * h)
        mid = start + h - 1
        second = t > mid
        m = np.where(second, (u > mid) & (u <= t), (u > t) & (u <= mid))
        mats.append(m)
    lstart = (t // leaf) * leaf
    mats.append((u >= lstart) & (u <= t))
    return np.concatenate([m.astype(np.float32) for m in mats], axis=0)


def _hgrn_chunk_terms(qr, fz, lb, msum, C, leaf):
    levels = _hgrn_levels(C, leaf)
    sq = _silu(qr)
    sp = 1.0 / (1.0 + jnp.exp(-fz))
    sn = 1.0 / (1.0 + jnp.exp(fz))
    f = lb + (1.0 - lb) * sp
    lf = jnp.log(f)
    k = (1.0 - lb) * sn
    hi, lo = _split2(lf)
    dsum = _dot(msum, hi) + _dot(msum, lo)
    b = dsum[0:C]
    kdec = dsum[C:2 * C]
    rowi = lax.broadcasted_iota(jnp.int32, (C, 1), 0)
    lev = []
    for n, h in enumerate(levels):
        e = jnp.exp(dsum[(2 + n) * C:(3 + n) * C])
        selq = jnp.where((rowi % (2 * h)) >= h, 1.0, 0.0)
        qm = (sq * e * selq).astype(BF16)
        km = (k * e * (1.0 - selq)).astype(BF16)
        lev.append((h, e, selq, qm, km))
    dleaf = dsum[(2 + len(levels)) * C:(3 + len(levels)) * C]
    eq = jnp.exp(dleaf)
    ek = jnp.exp(jnp.minimum(-dleaf, EXP_CLAMP))
    return dict(sq=sq, sp=sp, sn=sn, f=f, k=k, b=b, kdec=kdec, lev=lev, eq=eq, ek=ek,
                ql=(sq * eq).astype(BF16), kl=(k * ek).astype(BF16),
                qs=(sq * jnp.exp(b)).astype(BF16), ke=(k * jnp.exp(kdec)).astype(BF16),
                e_c=jnp.exp(b[C - 1:C, :]))


def _hgrn_masks(C, leaf, transposed):
    a = lax.broadcasted_iota(jnp.int32, (C, C), 0)
    bb = lax.broadcasted_iota(jnp.int32, (C, C), 1)
    t, s = (bb, a) if transposed else (a, bb)
    lev = [None if 2 * h == C else (t // (2 * h)) == (s // (2 * h)) for h in _hgrn_levels(C, leaf)]
    if leaf == C:
        leafm = s <= t
    else:
        leafm = ((t // leaf) == (s // leaf)) & (s <= t)
    return lev, leafm


def _hgrn_fwd(p1, f1, lb, onorm, *, H, name, tb=512):
    S = p1.shape[0]
    dk = HEAD_DIM
    C = min(HGRN_CHUNK, S)
    leaf = min(HGRN_LEAF, C)
    tb = min(tb, S)
    nc = tb // C
    G = HGRN_HEADS_PER_STEP
    assert H % G == 0
    msum = jnp.asarray(_hgrn_sum_matrix(C, leaf), BF16)

    def body(q_ref, f_ref, v_ref, g_ref, lb_ref, on_ref, ms_ref, o_ref, y_ref, st_ref, at_ref, st_sc):
        @pl.when(pl.program_id(1) == 0)
        def _():
            st_sc[...] = jnp.zeros_like(st_sc)

        msv = ms_ref[...]
        lmask, leafm = _hgrn_masks(C, leaf, False)

        def chunk(n, carry):
            rows = pl.ds(pl.multiple_of(n * C, C), C)
            for g in range(G):
                hd = slice(g * dk, (g + 1) * dk)
                tm = _hgrn_chunk_terms(q_ref[rows, hd].astype(F32), f_ref[rows, hd], lb_ref[:, hd], msv, C, leaf)
                v = v_ref[rows, hd]
                st = st_sc[g]
                st_ref[g, n] = st
                a = jnp.where(leafm, _dot_nt(tm["ql"], tm["kl"]), 0.0)
                for (h, e, selq, qm, km), m in zip(tm["lev"], lmask):
                    al = _dot_nt(qm, km)
                    a = a + (al if m is None else jnp.where(m, al, 0.0))
                at_ref[g, n] = jnp.transpose(a).astype(BF16)
                o = _dot_nt(tm["qs"], st.astype(BF16)) + _dot(a.astype(BF16), v)
                st_sc[g] = st * tm["e_c"] + _dot(v.T, tm["ke"])
                o_ref[rows, hd] = o
                rn = lax.rsqrt(jnp.mean(o * o, axis=-1, keepdims=True) + EPS)
                y = ((o * rn) * on_ref[:, hd]) * _silu(g_ref[rows, hd].astype(F32))
                y_ref[rows, hd] = y.astype(BF16)
            return carry

        lax.fori_loop(0, nc, chunk, 0)

    blk = lambda off: pl.BlockSpec((tb, G * dk), lambda h, i: (i, off // G + h))
    vec = pl.BlockSpec((1, G * dk), lambda h, i: (0, h))
    return pl.pallas_call(
        body, grid=(H // G, S // tb),
        in_specs=[blk(0), blk(0), blk(H), blk(2 * H), vec, vec,
                  pl.BlockSpec(msum.shape, lambda h, i: (0, 0))],
        out_specs=[blk(0), blk(0), pl.BlockSpec((G, nc, dk, dk), lambda h, i: (h, i, 0, 0)),
                   pl.BlockSpec((G, nc, C, C), lambda h, i: (h, i, 0, 0))],
        out_shape=[SDS((S, H * dk), F32), SDS((S, H * dk), BF16), SDS((H, S // C, dk, dk), F32),
                   SDS((H, S // C, C, C), BF16)],
        scratch_shapes=[pltpu.VMEM((G, dk, dk), F32)],
        compiler_params=_params("parallel", "arbitrary"), name=name,
    )(p1, f1, p1, p1, lb, onorm, msum)


def _hgrn_post_bwd(dy, o, p1, onorm, *, H, name, tm=512):
    S = dy.shape[0]
    dk = HEAD_DIM
    tm = min(tm, S)
    G = POST_HEADS_PER_STEP
    assert H % G == 0

    def body(dy_ref, o_ref, g_ref, on_ref, do_ref, dg_ref, don_ref):
        @pl.when(pl.program_id(1) == 0)
        def _():
            don_ref[...] = jnp.zeros_like(don_ref)

        for k in range(G):
            hd = slice(k * dk, (k + 1) * dk)
            dyv = dy_ref[:, hd].astype(F32)
            ov = o_ref[:, hd]
            g = g_ref[:, hd].astype(F32)
            onv = on_ref[:, hd]
            rn = lax.rsqrt(jnp.mean(ov * ov, axis=-1, keepdims=True) + EPS)
            oh = ov * rn
            dn = dyv * _silu(g)
            dg_ref[:, hd] = (dyv * (oh * onv) * _dsilu(g)).astype(BF16)
            don_ref[:, hd] += jnp.sum(dn * oh, axis=0, keepdims=True)
            doh = dn * onv
            do_ref[:, hd] = (rn * (doh - oh * jnp.mean(doh * oh, axis=-1, keepdims=True))).astype(BF16)

    blk = pl.BlockSpec((tm, G * dk), lambda h, i: (i, h))
    vec = pl.BlockSpec((1, G * dk), lambda h, i: (0, h))
    return pl.pallas_call(
        body, grid=(H // G, S // tm),
        in_specs=[blk, blk, pl.BlockSpec((tm, G * dk), lambda h, i: (i, 2 * H // G + h)), vec],
        out_specs=[blk, blk, vec],
        out_shape=[SDS((S, H * dk), BF16), SDS((S, H * dk), BF16), SDS((1, H * dk), F32)],
        compiler_params=_params("parallel", "arbitrary"), name=name,
    )(dy, o, p1, onorm)


def _hgrn_bwd(p1, f1, lb, do, states, a_t, *, H, name, tb=512):
    S = p1.shape[0]
    dk = HEAD_DIM
    C = min(HGRN_CHUNK, S)
    leaf = min(HGRN_LEAF, C)
    tb = min(tb, S)
    nc = tb // C
    nb = S // tb
    G = HGRN_HEADS_PER_STEP
    assert H % G == 0
    msum = jnp.asarray(_hgrn_sum_matrix(C, leaf), BF16)
    rtri = jnp.asarray(np.triu(np.ones((C, C), np.float32)), BF16)

    def body(q_ref, f_ref, v_ref, do_ref, st_ref, at_ref, lb_ref, ms_ref, rt_ref,
             dq_ref, df_ref, dv_ref, dlb_ref, g_sc):
        @pl.when(pl.program_id(1) == 0)
        def _():
            g_sc[...] = jnp.zeros_like(g_sc)
            dlb_ref[...] = jnp.zeros_like(dlb_ref)

        msv = ms_ref[...]
        rtv = rt_ref[...]
        lmask, leafm = _hgrn_masks(C, leaf, False)
        lmask_t, leafm_t = _hgrn_masks(C, leaf, True)
        f32 = lambda z: z.astype(F32)

        def head_chunk(g, n):
            hd = slice(g * dk, (g + 1) * dk)
            rows = pl.ds(pl.multiple_of(n * C, C), C)
            lbv = lb_ref[:, hd]
            qr = q_ref[rows, hd].astype(F32)
            tm = _hgrn_chunk_terms(qr, f_ref[rows, hd], lbv, msv, C, leaf)
            v = v_ref[rows, hd]
            dov = do_ref[rows, hd]
            st0 = st_ref[g, n]
            gt = g_sc[g]
            gtb = gt.astype(BF16)
            da = _dot_nt(dov, v)
            da_t = _dot_nt(v, dov)

            dal = jnp.where(leafm, da, 0.0).astype(BF16)
            dal_t = jnp.where(leafm_t, da_t, 0.0).astype(BF16)
            dql = _dot(dal, tm["kl"])
            dkl = _dot(dal_t, tm["ql"])
            dsq = dql * tm["eq"]
            dkk = dkl * tm["ek"]
            xq = f32(tm["ql"]) * dql
            xk = f32(tm["kl"]) * dkl
            for (h, e, selq, qm, km), m, m_t in zip(tm["lev"], lmask, lmask_t):
                dl = (da if m is None else jnp.where(m, da, 0.0)).astype(BF16)
                dl_t = (da_t if m_t is None else jnp.where(m_t, da_t, 0.0)).astype(BF16)
                dqm = _dot(dl, km)
                dkm = _dot(dl_t, qm)
                dsq = dsq + jnp.where(selq > 0.5, dqm * e, 0.0)
                dkk = dkk + jnp.where(selq > 0.5, 0.0, dkm * e)
                xq = xq + f32(qm) * dqm
                xk = xk + f32(km) * dkm
            dqs = _dot(dov, st0.astype(BF16))
            dke = _dot(v, gtb)
            dsq = dsq + dqs * jnp.exp(tm["b"])
            dkk = dkk + dke * jnp.exp(tm["kdec"])
            xq = xq + f32(tm["qs"]) * dqs
            xk = xk + f32(tm["ke"]) * dke
            dvv = _dot(at_ref[g, n], dov) + _dot_nt(tm["ke"], gtb)
            r_end = jnp.sum(f32(gtb) * _dot(v.T, tm["ke"]) + gt * (st0 * tm["e_c"]), axis=0, keepdims=True)
            g_sc[g] = gt * tm["e_c"] + _dot(dov.T, tm["qs"])
            xh, xm, xl = _split3(xq - xk)
            dlf = (_dot(rtv, xh) + _dot(rtv, xm)) + _dot(rtv, xl) + r_end
            dlf_f = dlf / tm["f"]
            dsp = (1.0 - lbv) * (dlf_f - dkk)
            df_ref[rows, hd] = (dsp * (tm["sp"] * tm["sn"])).astype(BF16)
            dq_ref[rows, hd] = (dsq * _dsilu(qr)).astype(BF16)
            dv_ref[rows, hd] = dvv.astype(BF16)
            dlb_ref[:, hd] += jnp.sum(dlf_f * tm["sn"] - dkk * tm["sn"], axis=0, keepdims=True)

        def chunk(nn, carry):
            for g in range(G):
                head_chunk(g, nc - 1 - nn)
            return carry

        lax.fori_loop(0, nc, chunk, 0)

    blk = lambda off: pl.BlockSpec((tb, G * dk), lambda h, i: (nb - 1 - i, off // G + h))
    vec = pl.BlockSpec((1, G * dk), lambda h, i: (0, h))
    return pl.pallas_call(
        body, grid=(H // G, nb),
        in_specs=[blk(0), blk(0), blk(H), blk(0),
                  pl.BlockSpec((G, nc, dk, dk), lambda h, i: (h, nb - 1 - i, 0, 0)),
                  pl.BlockSpec((G, nc, C, C), lambda h, i: (h, nb - 1 - i, 0, 0)), vec,
                  pl.BlockSpec(msum.shape, lambda h, i: (0, 0)), pl.BlockSpec((C, C), lambda h, i: (0, 0))],
        out_specs=[blk(0), blk(0), blk(0), vec],
        out_shape=[SDS((S, H * dk), BF16)] * 3 + [SDS((1, H * dk), F32)],
        scratch_shapes=[pltpu.VMEM((G, dk, dk), F32)],
        compiler_params=_params("parallel", "arbitrary"), name=name,
    )(p1, f1, p1, do, states, a_t, lb, msum, rtri)


def _lb_fwd(logits, *, name):
    W = logits.shape[1]

    def body(l_ref, lb_ref):
        l = l_ref[...]
        m = jnp.max(l, axis=0, keepdims=True)
        e = jnp.exp(l - m)
        p = e / jnp.sum(e, axis=0, keepdims=True)
        lb_ref[...] = (p[0:1] + p[1:2]) - p[0:1]

    return pl.pallas_call(body, out_shape=SDS((1, W), F32), name=name)(logits)


STAT_ROWS = 8


def _stats_reduce(stats_all, logits, *, name):
    W = logits.shape[1]

    def body(s_ref, l_ref, g_ref):
        tot = s_ref[0]
        for d in range(1, N_DEV):
            tot = tot + s_ref[d]
        l = l_ref[...]
        m = jnp.max(l, axis=0, keepdims=True)
        e = jnp.exp(l - m)
        p = e / jnp.sum(e, axis=0, keepdims=True)
        dlb = tot[2:3]
        dl0 = -(p[0:1] * p[1:2]) * dlb
        dl1 = (p[1:2] * (1.0 - p[1:2])) * dlb
        g_ref[0:2] = tot[0:2]
        g_ref[2:3] = dl0
        g_ref[3:4] = dl1
        g_ref[4:7] = tot[3:6]
        g_ref[7:8] = jnp.zeros((1, W), F32)

    return pl.pallas_call(body, out_shape=SDS((STAT_ROWS, W), F32), name=name)(stats_all, logits)


def _adamw(w, m, v, g_parts, *, name, tr=128):
    R, C = w.shape
    ns = len(g_parts)
    n, Rs = g_parts[0].shape[0], g_parts[0].shape[1]
    assert all(p.shape == (n, Rs, C) for p in g_parts) and ns * Rs == R
    tr = min(tr, Rs)
    assert Rs % tr == 0
    nts = Rs // tr
    c1 = 1.0 / (1.0 - ADAM_B1 ** ADAM_STEP)
    c2 = 1.0 / (1.0 - ADAM_B2 ** ADAM_STEP)

    def body(*refs):
        w_ref, m_ref, v_ref = refs[:3]
        g_refs = refs[3:3 + ns]
        go_ref, d_ref, mo_ref, vo_ref = refs[3 + ns:]

        def update(g_ref):
            g = g_ref[0].astype(F32)
            for k in range(1, n):
                g = g + g_ref[k].astype(F32)
            mn = ADAM_B1 * m_ref[...] + (1.0 - ADAM_B1) * g
            vn = ADAM_B2 * v_ref[...] + (1.0 - ADAM_B2) * (g * g)
            d_ref[...] = -ADAM_LR * ((mn * c1) / (jnp.sqrt(vn * c2) + ADAM_EPS) + ADAM_WD * w_ref[...])
            go_ref[...] = g
            mo_ref[...] = mn
            vo_ref[...] = vn

        for s in range(ns):
            if ns == 1:
                update(g_refs[s])
            else:
                pl.when(pl.program_id(0) // nts == s)(functools.partial(update, g_refs[s]))

    def g_map(i, s):
        return (0, jnp.clip(i - s * nts, 0, nts - 1), 0)

    blk = pl.BlockSpec((tr, C), lambda i: (i, 0))
    return pl.pallas_call(
        body, grid=(R // tr,),
        in_specs=[blk, blk, blk] + [pl.BlockSpec((n, tr, C), functools.partial(g_map, s=s)) for s in range(ns)],
        out_specs=[blk] * 4, out_shape=[SDS((R, C), F32)] * 4,
        compiler_params=_params("parallel"), name=name,
    )(w, m, v, *g_parts)


ANY = pl.BlockSpec(memory_space=pl.ANY)
STAGE_BYTES = 2 * 1024 * 1024


def _stage_shape(shape, dtype):
    row_bytes = int(np.prod(shape[1:])) * jnp.dtype(dtype).itemsize
    rows = max(1, min(shape[0], STAGE_BYTES // row_bytes))
    while shape[0] % rows:
        rows -= 1
    return (rows,) + tuple(shape[1:])


def _staged_copy(frm, to, buf, sems):
    rows = buf.shape[0]
    for r0 in range(0, frm.shape[0], rows):
        cp = pltpu.make_async_copy(frm.at[pl.ds(r0, rows)], buf, sems.at[0])
        cp.start()
        cp.wait()
        cp = pltpu.make_async_copy(buf, to.at[pl.ds(r0, rows)], sems.at[1])
        cp.start()
        cp.wait()


def _all_gather(shards, out_shapes, views, *, name):
    n = len(shards)

    def body(*refs):
        ins, outs = refs[:n], refs[n:2 * n]
        send_sems, recv_sems, local_sems = refs[2 * n:2 * n + 3]
        bufs = refs[2 * n + 3:]
        x, y, c = lax.axis_index("x"), lax.axis_index("y"), lax.axis_index("c")
        me, sibling = (x, y, c), (x, y, 1 - c)
        chips = [(1 - x, y), (x, 1 - y), (1 - x, 1 - y)]

        def dev(p):
            return 4 * p[0] + 2 * p[1] + p[2]

        def copy(a, k, block, to, src=None):
            dst = views[a](outs[a], dev(block))
            return pltpu.make_async_remote_copy(
                src_ref=dst if src is None else src, dst_ref=dst,
                send_sem=send_sems.at[a, k], recv_sem=recv_sems.at[a, k],
                device_id=to, device_id_type=MESH)

        first, passed = [], []
        for a in range(n):
            first.append(copy(a, 0, me, sibling, src=ins[a]))
            first += [copy(a, 1 + j, me, (*chip, c), src=ins[a]) for j, chip in enumerate(chips)]
        for cp in first:
            cp.start()
        for a in range(n):
            _staged_copy(ins[a], views[a](outs[a], dev(me)), bufs[a], local_sems)
        for j, chip in enumerate(chips):
            for a in range(n):
                copy(a, 1 + j, (*chip, c), me).wait_recv()
                cp = copy(a, 4 + j, (*chip, c), sibling)
                cp.start()
                passed.append(cp)
        for a in range(n):
            copy(a, 0, sibling, me).wait_recv()
            for j, chip in enumerate(chips):
                copy(a, 4 + j, (*chip, 1 - c), me).wait_recv()
        for cp in first + passed:
            cp.wait_send()

    return pl.pallas_call(
        body, in_specs=[ANY] * n, out_specs=[ANY] * n, out_shape=list(out_shapes),
        scratch_shapes=[pltpu.SemaphoreType.DMA((n, 7)), pltpu.SemaphoreType.DMA((n, 7)),
                        pltpu.SemaphoreType.DMA((2,))]
        + [pltpu.VMEM(_stage_shape(s.shape, s.dtype), s.dtype) for s in shards],
        name=name,
    )(*shards)


HBM = pl.BlockSpec(memory_space=pltpu.HBM)
SEM = pl.BlockSpec(memory_space=pltpu.SEMAPHORE)
EFFECT = pltpu.SideEffectType.DATAFLOW_SIDE_EFFECTING


def _relations(x, y, c):
    for m in range(1, N_DEV):
        yield m, (1 - x if m & 4 else x, 1 - y if m & 2 else y, 1 - c if m & 1 else c)


def _dev_id(p):
    return 4 * p[0] + 2 * p[1] + p[2]


def _send_start(srcs, land_shapes, src_views, dst_views, after, *, name):
    n = len(srcs)

    def body(*refs):
        ins, lands = refs[:n], refs[n:2 * n]
        send_sems, recv_sems, token = refs[2 * n + 1], refs[2 * n + 2], refs[-1]
        x, y, c = lax.axis_index("x"), lax.axis_index("y"), lax.axis_index("c")
        me = _dev_id((x, y, c))
        for m, p in _relations(x, y, c):
            for a in range(n):
                pltpu.make_async_remote_copy(
                    src_ref=src_views[a](ins[a], me, _dev_id(p), m), dst_ref=dst_views[a](lands[a], me, m),
                    send_sem=send_sems.at[a * (N_DEV - 1) + m - 1], recv_sem=recv_sems.at[a * (N_DEV - 1) + m - 1],
                    device_id=p, device_id_type=MESH).start()
        token[...] = jnp.zeros_like(token)

    lands = [pltpu.with_memory_space_constraint(lax.empty(s.shape, s.dtype), pltpu.HBM) for s in land_shapes]
    srcs = [pltpu.with_memory_space_constraint(v, pltpu.HBM) for v in srcs]
    res = pl.pallas_call(
        body, name=name,
        out_shape=[pltpu.SemaphoreType.DMA((n * (N_DEV - 1),)), pltpu.SemaphoreType.DMA((n * (N_DEV - 1),))]
        + [pltpu.HBM(v.shape, v.dtype) for v in srcs] + [pltpu.HBM(s.shape, s.dtype) for s in land_shapes]
        + [SDS((8, 128), F32)],
        in_specs=[HBM] * (2 * n) + [ANY],
        out_specs=[SEM, SEM] + [HBM] * (2 * n) + [pl.BlockSpec(memory_space=pltpu.VMEM)],
        input_output_aliases={i: 2 + i for i in range(2 * n)},
        compiler_params=pltpu.CompilerParams(has_side_effects=EFFECT),
    )(*srcs, *lands, after)
    return res[0], res[1], res[2:2 + n], res[2 + n:2 + 2 * n], res[-1]


def _send_wait(started, src_views, dst_views, own_views, own_shapes, after, *, name):
    send_sems, recv_sems, srcs, lands, _ = started
    n = len(srcs)

    def body(*refs):
        ins, lnd = refs[:n], refs[n:2 * n]
        send_sems, recv_sems = refs[2 * n], refs[2 * n + 1]
        got = refs[2 * n + 3 + n:2 * n + 3 + 2 * n]
        local_sems = refs[2 * n + 3 + 2 * n]
        bufs = refs[2 * n + 4 + 2 * n:]
        x, y, c = lax.axis_index("x"), lax.axis_index("y"), lax.axis_index("c")
        me = _dev_id((x, y, c))
        for m, p in _relations(x, y, c):
            for a in range(n):
                cp = pltpu.make_async_remote_copy(
                    src_ref=src_views[a](ins[a], me, _dev_id(p), m), dst_ref=dst_views[a](lnd[a], me, m),
                    send_sem=send_sems.at[a * (N_DEV - 1) + m - 1], recv_sem=recv_sems.at[a * (N_DEV - 1) + m - 1],
                    device_id=p, device_id_type=MESH)
                cp.wait_send()
                cp.wait_recv()
        for a in range(n):
            frm, to = own_views[a](ins[a], got[a], me)
            _staged_copy(frm, to, bufs[a], local_sems)

    res = pl.pallas_call(
        body, name=name,
        out_shape=[pltpu.HBM(v.shape, v.dtype) for v in srcs] + [pltpu.HBM(v.shape, v.dtype) for v in lands],
        in_specs=[HBM] * (2 * n) + [SEM, SEM, ANY], out_specs=[HBM] * (2 * n),
        input_output_aliases={i: i for i in range(2 * n)},
        scratch_shapes=[pltpu.SemaphoreType.DMA((2,))]
        + [pltpu.VMEM(_stage_shape(s, v.dtype), v.dtype) for s, v in zip(own_shapes, srcs)],
        compiler_params=pltpu.CompilerParams(has_side_effects=EFFECT),
    )(*srcs, *lands, send_sems, recv_sems, after)
    return res[n:]


def kernel(x, norm_gains, fox_w_in, fox_b_f, hgrn_w_in, hgrn_lb_logits, hgrn_onorm, w_out, final_gain, loss_target, m_norm_gains, m_fox_w_in, m_fox_b_f, m_hgrn_w_in, m_hgrn_lb_logits, m_hgrn_onorm, m_w_out, m_final_gain, v_norm_gains, v_fox_w_in, v_fox_b_f, v_hgrn_w_in, v_hgrn_lb_logits, v_hgrn_onorm, v_w_out, v_final_gain):
    _, S, D = x.shape
    H = FOX_HEADS
    W = H * HEAD_DIM
    assert HGRN_HEADS == H and w_out.shape[2] == D
    cf = fox_w_in.shape[2]
    ch = hgrn_w_in.shape[2]
    ro = w_out.shape[1]
    co = hgrn_onorm.shape[1]
    assert N_DEV * cf == 4 * W + H and N_DEV * ch == 4 * W and N_DEV * ro == W and N_DEV * co == W
    x2 = x.reshape(S, D)
    tgt = loss_target.reshape(S, D)

    col = lambda n: (lambda r, i: r.at[:, pl.ds(pl.multiple_of(i * n, n), n)])
    row = lambda n: (lambda r, i: r.at[pl.ds(pl.multiple_of(i * n, n), n), :])

    (wf_g,) = _all_gather([fox_w_in[0].astype(BF16)], [SDS((N_DEV, D, cf), BF16)], [lambda r, p: r.at[p]],
                          name="gather_fox_w_in")

    late_views = [col(ch), row(ro), row(ro), col(co)]
    late = _send_start(
        [hgrn_w_in[0].astype(BF16), w_out[0].astype(BF16), w_out[1].astype(BF16), hgrn_onorm],
        [SDS((D, 4 * W), BF16), SDS((W, D), BF16), SDS((W, D), BF16), SDS((1, W), F32)],
        [lambda r, me, p, m: r] * 4, [lambda r, me, m, v=v: v(r, me) for v in late_views], wf_g[0, 0:8],
        name="gather_later_start")
    ng0 = norm_gains[0:1] + late[4][0:1, 0:1]
    wf = jnp.transpose(wf_g, (1, 0, 2)).reshape(D, N_DEV * cf)
    wf_main = jnp.concatenate([wf[:, :3 * W], wf[:, 3 * W + H:]], axis=1)
    wfl_t = wf[:, 3 * W:3 * W + H].T

    h0 = _rms_fwd(x2, ng0, name="rms0_fwd")
    p0 = _mm_nn([h0], wf_main, BF16, scale_cols=(W, LOG2E * HEAD_DIM ** -0.5), name="fox_in_proj")
    fl_t = _mm_nt_rows(wfl_t, h0, name="fox_forget_proj")
    b_col = fox_b_f.reshape(H, 1)
    kaug = _fox_key_aug(*_fox_gate_fwd(fl_t, b_col, name="fox_gate_fwd"))
    o0, y0, qaug = _fox_fwd(p0, kaug, H=H, name="fox_attn_fwd")
    wh, wo0, wo1, onorm = _send_wait(
        late, [lambda r, me, p, m: r] * 4, [lambda r, me, m, v=v: v(r, me) for v in late_views],
        [lambda src, land, me, v=v: (src, v(land, me)) for v in late_views],
        [(D, ch), (ro, D), (ro, D), (1, co)], y0[0:16], name="gather_later_wait")
    x1 = _mm_nn([y0], wo0, F32, residual=x2, name="fox_out_proj")

    lb = _lb_fwd(hgrn_lb_logits, name="hgrn_lower_bound")
    h1 = _rms_fwd(x1, norm_gains[1:2], name="rms1_fwd")
    p1 = _mm_nn([h1], wh, BF16, b_cols=[(0, W), (2 * W, 4 * W)], name="hgrn_in_proj")
    f1 = _mm_nn([h1], wh, F32, b_cols=[(W, 2 * W)], name="hgrn_forget_proj")
    o1, y1, states, a_t1 = _hgrn_fwd(p1, f1, lb, onorm, H=H, name="hgrn_fwd")
    xo = _mm_nn([y1], wo1, F32, residual=x1, name="hgrn_out_proj")

    dx2b, loss_part, dgf = _loss_head(xo, final_gain.reshape(1, D), tgt, name="loss_head")
    loss = lax.psum(jnp.sum(loss_part), ("x", "y", "c"))

    dy1 = _mm_nn([dx2b], wo1, BF16, b_t=True, name="hgrn_out_proj_dx")
    dwo1 = _mm_tn(y1, [dx2b], BF16, name="hgrn_out_proj_dw")
    do1, dg1, donorm = _hgrn_post_bwd(dy1, o1, p1, onorm, H=H, name="hgrn_post_bwd")
    dq1, df1, di1, dlb = _hgrn_bwd(p1, f1, lb, do1, states, a_t1, H=H, name="hgrn_bwd")
    segs1 = [dq1, df1, di1, dg1]
    dh1 = _mm_nn(segs1, wh, BF16, b_t=True, tn=D, name="hgrn_in_proj_dx")
    dwh = _mm_tn(h1, segs1, BF16, name="hgrn_in_proj_dw")
    part_views = [col(ch), row(ro)]
    slot = lambda r, me, m: r.at[m]
    ex1 = _send_start([dwh, dwo1], [SDS((N_DEV, D, ch), BF16), SDS((N_DEV, ro, D), BF16)],
                      [lambda r, me, p, m, v=v: v(r, p) for v in part_views], [slot] * 2, dwo1[0:8],
                      name="exchange_layer1_start")
    ng1 = norm_gains[1:2] + ex1[4][0:1, 0:1]
    dx1b, dng1 = _rms_bwd(x1, ng1, dh1, dx2b, BF16, name="rms1_bwd")

    dy0 = _mm_nn([dx1b], wo0, BF16, b_t=True, name="fox_out_proj_dx")
    dwo0 = _mm_tn(y0, [dx1b], BF16, name="fox_out_proj_dw")
    do0, dg0, doaug = _fox_post_bwd(dy0, o0, p0, H=H, name="fox_post_bwd")
    dq0, dc_row, dk0, dv0, dc_key = _fox_bwd(p0, kaug, qaug, do0, doaug, H=H, name="fox_attn_bwd")
    dfl_t, dbf = _fox_gate_bwd(dc_row.reshape(H, S), dc_key.reshape(H, S), fl_t, b_col, name="fox_gate_bwd")
    dfl_tb = dfl_t.astype(BF16)
    dwfl_t = _mm_nn([dfl_tb], h0, BF16, name="fox_forget_proj_dw")
    segs0 = [dq0, dk0, dv0, dg0]
    dwf_main = _mm_tn(h0, segs0, BF16, name="fox_in_proj_dw")
    dwf = jnp.concatenate([dwf_main[:, :3 * W], dwfl_t.T, dwf_main[:, 3 * W:]], axis=1)
    dwf_blocks = jnp.transpose(dwf.reshape(D, N_DEV, cf), (1, 0, 2))
    ex0 = _send_start([dwf_blocks, dwo0], [SDS((N_DEV, D, cf), BF16), SDS((N_DEV, ro, D), BF16)],
                      [lambda r, me, p, m: r.at[p], lambda r, me, p, m: row(ro)(r, p)], [slot] * 2, dwo0[0:8],
                      name="exchange_layer0_start")
    wfl_t0 = wfl_t + ex0[4][0:1, 0:1].astype(BF16)
    dh0_f = _mm_nn([dfl_tb.T], wfl_t0, BF16, name="fox_forget_proj_dx")
    dh0 = _mm_nn(segs0, wf_main, BF16, residual=dh0_f, b_t=True, tn=D, name="fox_in_proj_dx")
    grad_x, dng0 = _rms_bwd(x2, norm_gains[0:1], dh0, dx1b, F32, name="rms0_bwd")

    own1 = [lambda src, land, me, v=v: (v(src, me), land.at[0]) for v in part_views]
    rh, ro1 = _send_wait(ex1, [lambda r, me, p, m, v=v: v(r, p) for v in part_views], [slot] * 2, own1,
                         [(D, ch), (ro, D)], dng0, name="exchange_layer1_wait")

    pad = lambda a: jnp.pad(a, ((0, 0), (0, W - a.shape[1])))
    stats = jnp.concatenate([dng0, dng1, dlb, dgf, pad(dbf.reshape(1, H)), donorm,
                             jnp.zeros((2, W), F32)], axis=0)
    assert D == W
    (stats_all,) = _all_gather([stats], [SDS((N_DEV, STAT_ROWS, W), F32)], [lambda r, p: r.at[p]],
                               name="gather_small_grads")
    g_small = _stats_reduce(stats_all, hgrn_lb_logits, name="reduce_small_grads")
    me = 4 * lax.axis_index("x") + 2 * lax.axis_index("y") + lax.axis_index("c")
    g_onorm = lax.dynamic_slice_in_dim(g_small[6:7], me * co, co, axis=1)

    def upd(w, m, v, parts, name):
        shp = w.shape
        r2 = (-1, shp[-1])
        g, d, mn, vn = _adamw(w.reshape(r2), m.reshape(r2), v.reshape(r2), parts, name=name)
        return g.reshape(shp), d.reshape(shp), mn.reshape(shp), vn.reshape(shp)

    res = {
        "norm_gains": upd(norm_gains, m_norm_gains, v_norm_gains, [g_small[None, 0:2]], "adamw_norm_gains"),
        "fox_b_f": upd(fox_b_f, m_fox_b_f, v_fox_b_f, [g_small[None, 5:6, :H]], "adamw_fox_b_f"),
        "hgrn_w_in": upd(hgrn_w_in, m_hgrn_w_in, v_hgrn_w_in, [rh], "adamw_hgrn_w_in"),
        "hgrn_lb_logits": upd(hgrn_lb_logits, m_hgrn_lb_logits, v_hgrn_lb_logits, [g_small[None, 2:4]],
                              "adamw_hgrn_lb_logits"),
        "hgrn_onorm": upd(hgrn_onorm, m_hgrn_onorm, v_hgrn_onorm, [g_onorm[None]], "adamw_hgrn_onorm"),
        "final_gain": upd(final_gain.reshape(1, D), m_final_gain.reshape(1, D), v_final_gain.reshape(1, D),
                          [g_small[None, 4:5]], "adamw_final_gain"),
    }
    rf, ro0 = _send_wait(ex0, [lambda r, me, p, m: r.at[p], lambda r, me, p, m: row(ro)(r, p)], [slot] * 2,
                         [lambda src, land, me: (src.at[me], land.at[0]),
                          lambda src, land, me: (row(ro)(src, me), land.at[0])],
                         [(D, cf), (ro, D)], res["hgrn_w_in"][0][0, 0:8], name="exchange_layer0_wait")
    res["fox_w_in"] = upd(fox_w_in, m_fox_w_in, v_fox_w_in, [rf], "adamw_fox_w_in")
    res["w_out"] = upd(w_out, m_w_out, v_w_out, [ro0, ro1], "adamw_w_out")
    order = ["norm_gains", "fox_w_in", "fox_b_f", "hgrn_w_in", "hgrn_lb_logits", "hgrn_onorm", "w_out", "final_gain"]
    fix = lambda n, a: a.reshape(D) if n == "final_gain" else a
    outs = [loss, grad_x.reshape(1, S, D)]
    for k in range(4):
        outs += [fix(n, res[n][k]) for n in order]
    return tuple(outs)
```
